```python
import math
import jax, jax.numpy as jnp
from jax import lax
import numpy as np

D_MODEL = 1024
BATCH = 8
SEQ = 2048
DEPTH = 1

D_MIX = D_MODEL
D_RWKV = D_MIX // 2
RWKV_HEAD = 64
N_RWKV_HEADS = D_RWKV // RWKV_HEAD
D_ATT = D_MIX - D_RWKV
ATT_HEAD = 64
N_ATT_HEADS = D_ATT // ATT_HEAD
RANK_DECAY = 64
RANK_A = 64
RANK_GATE = 128
RWKV_GN_EPS = 64e-5
Q_LORA = D_MODEL // 4
KV_LORA = D_MODEL // 8
IDX_HEADS = 8
IDX_DIM = 64
TOPK_MAX = 256
Q_BLOCK = 128
N_EXPERTS = 32
TOP_K_EXPERTS = 4
D_FF = D_MODEL
SWIGLU_LIMIT = 7.0
SWIGLU_ALPHA = 1.702
MOE_BLOCK = 128
NEG_INF = -1e30

SPLIT_SHIFT = (D_RWKV, D_RWKV, D_RWKV, RANK_DECAY, RANK_A, RANK_GATE)
SPLIT_ATT = (Q_LORA, KV_LORA, IDX_DIM, IDX_HEADS)
N_SHIFT = sum(SPLIT_SHIFT)
N_IN = N_SHIFT + sum(SPLIT_ATT)

kernel_name = "hymba_rwkv7_dsa_moe_deepnorm_adaln"


def _split(z, sizes):
    idx = list(np.cumsum(sizes)[:-1])
    return jnp.split(z, idx, axis=-1)


def _layernorm(x, g, b, eps=1e-5):
    xf = x.astype(jnp.float32)
    mu = jnp.mean(xf, axis=-1, keepdims=True)
    var = jnp.mean(jnp.square(xf - mu), axis=-1, keepdims=True)
    return ((xf - mu) * lax.rsqrt(var + eps)).astype(x.dtype) * g + b


def _rmsnorm(x, g, eps=1e-6):
    xf = x.astype(jnp.float32)
    ms = jnp.mean(jnp.square(xf), axis=-1, keepdims=True)
    return (xf * lax.rsqrt(ms + eps)).astype(x.dtype) * g


def _alibi_slopes(n_heads):
    return jnp.asarray([2.0 ** (-8.0 * (h + 1) / n_heads) for h in range(n_heads)], jnp.float32)


def _rwkv7_scan(r, w, k, v, a, b):
    B, _, H, N = r.shape

    def step(state, inp):
        r_t, w_t, k_t, v_t, a_t, b_t = inp
        sa = jnp.einsum('bhvk,bhk->bhv', state, a_t)
        state = (state * w_t[:, :, None, :] + sa[..., None] * b_t[:, :, None, :]
                 + v_t[..., None] * k_t[:, :, None, :])
        return state, jnp.einsum('bhvk,bhk->bhv', state, r_t)

    xs = tuple(jnp.moveaxis(z, 1, 0) for z in (r, w, k, v, a, b))
    s0 = jnp.zeros((B, H, N, N), jnp.float32)
    _, o = lax.scan(step, s0, xs)
    return jnp.moveaxis(o, 0, 1)


def _rwkv7_group(pr, pk, pv, pw, pa, pg, w0, w2, a0, a2, g2, k_k, k_a, r_k, ln_w, ln_b):
    B, S, _ = pr.shape
    H, N = N_RWKV_HEADS, RWKV_HEAD
    f32 = jnp.float32
    w_log = -jax.nn.softplus(-(w0 + jnp.tanh(pw) @ w2).astype(f32)) - 0.5
    decay = jnp.exp(-jnp.exp(w_log))
    a = jax.nn.sigmoid((a0 + pa @ a2).astype(f32))
    g = jax.nn.sigmoid(pg) @ g2
    kk = (pk * k_k).astype(f32).reshape(B, S, H, N)
    kk = kk / jnp.maximum(jnp.sqrt(jnp.sum(kk * kk, axis=-1, keepdims=True)), 1e-12)
    k = (pk.astype(f32) * (1.0 + (a - 1.0) * k_a)).reshape(B, S, H, N)
    r = pr.astype(f32).reshape(B, S, H, N)
    v = pv.astype(f32).reshape(B, S, H, N)
    a_h = a.reshape(B, S, H, N)
    o = _rwkv7_scan(r, decay.reshape(B, S, H, N), k, v, -kk, kk * a_h)
    mu = jnp.mean(o, axis=-1, keepdims=True)
    var = jnp.mean(jnp.square(o - mu), axis=-1, keepdims=True)
    o = ((o - mu) * lax.rsqrt(var + RWKV_GN_EPS)).reshape(B, S, D_RWKV) * ln_w + ln_b
    bonus = jnp.sum(r * k * r_k, axis=-1, keepdims=True) * v
    o = (o + bonus.reshape(B, S, D_RWKV)) * g
    return o.astype(pr.dtype)


def _dsa_sparse_attention(q_abs, iq, iw, k_idx, c_lat, topk):
    B, S, H, C = q_abs.shape
    nb = S // Q_BLOCK
    f32 = jnp.float32
    slopes = _alibi_slopes(H)
    spos = jnp.arange(S, dtype=jnp.int32)
    tpos = spos.reshape(nb, Q_BLOCK)
    k_idx_f = k_idx.astype(f32)
    c_f = c_lat.astype(f32)
    idx_scale = IDX_DIM ** -0.5
    att_scale = ATT_HEAD ** -0.5
    gather = jax.vmap(lambda cb, ib: cb[ib])

    def blk(z):
        return jnp.moveaxis(z.reshape((B, nb, Q_BLOCK) + z.shape[2:]), 1, 0)

    def one_block(args):
        qa, iqb, iwb, tp = args
        s_idx = jnp.einsum('bqhd,bsd->bqhs', iqb.astype(f32), k_idx_f) * idx_scale
        score = jnp.einsum('bqh,bqhs->bqs', iwb.astype(f32), jax.nn.relu(s_idx))
        causal = spos[None, :] <= tp[:, None]
        score = jnp.where(causal[None], score, NEG_INF)
        _, sel = lax.top_k(score, topk)
        c_sel = gather(c_f, sel)
        logits = jnp.einsum('bqhc,bqkc->bqhk', qa.astype(f32), c_sel) * att_scale
        dist = tp[None, :, None] - sel
        logits = logits - slopes[None, None, :, None] * dist.astype(f32)[:, :, None, :]
        logits = jnp.where((dist >= 0)[:, :, None, :], logits, NEG_INF)
        p = jax.nn.softmax(logits, axis=-1)
        return jnp.einsum('bqhk,bqkc->bqhc', p, c_sel)

    o_lat = lax.map(one_block, (blk(q_abs), blk(iq), blk(iw), tpos))
    return jnp.moveaxis(o_lat, 0, 1).reshape(B, S, H, C).astype(q_abs.dtype)


def _dsa_group(pq, pkv, pik, piw, q_norm, w_q_up, kv_norm, w_uk, w_uv, idx_w_q, idx_ln_g, idx_ln_b):
    B, S, _ = pq.shape
    topk = min(TOPK_MAX, S // 4)
    q_lat = _rmsnorm(pq, q_norm)
    q = (q_lat @ w_q_up).reshape(B, S, N_ATT_HEADS, ATT_HEAD)
    q_abs = jnp.einsum('bshd,hdc->bshc', q, w_uk)
    c_lat = _rmsnorm(pkv, kv_norm)
    iq = (q_lat @ idx_w_q).reshape(B, S, IDX_HEADS, IDX_DIM)
    ik = _layernorm(pik, idx_ln_g, idx_ln_b)
    iw = piw * (IDX_HEADS ** -0.5)
    o_lat = _dsa_sparse_attention(q_abs, iq, iw, ik, c_lat, topk)
    o = jnp.einsum('bshc,hcd->bshd', o_lat, w_uv)
    return o.reshape(B, S, D_ATT)


def _moe(u, w_router, b_router, w_gu, b_gu, w_dn, b_dn):
    B, S, D = u.shape
    T = B * S
    TK = T * TOP_K_EXPERTS
    xt = u.reshape(T, D)
    logits = (xt @ w_router + b_router).astype(jnp.float32)
    top_val, top_idx = lax.top_k(logits, TOP_K_EXPERTS)
    gates = jax.nn.softmax(top_val, axis=-1)
    flat_e = top_idx.reshape(-1)
    flat_tok = jnp.repeat(jnp.arange(T, dtype=jnp.int32), TOP_K_EXPERTS)
    flat_g = gates.reshape(-1)
    order = jnp.argsort(flat_e)
    e_sorted = flat_e[order]
    counts = jnp.bincount(flat_e, length=N_EXPERTS)
    starts = jnp.cumsum(counts) - counts
    padded = (counts + MOE_BLOCK - 1) // MOE_BLOCK * MOE_BLOCK
    pad_ends = jnp.cumsum(padded)
    pad_starts = pad_ends - padded
    dest = pad_starts[e_sorted] + (jnp.arange(TK, dtype=jnp.int32) - starts[e_sorted])
    n_blocks = -(-TK // MOE_BLOCK) + N_EXPERTS
    n_rows = n_blocks * MOE_BLOCK
    row_tok = jnp.zeros((n_rows,), jnp.int32).at[dest].set(flat_tok[order])
    row_gate = jnp.zeros((n_rows,), jnp.float32).at[dest].set(flat_g[order])
    block_e = jnp.minimum(jnp.searchsorted(pad_ends, jnp.arange(n_blocks) * MOE_BLOCK, side='right'),
                          N_EXPERTS - 1).astype(jnp.int32)

    def expert_block(args):
        e, tok = args
        xb = xt[tok]
        gu = xb @ w_gu[e] + b_gu[e]
        gate = jnp.minimum(gu[:, 0::2], SWIGLU_LIMIT)
        up = jnp.clip(gu[:, 1::2], -SWIGLU_LIMIT, SWIGLU_LIMIT)
        h = (up + 1.0) * (gate * jax.nn.sigmoid(gate * SWIGLU_ALPHA))
        return h @ w_dn[e] + b_dn[e]

    out = lax.map(expert_block, (block_e, row_tok.reshape(n_blocks, MOE_BLOCK)))
    out = out.reshape(n_rows, D).astype(jnp.float32) * row_gate[:, None]
    y = jnp.zeros((T, D), jnp.float32).at[row_tok].add(out)
    return y.reshape(B, S, D).astype(u.dtype)


def setup_inputs(seed: int = 0) -> dict:
    key = jax.random.key(seed)
    ks = iter(jax.random.split(key, 40))
    L = DEPTH
    beta = (8.0 * DEPTH) ** -0.25

    def nrm(shape, scale):
        return jax.random.normal(next(ks), shape, jnp.float32) * scale

    d = D_MODEL
    inp = {}
    inp['x'] = nrm((BATCH, SEQ, d), 1.0)
    inp['c'] = nrm((BATCH, d), 1.0)
    inp['w_ada'] = nrm((L, d, 6 * d), 0.5 * d ** -0.5)
    inp['b_ada'] = nrm((L, 6 * d), 0.01)
    inp['w_in'] = nrm((L, d, N_IN), d ** -0.5)
    inp['shift_mu'] = jax.random.uniform(next(ks), (L, N_SHIFT), jnp.float32)
    inp['rwkv_w0'] = jax.random.uniform(next(ks), (L, D_RWKV), jnp.float32, -6.0, -1.0)
    inp['rwkv_w2'] = nrm((L, RANK_DECAY, D_RWKV), 0.5 * RANK_DECAY ** -0.5)
    inp['rwkv_a0'] = nrm((L, D_RWKV), 0.1)
    inp['rwkv_a2'] = nrm((L, RANK_A, D_RWKV), RANK_A ** -0.5)
    inp['rwkv_g2'] = nrm((L, RANK_GATE, D_RWKV), RANK_GATE ** -0.5)
    inp['rwkv_k_k'] = 0.85 + nrm((L, D_RWKV), 0.05)
    inp['rwkv_k_a'] = 1.0 + nrm((L, D_RWKV), 0.05)
    inp['rwkv_r_k'] = nrm((L, N_RWKV_HEADS, RWKV_HEAD), 0.1)
    inp['rwkv_ln_w'] = 1.0 + nrm((L, D_RWKV), 0.05)
    inp['rwkv_ln_b'] = nrm((L, D_RWKV), 0.01)
    inp['mla_q_norm'] = 1.0 + nrm((L, Q_LORA), 0.05)
    inp['mla_w_q_up'] = nrm((L, Q_LORA, N_ATT_HEADS * ATT_HEAD), Q_LORA ** -0.5)
    inp['mla_kv_norm'] = 1.0 + nrm((L, KV_LORA), 0.05)
    inp['mla_w_uk'] = nrm((L, N_ATT_HEADS, ATT_HEAD, KV_LORA), ATT_HEAD ** -0.5)
    inp['mla_w_uv'] = nrm((L, N_ATT_HEADS, KV_LORA, ATT_HEAD), KV_LORA ** -0.5)
    inp['idx_w_q'] = nrm((L, Q_LORA, IDX_HEADS * IDX_DIM), Q_LORA ** -0.5)
    inp['idx_ln_g'] = 1.0 + nrm((L, IDX_DIM), 0.05)
    inp['idx_ln_b'] = nrm((L, IDX_DIM), 0.01)
    inp['w_out'] = nrm((L, D_MIX, d), beta * D_MIX ** -0.5)
    inp['ln1_g'] = 1.0 + nrm((L, d), 0.05)
    inp['ln1_b'] = nrm((L, d), 0.01)
    inp['w_router'] = nrm((L, d, N_EXPERTS), d ** -0.5)
    inp['b_router'] = nrm((L, N_EXPERTS), 0.01)
    inp['w_gu'] = nrm((L, N_EXPERTS, d, 2 * D_FF), d ** -0.5)
    inp['b_gu'] = nrm((L, N_EXPERTS, 2 * D_FF), 0.01)
    inp['w_dn'] = nrm((L, N_EXPERTS, D_FF, d), beta * D_FF ** -0.5)
    inp['b_dn'] = nrm((L, N_EXPERTS, d), 0.01)
    inp['ln2_g'] = 1.0 + nrm((L, d), 0.05)
    inp['ln2_b'] = nrm((L, d), 0.01)
    return inp


def reference(x, c, w_ada, b_ada, w_in, shift_mu, rwkv_w0, rwkv_w2, rwkv_a0, rwkv_a2, rwkv_g2,
              rwkv_k_k, rwkv_k_a, rwkv_r_k, rwkv_ln_w, rwkv_ln_b, mla_q_norm, mla_w_q_up,
              mla_kv_norm, mla_w_uk, mla_w_uv, idx_w_q, idx_ln_g, idx_ln_b, w_out, ln1_g, ln1_b,
              w_router, b_router, w_gu, b_gu, w_dn, b_dn, ln2_g, ln2_b):
    alpha = (2.0 * DEPTH) ** 0.25
    c_act = jax.nn.silu(c)
    for l in range(DEPTH):
        mod = c_act @ w_ada[l] + b_ada[l]
        sh1, sc1, gt1, sh2, sc2, gt2 = [m[:, None, :] for m in jnp.split(mod, 6, axis=-1)]

        u = x * (1.0 + sc1) + sh1
        p = u @ w_in[l]
        p_sh = p[..., :N_SHIFT]
        p_prev = jnp.pad(p_sh[:, :-1], ((0, 0), (1, 0), (0, 0)))
        p_sh = p_sh + shift_mu[l] * (p_prev - p_sh)
        pr, pk, pv, pw, pa, pg = _split(p_sh, SPLIT_SHIFT)
        pq, pkv, pik, piw = _split(p[..., N_SHIFT:], SPLIT_ATT)
        o_rwkv = _rwkv7_group(pr, pk, pv, pw, pa, pg, rwkv_w0[l], rwkv_w2[l], rwkv_a0[l], rwkv_a2[l],
                              rwkv_g2[l], rwkv_k_k[l], rwkv_k_a[l], rwkv_r_k[l], rwkv_ln_w[l], rwkv_ln_b[l])
        o_dsa = _dsa_group(pq, pkv, pik, piw, mla_q_norm[l], mla_w_q_up[l], mla_kv_norm[l], mla_w_uk[l],
                           mla_w_uv[l], idx_w_q[l], idx_ln_g[l], idx_ln_b[l])
        mix = jnp.concatenate([o_rwkv, o_dsa], axis=-1) @ w_out[l]
        x = _layernorm(alpha * x + (1.0 + gt1) * mix, ln1_g[l], ln1_b[l])

        u = x * (1.0 + sc2) + sh2
        ffn = _moe(u, w_router[l], b_router[l], w_gu[l], b_gu[l], w_dn[l], b_dn[l])
        x = _layernorm(alpha * x + (1.0 + gt2) * ffn, ln2_g[l], ln2_b[l])
    return x
```

```python
import functools
import math

import jax
import jax.numpy as jnp
import numpy as np
from jax import lax
from jax.experimental import pallas as pl
from jax.experimental.pallas import tpu as pltpu

F32 = jnp.float32
BF16 = jnp.bfloat16
I32 = jnp.int32

RWKV_HEAD = 64
N_RWKV_HEADS = 8
D_RWKV = RWKV_HEAD * N_RWKV_HEADS
RANK_DECAY = 64
RANK_A = 64
RANK_GATE = 128
RWKV_GN_EPS = 64e-5
ATT_HEAD = 64
N_ATT_HEADS = 8
D_ATT = ATT_HEAD * N_ATT_HEADS
Q_LORA = 256
KV_LORA = 128
IDX_HEADS = 8
IDX_DIM = 64
TOPK_MAX = 256
N_EXPERTS = 32
TOP_K_EXPERTS = 4
SWIGLU_LIMIT = 7.0
SWIGLU_ALPHA = 1.702
NEG_BIG = -1e30
INT_MIN = -(2 ** 31)

LANES = 128
VMEM_LIMIT = 48 * 1024 * 1024

TM_PROJ = 256
L_CHUNK = 64
TQ = 128
TM_ROUTE = 256
BM_EXPERT = 256

_SEG = (("r", 512, 512), ("k", 512, 512), ("v", 512, 512), ("w", 64, 128), ("a", 64, 128), ("g", 128, 128),
        ("q", 256, 256), ("kv", 128, 128), ("ik", 64, 128), ("iw", 8, 128))
N_SHIFT_P = 512 * 3 + 128 * 3
N_IN_P = sum(s[2] for s in _SEG)


def _cparams(sem):
    return pltpu.CompilerParams(dimension_semantics=sem, vmem_limit_bytes=VMEM_LIMIT)


def _bdot(a, b):
    return jnp.dot(a.astype(BF16), b.astype(BF16), preferred_element_type=F32)


def _bdot_nt(a, b):
    return lax.dot_general(a.astype(BF16), b.astype(BF16), (((1,), (1,)), ((), ())), preferred_element_type=F32)


def _bdot_tn(a, b):
    return lax.dot_general(a.astype(BF16), b.astype(BF16), (((0,), (0,)), ((), ())), preferred_element_type=F32)


def _split2(a):
    hi = a.astype(BF16)
    lo = (a - hi.astype(F32)).astype(BF16)
    return hi, lo


def _split3(a):
    hi = a.astype(BF16)
    r1 = a - hi.astype(F32)
    mid = r1.astype(BF16)
    lo = (r1 - mid.astype(F32)).astype(BF16)
    return hi, mid, lo


def _dot3(a, b, dims=(((1,), (0,)), ((), ()))):
    ah, al = _split2(a)
    bh, bl = _split2(b)
    d = functools.partial(lax.dot_general, dimension_numbers=dims, preferred_element_type=F32)
    return d(ah, bh) + (d(ah, bl) + d(al, bh))


def _dot_exact_rhs(a, b_exact, nsplit=3):
    parts = _split3(a) if nsplit == 3 else _split2(a)
    acc = None
    for p in parts[::-1]:
        t = jnp.dot(p, b_exact, preferred_element_type=F32)
        acc = t if acc is None else acc + t
    return acc


def _dot_exact_lhs(a_exact, b, nsplit=3):
    parts = _split3(b) if nsplit == 3 else _split2(b)
    acc = None
    for p in parts[::-1]:
        t = jnp.dot(a_exact, p, preferred_element_type=F32)
        acc = t if acc is None else acc + t
    return acc


def _sigmoid(x):
    return 1.0 / (1.0 + jnp.exp(-x))


def _softplus(x):
    return jnp.maximum(x, 0.0) + jnp.log(1.0 + jnp.exp(-jnp.abs(x)))


def _ada_kernel(c_ref, w_ref, b_ref, o_ref):
    c = c_ref[...]
    o_ref[...] = _dot3(c * _sigmoid(c), w_ref[...]) + b_ref[...]


def _ada_mod(c, w_ada, b_ada):
    B, D = c.shape
    N = w_ada.shape[1]
    tn = 1024
    return pl.pallas_call(
        _ada_kernel,
        grid=(N // tn,),
        in_specs=[pl.BlockSpec((B, D), lambda j: (0, 0)),
                  pl.BlockSpec((D, tn), lambda j: (0, j)),
                  pl.BlockSpec((1, tn), lambda j: (0, j))],
        out_specs=pl.BlockSpec((B, tn), lambda j: (0, j)),
        out_shape=jax.ShapeDtypeStruct((B, N), F32),
        compiler_params=_cparams(("arbitrary",)),
        name="ada_mod",
    )(c, w_ada, b_ada.reshape(1, N))


def _in_proj_kernel(x_ref, sc_ref, sh_ref, win_ref, mu_ref, w0_ref, w2_ref, a0_ref, a2_ref, g2_ref, kk_ref, ka_ref,
                    ones_ref, qn_ref, wq_ref, wuk_ref, kvn_ref, wiq_ref, ig_ref, ib_ref,
                    r_o, lw_o, k_o, v_o, kkn_o, a_o, g_o, qabs_o, iq_o, ik_o, iw_o, cl_o, carry):
    i = pl.program_id(1)
    tm = x_ref.shape[0]

    @pl.when(i == 0)
    def _():
        carry[...] = jnp.zeros_like(carry)

    u = x_ref[...] * (1.0 + sc_ref[...]) + sh_ref[...]
    p = _bdot(u, win_ref[...])
    ps = p[:, :N_SHIFT_P]
    rows = lax.broadcasted_iota(I32, (tm, 1), 0)
    prev = jnp.where(rows == 0, carry[0:1, :], pltpu.roll(ps, 1, 0))
    carry[0:1, :] = ps[tm - 1:tm, :]
    ps = ps + mu_ref[...] * (prev - ps)

    pr, pk, pv = ps[:, 0:512], ps[:, 512:1024], ps[:, 1024:1536]
    pw, pa, pg = ps[:, 1536:1664], ps[:, 1664:1792], ps[:, 1792:1920]
    w_log = -_softplus(-(w0_ref[...] + _dot3(jnp.tanh(pw), w2_ref[...]))) - 0.5
    lw_o[...] = -jnp.exp(w_log)
    a = _sigmoid(a0_ref[...] + _dot3(pa, a2_ref[...]))
    g_o[...] = _dot3(_sigmoid(pg), g2_ref[...])
    kk = pk * kk_ref[...]
    ssq = _dot_exact_rhs(kk * kk, ones_ref[...], nsplit=2)
    kkn_o[...] = kk / jnp.maximum(jnp.sqrt(ssq), 1e-12)
    k_o[...] = pk * (1.0 + (a - 1.0) * ka_ref[...])
    r_o[...] = pr
    v_o[...] = pv
    a_o[...] = a

    pq, pkv = p[:, 1920:2176], p[:, 2176:2304]
    pik, piw = p[:, 2304:2432], p[:, 2432:2560]
    q_lat = pq * lax.rsqrt(jnp.mean(pq * pq, axis=-1, keepdims=True) + 1e-6) * qn_ref[...]
    q = _bdot(q_lat, wq_ref[...])
    qabs_o[...] = (_bdot(q, wuk_ref[...]) * (ATT_HEAD ** -0.5)).astype(BF16)
    cl_o[...] = (pkv * lax.rsqrt(jnp.mean(pkv * pkv, axis=-1, keepdims=True) + 1e-6) * kvn_ref[...]).astype(BF16)
    iq_o[...] = (_bdot(q_lat, wiq_ref[...]) * (IDX_DIM ** -0.5)).astype(BF16)
    lane = lax.broadcasted_iota(I32, (1, LANES), 1)
    valid = lane < IDX_DIM
    mu = jnp.sum(pik, axis=-1, keepdims=True) * (1.0 / IDX_DIM)
    dlt = jnp.where(valid, pik - mu, 0.0)
    var = jnp.sum(dlt * dlt, axis=-1, keepdims=True) * (1.0 / IDX_DIM)
    ik_o[...] = jnp.where(valid, dlt * lax.rsqrt(var + 1e-5) * ig_ref[...] + ib_ref[...], 0.0).astype(BF16)
    iw_o[...] = piw * (IDX_HEADS ** -0.5)


def _pad_cols(w, widths):
    parts, o = [], 0
    for true, padded in widths:
        seg = w[..., o:o + true]
        if padded > true:
            seg = jnp.pad(seg, [(0, 0)] * (w.ndim - 1) + [(0, padded - true)])
        parts.append(seg)
        o += true
    return jnp.concatenate(parts, axis=-1)


def _pad_rows(w, rows):
    return jnp.pad(w, ((0, rows - w.shape[0]), (0, 0)))


def _block_diag(blocks):
    H, a, b = blocks.shape
    eye = jnp.eye(H, dtype=blocks.dtype)
    return (eye[:, None, :, None] * blocks[:, :, None, :]).reshape(H * a, H * b)


def _head_ones(n, head):
    idx = np.arange(n) // head
    return jnp.asarray(idx[:, None] == idx[None, :], BF16)


def _in_proj(x, sc1, sh1, w_in, shift_mu, w0, w2, a0, a2, g2, k_k, k_a, q_norm, w_q_up, w_uk, kv_norm, idx_w_q,
             idx_ln_g, idx_ln_b):
    B, S, D = x.shape
    tm = min(TM_PROJ, S)
    widths = tuple((s[1], s[2]) for s in _SEG)
    win_p = _pad_cols(w_in, widths).astype(BF16)
    mu_p = _pad_cols(shift_mu.reshape(1, -1), widths[:6])
    w2_p = _pad_rows(w2, LANES)
    a2_p = _pad_rows(a2, LANES)
    wuk_bd = _block_diag(w_uk).astype(BF16)
    wiq_p = _pad_cols(idx_w_q, ((IDX_DIM, LANES),) * IDX_HEADS).astype(BF16)
    ig_p = _pad_cols(idx_ln_g.reshape(1, -1), ((IDX_DIM, LANES),))
    ib_p = _pad_cols(idx_ln_b.reshape(1, -1), ((IDX_DIM, LANES),))
    row = lambda v: v.reshape(1, -1)
    tok = lambda n: pl.BlockSpec((None, tm, n), lambda b, i: (b, i, 0))
    mod = pl.BlockSpec((None, 1, D), lambda b, i: (b, 0, 0))
    full = lambda a: pl.BlockSpec(a.shape, lambda b, i: (0,) * a.ndim)
    consts = [win_p, mu_p, row(w0), w2_p, row(a0), a2_p, g2, row(k_k), row(k_a), _head_ones(D_RWKV, RWKV_HEAD),
              row(q_norm), w_q_up.astype(BF16), wuk_bd, row(kv_norm), wiq_p, ig_p, ib_p]
    outs = [(D_RWKV, F32)] * 7 + [(N_ATT_HEADS * KV_LORA, BF16), (IDX_HEADS * LANES, BF16), (LANES, BF16),
                                  (LANES, F32), (KV_LORA, BF16)]
    return pl.pallas_call(
        _in_proj_kernel,
        grid=(B, S // tm),
        in_specs=[tok(D), mod, mod] + [full(a) for a in consts],
        out_specs=[tok(n) for n, _ in outs],
        out_shape=[jax.ShapeDtypeStruct((B, S, n), dt) for n, dt in outs],
        scratch_shapes=[pltpu.VMEM((8, N_SHIFT_P), F32)],
        compiler_params=_cparams(("arbitrary", "arbitrary")),
        name="in_proj",
    )(x, sc1, sh1, *consts)


def _rwkv_kernel(r_ref, lw_ref, k_ref, v_ref, kk_ref, a_ref, g_ref, rk_ref, lnw_ref, lnb_ref, tri_ref, ones_ref,
                 o_ref, state):
    c = pl.program_id(1)
    L = r_ref.shape[0]

    @pl.when(c == 0)
    def _():
        state[...] = jnp.zeros_like(state)

    r, lw, k, v, kk, a = r_ref[...], lw_ref[...], k_ref[...], v_ref[...], kk_ref[...], a_ref[...]
    cum = _dot_exact_lhs(tri_ref[...], lw)
    cum_last = cum[L - 1:L, :]
    w_incl = jnp.exp(cum)
    w_inv = jnp.exp(-cum)
    w_rel = jnp.exp(cum_last - cum)
    w_last = jnp.exp(cum_last)
    bvec = kk * a
    at = -kk * jnp.exp(cum - lw)
    rt = r * w_incl
    bt = bvec * w_inv
    kt = k * w_inv
    bh = bvec * w_rel
    kh = k * w_rel
    ti = lax.broadcasted_iota(I32, (L, L), 0)
    tj = lax.broadcasted_iota(I32, (L, L), 1)
    strict = tj < ti
    incl = tj <= ti
    for h in range(N_RWKV_HEADS):
        sl = slice(h * RWKV_HEAD, (h + 1) * RWKV_HEAD)
        s0 = state[h]
        ath, rth, bth, kth, vh = at[:, sl], rt[:, sl], bt[:, sl], kt[:, sl], v[:, sl]
        n_ab = jnp.where(strict, _dot3(ath, bth, (((1,), (1,)), ((), ()))), 0.0)
        a_ak = jnp.where(strict, _dot3(ath, kth, (((1,), (1,)), ((), ()))), 0.0)
        a_rb = jnp.where(incl, _dot3(rth, bth, (((1,), (1,)), ((), ()))), 0.0)
        a_rk = jnp.where(incl, _dot3(rth, kth, (((1,), (1,)), ((), ()))), 0.0)
        x = _dot3(ath, s0, (((1,), (1,)), ((), ()))) + _dot3(a_ak, vh)
        pw = n_ab
        for it in range(int(math.log2(L))):
            x = x + _dot3(pw, x)
            if it + 1 < int(math.log2(L)):
                pw = _dot3(pw, pw)
        sa = x
        o = _dot3(rth, s0, (((1,), (1,)), ((), ()))) + _dot3(a_rb, sa) + _dot3(a_rk, vh)
        tn = (((0,), (0,)), ((), ()))
        state[h] = s0 * w_last[:, sl] + _dot3(sa, bh[:, sl], tn) + _dot3(vh, kh[:, sl], tn)
        mu = jnp.mean(o, axis=-1, keepdims=True)
        d = o - mu
        var = jnp.mean(d * d, axis=-1, keepdims=True)
        o_ref[:, sl] = d * lax.rsqrt(var + RWKV_GN_EPS)
    bonus = _dot_exact_rhs(r * k * rk_ref[...], ones_ref[...], nsplit=3) * v
    o_ref[...] = (o_ref[...] * lnw_ref[...] + lnb_ref[...] + bonus) * g_ref[...]


def _rwkv_scan(r, lw, k, v, kk, a, g, r_k, ln_w, ln_b):
    B, S, DR = r.shape
    L = min(L_CHUNK, S)
    tri = jnp.asarray(np.tril(np.ones((L, L))), BF16)
    row = lambda z: z.reshape(1, -1)
    tok = pl.BlockSpec((None, L, DR), lambda b, c: (b, c, 0))
    full = lambda z: pl.BlockSpec(z.shape, lambda b, c: (0,) * z.ndim)
    consts = [row(r_k), row(ln_w), row(ln_b), tri, _head_ones(DR, RWKV_HEAD)]
    return pl.pallas_call(
        _rwkv_kernel,
        grid=(B, S // L),
        in_specs=[tok] * 7 + [full(z) for z in consts],
        out_specs=tok,
        out_shape=jax.ShapeDtypeStruct((B, S, DR), F32),
        scratch_shapes=[pltpu.VMEM((N_RWKV_HEADS, RWKV_HEAD, RWKV_HEAD), F32)],
        compiler_params=_cparams(("arbitrary", "arbitrary")),
        name="rwkv_scan",
    )(r, lw, k, v, kk, a, g, *consts)


def _dsa_kernel(iq_ref, iw_ref, qa_ref, ik_ref, cl_ref, wuv_ref, triu_ref, o_ref, key_ref, bias_ref, *, topk):
    qi = pl.program_id(1)
    tq = iq_ref.shape[0]
    S = ik_ref.shape[0]
    ik = ik_ref[...]
    cl = cl_ref[...]
    iw = iw_ref[...]

    score = jnp.zeros((tq, S), F32)
    for h in range(IDX_HEADS):
        s = lax.dot_general(iq_ref[:, h * LANES:(h + 1) * LANES], ik, (((1,), (1,)), ((), ())),
                            preferred_element_type=F32)
        score = score + iw[:, h:h + 1] * jnp.maximum(s, 0.0)

    tpos = qi * tq + lax.broadcasted_iota(I32, (tq, 1), 0)
    spos = lax.broadcasted_iota(I32, (1, S), 1)
    causal = spos <= tpos
    bits = pltpu.bitcast(score + 0.0, I32)
    key = bits ^ ((bits >> 31) & 0x7FFFFFFF)
    key_ref[...] = jnp.where(causal, key, INT_MIN)
    kcount = jnp.minimum(topk, tpos + 1).astype(F32)

    def count_ge(cand):
        return jnp.sum(jnp.where(key_ref[...] >= cand, 1.0, 0.0), axis=-1, keepdims=True)

    thr = jnp.where(count_ge(jnp.zeros((tq, 1), I32)) >= kcount, 0, INT_MIN).astype(I32)

    def bit_step(i, thr):
        cand = thr | (1 << (30 - i))
        return jnp.where(count_ge(cand) >= kcount, cand, thr)

    thr = lax.fori_loop(0, 31, bit_step, thr)

    key = key_ref[...]
    gt = key > thr
    eq = key == thr
    need = kcount - jnp.sum(jnp.where(gt, 1.0, 0.0), axis=-1, keepdims=True)
    eqf = jnp.where(eq, 1.0, 0.0)
    carry = jnp.zeros((tq, 1), F32)
    for j in range(S // LANES):
        sl = slice(j * LANES, (j + 1) * LANES)
        e = eqf[:, sl]
        before = jnp.dot(e.astype(BF16), triu_ref[...], preferred_element_type=F32) + carry
        sel = gt[:, sl] | (eq[:, sl] & (before < need))
        bias_ref[:, sl] = jnp.where(sel, 0.0, NEG_BIG)
        carry = carry + jnp.sum(e, axis=-1, keepdims=True)

    dist = (tpos - spos).astype(F32)
    outs = []
    for h in range(N_ATT_HEADS):
        slope = 2.0 ** (-8.0 * (h + 1) / N_ATT_HEADS)
        lg = lax.dot_general(qa_ref[:, h * KV_LORA:(h + 1) * KV_LORA], cl, (((1,), (1,)), ((), ())),
                             preferred_element_type=F32)
        lg = lg - slope * dist + bias_ref[...]
        m = jnp.max(lg, axis=-1, keepdims=True)
        p = jnp.exp(lg - m)
        den = jnp.sum(p, axis=-1, keepdims=True)
        outs.append(jnp.dot(p.astype(BF16), cl, preferred_element_type=F32) / den)
    o_lat = jnp.concatenate(outs, axis=-1)
    o_ref[...] = _bdot(o_lat, wuv_ref[...])


def _dsa_attn(iq, iw, qabs, ik, cl, w_uv):
    B, S, _ = iq.shape
    tq = min(TQ, S)
    topk = min(TOPK_MAX, S // 4)
    wuv_bd = _block_diag(w_uv).astype(BF16)
    triu = jnp.asarray(np.triu(np.ones((LANES, LANES)), 1), BF16)
    tok = lambda n: pl.BlockSpec((None, tq, n), lambda b, i: (b, i, 0))
    seq = lambda n: pl.BlockSpec((None, S, n), lambda b, i: (b, 0, 0))
    full = lambda z: pl.BlockSpec(z.shape, lambda b, i: (0,) * z.ndim)
    return pl.pallas_call(
        functools.partial(_dsa_kernel, topk=topk),
        grid=(B, S // tq),
        in_specs=[tok(IDX_HEADS * LANES), tok(LANES), tok(N_ATT_HEADS * KV_LORA), seq(LANES), seq(KV_LORA),
                  full(wuv_bd), full(triu)],
        out_specs=tok(D_ATT),
        out_shape=jax.ShapeDtypeStruct((B, S, D_ATT), F32),
        scratch_shapes=[pltpu.VMEM((tq, S), I32), pltpu.VMEM((tq, S), F32)],
        compiler_params=_cparams(("arbitrary", "arbitrary")),
        name="dsa_attn",
    )(iq, iw, qabs, ik, cl, wuv_bd, triu)


def _layernorm_rows(y, g, b):
    mu = jnp.mean(y, axis=-1, keepdims=True)
    d = y - mu
    var = jnp.mean(d * d, axis=-1, keepdims=True)
    return d * lax.rsqrt(var + 1e-5) * g + b


def _mix_kernel(orw_ref, ods_ref, x_ref, gt_ref, sc_ref, sh_ref, wtop_ref, wbot_ref, g_ref, b_ref, wr_ref, br_ref,
                tril_ref, x1_o, u2_o, route_o, cnt_o, carry, *, alpha):
    first = (pl.program_id(0) == 0) & (pl.program_id(1) == 0)
    tm = x_ref.shape[0]

    @pl.when(first)
    def _():
        carry[...] = jnp.zeros_like(carry)

    mix = _bdot(orw_ref[...], wtop_ref[...]) + _bdot(ods_ref[...], wbot_ref[...])
    x1 = _layernorm_rows(alpha * x_ref[...] + (1.0 + gt_ref[...]) * mix, g_ref[...], b_ref[...])
    x1_o[...] = x1
    u2 = x1 * (1.0 + sc_ref[...]) + sh_ref[...]
    u2_o[...] = u2

    lg = _dot3(u2, wr_ref[...]) + br_ref[...]
    lane = lax.broadcasted_iota(I32, (tm, LANES), 1)
    idxs, vals = [], []
    for _ in range(TOP_K_EXPERTS):
        m = jnp.max(lg, axis=-1, keepdims=True)
        idx = jnp.min(jnp.where(lg == m, lane, LANES), axis=-1, keepdims=True)
        idxs.append(idx)
        vals.append(m)
        lg = jnp.where(lane == idx, -jnp.inf, lg)
    es = [jnp.exp(v - vals[0]) for v in vals]
    den = es[0] + es[1] + es[2] + es[3]
    hot = jnp.zeros((tm, LANES), F32)
    for idx in idxs:
        hot = hot + jnp.where(lane == idx, 1.0, 0.0)
    before = jnp.dot(tril_ref[...], hot.astype(BF16), preferred_element_type=F32) + carry[0:1, :]
    route = jnp.zeros((tm, LANES), F32)
    for k in range(TOP_K_EXPERTS):
        rank = jnp.sum(jnp.where(lane == idxs[k], before, 0.0), axis=-1, keepdims=True)
        route = jnp.where(lane == k, idxs[k].astype(F32), route)
        route = jnp.where(lane == TOP_K_EXPERTS + k, es[k] / den, route)
        route = jnp.where(lane == 2 * TOP_K_EXPERTS + k, rank, route)
    route_o[...] = route
    carry[0:1, :] = carry[0:1, :] + jnp.sum(hot, axis=0, keepdims=True)
    cnt_o[...] = carry[...]


def _mix_out(o_rwkv, o_dsa, x, gt1, sc2, sh2, w_out, ln_g, ln_b, w_router, b_router):
    B, S, D = x.shape
    tm = min(TM_PROJ, S)
    alpha = 2.0 ** 0.25
    wtop = w_out[:D_RWKV].astype(BF16)
    wbot = w_out[D_RWKV:].astype(BF16)
    wr_p = jnp.pad(w_router, ((0, 0), (0, LANES - N_EXPERTS)))
    br_p = jnp.pad(b_router.reshape(1, -1), ((0, 0), (0, LANES - N_EXPERTS)), constant_values=NEG_BIG)
    tril = jnp.asarray(np.tril(np.ones((tm, tm)), -1), BF16)
    row = lambda v: v.reshape(1, -1)
    tok = lambda n: pl.BlockSpec((None, tm, n), lambda b, i: (b, i, 0))
    mod = pl.BlockSpec((None, 1, D), lambda b, i: (b, 0, 0))
    full = lambda a: pl.BlockSpec(a.shape, lambda b, i: (0,) * a.ndim)
    consts = [wtop, wbot, row(ln_g), row(ln_b), wr_p, br_p, tril]
    return pl.pallas_call(
        functools.partial(_mix_kernel, alpha=alpha),
        grid=(B, S // tm),
        in_specs=[tok(D_RWKV), tok(D_ATT), tok(D), mod, mod, mod] + [full(a) for a in consts],
        out_specs=[tok(D), tok(D), tok(LANES), pl.BlockSpec((8, LANES), lambda b, i: (0, 0))],
        out_shape=[jax.ShapeDtypeStruct((B, S, D), F32), jax.ShapeDtypeStruct((B, S, D), F32),
                   jax.ShapeDtypeStruct((B, S, LANES), F32), jax.ShapeDtypeStruct((8, LANES), F32)],
        scratch_shapes=[pltpu.VMEM((8, LANES), F32)],
        compiler_params=_cparams(("arbitrary", "arbitrary")),
        name="mix_out",
    )(o_rwkv, o_dsa, x, gt1, sc2, sh2, *consts)


def _dispatch_kernel(pos_ref, u_ref, xs_in, xs_out, sem):
    del xs_in
    tm = u_ref.shape[0]

    def row_copy(r, dst):
        return pltpu.make_async_copy(u_ref.at[pl.ds(r, 1)], xs_out.at[pl.ds(dst, 1)], sem)

    def start(r, carry):
        for k in range(TOP_K_EXPERTS):
            row_copy(r, pos_ref[r * TOP_K_EXPERTS + k]).start()
        return carry

    lax.fori_loop(0, tm, start, 0)

    def wait(j, carry):
        row_copy(0, 0).wait()
        return carry

    lax.fori_loop(0, tm * TOP_K_EXPERTS, wait, 0)


def _moe_dispatch(u2, pos, n_rows):
    T, D = u2.shape
    tm = min(TM_ROUTE, T)
    xs0 = jnp.zeros((n_rows, D), F32)
    return pl.pallas_call(
        _dispatch_kernel,
        grid=(T // tm,),
        in_specs=[pl.BlockSpec((tm * TOP_K_EXPERTS,), lambda i: (i,), memory_space=pltpu.SMEM),
                  pl.BlockSpec((tm, D), lambda i: (i, 0)),
                  pl.BlockSpec(memory_space=pl.ANY)],
        out_specs=pl.BlockSpec(memory_space=pl.ANY),
        out_shape=jax.ShapeDtypeStruct((n_rows, D), F32),
        scratch_shapes=[pltpu.SemaphoreType.DMA(())],
        input_output_aliases={2: 0},
        compiler_params=_cparams(("arbitrary",)),
        name="moe_dispatch",
    )(pos, u2, xs0)


def _expert_kernel(be_ref, nb_ref, xs_ref, wg_ref, wu_ref, bg_ref, bu_ref, wd_ref, bd_ref, ys_ref):
    i = pl.program_id(0)

    @pl.when(i < nb_ref[0])
    def _():
        xb = xs_ref[...].astype(BF16)
        gate = jnp.dot(xb, wg_ref[...], preferred_element_type=F32) + bg_ref[...]
        up = jnp.dot(xb, wu_ref[...], preferred_element_type=F32) + bu_ref[...]
        gate = jnp.minimum(gate, SWIGLU_LIMIT)
        up = jnp.clip(up, -SWIGLU_LIMIT, SWIGLU_LIMIT)
        h = (up + 1.0) * (gate * _sigmoid(gate * SWIGLU_ALPHA))
        ys_ref[...] = _bdot(h, wd_ref[...]) + bd_ref[...]

    @pl.when(i >= nb_ref[0])
    def _():
        ys_ref[...] = jnp.zeros_like(ys_ref)


def _moe_experts(xs, block_e, n_used, w_g, w_u, b_g, b_u, w_dn, b_dn):
    n_rows, D = xs.shape
    E, _, Fh = w_g.shape
    bm = BM_EXPERT
    n_blocks = n_rows // bm
    wspec = lambda shp: pl.BlockSpec((None,) + shp, lambda i, be, nb: (be[i], 0, 0))
    return pl.pallas_call(
        _expert_kernel,
        grid_spec=pltpu.PrefetchScalarGridSpec(
            num_scalar_prefetch=2,
            grid=(n_blocks,),
            in_specs=[pl.BlockSpec((bm, D), lambda i, be, nb: (i, 0)),
                      wspec((D, Fh)), wspec((D, Fh)), wspec((1, Fh)), wspec((1, Fh)), wspec((Fh, D)), wspec((1, D))],
            out_specs=pl.BlockSpec((bm, D), lambda i, be, nb: (i, 0)),
        ),
        out_shape=jax.ShapeDtypeStruct((n_rows, D), F32),
        compiler_params=_cparams(("arbitrary",)),
        name="moe_experts",
    )(block_e, n_used, xs, w_g, w_u, b_g, b_u, w_dn, b_dn)


def _combine_kernel(pos_ref, ys_ref, x1_ref, route_ref, gt_ref, g_ref, b_ref, o_ref, buf, sem, *, alpha):
    tm = x1_ref.shape[0]

    def row_copy(r, k, src):
        return pltpu.make_async_copy(ys_ref.at[pl.ds(src, 1)], buf.at[k, pl.ds(r, 1)], sem)

    def start(r, carry):
        for k in range(TOP_K_EXPERTS):
            row_copy(r, k, pos_ref[r * TOP_K_EXPERTS + k]).start()
        return carry

    lax.fori_loop(0, tm, start, 0)

    def wait(j, carry):
        row_copy(0, 0, 0).wait()
        return carry

    lax.fori_loop(0, tm * TOP_K_EXPERTS, wait, 0)

    route = route_ref[...]
    ffn = jnp.zeros(x1_ref.shape, F32)
    for k in range(TOP_K_EXPERTS):
        ffn = ffn + route[:, TOP_K_EXPERTS + k:TOP_K_EXPERTS + k + 1] * buf[k]
    o_ref[...] = _layernorm_rows(alpha * x1_ref[...] + (1.0 + gt_ref[...]) * ffn, g_ref[...], b_ref[...])


def _moe_combine(ys, pos, x1, route, gt2, ln_g, ln_b, tiles_per_batch):
    T, D = x1.shape
    tm = min(TM_ROUTE, T)
    row = lambda v: v.reshape(1, -1)
    return pl.pallas_call(
        functools.partial(_combine_kernel, alpha=2.0 ** 0.25),
        grid=(T // tm,),
        in_specs=[pl.BlockSpec((tm * TOP_K_EXPERTS,), lambda i: (i,), memory_space=pltpu.SMEM),
                  pl.BlockSpec(memory_space=pl.ANY),
                  pl.BlockSpec((tm, D), lambda i: (i, 0)),
                  pl.BlockSpec((tm, LANES), lambda i: (i, 0)),
                  pl.BlockSpec((None, 1, D), lambda i: (i // tiles_per_batch, 0, 0)),
                  pl.BlockSpec((1, D), lambda i: (0, 0)),
                  pl.BlockSpec((1, D), lambda i: (0, 0))],
        out_specs=pl.BlockSpec((tm, D), lambda i: (i, 0)),
        out_shape=jax.ShapeDtypeStruct((T, D), F32),
        scratch_shapes=[pltpu.VMEM((TOP_K_EXPERTS, tm, D), F32), pltpu.SemaphoreType.DMA(())],
        compiler_params=_cparams(("arbitrary",)),
        name="moe_combine",
    )(pos, ys, x1, route, gt2, row(ln_g), row(ln_b))


def _moe_and_norm(x1, u2, route, counts, gt2, w_gu, b_gu, w_dn, b_dn, ln_g, ln_b):
    B, S, D = x1.shape
    T = B * S
    bm = BM_EXPERT
    n_blocks = -(-T * TOP_K_EXPERTS // bm) + N_EXPERTS
    cnt = counts[0, :N_EXPERTS].astype(I32)
    padded = (cnt + bm - 1) // bm * bm
    pad_ends = jnp.cumsum(padded)
    pad_starts = pad_ends - padded
    route2 = route.reshape(T, LANES)
    e_idx = route2[:, :TOP_K_EXPERTS].astype(I32)
    rank = route2[:, 2 * TOP_K_EXPERTS:3 * TOP_K_EXPERTS].astype(I32)
    pos = (pad_starts[e_idx] + rank).reshape(-1)
    block_e = jnp.minimum(jnp.searchsorted(pad_ends, jnp.arange(n_blocks, dtype=I32) * bm, side='right'),
                          N_EXPERTS - 1).astype(I32)
    n_used = (pad_ends[-1:] // bm).astype(I32)
    w_g = w_gu[:, :, 0::2].astype(BF16)
    w_u = w_gu[:, :, 1::2].astype(BF16)
    b_g = b_gu[:, None, 0::2]
    b_u = b_gu[:, None, 1::2]
    xs = _moe_dispatch(u2.reshape(T, D), pos, n_blocks * bm)
    ys = _moe_experts(xs, block_e, n_used, w_g, w_u, b_g, b_u, w_dn, b_dn[:, None, :])
    out = _moe_combine(ys, pos, x1.reshape(T, D), route2, gt2, ln_g, ln_b, S // min(TM_ROUTE, T))
    return out.reshape(B, S, D)


def kernel(x, c, w_ada, b_ada, w_in, shift_mu, rwkv_w0, rwkv_w2, rwkv_a0, rwkv_a2, rwkv_g2, rwkv_k_k, rwkv_k_a, rwkv_r_k, rwkv_ln_w, rwkv_ln_b, mla_q_norm, mla_w_q_up, mla_kv_norm, mla_w_uk, mla_w_uv, idx_w_q, idx_ln_g, idx_ln_b, w_out, ln1_g, ln1_b, w_router, b_router, w_gu, b_gu, w_dn, b_dn, ln2_g, ln2_b):
    depth = w_ada.shape[0]
    assert depth == 1, "DeepNorm constants below are for a single layer"
    l = 0
    mod = _ada_mod(c, w_ada[l], b_ada[l])
    sh1, sc1, gt1, sh2, sc2, gt2 = [m[:, None, :] for m in jnp.split(mod, 6, axis=-1)]
    r, lw, k, v, kk, a, g, qabs, iq, ik, iw, cl = _in_proj(
        x, sc1, sh1, w_in[l], shift_mu[l], rwkv_w0[l], rwkv_w2[l], rwkv_a0[l], rwkv_a2[l], rwkv_g2[l], rwkv_k_k[l],
        rwkv_k_a[l], mla_q_norm[l], mla_w_q_up[l], mla_w_uk[l], mla_kv_norm[l], idx_w_q[l], idx_ln_g[l], idx_ln_b[l])
    o_rwkv = _rwkv_scan(r, lw, k, v, kk, a, g, rwkv_r_k[l], rwkv_ln_w[l], rwkv_ln_b[l])
    o_dsa = _dsa_attn(iq, iw, qabs, ik, cl, mla_w_uv[l])
    x1, u2, route, counts = _mix_out(o_rwkv, o_dsa, x, gt1, sc2, sh2, w_out[l], ln1_g[l], ln1_b[l], w_router[l],
                                     b_router[l])
    return _moe_and_norm(x1, u2, route, counts, gt2, w_gu[l], b_gu[l], w_dn[l], b_dn[l], ln2_g[l], ln2_b[l])
```

```python
import functools
import math

import jax
import jax.numpy as jnp
import numpy as np
from jax import lax
from jax.experimental import pallas as pl
from jax.experimental.pallas import tpu as pltpu

F32 = jnp.float32
BF16 = jnp.bfloat16
I32 = jnp.int32

RWKV_HEAD = 64
N_RWKV_HEADS = 8
D_RWKV = RWKV_HEAD * N_RWKV_HEADS
RANK_DECAY = 64
RANK_A = 64
RANK_GATE = 128
RWKV_GN_EPS = 64e-5
ATT_HEAD = 64
N_ATT_HEADS = 8
D_ATT = ATT_HEAD * N_ATT_HEADS
Q_LORA = 256
KV_LORA = 128
IDX_HEADS = 8
IDX_DIM = 64
TOPK_MAX = 256
N_EXPERTS = 32
TOP_K_EXPERTS = 4
SWIGLU_LIMIT = 7.0
SWIGLU_ALPHA = 1.702
NEG_BIG = -1e30
INT_MIN = -(2 ** 31)

LANES = 128
VMEM_LIMIT = 48 * 1024 * 1024
VMEM_LIMIT_EXPERTS = 58 * 1024 * 1024

TM_PROJ = 256
L_CHUNK = 64
TQ = 128
TM_ROUTE = 256
BM_EXPERT = 256

_SEG = (("r", 512, 512), ("k", 512, 512), ("v", 512, 512), ("w", 64, 128), ("a", 64, 128), ("g", 128, 128),
        ("q", 256, 256), ("kv", 128, 128), ("ik", 64, 128), ("iw", 8, 128))
N_SHIFT_P = 512 * 3 + 128 * 3
N_IN_P = sum(s[2] for s in _SEG)


def _cparams(sem):
    return pltpu.CompilerParams(dimension_semantics=sem, vmem_limit_bytes=VMEM_LIMIT)


def _bdot(a, b):
    return jnp.dot(a.astype(BF16), b.astype(BF16), preferred_element_type=F32)


def _bdot_nt(a, b):
    return lax.dot_general(a.astype(BF16), b.astype(BF16), (((1,), (1,)), ((), ())), preferred_element_type=F32)


def _bdot_tn(a, b):
    return lax.dot_general(a.astype(BF16), b.astype(BF16), (((0,), (0,)), ((), ())), preferred_element_type=F32)


def _split2(a):
    hi = a.astype(BF16)
    lo = (a - hi.astype(F32)).astype(BF16)
    return hi, lo


def _split3(a):
    hi = a.astype(BF16)
    r1 = a - hi.astype(F32)
    mid = r1.astype(BF16)
    lo = (r1 - mid.astype(F32)).astype(BF16)
    return hi, mid, lo


def _dot3(a, b, dims=(((1,), (0,)), ((), ()))):
    ah, al = _split2(a)
    bh, bl = _split2(b)
    d = functools.partial(lax.dot_general, dimension_numbers=dims, preferred_element_type=F32)
    return d(ah, bh) + (d(ah, bl) + d(al, bh))


def _dot_exact_rhs(a, b_exact, nsplit=3):
    parts = _split3(a) if nsplit == 3 else _split2(a)
    acc = None
    for p in parts[::-1]:
        t = jnp.dot(p, b_exact, preferred_element_type=F32)
        acc = t if acc is None else acc + t
    return acc


def _dot_exact_lhs(a_exact, b, nsplit=3):
    parts = _split3(b) if nsplit == 3 else _split2(b)
    acc = None
    for p in parts[::-1]:
        t = jnp.dot(a_exact, p, preferred_element_type=F32)
        acc = t if acc is None else acc + t
    return acc


def _sigmoid(x):
    return 1.0 / (1.0 + jnp.exp(-x))


def _softplus(x):
    return jnp.maximum(x, 0.0) + jnp.log(1.0 + jnp.exp(-jnp.abs(x)))


def _ada_kernel(c_ref, w_ref, b_ref, o_ref):
    c = c_ref[...]
    o_ref[...] = _dot3(c * _sigmoid(c), w_ref[...]) + b_ref[...]


def _ada_mod(c, w_ada, b_ada):
    B, D = c.shape
    N = w_ada.shape[1]
    tn = 1024
    return pl.pallas_call(
        _ada_kernel,
        grid=(N // tn,),
        in_specs=[pl.BlockSpec((B, D), lambda j: (0, 0)),
                  pl.BlockSpec((D, tn), lambda j: (0, j)),
                  pl.BlockSpec((1, tn), lambda j: (0, j))],
        out_specs=pl.BlockSpec((B, tn), lambda j: (0, j)),
        out_shape=jax.ShapeDtypeStruct((B, N), F32),
        compiler_params=_cparams(("arbitrary",)),
        name="ada_mod",
    )(c, w_ada, b_ada.reshape(1, N))


def _in_proj_kernel(x_ref, sc_ref, sh_ref, win_ref, mu_ref, w0_ref, w2_ref, a0_ref, a2_ref, g2_ref, kk_ref, ka_ref,
                    ones_ref, qn_ref, wq_ref, wuk_ref, kvn_ref, wiq_ref, ig_ref, ib_ref,
                    r_o, lw_o, k_o, v_o, kkn_o, a_o, g_o, qabs_o, iq_o, ik_o, iw_o, cl_o, carry):
    i = pl.program_id(1)
    tm = x_ref.shape[0]

    @pl.when(i == 0)
    def _():
        carry[...] = jnp.zeros_like(carry)

    u = x_ref[...] * (1.0 + sc_ref[...]) + sh_ref[...]
    p = _bdot(u, win_ref[...])
    ps = p[:, :N_SHIFT_P]
    rows = lax.broadcasted_iota(I32, (tm, 1), 0)
    prev = jnp.where(rows == 0, carry[0:1, :], pltpu.roll(ps, 1, 0))
    carry[0:1, :] = ps[tm - 1:tm, :]
    ps = ps + mu_ref[...] * (prev - ps)

    pr, pk, pv = ps[:, 0:512], ps[:, 512:1024], ps[:, 1024:1536]
    pw, pa, pg = ps[:, 1536:1664], ps[:, 1664:1792], ps[:, 1792:1920]
    w_log = -_softplus(-(w0_ref[...] + _dot3(jnp.tanh(pw), w2_ref[...]))) - 0.5
    lw_o[...] = -jnp.exp(w_log)
    a = _sigmoid(a0_ref[...] + _dot3(pa, a2_ref[...]))
    g_o[...] = _dot3(_sigmoid(pg), g2_ref[...])
    kk = pk * kk_ref[...]
    ssq = _dot_exact_rhs(kk * kk, ones_ref[...], nsplit=2)
    kkn_o[...] = kk / jnp.maximum(jnp.sqrt(ssq), 1e-12)
    k_o[...] = pk * (1.0 + (a - 1.0) * ka_ref[...])
    r_o[...] = pr
    v_o[...] = pv
    a_o[...] = a

    pq, pkv = p[:, 1920:2176], p[:, 2176:2304]
    pik, piw = p[:, 2304:2432], p[:, 2432:2560]
    q_lat = pq * lax.rsqrt(jnp.mean(pq * pq, axis=-1, keepdims=True) + 1e-6) * qn_ref[...]
    q = _bdot(q_lat, wq_ref[...])
    qabs_o[...] = (_bdot(q, wuk_ref[...]) * (ATT_HEAD ** -0.5)).astype(BF16)
    cl_o[...] = (pkv * lax.rsqrt(jnp.mean(pkv * pkv, axis=-1, keepdims=True) + 1e-6) * kvn_ref[...]).astype(BF16)
    iq_o[...] = (_bdot(q_lat, wiq_ref[...]) * (IDX_DIM ** -0.5)).astype(BF16)
    lane = lax.broadcasted_iota(I32, (1, LANES), 1)
    valid = lane < IDX_DIM
    mu = jnp.sum(pik, axis=-1, keepdims=True) * (1.0 / IDX_DIM)
    dlt = jnp.where(valid, pik - mu, 0.0)
    var = jnp.sum(dlt * dlt, axis=-1, keepdims=True) * (1.0 / IDX_DIM)
    ik_o[...] = jnp.where(valid, dlt * lax.rsqrt(var + 1e-5) * ig_ref[...] + ib_ref[...], 0.0).astype(BF16)
    iw_o[...] = piw * (IDX_HEADS ** -0.5)


def _pad_cols(w, widths):
    parts, o = [], 0
    for true, padded in widths:
        seg = w[..., o:o + true]
        if padded > true:
            seg = jnp.pad(seg, [(0, 0)] * (w.ndim - 1) + [(0, padded - true)])
        parts.append(seg)
        o += true
    return jnp.concatenate(parts, axis=-1)


def _pad_rows(w, rows):
    return jnp.pad(w, ((0, rows - w.shape[0]), (0, 0)))


def _block_diag(blocks):
    H, a, b = blocks.shape
    eye = jnp.eye(H, dtype=blocks.dtype)
    return (eye[:, None, :, None] * blocks[:, :, None, :]).reshape(H * a, H * b)


def _head_ones(n, head):
    idx = np.arange(n) // head
    return jnp.asarray(idx[:, None] == idx[None, :], BF16)


def _in_proj(x, sc1, sh1, w_in, shift_mu, w0, w2, a0, a2, g2, k_k, k_a, q_norm, w_q_up, w_uk, kv_norm, idx_w_q,
             idx_ln_g, idx_ln_b):
    B, S, D = x.shape
    tm = min(TM_PROJ, S)
    widths = tuple((s[1], s[2]) for s in _SEG)
    win_p = _pad_cols(w_in, widths).astype(BF16)
    mu_p = _pad_cols(shift_mu.reshape(1, -1), widths[:6])
    w2_p = _pad_rows(w2, LANES)
    a2_p = _pad_rows(a2, LANES)
    wuk_bd = _block_diag(w_uk).astype(BF16)
    wiq_p = _pad_cols(idx_w_q, ((IDX_DIM, LANES),) * IDX_HEADS).astype(BF16)
    ig_p = _pad_cols(idx_ln_g.reshape(1, -1), ((IDX_DIM, LANES),))
    ib_p = _pad_cols(idx_ln_b.reshape(1, -1), ((IDX_DIM, LANES),))
    row = lambda v: v.reshape(1, -1)
    tok = lambda n: pl.BlockSpec((None, tm, n), lambda b, i: (b, i, 0))
    mod = pl.BlockSpec((None, 1, D), lambda b, i: (b, 0, 0))
    full = lambda a: pl.BlockSpec(a.shape, lambda b, i: (0,) * a.ndim)
    consts = [win_p, mu_p, row(w0), w2_p, row(a0), a2_p, g2, row(k_k), row(k_a), _head_ones(D_RWKV, RWKV_HEAD),
              row(q_norm), w_q_up.astype(BF16), wuk_bd, row(kv_norm), wiq_p, ig_p, ib_p]
    outs = [(D_RWKV, F32)] * 7 + [(N_ATT_HEADS * KV_LORA, BF16), (IDX_HEADS * LANES, BF16), (LANES, BF16),
                                  (LANES, F32), (KV_LORA, BF16)]
    return pl.pallas_call(
        _in_proj_kernel,
        grid=(B, S // tm),
        in_specs=[tok(D), mod, mod] + [full(a) for a in consts],
        out_specs=[tok(n) for n, _ in outs],
        out_shape=[jax.ShapeDtypeStruct((B, S, n), dt) for n, dt in outs],
        scratch_shapes=[pltpu.VMEM((8, N_SHIFT_P), F32)],
        compiler_params=_cparams(("arbitrary", "arbitrary")),
        name="in_proj",
    )(x, sc1, sh1, *consts)


def _rwkv_kernel(r_ref, lw_ref, k_ref, v_ref, kk_ref, a_ref, g_ref, rk_ref, lnw_ref, lnb_ref, tri_ref, ones_ref,
                 o_ref, state):
    c = pl.program_id(1)
    L = r_ref.shape[0]

    @pl.when(c == 0)
    def _():
        state[...] = jnp.zeros_like(state)

    r, lw, k, v, kk, a = r_ref[...], lw_ref[...], k_ref[...], v_ref[...], kk_ref[...], a_ref[...]
    cum = _dot_exact_lhs(tri_ref[...], lw)
    cum_last = cum[L - 1:L, :]
    w_incl = jnp.exp(cum)
    w_inv = jnp.exp(-cum)
    w_rel = jnp.exp(cum_last - cum)
    w_last = jnp.exp(cum_last)
    bvec = kk * a
    at = -kk * jnp.exp(cum - lw)
    rt = r * w_incl
    bt = bvec * w_inv
    kt = k * w_inv
    bh = bvec * w_rel
    kh = k * w_rel
    ti = lax.broadcasted_iota(I32, (L, L), 0)
    tj = lax.broadcasted_iota(I32, (L, L), 1)
    strict = tj < ti
    incl = tj <= ti
    eye = lax.broadcasted_iota(I32, (RWKV_HEAD, RWKV_HEAD), 0) == lax.broadcasted_iota(I32, (RWKV_HEAD, RWKV_HEAD), 1)
    NT = (((1,), (1,)), ((), ()))
    TN = (((0,), (0,)), ((), ()))
    heads = range(N_RWKV_HEADS)
    sls = [slice(h * RWKV_HEAD, (h + 1) * RWKV_HEAD) for h in heads]
    vh = [v[:, s] for s in sls]
    ath = [at[:, s] for s in sls]
    n_ab = [jnp.where(strict, _dot3(ath[h], bt[:, sls[h]], NT), 0.0) for h in heads]
    a_ak = [jnp.where(strict, _dot3(ath[h], kt[:, sls[h]], NT), 0.0) for h in heads]
    a_rb = [jnp.where(incl, _dot3(rt[:, sls[h]], bt[:, sls[h]], NT), 0.0) for h in heads]
    a_rk = [jnp.where(incl, _dot3(rt[:, sls[h]], kt[:, sls[h]], NT), 0.0) for h in heads]
    akv = [_dot3(a_ak[h], vh[h]) for h in heads]
    eye_l = jnp.where(ti == tj, 1.0, 0.0)
    tinv = [eye_l + n_ab[h] for h in heads]
    pw = n_ab
    for _ in range(int(math.log2(L)) - 1):
        pw = [_dot3(pw[h], pw[h]) for h in heads]
        tinv = [tinv[h] + _dot3(pw[h], tinv[h]) for h in heads]
    a_t = [_dot3(tinv[h], ath[h]) for h in heads]
    y = [_dot3(tinv[h], akv[h]) for h in heads]
    m_c = [jnp.where(eye, w_last[:, sls[h]], 0.0) + _dot3(a_t[h], bh[:, sls[h]], TN) for h in heads]
    c_c = [_dot3(y[h], bh[:, sls[h]], TN) + _dot3(vh[h], kh[:, sls[h]], TN) for h in heads]
    q_c = [rt[:, sls[h]] + _dot3(a_rb[h], a_t[h]) for h in heads]
    o_loc = [_dot3(a_rb[h], y[h]) + _dot3(a_rk[h], vh[h]) for h in heads]
    s0 = [state[h] for h in heads]
    o = [o_loc[h] + _dot3(q_c[h], s0[h], NT) for h in heads]
    for h in heads:
        state[h] = _dot3(s0[h], m_c[h]) + c_c[h]
    for h in heads:
        mu = jnp.mean(o[h], axis=-1, keepdims=True)
        d = o[h] - mu
        var = jnp.mean(d * d, axis=-1, keepdims=True)
        o_ref[:, sls[h]] = d * lax.rsqrt(var + RWKV_GN_EPS)
    bonus = _dot_exact_rhs(r * k * rk_ref[...], ones_ref[...], nsplit=3) * v
    o_ref[...] = (o_ref[...] * lnw_ref[...] + lnb_ref[...] + bonus) * g_ref[...]


def _rwkv_scan(r, lw, k, v, kk, a, g, r_k, ln_w, ln_b):
    B, S, DR = r.shape
    L = min(L_CHUNK, S)
    tri = jnp.asarray(np.tril(np.ones((L, L))), BF16)
    row = lambda z: z.reshape(1, -1)
    tok = pl.BlockSpec((None, L, DR), lambda b, c: (b, c, 0))
    full = lambda z: pl.BlockSpec(z.shape, lambda b, c: (0,) * z.ndim)
    consts = [row(r_k), row(ln_w), row(ln_b), tri, _head_ones(DR, RWKV_HEAD)]
    return pl.pallas_call(
        _rwkv_kernel,
        grid=(B, S // L),
        in_specs=[tok] * 7 + [full(z) for z in consts],
        out_specs=tok,
        out_shape=jax.ShapeDtypeStruct((B, S, DR), F32),
        scratch_shapes=[pltpu.VMEM((N_RWKV_HEADS, RWKV_HEAD, RWKV_HEAD), F32)],
        compiler_params=_cparams(("arbitrary", "arbitrary")),
        name="rwkv_scan",
    )(r, lw, k, v, kk, a, g, *consts)


def _dsa_kernel(iq_ref, iw_ref, qa_ref, ik_ref, cl_ref, wuv_ref, triu_ref, o_ref, key_ref, bias_ref, *, topk):
    qi = pl.program_id(1)
    tq = iq_ref.shape[0]
    S = ik_ref.shape[0]
    ik = ik_ref[...]
    cl = cl_ref[...]
    iw = iw_ref[...]

    score = jnp.zeros((tq, S), F32)
    for h in range(IDX_HEADS):
        s = lax.dot_general(iq_ref[:, h * LANES:(h + 1) * LANES], ik, (((1,), (1,)), ((), ())),
                            preferred_element_type=F32)
        score = score + iw[:, h:h + 1] * jnp.maximum(s, 0.0)

    tpos = qi * tq + lax.broadcasted_iota(I32, (tq, 1), 0)
    spos = lax.broadcasted_iota(I32, (1, S), 1)
    causal = spos <= tpos
    bits = pltpu.bitcast(score + 0.0, I32)
    key = bits ^ ((bits >> 31) & 0x7FFFFFFF)
    key_ref[...] = jnp.where(causal, key, INT_MIN)
    kcount = jnp.minimum(topk, tpos + 1).astype(F32)

    def count_ge(cand):
        return jnp.sum(jnp.where(key_ref[...] >= cand, 1.0, 0.0), axis=-1, keepdims=True)

    thr = jnp.where(count_ge(jnp.zeros((tq, 1), I32)) >= kcount, 0, INT_MIN).astype(I32)

    def bit_step(i, thr):
        cand = thr | (1 << (30 - i))
        return jnp.where(count_ge(cand) >= kcount, cand, thr)

    thr = lax.fori_loop(0, 31, bit_step, thr)

    key = key_ref[...]
    gt = key > thr
    eq = key == thr
    need = kcount - jnp.sum(jnp.where(gt, 1.0, 0.0), axis=-1, keepdims=True)
    eqf = jnp.where(eq, 1.0, 0.0)
    carry = jnp.zeros((tq, 1), F32)
    for j in range(S // LANES):
        sl = slice(j * LANES, (j + 1) * LANES)
        e = eqf[:, sl]
        before = jnp.dot(e.astype(BF16), triu_ref[...], preferred_element_type=F32) + carry
        sel = gt[:, sl] | (eq[:, sl] & (before < need))
        bias_ref[:, sl] = jnp.where(sel, 0.0, NEG_BIG)
        carry = carry + jnp.sum(e, axis=-1, keepdims=True)

    dist = (tpos - spos).astype(F32)
    outs = []
    for h in range(N_ATT_HEADS):
        slope = 2.0 ** (-8.0 * (h + 1) / N_ATT_HEADS)
        lg = lax.dot_general(qa_ref[:, h * KV_LORA:(h + 1) * KV_LORA], cl, (((1,), (1,)), ((), ())),
                             preferred_element_type=F32)
        lg = lg - slope * dist + bias_ref[...]
        m = jnp.max(lg, axis=-1, keepdims=True)
        p = jnp.exp(lg - m)
        den = jnp.sum(p, axis=-1, keepdims=True)
        outs.append(jnp.dot(p.astype(BF16), cl, preferred_element_type=F32) / den)
    o_lat = jnp.concatenate(outs, axis=-1)
    o_ref[...] = _bdot(o_lat, wuv_ref[...])


def _dsa_attn(iq, iw, qabs, ik, cl, w_uv):
    B, S, _ = iq.shape
    tq = min(TQ, S)
    topk = min(TOPK_MAX, S // 4)
    wuv_bd = _block_diag(w_uv).astype(BF16)
    triu = jnp.asarray(np.triu(np.ones((LANES, LANES)), 1), BF16)
    tok = lambda n: pl.BlockSpec((None, tq, n), lambda b, i: (b, i, 0))
    seq = lambda n: pl.BlockSpec((None, S, n), lambda b, i: (b, 0, 0))
    full = lambda z: pl.BlockSpec(z.shape, lambda b, i: (0,) * z.ndim)
    return pl.pallas_call(
        functools.partial(_dsa_kernel, topk=topk),
        grid=(B, S // tq),
        in_specs=[tok(IDX_HEADS * LANES), tok(LANES), tok(N_ATT_HEADS * KV_LORA), seq(LANES), seq(KV_LORA),
                  full(wuv_bd), full(triu)],
        out_specs=tok(D_ATT),
        out_shape=jax.ShapeDtypeStruct((B, S, D_ATT), F32),
        scratch_shapes=[pltpu.VMEM((tq, S), I32), pltpu.VMEM((tq, S), F32)],
        compiler_params=_cparams(("arbitrary", "arbitrary")),
        name="dsa_attn",
    )(iq, iw, qabs, ik, cl, wuv_bd, triu)


def _layernorm_rows(y, g, b):
    mu = jnp.mean(y, axis=-1, keepdims=True)
    d = y - mu
    var = jnp.mean(d * d, axis=-1, keepdims=True)
    return d * lax.rsqrt(var + 1e-5) * g + b


def _mix_kernel(orw_ref, ods_ref, x_ref, gt_ref, sc_ref, sh_ref, wtop_ref, wbot_ref, g_ref, b_ref, wr_ref, br_ref,
                tril_ref, x1_o, u2_o, route_o, cnt_o, carry, *, alpha):
    first = (pl.program_id(0) == 0) & (pl.program_id(1) == 0)
    tm = x_ref.shape[0]

    @pl.when(first)
    def _():
        carry[...] = jnp.zeros_like(carry)

    mix = _bdot(orw_ref[...], wtop_ref[...]) + _bdot(ods_ref[...], wbot_ref[...])
    x1 = _layernorm_rows(alpha * x_ref[...] + (1.0 + gt_ref[...]) * mix, g_ref[...], b_ref[...])
    x1_o[...] = x1
    u2 = x1 * (1.0 + sc_ref[...]) + sh_ref[...]
    u2_o[...] = u2

    lg = _dot3(u2, wr_ref[...]) + br_ref[...]
    lane = lax.broadcasted_iota(I32, (tm, LANES), 1)
    idxs, vals = [], []
    for _ in range(TOP_K_EXPERTS):
        m = jnp.max(lg, axis=-1, keepdims=True)
        idx = jnp.min(jnp.where(lg == m, lane, LANES), axis=-1, keepdims=True)
        idxs.append(idx)
        vals.append(m)
        lg = jnp.where(lane == idx, -jnp.inf, lg)
    es = [jnp.exp(v - vals[0]) for v in vals]
    den = es[0] + es[1] + es[2] + es[3]
    hot = jnp.zeros((tm, LANES), F32)
    for idx in idxs:
        hot = hot + jnp.where(lane == idx, 1.0, 0.0)
    before = jnp.dot(tril_ref[...], hot.astype(BF16), preferred_element_type=F32) + carry[0:1, :]
    route = jnp.zeros((tm, LANES), F32)
    for k in range(TOP_K_EXPERTS):
        rank = jnp.sum(jnp.where(lane == idxs[k], before, 0.0), axis=-1, keepdims=True)
        route = jnp.where(lane == k, idxs[k].astype(F32), route)
        route = jnp.where(lane == TOP_K_EXPERTS + k, es[k] / den, route)
        route = jnp.where(lane == 2 * TOP_K_EXPERTS + k, rank, route)
    route_o[...] = route
    carry[0:1, :] = carry[0:1, :] + jnp.sum(hot, axis=0, keepdims=True)
    cnt_o[...] = carry[...]


def _mix_out(o_rwkv, o_dsa, x, gt1, sc2, sh2, w_out, ln_g, ln_b, w_router, b_router):
    B, S, D = x.shape
    tm = min(TM_PROJ, S)
    alpha = 2.0 ** 0.25
    wtop = w_out[:D_RWKV].astype(BF16)
    wbot = w_out[D_RWKV:].astype(BF16)
    wr_p = jnp.pad(w_router, ((0, 0), (0, LANES - N_EXPERTS)))
    br_p = jnp.pad(b_router.reshape(1, -1), ((0, 0), (0, LANES - N_EXPERTS)), constant_values=NEG_BIG)
    tril = jnp.asarray(np.tril(np.ones((tm, tm)), -1), BF16)
    row = lambda v: v.reshape(1, -1)
    tok = lambda n: pl.BlockSpec((None, tm, n), lambda b, i: (b, i, 0))
    mod = pl.BlockSpec((None, 1, D), lambda b, i: (b, 0, 0))
    full = lambda a: pl.BlockSpec(a.shape, lambda b, i: (0,) * a.ndim)
    consts = [wtop, wbot, row(ln_g), row(ln_b), wr_p, br_p, tril]
    return pl.pallas_call(
        functools.partial(_mix_kernel, alpha=alpha),
        grid=(B, S // tm),
        in_specs=[tok(D_RWKV), tok(D_ATT), tok(D), mod, mod, mod] + [full(a) for a in consts],
        out_specs=[tok(D), tok(D), tok(LANES), pl.BlockSpec((8, LANES), lambda b, i: (0, 0))],
        out_shape=[jax.ShapeDtypeStruct((B, S, D), F32), jax.ShapeDtypeStruct((B, S, D), F32),
                   jax.ShapeDtypeStruct((B, S, LANES), F32), jax.ShapeDtypeStruct((8, LANES), F32)],
        scratch_shapes=[pltpu.VMEM((8, LANES), F32)],
        compiler_params=_cparams(("arbitrary", "arbitrary")),
        name="mix_out",
    )(o_rwkv, o_dsa, x, gt1, sc2, sh2, *consts)


def _dispatch_kernel(pos_ref, u_ref, xs_in, xs_out, sem):
    del xs_in
    tm = u_ref.shape[0]

    def row_copy(r, dst):
        return pltpu.make_async_copy(u_ref.at[pl.ds(r, 1)], xs_out.at[pl.ds(dst, 1)], sem)

    def start(r, carry):
        for k in range(TOP_K_EXPERTS):
            row_copy(r, pos_ref[r * TOP_K_EXPERTS + k]).start()
        return carry

    lax.fori_loop(0, tm, start, 0)

    def wait(j, carry):
        row_copy(0, 0).wait()
        return carry

    lax.fori_loop(0, tm * TOP_K_EXPERTS, wait, 0)


def _moe_dispatch(u2, pos, n_rows):
    T, D = u2.shape
    tm = min(TM_ROUTE, T)
    xs0 = jnp.zeros((n_rows, D), F32)
    return pl.pallas_call(
        _dispatch_kernel,
        grid=(T // tm,),
        in_specs=[pl.BlockSpec((tm * TOP_K_EXPERTS,), lambda i: (i,), memory_space=pltpu.SMEM),
                  pl.BlockSpec((tm, D), lambda i: (i, 0)),
                  pl.BlockSpec(memory_space=pl.ANY)],
        out_specs=pl.BlockSpec(memory_space=pl.ANY),
        out_shape=jax.ShapeDtypeStruct((n_rows, D), F32),
        scratch_shapes=[pltpu.SemaphoreType.DMA(())],
        input_output_aliases={2: 0},
        compiler_params=_cparams(("arbitrary",)),
        name="moe_dispatch",
    )(pos, u2, xs0)


GU_GROUP = 2 * LANES


def _deinterleave_perm():
    p = np.zeros((GU_GROUP, GU_GROUP), np.float32)
    l = np.arange(LANES)
    p[2 * l, l] = 1.0
    p[2 * l + 1, LANES + l] = 1.0
    return jnp.asarray(p, BF16)


def _expert_kernel(be_ref, nb_ref, xs_ref, wgu_ref, bgu_ref, wd_ref, bd_ref, perm_ref, ys_ref, wp, wdb):
    i = pl.program_id(0)
    used = i < nb_ref[0]
    new_expert = (i == 0) | (be_ref[i] != be_ref[jnp.maximum(i - 1, 0)])
    n_groups = wgu_ref.shape[1] // GU_GROUP

    @pl.when(used & new_expert)
    def _():
        for j in range(n_groups):
            sl = slice(j * GU_GROUP, (j + 1) * GU_GROUP)
            wp[:, sl] = jnp.dot(wgu_ref[:, sl].astype(BF16), perm_ref[...], preferred_element_type=F32).astype(BF16)
        wdb[...] = wd_ref[...].astype(BF16)

    @pl.when(used)
    def _():
        xb = xs_ref[...].astype(BF16)
        gu = jnp.dot(xb, wp[...], preferred_element_type=F32) + bgu_ref[...]
        hs = []
        for j in range(n_groups):
            gate = jnp.minimum(gu[:, j * GU_GROUP:j * GU_GROUP + LANES], SWIGLU_LIMIT)
            up = jnp.clip(gu[:, j * GU_GROUP + LANES:(j + 1) * GU_GROUP], -SWIGLU_LIMIT, SWIGLU_LIMIT)
            hs.append(((up + 1.0) * (gate * _sigmoid(gate * SWIGLU_ALPHA))).astype(BF16))
        h = jnp.concatenate(hs, axis=-1)
        ys_ref[...] = jnp.dot(h, wdb[...], preferred_element_type=F32) + bd_ref[...]

    @pl.when(jnp.logical_not(used))
    def _():
        ys_ref[...] = jnp.zeros_like(ys_ref)


def _moe_experts(xs, block_e, n_used, w_gu, b_gu_p, w_dn, b_dn):
    n_rows, D = xs.shape
    E, _, F2 = w_gu.shape
    bm = BM_EXPERT
    n_blocks = n_rows // bm
    perm = _deinterleave_perm()
    wspec = lambda shp: pl.BlockSpec((None,) + shp, lambda i, be, nb: (be[i], 0, 0))
    return pl.pallas_call(
        _expert_kernel,
        grid_spec=pltpu.PrefetchScalarGridSpec(
            num_scalar_prefetch=2,
            grid=(n_blocks,),
            in_specs=[pl.BlockSpec((bm, D), lambda i, be, nb: (i, 0)),
                      wspec((D, F2)), wspec((1, F2)), wspec((F2 // 2, D)), wspec((1, D)),
                      pl.BlockSpec(perm.shape, lambda i, be, nb: (0, 0))],
            out_specs=pl.BlockSpec((bm, D), lambda i, be, nb: (i, 0)),
            scratch_shapes=[pltpu.VMEM((D, F2), BF16), pltpu.VMEM((F2 // 2, D), BF16)],
        ),
        out_shape=jax.ShapeDtypeStruct((n_rows, D), F32),
        compiler_params=pltpu.CompilerParams(dimension_semantics=("arbitrary",), vmem_limit_bytes=VMEM_LIMIT_EXPERTS),
        name="moe_experts",
    )(block_e, n_used, xs, w_gu, b_gu_p, w_dn, b_dn, perm)


def _combine_kernel(pos_ref, ys_ref, x1_ref, route_ref, gt_ref, g_ref, b_ref, o_ref, buf, sem, *, alpha):
    tm = x1_ref.shape[0]

    def row_copy(r, k, src):
        return pltpu.make_async_copy(ys_ref.at[pl.ds(src, 1)], buf.at[k, pl.ds(r, 1)], sem)

    def start(r, carry):
        for k in range(TOP_K_EXPERTS):
            row_copy(r, k, pos_ref[r * TOP_K_EXPERTS + k]).start()
        return carry

    lax.fori_loop(0, tm, start, 0)

    def wait(j, carry):
        row_copy(0, 0, 0).wait()
        return carry

    lax.fori_loop(0, tm * TOP_K_EXPERTS, wait, 0)

    route = route_ref[...]
    ffn = jnp.zeros(x1_ref.shape, F32)
    for k in range(TOP_K_EXPERTS):
        ffn = ffn + route[:, TOP_K_EXPERTS + k:TOP_K_EXPERTS + k + 1] * buf[k]
    o_ref[...] = _layernorm_rows(alpha * x1_ref[...] + (1.0 + gt_ref[...]) * ffn, g_ref[...], b_ref[...])


def _moe_combine(ys, pos, x1, route, gt2, ln_g, ln_b, tiles_per_batch):
    T, D = x1.shape
    tm = min(TM_ROUTE, T)
    row = lambda v: v.reshape(1, -1)
    return pl.pallas_call(
        functools.partial(_combine_kernel, alpha=2.0 ** 0.25),
        grid=(T // tm,),
        in_specs=[pl.BlockSpec((tm * TOP_K_EXPERTS,), lambda i: (i,), memory_space=pltpu.SMEM),
                  pl.BlockSpec(memory_space=pl.ANY),
                  pl.BlockSpec((tm, D), lambda i: (i, 0)),
                  pl.BlockSpec((tm, LANES), lambda i: (i, 0)),
                  pl.BlockSpec((None, 1, D), lambda i: (i // tiles_per_batch, 0, 0)),
                  pl.BlockSpec((1, D), lambda i: (0, 0)),
                  pl.BlockSpec((1, D), lambda i: (0, 0))],
        out_specs=pl.BlockSpec((tm, D), lambda i: (i, 0)),
        out_shape=jax.ShapeDtypeStruct((T, D), F32),
        scratch_shapes=[pltpu.VMEM((TOP_K_EXPERTS, tm, D), F32), pltpu.SemaphoreType.DMA(())],
        compiler_params=_cparams(("arbitrary",)),
        name="moe_combine",
    )(pos, ys, x1, route, gt2, row(ln_g), row(ln_b))


def _moe_and_norm(x1, u2, route, counts, gt2, w_gu, b_gu, w_dn, b_dn, ln_g, ln_b):
    B, S, D = x1.shape
    T = B * S
    bm = BM_EXPERT
    n_blocks = -(-T * TOP_K_EXPERTS // bm) + N_EXPERTS
    cnt = counts[0, :N_EXPERTS].astype(I32)
    padded = (cnt + bm - 1) // bm * bm
    pad_ends = jnp.cumsum(padded)
    pad_starts = pad_ends - padded
    route2 = route.reshape(T, LANES)
    e_idx = route2[:, :TOP_K_EXPERTS].astype(I32)
    rank = route2[:, 2 * TOP_K_EXPERTS:3 * TOP_K_EXPERTS].astype(I32)
    pos = (pad_starts[e_idx] + rank).reshape(-1)
    blk_row = jnp.arange(n_blocks, dtype=I32) * bm
    block_e = jnp.minimum(jnp.sum((blk_row[:, None] >= pad_ends[None, :]).astype(I32), axis=1), N_EXPERTS - 1)
    n_used = (pad_ends[-1:] // bm).astype(I32)
    E, F2 = b_gu.shape
    b_gu_p = b_gu.reshape(E, F2 // GU_GROUP, LANES, 2).transpose(0, 1, 3, 2).reshape(E, 1, F2)
    xs = _moe_dispatch(u2.reshape(T, D), pos, n_blocks * bm)
    ys = _moe_experts(xs, block_e, n_used, w_gu, b_gu_p, w_dn, b_dn[:, None, :])
    out = _moe_combine(ys, pos, x1.reshape(T, D), route2, gt2, ln_g, ln_b, S // min(TM_ROUTE, T))
    return out.reshape(B, S, D)


def kernel(x, c, w_ada, b_ada, w_in, shift_mu, rwkv_w0, rwkv_w2, rwkv_a0, rwkv_a2, rwkv_g2, rwkv_k_k, rwkv_k_a, rwkv_r_k, rwkv_ln_w, rwkv_ln_b, mla_q_norm, mla_w_q_up, mla_kv_norm, mla_w_uk, mla_w_uv, idx_w_q, idx_ln_g, idx_ln_b, w_out, ln1_g, ln1_b, w_router, b_router, w_gu, b_gu, w_dn, b_dn, ln2_g, ln2_b):
    depth = w_ada.shape[0]
    assert depth == 1, "DeepNorm constants below are for a single layer"
    l = 0
    mod = _ada_mod(c, w_ada[l], b_ada[l])
    sh1, sc1, gt1, sh2, sc2, gt2 = [m[:, None, :] for m in jnp.split(mod, 6, axis=-1)]
    r, lw, k, v, kk, a, g, qabs, iq, ik, iw, cl = _in_proj(
        x, sc1, sh1, w_in[l], shift_mu[l], rwkv_w0[l], rwkv_w2[l], rwkv_a0[l], rwkv_a2[l], rwkv_g2[l], rwkv_k_k[l],
        rwkv_k_a[l], mla_q_norm[l], mla_w_q_up[l], mla_w_uk[l], mla_kv_norm[l], idx_w_q[l], idx_ln_g[l], idx_ln_b[l])
    o_rwkv = _rwkv_scan(r, lw, k, v, kk, a, g, rwkv_r_k[l], rwkv_ln_w[l], rwkv_ln_b[l])
    o_dsa = _dsa_attn(iq, iw, qabs, ik, cl, mla_w_uv[l])
    x1, u2, route, counts = _mix_out(o_rwkv, o_dsa, x, gt1, sc2, sh2, w_out[l], ln1_g[l], ln1_b[l], w_router[l],
                                     b_router[l])
    return _moe_and_norm(x1, u2, route, counts, gt2, w_gu[l], b_gu[l], w_dn[l], b_dn[l], ln2_g[l], ln2_b[l])
```

```python
import functools
import math

import jax
import jax.numpy as jnp
import numpy as np
from jax import lax
from jax.experimental import pallas as pl
from jax.experimental.pallas import tpu as pltpu

F32 = jnp.float32
BF16 = jnp.bfloat16
I32 = jnp.int32

RWKV_HEAD = 64
N_RWKV_HEADS = 8
D_RWKV = RWKV_HEAD * N_RWKV_HEADS
RANK_DECAY = 64
RANK_A = 64
RANK_GATE = 128
RWKV_GN_EPS = 64e-5
ATT_HEAD = 64
N_ATT_HEADS = 8
D_ATT = ATT_HEAD * N_ATT_HEADS
Q_LORA = 256
KV_LORA = 128
IDX_HEADS = 8
IDX_DIM = 64
TOPK_MAX = 256
N_EXPERTS = 32
TOP_K_EXPERTS = 4
SWIGLU_LIMIT = 7.0
SWIGLU_ALPHA = 1.702
NEG_BIG = -1e30
LOG2E = 1.4426950408889634
INT_MIN = -(2 ** 31)

LANES = 128
SUBLANES = 8
VMEM_LIMIT = 48 * 1024 * 1024
VMEM_LIMIT_EXPERTS = 58 * 1024 * 1024

TM_PROJ = 256
L_CHUNK = 64
TQ = 128
TM_ROUTE = 256
BM_EXPERT = 256

_SEG = (("r", 512, 512), ("k", 512, 512), ("v", 512, 512), ("w", 64, 128), ("a", 64, 128), ("g", 128, 128),
        ("q", 256, 256), ("kv", 128, 128), ("ik", 64, 128), ("iw", 8, 128))
N_SHIFT_P = 512 * 3 + 128 * 3
N_IN_P = sum(s[2] for s in _SEG)


def _cparams(sem):
    return pltpu.CompilerParams(dimension_semantics=sem, vmem_limit_bytes=VMEM_LIMIT)


def _bdot(a, b):
    return jnp.dot(a.astype(BF16), b.astype(BF16), preferred_element_type=F32)


def _bdot_nt(a, b):
    return lax.dot_general(a.astype(BF16), b.astype(BF16), (((1,), (1,)), ((), ())), preferred_element_type=F32)


def _bdot_tn(a, b):
    return lax.dot_general(a.astype(BF16), b.astype(BF16), (((0,), (0,)), ((), ())), preferred_element_type=F32)


def _split2(a):
    hi = a.astype(BF16)
    lo = (a - hi.astype(F32)).astype(BF16)
    return hi, lo


def _split3(a):
    hi = a.astype(BF16)
    r1 = a - hi.astype(F32)
    mid = r1.astype(BF16)
    lo = (r1 - mid.astype(F32)).astype(BF16)
    return hi, mid, lo


def _dot3(a, b, dims=(((1,), (0,)), ((), ()))):
    ah, al = _split2(a)
    bh, bl = _split2(b)
    d = functools.partial(lax.dot_general, dimension_numbers=dims, preferred_element_type=F32)
    return d(ah, bh) + (d(ah, bl) + d(al, bh))


def _dot_exact_rhs(a, b_exact, nsplit=3):
    parts = _split3(a) if nsplit == 3 else _split2(a)
    acc = None
    for p in parts[::-1]:
        t = jnp.dot(p, b_exact, preferred_element_type=F32)
        acc = t if acc is None else acc + t
    return acc


def _dot_exact_lhs(a_exact, b, nsplit=3):
    parts = _split3(b) if nsplit == 3 else _split2(b)
    acc = None
    for p in parts[::-1]:
        t = jnp.dot(a_exact, p, preferred_element_type=F32)
        acc = t if acc is None else acc + t
    return acc


def _sigmoid(x):
    return 1.0 / (1.0 + jnp.exp(-x))


def _softplus(x):
    return jnp.maximum(x, 0.0) + jnp.log(1.0 + jnp.exp(-jnp.abs(x)))


def _ada_kernel(c_ref, w_ref, b_ref, o_ref):
    c = c_ref[...]
    o_ref[...] = _dot3(c * _sigmoid(c), w_ref[...]) + b_ref[...]


def _ada_mod(c, w_ada, b_ada):
    B, D = c.shape
    N = w_ada.shape[1]
    tn = 1024
    return pl.pallas_call(
        _ada_kernel,
        grid=(N // tn,),
        in_specs=[pl.BlockSpec((B, D), lambda j: (0, 0)),
                  pl.BlockSpec((D, tn), lambda j: (0, j)),
                  pl.BlockSpec((1, tn), lambda j: (0, j))],
        out_specs=pl.BlockSpec((B, tn), lambda j: (0, j)),
        out_shape=jax.ShapeDtypeStruct((B, N), F32),
        compiler_params=_cparams(("arbitrary",)),
        name="ada_mod",
    )(c, w_ada, b_ada.reshape(1, N))


def _in_proj_kernel(x_ref, sc_ref, sh_ref, win_ref, mu_ref, w0_ref, w2_ref, a0_ref, a2_ref, g2_ref, kk_ref, ka_ref,
                    ones_ref, qn_ref, wq_ref, wuk_ref, kvn_ref, wiq_ref, ig_ref, ib_ref,
                    r_o, lw_o, k_o, v_o, kkn_o, a_o, g_o, qabs_o, iq_o, ik_o, iw_o, cl_o, carry):
    i = pl.program_id(1)
    tm = x_ref.shape[0]

    @pl.when(i == 0)
    def _():
        carry[...] = jnp.zeros_like(carry)

    u = x_ref[...] * (1.0 + sc_ref[...]) + sh_ref[...]
    p = _bdot(u, win_ref[...])
    ps = p[:, :N_SHIFT_P]
    rows = lax.broadcasted_iota(I32, (tm, 1), 0)
    prev = jnp.where(rows == 0, carry[0:1, :], pltpu.roll(ps, 1, 0))
    carry[0:1, :] = ps[tm - 1:tm, :]
    ps = ps + mu_ref[...] * (prev - ps)

    pr, pk, pv = ps[:, 0:512], ps[:, 512:1024], ps[:, 1024:1536]
    pw, pa, pg = ps[:, 1536:1664], ps[:, 1664:1792], ps[:, 1792:1920]
    w_log = -_softplus(-(w0_ref[...] + _dot3(jnp.tanh(pw), w2_ref[...]))) - 0.5
    lw_o[...] = -jnp.exp(w_log)
    a = _sigmoid(a0_ref[...] + _dot3(pa, a2_ref[...]))
    g_o[...] = _dot3(_sigmoid(pg), g2_ref[...])
    kk = pk * kk_ref[...]
    ssq = _dot_exact_rhs(kk * kk, ones_ref[...], nsplit=2)
    kkn_o[...] = kk / jnp.maximum(jnp.sqrt(ssq), 1e-12)
    k_o[...] = pk * (1.0 + (a - 1.0) * ka_ref[...])
    r_o[...] = pr
    v_o[...] = pv
    a_o[...] = a

    pq, pkv = p[:, 1920:2176], p[:, 2176:2304]
    pik, piw = p[:, 2304:2432], p[:, 2432:2560]
    q_lat = pq * lax.rsqrt(jnp.mean(pq * pq, axis=-1, keepdims=True) + 1e-6) * qn_ref[...]
    q = _bdot(q_lat, wq_ref[...])
    qabs_o[...] = (_bdot(q, wuk_ref[...]) * (ATT_HEAD ** -0.5 * LOG2E)).astype(BF16)
    c_lat = pkv * lax.rsqrt(jnp.mean(pkv * pkv, axis=-1, keepdims=True) + 1e-6) * kvn_ref[...]
    spos = i * tm + rows
    s_hi = (spos >> 7).astype(F32)
    s_lo = (spos & (LANES - 1)).astype(F32)
    lane_t = lax.broadcasted_iota(I32, (1, LANES), 1)
    extra = jnp.where(lane_t == 0, 1.0, jnp.where((lane_t == 1) | (lane_t == 2), s_hi,
                                                  jnp.where((lane_t == 3) | (lane_t == 4), s_lo, 0.0)))
    cl_o[...] = jnp.concatenate([c_lat, extra], axis=-1).astype(BF16)
    iq_o[...] = (_bdot(q_lat, wiq_ref[...]) * (IDX_DIM ** -0.5)).astype(BF16)
    lane = lax.broadcasted_iota(I32, (1, LANES), 1)
    valid = lane < IDX_DIM
    mu = jnp.sum(pik, axis=-1, keepdims=True) * (1.0 / IDX_DIM)
    dlt = jnp.where(valid, pik - mu, 0.0)
    var = jnp.sum(dlt * dlt, axis=-1, keepdims=True) * (1.0 / IDX_DIM)
    ik_o[...] = jnp.where(valid, dlt * lax.rsqrt(var + 1e-5) * ig_ref[...] + ib_ref[...], 0.0).astype(BF16)
    iw_o[...] = piw * (IDX_HEADS ** -0.5)


def _pad_cols(w, widths):
    parts, o = [], 0
    for true, padded in widths:
        seg = w[..., o:o + true]
        if padded > true:
            seg = jnp.pad(seg, [(0, 0)] * (w.ndim - 1) + [(0, padded - true)])
        parts.append(seg)
        o += true
    return jnp.concatenate(parts, axis=-1)


def _pad_rows(w, rows):
    return jnp.pad(w, ((0, rows - w.shape[0]), (0, 0)))


def _block_diag(blocks):
    H, a, b = blocks.shape
    eye = jnp.eye(H, dtype=blocks.dtype)
    return (eye[:, None, :, None] * blocks[:, :, None, :]).reshape(H * a, H * b)


def _head_ones(n, head):
    idx = np.arange(n) // head
    return jnp.asarray(idx[:, None] == idx[None, :], BF16)


def _in_proj(x, sc1, sh1, w_in, shift_mu, w0, w2, a0, a2, g2, k_k, k_a, q_norm, w_q_up, w_uk, kv_norm, idx_w_q,
             idx_ln_g, idx_ln_b):
    B, S, D = x.shape
    tm = min(TM_PROJ, S)
    widths = tuple((s[1], s[2]) for s in _SEG)
    win_p = _pad_cols(w_in, widths).astype(BF16)
    mu_p = _pad_cols(shift_mu.reshape(1, -1), widths[:6])
    w2_p = _pad_rows(w2, LANES)
    a2_p = _pad_rows(a2, LANES)
    wuk_bd = _block_diag(w_uk).astype(BF16)
    wiq_p = _pad_cols(idx_w_q, ((IDX_DIM, LANES),) * IDX_HEADS).astype(BF16)
    ig_p = _pad_cols(idx_ln_g.reshape(1, -1), ((IDX_DIM, LANES),))
    ib_p = _pad_cols(idx_ln_b.reshape(1, -1), ((IDX_DIM, LANES),))
    row = lambda v: v.reshape(1, -1)
    tok = lambda n: pl.BlockSpec((None, tm, n), lambda b, i: (b, i, 0))
    mod = pl.BlockSpec((None, 1, D), lambda b, i: (b, 0, 0))
    full = lambda a: pl.BlockSpec(a.shape, lambda b, i: (0,) * a.ndim)
    consts = [win_p, mu_p, row(w0), w2_p, row(a0), a2_p, g2, row(k_k), row(k_a), _head_ones(D_RWKV, RWKV_HEAD),
              row(q_norm), w_q_up.astype(BF16), wuk_bd, row(kv_norm), wiq_p, ig_p, ib_p]
    outs = [(D_RWKV, F32)] * 7 + [(N_ATT_HEADS * KV_LORA, BF16), (IDX_HEADS * LANES, BF16), (LANES, BF16),
                                  (LANES, F32), (KV_LORA + LANES, BF16)]
    return pl.pallas_call(
        _in_proj_kernel,
        grid=(B, S // tm),
        in_specs=[tok(D), mod, mod] + [full(a) for a in consts],
        out_specs=[tok(n) for n, _ in outs],
        out_shape=[jax.ShapeDtypeStruct((B, S, n), dt) for n, dt in outs],
        scratch_shapes=[pltpu.VMEM((8, N_SHIFT_P), F32)],
        compiler_params=_cparams(("arbitrary", "arbitrary")),
        name="in_proj",
    )(x, sc1, sh1, *consts)


def _rwkv_kernel(r_ref, lw_ref, k_ref, v_ref, kk_ref, a_ref, g_ref, rk_ref, lnw_ref, lnb_ref, tri_ref, ones_ref,
                 o_ref, state):
    c = pl.program_id(1)
    L = r_ref.shape[0]

    @pl.when(c == 0)
    def _():
        state[...] = jnp.zeros_like(state)

    r, lw, k, v, kk, a = r_ref[...], lw_ref[...], k_ref[...], v_ref[...], kk_ref[...], a_ref[...]
    cum = _dot_exact_lhs(tri_ref[...], lw)
    cum_last = cum[L - 1:L, :]
    w_incl = jnp.exp(cum)
    w_inv = jnp.exp(-cum)
    w_rel = jnp.exp(cum_last - cum)
    w_last = jnp.exp(cum_last)
    bvec = kk * a
    at = -kk * jnp.exp(cum - lw)
    rt = r * w_incl
    bt = bvec * w_inv
    kt = k * w_inv
    bh = bvec * w_rel
    kh = k * w_rel
    ti = lax.broadcasted_iota(I32, (L, L), 0)
    tj = lax.broadcasted_iota(I32, (L, L), 1)
    strict = tj < ti
    incl = tj <= ti
    eye = lax.broadcasted_iota(I32, (RWKV_HEAD, RWKV_HEAD), 0) == lax.broadcasted_iota(I32, (RWKV_HEAD, RWKV_HEAD), 1)
    NT = (((1,), (1,)), ((), ()))
    TN = (((0,), (0,)), ((), ()))
    heads = range(N_RWKV_HEADS)
    sls = [slice(h * RWKV_HEAD, (h + 1) * RWKV_HEAD) for h in heads]
    mm = lambda x, y, dims=(((1,), (0,)), ((), ())): lax.dot_general(
        x.astype(BF16), y.astype(BF16), dims, preferred_element_type=F32)
    at_b, rt_b, bt_b, kt_b, bh_b, kh_b, v_b = [z.astype(BF16) for z in (at, rt, bt, kt, bh, kh, v)]
    vh = [v_b[:, s] for s in sls]
    ath = [at_b[:, s] for s in sls]
    ar = [jnp.concatenate([at_b[:, s], rt_b[:, s]], axis=0) for s in sls]
    g_b = [mm(ar[h], bt_b[:, sls[h]], NT) for h in heads]
    g_k = [mm(ar[h], kt_b[:, sls[h]], NT) for h in heads]
    n_ab = [jnp.where(strict, g_b[h][:L], 0.0) for h in heads]
    a_ak = [jnp.where(strict, g_k[h][:L], 0.0) for h in heads]
    a_rb = [jnp.where(incl, g_b[h][L:], 0.0).astype(BF16) for h in heads]
    a_rk = [jnp.where(incl, g_k[h][L:], 0.0) for h in heads]
    akv = [mm(a_ak[h], vh[h]) for h in heads]
    eye_l = jnp.where(ti == tj, 1.0, 0.0)
    tinv = [eye_l + n_ab[h] for h in heads]
    pw = n_ab
    for _ in range(int(math.log2(L)) - 1):
        pw = [mm(pw[h], pw[h]) for h in heads]
        tinv = [tinv[h] + mm(pw[h], tinv[h]) for h in heads]
    tinv = [t.astype(BF16) for t in tinv]
    a_t = [mm(tinv[h], ath[h]).astype(BF16) for h in heads]
    y = [mm(tinv[h], akv[h]).astype(BF16) for h in heads]
    m_c = [jnp.where(eye, w_last[:, sls[h]], 0.0) + mm(a_t[h], bh_b[:, sls[h]], TN) for h in heads]
    c_c = [mm(y[h], bh_b[:, sls[h]], TN) + mm(vh[h], kh_b[:, sls[h]], TN) for h in heads]
    q_c = [rt[:, sls[h]] + mm(a_rb[h], a_t[h]) for h in heads]
    o_loc = [mm(a_rb[h], y[h]) + mm(a_rk[h], vh[h]) for h in heads]
    s0 = [state[h] for h in heads]
    o = [o_loc[h] + _dot3(q_c[h], s0[h], NT) for h in heads]
    for h in heads:
        state[h] = _dot3(s0[h], m_c[h]) + c_c[h]
    for h in heads:
        mu = jnp.mean(o[h], axis=-1, keepdims=True)
        d = o[h] - mu
        var = jnp.mean(d * d, axis=-1, keepdims=True)
        o_ref[:, sls[h]] = d * lax.rsqrt(var + RWKV_GN_EPS)
    bonus = _dot_exact_rhs(r * k * rk_ref[...], ones_ref[...], nsplit=3) * v
    o_ref[...] = (o_ref[...] * lnw_ref[...] + lnb_ref[...] + bonus) * g_ref[...]


def _rwkv_scan(r, lw, k, v, kk, a, g, r_k, ln_w, ln_b):
    B, S, DR = r.shape
    L = min(L_CHUNK, S)
    tri = jnp.asarray(np.tril(np.ones((L, L))), BF16)
    row = lambda z: z.reshape(1, -1)
    tok = pl.BlockSpec((None, L, DR), lambda b, c: (b, c, 0))
    full = lambda z: pl.BlockSpec(z.shape, lambda b, c: (0,) * z.ndim)
    consts = [row(r_k), row(ln_w), row(ln_b), tri, _head_ones(DR, RWKV_HEAD)]
    return pl.pallas_call(
        _rwkv_kernel,
        grid=(B, S // L),
        in_specs=[tok] * 7 + [full(z) for z in consts],
        out_specs=tok,
        out_shape=jax.ShapeDtypeStruct((B, S, DR), F32),
        scratch_shapes=[pltpu.VMEM((N_RWKV_HEADS, RWKV_HEAD, RWKV_HEAD), F32)],
        compiler_params=_cparams(("arbitrary", "arbitrary")),
        name="rwkv_scan",
    )(r, lw, k, v, kk, a, g, *consts)


def _alibi_cols():
    slope = np.asarray([2.0 ** (-8.0 * (h + 1) / N_ATT_HEADS) * LOG2E for h in range(N_ATT_HEADS)], np.float32)
    c_hi = slope.astype(BF16).astype(np.float32)
    c_lo = (slope - c_hi).astype(BF16).astype(np.float32)
    t = np.zeros((N_ATT_HEADS, LANES), np.float32)
    t[:, 1], t[:, 2], t[:, 3], t[:, 4] = LANES * c_hi, LANES * c_lo, c_hi, c_lo
    return jnp.asarray(t.astype(BF16))


def _dsa_kernel(iq_ref, iw_ref, qa_ref, ik_ref, ca_ref, wuv_ref, triu_ref, acol_ref, o_ref, key_ref, bias_ref, *,
                topk, q_off, select):
    qi = pl.program_id(1) + q_off
    tq = qa_ref.shape[0]
    sk = ca_ref.shape[0]
    ca = ca_ref[...]
    tpos = qi * tq + lax.broadcasted_iota(I32, (tq, 1), 0)
    spos = lax.broadcasted_iota(I32, (1, sk), 1)
    causal = spos <= tpos

    if not select:
        bias_ref[...] = jnp.where(causal, 0.0, NEG_BIG)
    else:
        ik = ik_ref[...]
        iw = iw_ref[...]
        score = jnp.zeros((tq, sk), F32)
        for h in range(IDX_HEADS):
            s = lax.dot_general(iq_ref[:, h * LANES:(h + 1) * LANES], ik, (((1,), (1,)), ((), ())),
                                preferred_element_type=F32)
            score = score + iw[:, h:h + 1] * jnp.maximum(s, 0.0)
        bits = pltpu.bitcast(score + 0.0, I32)
        key = bits ^ ((bits >> 31) & 0x7FFFFFFF)
        key_ref[...] = jnp.where(causal, key, INT_MIN)
        kcount = jnp.minimum(topk, tpos + 1).astype(F32)

        def count_ge(cand):
            return jnp.sum(jnp.where(key_ref[...] >= cand, 1.0, 0.0), axis=-1, keepdims=True)

        thr = jnp.where(count_ge(jnp.zeros((tq, 1), I32)) >= kcount, 0, INT_MIN).astype(I32)

        def bit_step(i, thr):
            cand = thr | (1 << (30 - i))
            return jnp.where(count_ge(cand) >= kcount, cand, thr)

        thr = lax.fori_loop(0, 31, bit_step, thr)

        key = key_ref[...]
        gt = key > thr
        eq = key == thr
        need = kcount - jnp.sum(jnp.where(gt, 1.0, 0.0), axis=-1, keepdims=True)
        eqf = jnp.where(eq, 1.0, 0.0)
        carry = jnp.zeros((tq, 1), F32)
        for j in range(sk // LANES):
            sl = slice(j * LANES, (j + 1) * LANES)
            e = eqf[:, sl]
            before = jnp.dot(e.astype(BF16), triu_ref[...], preferred_element_type=F32) + carry
            sel = gt[:, sl] | (eq[:, sl] & (before < need))
            bias_ref[:, sl] = jnp.where(sel, 0.0, NEG_BIG)
            carry = carry + jnp.sum(e, axis=-1, keepdims=True)

    outs = []
    for h in range(N_ATT_HEADS):
        q_aug = jnp.concatenate([qa_ref[:, h * KV_LORA:(h + 1) * KV_LORA],
                                 jnp.broadcast_to(acol_ref[h:h + 1, :], (tq, LANES))], axis=-1)
        lg = lax.dot_general(q_aug, ca, (((1,), (1,)), ((), ())), preferred_element_type=F32) + bias_ref[...]
        m = jnp.max(lg, axis=-1, keepdims=True)
        p = jnp.exp2(lg - m)
        pv = jnp.dot(p.astype(BF16), ca, preferred_element_type=F32)
        outs.append(pv[:, :KV_LORA] / pv[:, KV_LORA:KV_LORA + 1])
    o_lat = jnp.concatenate(outs, axis=-1)
    o_ref[...] = _bdot(o_lat, wuv_ref[...])


def _dsa_attn(iq, iw, qabs, ik, ca, w_uv):
    B, S, _ = iq.shape
    tq = min(TQ, S)
    topk = min(TOPK_MAX, S // 4)
    wuv_bd = _block_diag(w_uv).astype(BF16)
    triu = jnp.asarray(np.triu(np.ones((LANES, LANES)), 1), BF16)
    acol = _alibi_cols()
    full = lambda z: pl.BlockSpec(z.shape, lambda b, i: (0,) * z.ndim)
    nq = 2 if S // tq >= 2 else 1
    outs = []
    for q_off in range(0, S // tq, nq):
        sk = (q_off + nq) * tq
        tok = lambda n, q_off=q_off: pl.BlockSpec((None, tq, n), lambda b, i: (b, i + q_off, 0))
        seq = lambda n, sk=sk: pl.BlockSpec((None, sk, n), lambda b, i: (b, 0, 0))
        outs.append(pl.pallas_call(
            functools.partial(_dsa_kernel, topk=topk, q_off=q_off, select=sk > topk),
            grid=(B, nq),
            in_specs=[tok(IDX_HEADS * LANES), tok(LANES), tok(N_ATT_HEADS * KV_LORA), seq(LANES), seq(KV_LORA + LANES),
                      full(wuv_bd), full(triu), full(acol)],
            out_specs=pl.BlockSpec((None, tq, D_ATT), lambda b, i: (b, i, 0)),
            out_shape=jax.ShapeDtypeStruct((B, nq * tq, D_ATT), F32),
            scratch_shapes=[pltpu.VMEM((tq, sk), I32), pltpu.VMEM((tq, sk), F32)],
            compiler_params=_cparams(("arbitrary", "arbitrary")),
            name=f"dsa_attn_k{sk}",
        )(iq, iw, qabs, ik, ca, wuv_bd, triu, acol))
    return jnp.concatenate(outs, axis=1)


def _layernorm_rows(y, g, b):
    mu = jnp.mean(y, axis=-1, keepdims=True)
    d = y - mu
    var = jnp.mean(d * d, axis=-1, keepdims=True)
    return d * lax.rsqrt(var + 1e-5) * g + b


def _mix_kernel(orw_ref, ods_ref, x_ref, gt_ref, sc_ref, sh_ref, wtop_ref, wbot_ref, g_ref, b_ref, wr_ref, br_ref,
                tril_ref, x1_o, u2_o, route_o, cnt_o, carry, *, alpha):
    first = (pl.program_id(0) == 0) & (pl.program_id(1) == 0)
    tm = x_ref.shape[0]

    @pl.when(first)
    def _():
        carry[...] = jnp.zeros_like(carry)

    mix = _bdot(orw_ref[...], wtop_ref[...]) + _bdot(ods_ref[...], wbot_ref[...])
    x1 = _layernorm_rows(alpha * x_ref[...] + (1.0 + gt_ref[...]) * mix, g_ref[...], b_ref[...])
    x1_o[...] = x1
    u2 = x1 * (1.0 + sc_ref[...]) + sh_ref[...]
    u2_o[...] = u2

    lg = _dot3(u2, wr_ref[...]) + br_ref[...]
    lane = lax.broadcasted_iota(I32, (tm, LANES), 1)
    idxs, vals = [], []
    for _ in range(TOP_K_EXPERTS):
        m = jnp.max(lg, axis=-1, keepdims=True)
        idx = jnp.min(jnp.where(lg == m, lane, LANES), axis=-1, keepdims=True)
        idxs.append(idx)
        vals.append(m)
        lg = jnp.where(lane == idx, -jnp.inf, lg)
    es = [jnp.exp(v - vals[0]) for v in vals]
    den = es[0] + es[1] + es[2] + es[3]
    hot = jnp.zeros((tm, LANES), F32)
    for idx in idxs:
        hot = hot + jnp.where(lane == idx, 1.0, 0.0)
    before = jnp.dot(tril_ref[...], hot.astype(BF16), preferred_element_type=F32) + carry[0:1, :]
    route = jnp.zeros((tm, LANES), F32)
    for k in range(TOP_K_EXPERTS):
        rank = jnp.sum(jnp.where(lane == idxs[k], before, 0.0), axis=-1, keepdims=True)
        route = jnp.where(lane == k, idxs[k].astype(F32), route)
        route = jnp.where(lane == TOP_K_EXPERTS + k, es[k] / den, route)
        route = jnp.where(lane == 2 * TOP_K_EXPERTS + k, rank, route)
    route_o[...] = route
    carry[0:1, :] = carry[0:1, :] + jnp.sum(hot, axis=0, keepdims=True)
    cnt_o[...] = carry[...]


def _mix_out(o_rwkv, o_dsa, x, gt1, sc2, sh2, w_out, ln_g, ln_b, w_router, b_router):
    B, S, D = x.shape
    tm = min(TM_PROJ, S)
    alpha = 2.0 ** 0.25
    wtop = w_out[:D_RWKV].astype(BF16)
    wbot = w_out[D_RWKV:].astype(BF16)
    wr_p = jnp.pad(w_router, ((0, 0), (0, LANES - N_EXPERTS)))
    br_p = jnp.pad(b_router.reshape(1, -1), ((0, 0), (0, LANES - N_EXPERTS)), constant_values=NEG_BIG)
    tril = jnp.asarray(np.tril(np.ones((tm, tm)), -1), BF16)
    row = lambda v: v.reshape(1, -1)
    tok = lambda n: pl.BlockSpec((None, tm, n), lambda b, i: (b, i, 0))
    mod = pl.BlockSpec((None, 1, D), lambda b, i: (b, 0, 0))
    full = lambda a: pl.BlockSpec(a.shape, lambda b, i: (0,) * a.ndim)
    consts = [wtop, wbot, row(ln_g), row(ln_b), wr_p, br_p, tril]
    return pl.pallas_call(
        functools.partial(_mix_kernel, alpha=alpha),
        grid=(B, S // tm),
        in_specs=[tok(D_RWKV), tok(D_ATT), tok(D), mod, mod, mod] + [full(a) for a in consts],
        out_specs=[tok(D), tok(D), tok(LANES), pl.BlockSpec((8, LANES), lambda b, i: (0, 0))],
        out_shape=[jax.ShapeDtypeStruct((B, S, D), F32), jax.ShapeDtypeStruct((B, S, D), F32),
                   jax.ShapeDtypeStruct((B, S, LANES), F32), jax.ShapeDtypeStruct((8, LANES), F32)],
        scratch_shapes=[pltpu.VMEM((8, LANES), F32)],
        compiler_params=_cparams(("arbitrary", "arbitrary")),
        name="mix_out",
    )(o_rwkv, o_dsa, x, gt1, sc2, sh2, *consts)


ISSUE_UNROLL = 8


def _dispatch_kernel(pos_ref, u_ref, xs_in, xs_out, sem):
    del xs_in
    tm = u_ref.shape[0]

    def start(r, carry):
        for k in range(TOP_K_EXPERTS):
            dst = pos_ref[r * TOP_K_EXPERTS + k]
            pltpu.make_async_copy(u_ref.at[pl.ds(r, 1)], xs_out.at[pl.ds(dst, 1)], sem).start()
        return carry

    lax.fori_loop(0, tm, start, 0, unroll=ISSUE_UNROLL)
    for k in range(TOP_K_EXPERTS):
        pltpu.make_async_copy(u_ref, xs_out.at[pl.ds(0, tm)], sem).wait()


def _moe_dispatch(u2, pos, n_rows):
    T, D = u2.shape
    tm = min(TM_ROUTE, T)
    return pl.pallas_call(
        _dispatch_kernel,
        grid=(T // tm,),
        in_specs=[pl.BlockSpec((tm * TOP_K_EXPERTS,), lambda i: (i,), memory_space=pltpu.SMEM),
                  pl.BlockSpec((tm, D), lambda i: (i, 0)),
                  pl.BlockSpec(memory_space=pl.ANY)],
        out_specs=pl.BlockSpec(memory_space=pl.ANY),
        out_shape=jax.ShapeDtypeStruct((n_rows, D), F32),
        scratch_shapes=[pltpu.SemaphoreType.DMA(())],
        input_output_aliases={2: 0},
        compiler_params=_cparams(("arbitrary",)),
        name="moe_dispatch",
    )(pos, u2, jnp.zeros((n_rows, D), F32))


GU_GROUP = 2 * LANES


def _deinterleave_perm():
    p = np.zeros((GU_GROUP, GU_GROUP), np.float32)
    l = np.arange(LANES)
    p[2 * l, l] = 1.0
    p[2 * l + 1, LANES + l] = 1.0
    return jnp.asarray(p, BF16)


def _expert_kernel(be_ref, nb_ref, xs_ref, wgu_ref, bgu_ref, wd_ref, bd_ref, perm_ref, ys_ref, wp, wdb):
    i = pl.program_id(0)
    used = i < nb_ref[0]
    new_expert = (i == 0) | (be_ref[i] != be_ref[jnp.maximum(i - 1, 0)])
    n_groups = wgu_ref.shape[1] // GU_GROUP

    @pl.when(used & new_expert)
    def _():
        for j in range(n_groups):
            sl = slice(j * GU_GROUP, (j + 1) * GU_GROUP)
            wp[:, sl] = jnp.dot(wgu_ref[:, sl].astype(BF16), perm_ref[...], preferred_element_type=F32).astype(BF16)
        wdb[...] = wd_ref[...].astype(BF16)

    @pl.when(used)
    def _():
        xb = xs_ref[...].astype(BF16)
        gu = jnp.dot(xb, wp[...], preferred_element_type=F32) + bgu_ref[...]
        hs = []
        for j in range(n_groups):
            gate = jnp.minimum(gu[:, j * GU_GROUP:j * GU_GROUP + LANES], SWIGLU_LIMIT)
            up = jnp.clip(gu[:, j * GU_GROUP + LANES:(j + 1) * GU_GROUP], -SWIGLU_LIMIT, SWIGLU_LIMIT)
            hs.append(((up + 1.0) * (gate * _sigmoid(gate * SWIGLU_ALPHA))).astype(BF16))
        h = jnp.concatenate(hs, axis=-1)
        ys_ref[...] = jnp.dot(h, wdb[...], preferred_element_type=F32) + bd_ref[...]

    @pl.when(jnp.logical_not(used))
    def _():
        ys_ref[...] = jnp.zeros_like(ys_ref)


def _moe_experts(xs, block_e, n_used, w_gu, b_gu_p, w_dn, b_dn):
    n_rows, D = xs.shape
    E, _, F2 = w_gu.shape
    bm = BM_EXPERT
    n_blocks = n_rows // bm
    perm = _deinterleave_perm()
    wspec = lambda shp: pl.BlockSpec((None,) + shp, lambda i, be, nb: (be[i], 0, 0))
    return pl.pallas_call(
        _expert_kernel,
        grid_spec=pltpu.PrefetchScalarGridSpec(
            num_scalar_prefetch=2,
            grid=(n_blocks,),
            in_specs=[pl.BlockSpec((bm, D), lambda i, be, nb: (jnp.minimum(i, nb[0] - 1), 0)),
                      wspec((D, F2)), wspec((1, F2)), wspec((F2 // 2, D)), wspec((1, D)),
                      pl.BlockSpec(perm.shape, lambda i, be, nb: (0, 0))],
            out_specs=pl.BlockSpec((bm, D), lambda i, be, nb: (i, 0)),
            scratch_shapes=[pltpu.VMEM((D, F2), BF16), pltpu.VMEM((F2 // 2, D), BF16)],
        ),
        out_shape=jax.ShapeDtypeStruct((n_rows, D), F32),
        compiler_params=pltpu.CompilerParams(dimension_semantics=("arbitrary",), vmem_limit_bytes=VMEM_LIMIT_EXPERTS),
        name="moe_experts",
    )(block_e, n_used, xs, w_gu, b_gu_p, w_dn, b_dn, perm)


def _combine_kernel(pos_ref, posn_ref, ys_ref, x1_ref, route_ref, gt_ref, g_ref, b_ref, o_ref, buf, sem, *, alpha):
    i = pl.program_id(0)
    n = pl.num_programs(0)
    tm = x1_ref.shape[0]
    slot = i % 2

    def gather(p_ref, s):
        def start(r, carry):
            for k in range(TOP_K_EXPERTS):
                src = p_ref[r * TOP_K_EXPERTS + k]
                pltpu.make_async_copy(ys_ref.at[pl.ds(src, 1)], buf.at[s, k, pl.ds(r, 1)], sem.at[s]).start()
            return carry
        lax.fori_loop(0, tm, start, 0, unroll=ISSUE_UNROLL)

    @pl.when(i == 0)
    def _():
        gather(pos_ref, 0)

    @pl.when(i + 1 < n)
    def _():
        gather(posn_ref, 1 - slot)

    for k in range(TOP_K_EXPERTS):
        pltpu.make_async_copy(ys_ref.at[pl.ds(0, tm)], buf.at[slot, k], sem.at[slot]).wait()

    route = route_ref[...]
    ffn = jnp.zeros(x1_ref.shape, F32)
    for k in range(TOP_K_EXPERTS):
        ffn = ffn + route[:, TOP_K_EXPERTS + k:TOP_K_EXPERTS + k + 1] * buf[slot, k]
    o_ref[...] = _layernorm_rows(alpha * x1_ref[...] + (1.0 + gt_ref[...]) * ffn, g_ref[...], b_ref[...])


def _moe_combine(ys, pos, x1, route, gt2, ln_g, ln_b, tiles_per_batch):
    T, D = x1.shape
    tm = min(TM_ROUTE, T)
    row = lambda v: v.reshape(1, -1)
    n_tiles = T // tm
    return pl.pallas_call(
        functools.partial(_combine_kernel, alpha=2.0 ** 0.25),
        grid=(n_tiles,),
        in_specs=[pl.BlockSpec((tm * TOP_K_EXPERTS,), lambda i: (i,), memory_space=pltpu.SMEM),
                  pl.BlockSpec((tm * TOP_K_EXPERTS,), lambda i: (jnp.minimum(i + 1, n_tiles - 1),),
                               memory_space=pltpu.SMEM),
                  pl.BlockSpec(memory_space=pl.ANY),
                  pl.BlockSpec((tm, D), lambda i: (i, 0)),
                  pl.BlockSpec((tm, LANES), lambda i: (i, 0)),
                  pl.BlockSpec((None, 1, D), lambda i: (i // tiles_per_batch, 0, 0)),
                  pl.BlockSpec((1, D), lambda i: (0, 0)),
                  pl.BlockSpec((1, D), lambda i: (0, 0))],
        out_specs=pl.BlockSpec((tm, D), lambda i: (i, 0)),
        out_shape=jax.ShapeDtypeStruct((T, D), F32),
        scratch_shapes=[pltpu.VMEM((2, TOP_K_EXPERTS, tm, D), F32), pltpu.SemaphoreType.DMA((2,))],
        compiler_params=_cparams(("arbitrary",)),
        name="moe_combine",
    )(pos, pos, ys, x1, route, gt2, row(ln_g), row(ln_b))


def _moe_and_norm(x1, u2, route, counts, gt2, w_gu, b_gu, w_dn, b_dn, ln_g, ln_b):
    B, S, D = x1.shape
    T = B * S
    bm = BM_EXPERT
    n_blocks = -(-T * TOP_K_EXPERTS // bm) + N_EXPERTS
    cnt = counts[0, :N_EXPERTS].astype(I32)
    padded = (cnt + bm - 1) // bm * bm
    pad_ends = jnp.cumsum(padded)
    pad_starts = pad_ends - padded
    route2 = route.reshape(T, LANES)
    e_idx = route2[:, :TOP_K_EXPERTS].astype(I32)
    rank = route2[:, 2 * TOP_K_EXPERTS:3 * TOP_K_EXPERTS].astype(I32)
    pos = (pad_starts[e_idx] + rank).reshape(-1)
    blk_row = jnp.arange(n_blocks, dtype=I32) * bm
    block_e = jnp.minimum(jnp.sum((blk_row[:, None] >= pad_ends[None, :]).astype(I32), axis=1), N_EXPERTS - 1)
    n_used = (pad_ends[-1:] // bm).astype(I32)
    E, F2 = b_gu.shape
    b_gu_p = b_gu.reshape(E, F2 // GU_GROUP, LANES, 2).transpose(0, 1, 3, 2).reshape(E, 1, F2)
    xs = _moe_dispatch(u2.reshape(T, D), pos, n_blocks * bm)
    ys = _moe_experts(xs, block_e, n_used, w_gu, b_gu_p, w_dn, b_dn[:, None, :])
    out = _moe_combine(ys, pos, x1.reshape(T, D), route2, gt2, ln_g, ln_b, S // min(TM_ROUTE, T))
    return out.reshape(B, S, D)


def kernel(x, c, w_ada, b_ada, w_in, shift_mu, rwkv_w0, rwkv_w2, rwkv_a0, rwkv_a2, rwkv_g2, rwkv_k_k, rwkv_k_a, rwkv_r_k, rwkv_ln_w, rwkv_ln_b, mla_q_norm, mla_w_q_up, mla_kv_norm, mla_w_uk, mla_w_uv, idx_w_q, idx_ln_g, idx_ln_b, w_out, ln1_g, ln1_b, w_router, b_router, w_gu, b_gu, w_dn, b_dn, ln2_g, ln2_b):
    depth = w_ada.shape[0]
    assert depth == 1, "DeepNorm constants below are for a single layer"
    l = 0
    mod = _ada_mod(c, w_ada[l], b_ada[l])
    sh1, sc1, gt1, sh2, sc2, gt2 = [m[:, None, :] for m in jnp.split(mod, 6, axis=-1)]
    r, lw, k, v, kk, a, g, qabs, iq, ik, iw, cl = _in_proj(
        x, sc1, sh1, w_in[l], shift_mu[l], rwkv_w0[l], rwkv_w2[l], rwkv_a0[l], rwkv_a2[l], rwkv_g2[l], rwkv_k_k[l],
        rwkv_k_a[l], mla_q_norm[l], mla_w_q_up[l], mla_w_uk[l], mla_kv_norm[l], idx_w_q[l], idx_ln_g[l], idx_ln_b[l])
    o_rwkv = _rwkv_scan(r, lw, k, v, kk, a, g, rwkv_r_k[l], rwkv_ln_w[l], rwkv_ln_b[l])
    o_dsa = _dsa_attn(iq, iw, qabs, ik, cl, mla_w_uv[l])
    x1, u2, route, counts = _mix_out(o_rwkv, o_dsa, x, gt1, sc2, sh2, w_out[l], ln1_g[l], ln1_b[l], w_router[l],
                                     b_router[l])
    return _moe_and_norm(x1, u2, route, counts, gt2, w_gu[l], b_gu[l], w_dn[l], b_dn[l], ln2_g[l], ln2_b[l])
```

```python
import functools
import math

import jax
import jax.numpy as jnp
import numpy as np
from jax import lax
from jax.experimental import pallas as pl
from jax.experimental.pallas import tpu as pltpu

F32 = jnp.float32
BF16 = jnp.bfloat16
I32 = jnp.int32

RWKV_HEAD = 64
N_RWKV_HEADS = 8
D_RWKV = RWKV_HEAD * N_RWKV_HEADS
RANK_DECAY = 64
RANK_A = 64
RANK_GATE = 128
RWKV_GN_EPS = 64e-5
ATT_HEAD = 64
N_ATT_HEADS = 8
D_ATT = ATT_HEAD * N_ATT_HEADS
Q_LORA = 256
KV_LORA = 128
IDX_HEADS = 8
IDX_DIM = 64
TOPK_MAX = 256
N_EXPERTS = 32
TOP_K_EXPERTS = 4
SWIGLU_LIMIT = 7.0
SWIGLU_ALPHA = 1.702
NEG_BIG = -1e30
LOG2E = 1.4426950408889634
INT_MIN = -(2 ** 31)

LANES = 128
SUBLANES = 8
VMEM_LIMIT = 48 * 1024 * 1024
VMEM_LIMIT_EXPERTS = 58 * 1024 * 1024

TM_PROJ = 256
L_CHUNK = 64
CHUNKS_PER_STEP = 4
TQ = 128
KEY_CHUNK = 256
TM_ROUTE = 256
BM_EXPERT = 512

_SEG = (("r", 512, 512), ("k", 512, 512), ("v", 512, 512), ("w", 64, 128), ("a", 64, 128), ("g", 128, 128),
        ("q", 256, 256), ("kv", 128, 128), ("ik", 64, 128), ("iw", 8, 128))
N_SHIFT_P = 512 * 3 + 128 * 3
N_IN_P = sum(s[2] for s in _SEG)


def _cparams(sem):
    return pltpu.CompilerParams(dimension_semantics=sem, vmem_limit_bytes=VMEM_LIMIT)


def _bdot(a, b):
    return jnp.dot(a.astype(BF16), b.astype(BF16), preferred_element_type=F32)


def _bdot_nt(a, b):
    return lax.dot_general(a.astype(BF16), b.astype(BF16), (((1,), (1,)), ((), ())), preferred_element_type=F32)


def _bdot_tn(a, b):
    return lax.dot_general(a.astype(BF16), b.astype(BF16), (((0,), (0,)), ((), ())), preferred_element_type=F32)


def _split2(a):
    hi = a.astype(BF16)
    lo = (a - hi.astype(F32)).astype(BF16)
    return hi, lo


def _split3(a):
    hi = a.astype(BF16)
    r1 = a - hi.astype(F32)
    mid = r1.astype(BF16)
    lo = (r1 - mid.astype(F32)).astype(BF16)
    return hi, mid, lo


def _dot3(a, b, dims=(((1,), (0,)), ((), ()))):
    ah, al = _split2(a)
    bh, bl = _split2(b)
    d = functools.partial(lax.dot_general, dimension_numbers=dims, preferred_element_type=F32)
    return d(ah, bh) + (d(ah, bl) + d(al, bh))


def _dot_exact_rhs(a, b_exact, nsplit=3):
    parts = _split3(a) if nsplit == 3 else _split2(a)
    acc = None
    for p in parts[::-1]:
        t = jnp.dot(p, b_exact, preferred_element_type=F32)
        acc = t if acc is None else acc + t
    return acc


def _dot_exact_lhs(a_exact, b, nsplit=3):
    parts = _split3(b) if nsplit == 3 else _split2(b)
    acc = None
    for p in parts[::-1]:
        t = jnp.dot(a_exact, p, preferred_element_type=F32)
        acc = t if acc is None else acc + t
    return acc


def _sigmoid(x):
    return 1.0 / (1.0 + jnp.exp(-x))


def _softplus(x):
    return jnp.maximum(x, 0.0) + jnp.log(1.0 + jnp.exp(-jnp.abs(x)))


def _ada_kernel(c_ref, w_ref, b_ref, o_ref):
    c = c_ref[...]
    o_ref[...] = _dot3(c * _sigmoid(c), w_ref[...]) + b_ref[...]


def _ada_mod(c, w_ada, b_ada):
    B, D = c.shape
    N = w_ada.shape[1]
    tn = 1024
    return pl.pallas_call(
        _ada_kernel,
        grid=(N // tn,),
        in_specs=[pl.BlockSpec((B, D), lambda j: (0, 0)),
                  pl.BlockSpec((D, tn), lambda j: (0, j)),
                  pl.BlockSpec((1, tn), lambda j: (0, j))],
        out_specs=pl.BlockSpec((B, tn), lambda j: (0, j)),
        out_shape=jax.ShapeDtypeStruct((B, N), F32),
        compiler_params=_cparams(("arbitrary",)),
        name="ada_mod",
    )(c, w_ada, b_ada.reshape(1, N))


def _in_proj_kernel(x_ref, sc_ref, sh_ref, win_ref, mu_ref, w0_ref, w2_ref, a0_ref, a2_ref, g2_ref, kk_ref, ka_ref,
                    ones_ref, qn_ref, wq_ref, wuk_ref, kvn_ref, wiq_ref, ig_ref, ib_ref,
                    r_o, lw_o, k_o, v_o, kkn_o, a_o, g_o, qabs_o, iq_o, ik_o, iw_o, cl_o, carry):
    i = pl.program_id(1)
    tm = x_ref.shape[0]

    @pl.when(i == 0)
    def _():
        carry[...] = jnp.zeros_like(carry)

    u = x_ref[...] * (1.0 + sc_ref[...]) + sh_ref[...]
    p = _bdot(u, win_ref[...])
    ps = p[:, :N_SHIFT_P]
    rows = lax.broadcasted_iota(I32, (tm, 1), 0)
    prev = jnp.where(rows == 0, carry[0:1, :], pltpu.roll(ps, 1, 0))
    carry[0:1, :] = ps[tm - 1:tm, :]
    ps = ps + mu_ref[...] * (prev - ps)

    pr, pk, pv = ps[:, 0:512], ps[:, 512:1024], ps[:, 1024:1536]
    pw, pa, pg = ps[:, 1536:1664], ps[:, 1664:1792], ps[:, 1792:1920]
    w_log = -_softplus(-(w0_ref[...] + _dot3(jnp.tanh(pw), w2_ref[...]))) - 0.5
    lw_o[...] = -jnp.exp(w_log)
    a = _sigmoid(a0_ref[...] + _dot3(pa, a2_ref[...]))
    g_o[...] = _dot3(_sigmoid(pg), g2_ref[...])
    kk = pk * kk_ref[...]
    ssq = _dot_exact_rhs(kk * kk, ones_ref[...], nsplit=2)
    kkn_o[...] = kk / jnp.maximum(jnp.sqrt(ssq), 1e-12)
    k_o[...] = pk * (1.0 + (a - 1.0) * ka_ref[...])
    r_o[...] = pr
    v_o[...] = pv
    a_o[...] = a

    pq, pkv = p[:, 1920:2176], p[:, 2176:2304]
    pik, piw = p[:, 2304:2432], p[:, 2432:2560]
    q_lat = pq * lax.rsqrt(jnp.mean(pq * pq, axis=-1, keepdims=True) + 1e-6) * qn_ref[...]
    q = _bdot(q_lat, wq_ref[...])
    qabs_o[...] = (_bdot(q, wuk_ref[...]) * (ATT_HEAD ** -0.5 * LOG2E)).astype(BF16)
    c_lat = pkv * lax.rsqrt(jnp.mean(pkv * pkv, axis=-1, keepdims=True) + 1e-6) * kvn_ref[...]
    spos = i * tm + rows
    s_hi = (spos >> 7).astype(F32)
    s_lo = (spos & (LANES - 1)).astype(F32)
    lane_t = lax.broadcasted_iota(I32, (1, LANES), 1)
    extra = jnp.where(lane_t == 0, 1.0, jnp.where((lane_t == 1) | (lane_t == 2), s_hi,
                                                  jnp.where((lane_t == 3) | (lane_t == 4), s_lo, 0.0)))
    cl_o[...] = jnp.concatenate([c_lat, extra], axis=-1).astype(BF16)
    iq_o[...] = (_bdot(q_lat, wiq_ref[...]) * (IDX_DIM ** -0.5)).astype(BF16)
    lane = lax.broadcasted_iota(I32, (1, LANES), 1)
    valid = lane < IDX_DIM
    mu = jnp.sum(pik, axis=-1, keepdims=True) * (1.0 / IDX_DIM)
    dlt = jnp.where(valid, pik - mu, 0.0)
    var = jnp.sum(dlt * dlt, axis=-1, keepdims=True) * (1.0 / IDX_DIM)
    ik_o[...] = jnp.where(valid, dlt * lax.rsqrt(var + 1e-5) * ig_ref[...] + ib_ref[...], 0.0).astype(BF16)
    iw_o[...] = piw * (IDX_HEADS ** -0.5)


def _pad_cols(w, widths):
    parts, o = [], 0
    for true, padded in widths:
        seg = w[..., o:o + true]
        if padded > true:
            seg = jnp.pad(seg, [(0, 0)] * (w.ndim - 1) + [(0, padded - true)])
        parts.append(seg)
        o += true
    return jnp.concatenate(parts, axis=-1)


def _pad_rows(w, rows):
    return jnp.pad(w, ((0, rows - w.shape[0]), (0, 0)))


def _block_diag(blocks):
    H, a, b = blocks.shape
    eye = jnp.eye(H, dtype=blocks.dtype)
    return (eye[:, None, :, None] * blocks[:, :, None, :]).reshape(H * a, H * b)


def _head_ones(n, head):
    idx = np.arange(n) // head
    return jnp.asarray(idx[:, None] == idx[None, :], BF16)


def _in_proj(x, sc1, sh1, w_in, shift_mu, w0, w2, a0, a2, g2, k_k, k_a, q_norm, w_q_up, w_uk, kv_norm, idx_w_q,
             idx_ln_g, idx_ln_b):
    B, S, D = x.shape
    tm = min(TM_PROJ, S)
    widths = tuple((s[1], s[2]) for s in _SEG)
    win_p = _pad_cols(w_in, widths).astype(BF16)
    mu_p = _pad_cols(shift_mu.reshape(1, -1), widths[:6])
    w2_p = _pad_rows(w2, LANES)
    a2_p = _pad_rows(a2, LANES)
    wuk_bd = _block_diag(w_uk).astype(BF16)
    wiq_p = _pad_cols(idx_w_q, ((IDX_DIM, LANES),) * IDX_HEADS).astype(BF16)
    ig_p = _pad_cols(idx_ln_g.reshape(1, -1), ((IDX_DIM, LANES),))
    ib_p = _pad_cols(idx_ln_b.reshape(1, -1), ((IDX_DIM, LANES),))
    row = lambda v: v.reshape(1, -1)
    tok = lambda n: pl.BlockSpec((None, tm, n), lambda b, i: (b, i, 0))
    mod = pl.BlockSpec((None, 1, D), lambda b, i: (b, 0, 0))
    full = lambda a: pl.BlockSpec(a.shape, lambda b, i: (0,) * a.ndim)
    consts = [win_p, mu_p, row(w0), w2_p, row(a0), a2_p, g2, row(k_k), row(k_a), _head_ones(D_RWKV, RWKV_HEAD),
              row(q_norm), w_q_up.astype(BF16), wuk_bd, row(kv_norm), wiq_p, ig_p, ib_p]
    outs = [(D_RWKV, F32)] * 7 + [(N_ATT_HEADS * KV_LORA, BF16), (IDX_HEADS * LANES, BF16), (LANES, BF16),
                                  (LANES, F32), (KV_LORA + LANES, BF16)]
    return pl.pallas_call(
        _in_proj_kernel,
        grid=(B, S // tm),
        in_specs=[tok(D), mod, mod] + [full(a) for a in consts],
        out_specs=[tok(n) for n, _ in outs],
        out_shape=[jax.ShapeDtypeStruct((B, S, n), dt) for n, dt in outs],
        scratch_shapes=[pltpu.VMEM((8, N_SHIFT_P), F32)],
        compiler_params=_cparams(("arbitrary", "arbitrary")),
        name="in_proj",
    )(x, sc1, sh1, *consts)


def _rwkv_kernel(r_ref, lw_ref, k_ref, v_ref, kk_ref, a_ref, g_ref, rk_ref, lnw_ref, lnb_ref, tri_ref, ones_ref,
                 o_ref, state, *, L):
    c = pl.program_id(1)
    nc = r_ref.shape[0] // L

    @pl.when(c == 0)
    def _():
        state[...] = jnp.zeros_like(state)

    r, lw, k, v, kk, a = r_ref[...], lw_ref[...], k_ref[...], v_ref[...], kk_ref[...], a_ref[...]
    cum = _dot_exact_lhs(tri_ref[...], lw)
    rows = [slice(ci * L, (ci + 1) * L) for ci in range(nc)]
    last = [cum[(ci + 1) * L - 1:(ci + 1) * L, :] for ci in range(nc)]
    cum_last = jnp.concatenate([jnp.broadcast_to(z, (L, z.shape[1])) for z in last], axis=0)
    w_incl = jnp.exp(cum)
    w_inv = jnp.exp(-cum)
    w_rel = jnp.exp(cum_last - cum)
    w_last = [jnp.exp(z) for z in last]
    bvec = kk * a
    at = -kk * jnp.exp(cum - lw)
    rt = r * w_incl
    bt = bvec * w_inv
    kt = k * w_inv
    bh = bvec * w_rel
    kh = k * w_rel
    ti = lax.broadcasted_iota(I32, (L, L), 0)
    tj = lax.broadcasted_iota(I32, (L, L), 1)
    strict = tj < ti
    incl = tj <= ti
    eye = lax.broadcasted_iota(I32, (RWKV_HEAD, RWKV_HEAD), 0) == lax.broadcasted_iota(I32, (RWKV_HEAD, RWKV_HEAD), 1)
    NT = (((1,), (1,)), ((), ()))
    TN = (((0,), (0,)), ((), ()))
    heads = range(N_RWKV_HEADS)
    sls = [slice(h * RWKV_HEAD, (h + 1) * RWKV_HEAD) for h in heads]
    units = [(ci, h) for ci in range(nc) for h in heads]
    idx = range(len(units))
    mm = lambda x, y, dims=(((1,), (0,)), ((), ())): lax.dot_general(
        x.astype(BF16), y.astype(BF16), dims, preferred_element_type=F32)
    at_b, rt_b, bt_b, kt_b, bh_b, kh_b, v_b = [z.astype(BF16) for z in (at, rt, bt, kt, bh, kh, v)]
    cut = lambda z, u: z[rows[u[0]], sls[u[1]]]
    vh = [cut(v_b, u) for u in units]
    ath = [cut(at_b, u) for u in units]
    ar = [jnp.concatenate([cut(at_b, u), cut(rt_b, u)], axis=0) for u in units]
    g_b = [mm(ar[i], cut(bt_b, units[i]), NT) for i in idx]
    g_k = [mm(ar[i], cut(kt_b, units[i]), NT) for i in idx]
    n_ab = [jnp.where(strict, g_b[i][:L], 0.0) for i in idx]
    a_ak = [jnp.where(strict, g_k[i][:L], 0.0) for i in idx]
    a_rb = [jnp.where(incl, g_b[i][L:], 0.0).astype(BF16) for i in idx]
    a_rk = [jnp.where(incl, g_k[i][L:], 0.0) for i in idx]
    akv = [mm(a_ak[i], vh[i]) for i in idx]
    eye_l = jnp.where(ti == tj, 1.0, 0.0)
    tinv = [eye_l + n_ab[i] for i in idx]
    pw = n_ab
    for _ in range(int(math.log2(L)) - 1):
        pw = [mm(pw[i], pw[i]) for i in idx]
        tinv = [tinv[i] + mm(pw[i], tinv[i]) for i in idx]
    tinv = [t.astype(BF16) for t in tinv]
    a_t = [mm(tinv[i], ath[i]).astype(BF16) for i in idx]
    y = [mm(tinv[i], akv[i]).astype(BF16) for i in idx]
    m_c = [jnp.where(eye, w_last[units[i][0]][:, sls[units[i][1]]], 0.0) + mm(a_t[i], cut(bh_b, units[i]), TN)
           for i in idx]
    c_c = [mm(y[i], cut(bh_b, units[i]), TN) + mm(vh[i], cut(kh_b, units[i]), TN) for i in idx]
    q_c = [cut(rt, units[i]) + mm(a_rb[i], a_t[i]) for i in idx]
    o_loc = [mm(a_rb[i], y[i]) + mm(a_rk[i], vh[i]) for i in idx]
    s = [state[h] for h in heads]
    for ci in range(nc):
        for h in heads:
            i = ci * N_RWKV_HEADS + h
            o = o_loc[i] + _dot3(q_c[i], s[h], NT)
            s[h] = _dot3(s[h], m_c[i]) + c_c[i]
            mu = jnp.mean(o, axis=-1, keepdims=True)
            d = o - mu
            var = jnp.mean(d * d, axis=-1, keepdims=True)
            o_ref[rows[ci], sls[h]] = d * lax.rsqrt(var + RWKV_GN_EPS)
    for h in heads:
        state[h] = s[h]
    bonus = _dot_exact_rhs(r * k * rk_ref[...], ones_ref[...], nsplit=3) * v
    o_ref[...] = (o_ref[...] * lnw_ref[...] + lnb_ref[...] + bonus) * g_ref[...]


def _rwkv_scan(r, lw, k, v, kk, a, g, r_k, ln_w, ln_b):
    B, S, DR = r.shape
    L = min(L_CHUNK, S)
    lb = min(L * CHUNKS_PER_STEP, S)
    tri = jnp.asarray(np.kron(np.eye(lb // L), np.tril(np.ones((L, L)))), BF16)
    row = lambda z: z.reshape(1, -1)
    tok = pl.BlockSpec((None, lb, DR), lambda b, c: (b, c, 0))
    full = lambda z: pl.BlockSpec(z.shape, lambda b, c: (0,) * z.ndim)
    consts = [row(r_k), row(ln_w), row(ln_b), tri, _head_ones(DR, RWKV_HEAD)]
    return pl.pallas_call(
        functools.partial(_rwkv_kernel, L=L),
        grid=(B, S // lb),
        in_specs=[tok] * 7 + [full(z) for z in consts],
        out_specs=tok,
        out_shape=jax.ShapeDtypeStruct((B, S, DR), F32),
        scratch_shapes=[pltpu.VMEM((N_RWKV_HEADS, RWKV_HEAD, RWKV_HEAD), F32)],
        compiler_params=_cparams(("arbitrary", "arbitrary")),
        name="rwkv_scan",
    )(r, lw, k, v, kk, a, g, *consts)


def _alibi_cols():
    slope = np.asarray([2.0 ** (-8.0 * (h + 1) / N_ATT_HEADS) * LOG2E for h in range(N_ATT_HEADS)], np.float32)
    c_hi = slope.astype(BF16).astype(np.float32)
    c_lo = (slope - c_hi).astype(BF16).astype(np.float32)
    t = np.zeros((N_ATT_HEADS, LANES), np.float32)
    t[:, 1], t[:, 2], t[:, 3], t[:, 4] = LANES * c_hi, LANES * c_lo, c_hi, c_lo
    return jnp.asarray(t.astype(BF16))


def _dsa_kernel(iq_ref, iw_ref, qa_ref, ik_ref, ca_ref, wuv_ref, triu_ref, acol_ref, o_ref, key_ref, bias_ref,
                lg_ref, *, topk, q_off, select):
    qi = pl.program_id(1) + q_off
    tq = qa_ref.shape[0]
    sk = ca_ref.shape[0]
    kc = min(KEY_CHUNK, sk)
    chunks = [slice(c * kc, (c + 1) * kc) for c in range(sk // kc)]
    tpos = qi * tq + lax.broadcasted_iota(I32, (tq, 1), 0)
    spos = lax.broadcasted_iota(I32, (1, sk), 1)
    NT = (((1,), (1,)), ((), ()))

    if not select:
        bias_ref[...] = jnp.where(spos <= tpos, 0.0, NEG_BIG)
    else:
        iw = iw_ref[...]
        iwh = [iw[:, h:h + 1] for h in range(IDX_HEADS)]
        for cs in chunks:
            ikc = ik_ref[cs, :]
            score = jnp.zeros((tq, kc), F32)
            for h in range(IDX_HEADS):
                s = lax.dot_general(iq_ref[:, h * LANES:(h + 1) * LANES], ikc, NT, preferred_element_type=F32)
                score = score + iwh[h] * jnp.maximum(s, 0.0)
            bits = pltpu.bitcast(score + 0.0, I32)
            key = bits ^ ((bits >> 31) & 0x7FFFFFFF)
            key_ref[:, cs] = jnp.where(spos[:, cs] <= tpos, key, INT_MIN)
        kcount = jnp.minimum(topk, tpos + 1).astype(F32)

        def count_ge(cand):
            return jnp.sum(jnp.where(key_ref[...] >= cand, 1.0, 0.0), axis=-1, keepdims=True)

        thr = jnp.where(count_ge(jnp.zeros((tq, 1), I32)) >= kcount, 0, INT_MIN).astype(I32)

        def bit_step(i, thr):
            cand = thr | (1 << (30 - i))
            return jnp.where(count_ge(cand) >= kcount, cand, thr)

        thr = lax.fori_loop(0, 31, bit_step, thr)

        key = key_ref[...]
        gt = key > thr
        eq = key == thr
        need = kcount - jnp.sum(jnp.where(gt, 1.0, 0.0), axis=-1, keepdims=True)
        eqf = jnp.where(eq, 1.0, 0.0)
        carry = jnp.zeros((tq, 1), F32)
        for j in range(sk // LANES):
            sl = slice(j * LANES, (j + 1) * LANES)
            e = eqf[:, sl]
            before = jnp.dot(e.astype(BF16), triu_ref[...], preferred_element_type=F32) + carry
            sel = gt[:, sl] | (eq[:, sl] & (before < need))
            bias_ref[:, sl] = jnp.where(sel, 0.0, NEG_BIG)
            carry = carry + jnp.sum(e, axis=-1, keepdims=True)

    outs = []
    for h in range(N_ATT_HEADS):
        q_aug = jnp.concatenate([qa_ref[:, h * KV_LORA:(h + 1) * KV_LORA],
                                 jnp.broadcast_to(acol_ref[h:h + 1, :], (tq, LANES))], axis=-1)
        m = jnp.full((tq, 1), -jnp.inf, F32)
        for cs in chunks:
            lg = lax.dot_general(q_aug, ca_ref[cs, :], NT, preferred_element_type=F32) + bias_ref[:, cs]
            lg_ref[:, cs] = lg
            m = jnp.maximum(m, jnp.max(lg, axis=-1, keepdims=True))
        pv = jnp.zeros((tq, KV_LORA + LANES), F32)
        for cs in chunks:
            p = jnp.exp2(lg_ref[:, cs] - m).astype(BF16)
            pv = pv + jnp.dot(p, ca_ref[cs, :], preferred_element_type=F32)
        outs.append(pv[:, :KV_LORA] / pv[:, KV_LORA:KV_LORA + 1])
    o_lat = jnp.concatenate(outs, axis=-1)
    o_ref[...] = _bdot(o_lat, wuv_ref[...])


def _dsa_attn(iq, iw, qabs, ik, ca, w_uv):
    B, S, _ = iq.shape
    tq = min(TQ, S)
    topk = min(TOPK_MAX, S // 4)
    wuv_bd = _block_diag(w_uv).astype(BF16)
    triu = jnp.asarray(np.triu(np.ones((LANES, LANES)), 1), BF16)
    acol = _alibi_cols()
    full = lambda z: pl.BlockSpec(z.shape, lambda b, i: (0,) * z.ndim)
    nq = 2 if S // tq >= 2 else 1
    outs = []
    for q_off in range(0, S // tq, nq):
        sk = (q_off + nq) * tq
        tok = lambda n, q_off=q_off: pl.BlockSpec((None, tq, n), lambda b, i: (b, i + q_off, 0))
        seq = lambda n, sk=sk: pl.BlockSpec((None, sk, n), lambda b, i: (b, 0, 0))
        outs.append(pl.pallas_call(
            functools.partial(_dsa_kernel, topk=topk, q_off=q_off, select=sk > topk),
            grid=(B, nq),
            in_specs=[tok(IDX_HEADS * LANES), tok(LANES), tok(N_ATT_HEADS * KV_LORA), seq(LANES), seq(KV_LORA + LANES),
                      full(wuv_bd), full(triu), full(acol)],
            out_specs=pl.BlockSpec((None, tq, D_ATT), lambda b, i: (b, i, 0)),
            out_shape=jax.ShapeDtypeStruct((B, nq * tq, D_ATT), F32),
            scratch_shapes=[pltpu.VMEM((tq, sk), I32), pltpu.VMEM((tq, sk), F32), pltpu.VMEM((tq, sk), F32)],
            compiler_params=_cparams(("arbitrary", "arbitrary")),
            name=f"dsa_attn_k{sk}",
        )(iq, iw, qabs, ik, ca, wuv_bd, triu, acol))
    return jnp.concatenate(outs, axis=1)


def _layernorm_rows(y, g, b):
    mu = jnp.mean(y, axis=-1, keepdims=True)
    d = y - mu
    var = jnp.mean(d * d, axis=-1, keepdims=True)
    return d * lax.rsqrt(var + 1e-5) * g + b


def _mix_kernel(orw_ref, ods_ref, x_ref, gt_ref, sc_ref, sh_ref, wtop_ref, wbot_ref, g_ref, b_ref, wr_ref, br_ref,
                tril_ref, x1_o, u2_o, route_o, cnt_o, carry, *, alpha):
    first = (pl.program_id(0) == 0) & (pl.program_id(1) == 0)
    tm = x_ref.shape[0]

    @pl.when(first)
    def _():
        carry[...] = jnp.zeros_like(carry)

    mix = _bdot(orw_ref[...], wtop_ref[...]) + _bdot(ods_ref[...], wbot_ref[...])
    x1 = _layernorm_rows(alpha * x_ref[...] + (1.0 + gt_ref[...]) * mix, g_ref[...], b_ref[...])
    x1_o[...] = x1
    u2 = x1 * (1.0 + sc_ref[...]) + sh_ref[...]
    u2_o[...] = u2

    lg = _dot3(u2, wr_ref[...]) + br_ref[...]
    lane = lax.broadcasted_iota(I32, (tm, LANES), 1)
    idxs, vals = [], []
    for _ in range(TOP_K_EXPERTS):
        m = jnp.max(lg, axis=-1, keepdims=True)
        idx = jnp.min(jnp.where(lg == m, lane, LANES), axis=-1, keepdims=True)
        idxs.append(idx)
        vals.append(m)
        lg = jnp.where(lane == idx, -jnp.inf, lg)
    es = [jnp.exp(v - vals[0]) for v in vals]
    den = es[0] + es[1] + es[2] + es[3]
    hot = jnp.zeros((tm, LANES), F32)
    for idx in idxs:
        hot = hot + jnp.where(lane == idx, 1.0, 0.0)
    before = jnp.dot(tril_ref[...], hot.astype(BF16), preferred_element_type=F32) + carry[0:1, :]
    route = jnp.zeros((tm, LANES), F32)
    for k in range(TOP_K_EXPERTS):
        rank = jnp.sum(jnp.where(lane == idxs[k], before, 0.0), axis=-1, keepdims=True)
        route = jnp.where(lane == k, idxs[k].astype(F32), route)
        route = jnp.where(lane == TOP_K_EXPERTS + k, es[k] / den, route)
        route = jnp.where(lane == 2 * TOP_K_EXPERTS + k, rank, route)
    route_o[...] = route
    carry[0:1, :] = carry[0:1, :] + jnp.sum(hot, axis=0, keepdims=True)
    cnt_o[...] = carry[...]


def _mix_out(o_rwkv, o_dsa, x, gt1, sc2, sh2, w_out, ln_g, ln_b, w_router, b_router):
    B, S, D = x.shape
    tm = min(TM_PROJ, S)
    alpha = 2.0 ** 0.25
    wtop = w_out[:D_RWKV].astype(BF16)
    wbot = w_out[D_RWKV:].astype(BF16)
    wr_p = jnp.pad(w_router, ((0, 0), (0, LANES - N_EXPERTS)))
    br_p = jnp.pad(b_router.reshape(1, -1), ((0, 0), (0, LANES - N_EXPERTS)), constant_values=NEG_BIG)
    tril = jnp.asarray(np.tril(np.ones((tm, tm)), -1), BF16)
    row = lambda v: v.reshape(1, -1)
    tok = lambda n: pl.BlockSpec((None, tm, n), lambda b, i: (b, i, 0))
    mod = pl.BlockSpec((None, 1, D), lambda b, i: (b, 0, 0))
    full = lambda a: pl.BlockSpec(a.shape, lambda b, i: (0,) * a.ndim)
    consts = [wtop, wbot, row(ln_g), row(ln_b), wr_p, br_p, tril]
    return pl.pallas_call(
        functools.partial(_mix_kernel, alpha=alpha),
        grid=(B, S // tm),
        in_specs=[tok(D_RWKV), tok(D_ATT), tok(D), mod, mod, mod] + [full(a) for a in consts],
        out_specs=[tok(D), tok(D), tok(LANES), pl.BlockSpec((8, LANES), lambda b, i: (0, 0))],
        out_shape=[jax.ShapeDtypeStruct((B, S, D), F32), jax.ShapeDtypeStruct((B, S, D), F32),
                   jax.ShapeDtypeStruct((B, S, LANES), F32), jax.ShapeDtypeStruct((8, LANES), F32)],
        scratch_shapes=[pltpu.VMEM((8, LANES), F32)],
        compiler_params=_cparams(("arbitrary", "arbitrary")),
        name="mix_out",
    )(o_rwkv, o_dsa, x, gt1, sc2, sh2, *consts)


ISSUE_UNROLL = 8


def _dispatch_kernel(pos_ref, u_ref, xs_in, xs_out, sem):
    del xs_in
    tm = u_ref.shape[0]

    def start(r, carry):
        for k in range(TOP_K_EXPERTS):
            dst = pos_ref[r * TOP_K_EXPERTS + k]
            pltpu.make_async_copy(u_ref.at[pl.ds(r, 1)], xs_out.at[pl.ds(dst, 1)], sem).start(priority=k % 2)
        return carry

    lax.fori_loop(0, tm, start, 0, unroll=ISSUE_UNROLL)
    for k in range(TOP_K_EXPERTS):
        pltpu.make_async_copy(u_ref, xs_out.at[pl.ds(0, tm)], sem).wait()


def _moe_dispatch(u2, pos, n_rows):
    T, D = u2.shape
    tm = min(TM_ROUTE, T)
    return pl.pallas_call(
        _dispatch_kernel,
        grid=(T // tm,),
        in_specs=[pl.BlockSpec((tm * TOP_K_EXPERTS,), lambda i: (i,), memory_space=pltpu.SMEM),
                  pl.BlockSpec((tm, D), lambda i: (i, 0)),
                  pl.BlockSpec(memory_space=pl.ANY)],
        out_specs=pl.BlockSpec(memory_space=pl.ANY),
        out_shape=jax.ShapeDtypeStruct((n_rows, D), F32),
        scratch_shapes=[pltpu.SemaphoreType.DMA(())],
        input_output_aliases={2: 0},
        compiler_params=_cparams(("arbitrary",)),
        name="moe_dispatch",
    )(pos, u2, jnp.zeros((n_rows, D), F32))


GU_GROUP = 2 * LANES


def _deinterleave_perm():
    p = np.zeros((GU_GROUP, GU_GROUP), np.float32)
    l = np.arange(LANES)
    p[2 * l, l] = 1.0
    p[2 * l + 1, LANES + l] = 1.0
    return jnp.asarray(p, BF16)


def _expert_kernel(be_ref, nb_ref, xs_ref, wgu_ref, bgu_ref, wd_ref, bd_ref, perm_ref, ys_ref, wp, wdb):
    i = pl.program_id(0)
    used = i < nb_ref[0]
    new_expert = (i == 0) | (be_ref[i] != be_ref[jnp.maximum(i - 1, 0)])
    n_groups = wgu_ref.shape[1] // GU_GROUP

    @pl.when(used & new_expert)
    def _():
        for j in range(n_groups):
            sl = slice(j * GU_GROUP, (j + 1) * GU_GROUP)
            wp[:, sl] = jnp.dot(wgu_ref[:, sl].astype(BF16), perm_ref[...], preferred_element_type=F32).astype(BF16)
        wdb[...] = wd_ref[...].astype(BF16)

    @pl.when(used)
    def _():
        xb = xs_ref[...].astype(BF16)
        gu = jnp.dot(xb, wp[...], preferred_element_type=F32) + bgu_ref[...]
        hs = []
        for j in range(n_groups):
            gate = jnp.minimum(gu[:, j * GU_GROUP:j * GU_GROUP + LANES], SWIGLU_LIMIT)
            up = jnp.clip(gu[:, j * GU_GROUP + LANES:(j + 1) * GU_GROUP], -SWIGLU_LIMIT, SWIGLU_LIMIT)
            hs.append(((up + 1.0) * (gate * _sigmoid(gate * SWIGLU_ALPHA))).astype(BF16))
        h = jnp.concatenate(hs, axis=-1)
        ys_ref[...] = jnp.dot(h, wdb[...], preferred_element_type=F32) + bd_ref[...]

    @pl.when(jnp.logical_not(used))
    def _():
        ys_ref[...] = jnp.zeros_like(ys_ref)


def _moe_experts(xs, block_e, n_used, w_gu, b_gu_p, w_dn, b_dn):
    n_rows, D = xs.shape
    E, _, F2 = w_gu.shape
    bm = BM_EXPERT
    n_blocks = n_rows // bm
    perm = _deinterleave_perm()
    wspec = lambda shp: pl.BlockSpec((None,) + shp, lambda i, be, nb: (be[i], 0, 0))
    return pl.pallas_call(
        _expert_kernel,
        grid_spec=pltpu.PrefetchScalarGridSpec(
            num_scalar_prefetch=2,
            grid=(n_blocks,),
            in_specs=[pl.BlockSpec((bm, D), lambda i, be, nb: (jnp.minimum(i, nb[0] - 1), 0)),
                      wspec((D, F2)), wspec((1, F2)), wspec((F2 // 2, D)), wspec((1, D)),
                      pl.BlockSpec(perm.shape, lambda i, be, nb: (0, 0))],
            out_specs=pl.BlockSpec((bm, D), lambda i, be, nb: (i, 0)),
            scratch_shapes=[pltpu.VMEM((D, F2), BF16), pltpu.VMEM((F2 // 2, D), BF16)],
        ),
        out_shape=jax.ShapeDtypeStruct((n_rows, D), F32),
        compiler_params=pltpu.CompilerParams(dimension_semantics=("arbitrary",), vmem_limit_bytes=VMEM_LIMIT_EXPERTS),
        name="moe_experts",
    )(block_e, n_used, xs, w_gu, b_gu_p, w_dn, b_dn, perm)


def _combine_kernel(pos_ref, posn_ref, ys_ref, x1_ref, route_ref, gt_ref, g_ref, b_ref, o_ref, buf, sem, *, alpha):
    i = pl.program_id(0)
    n = pl.num_programs(0)
    tm = x1_ref.shape[0]
    slot = i % 2

    def gather(p_ref, s):
        def start(r, carry):
            for k in range(TOP_K_EXPERTS):
                src = p_ref[r * TOP_K_EXPERTS + k]
                pltpu.make_async_copy(ys_ref.at[pl.ds(src, 1)], buf.at[s, k, pl.ds(r, 1)],
                                      sem.at[s]).start(priority=k % 2)
            return carry
        lax.fori_loop(0, tm, start, 0, unroll=ISSUE_UNROLL)

    @pl.when(i == 0)
    def _():
        gather(pos_ref, 0)

    @pl.when(i + 1 < n)
    def _():
        gather(posn_ref, 1 - slot)

    for k in range(TOP_K_EXPERTS):
        pltpu.make_async_copy(ys_ref.at[pl.ds(0, tm)], buf.at[slot, k], sem.at[slot]).wait()

    route = route_ref[...]
    ffn = jnp.zeros(x1_ref.shape, F32)
    for k in range(TOP_K_EXPERTS):
        ffn = ffn + route[:, TOP_K_EXPERTS + k:TOP_K_EXPERTS + k + 1] * buf[slot, k]
    o_ref[...] = _layernorm_rows(alpha * x1_ref[...] + (1.0 + gt_ref[...]) * ffn, g_ref[...], b_ref[...])


def _moe_combine(ys, pos, x1, route, gt2, ln_g, ln_b, tiles_per_batch):
    T, D = x1.shape
    tm = min(TM_ROUTE, T)
    row = lambda v: v.reshape(1, -1)
    n_tiles = T // tm
    return pl.pallas_call(
        functools.partial(_combine_kernel, alpha=2.0 ** 0.25),
        grid=(n_tiles,),
        in_specs=[pl.BlockSpec((tm * TOP_K_EXPERTS,), lambda i: (i,), memory_space=pltpu.SMEM),
                  pl.BlockSpec((tm * TOP_K_EXPERTS,), lambda i: (jnp.minimum(i + 1, n_tiles - 1),),
                               memory_space=pltpu.SMEM),
                  pl.BlockSpec(memory_space=pl.ANY),
                  pl.BlockSpec((tm, D), lambda i: (i, 0)),
                  pl.BlockSpec((tm, LANES), lambda i: (i, 0)),
                  pl.BlockSpec((None, 1, D), lambda i: (i // tiles_per_batch, 0, 0)),
                  pl.BlockSpec((1, D), lambda i: (0, 0)),
                  pl.BlockSpec((1, D), lambda i: (0, 0))],
        out_specs=pl.BlockSpec((tm, D), lambda i: (i, 0)),
        out_shape=jax.ShapeDtypeStruct((T, D), F32),
        scratch_shapes=[pltpu.VMEM((2, TOP_K_EXPERTS, tm, D), F32), pltpu.SemaphoreType.DMA((2,))],
        compiler_params=_cparams(("arbitrary",)),
        name="moe_combine",
    )(pos, pos, ys, x1, route, gt2, row(ln_g), row(ln_b))


def _moe_and_norm(x1, u2, route, counts, gt2, w_gu, b_gu, w_dn, b_dn, ln_g, ln_b):
    B, S, D = x1.shape
    T = B * S
    bm = BM_EXPERT
    n_blocks = -(-T * TOP_K_EXPERTS // bm) + N_EXPERTS
    cnt = counts[0, :N_EXPERTS].astype(I32)
    padded = (cnt + bm - 1) // bm * bm
    pad_ends = jnp.cumsum(padded)
    pad_starts = pad_ends - padded
    route2 = route.reshape(T, LANES)
    e_idx = route2[:, :TOP_K_EXPERTS].astype(I32)
    rank = route2[:, 2 * TOP_K_EXPERTS:3 * TOP_K_EXPERTS].astype(I32)
    pos = (pad_starts[e_idx] + rank).reshape(-1)
    blk_row = jnp.arange(n_blocks, dtype=I32) * bm
    block_e = jnp.minimum(jnp.sum((blk_row[:, None] >= pad_ends[None, :]).astype(I32), axis=1), N_EXPERTS - 1)
    n_used = (pad_ends[-1:] // bm).astype(I32)
    E, F2 = b_gu.shape
    b_gu_p = b_gu.reshape(E, F2 // GU_GROUP, LANES, 2).transpose(0, 1, 3, 2).reshape(E, 1, F2)
    xs = _moe_dispatch(u2.reshape(T, D), pos, n_blocks * bm)
    ys = _moe_experts(xs, block_e, n_used, w_gu, b_gu_p, w_dn, b_dn[:, None, :])
    out = _moe_combine(ys, pos, x1.reshape(T, D), route2, gt2, ln_g, ln_b, S // min(TM_ROUTE, T))
    return out.reshape(B, S, D)


def kernel(x, c, w_ada, b_ada, w_in, shift_mu, rwkv_w0, rwkv_w2, rwkv_a0, rwkv_a2, rwkv_g2, rwkv_k_k, rwkv_k_a, rwkv_r_k, rwkv_ln_w, rwkv_ln_b, mla_q_norm, mla_w_q_up, mla_kv_norm, mla_w_uk, mla_w_uv, idx_w_q, idx_ln_g, idx_ln_b, w_out, ln1_g, ln1_b, w_router, b_router, w_gu, b_gu, w_dn, b_dn, ln2_g, ln2_b):
    depth = w_ada.shape[0]
    assert depth == 1, "DeepNorm constants below are for a single layer"
    l = 0
    mod = _ada_mod(c, w_ada[l], b_ada[l])
    sh1, sc1, gt1, sh2, sc2, gt2 = [m[:, None, :] for m in jnp.split(mod, 6, axis=-1)]
    r, lw, k, v, kk, a, g, qabs, iq, ik, iw, cl = _in_proj(
        x, sc1, sh1, w_in[l], shift_mu[l], rwkv_w0[l], rwkv_w2[l], rwkv_a0[l], rwkv_a2[l], rwkv_g2[l], rwkv_k_k[l],
        rwkv_k_a[l], mla_q_norm[l], mla_w_q_up[l], mla_w_uk[l], mla_kv_norm[l], idx_w_q[l], idx_ln_g[l], idx_ln_b[l])
    o_rwkv = _rwkv_scan(r, lw, k, v, kk, a, g, rwkv_r_k[l], rwkv_ln_w[l], rwkv_ln_b[l])
    o_dsa = _dsa_attn(iq, iw, qabs, ik, cl, mla_w_uv[l])
    x1, u2, route, counts = _mix_out(o_rwkv, o_dsa, x, gt1, sc2, sh2, w_out[l], ln1_g[l], ln1_b[l], w_router[l],
                                     b_router[l])
    return _moe_and_norm(x1, u2, route, counts, gt2, w_gu[l], b_gu[l], w_dn[l], b_dn[l], ln2_g[l], ln2_b[l])
```

```python
import functools
import math

import jax
import jax.numpy as jnp
import numpy as np
from jax import lax
from jax.experimental import pallas as pl
from jax.experimental.pallas import tpu as pltpu

F32 = jnp.float32
BF16 = jnp.bfloat16
I32 = jnp.int32

RWKV_HEAD = 64
N_RWKV_HEADS = 8
D_RWKV = RWKV_HEAD * N_RWKV_HEADS
RANK_DECAY = 64
RANK_A = 64
RANK_GATE = 128
RWKV_GN_EPS = 64e-5
ATT_HEAD = 64
N_ATT_HEADS = 8
D_ATT = ATT_HEAD * N_ATT_HEADS
Q_LORA = 256
KV_LORA = 128
IDX_HEADS = 8
IDX_DIM = 64
TOPK_MAX = 256
N_EXPERTS = 32
TOP_K_EXPERTS = 4
SWIGLU_LIMIT = 7.0
SWIGLU_ALPHA = 1.702
NEG_BIG = -1e30
LOG2E = 1.4426950408889634
INT_MIN = -(2 ** 31)

LANES = 128
SUBLANES = 8
VMEM_LIMIT = 48 * 1024 * 1024
VMEM_LIMIT_EXPERTS = 58 * 1024 * 1024

TM_PROJ = 256
L_CHUNK = 64
CHUNKS_PER_STEP = 4
TQ = 256
KEY_CHUNK = 128
SUM_ROWS = 64
TM_ROUTE = 256
BM_EXPERT = 512

_SEG = (("r", 512, 512), ("k", 512, 512), ("v", 512, 512), ("w", 64, 128), ("a", 64, 128), ("g", 128, 128),
        ("q", 256, 256), ("kv", 128, 128), ("ik", 64, 128), ("iw", 8, 128))
N_SHIFT_P = 512 * 3 + 128 * 3
N_IN_P = sum(s[2] for s in _SEG)


def _cparams(sem):
    return pltpu.CompilerParams(dimension_semantics=sem, vmem_limit_bytes=VMEM_LIMIT)


def _bdot(a, b):
    return jnp.dot(a.astype(BF16), b.astype(BF16), preferred_element_type=F32)


def _bdot_nt(a, b):
    return lax.dot_general(a.astype(BF16), b.astype(BF16), (((1,), (1,)), ((), ())), preferred_element_type=F32)


def _bdot_tn(a, b):
    return lax.dot_general(a.astype(BF16), b.astype(BF16), (((0,), (0,)), ((), ())), preferred_element_type=F32)


def _split2(a):
    hi = a.astype(BF16)
    lo = (a - hi.astype(F32)).astype(BF16)
    return hi, lo


def _split3(a):
    hi = a.astype(BF16)
    r1 = a - hi.astype(F32)
    mid = r1.astype(BF16)
    lo = (r1 - mid.astype(F32)).astype(BF16)
    return hi, mid, lo


def _dot3(a, b, dims=(((1,), (0,)), ((), ()))):
    ah, al = _split2(a)
    bh, bl = _split2(b)
    d = functools.partial(lax.dot_general, dimension_numbers=dims, preferred_element_type=F32)
    return d(ah, bh) + (d(ah, bl) + d(al, bh))


def _dot_exact_rhs(a, b_exact, nsplit=3):
    parts = _split3(a) if nsplit == 3 else _split2(a)
    acc = None
    for p in parts[::-1]:
        t = jnp.dot(p, b_exact, preferred_element_type=F32)
        acc = t if acc is None else acc + t
    return acc


def _dot_exact_lhs(a_exact, b, nsplit=3):
    parts = _split3(b) if nsplit == 3 else _split2(b)
    acc = None
    for p in parts[::-1]:
        t = jnp.dot(a_exact, p, preferred_element_type=F32)
        acc = t if acc is None else acc + t
    return acc


def _sigmoid(x):
    return 1.0 / (1.0 + jnp.exp(-x))


def _softplus(x):
    return jnp.maximum(x, 0.0) + jnp.log(1.0 + jnp.exp(-jnp.abs(x)))


def _ada_kernel(c_ref, w_ref, b_ref, o_ref):
    c = c_ref[...]
    o_ref[...] = _dot3(c * _sigmoid(c), w_ref[...]) + b_ref[...]


def _ada_mod(c, w_ada, b_ada):
    B, D = c.shape
    N = w_ada.shape[1]
    tn = 1024
    return pl.pallas_call(
        _ada_kernel,
        grid=(N // tn,),
        in_specs=[pl.BlockSpec((B, D), lambda j: (0, 0)),
                  pl.BlockSpec((D, tn), lambda j: (0, j)),
                  pl.BlockSpec((1, tn), lambda j: (0, j))],
        out_specs=pl.BlockSpec((B, tn), lambda j: (0, j)),
        out_shape=jax.ShapeDtypeStruct((B, N), F32),
        compiler_params=_cparams(("arbitrary",)),
        name="ada_mod",
    )(c, w_ada, b_ada.reshape(1, N))


def _in_proj_kernel(x_ref, sc_ref, sh_ref, win_ref, mu_ref, w0_ref, w2_ref, a0_ref, a2_ref, g2_ref, kk_ref, ka_ref,
                    ones_ref, qn_ref, wq_ref, wuk_ref, kvn_ref, wiq_ref, ig_ref, ib_ref,
                    r_o, lw_o, k_o, v_o, kkn_o, a_o, g_o, qabs_o, iq_o, ik_o, iw_o, cl_o, clt_o, carry):
    i = pl.program_id(1)
    tm = x_ref.shape[0]

    @pl.when(i == 0)
    def _():
        carry[...] = jnp.zeros_like(carry)

    u = x_ref[...] * (1.0 + sc_ref[...]) + sh_ref[...]
    p = _bdot(u, win_ref[...])
    ps = p[:, :N_SHIFT_P]
    rows = lax.broadcasted_iota(I32, (tm, 1), 0)
    prev = jnp.where(rows == 0, carry[0:1, :], pltpu.roll(ps, 1, 0))
    carry[0:1, :] = ps[tm - 1:tm, :]
    ps = ps + mu_ref[...] * (prev - ps)

    pr, pk, pv = ps[:, 0:512], ps[:, 512:1024], ps[:, 1024:1536]
    pw, pa, pg = ps[:, 1536:1664], ps[:, 1664:1792], ps[:, 1792:1920]
    w_log = -_softplus(-(w0_ref[...] + _dot3(jnp.tanh(pw), w2_ref[...]))) - 0.5
    lw_o[...] = -jnp.exp(w_log)
    a = _sigmoid(a0_ref[...] + _dot3(pa, a2_ref[...]))
    g_o[...] = _dot3(_sigmoid(pg), g2_ref[...])
    kk = pk * kk_ref[...]
    ssq = _dot_exact_rhs(kk * kk, ones_ref[...], nsplit=2)
    kkn_o[...] = kk / jnp.maximum(jnp.sqrt(ssq), 1e-12)
    k_o[...] = pk * (1.0 + (a - 1.0) * ka_ref[...])
    r_o[...] = pr
    v_o[...] = pv
    a_o[...] = a

    pq, pkv = p[:, 1920:2176], p[:, 2176:2304]
    pik, piw = p[:, 2304:2432], p[:, 2432:2560]
    q_lat = pq * lax.rsqrt(jnp.mean(pq * pq, axis=-1, keepdims=True) + 1e-6) * qn_ref[...]
    q = _bdot(q_lat, wq_ref[...])
    qabs_o[...] = (_bdot(q, wuk_ref[...]) * (ATT_HEAD ** -0.5 * LOG2E)).astype(BF16)
    c_lat = pkv * lax.rsqrt(jnp.mean(pkv * pkv, axis=-1, keepdims=True) + 1e-6) * kvn_ref[...]
    spos = i * tm + rows
    s_hi = (spos >> 7).astype(F32)
    s_lo = (spos & (LANES - 1)).astype(F32)
    lane_t = lax.broadcasted_iota(I32, (1, LANES), 1)
    extra = jnp.where(lane_t == 0, 1.0, jnp.where((lane_t == 1) | (lane_t == 2), s_hi,
                                                  jnp.where((lane_t == 3) | (lane_t == 4), s_lo, 0.0)))
    c_aug = jnp.concatenate([c_lat, extra], axis=-1)
    cl_o[...] = c_aug.astype(BF16)
    clt_o[...] = jnp.transpose(c_aug).astype(BF16)
    iq_o[...] = (_bdot(q_lat, wiq_ref[...]) * (IDX_DIM ** -0.5)).astype(BF16)
    lane = lax.broadcasted_iota(I32, (1, LANES), 1)
    valid = lane < IDX_DIM
    mu = jnp.sum(pik, axis=-1, keepdims=True) * (1.0 / IDX_DIM)
    dlt = jnp.where(valid, pik - mu, 0.0)
    var = jnp.sum(dlt * dlt, axis=-1, keepdims=True) * (1.0 / IDX_DIM)
    ik_o[...] = jnp.where(valid, dlt * lax.rsqrt(var + 1e-5) * ig_ref[...] + ib_ref[...], 0.0).astype(BF16)
    iw_o[...] = piw * (IDX_HEADS ** -0.5)


def _pad_cols(w, widths):
    parts, o = [], 0
    for true, padded in widths:
        seg = w[..., o:o + true]
        if padded > true:
            seg = jnp.pad(seg, [(0, 0)] * (w.ndim - 1) + [(0, padded - true)])
        parts.append(seg)
        o += true
    return jnp.concatenate(parts, axis=-1)


def _pad_rows(w, rows):
    return jnp.pad(w, ((0, rows - w.shape[0]), (0, 0)))


def _block_diag(blocks):
    H, a, b = blocks.shape
    eye = jnp.eye(H, dtype=blocks.dtype)
    return (eye[:, None, :, None] * blocks[:, :, None, :]).reshape(H * a, H * b)


def _head_ones(n, head):
    idx = np.arange(n) // head
    return jnp.asarray(idx[:, None] == idx[None, :], BF16)


def _in_proj(x, sc1, sh1, w_in, shift_mu, w0, w2, a0, a2, g2, k_k, k_a, q_norm, w_q_up, w_uk, kv_norm, idx_w_q,
             idx_ln_g, idx_ln_b):
    B, S, D = x.shape
    tm = min(TM_PROJ, S)
    widths = tuple((s[1], s[2]) for s in _SEG)
    win_p = _pad_cols(w_in, widths).astype(BF16)
    mu_p = _pad_cols(shift_mu.reshape(1, -1), widths[:6])
    w2_p = _pad_rows(w2, LANES)
    a2_p = _pad_rows(a2, LANES)
    wuk_bd = _block_diag(w_uk).astype(BF16)
    wiq_p = _pad_cols(idx_w_q, ((IDX_DIM, LANES),) * IDX_HEADS).astype(BF16)
    ig_p = _pad_cols(idx_ln_g.reshape(1, -1), ((IDX_DIM, LANES),))
    ib_p = _pad_cols(idx_ln_b.reshape(1, -1), ((IDX_DIM, LANES),))
    row = lambda v: v.reshape(1, -1)
    tok = lambda n: pl.BlockSpec((None, tm, n), lambda b, i: (b, i, 0))
    mod = pl.BlockSpec((None, 1, D), lambda b, i: (b, 0, 0))
    full = lambda a: pl.BlockSpec(a.shape, lambda b, i: (0,) * a.ndim)
    consts = [win_p, mu_p, row(w0), w2_p, row(a0), a2_p, g2, row(k_k), row(k_a), _head_ones(D_RWKV, RWKV_HEAD),
              row(q_norm), w_q_up.astype(BF16), wuk_bd, row(kv_norm), wiq_p, ig_p, ib_p]
    outs = [(D_RWKV, F32)] * 7 + [(N_ATT_HEADS * KV_LORA, BF16), (IDX_HEADS * LANES, BF16), (LANES, BF16),
                                  (LANES, F32), (KV_LORA + LANES, BF16)]
    return pl.pallas_call(
        _in_proj_kernel,
        grid=(B, S // tm),
        in_specs=[tok(D), mod, mod] + [full(a) for a in consts],
        out_specs=[tok(n) for n, _ in outs] + [pl.BlockSpec((None, KV_LORA + LANES, tm), lambda b, i: (b, 0, i))],
        out_shape=[jax.ShapeDtypeStruct((B, S, n), dt) for n, dt in outs]
        + [jax.ShapeDtypeStruct((B, KV_LORA + LANES, S), BF16)],
        scratch_shapes=[pltpu.VMEM((8, N_SHIFT_P), F32)],
        compiler_params=_cparams(("arbitrary", "arbitrary")),
        name="in_proj",
    )(x, sc1, sh1, *consts)


def _rwkv_kernel(r_ref, lw_ref, k_ref, v_ref, kk_ref, a_ref, g_ref, rk_ref, lnw_ref, lnb_ref, tri_ref, ones_ref,
                 o_ref, state, *, L):
    c = pl.program_id(1)
    nc = r_ref.shape[0] // L

    @pl.when(c == 0)
    def _():
        state[...] = jnp.zeros_like(state)

    r, lw, k, v, kk, a = r_ref[...], lw_ref[...], k_ref[...], v_ref[...], kk_ref[...], a_ref[...]
    cum = _dot_exact_lhs(tri_ref[...], lw)
    rows = [slice(ci * L, (ci + 1) * L) for ci in range(nc)]
    last = [cum[(ci + 1) * L - 1:(ci + 1) * L, :] for ci in range(nc)]
    cum_last = jnp.concatenate([jnp.broadcast_to(z, (L, z.shape[1])) for z in last], axis=0)
    w_incl = jnp.exp(cum)
    w_inv = jnp.exp(-cum)
    w_rel = jnp.exp(cum_last - cum)
    w_last = [jnp.exp(z) for z in last]
    bvec = kk * a
    at = -kk * jnp.exp(cum - lw)
    rt = r * w_incl
    bt = bvec * w_inv
    kt = k * w_inv
    bh = bvec * w_rel
    kh = k * w_rel
    ti = lax.broadcasted_iota(I32, (L, L), 0)
    tj = lax.broadcasted_iota(I32, (L, L), 1)
    strict = tj < ti
    incl = tj <= ti
    eye = lax.broadcasted_iota(I32, (RWKV_HEAD, RWKV_HEAD), 0) == lax.broadcasted_iota(I32, (RWKV_HEAD, RWKV_HEAD), 1)
    NT = (((1,), (1,)), ((), ()))
    TN = (((0,), (0,)), ((), ()))
    heads = range(N_RWKV_HEADS)
    sls = [slice(h * RWKV_HEAD, (h + 1) * RWKV_HEAD) for h in heads]
    units = [(ci, h) for ci in range(nc) for h in heads]
    idx = range(len(units))
    mm = lambda x, y, dims=(((1,), (0,)), ((), ())): lax.dot_general(
        x.astype(BF16), y.astype(BF16), dims, preferred_element_type=F32)
    at_b, rt_b, bt_b, kt_b, bh_b, kh_b, v_b = [z.astype(BF16) for z in (at, rt, bt, kt, bh, kh, v)]
    cut = lambda z, u: z[rows[u[0]], sls[u[1]]]
    vh = [cut(v_b, u) for u in units]
    ath = [cut(at_b, u) for u in units]
    ar = [jnp.concatenate([cut(at_b, u), cut(rt_b, u)], axis=0) for u in units]
    g_b = [mm(ar[i], cut(bt_b, units[i]), NT) for i in idx]
    g_k = [mm(ar[i], cut(kt_b, units[i]), NT) for i in idx]
    n_ab = [jnp.where(strict, g_b[i][:L], 0.0) for i in idx]
    a_ak = [jnp.where(strict, g_k[i][:L], 0.0) for i in idx]
    a_rb = [jnp.where(incl, g_b[i][L:], 0.0).astype(BF16) for i in idx]
    a_rk = [jnp.where(incl, g_k[i][L:], 0.0) for i in idx]
    akv = [mm(a_ak[i], vh[i]) for i in idx]
    eye_l = jnp.where(ti == tj, 1.0, 0.0)
    tinv = [eye_l + n_ab[i] for i in idx]
    pw = n_ab
    for _ in range(int(math.log2(L)) - 1):
        pw = [mm(pw[i], pw[i]) for i in idx]
        tinv = [tinv[i] + mm(pw[i], tinv[i]) for i in idx]
    tinv = [t.astype(BF16) for t in tinv]
    a_t = [mm(tinv[i], ath[i]).astype(BF16) for i in idx]
    y = [mm(tinv[i], akv[i]).astype(BF16) for i in idx]
    m_c = [jnp.where(eye, w_last[units[i][0]][:, sls[units[i][1]]], 0.0) + mm(a_t[i], cut(bh_b, units[i]), TN)
           for i in idx]
    c_c = [mm(y[i], cut(bh_b, units[i]), TN) + mm(vh[i], cut(kh_b, units[i]), TN) for i in idx]
    q_c = [cut(rt, units[i]) + mm(a_rb[i], a_t[i]) for i in idx]
    o_loc = [mm(a_rb[i], y[i]) + mm(a_rk[i], vh[i]) for i in idx]
    s = [state[h] for h in heads]
    for ci in range(nc):
        for h in heads:
            i = ci * N_RWKV_HEADS + h
            o = o_loc[i] + _dot3(q_c[i], s[h], NT)
            s[h] = _dot3(s[h], m_c[i]) + c_c[i]
            mu = jnp.mean(o, axis=-1, keepdims=True)
            d = o - mu
            var = jnp.mean(d * d, axis=-1, keepdims=True)
            o_ref[rows[ci], sls[h]] = d * lax.rsqrt(var + RWKV_GN_EPS)
    for h in heads:
        state[h] = s[h]
    bonus = _dot_exact_rhs(r * k * rk_ref[...], ones_ref[...], nsplit=3) * v
    o_ref[...] = (o_ref[...] * lnw_ref[...] + lnb_ref[...] + bonus) * g_ref[...]


def _rwkv_scan(r, lw, k, v, kk, a, g, r_k, ln_w, ln_b):
    B, S, DR = r.shape
    L = min(L_CHUNK, S)
    lb = min(L * CHUNKS_PER_STEP, S)
    tri = jnp.asarray(np.kron(np.eye(lb // L), np.tril(np.ones((L, L)))), BF16)
    row = lambda z: z.reshape(1, -1)
    tok = pl.BlockSpec((None, lb, DR), lambda b, c: (b, c, 0))
    full = lambda z: pl.BlockSpec(z.shape, lambda b, c: (0,) * z.ndim)
    consts = [row(r_k), row(ln_w), row(ln_b), tri, _head_ones(DR, RWKV_HEAD)]
    return pl.pallas_call(
        functools.partial(_rwkv_kernel, L=L),
        grid=(B, S // lb),
        in_specs=[tok] * 7 + [full(z) for z in consts],
        out_specs=tok,
        out_shape=jax.ShapeDtypeStruct((B, S, DR), F32),
        scratch_shapes=[pltpu.VMEM((N_RWKV_HEADS, RWKV_HEAD, RWKV_HEAD), F32)],
        compiler_params=_cparams(("arbitrary", "arbitrary")),
        name="rwkv_scan",
    )(r, lw, k, v, kk, a, g, *consts)


def _alibi_cols():
    slope = np.asarray([2.0 ** (-8.0 * (h + 1) / N_ATT_HEADS) * LOG2E for h in range(N_ATT_HEADS)], np.float32)
    c_hi = slope.astype(BF16).astype(np.float32)
    c_lo = (slope - c_hi).astype(BF16).astype(np.float32)
    t = np.zeros((N_ATT_HEADS, LANES), np.float32)
    t[:, 1], t[:, 2], t[:, 3], t[:, 4] = LANES * c_hi, LANES * c_lo, c_hi, c_lo
    return jnp.asarray(t.astype(BF16))


def _dsa_kernel(iq_ref, iw_ref, qa_ref, ik_ref, ca_ref, cat_ref, wuv_ref, tril_ref, acol_ref, o_ref, key_ref,
                bias_ref, lg_ref, p_ref, *, topk, q_off, select):
    qi = pl.program_id(1) + q_off
    tq = qa_ref.shape[0]
    sk = ca_ref.shape[0]
    kc = min(KEY_CHUNK, sk)
    chunks = [slice(c * kc, (c + 1) * kc) for c in range(sk // kc)]
    tpos = qi * tq + lax.broadcasted_iota(I32, (1, tq), 1)
    srow = lax.broadcasted_iota(I32, (kc, tq), 0)
    NT = (((1,), (1,)), ((), ()))

    if not select:
        for c, cs in enumerate(chunks):
            bias_ref[cs, :] = jnp.where(srow + c * kc <= tpos, 0.0, NEG_BIG)
    else:
        iw_t = jnp.transpose(iw_ref[...])
        for c, cs in enumerate(chunks):
            ikc = ik_ref[cs, :]
            score = jnp.zeros((kc, tq), F32)
            for h in range(IDX_HEADS):
                s = lax.dot_general(ikc, iq_ref[:, h * LANES:(h + 1) * LANES], NT, preferred_element_type=F32)
                score = score + iw_t[h:h + 1, :] * jnp.maximum(s, 0.0)
            bits = pltpu.bitcast(score + 0.0, I32)
            key = bits ^ ((bits >> 31) & 0x7FFFFFFF)
            key_ref[cs, :] = jnp.where(srow + c * kc <= tpos, key, INT_MIN)
        kcount = jnp.minimum(topk, tpos + 1).astype(F32)

        def sum_keys(x):
            part = jnp.sum(x.reshape(x.shape[0] // SUM_ROWS, SUM_ROWS, x.shape[1]), axis=0)
            return jnp.sum(part, axis=0, keepdims=True)

        def count_ge(cand):
            return sum_keys(jnp.where(key_ref[...] >= cand, 1.0, 0.0))

        thr = jnp.where(count_ge(jnp.zeros((1, tq), I32)) >= kcount, 0, INT_MIN).astype(I32)

        def bit_step(i, thr):
            cand = thr | (1 << (30 - i))
            return jnp.where(count_ge(cand) >= kcount, cand, thr)

        thr = lax.fori_loop(0, 31, bit_step, thr)

        need = kcount - sum_keys(jnp.where(key_ref[...] > thr, 1.0, 0.0))
        carry = jnp.zeros((1, tq), F32)
        for j in range(sk // LANES):
            sl = slice(j * LANES, (j + 1) * LANES)
            key = key_ref[sl, :]
            eq = key == thr
            e = jnp.where(eq, 1.0, 0.0)
            before = jnp.dot(tril_ref[...], e.astype(BF16), preferred_element_type=F32) + carry
            sel = (key > thr) | (eq & (before < need))
            bias_ref[sl, :] = jnp.where(sel, 0.0, NEG_BIG)
            carry = carry + jnp.sum(e, axis=0, keepdims=True)

    outs = []
    for h in range(N_ATT_HEADS):
        q_aug = jnp.concatenate([qa_ref[:, h * KV_LORA:(h + 1) * KV_LORA],
                                 jnp.broadcast_to(acol_ref[h:h + 1, :], (tq, LANES))], axis=-1)
        lg_all = lax.dot_general(ca_ref[...], q_aug, NT, preferred_element_type=F32)
        m = jnp.full((1, tq), -jnp.inf, F32)
        for cs in chunks:
            lg = lg_all[cs, :] + bias_ref[cs, :]
            lg_ref[cs, :] = lg
            m = jnp.maximum(m, jnp.max(lg, axis=0, keepdims=True))
        for cs in chunks:
            p_ref[cs, :] = jnp.exp2(lg_ref[cs, :] - m).astype(BF16)
        pv = jnp.dot(cat_ref[...], p_ref[...], preferred_element_type=F32)
        outs.append((pv[:KV_LORA, :] / pv[KV_LORA:KV_LORA + 1, :]).astype(BF16))
    o_lat_t = jnp.concatenate(outs, axis=0)
    o_ref[...] = lax.dot_general(o_lat_t, wuv_ref[...], (((0,), (0,)), ((), ())), preferred_element_type=F32)


def _dsa_attn(iq, iw, qabs, ik, ca, cat, w_uv):
    B, S, _ = iq.shape
    tq = min(TQ, S)
    topk = min(TOPK_MAX, S // 4)
    wuv_bd = _block_diag(w_uv).astype(BF16)
    tril = jnp.asarray(np.tril(np.ones((LANES, LANES)), -1), BF16)
    acol = _alibi_cols()
    full = lambda z: pl.BlockSpec(z.shape, lambda b, i: (0,) * z.ndim)
    nq = 1
    outs = []
    for q_off in range(0, S // tq, nq):
        sk = (q_off + nq) * tq
        tok = lambda n, q_off=q_off: pl.BlockSpec((None, tq, n), lambda b, i: (b, i + q_off, 0))
        seq = lambda n, sk=sk: pl.BlockSpec((None, sk, n), lambda b, i: (b, 0, 0))
        outs.append(pl.pallas_call(
            functools.partial(_dsa_kernel, topk=topk, q_off=q_off, select=sk > topk),
            grid=(B, nq),
            in_specs=[tok(IDX_HEADS * LANES), tok(LANES), tok(N_ATT_HEADS * KV_LORA), seq(LANES), seq(KV_LORA + LANES),
                      pl.BlockSpec((None, KV_LORA + LANES, sk), lambda b, i: (b, 0, 0)),
                      full(wuv_bd), full(tril), full(acol)],
            out_specs=pl.BlockSpec((None, tq, D_ATT), lambda b, i: (b, i, 0)),
            out_shape=jax.ShapeDtypeStruct((B, nq * tq, D_ATT), F32),
            scratch_shapes=[pltpu.VMEM((sk, tq), I32), pltpu.VMEM((sk, tq), F32), pltpu.VMEM((sk, tq), F32),
                            pltpu.VMEM((sk, tq), BF16)],
            compiler_params=_cparams(("arbitrary", "arbitrary")),
            name=f"dsa_attn_k{sk}",
        )(iq, iw, qabs, ik, ca, cat, wuv_bd, tril, acol))
    return jnp.concatenate(outs, axis=1)


def _layernorm_rows(y, g, b):
    mu = jnp.mean(y, axis=-1, keepdims=True)
    d = y - mu
    var = jnp.mean(d * d, axis=-1, keepdims=True)
    return d * lax.rsqrt(var + 1e-5) * g + b


def _mix_kernel(orw_ref, ods_ref, x_ref, gt_ref, sc_ref, sh_ref, wtop_ref, wbot_ref, g_ref, b_ref, wr_ref, br_ref,
                tril_ref, x1_o, u2_o, route_o, cnt_o, carry, *, alpha):
    first = (pl.program_id(0) == 0) & (pl.program_id(1) == 0)
    tm = x_ref.shape[0]

    @pl.when(first)
    def _():
        carry[...] = jnp.zeros_like(carry)

    mix = _bdot(orw_ref[...], wtop_ref[...]) + _bdot(ods_ref[...], wbot_ref[...])
    x1 = _layernorm_rows(alpha * x_ref[...] + (1.0 + gt_ref[...]) * mix, g_ref[...], b_ref[...])
    x1_o[...] = x1
    u2 = x1 * (1.0 + sc_ref[...]) + sh_ref[...]
    u2_o[...] = u2

    lg = _dot3(u2, wr_ref[...]) + br_ref[...]
    lane = lax.broadcasted_iota(I32, (tm, LANES), 1)
    idxs, vals = [], []
    for _ in range(TOP_K_EXPERTS):
        m = jnp.max(lg, axis=-1, keepdims=True)
        idx = jnp.min(jnp.where(lg == m, lane, LANES), axis=-1, keepdims=True)
        idxs.append(idx)
        vals.append(m)
        lg = jnp.where(lane == idx, -jnp.inf, lg)
    es = [jnp.exp(v - vals[0]) for v in vals]
    den = es[0] + es[1] + es[2] + es[3]
    hot = jnp.zeros((tm, LANES), F32)
    for idx in idxs:
        hot = hot + jnp.where(lane == idx, 1.0, 0.0)
    before = jnp.dot(tril_ref[...], hot.astype(BF16), preferred_element_type=F32) + carry[0:1, :]
    route = jnp.zeros((tm, LANES), F32)
    for k in range(TOP_K_EXPERTS):
        rank = jnp.sum(jnp.where(lane == idxs[k], before, 0.0), axis=-1, keepdims=True)
        route = jnp.where(lane == k, idxs[k].astype(F32), route)
        route = jnp.where(lane == TOP_K_EXPERTS + k, es[k] / den, route)
        route = jnp.where(lane == 2 * TOP_K_EXPERTS + k, rank, route)
    route_o[...] = route
    carry[0:1, :] = carry[0:1, :] + jnp.sum(hot, axis=0, keepdims=True)
    cnt_o[...] = carry[...]


def _mix_out(o_rwkv, o_dsa, x, gt1, sc2, sh2, w_out, ln_g, ln_b, w_router, b_router):
    B, S, D = x.shape
    tm = min(TM_PROJ, S)
    alpha = 2.0 ** 0.25
    wtop = w_out[:D_RWKV].astype(BF16)
    wbot = w_out[D_RWKV:].astype(BF16)
    wr_p = jnp.pad(w_router, ((0, 0), (0, LANES - N_EXPERTS)))
    br_p = jnp.pad(b_router.reshape(1, -1), ((0, 0), (0, LANES - N_EXPERTS)), constant_values=NEG_BIG)
    tril = jnp.asarray(np.tril(np.ones((tm, tm)), -1), BF16)
    row = lambda v: v.reshape(1, -1)
    tok = lambda n: pl.BlockSpec((None, tm, n), lambda b, i: (b, i, 0))
    mod = pl.BlockSpec((None, 1, D), lambda b, i: (b, 0, 0))
    full = lambda a: pl.BlockSpec(a.shape, lambda b, i: (0,) * a.ndim)
    consts = [wtop, wbot, row(ln_g), row(ln_b), wr_p, br_p, tril]
    return pl.pallas_call(
        functools.partial(_mix_kernel, alpha=alpha),
        grid=(B, S // tm),
        in_specs=[tok(D_RWKV), tok(D_ATT), tok(D), mod, mod, mod] + [full(a) for a in consts],
        out_specs=[tok(D), tok(D), tok(LANES), pl.BlockSpec((8, LANES), lambda b, i: (0, 0))],
        out_shape=[jax.ShapeDtypeStruct((B, S, D), F32), jax.ShapeDtypeStruct((B, S, D), F32),
                   jax.ShapeDtypeStruct((B, S, LANES), F32), jax.ShapeDtypeStruct((8, LANES), F32)],
        scratch_shapes=[pltpu.VMEM((8, LANES), F32)],
        compiler_params=_cparams(("arbitrary", "arbitrary")),
        name="mix_out",
    )(o_rwkv, o_dsa, x, gt1, sc2, sh2, *consts)


ISSUE_UNROLL = 8


def _dispatch_kernel(pos_ref, u_ref, xs_in, xs_out, sem):
    del xs_in
    tm = u_ref.shape[0]

    def start(r, carry):
        for k in range(TOP_K_EXPERTS):
            dst = pos_ref[r * TOP_K_EXPERTS + k]
            pltpu.make_async_copy(u_ref.at[pl.ds(r, 1)], xs_out.at[pl.ds(dst, 1)], sem).start(priority=k % 2)
        return carry

    lax.fori_loop(0, tm, start, 0, unroll=ISSUE_UNROLL)
    for k in range(TOP_K_EXPERTS):
        pltpu.make_async_copy(u_ref, xs_out.at[pl.ds(0, tm)], sem).wait()


def _moe_dispatch(u2, pos, n_rows):
    T, D = u2.shape
    tm = min(TM_ROUTE, T)
    return pl.pallas_call(
        _dispatch_kernel,
        grid=(T // tm,),
        in_specs=[pl.BlockSpec((tm * TOP_K_EXPERTS,), lambda i: (i,), memory_space=pltpu.SMEM),
                  pl.BlockSpec((tm, D), lambda i: (i, 0)),
                  pl.BlockSpec(memory_space=pl.ANY)],
        out_specs=pl.BlockSpec(memory_space=pl.ANY),
        out_shape=jax.ShapeDtypeStruct((n_rows, D), F32),
        scratch_shapes=[pltpu.SemaphoreType.DMA(())],
        input_output_aliases={2: 0},
        compiler_params=_cparams(("arbitrary",)),
        name="moe_dispatch",
    )(pos, u2, jnp.zeros((n_rows, D), F32))


GU_GROUP = 2 * LANES


def _deinterleave_perm():
    p = np.zeros((GU_GROUP, GU_GROUP), np.float32)
    l = np.arange(LANES)
    p[2 * l, l] = 1.0
    p[2 * l + 1, LANES + l] = 1.0
    return jnp.asarray(p, BF16)


def _expert_kernel(be_ref, nb_ref, xs_ref, wgu_ref, bgu_ref, wd_ref, bd_ref, perm_ref, ys_ref, wp, wdb):
    i = pl.program_id(0)
    used = i < nb_ref[0]
    new_expert = (i == 0) | (be_ref[i] != be_ref[jnp.maximum(i - 1, 0)])
    n_groups = wgu_ref.shape[1] // GU_GROUP

    @pl.when(used & new_expert)
    def _():
        for j in range(n_groups):
            sl = slice(j * GU_GROUP, (j + 1) * GU_GROUP)
            wp[:, sl] = jnp.dot(wgu_ref[:, sl].astype(BF16), perm_ref[...], preferred_element_type=F32).astype(BF16)
        wdb[...] = wd_ref[...].astype(BF16)

    @pl.when(used)
    def _():
        xb = xs_ref[...].astype(BF16)
        gu = jnp.dot(xb, wp[...], preferred_element_type=F32) + bgu_ref[...]
        hs = []
        for j in range(n_groups):
            gate = jnp.minimum(gu[:, j * GU_GROUP:j * GU_GROUP + LANES], SWIGLU_LIMIT)
            up = jnp.clip(gu[:, j * GU_GROUP + LANES:(j + 1) * GU_GROUP], -SWIGLU_LIMIT, SWIGLU_LIMIT)
            hs.append(((up + 1.0) * (gate * _sigmoid(gate * SWIGLU_ALPHA))).astype(BF16))
        h = jnp.concatenate(hs, axis=-1)
        ys_ref[...] = jnp.dot(h, wdb[...], preferred_element_type=F32) + bd_ref[...]

    @pl.when(jnp.logical_not(used))
    def _():
        ys_ref[...] = jnp.zeros_like(ys_ref)


def _moe_experts(xs, block_e, n_used, w_gu, b_gu_p, w_dn, b_dn):
    n_rows, D = xs.shape
    E, _, F2 = w_gu.shape
    bm = BM_EXPERT
    n_blocks = n_rows // bm
    perm = _deinterleave_perm()
    wspec = lambda shp: pl.BlockSpec((None,) + shp, lambda i, be, nb: (be[i], 0, 0))
    return pl.pallas_call(
        _expert_kernel,
        grid_spec=pltpu.PrefetchScalarGridSpec(
            num_scalar_prefetch=2,
            grid=(n_blocks,),
            in_specs=[pl.BlockSpec((bm, D), lambda i, be, nb: (jnp.minimum(i, nb[0] - 1), 0)),
                      wspec((D, F2)), wspec((1, F2)), wspec((F2 // 2, D)), wspec((1, D)),
                      pl.BlockSpec(perm.shape, lambda i, be, nb: (0, 0))],
            out_specs=pl.BlockSpec((bm, D), lambda i, be, nb: (i, 0)),
            scratch_shapes=[pltpu.VMEM((D, F2), BF16), pltpu.VMEM((F2 // 2, D), BF16)],
        ),
        out_shape=jax.ShapeDtypeStruct((n_rows, D), F32),
        compiler_params=pltpu.CompilerParams(dimension_semantics=("arbitrary",), vmem_limit_bytes=VMEM_LIMIT_EXPERTS),
        name="moe_experts",
    )(block_e, n_used, xs, w_gu, b_gu_p, w_dn, b_dn, perm)


def _combine_kernel(pos_ref, posn_ref, ys_ref, x1_ref, route_ref, gt_ref, g_ref, b_ref, o_ref, buf, sem, *, alpha):
    i = pl.program_id(0)
    n = pl.num_programs(0)
    tm = x1_ref.shape[0]
    slot = i % 2

    def gather(p_ref, s):
        def start(r, carry):
            for k in range(TOP_K_EXPERTS):
                src = p_ref[r * TOP_K_EXPERTS + k]
                pltpu.make_async_copy(ys_ref.at[pl.ds(src, 1)], buf.at[s, k, pl.ds(r, 1)],
                                      sem.at[s]).start(priority=k % 2)
            return carry
        lax.fori_loop(0, tm, start, 0, unroll=ISSUE_UNROLL)

    @pl.when(i == 0)
    def _():
        gather(pos_ref, 0)

    @pl.when(i + 1 < n)
    def _():
        gather(posn_ref, 1 - slot)

    for k in range(TOP_K_EXPERTS):
        pltpu.make_async_copy(ys_ref.at[pl.ds(0, tm)], buf.at[slot, k], sem.at[slot]).wait()

    route = route_ref[...]
    ffn = jnp.zeros(x1_ref.shape, F32)
    for k in range(TOP_K_EXPERTS):
        ffn = ffn + route[:, TOP_K_EXPERTS + k:TOP_K_EXPERTS + k + 1] * buf[slot, k]
    o_ref[...] = _layernorm_rows(alpha * x1_ref[...] + (1.0 + gt_ref[...]) * ffn, g_ref[...], b_ref[...])


def _moe_combine(ys, pos, x1, route, gt2, ln_g, ln_b, tiles_per_batch):
    T, D = x1.shape
    tm = min(TM_ROUTE, T)
    row = lambda v: v.reshape(1, -1)
    n_tiles = T // tm
    return pl.pallas_call(
        functools.partial(_combine_kernel, alpha=2.0 ** 0.25),
        grid=(n_tiles,),
        in_specs=[pl.BlockSpec((tm * TOP_K_EXPERTS,), lambda i: (i,), memory_space=pltpu.SMEM),
                  pl.BlockSpec((tm * TOP_K_EXPERTS,), lambda i: (jnp.minimum(i + 1, n_tiles - 1),),
                               memory_space=pltpu.SMEM),
                  pl.BlockSpec(memory_space=pl.ANY),
                  pl.BlockSpec((tm, D), lambda i: (i, 0)),
                  pl.BlockSpec((tm, LANES), lambda i: (i, 0)),
                  pl.BlockSpec((None, 1, D), lambda i: (i // tiles_per_batch, 0, 0)),
                  pl.BlockSpec((1, D), lambda i: (0, 0)),
                  pl.BlockSpec((1, D), lambda i: (0, 0))],
        out_specs=pl.BlockSpec((tm, D), lambda i: (i, 0)),
        out_shape=jax.ShapeDtypeStruct((T, D), F32),
        scratch_shapes=[pltpu.VMEM((2, TOP_K_EXPERTS, tm, D), F32), pltpu.SemaphoreType.DMA((2,))],
        compiler_params=_cparams(("arbitrary",)),
        name="moe_combine",
    )(pos, pos, ys, x1, route, gt2, row(ln_g), row(ln_b))


def _moe_and_norm(x1, u2, route, counts, gt2, w_gu, b_gu, w_dn, b_dn, ln_g, ln_b):
    B, S, D = x1.shape
    T = B * S
    bm = BM_EXPERT
    n_blocks = -(-T * TOP_K_EXPERTS // bm) + N_EXPERTS
    cnt = counts[0, :N_EXPERTS].astype(I32)
    padded = (cnt + bm - 1) // bm * bm
    pad_ends = jnp.cumsum(padded)
    pad_starts = pad_ends - padded
    route2 = route.reshape(T, LANES)
    e_idx = route2[:, :TOP_K_EXPERTS].astype(I32)
    rank = route2[:, 2 * TOP_K_EXPERTS:3 * TOP_K_EXPERTS].astype(I32)
    pos = (pad_starts[e_idx] + rank).reshape(-1)
    blk_row = jnp.arange(n_blocks, dtype=I32) * bm
    block_e = jnp.minimum(jnp.sum((blk_row[:, None] >= pad_ends[None, :]).astype(I32), axis=1), N_EXPERTS - 1)
    n_used = (pad_ends[-1:] // bm).astype(I32)
    E, F2 = b_gu.shape
    b_gu_p = b_gu.reshape(E, F2 // GU_GROUP, LANES, 2).transpose(0, 1, 3, 2).reshape(E, 1, F2)
    xs = _moe_dispatch(u2.reshape(T, D), pos, n_blocks * bm)
    ys = _moe_experts(xs, block_e, n_used, w_gu, b_gu_p, w_dn, b_dn[:, None, :])
    out = _moe_combine(ys, pos, x1.reshape(T, D), route2, gt2, ln_g, ln_b, S // min(TM_ROUTE, T))
    return out.reshape(B, S, D)


def kernel(x, c, w_ada, b_ada, w_in, shift_mu, rwkv_w0, rwkv_w2, rwkv_a0, rwkv_a2, rwkv_g2, rwkv_k_k, rwkv_k_a, rwkv_r_k, rwkv_ln_w, rwkv_ln_b, mla_q_norm, mla_w_q_up, mla_kv_norm, mla_w_uk, mla_w_uv, idx_w_q, idx_ln_g, idx_ln_b, w_out, ln1_g, ln1_b, w_router, b_router, w_gu, b_gu, w_dn, b_dn, ln2_g, ln2_b):
    depth = w_ada.shape[0]
    assert depth == 1, "DeepNorm constants below are for a single layer"
    l = 0
    mod = _ada_mod(c, w_ada[l], b_ada[l])
    sh1, sc1, gt1, sh2, sc2, gt2 = [m[:, None, :] for m in jnp.split(mod, 6, axis=-1)]
    r, lw, k, v, kk, a, g, qabs, iq, ik, iw, cl, clt = _in_proj(
        x, sc1, sh1, w_in[l], shift_mu[l], rwkv_w0[l], rwkv_w2[l], rwkv_a0[l], rwkv_a2[l], rwkv_g2[l], rwkv_k_k[l],
        rwkv_k_a[l], mla_q_norm[l], mla_w_q_up[l], mla_w_uk[l], mla_kv_norm[l], idx_w_q[l], idx_ln_g[l], idx_ln_b[l])
    o_rwkv = _rwkv_scan(r, lw, k, v, kk, a, g, rwkv_r_k[l], rwkv_ln_w[l], rwkv_ln_b[l])
    o_dsa = _dsa_attn(iq, iw, qabs, ik, cl, clt, mla_w_uv[l])
    x1, u2, route, counts = _mix_out(o_rwkv, o_dsa, x, gt1, sc2, sh2, w_out[l], ln1_g[l], ln1_b[l], w_router[l],
                                     b_router[l])
    return _moe_and_norm(x1, u2, route, counts, gt2, w_gu[l], b_gu[l], w_dn[l], b_dn[l], ln2_g[l], ln2_b[l])
```

```python
import functools
import math

import jax
import jax.numpy as jnp
import numpy as np
from jax import lax
from jax.experimental import pallas as pl
from jax.experimental.pallas import tpu as pltpu

F32 = jnp.float32
BF16 = jnp.bfloat16
I32 = jnp.int32

RWKV_HEAD = 64
N_RWKV_HEADS = 8
D_RWKV = RWKV_HEAD * N_RWKV_HEADS
RANK_DECAY = 64
RANK_A = 64
RANK_GATE = 128
RWKV_GN_EPS = 64e-5
ATT_HEAD = 64
N_ATT_HEADS = 8
D_ATT = ATT_HEAD * N_ATT_HEADS
Q_LORA = 256
KV_LORA = 128
IDX_HEADS = 8
IDX_DIM = 64
TOPK_MAX = 256
N_EXPERTS = 32
TOP_K_EXPERTS = 4
SWIGLU_LIMIT = 7.0
SWIGLU_ALPHA = 1.702
NEG_BIG = -1e30
LOG2E = 1.4426950408889634
INT_MIN = -(2 ** 31)

LANES = 128
SUBLANES = 8
VMEM_LIMIT = 48 * 1024 * 1024
VMEM_LIMIT_EXPERTS = 58 * 1024 * 1024

TM_PROJ = 256
L_CHUNK = 64
CHUNKS_PER_STEP = 4
TQ = 256
KEY_CHUNK = 128
SUM_ROWS = 64
TM_ROUTE = 256
BM_EXPERT = 512

_SEG = (("r", 512, 512), ("k", 512, 512), ("v", 512, 512), ("w", 64, 128), ("a", 64, 128), ("g", 128, 128),
        ("q", 256, 256), ("kv", 128, 128), ("ik", 64, 128), ("iw", 8, 128))
N_SHIFT_P = 512 * 3 + 128 * 3
N_IN_P = sum(s[2] for s in _SEG)


def _cparams(sem):
    return pltpu.CompilerParams(dimension_semantics=sem, vmem_limit_bytes=VMEM_LIMIT)


def _bdot(a, b):
    return jnp.dot(a.astype(BF16), b.astype(BF16), preferred_element_type=F32)


def _bdot_nt(a, b):
    return lax.dot_general(a.astype(BF16), b.astype(BF16), (((1,), (1,)), ((), ())), preferred_element_type=F32)


def _bdot_tn(a, b):
    return lax.dot_general(a.astype(BF16), b.astype(BF16), (((0,), (0,)), ((), ())), preferred_element_type=F32)


def _split2(a):
    hi = a.astype(BF16)
    lo = (a - hi.astype(F32)).astype(BF16)
    return hi, lo


def _split3(a):
    hi = a.astype(BF16)
    r1 = a - hi.astype(F32)
    mid = r1.astype(BF16)
    lo = (r1 - mid.astype(F32)).astype(BF16)
    return hi, mid, lo


def _dot3(a, b, dims=(((1,), (0,)), ((), ()))):
    ah, al = _split2(a)
    bh, bl = _split2(b)
    d = functools.partial(lax.dot_general, dimension_numbers=dims, preferred_element_type=F32)
    return d(ah, bh) + (d(ah, bl) + d(al, bh))


def _dot_exact_rhs(a, b_exact, nsplit=3):
    parts = _split3(a) if nsplit == 3 else _split2(a)
    acc = None
    for p in parts[::-1]:
        t = jnp.dot(p, b_exact, preferred_element_type=F32)
        acc = t if acc is None else acc + t
    return acc


def _dot_exact_lhs(a_exact, b, nsplit=3):
    parts = _split3(b) if nsplit == 3 else _split2(b)
    acc = None
    for p in parts[::-1]:
        t = jnp.dot(a_exact, p, preferred_element_type=F32)
        acc = t if acc is None else acc + t
    return acc


def _sigmoid(x):
    return 1.0 / (1.0 + jnp.exp(-x))


def _softplus(x):
    return jnp.maximum(x, 0.0) + jnp.log(1.0 + jnp.exp(-jnp.abs(x)))


def _ada_kernel(c_ref, w_ref, b_ref, o_ref):
    c = c_ref[...]
    o_ref[...] = _dot3(c * _sigmoid(c), w_ref[...]) + b_ref[...]


def _ada_mod(c, w_ada, b_ada):
    B, D = c.shape
    N = w_ada.shape[1]
    tn = 1024
    return pl.pallas_call(
        _ada_kernel,
        grid=(N // tn,),
        in_specs=[pl.BlockSpec((B, D), lambda j: (0, 0)),
                  pl.BlockSpec((D, tn), lambda j: (0, j)),
                  pl.BlockSpec((1, tn), lambda j: (0, j))],
        out_specs=pl.BlockSpec((B, tn), lambda j: (0, j)),
        out_shape=jax.ShapeDtypeStruct((B, N), F32),
        compiler_params=_cparams(("arbitrary",)),
        name="ada_mod",
    )(c, w_ada, b_ada.reshape(1, N))


def _in_proj_kernel(x_ref, sc_ref, sh_ref, win_ref, mu_ref, w0_ref, w2_ref, a0_ref, a2_ref, g2_ref, kk_ref, ka_ref,
                    ones_ref, qn_ref, wq_ref, wuk_ref, kvn_ref, wiq_ref, ig_ref, ib_ref,
                    r_o, lw_o, k_o, v_o, kkn_o, a_o, g_o, qabs_o, iq_o, ik_o, iw_o, cl_o, clt_o, carry):
    i = pl.program_id(1)
    tm = x_ref.shape[0]

    @pl.when(i == 0)
    def _():
        carry[...] = jnp.zeros_like(carry)

    u = x_ref[...] * (1.0 + sc_ref[...]) + sh_ref[...]
    p = _bdot(u, win_ref[...])
    ps = p[:, :N_SHIFT_P]
    rows = lax.broadcasted_iota(I32, (tm, 1), 0)
    prev = jnp.where(rows == 0, carry[0:1, :], pltpu.roll(ps, 1, 0))
    carry[0:1, :] = ps[tm - 1:tm, :]
    ps = ps + mu_ref[...] * (prev - ps)

    pr, pk, pv = ps[:, 0:512], ps[:, 512:1024], ps[:, 1024:1536]
    pw, pa, pg = ps[:, 1536:1664], ps[:, 1664:1792], ps[:, 1792:1920]
    w_log = -_softplus(-(w0_ref[...] + _dot3(jnp.tanh(pw), w2_ref[...]))) - 0.5
    lw_o[...] = -jnp.exp(w_log)
    a = _sigmoid(a0_ref[...] + _dot3(pa, a2_ref[...]))
    g_o[...] = _dot3(_sigmoid(pg), g2_ref[...])
    kk = pk * kk_ref[...]
    ssq = _dot_exact_rhs(kk * kk, ones_ref[...], nsplit=2)
    kkn_o[...] = kk / jnp.maximum(jnp.sqrt(ssq), 1e-12)
    k_o[...] = pk * (1.0 + (a - 1.0) * ka_ref[...])
    r_o[...] = pr
    v_o[...] = pv
    a_o[...] = a

    pq, pkv = p[:, 1920:2176], p[:, 2176:2304]
    pik, piw = p[:, 2304:2432], p[:, 2432:2560]
    q_lat = pq * lax.rsqrt(jnp.mean(pq * pq, axis=-1, keepdims=True) + 1e-6) * qn_ref[...]
    q = _bdot(q_lat, wq_ref[...])
    qabs_o[...] = (_bdot(q, wuk_ref[...]) * (ATT_HEAD ** -0.5 * LOG2E)).astype(BF16)
    c_lat = pkv * lax.rsqrt(jnp.mean(pkv * pkv, axis=-1, keepdims=True) + 1e-6) * kvn_ref[...]
    spos = i * tm + rows
    s_hi = (spos >> 7).astype(F32)
    s_lo = (spos & (LANES - 1)).astype(F32)
    lane_t = lax.broadcasted_iota(I32, (1, LANES), 1)
    extra = jnp.where(lane_t == 0, 1.0, jnp.where((lane_t == 1) | (lane_t == 2), s_hi,
                                                  jnp.where((lane_t == 3) | (lane_t == 4), s_lo, 0.0)))
    c_aug = jnp.concatenate([c_lat, extra], axis=-1)
    cl_o[...] = c_aug.astype(BF16)
    clt_o[...] = jnp.transpose(c_aug).astype(BF16)
    iq_o[...] = (_bdot(q_lat, wiq_ref[...]) * (IDX_DIM ** -0.5)).astype(BF16)
    lane = lax.broadcasted_iota(I32, (1, LANES), 1)
    valid = lane < IDX_DIM
    mu = jnp.sum(pik, axis=-1, keepdims=True) * (1.0 / IDX_DIM)
    dlt = jnp.where(valid, pik - mu, 0.0)
    var = jnp.sum(dlt * dlt, axis=-1, keepdims=True) * (1.0 / IDX_DIM)
    ik_o[...] = jnp.where(valid, dlt * lax.rsqrt(var + 1e-5) * ig_ref[...] + ib_ref[...], 0.0).astype(BF16)
    iw_o[...] = piw * (IDX_HEADS ** -0.5)


def _pad_cols(w, widths):
    parts, o = [], 0
    for true, padded in widths:
        seg = w[..., o:o + true]
        if padded > true:
            seg = jnp.pad(seg, [(0, 0)] * (w.ndim - 1) + [(0, padded - true)])
        parts.append(seg)
        o += true
    return jnp.concatenate(parts, axis=-1)


def _pad_rows(w, rows):
    return jnp.pad(w, ((0, rows - w.shape[0]), (0, 0)))


def _block_diag(blocks):
    H, a, b = blocks.shape
    eye = jnp.eye(H, dtype=blocks.dtype)
    return (eye[:, None, :, None] * blocks[:, :, None, :]).reshape(H * a, H * b)


def _head_ones(n, head):
    idx = np.arange(n) // head
    return jnp.asarray(idx[:, None] == idx[None, :], BF16)


def _in_proj(x, sc1, sh1, w_in, shift_mu, w0, w2, a0, a2, g2, k_k, k_a, q_norm, w_q_up, w_uk, kv_norm, idx_w_q,
             idx_ln_g, idx_ln_b):
    B, S, D = x.shape
    tm = min(TM_PROJ, S)
    widths = tuple((s[1], s[2]) for s in _SEG)
    win_p = _pad_cols(w_in, widths).astype(BF16)
    mu_p = _pad_cols(shift_mu.reshape(1, -1), widths[:6])
    w2_p = _pad_rows(w2, LANES)
    a2_p = _pad_rows(a2, LANES)
    wuk_bd = _block_diag(w_uk).astype(BF16)
    wiq_p = _pad_cols(idx_w_q, ((IDX_DIM, LANES),) * IDX_HEADS).astype(BF16)
    ig_p = _pad_cols(idx_ln_g.reshape(1, -1), ((IDX_DIM, LANES),))
    ib_p = _pad_cols(idx_ln_b.reshape(1, -1), ((IDX_DIM, LANES),))
    row = lambda v: v.reshape(1, -1)
    tok = lambda n: pl.BlockSpec((None, tm, n), lambda b, i: (b, i, 0))
    mod = pl.BlockSpec((None, 1, D), lambda b, i: (b, 0, 0))
    full = lambda a: pl.BlockSpec(a.shape, lambda b, i: (0,) * a.ndim)
    consts = [win_p, mu_p, row(w0), w2_p, row(a0), a2_p, g2, row(k_k), row(k_a), _head_ones(D_RWKV, RWKV_HEAD),
              row(q_norm), w_q_up.astype(BF16), wuk_bd, row(kv_norm), wiq_p, ig_p, ib_p]
    outs = [(D_RWKV, F32)] * 7 + [(N_ATT_HEADS * KV_LORA, BF16), (IDX_HEADS * LANES, BF16), (LANES, BF16),
                                  (LANES, F32), (KV_LORA + LANES, BF16)]
    return pl.pallas_call(
        _in_proj_kernel,
        grid=(B, S // tm),
        in_specs=[tok(D), mod, mod] + [full(a) for a in consts],
        out_specs=[tok(n) for n, _ in outs] + [pl.BlockSpec((None, KV_LORA + LANES, tm), lambda b, i: (b, 0, i))],
        out_shape=[jax.ShapeDtypeStruct((B, S, n), dt) for n, dt in outs]
        + [jax.ShapeDtypeStruct((B, KV_LORA + LANES, S), BF16)],
        scratch_shapes=[pltpu.VMEM((8, N_SHIFT_P), F32)],
        compiler_params=_cparams(("arbitrary", "arbitrary")),
        name="in_proj",
    )(x, sc1, sh1, *consts)


def _rwkv_kernel(r_ref, lw_ref, k_ref, v_ref, kk_ref, a_ref, g_ref, rk_ref, lnw_ref, lnb_ref, tri_ref, ones_ref,
                 o_ref, state, *, L):
    c = pl.program_id(1)
    nc = r_ref.shape[0] // L

    @pl.when(c == 0)
    def _():
        state[...] = jnp.zeros_like(state)

    r, lw, k, v, kk, a = r_ref[...], lw_ref[...], k_ref[...], v_ref[...], kk_ref[...], a_ref[...]
    cum = _dot_exact_lhs(tri_ref[...], lw)
    rows = [slice(ci * L, (ci + 1) * L) for ci in range(nc)]
    last = [cum[(ci + 1) * L - 1:(ci + 1) * L, :] for ci in range(nc)]
    cum_last = jnp.concatenate([jnp.broadcast_to(z, (L, z.shape[1])) for z in last], axis=0)
    w_incl = jnp.exp(cum)
    w_inv = jnp.exp(-cum)
    w_rel = jnp.exp(cum_last - cum)
    w_last = [jnp.exp(z) for z in last]
    bvec = kk * a
    at = -kk * jnp.exp(cum - lw)
    rt = r * w_incl
    bt = bvec * w_inv
    kt = k * w_inv
    bh = bvec * w_rel
    kh = k * w_rel
    ti = lax.broadcasted_iota(I32, (L, L), 0)
    tj = lax.broadcasted_iota(I32, (L, L), 1)
    strict = tj < ti
    incl = tj <= ti
    eye = lax.broadcasted_iota(I32, (RWKV_HEAD, RWKV_HEAD), 0) == lax.broadcasted_iota(I32, (RWKV_HEAD, RWKV_HEAD), 1)
    NT = (((1,), (1,)), ((), ()))
    TN = (((0,), (0,)), ((), ()))
    heads = range(N_RWKV_HEADS)
    sls = [slice(h * RWKV_HEAD, (h + 1) * RWKV_HEAD) for h in heads]
    units = [(ci, h) for ci in range(nc) for h in heads]
    idx = range(len(units))
    mm = lambda x, y, dims=(((1,), (0,)), ((), ())): lax.dot_general(
        x.astype(BF16), y.astype(BF16), dims, preferred_element_type=F32)
    at_b, rt_b, bt_b, kt_b, bh_b, kh_b, v_b = [z.astype(BF16) for z in (at, rt, bt, kt, bh, kh, v)]
    cut = lambda z, u: z[rows[u[0]], sls[u[1]]]
    vh = [cut(v_b, u) for u in units]
    ath = [cut(at_b, u) for u in units]
    ar = [jnp.concatenate([cut(at_b, u), cut(rt_b, u)], axis=0) for u in units]
    g_b = [mm(ar[i], cut(bt_b, units[i]), NT) for i in idx]
    g_k = [mm(ar[i], cut(kt_b, units[i]), NT) for i in idx]
    n_ab = [jnp.where(strict, g_b[i][:L], 0.0) for i in idx]
    a_ak = [jnp.where(strict, g_k[i][:L], 0.0) for i in idx]
    a_rb = [jnp.where(incl, g_b[i][L:], 0.0).astype(BF16) for i in idx]
    a_rk = [jnp.where(incl, g_k[i][L:], 0.0) for i in idx]
    akv = [mm(a_ak[i], vh[i]) for i in idx]
    eye_l = jnp.where(ti == tj, 1.0, 0.0)
    tinv = [eye_l + n_ab[i] for i in idx]
    pw = n_ab
    for _ in range(int(math.log2(L)) - 1):
        pw = [mm(pw[i], pw[i]) for i in idx]
        tinv = [tinv[i] + mm(pw[i], tinv[i]) for i in idx]
    tinv = [t.astype(BF16) for t in tinv]
    a_t = [mm(tinv[i], ath[i]).astype(BF16) for i in idx]
    y = [mm(tinv[i], akv[i]).astype(BF16) for i in idx]
    m_c = [jnp.where(eye, w_last[units[i][0]][:, sls[units[i][1]]], 0.0) + mm(a_t[i], cut(bh_b, units[i]), TN)
           for i in idx]
    c_c = [mm(y[i], cut(bh_b, units[i]), TN) + mm(vh[i], cut(kh_b, units[i]), TN) for i in idx]
    q_c = [cut(rt, units[i]) + mm(a_rb[i], a_t[i]) for i in idx]
    o_loc = [mm(a_rb[i], y[i]) + mm(a_rk[i], vh[i]) for i in idx]
    s = [state[h] for h in heads]
    for ci in range(nc):
        for h in heads:
            i = ci * N_RWKV_HEADS + h
            o = o_loc[i] + _dot3(q_c[i], s[h], NT)
            s[h] = _dot3(s[h], m_c[i]) + c_c[i]
            mu = jnp.mean(o, axis=-1, keepdims=True)
            d = o - mu
            var = jnp.mean(d * d, axis=-1, keepdims=True)
            o_ref[rows[ci], sls[h]] = d * lax.rsqrt(var + RWKV_GN_EPS)
    for h in heads:
        state[h] = s[h]
    bonus = _dot_exact_rhs(r * k * rk_ref[...], ones_ref[...], nsplit=3) * v
    o_ref[...] = (o_ref[...] * lnw_ref[...] + lnb_ref[...] + bonus) * g_ref[...]


def _rwkv_scan(r, lw, k, v, kk, a, g, r_k, ln_w, ln_b):
    B, S, DR = r.shape
    L = min(L_CHUNK, S)
    lb = min(L * CHUNKS_PER_STEP, S)
    tri = jnp.asarray(np.kron(np.eye(lb // L), np.tril(np.ones((L, L)))), BF16)
    row = lambda z: z.reshape(1, -1)
    tok = pl.BlockSpec((None, lb, DR), lambda b, c: (b, c, 0))
    full = lambda z: pl.BlockSpec(z.shape, lambda b, c: (0,) * z.ndim)
    consts = [row(r_k), row(ln_w), row(ln_b), tri, _head_ones(DR, RWKV_HEAD)]
    return pl.pallas_call(
        functools.partial(_rwkv_kernel, L=L),
        grid=(B, S // lb),
        in_specs=[tok] * 7 + [full(z) for z in consts],
        out_specs=tok,
        out_shape=jax.ShapeDtypeStruct((B, S, DR), F32),
        scratch_shapes=[pltpu.VMEM((N_RWKV_HEADS, RWKV_HEAD, RWKV_HEAD), F32)],
        compiler_params=_cparams(("arbitrary", "arbitrary")),
        name="rwkv_scan",
    )(r, lw, k, v, kk, a, g, *consts)


def _alibi_cols():
    slope = np.asarray([2.0 ** (-8.0 * (h + 1) / N_ATT_HEADS) * LOG2E for h in range(N_ATT_HEADS)], np.float32)
    c_hi = slope.astype(BF16).astype(np.float32)
    c_lo = (slope - c_hi).astype(BF16).astype(np.float32)
    t = np.zeros((N_ATT_HEADS, LANES), np.float32)
    t[:, 1], t[:, 2], t[:, 3], t[:, 4] = LANES * c_hi, LANES * c_lo, c_hi, c_lo
    return jnp.asarray(t.astype(BF16))


def _dsa_kernel(iq_ref, iw_ref, qa_ref, ik_ref, ca_ref, cat_ref, wuv_ref, tril_ref, acol_ref, o_ref, key_ref,
                bias_ref, lg_ref, p_ref, *, topk, q_off, select):
    qi = pl.program_id(1) + q_off
    tq = qa_ref.shape[0]
    sk = ca_ref.shape[0]
    kc = min(KEY_CHUNK, sk)
    chunks = [slice(c * kc, (c + 1) * kc) for c in range(sk // kc)]
    tpos = qi * tq + lax.broadcasted_iota(I32, (1, tq), 1)
    srow = lax.broadcasted_iota(I32, (kc, tq), 0)
    NT = (((1,), (1,)), ((), ()))

    if not select:
        for c, cs in enumerate(chunks):
            bias_ref[cs, :] = jnp.where(srow + c * kc <= tpos, 0.0, NEG_BIG)
    else:
        iw_t = jnp.transpose(iw_ref[...])
        for c, cs in enumerate(chunks):
            ikc = ik_ref[cs, :]
            score = jnp.zeros((kc, tq), F32)
            for h in range(IDX_HEADS):
                s = lax.dot_general(ikc, iq_ref[:, h * LANES:(h + 1) * LANES], NT, preferred_element_type=F32)
                score = score + iw_t[h:h + 1, :] * jnp.maximum(s, 0.0)
            bits = pltpu.bitcast(score + 0.0, I32)
            key = bits ^ ((bits >> 31) & 0x7FFFFFFF)
            key_ref[cs, :] = jnp.where(srow + c * kc <= tpos, key, INT_MIN)
        kcount = jnp.minimum(topk, tpos + 1).astype(F32)

        def sum_keys(x):
            part = jnp.sum(x.reshape(x.shape[0] // SUM_ROWS, SUM_ROWS, x.shape[1]), axis=0)
            return jnp.sum(part, axis=0, keepdims=True)

        def count_ge(cand):
            return sum_keys(jnp.where(key_ref[...] >= cand, 1.0, 0.0))

        thr = jnp.where(count_ge(jnp.zeros((1, tq), I32)) >= kcount, 0, INT_MIN).astype(I32)

        def bit_step(i, thr):
            cand = thr | (1 << (30 - i))
            return jnp.where(count_ge(cand) >= kcount, cand, thr)

        thr = lax.fori_loop(0, 31, bit_step, thr)

        need = kcount - sum_keys(jnp.where(key_ref[...] > thr, 1.0, 0.0))
        carry = jnp.zeros((1, tq), F32)
        for j in range(sk // LANES):
            sl = slice(j * LANES, (j + 1) * LANES)
            key = key_ref[sl, :]
            eq = key == thr
            e = jnp.where(eq, 1.0, 0.0)
            before = jnp.dot(tril_ref[...], e.astype(BF16), preferred_element_type=F32) + carry
            sel = (key > thr) | (eq & (before < need))
            bias_ref[sl, :] = jnp.where(sel, 0.0, NEG_BIG)
            carry = carry + jnp.sum(e, axis=0, keepdims=True)

    outs = []
    for h in range(N_ATT_HEADS):
        q_aug = jnp.concatenate([qa_ref[:, h * KV_LORA:(h + 1) * KV_LORA],
                                 jnp.broadcast_to(acol_ref[h:h + 1, :], (tq, LANES))], axis=-1)
        lg_all = lax.dot_general(ca_ref[...], q_aug, NT, preferred_element_type=F32)
        m = jnp.full((1, tq), -jnp.inf, F32)
        for cs in chunks:
            lg = lg_all[cs, :] + bias_ref[cs, :]
            lg_ref[cs, :] = lg
            m = jnp.maximum(m, jnp.max(lg, axis=0, keepdims=True))
        for cs in chunks:
            p_ref[cs, :] = jnp.exp2(lg_ref[cs, :] - m).astype(BF16)
        pv = jnp.dot(cat_ref[...], p_ref[...], preferred_element_type=F32)
        outs.append((pv[:KV_LORA, :] / pv[KV_LORA:KV_LORA + 1, :]).astype(BF16))
    o_lat_t = jnp.concatenate(outs, axis=0)
    o_ref[...] = lax.dot_general(o_lat_t, wuv_ref[...], (((0,), (0,)), ((), ())), preferred_element_type=F32)


def _dsa_attn(iq, iw, qabs, ik, ca, cat, w_uv):
    B, S, _ = iq.shape
    tq = min(TQ, S)
    topk = min(TOPK_MAX, S // 4)
    wuv_bd = _block_diag(w_uv).astype(BF16)
    tril = jnp.asarray(np.tril(np.ones((LANES, LANES)), -1), BF16)
    acol = _alibi_cols()
    full = lambda z: pl.BlockSpec(z.shape, lambda b, i: (0,) * z.ndim)
    nq = 1
    outs = []
    for q_off in range(0, S // tq, nq):
        sk = (q_off + nq) * tq
        tok = lambda n, q_off=q_off: pl.BlockSpec((None, tq, n), lambda b, i: (b, i + q_off, 0))
        seq = lambda n, sk=sk: pl.BlockSpec((None, sk, n), lambda b, i: (b, 0, 0))
        outs.append(pl.pallas_call(
            functools.partial(_dsa_kernel, topk=topk, q_off=q_off, select=sk > topk),
            grid=(B, nq),
            in_specs=[tok(IDX_HEADS * LANES), tok(LANES), tok(N_ATT_HEADS * KV_LORA), seq(LANES), seq(KV_LORA + LANES),
                      pl.BlockSpec((None, KV_LORA + LANES, sk), lambda b, i: (b, 0, 0)),
                      full(wuv_bd), full(tril), full(acol)],
            out_specs=pl.BlockSpec((None, tq, D_ATT), lambda b, i: (b, i, 0)),
            out_shape=jax.ShapeDtypeStruct((B, nq * tq, D_ATT), F32),
            scratch_shapes=[pltpu.VMEM((sk, tq), I32), pltpu.VMEM((sk, tq), F32), pltpu.VMEM((sk, tq), F32),
                            pltpu.VMEM((sk, tq), BF16)],
            compiler_params=_cparams(("arbitrary", "arbitrary")),
            name=f"dsa_attn_k{sk}",
        )(iq, iw, qabs, ik, ca, cat, wuv_bd, tril, acol))
    return jnp.concatenate(outs, axis=1)


def _layernorm_rows(y, g, b):
    mu = jnp.mean(y, axis=-1, keepdims=True)
    d = y - mu
    var = jnp.mean(d * d, axis=-1, keepdims=True)
    return d * lax.rsqrt(var + 1e-5) * g + b


def _mix_kernel(orw_ref, ods_ref, x_ref, gt_ref, sc_ref, sh_ref, wtop_ref, wbot_ref, g_ref, b_ref, wr_ref, br_ref,
                tril_ref, triu_ref, x1_o, u2_o, route_o, tbl_o, cnt_o, carry, *, alpha):
    first = (pl.program_id(0) == 0) & (pl.program_id(1) == 0)
    tm = x_ref.shape[0]

    @pl.when(first)
    def _():
        carry[...] = jnp.zeros_like(carry)

    mix = _bdot(orw_ref[...], wtop_ref[...]) + _bdot(ods_ref[...], wbot_ref[...])
    x1 = _layernorm_rows(alpha * x_ref[...] + (1.0 + gt_ref[...]) * mix, g_ref[...], b_ref[...])
    x1_o[...] = x1
    u2 = x1 * (1.0 + sc_ref[...]) + sh_ref[...]
    u2_o[...] = u2

    lg = _dot3(u2, wr_ref[...]) + br_ref[...]
    lane = lax.broadcasted_iota(I32, (tm, LANES), 1)
    idxs, vals = [], []
    for _ in range(TOP_K_EXPERTS):
        m = jnp.max(lg, axis=-1, keepdims=True)
        idx = jnp.min(jnp.where(lg == m, lane, LANES), axis=-1, keepdims=True)
        idxs.append(idx)
        vals.append(m)
        lg = jnp.where(lane == idx, -jnp.inf, lg)
    es = [jnp.exp(v - vals[0]) for v in vals]
    den = es[0] + es[1] + es[2] + es[3]
    hot = jnp.zeros((tm, LANES), F32)
    for idx in idxs:
        hot = hot + jnp.where(lane == idx, 1.0, 0.0)
    before = jnp.dot(tril_ref[...], hot.astype(BF16), preferred_element_type=F32)
    cnt = jnp.sum(hot, axis=0, keepdims=True)
    n_run = jnp.floor((cnt + (RUN_ALIGN - 1)) * (1.0 / RUN_ALIGN))
    n_stage = jnp.floor((n_run * RUN_ALIGN + (STAGE_ALIGN - 1)) * (1.0 / STAGE_ALIGN))
    off = jnp.dot(jnp.broadcast_to(n_stage, (SUBLANES, LANES)).astype(BF16), triu_ref[...],
                  preferred_element_type=F32)[0:1, :] * STAGE_ALIGN
    where_in_stage = off + before
    route = jnp.zeros((tm, LANES), F32)
    for k in range(TOP_K_EXPERTS):
        spos = jnp.sum(jnp.where(lane == idxs[k], where_in_stage, 0.0), axis=-1, keepdims=True)
        route = jnp.where(lane == k, idxs[k].astype(F32), route)
        route = jnp.where(lane == TOP_K_EXPERTS + k, es[k] / den, route)
        route = jnp.where(lane == 2 * TOP_K_EXPERTS + k, spos, route)
    route_o[...] = route
    sub = lax.broadcasted_iota(I32, (SUBLANES, LANES), 0)
    tbl_o[...] = jnp.where(sub == 0, n_run, jnp.where(sub == 1, off, jnp.where(sub == 2, carry[0:1, :], 0.0)))
    carry[0:1, :] = carry[0:1, :] + n_run * RUN_ALIGN
    cnt_o[...] = carry[...]


def _mix_out(o_rwkv, o_dsa, x, gt1, sc2, sh2, w_out, ln_g, ln_b, w_router, b_router):
    B, S, D = x.shape
    tm = min(TM_PROJ, S)
    alpha = 2.0 ** 0.25
    wtop = w_out[:D_RWKV].astype(BF16)
    wbot = w_out[D_RWKV:].astype(BF16)
    wr_p = jnp.pad(w_router, ((0, 0), (0, LANES - N_EXPERTS)))
    br_p = jnp.pad(b_router.reshape(1, -1), ((0, 0), (0, LANES - N_EXPERTS)), constant_values=NEG_BIG)
    tril = jnp.asarray(np.tril(np.ones((tm, tm)), -1), BF16)
    triu = jnp.asarray(np.triu(np.ones((LANES, LANES)), 1), BF16)
    row = lambda v: v.reshape(1, -1)
    tok = lambda n: pl.BlockSpec((None, tm, n), lambda b, i: (b, i, 0))
    mod = pl.BlockSpec((None, 1, D), lambda b, i: (b, 0, 0))
    full = lambda a: pl.BlockSpec(a.shape, lambda b, i: (0,) * a.ndim)
    consts = [wtop, wbot, row(ln_g), row(ln_b), wr_p, br_p, tril, triu]
    return pl.pallas_call(
        functools.partial(_mix_kernel, alpha=alpha),
        grid=(B, S // tm),
        in_specs=[tok(D_RWKV), tok(D_ATT), tok(D), mod, mod, mod] + [full(a) for a in consts],
        out_specs=[tok(D), tok(D), tok(LANES), pl.BlockSpec((None, None, SUBLANES, LANES), lambda b, i: (b, i, 0, 0)),
                   pl.BlockSpec((SUBLANES, LANES), lambda b, i: (0, 0))],
        out_shape=[jax.ShapeDtypeStruct((B, S, D), F32), jax.ShapeDtypeStruct((B, S, D), F32),
                   jax.ShapeDtypeStruct((B, S, LANES), F32),
                   jax.ShapeDtypeStruct((B, S // tm, SUBLANES, LANES), F32),
                   jax.ShapeDtypeStruct((SUBLANES, LANES), F32)],
        scratch_shapes=[pltpu.VMEM((8, LANES), F32)],
        compiler_params=_cparams(("arbitrary", "arbitrary")),
        name="mix_out",
    )(o_rwkv, o_dsa, x, gt1, sc2, sh2, *consts)


RUN_ALIGN = SUBLANES
STAGE_ALIGN = 32
STAGE_ROWS = 2048
STAGE_CHUNK = 256


def _run_copies(nrun_ref, tile, copy_of):
    for e in range(N_EXPERTS):
        pieces = (nrun_ref[tile * N_EXPERTS + e] * RUN_ALIGN + STAGE_ALIGN - 1) // STAGE_ALIGN

        def piece(j, carry, e=e):
            copy_of(e, j).start()
            return carry

        lax.fori_loop(0, pieces, piece, 0)


def _dispatch_kernel(nrun_ref, off_ref, start_ref, npiece_ref, nstage_ref, zs_ref, zn_ref, tail_ref,
                     u_ref, route_ref, xs_out, stag, zeros, sem, zsem):
    i = pl.program_id(0)
    n = pl.num_programs(0)
    tm = u_ref.shape[0]
    bm = zeros.shape[0]
    slot = i % 2

    @pl.when(i == 0)
    def _():
        zeros[...] = jnp.zeros_like(zeros)
        fills = []
        for e in range(N_EXPERTS):
            for b in range(bm.bit_length()):
                rows = RUN_ALIGN << b
                if rows > bm:
                    break
                done = (zn_ref[e] >> (b + 1)) << (b + 1)
                dst = pl.multiple_of(zs_ref[e] + done * RUN_ALIGN, RUN_ALIGN)
                fills.append(((zn_ref[e] >> b) & 1 == 1,
                              pltpu.make_async_copy(zeros.at[pl.ds(0, rows)], xs_out.at[pl.ds(dst, rows)], zsem)))
        for pred, cp in fills:
            pl.when(pred)(cp.start)

        def tail_copy(j):
            return pltpu.make_async_copy(zeros, xs_out.at[pl.ds(pl.multiple_of(j * bm, bm), bm)], zsem)

        lax.fori_loop(tail_ref[0], tail_ref[1], lambda j, c: (tail_copy(j).start(), c)[1], 0)
        for pred, cp in fills:
            pl.when(pred)(cp.wait)
        lax.fori_loop(tail_ref[0], tail_ref[1], lambda j, c: (tail_copy(j).wait(), c)[1], 0)

    route_t = jnp.transpose(route_ref[...])
    spos = [route_t[2 * TOP_K_EXPERTS + k:2 * TOP_K_EXPERTS + k + 1, :].astype(I32) for k in range(TOP_K_EXPERTS)]
    ub = u_ref[...].astype(BF16)
    srow = lax.broadcasted_iota(I32, (STAGE_CHUNK, tm), 0)
    for c in range(STAGE_ROWS // STAGE_CHUNK):
        @pl.when(c * STAGE_CHUNK < nstage_ref[i])
        def _(c=c):
            rows = srow + c * STAGE_CHUNK
            sel = (rows == spos[0]) | (rows == spos[1]) | (rows == spos[2]) | (rows == spos[3])
            stag[slot, c * STAGE_CHUNK:(c + 1) * STAGE_CHUNK, :] = jnp.dot(
                jnp.where(sel, 1.0, 0.0).astype(BF16), ub, preferred_element_type=F32)

    def piece_copy(s, tile):
        def copy_of(e, j):
            src = pl.multiple_of(off_ref[tile * N_EXPERTS + e] + j * STAGE_ALIGN, STAGE_ALIGN)
            dst = pl.multiple_of(start_ref[tile * N_EXPERTS + e] + j * STAGE_ALIGN, RUN_ALIGN)
            return pltpu.make_async_copy(stag.at[s, pl.ds(src, STAGE_ALIGN)], xs_out.at[pl.ds(dst, STAGE_ALIGN)], sem)
        return copy_of

    def drain(tile):
        def w(j, carry):
            pltpu.make_async_copy(stag.at[0, pl.ds(0, STAGE_ALIGN)], xs_out.at[pl.ds(0, STAGE_ALIGN)], sem).wait()
            return carry
        lax.fori_loop(0, npiece_ref[tile], w, 0)

    @pl.when(i > 0)
    def _():
        drain(i - 1)

    _run_copies(nrun_ref, i, piece_copy(slot, i))

    @pl.when(i == n - 1)
    def _():
        drain(i)


def _moe_dispatch(u2, route, tables, n_rows):
    T, D = u2.shape
    tm = TM_ROUTE
    bm = BM_EXPERT
    return pl.pallas_call(
        _dispatch_kernel,
        grid_spec=pltpu.PrefetchScalarGridSpec(
            num_scalar_prefetch=len(tables),
            grid=(T // tm,),
            in_specs=[pl.BlockSpec((tm, D), lambda i, *_: (i, 0)),
                      pl.BlockSpec((tm, LANES), lambda i, *_: (i, 0))],
            out_specs=pl.BlockSpec(memory_space=pl.ANY),
            scratch_shapes=[pltpu.VMEM((2, STAGE_ROWS, D), F32), pltpu.VMEM((bm, D), F32),
                            pltpu.SemaphoreType.DMA(()), pltpu.SemaphoreType.DMA(())],
        ),
        out_shape=jax.ShapeDtypeStruct((n_rows, D), F32),
        compiler_params=_cparams(("arbitrary",)),
        name="moe_dispatch",
    )(*tables, u2, route)


GU_GROUP = 2 * LANES


def _deinterleave_perm():
    p = np.zeros((GU_GROUP, GU_GROUP), np.float32)
    l = np.arange(LANES)
    p[2 * l, l] = 1.0
    p[2 * l + 1, LANES + l] = 1.0
    return jnp.asarray(p, BF16)


def _expert_kernel(be_ref, nb_ref, xs_ref, wgu_ref, bgu_ref, wd_ref, bd_ref, perm_ref, ys_ref, wp, wdb):
    i = pl.program_id(0)
    used = i < nb_ref[0]
    new_expert = (i == 0) | (be_ref[i] != be_ref[jnp.maximum(i - 1, 0)])
    n_groups = wgu_ref.shape[1] // GU_GROUP

    @pl.when(used & new_expert)
    def _():
        for j in range(n_groups):
            sl = slice(j * GU_GROUP, (j + 1) * GU_GROUP)
            wp[:, sl] = jnp.dot(wgu_ref[:, sl].astype(BF16), perm_ref[...], preferred_element_type=F32).astype(BF16)
        wdb[...] = wd_ref[...].astype(BF16)

    @pl.when(used)
    def _():
        xb = xs_ref[...].astype(BF16)
        gu = jnp.dot(xb, wp[...], preferred_element_type=F32) + bgu_ref[...]
        hs = []
        for j in range(n_groups):
            gate = jnp.minimum(gu[:, j * GU_GROUP:j * GU_GROUP + LANES], SWIGLU_LIMIT)
            up = jnp.clip(gu[:, j * GU_GROUP + LANES:(j + 1) * GU_GROUP], -SWIGLU_LIMIT, SWIGLU_LIMIT)
            hs.append(((up + 1.0) * (gate * _sigmoid(gate * SWIGLU_ALPHA))).astype(BF16))
        h = jnp.concatenate(hs, axis=-1)
        ys_ref[...] = jnp.dot(h, wdb[...], preferred_element_type=F32) + bd_ref[...]

    @pl.when(jnp.logical_not(used))
    def _():
        ys_ref[...] = jnp.zeros_like(ys_ref)


def _moe_experts(xs, block_e, n_used, w_gu, b_gu_p, w_dn, b_dn):
    n_rows, D = xs.shape
    E, _, F2 = w_gu.shape
    bm = BM_EXPERT
    n_blocks = n_rows // bm
    perm = _deinterleave_perm()
    wspec = lambda shp: pl.BlockSpec((None,) + shp, lambda i, be, nb: (be[i], 0, 0))
    return pl.pallas_call(
        _expert_kernel,
        grid_spec=pltpu.PrefetchScalarGridSpec(
            num_scalar_prefetch=2,
            grid=(n_blocks,),
            in_specs=[pl.BlockSpec((bm, D), lambda i, be, nb: (jnp.minimum(i, nb[0] - 1), 0)),
                      wspec((D, F2)), wspec((1, F2)), wspec((F2 // 2, D)), wspec((1, D)),
                      pl.BlockSpec(perm.shape, lambda i, be, nb: (0, 0))],
            out_specs=pl.BlockSpec((bm, D), lambda i, be, nb: (i, 0)),
            scratch_shapes=[pltpu.VMEM((D, F2), BF16), pltpu.VMEM((F2 // 2, D), BF16)],
        ),
        out_shape=jax.ShapeDtypeStruct((n_rows, D), F32),
        compiler_params=pltpu.CompilerParams(dimension_semantics=("arbitrary",), vmem_limit_bytes=VMEM_LIMIT_EXPERTS),
        name="moe_experts",
    )(block_e, n_used, xs, w_gu, b_gu_p, w_dn, b_dn, perm)


def _combine_kernel(nrun_ref, off_ref, start_ref, npiece_ref, nstage_ref, ys_ref, x1_ref, route_ref, gt_ref, g_ref,
                    b_ref, o_ref, stag, sem, *, alpha):
    i = pl.program_id(0)
    n = pl.num_programs(0)
    tm = x1_ref.shape[0]
    slot = i % 2

    def gather(tile, s):
        def copy_of(e, j):
            src = pl.multiple_of(start_ref[tile * N_EXPERTS + e] + j * STAGE_ALIGN, RUN_ALIGN)
            dst = pl.multiple_of(off_ref[tile * N_EXPERTS + e] + j * STAGE_ALIGN, STAGE_ALIGN)
            return pltpu.make_async_copy(ys_ref.at[pl.ds(src, STAGE_ALIGN)], stag.at[s, pl.ds(dst, STAGE_ALIGN)],
                                         sem.at[s])
        _run_copies(nrun_ref, tile, copy_of)

    @pl.when(i == 0)
    def _():
        stag[...] = jnp.zeros_like(stag)
        gather(0, 0)

    @pl.when(i + 1 < n)
    def _():
        gather(i + 1, 1 - slot)

    def w(j, carry):
        pltpu.make_async_copy(ys_ref.at[pl.ds(0, STAGE_ALIGN)], stag.at[slot, pl.ds(0, STAGE_ALIGN)],
                              sem.at[slot]).wait()
        return carry
    lax.fori_loop(0, npiece_ref[i], w, 0)

    route = route_ref[...]
    spos = [route[:, 2 * TOP_K_EXPERTS + k:2 * TOP_K_EXPERTS + k + 1].astype(I32) for k in range(TOP_K_EXPERTS)]
    gate = [route[:, TOP_K_EXPERTS + k:TOP_K_EXPERTS + k + 1] for k in range(TOP_K_EXPERTS)]
    scol = lax.broadcasted_iota(I32, (tm, STAGE_CHUNK), 1)
    acc_ref = o_ref
    acc_ref[...] = jnp.zeros_like(acc_ref)
    for c in range(STAGE_ROWS // STAGE_CHUNK):
        @pl.when(c * STAGE_CHUNK < nstage_ref[i])
        def _(c=c):
            cols = scol + c * STAGE_CHUNK
            wgt = jnp.zeros((tm, STAGE_CHUNK), F32)
            for k in range(TOP_K_EXPERTS):
                wgt = wgt + jnp.where(cols == spos[k], gate[k], 0.0)
            acc_ref[...] += jnp.dot(wgt.astype(BF16), stag[slot, c * STAGE_CHUNK:(c + 1) * STAGE_CHUNK, :].astype(BF16),
                                    preferred_element_type=F32)
    o_ref[...] = _layernorm_rows(alpha * x1_ref[...] + (1.0 + gt_ref[...]) * acc_ref[...], g_ref[...], b_ref[...])


def _moe_combine(ys, tables, x1, route, gt2, ln_g, ln_b, tiles_per_batch):
    T, D = x1.shape
    tm = TM_ROUTE
    row = lambda v: v.reshape(1, -1)
    return pl.pallas_call(
        functools.partial(_combine_kernel, alpha=2.0 ** 0.25),
        grid_spec=pltpu.PrefetchScalarGridSpec(
            num_scalar_prefetch=len(tables),
            grid=(T // tm,),
            in_specs=[pl.BlockSpec(memory_space=pl.ANY),
                      pl.BlockSpec((tm, D), lambda i, *_: (i, 0)),
                      pl.BlockSpec((tm, LANES), lambda i, *_: (i, 0)),
                      pl.BlockSpec((None, 1, D), lambda i, *_: (i // tiles_per_batch, 0, 0)),
                      pl.BlockSpec((1, D), lambda i, *_: (0, 0)),
                      pl.BlockSpec((1, D), lambda i, *_: (0, 0))],
            out_specs=pl.BlockSpec((tm, D), lambda i, *_: (i, 0)),
            scratch_shapes=[pltpu.VMEM((2, STAGE_ROWS, D), F32), pltpu.SemaphoreType.DMA((2,))],
        ),
        out_shape=jax.ShapeDtypeStruct((T, D), F32),
        compiler_params=_cparams(("arbitrary",)),
        name="moe_combine",
    )(*tables, ys, x1, route, gt2, row(ln_g), row(ln_b))


def _moe_and_norm(x1, u2, route, tbl, totals, gt2, w_gu, b_gu, w_dn, b_dn, ln_g, ln_b):
    B, S, D = x1.shape
    T = B * S
    bm = BM_EXPERT
    tm = TM_ROUTE
    assert T % tm == 0 and TM_PROJ == tm and STAGE_ROWS >= tm * TOP_K_EXPERTS + N_EXPERTS * (STAGE_ALIGN - 1)
    n_tiles = T // tm
    max_rows = T * TOP_K_EXPERTS + n_tiles * N_EXPERTS * (RUN_ALIGN - 1) + N_EXPERTS * STAGE_ALIGN
    n_blocks = -(-max_rows // bm) + N_EXPERTS
    tot = totals[0, :N_EXPERTS].astype(I32)
    padded = (tot + STAGE_ALIGN + bm - 1) // bm * bm
    pad_ends = jnp.cumsum(padded)
    pad_starts = pad_ends - padded
    t3 = tbl.reshape(n_tiles, SUBLANES, LANES)[:, :, :N_EXPERTS].astype(I32)
    nrun, off, base = t3[:, 0, :], t3[:, 1, :], t3[:, 2, :]
    start = pad_starts[None, :] + base
    pieces = (nrun * RUN_ALIGN + STAGE_ALIGN - 1) // STAGE_ALIGN
    npiece = jnp.sum(pieces, axis=1)
    nstage = jnp.sum(pieces, axis=1) * STAGE_ALIGN
    flat = lambda z: z.reshape(-1).astype(I32)
    run_tables = [flat(nrun), flat(off), flat(start), flat(npiece), flat(nstage)]
    zs = pad_starts + tot
    zn = (pad_ends - zs) // RUN_ALIGN
    n_used = (pad_ends[-1:] // bm).astype(I32)
    tail = jnp.concatenate([n_used, jnp.full((1,), n_blocks, I32)])
    blk_row = jnp.arange(n_blocks, dtype=I32) * bm
    block_e = jnp.minimum(jnp.sum((blk_row[:, None] >= pad_ends[None, :]).astype(I32), axis=1), N_EXPERTS - 1)
    E, F2 = b_gu.shape
    b_gu_p = b_gu.reshape(E, F2 // GU_GROUP, LANES, 2).transpose(0, 1, 3, 2).reshape(E, 1, F2)
    route2 = route.reshape(T, LANES)
    xs = _moe_dispatch(u2.reshape(T, D), route2, run_tables + [flat(zs), flat(zn), tail], n_blocks * bm)
    ys = _moe_experts(xs, block_e, n_used, w_gu, b_gu_p, w_dn, b_dn[:, None, :])
    out = _moe_combine(ys, run_tables, x1.reshape(T, D), route2, gt2, ln_g, ln_b, S // tm)
    return out.reshape(B, S, D)


def kernel(x, c, w_ada, b_ada, w_in, shift_mu, rwkv_w0, rwkv_w2, rwkv_a0, rwkv_a2, rwkv_g2, rwkv_k_k, rwkv_k_a, rwkv_r_k, rwkv_ln_w, rwkv_ln_b, mla_q_norm, mla_w_q_up, mla_kv_norm, mla_w_uk, mla_w_uv, idx_w_q, idx_ln_g, idx_ln_b, w_out, ln1_g, ln1_b, w_router, b_router, w_gu, b_gu, w_dn, b_dn, ln2_g, ln2_b):
    depth = w_ada.shape[0]
    assert depth == 1, "DeepNorm constants below are for a single layer"
    l = 0
    mod = _ada_mod(c, w_ada[l], b_ada[l])
    sh1, sc1, gt1, sh2, sc2, gt2 = [m[:, None, :] for m in jnp.split(mod, 6, axis=-1)]
    r, lw, k, v, kk, a, g, qabs, iq, ik, iw, cl, clt = _in_proj(
        x, sc1, sh1, w_in[l], shift_mu[l], rwkv_w0[l], rwkv_w2[l], rwkv_a0[l], rwkv_a2[l], rwkv_g2[l], rwkv_k_k[l],
        rwkv_k_a[l], mla_q_norm[l], mla_w_q_up[l], mla_w_uk[l], mla_kv_norm[l], idx_w_q[l], idx_ln_g[l], idx_ln_b[l])
    o_rwkv = _rwkv_scan(r, lw, k, v, kk, a, g, rwkv_r_k[l], rwkv_ln_w[l], rwkv_ln_b[l])
    o_dsa = _dsa_attn(iq, iw, qabs, ik, cl, clt, mla_w_uv[l])
    x1, u2, route, tbl, totals = _mix_out(o_rwkv, o_dsa, x, gt1, sc2, sh2, w_out[l], ln1_g[l], ln1_b[l],
                                          w_router[l], b_router[l])
    return _moe_and_norm(x1, u2, route, tbl, totals, gt2, w_gu[l], b_gu[l], w_dn[l], b_dn[l], ln2_g[l], ln2_b[l])
```

```python
import functools
import math

import jax
import jax.numpy as jnp
import numpy as np
from jax import lax
from jax.experimental import pallas as pl
from jax.experimental.pallas import tpu as pltpu

F32 = jnp.float32
BF16 = jnp.bfloat16
I32 = jnp.int32

RWKV_HEAD = 64
N_RWKV_HEADS = 8
D_RWKV = RWKV_HEAD * N_RWKV_HEADS
RANK_DECAY = 64
RANK_A = 64
RANK_GATE = 128
RWKV_GN_EPS = 64e-5
ATT_HEAD = 64
N_ATT_HEADS = 8
D_ATT = ATT_HEAD * N_ATT_HEADS
Q_LORA = 256
KV_LORA = 128
IDX_HEADS = 8
IDX_DIM = 64
TOPK_MAX = 256
N_EXPERTS = 32
TOP_K_EXPERTS = 4
SWIGLU_LIMIT = 7.0
SWIGLU_ALPHA = 1.702
NEG_BIG = -1e30
LOG2E = 1.4426950408889634
INT_MIN = -(2 ** 31)

LANES = 128
SUBLANES = 8
VMEM_LIMIT = 48 * 1024 * 1024
VMEM_LIMIT_EXPERTS = 58 * 1024 * 1024

TM_PROJ = 256
L_CHUNK = 64
CHUNKS_PER_STEP = 4
TQ = 256
KEY_CHUNK = 128
SUM_ROWS = 64
TM_ROUTE = 256
BM_EXPERT = 512

_SEG = (("r", 512, 512), ("k", 512, 512), ("v", 512, 512), ("w", 64, 128), ("a", 64, 128), ("g", 128, 128),
        ("q", 256, 256), ("kv", 128, 128), ("ik", 64, 128), ("iw", 8, 128))
N_SHIFT_P = 512 * 3 + 128 * 3
N_IN_P = sum(s[2] for s in _SEG)


def _cparams(sem):
    return pltpu.CompilerParams(dimension_semantics=sem, vmem_limit_bytes=VMEM_LIMIT)


def _bdot(a, b):
    return jnp.dot(a.astype(BF16), b.astype(BF16), preferred_element_type=F32)


def _bdot_nt(a, b):
    return lax.dot_general(a.astype(BF16), b.astype(BF16), (((1,), (1,)), ((), ())), preferred_element_type=F32)


def _bdot_tn(a, b):
    return lax.dot_general(a.astype(BF16), b.astype(BF16), (((0,), (0,)), ((), ())), preferred_element_type=F32)


def _split2(a):
    hi = a.astype(BF16)
    lo = (a - hi.astype(F32)).astype(BF16)
    return hi, lo


def _split3(a):
    hi = a.astype(BF16)
    r1 = a - hi.astype(F32)
    mid = r1.astype(BF16)
    lo = (r1 - mid.astype(F32)).astype(BF16)
    return hi, mid, lo


def _dot3(a, b, dims=(((1,), (0,)), ((), ()))):
    ah, al = _split2(a)
    bh, bl = _split2(b)
    d = functools.partial(lax.dot_general, dimension_numbers=dims, preferred_element_type=F32)
    return d(ah, bh) + (d(ah, bl) + d(al, bh))


def _dot_exact_rhs(a, b_exact, nsplit=3):
    parts = _split3(a) if nsplit == 3 else _split2(a)
    acc = None
    for p in parts[::-1]:
        t = jnp.dot(p, b_exact, preferred_element_type=F32)
        acc = t if acc is None else acc + t
    return acc


def _dot_exact_lhs(a_exact, b, nsplit=3):
    parts = _split3(b) if nsplit == 3 else _split2(b)
    acc = None
    for p in parts[::-1]:
        t = jnp.dot(a_exact, p, preferred_element_type=F32)
        acc = t if acc is None else acc + t
    return acc


def _sigmoid(x):
    return 1.0 / (1.0 + jnp.exp(-x))


def _softplus(x):
    return jnp.maximum(x, 0.0) + jnp.log(1.0 + jnp.exp(-jnp.abs(x)))


def _ada_kernel(c_ref, w_ref, b_ref, o_ref):
    c = c_ref[...]
    o_ref[...] = _dot3(c * _sigmoid(c), w_ref[...]) + b_ref[...]


def _ada_mod(c, w_ada, b_ada):
    B, D = c.shape
    N = w_ada.shape[1]
    tn = 1024
    return pl.pallas_call(
        _ada_kernel,
        grid=(N // tn,),
        in_specs=[pl.BlockSpec((B, D), lambda j: (0, 0)),
                  pl.BlockSpec((D, tn), lambda j: (0, j)),
                  pl.BlockSpec((1, tn), lambda j: (0, j))],
        out_specs=pl.BlockSpec((B, tn), lambda j: (0, j)),
        out_shape=jax.ShapeDtypeStruct((B, N), F32),
        compiler_params=_cparams(("arbitrary",)),
        name="ada_mod",
    )(c, w_ada, b_ada.reshape(1, N))


def _in_proj_kernel(x_ref, sc_ref, sh_ref, win_ref, mu_ref, w0_ref, w2_ref, a0_ref, a2_ref, g2_ref, kk_ref, ka_ref,
                    ones_ref, qn_ref, wq_ref, wuk_ref, kvn_ref, wiq_ref, ig_ref, ib_ref,
                    r_o, lw_o, k_o, v_o, kkn_o, a_o, g_o, qabs_o, iq_o, ik_o, iw_o, cl_o, clt_o, carry):
    i = pl.program_id(1)
    tm = x_ref.shape[0]

    @pl.when(i == 0)
    def _():
        carry[...] = jnp.zeros_like(carry)

    u = x_ref[...] * (1.0 + sc_ref[...]) + sh_ref[...]
    p = _bdot(u, win_ref[...])
    ps = p[:, :N_SHIFT_P]
    rows = lax.broadcasted_iota(I32, (tm, 1), 0)
    prev = jnp.where(rows == 0, carry[0:1, :], pltpu.roll(ps, 1, 0))
    carry[0:1, :] = ps[tm - 1:tm, :]
    ps = ps + mu_ref[...] * (prev - ps)

    pr, pk, pv = ps[:, 0:512], ps[:, 512:1024], ps[:, 1024:1536]
    pw, pa, pg = ps[:, 1536:1664], ps[:, 1664:1792], ps[:, 1792:1920]
    w_log = -_softplus(-(w0_ref[...] + _dot3(jnp.tanh(pw), w2_ref[...]))) - 0.5
    lw_o[...] = -jnp.exp(w_log)
    a = _sigmoid(a0_ref[...] + _dot3(pa, a2_ref[...]))
    g_o[...] = _dot3(_sigmoid(pg), g2_ref[...])
    kk = pk * kk_ref[...]
    ssq = _dot_exact_rhs(kk * kk, ones_ref[...], nsplit=2)
    kkn_o[...] = kk / jnp.maximum(jnp.sqrt(ssq), 1e-12)
    k_o[...] = pk * (1.0 + (a - 1.0) * ka_ref[...])
    r_o[...] = pr
    v_o[...] = pv
    a_o[...] = a

    pq, pkv = p[:, 1920:2176], p[:, 2176:2304]
    pik, piw = p[:, 2304:2432], p[:, 2432:2560]
    q_lat = pq * lax.rsqrt(jnp.mean(pq * pq, axis=-1, keepdims=True) + 1e-6) * qn_ref[...]
    q = _bdot(q_lat, wq_ref[...])
    qabs_o[...] = (_bdot(q, wuk_ref[...]) * (ATT_HEAD ** -0.5 * LOG2E)).astype(BF16)
    c_lat = pkv * lax.rsqrt(jnp.mean(pkv * pkv, axis=-1, keepdims=True) + 1e-6) * kvn_ref[...]
    spos = i * tm + rows
    s_hi = (spos >> 7).astype(F32)
    s_lo = (spos & (LANES - 1)).astype(F32)
    lane_t = lax.broadcasted_iota(I32, (1, LANES), 1)
    extra = jnp.where(lane_t == 0, 1.0, jnp.where((lane_t == 1) | (lane_t == 2), s_hi,
                                                  jnp.where((lane_t == 3) | (lane_t == 4), s_lo, 0.0)))
    c_aug = jnp.concatenate([c_lat, extra], axis=-1)
    cl_o[...] = c_aug.astype(BF16)
    clt_o[...] = jnp.transpose(c_aug).astype(BF16)
    iq_o[...] = (_bdot(q_lat, wiq_ref[...]) * (IDX_DIM ** -0.5)).astype(BF16)
    lane = lax.broadcasted_iota(I32, (1, LANES), 1)
    valid = lane < IDX_DIM
    mu = jnp.sum(pik, axis=-1, keepdims=True) * (1.0 / IDX_DIM)
    dlt = jnp.where(valid, pik - mu, 0.0)
    var = jnp.sum(dlt * dlt, axis=-1, keepdims=True) * (1.0 / IDX_DIM)
    ik_o[...] = jnp.where(valid, dlt * lax.rsqrt(var + 1e-5) * ig_ref[...] + ib_ref[...], 0.0).astype(BF16)
    iw_o[...] = piw * (IDX_HEADS ** -0.5)


def _pad_cols(w, widths):
    parts, o = [], 0
    for true, padded in widths:
        seg = w[..., o:o + true]
        if padded > true:
            seg = jnp.pad(seg, [(0, 0)] * (w.ndim - 1) + [(0, padded - true)])
        parts.append(seg)
        o += true
    return jnp.concatenate(parts, axis=-1)


def _pad_rows(w, rows):
    return jnp.pad(w, ((0, rows - w.shape[0]), (0, 0)))


def _block_diag(blocks):
    H, a, b = blocks.shape
    eye = jnp.eye(H, dtype=blocks.dtype)
    return (eye[:, None, :, None] * blocks[:, :, None, :]).reshape(H * a, H * b)


def _head_ones(n, head):
    idx = np.arange(n) // head
    return jnp.asarray(idx[:, None] == idx[None, :], BF16)


def _in_proj(x, sc1, sh1, w_in, shift_mu, w0, w2, a0, a2, g2, k_k, k_a, q_norm, w_q_up, w_uk, kv_norm, idx_w_q,
             idx_ln_g, idx_ln_b):
    B, S, D = x.shape
    tm = min(TM_PROJ, S)
    widths = tuple((s[1], s[2]) for s in _SEG)
    win_p = _pad_cols(w_in, widths).astype(BF16)
    mu_p = _pad_cols(shift_mu.reshape(1, -1), widths[:6])
    w2_p = _pad_rows(w2, LANES)
    a2_p = _pad_rows(a2, LANES)
    wuk_bd = _block_diag(w_uk).astype(BF16)
    wiq_p = _pad_cols(idx_w_q, ((IDX_DIM, LANES),) * IDX_HEADS).astype(BF16)
    ig_p = _pad_cols(idx_ln_g.reshape(1, -1), ((IDX_DIM, LANES),))
    ib_p = _pad_cols(idx_ln_b.reshape(1, -1), ((IDX_DIM, LANES),))
    row = lambda v: v.reshape(1, -1)
    tok = lambda n: pl.BlockSpec((None, tm, n), lambda b, i: (b, i, 0))
    mod = pl.BlockSpec((None, 1, D), lambda b, i: (b, 0, 0))
    full = lambda a: pl.BlockSpec(a.shape, lambda b, i: (0,) * a.ndim)
    consts = [win_p, mu_p, row(w0), w2_p, row(a0), a2_p, g2, row(k_k), row(k_a), _head_ones(D_RWKV, RWKV_HEAD),
              row(q_norm), w_q_up.astype(BF16), wuk_bd, row(kv_norm), wiq_p, ig_p, ib_p]
    outs = [(D_RWKV, F32)] * 7 + [(N_ATT_HEADS * KV_LORA, BF16), (IDX_HEADS * LANES, BF16), (LANES, BF16),
                                  (LANES, F32), (KV_LORA + LANES, BF16)]
    return pl.pallas_call(
        _in_proj_kernel,
        grid=(B, S // tm),
        in_specs=[tok(D), mod, mod] + [full(a) for a in consts],
        out_specs=[tok(n) for n, _ in outs] + [pl.BlockSpec((None, KV_LORA + LANES, tm), lambda b, i: (b, 0, i))],
        out_shape=[jax.ShapeDtypeStruct((B, S, n), dt) for n, dt in outs]
        + [jax.ShapeDtypeStruct((B, KV_LORA + LANES, S), BF16)],
        scratch_shapes=[pltpu.VMEM((8, N_SHIFT_P), F32)],
        compiler_params=_cparams(("arbitrary", "arbitrary")),
        name="in_proj",
    )(x, sc1, sh1, *consts)


def _rwkv_kernel(r_ref, lw_ref, k_ref, v_ref, kk_ref, a_ref, g_ref, rk_ref, lnw_ref, lnb_ref, tri_ref, ones_ref,
                 o_ref, state, *, L):
    c = pl.program_id(1)
    nc = r_ref.shape[0] // L

    @pl.when(c == 0)
    def _():
        state[...] = jnp.zeros_like(state)

    r, lw, k, v, kk, a = r_ref[...], lw_ref[...], k_ref[...], v_ref[...], kk_ref[...], a_ref[...]
    cum = _dot_exact_lhs(tri_ref[...], lw)
    rows = [slice(ci * L, (ci + 1) * L) for ci in range(nc)]
    last = [cum[(ci + 1) * L - 1:(ci + 1) * L, :] for ci in range(nc)]
    cum_last = jnp.concatenate([jnp.broadcast_to(z, (L, z.shape[1])) for z in last], axis=0)
    w_incl = jnp.exp(cum)
    w_inv = jnp.exp(-cum)
    w_rel = jnp.exp(cum_last - cum)
    w_last = [jnp.exp(z) for z in last]
    bvec = kk * a
    at = -kk * jnp.exp(cum - lw)
    rt = r * w_incl
    bt = bvec * w_inv
    kt = k * w_inv
    bh = bvec * w_rel
    kh = k * w_rel
    ti = lax.broadcasted_iota(I32, (L, L), 0)
    tj = lax.broadcasted_iota(I32, (L, L), 1)
    strict = tj < ti
    incl = tj <= ti
    eye = lax.broadcasted_iota(I32, (RWKV_HEAD, RWKV_HEAD), 0) == lax.broadcasted_iota(I32, (RWKV_HEAD, RWKV_HEAD), 1)
    NT = (((1,), (1,)), ((), ()))
    TN = (((0,), (0,)), ((), ()))
    heads = range(N_RWKV_HEADS)
    sls = [slice(h * RWKV_HEAD, (h + 1) * RWKV_HEAD) for h in heads]
    units = [(ci, h) for ci in range(nc) for h in heads]
    idx = range(len(units))
    mm = lambda x, y, dims=(((1,), (0,)), ((), ())): lax.dot_general(
        x.astype(BF16), y.astype(BF16), dims, preferred_element_type=F32)
    at_b, rt_b, bt_b, kt_b, bh_b, kh_b, v_b = [z.astype(BF16) for z in (at, rt, bt, kt, bh, kh, v)]
    cut = lambda z, u: z[rows[u[0]], sls[u[1]]]
    vh = [cut(v_b, u) for u in units]
    ath = [cut(at_b, u) for u in units]
    ar = [jnp.concatenate([cut(at_b, u), cut(rt_b, u)], axis=0) for u in units]
    g_b = [mm(ar[i], cut(bt_b, units[i]), NT) for i in idx]
    g_k = [mm(ar[i], cut(kt_b, units[i]), NT) for i in idx]
    n_ab = [jnp.where(strict, g_b[i][:L], 0.0) for i in idx]
    a_ak = [jnp.where(strict, g_k[i][:L], 0.0) for i in idx]
    a_rb = [jnp.where(incl, g_b[i][L:], 0.0).astype(BF16) for i in idx]
    a_rk = [jnp.where(incl, g_k[i][L:], 0.0) for i in idx]
    akv = [mm(a_ak[i], vh[i]) for i in idx]
    eye_l = jnp.where(ti == tj, 1.0, 0.0)
    tinv = [eye_l + n_ab[i] for i in idx]
    pw = n_ab
    for _ in range(int(math.log2(L)) - 1):
        pw = [mm(pw[i], pw[i]) for i in idx]
        tinv = [tinv[i] + mm(pw[i], tinv[i]) for i in idx]
    tinv = [t.astype(BF16) for t in tinv]
    a_t = [mm(tinv[i], ath[i]).astype(BF16) for i in idx]
    y = [mm(tinv[i], akv[i]).astype(BF16) for i in idx]
    m_c = [jnp.where(eye, w_last[units[i][0]][:, sls[units[i][1]]], 0.0) + mm(a_t[i], cut(bh_b, units[i]), TN)
           for i in idx]
    c_c = [mm(y[i], cut(bh_b, units[i]), TN) + mm(vh[i], cut(kh_b, units[i]), TN) for i in idx]
    q_c = [cut(rt, units[i]) + mm(a_rb[i], a_t[i]) for i in idx]
    o_loc = [mm(a_rb[i], y[i]) + mm(a_rk[i], vh[i]) for i in idx]
    s = [state[h] for h in heads]
    for ci in range(nc):
        for h in heads:
            i = ci * N_RWKV_HEADS + h
            o = o_loc[i] + _dot3(q_c[i], s[h], NT)
            s[h] = _dot3(s[h], m_c[i]) + c_c[i]
            mu = jnp.mean(o, axis=-1, keepdims=True)
            d = o - mu
            var = jnp.mean(d * d, axis=-1, keepdims=True)
            o_ref[rows[ci], sls[h]] = d * lax.rsqrt(var + RWKV_GN_EPS)
    for h in heads:
        state[h] = s[h]
    bonus = _dot_exact_rhs(r * k * rk_ref[...], ones_ref[...], nsplit=3) * v
    o_ref[...] = (o_ref[...] * lnw_ref[...] + lnb_ref[...] + bonus) * g_ref[...]


def _rwkv_scan(r, lw, k, v, kk, a, g, r_k, ln_w, ln_b):
    B, S, DR = r.shape
    L = min(L_CHUNK, S)
    lb = min(L * CHUNKS_PER_STEP, S)
    tri = jnp.asarray(np.kron(np.eye(lb // L), np.tril(np.ones((L, L)))), BF16)
    row = lambda z: z.reshape(1, -1)
    tok = pl.BlockSpec((None, lb, DR), lambda b, c: (b, c, 0))
    full = lambda z: pl.BlockSpec(z.shape, lambda b, c: (0,) * z.ndim)
    consts = [row(r_k), row(ln_w), row(ln_b), tri, _head_ones(DR, RWKV_HEAD)]
    return pl.pallas_call(
        functools.partial(_rwkv_kernel, L=L),
        grid=(B, S // lb),
        in_specs=[tok] * 7 + [full(z) for z in consts],
        out_specs=tok,
        out_shape=jax.ShapeDtypeStruct((B, S, DR), F32),
        scratch_shapes=[pltpu.VMEM((N_RWKV_HEADS, RWKV_HEAD, RWKV_HEAD), F32)],
        compiler_params=_cparams(("arbitrary", "arbitrary")),
        name="rwkv_scan",
    )(r, lw, k, v, kk, a, g, *consts)


def _alibi_cols():
    slope = np.asarray([2.0 ** (-8.0 * (h + 1) / N_ATT_HEADS) * LOG2E for h in range(N_ATT_HEADS)], np.float32)
    c_hi = slope.astype(BF16).astype(np.float32)
    c_lo = (slope - c_hi).astype(BF16).astype(np.float32)
    t = np.zeros((N_ATT_HEADS, LANES), np.float32)
    t[:, 1], t[:, 2], t[:, 3], t[:, 4] = LANES * c_hi, LANES * c_lo, c_hi, c_lo
    return jnp.asarray(t.astype(BF16))


def _dsa_kernel(iq_ref, iw_ref, qa_ref, ik_ref, ca_ref, cat_ref, wuv_ref, tril_ref, acol_ref, o_ref, key_ref,
                bias_ref, lg_ref, p_ref, *, topk, q_off, select):
    qi = pl.program_id(1) + q_off
    tq = qa_ref.shape[0]
    sk = ca_ref.shape[0]
    kc = min(KEY_CHUNK, sk)
    chunks = [slice(c * kc, (c + 1) * kc) for c in range(sk // kc)]
    tpos = qi * tq + lax.broadcasted_iota(I32, (1, tq), 1)
    srow = lax.broadcasted_iota(I32, (kc, tq), 0)
    NT = (((1,), (1,)), ((), ()))

    if not select:
        for c, cs in enumerate(chunks):
            bias_ref[cs, :] = jnp.where(srow + c * kc <= tpos, 0.0, NEG_BIG)
    else:
        iw_t = jnp.transpose(iw_ref[...])
        for c, cs in enumerate(chunks):
            ikc = ik_ref[cs, :]
            score = jnp.zeros((kc, tq), F32)
            for h in range(IDX_HEADS):
                s = lax.dot_general(ikc, iq_ref[:, h * LANES:(h + 1) * LANES], NT, preferred_element_type=F32)
                score = score + iw_t[h:h + 1, :] * jnp.maximum(s, 0.0)
            bits = pltpu.bitcast(score + 0.0, I32)
            key = bits ^ ((bits >> 31) & 0x7FFFFFFF)
            key_ref[cs, :] = jnp.where(srow + c * kc <= tpos, key, INT_MIN)
        kcount = jnp.minimum(topk, tpos + 1).astype(F32)

        def sum_keys(x):
            part = jnp.sum(x.reshape(x.shape[0] // SUM_ROWS, SUM_ROWS, x.shape[1]), axis=0)
            return jnp.sum(part, axis=0, keepdims=True)

        def count_ge(cand):
            return sum_keys(jnp.where(key_ref[...] >= cand, 1.0, 0.0))

        thr = jnp.where(count_ge(jnp.zeros((1, tq), I32)) >= kcount, 0, INT_MIN).astype(I32)

        def bit_step(i, thr):
            cand = thr | (1 << (30 - i))
            return jnp.where(count_ge(cand) >= kcount, cand, thr)

        thr = lax.fori_loop(0, 31, bit_step, thr)

        need = kcount - sum_keys(jnp.where(key_ref[...] > thr, 1.0, 0.0))
        carry = jnp.zeros((1, tq), F32)
        for j in range(sk // LANES):
            sl = slice(j * LANES, (j + 1) * LANES)
            key = key_ref[sl, :]
            eq = key == thr
            e = jnp.where(eq, 1.0, 0.0)
            before = jnp.dot(tril_ref[...], e.astype(BF16), preferred_element_type=F32) + carry
            sel = (key > thr) | (eq & (before < need))
            bias_ref[sl, :] = jnp.where(sel, 0.0, NEG_BIG)
            carry = carry + jnp.sum(e, axis=0, keepdims=True)

    outs = []
    for h in range(N_ATT_HEADS):
        q_aug = jnp.concatenate([qa_ref[:, h * KV_LORA:(h + 1) * KV_LORA],
                                 jnp.broadcast_to(acol_ref[h:h + 1, :], (tq, LANES))], axis=-1)
        lg_all = lax.dot_general(ca_ref[...], q_aug, NT, preferred_element_type=F32)
        m = jnp.full((1, tq), -jnp.inf, F32)
        for cs in chunks:
            lg = lg_all[cs, :] + bias_ref[cs, :]
            lg_ref[cs, :] = lg
            m = jnp.maximum(m, jnp.max(lg, axis=0, keepdims=True))
        for cs in chunks:
            p_ref[cs, :] = jnp.exp2(lg_ref[cs, :] - m).astype(BF16)
        pv = jnp.dot(cat_ref[...], p_ref[...], preferred_element_type=F32)
        outs.append((pv[:KV_LORA, :] / pv[KV_LORA:KV_LORA + 1, :]).astype(BF16))
    o_lat_t = jnp.concatenate(outs, axis=0)
    o_ref[...] = lax.dot_general(o_lat_t, wuv_ref[...], (((0,), (0,)), ((), ())), preferred_element_type=F32)


def _dsa_attn(iq, iw, qabs, ik, ca, cat, w_uv):
    B, S, _ = iq.shape
    tq = min(TQ, S)
    topk = min(TOPK_MAX, S // 4)
    wuv_bd = _block_diag(w_uv).astype(BF16)
    tril = jnp.asarray(np.tril(np.ones((LANES, LANES)), -1), BF16)
    acol = _alibi_cols()
    full = lambda z: pl.BlockSpec(z.shape, lambda b, i: (0,) * z.ndim)
    nq = 1
    outs = []
    for q_off in range(0, S // tq, nq):
        sk = (q_off + nq) * tq
        tok = lambda n, q_off=q_off: pl.BlockSpec((None, tq, n), lambda b, i: (b, i + q_off, 0))
        seq = lambda n, sk=sk: pl.BlockSpec((None, sk, n), lambda b, i: (b, 0, 0))
        outs.append(pl.pallas_call(
            functools.partial(_dsa_kernel, topk=topk, q_off=q_off, select=sk > topk),
            grid=(B, nq),
            in_specs=[tok(IDX_HEADS * LANES), tok(LANES), tok(N_ATT_HEADS * KV_LORA), seq(LANES), seq(KV_LORA + LANES),
                      pl.BlockSpec((None, KV_LORA + LANES, sk), lambda b, i: (b, 0, 0)),
                      full(wuv_bd), full(tril), full(acol)],
            out_specs=pl.BlockSpec((None, tq, D_ATT), lambda b, i: (b, i, 0)),
            out_shape=jax.ShapeDtypeStruct((B, nq * tq, D_ATT), F32),
            scratch_shapes=[pltpu.VMEM((sk, tq), I32), pltpu.VMEM((sk, tq), F32), pltpu.VMEM((sk, tq), F32),
                            pltpu.VMEM((sk, tq), BF16)],
            compiler_params=_cparams(("arbitrary", "arbitrary")),
            name=f"dsa_attn_k{sk}",
        )(iq, iw, qabs, ik, ca, cat, wuv_bd, tril, acol))
    return jnp.concatenate(outs, axis=1)


def _layernorm_rows(y, g, b):
    mu = jnp.mean(y, axis=-1, keepdims=True)
    d = y - mu
    var = jnp.mean(d * d, axis=-1, keepdims=True)
    return d * lax.rsqrt(var + 1e-5) * g + b


def _mix_kernel(orw_ref, ods_ref, x_ref, gt_ref, sc_ref, sh_ref, wtop_ref, wbot_ref, g_ref, b_ref, wr_ref, br_ref,
                tril_ref, triu_ref, x1_o, u2_o, route_o, tbl_o, cnt_o, carry, *, alpha):
    first = (pl.program_id(0) == 0) & (pl.program_id(1) == 0)
    tm = x_ref.shape[0]

    @pl.when(first)
    def _():
        carry[...] = jnp.zeros_like(carry)

    mix = _bdot(orw_ref[...], wtop_ref[...]) + _bdot(ods_ref[...], wbot_ref[...])
    x1 = _layernorm_rows(alpha * x_ref[...] + (1.0 + gt_ref[...]) * mix, g_ref[...], b_ref[...])
    x1_o[...] = x1
    u2 = x1 * (1.0 + sc_ref[...]) + sh_ref[...]
    u2_o[...] = u2

    lg = _dot3(u2, wr_ref[...]) + br_ref[...]
    lane = lax.broadcasted_iota(I32, (tm, LANES), 1)
    idxs, vals = [], []
    for _ in range(TOP_K_EXPERTS):
        m = jnp.max(lg, axis=-1, keepdims=True)
        idx = jnp.min(jnp.where(lg == m, lane, LANES), axis=-1, keepdims=True)
        idxs.append(idx)
        vals.append(m)
        lg = jnp.where(lane == idx, -jnp.inf, lg)
    es = [jnp.exp(v - vals[0]) for v in vals]
    den = es[0] + es[1] + es[2] + es[3]
    hot = jnp.zeros((tm, LANES), F32)
    for idx in idxs:
        hot = hot + jnp.where(lane == idx, 1.0, 0.0)
    before = jnp.dot(tril_ref[...], hot.astype(BF16), preferred_element_type=F32)
    cnt = jnp.sum(hot, axis=0, keepdims=True)
    n_run = jnp.floor((cnt + (RUN_ALIGN - 1)) * (1.0 / RUN_ALIGN))
    n_stage = jnp.floor((n_run * RUN_ALIGN + (STAGE_ALIGN - 1)) * (1.0 / STAGE_ALIGN))
    off = jnp.dot(jnp.broadcast_to(n_stage, (SUBLANES, LANES)).astype(BF16), triu_ref[...],
                  preferred_element_type=F32)[0:1, :] * STAGE_ALIGN
    where_in_stage = off + before
    route = jnp.zeros((tm, LANES), F32)
    for k in range(TOP_K_EXPERTS):
        spos = jnp.sum(jnp.where(lane == idxs[k], where_in_stage, 0.0), axis=-1, keepdims=True)
        route = jnp.where(lane == k, idxs[k].astype(F32), route)
        route = jnp.where(lane == TOP_K_EXPERTS + k, es[k] / den, route)
        route = jnp.where(lane == 2 * TOP_K_EXPERTS + k, spos, route)
    route_o[...] = route
    sub = lax.broadcasted_iota(I32, (SUBLANES, LANES), 0)
    tbl_o[...] = jnp.where(sub == 0, n_run, jnp.where(sub == 1, off, jnp.where(sub == 2, carry[0:1, :], 0.0)))
    carry[0:1, :] = carry[0:1, :] + n_run * RUN_ALIGN
    cnt_o[...] = carry[...]


def _mix_out(o_rwkv, o_dsa, x, gt1, sc2, sh2, w_out, ln_g, ln_b, w_router, b_router):
    B, S, D = x.shape
    tm = min(TM_PROJ, S)
    alpha = 2.0 ** 0.25
    wtop = w_out[:D_RWKV].astype(BF16)
    wbot = w_out[D_RWKV:].astype(BF16)
    wr_p = jnp.pad(w_router, ((0, 0), (0, LANES - N_EXPERTS)))
    br_p = jnp.pad(b_router.reshape(1, -1), ((0, 0), (0, LANES - N_EXPERTS)), constant_values=NEG_BIG)
    tril = jnp.asarray(np.tril(np.ones((tm, tm)), -1), BF16)
    triu = jnp.asarray(np.triu(np.ones((LANES, LANES)), 1), BF16)
    row = lambda v: v.reshape(1, -1)
    tok = lambda n: pl.BlockSpec((None, tm, n), lambda b, i: (b, i, 0))
    mod = pl.BlockSpec((None, 1, D), lambda b, i: (b, 0, 0))
    full = lambda a: pl.BlockSpec(a.shape, lambda b, i: (0,) * a.ndim)
    consts = [wtop, wbot, row(ln_g), row(ln_b), wr_p, br_p, tril, triu]
    return pl.pallas_call(
        functools.partial(_mix_kernel, alpha=alpha),
        grid=(B, S // tm),
        in_specs=[tok(D_RWKV), tok(D_ATT), tok(D), mod, mod, mod] + [full(a) for a in consts],
        out_specs=[tok(D), tok(D), tok(LANES), pl.BlockSpec((None, None, SUBLANES, LANES), lambda b, i: (b, i, 0, 0)),
                   pl.BlockSpec((SUBLANES, LANES), lambda b, i: (0, 0))],
        out_shape=[jax.ShapeDtypeStruct((B, S, D), F32), jax.ShapeDtypeStruct((B, S, D), F32),
                   jax.ShapeDtypeStruct((B, S, LANES), F32),
                   jax.ShapeDtypeStruct((B, S // tm, SUBLANES, LANES), F32),
                   jax.ShapeDtypeStruct((SUBLANES, LANES), F32)],
        scratch_shapes=[pltpu.VMEM((8, LANES), F32)],
        compiler_params=_cparams(("arbitrary", "arbitrary")),
        name="mix_out",
    )(o_rwkv, o_dsa, x, gt1, sc2, sh2, *consts)


RUN_ALIGN = SUBLANES
STAGE_ALIGN = 16
STAGE_ROWS = 1536
STAGE_CHUNK = 256


def _run_copies(nrun_ref, tile, copy_of):
    for e in range(N_EXPERTS):
        pieces = (nrun_ref[tile * N_EXPERTS + e] * RUN_ALIGN + STAGE_ALIGN - 1) // STAGE_ALIGN

        def piece(j, carry, e=e):
            copy_of(e, j).start()
            return carry

        lax.fori_loop(0, pieces, piece, 0)


def _dispatch_kernel(nrun_ref, off_ref, start_ref, npiece_ref, nstage_ref, zs_ref, zn_ref, tail_ref,
                     u_ref, route_ref, xs_out, stag, zeros, sem, zsem):
    i = pl.program_id(0)
    n = pl.num_programs(0)
    tm = u_ref.shape[0]
    bm = zeros.shape[0]
    slot = i % 2

    @pl.when(i == 0)
    def _():
        zeros[...] = jnp.zeros_like(zeros)
        fills = []
        for e in range(N_EXPERTS):
            for b in range(bm.bit_length()):
                rows = RUN_ALIGN << b
                if rows > bm:
                    break
                done = (zn_ref[e] >> (b + 1)) << (b + 1)
                dst = pl.multiple_of(zs_ref[e] + done * RUN_ALIGN, RUN_ALIGN)
                fills.append(((zn_ref[e] >> b) & 1 == 1,
                              pltpu.make_async_copy(zeros.at[pl.ds(0, rows)], xs_out.at[pl.ds(dst, rows)], zsem)))
        for pred, cp in fills:
            pl.when(pred)(cp.start)

        def tail_copy(j):
            return pltpu.make_async_copy(zeros, xs_out.at[pl.ds(pl.multiple_of(j * bm, bm), bm)], zsem)

        lax.fori_loop(tail_ref[0], tail_ref[1], lambda j, c: (tail_copy(j).start(), c)[1], 0)
        for pred, cp in fills:
            pl.when(pred)(cp.wait)
        lax.fori_loop(tail_ref[0], tail_ref[1], lambda j, c: (tail_copy(j).wait(), c)[1], 0)

    route_t = jnp.transpose(route_ref[...])
    spos = [route_t[2 * TOP_K_EXPERTS + k:2 * TOP_K_EXPERTS + k + 1, :].astype(I32) for k in range(TOP_K_EXPERTS)]
    ub = u_ref[...].astype(BF16)
    srow = lax.broadcasted_iota(I32, (STAGE_CHUNK, tm), 0)
    for c in range(STAGE_ROWS // STAGE_CHUNK):
        @pl.when(c * STAGE_CHUNK < nstage_ref[i])
        def _(c=c):
            rows = srow + c * STAGE_CHUNK
            sel = (rows == spos[0]) | (rows == spos[1]) | (rows == spos[2]) | (rows == spos[3])
            stag[slot, c * STAGE_CHUNK:(c + 1) * STAGE_CHUNK, :] = jnp.dot(
                jnp.where(sel, 1.0, 0.0).astype(BF16), ub, preferred_element_type=F32)

    def piece_copy(s, tile):
        def copy_of(e, j):
            src = pl.multiple_of(off_ref[tile * N_EXPERTS + e] + j * STAGE_ALIGN, STAGE_ALIGN)
            dst = pl.multiple_of(start_ref[tile * N_EXPERTS + e] + j * STAGE_ALIGN, RUN_ALIGN)
            return pltpu.make_async_copy(stag.at[s, pl.ds(src, STAGE_ALIGN)], xs_out.at[pl.ds(dst, STAGE_ALIGN)], sem)
        return copy_of

    def drain(tile):
        def w(j, carry):
            pltpu.make_async_copy(stag.at[0, pl.ds(0, STAGE_ALIGN)], xs_out.at[pl.ds(0, STAGE_ALIGN)], sem).wait()
            return carry
        lax.fori_loop(0, npiece_ref[tile], w, 0)

    @pl.when(i > 0)
    def _():
        drain(i - 1)

    _run_copies(nrun_ref, i, piece_copy(slot, i))

    @pl.when(i == n - 1)
    def _():
        drain(i)


def _moe_dispatch(u2, route, tables, n_rows):
    T, D = u2.shape
    tm = TM_ROUTE
    bm = BM_EXPERT
    return pl.pallas_call(
        _dispatch_kernel,
        grid_spec=pltpu.PrefetchScalarGridSpec(
            num_scalar_prefetch=len(tables),
            grid=(T // tm,),
            in_specs=[pl.BlockSpec((tm, D), lambda i, *_: (i, 0)),
                      pl.BlockSpec((tm, LANES), lambda i, *_: (i, 0))],
            out_specs=pl.BlockSpec(memory_space=pl.ANY),
            scratch_shapes=[pltpu.VMEM((2, STAGE_ROWS, D), F32), pltpu.VMEM((bm, D), F32),
                            pltpu.SemaphoreType.DMA(()), pltpu.SemaphoreType.DMA(())],
        ),
        out_shape=jax.ShapeDtypeStruct((n_rows, D), F32),
        compiler_params=_cparams(("arbitrary",)),
        name="moe_dispatch",
    )(*tables, u2, route)


GU_GROUP = 2 * LANES


def _deinterleave_perm():
    p = np.zeros((GU_GROUP, GU_GROUP), np.float32)
    l = np.arange(LANES)
    p[2 * l, l] = 1.0
    p[2 * l + 1, LANES + l] = 1.0
    return jnp.asarray(p, BF16)


def _expert_kernel(be_ref, nb_ref, xs_ref, wgu_ref, bgu_ref, wd_ref, bd_ref, perm_ref, ys_ref, wp, wdb):
    i = pl.program_id(0)
    used = i < nb_ref[0]
    new_expert = (i == 0) | (be_ref[i] != be_ref[jnp.maximum(i - 1, 0)])
    n_groups = wgu_ref.shape[1] // GU_GROUP

    @pl.when(used & new_expert)
    def _():
        for j in range(n_groups):
            sl = slice(j * GU_GROUP, (j + 1) * GU_GROUP)
            wp[:, sl] = jnp.dot(wgu_ref[:, sl].astype(BF16), perm_ref[...], preferred_element_type=F32).astype(BF16)
        wdb[...] = wd_ref[...].astype(BF16)

    @pl.when(used)
    def _():
        xb = xs_ref[...].astype(BF16)
        gu = jnp.dot(xb, wp[...], preferred_element_type=F32) + bgu_ref[...]
        hs = []
        for j in range(n_groups):
            gate = jnp.minimum(gu[:, j * GU_GROUP:j * GU_GROUP + LANES], SWIGLU_LIMIT)
            up = jnp.clip(gu[:, j * GU_GROUP + LANES:(j + 1) * GU_GROUP], -SWIGLU_LIMIT, SWIGLU_LIMIT)
            hs.append(((up + 1.0) * (gate * _sigmoid(gate * SWIGLU_ALPHA))).astype(BF16))
        h = jnp.concatenate(hs, axis=-1)
        ys_ref[...] = jnp.dot(h, wdb[...], preferred_element_type=F32) + bd_ref[...]

    @pl.when(jnp.logical_not(used))
    def _():
        ys_ref[...] = jnp.zeros_like(ys_ref)


def _moe_experts(xs, block_e, n_used, w_gu, b_gu_p, w_dn, b_dn):
    n_rows, D = xs.shape
    E, _, F2 = w_gu.shape
    bm = BM_EXPERT
    n_blocks = n_rows // bm
    perm = _deinterleave_perm()
    wspec = lambda shp: pl.BlockSpec((None,) + shp, lambda i, be, nb: (be[i], 0, 0))
    return pl.pallas_call(
        _expert_kernel,
        grid_spec=pltpu.PrefetchScalarGridSpec(
            num_scalar_prefetch=2,
            grid=(n_blocks,),
            in_specs=[pl.BlockSpec((bm, D), lambda i, be, nb: (jnp.minimum(i, nb[0] - 1), 0)),
                      wspec((D, F2)), wspec((1, F2)), wspec((F2 // 2, D)), wspec((1, D)),
                      pl.BlockSpec(perm.shape, lambda i, be, nb: (0, 0))],
            out_specs=pl.BlockSpec((bm, D), lambda i, be, nb: (i, 0)),
            scratch_shapes=[pltpu.VMEM((D, F2), BF16), pltpu.VMEM((F2 // 2, D), BF16)],
        ),
        out_shape=jax.ShapeDtypeStruct((n_rows, D), F32),
        compiler_params=pltpu.CompilerParams(dimension_semantics=("arbitrary",), vmem_limit_bytes=VMEM_LIMIT_EXPERTS),
        name="moe_experts",
    )(block_e, n_used, xs, w_gu, b_gu_p, w_dn, b_dn, perm)


def _combine_kernel(nrun_ref, off_ref, start_ref, npiece_ref, nstage_ref, ys_ref, x1_ref, route_ref, gt_ref, g_ref,
                    b_ref, o_ref, stag, sem, *, alpha):
    i = pl.program_id(0)
    n = pl.num_programs(0)
    tm = x1_ref.shape[0]
    slot = i % 2

    def gather(tile, s):
        def copy_of(e, j):
            src = pl.multiple_of(start_ref[tile * N_EXPERTS + e] + j * STAGE_ALIGN, RUN_ALIGN)
            dst = pl.multiple_of(off_ref[tile * N_EXPERTS + e] + j * STAGE_ALIGN, STAGE_ALIGN)
            return pltpu.make_async_copy(ys_ref.at[pl.ds(src, STAGE_ALIGN)], stag.at[s, pl.ds(dst, STAGE_ALIGN)],
                                         sem.at[s])
        _run_copies(nrun_ref, tile, copy_of)

    @pl.when(i == 0)
    def _():
        stag[...] = jnp.zeros_like(stag)
        gather(0, 0)

    @pl.when(i + 1 < n)
    def _():
        gather(i + 1, 1 - slot)

    def w(j, carry):
        pltpu.make_async_copy(ys_ref.at[pl.ds(0, STAGE_ALIGN)], stag.at[slot, pl.ds(0, STAGE_ALIGN)],
                              sem.at[slot]).wait()
        return carry
    lax.fori_loop(0, npiece_ref[i], w, 0)

    route = route_ref[...]
    wide = lambda col: jnp.broadcast_to(col, (tm, STAGE_CHUNK))
    spos = [wide(route[:, 2 * TOP_K_EXPERTS + k:2 * TOP_K_EXPERTS + k + 1].astype(I32)) for k in range(TOP_K_EXPERTS)]
    gate = [wide(route[:, TOP_K_EXPERTS + k:TOP_K_EXPERTS + k + 1]) for k in range(TOP_K_EXPERTS)]
    scol = lax.broadcasted_iota(I32, (tm, STAGE_CHUNK), 1)
    acc_ref = o_ref
    acc_ref[...] = jnp.zeros_like(acc_ref)
    for c in range(STAGE_ROWS // STAGE_CHUNK):
        @pl.when(c * STAGE_CHUNK < nstage_ref[i])
        def _(c=c):
            cols = scol + c * STAGE_CHUNK
            wgt = jnp.zeros((tm, STAGE_CHUNK), F32)
            for k in range(TOP_K_EXPERTS):
                wgt = wgt + jnp.where(cols == spos[k], gate[k], 0.0)
            acc_ref[...] += jnp.dot(wgt.astype(BF16), stag[slot, c * STAGE_CHUNK:(c + 1) * STAGE_CHUNK, :].astype(BF16),
                                    preferred_element_type=F32)
    o_ref[...] = _layernorm_rows(alpha * x1_ref[...] + (1.0 + gt_ref[...]) * acc_ref[...], g_ref[...], b_ref[...])


def _moe_combine(ys, tables, x1, route, gt2, ln_g, ln_b, tiles_per_batch):
    T, D = x1.shape
    tm = TM_ROUTE
    row = lambda v: v.reshape(1, -1)
    return pl.pallas_call(
        functools.partial(_combine_kernel, alpha=2.0 ** 0.25),
        grid_spec=pltpu.PrefetchScalarGridSpec(
            num_scalar_prefetch=len(tables),
            grid=(T // tm,),
            in_specs=[pl.BlockSpec(memory_space=pl.ANY),
                      pl.BlockSpec((tm, D), lambda i, *_: (i, 0)),
                      pl.BlockSpec((tm, LANES), lambda i, *_: (i, 0)),
                      pl.BlockSpec((None, 1, D), lambda i, *_: (i // tiles_per_batch, 0, 0)),
                      pl.BlockSpec((1, D), lambda i, *_: (0, 0)),
                      pl.BlockSpec((1, D), lambda i, *_: (0, 0))],
            out_specs=pl.BlockSpec((tm, D), lambda i, *_: (i, 0)),
            scratch_shapes=[pltpu.VMEM((2, STAGE_ROWS, D), F32), pltpu.SemaphoreType.DMA((2,))],
        ),
        out_shape=jax.ShapeDtypeStruct((T, D), F32),
        compiler_params=_cparams(("arbitrary",)),
        name="moe_combine",
    )(*tables, ys, x1, route, gt2, row(ln_g), row(ln_b))


def _moe_and_norm(x1, u2, route, tbl, totals, gt2, w_gu, b_gu, w_dn, b_dn, ln_g, ln_b):
    B, S, D = x1.shape
    T = B * S
    bm = BM_EXPERT
    tm = TM_ROUTE
    assert T % tm == 0 and TM_PROJ == tm and STAGE_ROWS >= tm * TOP_K_EXPERTS + N_EXPERTS * (STAGE_ALIGN - 1)
    n_tiles = T // tm
    max_rows = T * TOP_K_EXPERTS + n_tiles * N_EXPERTS * (RUN_ALIGN - 1) + N_EXPERTS * STAGE_ALIGN
    n_blocks = -(-max_rows // bm) + N_EXPERTS
    tot = totals[0, :N_EXPERTS].astype(I32)
    padded = (tot + STAGE_ALIGN + bm - 1) // bm * bm
    pad_ends = jnp.cumsum(padded)
    pad_starts = pad_ends - padded
    t3 = tbl.reshape(n_tiles, SUBLANES, LANES)[:, :, :N_EXPERTS].astype(I32)
    nrun, off, base = t3[:, 0, :], t3[:, 1, :], t3[:, 2, :]
    start = pad_starts[None, :] + base
    pieces = (nrun * RUN_ALIGN + STAGE_ALIGN - 1) // STAGE_ALIGN
    npiece = jnp.sum(pieces, axis=1)
    nstage = jnp.sum(pieces, axis=1) * STAGE_ALIGN
    flat = lambda z: z.reshape(-1).astype(I32)
    run_tables = [flat(nrun), flat(off), flat(start), flat(npiece), flat(nstage)]
    zs = pad_starts + tot
    zn = (pad_ends - zs) // RUN_ALIGN
    n_used = (pad_ends[-1:] // bm).astype(I32)
    tail = jnp.concatenate([n_used, jnp.full((1,), n_blocks, I32)])
    blk_row = jnp.arange(n_blocks, dtype=I32) * bm
    block_e = jnp.minimum(jnp.sum((blk_row[:, None] >= pad_ends[None, :]).astype(I32), axis=1), N_EXPERTS - 1)
    E, F2 = b_gu.shape
    b_gu_p = b_gu.reshape(E, F2 // GU_GROUP, LANES, 2).transpose(0, 1, 3, 2).reshape(E, 1, F2)
    route2 = route.reshape(T, LANES)
    xs = _moe_dispatch(u2.reshape(T, D), route2, run_tables + [flat(zs), flat(zn), tail], n_blocks * bm)
    ys = _moe_experts(xs, block_e, n_used, w_gu, b_gu_p, w_dn, b_dn[:, None, :])
    out = _moe_combine(ys, run_tables, x1.reshape(T, D), route2, gt2, ln_g, ln_b, S // tm)
    return out.reshape(B, S, D)


def kernel(x, c, w_ada, b_ada, w_in, shift_mu, rwkv_w0, rwkv_w2, rwkv_a0, rwkv_a2, rwkv_g2, rwkv_k_k, rwkv_k_a, rwkv_r_k, rwkv_ln_w, rwkv_ln_b, mla_q_norm, mla_w_q_up, mla_kv_norm, mla_w_uk, mla_w_uv, idx_w_q, idx_ln_g, idx_ln_b, w_out, ln1_g, ln1_b, w_router, b_router, w_gu, b_gu, w_dn, b_dn, ln2_g, ln2_b):
    depth = w_ada.shape[0]
    assert depth == 1, "DeepNorm constants below are for a single layer"
    l = 0
    mod = _ada_mod(c, w_ada[l], b_ada[l])
    sh1, sc1, gt1, sh2, sc2, gt2 = [m[:, None, :] for m in jnp.split(mod, 6, axis=-1)]
    r, lw, k, v, kk, a, g, qabs, iq, ik, iw, cl, clt = _in_proj(
        x, sc1, sh1, w_in[l], shift_mu[l], rwkv_w0[l], rwkv_w2[l], rwkv_a0[l], rwkv_a2[l], rwkv_g2[l], rwkv_k_k[l],
        rwkv_k_a[l], mla_q_norm[l], mla_w_q_up[l], mla_w_uk[l], mla_kv_norm[l], idx_w_q[l], idx_ln_g[l], idx_ln_b[l])
    o_rwkv = _rwkv_scan(r, lw, k, v, kk, a, g, rwkv_r_k[l], rwkv_ln_w[l], rwkv_ln_b[l])
    o_dsa = _dsa_attn(iq, iw, qabs, ik, cl, clt, mla_w_uv[l])
    x1, u2, route, tbl, totals = _mix_out(o_rwkv, o_dsa, x, gt1, sc2, sh2, w_out[l], ln1_g[l], ln1_b[l],
                                          w_router[l], b_router[l])
    return _moe_and_norm(x1, u2, route, tbl, totals, gt2, w_gu[l], b_gu[l], w_dn[l], b_dn[l], ln2_g[l], ln2_b[l])
```

```python
import functools
import math

import jax
import jax.numpy as jnp
import numpy as np
from jax import lax
from jax.experimental import pallas as pl
from jax.experimental.pallas import tpu as pltpu

F32 = jnp.float32
BF16 = jnp.bfloat16
I32 = jnp.int32

RWKV_HEAD = 64
N_RWKV_HEADS = 8
D_RWKV = RWKV_HEAD * N_RWKV_HEADS
RANK_DECAY = 64
RANK_A = 64
RANK_GATE = 128
RWKV_GN_EPS = 64e-5
ATT_HEAD = 64
N_ATT_HEADS = 8
D_ATT = ATT_HEAD * N_ATT_HEADS
Q_LORA = 256
KV_LORA = 128
IDX_HEADS = 8
IDX_DIM = 64
TOPK_MAX = 256
N_EXPERTS = 32
TOP_K_EXPERTS = 4
SWIGLU_LIMIT = 7.0
SWIGLU_ALPHA = 1.702
NEG_BIG = -1e30
LOG2E = 1.4426950408889634
INT_MIN = -(2 ** 31)

LANES = 128
SUBLANES = 8
VMEM_LIMIT = 48 * 1024 * 1024
VMEM_LIMIT_EXPERTS = 58 * 1024 * 1024

TM_IN = 512
TM_PROJ = 256
L_CHUNK = 64
CHUNKS_PER_STEP = 8
TQ = 256
KEY_CHUNK = 128
SUM_ROWS = 64
TM_ROUTE = 256
BM_EXPERT = 512

_SEG = (("r", 512, 512), ("k", 512, 512), ("v", 512, 512), ("w", 64, 128), ("a", 64, 128), ("g", 128, 128),
        ("q", 256, 256), ("kv", 128, 128), ("ik", 64, 128), ("iw", 8, 128))
N_SHIFT_P = 512 * 3 + 128 * 3
N_IN_P = sum(s[2] for s in _SEG)


def _cparams(sem):
    return pltpu.CompilerParams(dimension_semantics=sem, vmem_limit_bytes=VMEM_LIMIT)


def _bdot(a, b):
    return jnp.dot(a.astype(BF16), b.astype(BF16), preferred_element_type=F32)


def _bdot_nt(a, b):
    return lax.dot_general(a.astype(BF16), b.astype(BF16), (((1,), (1,)), ((), ())), preferred_element_type=F32)


def _bdot_tn(a, b):
    return lax.dot_general(a.astype(BF16), b.astype(BF16), (((0,), (0,)), ((), ())), preferred_element_type=F32)


def _split2(a):
    hi = a.astype(BF16)
    lo = (a - hi.astype(F32)).astype(BF16)
    return hi, lo


def _split3(a):
    hi = a.astype(BF16)
    r1 = a - hi.astype(F32)
    mid = r1.astype(BF16)
    lo = (r1 - mid.astype(F32)).astype(BF16)
    return hi, mid, lo


def _dot3(a, b, dims=(((1,), (0,)), ((), ()))):
    ah, al = _split2(a)
    bh, bl = _split2(b)
    d = functools.partial(lax.dot_general, dimension_numbers=dims, preferred_element_type=F32)
    return d(ah, bh) + (d(ah, bl) + d(al, bh))


def _dot_exact_rhs(a, b_exact, nsplit=3):
    parts = _split3(a) if nsplit == 3 else _split2(a)
    acc = None
    for p in parts[::-1]:
        t = jnp.dot(p, b_exact, preferred_element_type=F32)
        acc = t if acc is None else acc + t
    return acc


def _dot_exact_lhs(a_exact, b, nsplit=3):
    parts = _split3(b) if nsplit == 3 else _split2(b)
    acc = None
    for p in parts[::-1]:
        t = jnp.dot(a_exact, p, preferred_element_type=F32)
        acc = t if acc is None else acc + t
    return acc


def _sigmoid(x):
    return 1.0 / (1.0 + jnp.exp(-x))


def _softplus(x):
    return jnp.maximum(x, 0.0) + jnp.log(1.0 + jnp.exp(-jnp.abs(x)))


def _ada_kernel(c_ref, w_ref, b_ref, o_ref):
    c = c_ref[...]
    o_ref[...] = _dot3(c * _sigmoid(c), w_ref[...]) + b_ref[...]


def _ada_mod(c, w_ada, b_ada):
    B, D = c.shape
    N = w_ada.shape[1]
    tn = 1024
    return pl.pallas_call(
        _ada_kernel,
        grid=(N // tn,),
        in_specs=[pl.BlockSpec((B, D), lambda j: (0, 0)),
                  pl.BlockSpec((D, tn), lambda j: (0, j)),
                  pl.BlockSpec((1, tn), lambda j: (0, j))],
        out_specs=pl.BlockSpec((B, tn), lambda j: (0, j)),
        out_shape=jax.ShapeDtypeStruct((B, N), F32),
        compiler_params=_cparams(("arbitrary",)),
        name="ada_mod",
    )(c, w_ada, b_ada.reshape(1, N))


def _in_proj_kernel(x_ref, sc_ref, sh_ref, win_ref, mu_ref, w0_ref, w2_ref, a0_ref, a2_ref, g2_ref, kk_ref, ka_ref,
                    ones_ref, qn_ref, wq_ref, wuk_ref, kvn_ref, wiq_ref, ig_ref, ib_ref,
                    r_o, lw_o, k_o, v_o, kkn_o, a_o, g_o, qabs_o, iq_o, ik_o, iw_o, cl_o, clt_o, carry):
    i = pl.program_id(1)
    tm = x_ref.shape[0]

    @pl.when(i == 0)
    def _():
        carry[...] = jnp.zeros_like(carry)

    u = x_ref[...] * (1.0 + sc_ref[...]) + sh_ref[...]
    p = _bdot(u, win_ref[...])
    ps = p[:, :N_SHIFT_P]
    rows = lax.broadcasted_iota(I32, (tm, 1), 0)
    prev = jnp.where(rows == 0, carry[0:1, :], pltpu.roll(ps, 1, 0))
    carry[0:1, :] = ps[tm - 1:tm, :]
    ps = ps + mu_ref[...] * (prev - ps)

    pr, pk, pv = ps[:, 0:512], ps[:, 512:1024], ps[:, 1024:1536]
    pw, pa, pg = ps[:, 1536:1664], ps[:, 1664:1792], ps[:, 1792:1920]
    w_log = -_softplus(-(w0_ref[...] + _dot3(jnp.tanh(pw), w2_ref[...]))) - 0.5
    lw_o[...] = -jnp.exp(w_log)
    a = _sigmoid(a0_ref[...] + _dot3(pa, a2_ref[...]))
    g_o[...] = _dot3(_sigmoid(pg), g2_ref[...])
    kk = pk * kk_ref[...]
    ssq = _dot_exact_rhs(kk * kk, ones_ref[...], nsplit=2)
    kkn_o[...] = kk / jnp.maximum(jnp.sqrt(ssq), 1e-12)
    k_o[...] = pk * (1.0 + (a - 1.0) * ka_ref[...])
    r_o[...] = pr
    v_o[...] = pv
    a_o[...] = a

    pq, pkv = p[:, 1920:2176], p[:, 2176:2304]
    pik, piw = p[:, 2304:2432], p[:, 2432:2560]
    q_lat = pq * lax.rsqrt(jnp.mean(pq * pq, axis=-1, keepdims=True) + 1e-6) * qn_ref[...]
    q = _bdot(q_lat, wq_ref[...])
    qabs_o[...] = (_bdot(q, wuk_ref[...]) * (ATT_HEAD ** -0.5 * LOG2E)).astype(BF16)
    c_lat = pkv * lax.rsqrt(jnp.mean(pkv * pkv, axis=-1, keepdims=True) + 1e-6) * kvn_ref[...]
    spos = i * tm + rows
    s_hi = (spos >> 7).astype(F32)
    s_lo = (spos & (LANES - 1)).astype(F32)
    lane_t = lax.broadcasted_iota(I32, (1, LANES), 1)
    extra = jnp.where(lane_t == 0, 1.0, jnp.where((lane_t == 1) | (lane_t == 2), s_hi,
                                                  jnp.where((lane_t == 3) | (lane_t == 4), s_lo, 0.0)))
    c_aug = jnp.concatenate([c_lat, extra], axis=-1)
    cl_o[...] = c_aug.astype(BF16)
    clt_o[...] = jnp.transpose(c_aug).astype(BF16)
    iq_o[...] = (_bdot(q_lat, wiq_ref[...]) * (IDX_DIM ** -0.5)).astype(BF16)
    lane = lax.broadcasted_iota(I32, (1, LANES), 1)
    valid = lane < IDX_DIM
    mu = jnp.sum(pik, axis=-1, keepdims=True) * (1.0 / IDX_DIM)
    dlt = jnp.where(valid, pik - mu, 0.0)
    var = jnp.sum(dlt * dlt, axis=-1, keepdims=True) * (1.0 / IDX_DIM)
    ik_o[...] = jnp.where(valid, dlt * lax.rsqrt(var + 1e-5) * ig_ref[...] + ib_ref[...], 0.0).astype(BF16)
    iw_o[...] = piw * (IDX_HEADS ** -0.5)


def _pad_cols(w, widths):
    parts, o = [], 0
    for true, padded in widths:
        seg = w[..., o:o + true]
        if padded > true:
            seg = jnp.pad(seg, [(0, 0)] * (w.ndim - 1) + [(0, padded - true)])
        parts.append(seg)
        o += true
    return jnp.concatenate(parts, axis=-1)


def _pad_rows(w, rows):
    return jnp.pad(w, ((0, rows - w.shape[0]), (0, 0)))


def _block_diag(blocks):
    H, a, b = blocks.shape
    eye = jnp.eye(H, dtype=blocks.dtype)
    return (eye[:, None, :, None] * blocks[:, :, None, :]).reshape(H * a, H * b)


def _head_ones(n, head):
    idx = np.arange(n) // head
    return jnp.asarray(idx[:, None] == idx[None, :], BF16)


def _in_proj(x, sc1, sh1, w_in, shift_mu, w0, w2, a0, a2, g2, k_k, k_a, q_norm, w_q_up, w_uk, kv_norm, idx_w_q,
             idx_ln_g, idx_ln_b):
    B, S, D = x.shape
    tm = min(TM_IN, S)
    widths = tuple((s[1], s[2]) for s in _SEG)
    win_p = _pad_cols(w_in, widths).astype(BF16)
    mu_p = _pad_cols(shift_mu.reshape(1, -1), widths[:6])
    w2_p = _pad_rows(w2, LANES)
    a2_p = _pad_rows(a2, LANES)
    wuk_bd = _block_diag(w_uk).astype(BF16)
    wiq_p = _pad_cols(idx_w_q, ((IDX_DIM, LANES),) * IDX_HEADS).astype(BF16)
    ig_p = _pad_cols(idx_ln_g.reshape(1, -1), ((IDX_DIM, LANES),))
    ib_p = _pad_cols(idx_ln_b.reshape(1, -1), ((IDX_DIM, LANES),))
    row = lambda v: v.reshape(1, -1)
    tok = lambda n: pl.BlockSpec((None, tm, n), lambda b, i: (b, i, 0))
    mod = pl.BlockSpec((None, 1, D), lambda b, i: (b, 0, 0))
    full = lambda a: pl.BlockSpec(a.shape, lambda b, i: (0,) * a.ndim)
    consts = [win_p, mu_p, row(w0), w2_p, row(a0), a2_p, g2, row(k_k), row(k_a), _head_ones(D_RWKV, RWKV_HEAD),
              row(q_norm), w_q_up.astype(BF16), wuk_bd, row(kv_norm), wiq_p, ig_p, ib_p]
    outs = [(D_RWKV, F32)] * 7 + [(N_ATT_HEADS * KV_LORA, BF16), (IDX_HEADS * LANES, BF16), (LANES, BF16),
                                  (LANES, F32), (KV_LORA + LANES, BF16)]
    return pl.pallas_call(
        _in_proj_kernel,
        grid=(B, S // tm),
        in_specs=[tok(D), mod, mod] + [full(a) for a in consts],
        out_specs=[tok(n) for n, _ in outs] + [pl.BlockSpec((None, KV_LORA + LANES, tm), lambda b, i: (b, 0, i))],
        out_shape=[jax.ShapeDtypeStruct((B, S, n), dt) for n, dt in outs]
        + [jax.ShapeDtypeStruct((B, KV_LORA + LANES, S), BF16)],
        scratch_shapes=[pltpu.VMEM((8, N_SHIFT_P), F32)],
        compiler_params=_cparams(("arbitrary", "arbitrary")),
        name="in_proj",
    )(x, sc1, sh1, *consts)


def _rwkv_kernel(r_ref, lw_ref, k_ref, v_ref, kk_ref, a_ref, g_ref, rk_ref, lnw_ref, lnb_ref, tri_ref, ones_ref,
                 o_ref, state, *, L):
    c = pl.program_id(1)
    nc = r_ref.shape[0] // L

    @pl.when(c == 0)
    def _():
        state[...] = jnp.zeros_like(state)

    r, lw, k, v, kk, a = r_ref[...], lw_ref[...], k_ref[...], v_ref[...], kk_ref[...], a_ref[...]
    cum = _dot_exact_lhs(tri_ref[...], lw)
    rows = [slice(ci * L, (ci + 1) * L) for ci in range(nc)]
    last = [cum[(ci + 1) * L - 1:(ci + 1) * L, :] for ci in range(nc)]
    cum_last = jnp.concatenate([jnp.broadcast_to(z, (L, z.shape[1])) for z in last], axis=0)
    w_incl = jnp.exp(cum)
    w_inv = jnp.exp(-cum)
    w_rel = jnp.exp(cum_last - cum)
    w_last = [jnp.exp(z) for z in last]
    bvec = kk * a
    at = -kk * jnp.exp(cum - lw)
    rt = r * w_incl
    bt = bvec * w_inv
    kt = k * w_inv
    bh = bvec * w_rel
    kh = k * w_rel
    ti = lax.broadcasted_iota(I32, (L, L), 0)
    tj = lax.broadcasted_iota(I32, (L, L), 1)
    strict = tj < ti
    incl = tj <= ti
    eye = lax.broadcasted_iota(I32, (RWKV_HEAD, RWKV_HEAD), 0) == lax.broadcasted_iota(I32, (RWKV_HEAD, RWKV_HEAD), 1)
    NT = (((1,), (1,)), ((), ()))
    TN = (((0,), (0,)), ((), ()))
    heads = range(N_RWKV_HEADS)
    sls = [slice(h * RWKV_HEAD, (h + 1) * RWKV_HEAD) for h in heads]
    units = [(ci, h) for ci in range(nc) for h in heads]
    idx = range(len(units))
    mm = lambda x, y, dims=(((1,), (0,)), ((), ())): lax.dot_general(
        x.astype(BF16), y.astype(BF16), dims, preferred_element_type=F32)
    at_b, rt_b, bt_b, kt_b, bh_b, kh_b, v_b = [z.astype(BF16) for z in (at, rt, bt, kt, bh, kh, v)]
    cut = lambda z, u: z[rows[u[0]], sls[u[1]]]
    vh = [cut(v_b, u) for u in units]
    ath = [cut(at_b, u) for u in units]
    ar = [jnp.concatenate([cut(at_b, u), cut(rt_b, u)], axis=0) for u in units]
    g_b = [mm(ar[i], cut(bt_b, units[i]), NT) for i in idx]
    g_k = [mm(ar[i], cut(kt_b, units[i]), NT) for i in idx]
    n_ab = [jnp.where(strict, g_b[i][:L], 0.0) for i in idx]
    a_ak = [jnp.where(strict, g_k[i][:L], 0.0) for i in idx]
    a_rb = [jnp.where(incl, g_b[i][L:], 0.0).astype(BF16) for i in idx]
    a_rk = [jnp.where(incl, g_k[i][L:], 0.0) for i in idx]
    akv = [mm(a_ak[i], vh[i]) for i in idx]
    eye_l = jnp.where(ti == tj, 1.0, 0.0)
    tinv = [eye_l + n_ab[i] for i in idx]
    pw = n_ab
    for _ in range(int(math.log2(L)) - 1):
        pw = [mm(pw[i], pw[i]) for i in idx]
        tinv = [tinv[i] + mm(pw[i], tinv[i]) for i in idx]
    tinv = [t.astype(BF16) for t in tinv]
    a_t = [mm(tinv[i], ath[i]).astype(BF16) for i in idx]
    y = [mm(tinv[i], akv[i]).astype(BF16) for i in idx]
    m_c = [jnp.where(eye, w_last[units[i][0]][:, sls[units[i][1]]], 0.0) + mm(a_t[i], cut(bh_b, units[i]), TN)
           for i in idx]
    c_c = [mm(y[i], cut(bh_b, units[i]), TN) + mm(vh[i], cut(kh_b, units[i]), TN) for i in idx]
    q_c = [cut(rt, units[i]) + mm(a_rb[i], a_t[i]) for i in idx]
    o_loc = [mm(a_rb[i], y[i]) + mm(a_rk[i], vh[i]) for i in idx]
    s = [state[h] for h in heads]
    for ci in range(nc):
        for h in heads:
            i = ci * N_RWKV_HEADS + h
            o = o_loc[i] + _dot3(q_c[i], s[h], NT)
            s[h] = _dot3(s[h], m_c[i]) + c_c[i]
            mu = jnp.mean(o, axis=-1, keepdims=True)
            d = o - mu
            var = jnp.mean(d * d, axis=-1, keepdims=True)
            o_ref[rows[ci], sls[h]] = d * lax.rsqrt(var + RWKV_GN_EPS)
    for h in heads:
        state[h] = s[h]
    bonus = _dot_exact_rhs(r * k * rk_ref[...], ones_ref[...], nsplit=3) * v
    o_ref[...] = (o_ref[...] * lnw_ref[...] + lnb_ref[...] + bonus) * g_ref[...]


def _rwkv_scan(r, lw, k, v, kk, a, g, r_k, ln_w, ln_b):
    B, S, DR = r.shape
    L = min(L_CHUNK, S)
    lb = min(L * CHUNKS_PER_STEP, S)
    tri = jnp.asarray(np.kron(np.eye(lb // L), np.tril(np.ones((L, L)))), BF16)
    row = lambda z: z.reshape(1, -1)
    tok = pl.BlockSpec((None, lb, DR), lambda b, c: (b, c, 0))
    full = lambda z: pl.BlockSpec(z.shape, lambda b, c: (0,) * z.ndim)
    consts = [row(r_k), row(ln_w), row(ln_b), tri, _head_ones(DR, RWKV_HEAD)]
    return pl.pallas_call(
        functools.partial(_rwkv_kernel, L=L),
        grid=(B, S // lb),
        in_specs=[tok] * 7 + [full(z) for z in consts],
        out_specs=tok,
        out_shape=jax.ShapeDtypeStruct((B, S, DR), F32),
        scratch_shapes=[pltpu.VMEM((N_RWKV_HEADS, RWKV_HEAD, RWKV_HEAD), F32)],
        compiler_params=_cparams(("arbitrary", "arbitrary")),
        name="rwkv_scan",
    )(r, lw, k, v, kk, a, g, *consts)


def _alibi_cols():
    slope = np.asarray([2.0 ** (-8.0 * (h + 1) / N_ATT_HEADS) * LOG2E for h in range(N_ATT_HEADS)], np.float32)
    c_hi = slope.astype(BF16).astype(np.float32)
    c_lo = (slope - c_hi).astype(BF16).astype(np.float32)
    t = np.zeros((N_ATT_HEADS, LANES), np.float32)
    t[:, 1], t[:, 2], t[:, 3], t[:, 4] = LANES * c_hi, LANES * c_lo, c_hi, c_lo
    return jnp.asarray(t.astype(BF16))


def _dsa_kernel(iq_ref, iw_ref, qa_ref, ik_ref, ca_ref, cat_ref, wuv_ref, tril_ref, acol_ref, o_ref, key_ref,
                bias_ref, lg_ref, p_ref, *, topk, q_off, select):
    qi = pl.program_id(1) + q_off
    tq = qa_ref.shape[0]
    sk = ca_ref.shape[0]
    kc = min(KEY_CHUNK, sk)
    chunks = [slice(c * kc, (c + 1) * kc) for c in range(sk // kc)]
    tpos = qi * tq + lax.broadcasted_iota(I32, (1, tq), 1)
    srow = lax.broadcasted_iota(I32, (kc, tq), 0)
    NT = (((1,), (1,)), ((), ()))

    if not select:
        for c, cs in enumerate(chunks):
            bias_ref[cs, :] = jnp.where(srow + c * kc <= tpos, 0.0, NEG_BIG)
    else:
        iw_t = jnp.transpose(iw_ref[...])
        for c, cs in enumerate(chunks):
            ikc = ik_ref[cs, :]
            score = jnp.zeros((kc, tq), F32)
            for h in range(IDX_HEADS):
                s = lax.dot_general(ikc, iq_ref[:, h * LANES:(h + 1) * LANES], NT, preferred_element_type=F32)
                score = score + iw_t[h:h + 1, :] * jnp.maximum(s, 0.0)
            bits = pltpu.bitcast(score + 0.0, I32)
            key = bits ^ ((bits >> 31) & 0x7FFFFFFF)
            key_ref[cs, :] = jnp.where(srow + c * kc <= tpos, key, INT_MIN)
        kcount = jnp.minimum(topk, tpos + 1).astype(F32)

        def sum_keys(x):
            part = jnp.sum(x.reshape(x.shape[0] // SUM_ROWS, SUM_ROWS, x.shape[1]), axis=0)
            return jnp.sum(part, axis=0, keepdims=True)

        def count_ge(cand):
            return sum_keys(jnp.where(key_ref[...] >= cand, 1.0, 0.0))

        thr = jnp.where(count_ge(jnp.zeros((1, tq), I32)) >= kcount, 0, INT_MIN).astype(I32)

        def bit_step(i, thr):
            cand = thr | (1 << (30 - i))
            return jnp.where(count_ge(cand) >= kcount, cand, thr)

        thr = lax.fori_loop(0, 31, bit_step, thr)

        need = kcount - sum_keys(jnp.where(key_ref[...] > thr, 1.0, 0.0))
        carry = jnp.zeros((1, tq), F32)
        for j in range(sk // LANES):
            sl = slice(j * LANES, (j + 1) * LANES)
            key = key_ref[sl, :]
            eq = key == thr
            e = jnp.where(eq, 1.0, 0.0)
            before = jnp.dot(tril_ref[...], e.astype(BF16), preferred_element_type=F32) + carry
            sel = (key > thr) | (eq & (before < need))
            bias_ref[sl, :] = jnp.where(sel, 0.0, NEG_BIG)
            carry = carry + jnp.sum(e, axis=0, keepdims=True)

    outs = []
    for h in range(N_ATT_HEADS):
        q_aug = jnp.concatenate([qa_ref[:, h * KV_LORA:(h + 1) * KV_LORA],
                                 jnp.broadcast_to(acol_ref[h:h + 1, :], (tq, LANES))], axis=-1)
        lg_all = lax.dot_general(ca_ref[...], q_aug, NT, preferred_element_type=F32)
        m = jnp.full((1, tq), -jnp.inf, F32)
        for cs in chunks:
            lg = lg_all[cs, :] + bias_ref[cs, :]
            lg_ref[cs, :] = lg
            m = jnp.maximum(m, jnp.max(lg, axis=0, keepdims=True))
        for cs in chunks:
            p_ref[cs, :] = jnp.exp2(lg_ref[cs, :] - m).astype(BF16)
        pv = jnp.dot(cat_ref[...], p_ref[...], preferred_element_type=F32)
        outs.append((pv[:KV_LORA, :] / pv[KV_LORA:KV_LORA + 1, :]).astype(BF16))
    o_lat_t = jnp.concatenate(outs, axis=0)
    o_ref[...] = lax.dot_general(o_lat_t, wuv_ref[...], (((0,), (0,)), ((), ())), preferred_element_type=F32)


def _dsa_attn(iq, iw, qabs, ik, ca, cat, w_uv):
    B, S, _ = iq.shape
    tq = min(TQ, S)
    topk = min(TOPK_MAX, S // 4)
    wuv_bd = _block_diag(w_uv).astype(BF16)
    tril = jnp.asarray(np.tril(np.ones((LANES, LANES)), -1), BF16)
    acol = _alibi_cols()
    full = lambda z: pl.BlockSpec(z.shape, lambda b, i: (0,) * z.ndim)
    nq = 1
    outs = []
    for q_off in range(0, S // tq, nq):
        sk = (q_off + nq) * tq
        tok = lambda n, q_off=q_off: pl.BlockSpec((None, tq, n), lambda b, i: (b, i + q_off, 0))
        seq = lambda n, sk=sk: pl.BlockSpec((None, sk, n), lambda b, i: (b, 0, 0))
        outs.append(pl.pallas_call(
            functools.partial(_dsa_kernel, topk=topk, q_off=q_off, select=sk > topk),
            grid=(B, nq),
            in_specs=[tok(IDX_HEADS * LANES), tok(LANES), tok(N_ATT_HEADS * KV_LORA), seq(LANES), seq(KV_LORA + LANES),
                      pl.BlockSpec((None, KV_LORA + LANES, sk), lambda b, i: (b, 0, 0)),
                      full(wuv_bd), full(tril), full(acol)],
            out_specs=pl.BlockSpec((None, tq, D_ATT), lambda b, i: (b, i, 0)),
            out_shape=jax.ShapeDtypeStruct((B, nq * tq, D_ATT), F32),
            scratch_shapes=[pltpu.VMEM((sk, tq), I32), pltpu.VMEM((sk, tq), F32), pltpu.VMEM((sk, tq), F32),
                            pltpu.VMEM((sk, tq), BF16)],
            compiler_params=_cparams(("arbitrary", "arbitrary")),
            name=f"dsa_attn_k{sk}",
        )(iq, iw, qabs, ik, ca, cat, wuv_bd, tril, acol))
    return jnp.concatenate(outs, axis=1)


def _layernorm_rows(y, g, b):
    mu = jnp.mean(y, axis=-1, keepdims=True)
    d = y - mu
    var = jnp.mean(d * d, axis=-1, keepdims=True)
    return d * lax.rsqrt(var + 1e-5) * g + b


def _mix_kernel(orw_ref, ods_ref, x_ref, gt_ref, sc_ref, sh_ref, wtop_ref, wbot_ref, g_ref, b_ref, wr_ref, br_ref,
                tril_ref, triu_ref, x1_o, u2_o, route_o, tbl_o, cnt_o, carry, *, alpha):
    first = (pl.program_id(0) == 0) & (pl.program_id(1) == 0)
    tm = x_ref.shape[0]

    @pl.when(first)
    def _():
        carry[...] = jnp.zeros_like(carry)

    mix = _bdot(orw_ref[...], wtop_ref[...]) + _bdot(ods_ref[...], wbot_ref[...])
    x1 = _layernorm_rows(alpha * x_ref[...] + (1.0 + gt_ref[...]) * mix, g_ref[...], b_ref[...])
    x1_o[...] = x1
    u2 = x1 * (1.0 + sc_ref[...]) + sh_ref[...]
    u2_o[...] = u2

    lg = _dot3(u2, wr_ref[...]) + br_ref[...]
    lane = lax.broadcasted_iota(I32, (tm, LANES), 1)
    lane_f = lane.astype(F32)
    idxs, vals = [], []
    for _ in range(TOP_K_EXPERTS):
        m = jnp.max(lg, axis=-1, keepdims=True)
        idx = jnp.min(jnp.where(lg == m, lane_f, float(LANES)), axis=-1, keepdims=True).astype(I32)
        idxs.append(idx)
        vals.append(m)
        lg = jnp.where(lane == idx, -jnp.inf, lg)
    es = [jnp.exp(v - vals[0]) for v in vals]
    den = es[0] + es[1] + es[2] + es[3]
    hot = jnp.zeros((tm, LANES), F32)
    for idx in idxs:
        hot = hot + jnp.where(lane == idx, 1.0, 0.0)
    before = jnp.dot(tril_ref[...], hot.astype(BF16), preferred_element_type=F32)
    cnt = jnp.sum(hot, axis=0, keepdims=True)
    n_run = jnp.floor((cnt + (RUN_ALIGN - 1)) * (1.0 / RUN_ALIGN))
    n_stage = jnp.floor((n_run * RUN_ALIGN + (STAGE_ALIGN - 1)) * (1.0 / STAGE_ALIGN))
    off = jnp.dot(jnp.broadcast_to(n_stage, (SUBLANES, LANES)).astype(BF16), triu_ref[...],
                  preferred_element_type=F32)[0:1, :] * STAGE_ALIGN
    where_in_stage = off + before
    route = jnp.zeros((tm, LANES), F32)
    for k in range(TOP_K_EXPERTS):
        spos = jnp.sum(jnp.where(lane == idxs[k], where_in_stage, 0.0), axis=-1, keepdims=True)
        route = jnp.where(lane == k, idxs[k].astype(F32), route)
        route = jnp.where(lane == TOP_K_EXPERTS + k, es[k] / den, route)
        route = jnp.where(lane == 2 * TOP_K_EXPERTS + k, spos, route)
    route_o[...] = route
    sub = lax.broadcasted_iota(I32, (SUBLANES, LANES), 0)
    tbl_o[...] = jnp.where(sub == 0, n_run, jnp.where(sub == 1, off, jnp.where(sub == 2, carry[0:1, :], 0.0)))
    carry[0:1, :] = carry[0:1, :] + n_run * RUN_ALIGN
    cnt_o[...] = carry[...]


def _mix_out(o_rwkv, o_dsa, x, gt1, sc2, sh2, w_out, ln_g, ln_b, w_router, b_router):
    B, S, D = x.shape
    tm = min(TM_PROJ, S)
    alpha = 2.0 ** 0.25
    wtop = w_out[:D_RWKV].astype(BF16)
    wbot = w_out[D_RWKV:].astype(BF16)
    wr_p = jnp.pad(w_router, ((0, 0), (0, LANES - N_EXPERTS)))
    br_p = jnp.pad(b_router.reshape(1, -1), ((0, 0), (0, LANES - N_EXPERTS)), constant_values=NEG_BIG)
    tril = jnp.asarray(np.tril(np.ones((tm, tm)), -1), BF16)
    triu = jnp.asarray(np.triu(np.ones((LANES, LANES)), 1), BF16)
    row = lambda v: v.reshape(1, -1)
    tok = lambda n: pl.BlockSpec((None, tm, n), lambda b, i: (b, i, 0))
    mod = pl.BlockSpec((None, 1, D), lambda b, i: (b, 0, 0))
    full = lambda a: pl.BlockSpec(a.shape, lambda b, i: (0,) * a.ndim)
    consts = [wtop, wbot, row(ln_g), row(ln_b), wr_p, br_p, tril, triu]
    return pl.pallas_call(
        functools.partial(_mix_kernel, alpha=alpha),
        grid=(B, S // tm),
        in_specs=[tok(D_RWKV), tok(D_ATT), tok(D), mod, mod, mod] + [full(a) for a in consts],
        out_specs=[tok(D), tok(D), tok(LANES), pl.BlockSpec((None, None, SUBLANES, LANES), lambda b, i: (b, i, 0, 0)),
                   pl.BlockSpec((SUBLANES, LANES), lambda b, i: (0, 0))],
        out_shape=[jax.ShapeDtypeStruct((B, S, D), F32), jax.ShapeDtypeStruct((B, S, D), F32),
                   jax.ShapeDtypeStruct((B, S, LANES), F32),
                   jax.ShapeDtypeStruct((B, S // tm, SUBLANES, LANES), F32),
                   jax.ShapeDtypeStruct((SUBLANES, LANES), F32)],
        scratch_shapes=[pltpu.VMEM((8, LANES), F32)],
        compiler_params=_cparams(("arbitrary", "arbitrary")),
        name="mix_out",
    )(o_rwkv, o_dsa, x, gt1, sc2, sh2, *consts)


RUN_ALIGN = SUBLANES
STAGE_ALIGN = 16
STAGE_ROWS = 1536
STAGE_CHUNK = 256


def _run_copies(nrun_ref, tile, copy_of):
    for e in range(N_EXPERTS):
        pieces = (nrun_ref[tile * N_EXPERTS + e] * RUN_ALIGN + STAGE_ALIGN - 1) // STAGE_ALIGN

        def piece(j, carry, e=e):
            copy_of(e, j).start(priority=e % 2)
            return carry

        lax.fori_loop(0, pieces, piece, 0)


def _dispatch_kernel(nrun_ref, off_ref, start_ref, npiece_ref, nstage_ref, zs_ref, zn_ref, tail_ref,
                     u_ref, route_ref, xs_out, stag, zeros, sem, zsem):
    i = pl.program_id(0)
    n = pl.num_programs(0)
    tm = u_ref.shape[0]
    bm = zeros.shape[0]
    slot = i % 2

    @pl.when(i == 0)
    def _():
        zeros[...] = jnp.zeros_like(zeros)
        fills = []
        for e in range(N_EXPERTS):
            for b in range(bm.bit_length()):
                rows = RUN_ALIGN << b
                if rows > bm:
                    break
                done = (zn_ref[e] >> (b + 1)) << (b + 1)
                dst = pl.multiple_of(zs_ref[e] + done * RUN_ALIGN, RUN_ALIGN)
                fills.append(((zn_ref[e] >> b) & 1 == 1,
                              pltpu.make_async_copy(zeros.at[pl.ds(0, rows)], xs_out.at[pl.ds(dst, rows)], zsem)))
        for pred, cp in fills:
            pl.when(pred)(cp.start)

        def tail_copy(j):
            return pltpu.make_async_copy(zeros, xs_out.at[pl.ds(pl.multiple_of(j * bm, bm), bm)], zsem)

        lax.fori_loop(tail_ref[0], tail_ref[1], lambda j, c: (tail_copy(j).start(), c)[1], 0)
        for pred, cp in fills:
            pl.when(pred)(cp.wait)
        lax.fori_loop(tail_ref[0], tail_ref[1], lambda j, c: (tail_copy(j).wait(), c)[1], 0)

    route_t = jnp.transpose(route_ref[...])
    spos = [route_t[2 * TOP_K_EXPERTS + k:2 * TOP_K_EXPERTS + k + 1, :].astype(I32) for k in range(TOP_K_EXPERTS)]
    ub = u_ref[...].astype(BF16)
    srow = lax.broadcasted_iota(I32, (STAGE_CHUNK, tm), 0)
    for c in range(STAGE_ROWS // STAGE_CHUNK):
        @pl.when(c * STAGE_CHUNK < nstage_ref[i])
        def _(c=c):
            rows = srow + c * STAGE_CHUNK
            sel = (rows == spos[0]) | (rows == spos[1]) | (rows == spos[2]) | (rows == spos[3])
            stag[slot, c * STAGE_CHUNK:(c + 1) * STAGE_CHUNK, :] = jnp.dot(
                jnp.where(sel, 1.0, 0.0).astype(BF16), ub, preferred_element_type=F32)

    def piece_copy(s, tile):
        def copy_of(e, j):
            src = pl.multiple_of(off_ref[tile * N_EXPERTS + e] + j * STAGE_ALIGN, STAGE_ALIGN)
            dst = pl.multiple_of(start_ref[tile * N_EXPERTS + e] + j * STAGE_ALIGN, RUN_ALIGN)
            return pltpu.make_async_copy(stag.at[s, pl.ds(src, STAGE_ALIGN)], xs_out.at[pl.ds(dst, STAGE_ALIGN)], sem)
        return copy_of

    def drain(tile):
        def w(j, carry):
            pltpu.make_async_copy(stag.at[0, pl.ds(0, STAGE_ALIGN)], xs_out.at[pl.ds(0, STAGE_ALIGN)], sem).wait()
            return carry
        lax.fori_loop(0, npiece_ref[tile], w, 0)

    @pl.when(i > 0)
    def _():
        drain(i - 1)

    _run_copies(nrun_ref, i, piece_copy(slot, i))

    @pl.when(i == n - 1)
    def _():
        drain(i)


def _moe_dispatch(u2, route, tables, n_rows):
    T, D = u2.shape
    tm = TM_ROUTE
    bm = BM_EXPERT
    return pl.pallas_call(
        _dispatch_kernel,
        grid_spec=pltpu.PrefetchScalarGridSpec(
            num_scalar_prefetch=len(tables),
            grid=(T // tm,),
            in_specs=[pl.BlockSpec((tm, D), lambda i, *_: (i, 0)),
                      pl.BlockSpec((tm, LANES), lambda i, *_: (i, 0))],
            out_specs=pl.BlockSpec(memory_space=pl.ANY),
            scratch_shapes=[pltpu.VMEM((2, STAGE_ROWS, D), F32), pltpu.VMEM((bm, D), F32),
                            pltpu.SemaphoreType.DMA(()), pltpu.SemaphoreType.DMA(())],
        ),
        out_shape=jax.ShapeDtypeStruct((n_rows, D), F32),
        compiler_params=_cparams(("arbitrary",)),
        name="moe_dispatch",
    )(*tables, u2, route)


GU_GROUP = 2 * LANES


def _deinterleave_perm():
    p = np.zeros((GU_GROUP, GU_GROUP), np.float32)
    l = np.arange(LANES)
    p[2 * l, l] = 1.0
    p[2 * l + 1, LANES + l] = 1.0
    return jnp.asarray(p, BF16)


def _expert_kernel(be_ref, nb_ref, xs_ref, wgu_ref, bgu_ref, wd_ref, bd_ref, perm_ref, ys_ref, wp, wdb):
    i = pl.program_id(0)
    used = i < nb_ref[0]
    new_expert = (i == 0) | (be_ref[i] != be_ref[jnp.maximum(i - 1, 0)])
    n_groups = wgu_ref.shape[1] // GU_GROUP

    @pl.when(used & new_expert)
    def _():
        for j in range(n_groups):
            sl = slice(j * GU_GROUP, (j + 1) * GU_GROUP)
            wp[:, sl] = jnp.dot(wgu_ref[:, sl].astype(BF16), perm_ref[...], preferred_element_type=F32).astype(BF16)
        wdb[...] = wd_ref[...].astype(BF16)

    @pl.when(used)
    def _():
        xb = xs_ref[...].astype(BF16)
        gu = jnp.dot(xb, wp[...], preferred_element_type=F32) + bgu_ref[...]
        hs = []
        for j in range(n_groups):
            gate = jnp.minimum(gu[:, j * GU_GROUP:j * GU_GROUP + LANES], SWIGLU_LIMIT)
            up = jnp.clip(gu[:, j * GU_GROUP + LANES:(j + 1) * GU_GROUP], -SWIGLU_LIMIT, SWIGLU_LIMIT)
            hs.append(((up + 1.0) * (gate * _sigmoid(gate * SWIGLU_ALPHA))).astype(BF16))
        h = jnp.concatenate(hs, axis=-1)
        ys_ref[...] = jnp.dot(h, wdb[...], preferred_element_type=F32) + bd_ref[...]

    @pl.when(jnp.logical_not(used))
    def _():
        ys_ref[...] = jnp.zeros_like(ys_ref)


def _moe_experts(xs, block_e, n_used, w_gu, b_gu_p, w_dn, b_dn):
    n_rows, D = xs.shape
    E, _, F2 = w_gu.shape
    bm = BM_EXPERT
    n_blocks = n_rows // bm
    perm = _deinterleave_perm()
    wspec = lambda shp: pl.BlockSpec((None,) + shp, lambda i, be, nb: (be[i], 0, 0))
    return pl.pallas_call(
        _expert_kernel,
        grid_spec=pltpu.PrefetchScalarGridSpec(
            num_scalar_prefetch=2,
            grid=(n_blocks,),
            in_specs=[pl.BlockSpec((bm, D), lambda i, be, nb: (jnp.minimum(i, nb[0] - 1), 0)),
                      wspec((D, F2)), wspec((1, F2)), wspec((F2 // 2, D)), wspec((1, D)),
                      pl.BlockSpec(perm.shape, lambda i, be, nb: (0, 0))],
            out_specs=pl.BlockSpec((bm, D), lambda i, be, nb: (i, 0)),
            scratch_shapes=[pltpu.VMEM((D, F2), BF16), pltpu.VMEM((F2 // 2, D), BF16)],
        ),
        out_shape=jax.ShapeDtypeStruct((n_rows, D), F32),
        compiler_params=pltpu.CompilerParams(dimension_semantics=("arbitrary",), vmem_limit_bytes=VMEM_LIMIT_EXPERTS),
        name="moe_experts",
    )(block_e, n_used, xs, w_gu, b_gu_p, w_dn, b_dn, perm)


def _combine_kernel(nrun_ref, off_ref, start_ref, npiece_ref, nstage_ref, ys_ref, x1_ref, route_ref, gt_ref, g_ref,
                    b_ref, o_ref, stag, sem, *, alpha):
    i = pl.program_id(0)
    n = pl.num_programs(0)
    tm = x1_ref.shape[0]
    slot = i % 2

    def gather(tile, s):
        def copy_of(e, j):
            src = pl.multiple_of(start_ref[tile * N_EXPERTS + e] + j * STAGE_ALIGN, RUN_ALIGN)
            dst = pl.multiple_of(off_ref[tile * N_EXPERTS + e] + j * STAGE_ALIGN, STAGE_ALIGN)
            return pltpu.make_async_copy(ys_ref.at[pl.ds(src, STAGE_ALIGN)], stag.at[s, pl.ds(dst, STAGE_ALIGN)],
                                         sem.at[s])
        _run_copies(nrun_ref, tile, copy_of)

    @pl.when(i == 0)
    def _():
        stag[...] = jnp.zeros_like(stag)
        gather(0, 0)

    @pl.when(i + 1 < n)
    def _():
        gather(i + 1, 1 - slot)

    def w(j, carry):
        pltpu.make_async_copy(ys_ref.at[pl.ds(0, STAGE_ALIGN)], stag.at[slot, pl.ds(0, STAGE_ALIGN)],
                              sem.at[slot]).wait()
        return carry
    lax.fori_loop(0, npiece_ref[i], w, 0)

    route = route_ref[...]
    wide = lambda col: jnp.broadcast_to(col, (tm, STAGE_CHUNK))
    spos = [wide(route[:, 2 * TOP_K_EXPERTS + k:2 * TOP_K_EXPERTS + k + 1].astype(I32)) for k in range(TOP_K_EXPERTS)]
    gate = [wide(route[:, TOP_K_EXPERTS + k:TOP_K_EXPERTS + k + 1]) for k in range(TOP_K_EXPERTS)]
    scol = lax.broadcasted_iota(I32, (tm, STAGE_CHUNK), 1)
    acc_ref = o_ref
    acc_ref[...] = jnp.zeros_like(acc_ref)
    for c in range(STAGE_ROWS // STAGE_CHUNK):
        @pl.when(c * STAGE_CHUNK < nstage_ref[i])
        def _(c=c):
            cols = scol + c * STAGE_CHUNK
            wgt = jnp.zeros((tm, STAGE_CHUNK), F32)
            for k in range(TOP_K_EXPERTS):
                wgt = wgt + jnp.where(cols == spos[k], gate[k], 0.0)
            acc_ref[...] += jnp.dot(wgt.astype(BF16), stag[slot, c * STAGE_CHUNK:(c + 1) * STAGE_CHUNK, :].astype(BF16),
                                    preferred_element_type=F32)
    o_ref[...] = _layernorm_rows(alpha * x1_ref[...] + (1.0 + gt_ref[...]) * acc_ref[...], g_ref[...], b_ref[...])


def _moe_combine(ys, tables, x1, route, gt2, ln_g, ln_b, tiles_per_batch):
    T, D = x1.shape
    tm = TM_ROUTE
    row = lambda v: v.reshape(1, -1)
    return pl.pallas_call(
        functools.partial(_combine_kernel, alpha=2.0 ** 0.25),
        grid_spec=pltpu.PrefetchScalarGridSpec(
            num_scalar_prefetch=len(tables),
            grid=(T // tm,),
            in_specs=[pl.BlockSpec(memory_space=pl.ANY),
                      pl.BlockSpec((tm, D), lambda i, *_: (i, 0)),
                      pl.BlockSpec((tm, LANES), lambda i, *_: (i, 0)),
                      pl.BlockSpec((None, 1, D), lambda i, *_: (i // tiles_per_batch, 0, 0)),
                      pl.BlockSpec((1, D), lambda i, *_: (0, 0)),
                      pl.BlockSpec((1, D), lambda i, *_: (0, 0))],
            out_specs=pl.BlockSpec((tm, D), lambda i, *_: (i, 0)),
            scratch_shapes=[pltpu.VMEM((2, STAGE_ROWS, D), F32), pltpu.SemaphoreType.DMA((2,))],
        ),
        out_shape=jax.ShapeDtypeStruct((T, D), F32),
        compiler_params=_cparams(("arbitrary",)),
        name="moe_combine",
    )(*tables, ys, x1, route, gt2, row(ln_g), row(ln_b))


def _moe_and_norm(x1, u2, route, tbl, totals, gt2, w_gu, b_gu, w_dn, b_dn, ln_g, ln_b):
    B, S, D = x1.shape
    T = B * S
    bm = BM_EXPERT
    tm = TM_ROUTE
    assert T % tm == 0 and TM_PROJ == tm and STAGE_ROWS >= tm * TOP_K_EXPERTS + N_EXPERTS * (STAGE_ALIGN - 1)
    n_tiles = T // tm
    max_rows = T * TOP_K_EXPERTS + n_tiles * N_EXPERTS * (RUN_ALIGN - 1) + N_EXPERTS * STAGE_ALIGN
    n_blocks = -(-max_rows // bm) + N_EXPERTS
    tot = totals[0, :N_EXPERTS].astype(I32)
    padded = (tot + STAGE_ALIGN + bm - 1) // bm * bm
    pad_ends = jnp.cumsum(padded)
    pad_starts = pad_ends - padded
    t3 = tbl.reshape(n_tiles, SUBLANES, LANES)[:, :, :N_EXPERTS].astype(I32)
    nrun, off, base = t3[:, 0, :], t3[:, 1, :], t3[:, 2, :]
    start = pad_starts[None, :] + base
    pieces = (nrun * RUN_ALIGN + STAGE_ALIGN - 1) // STAGE_ALIGN
    npiece = jnp.sum(pieces, axis=1)
    nstage = jnp.sum(pieces, axis=1) * STAGE_ALIGN
    flat = lambda z: z.reshape(-1).astype(I32)
    run_tables = [flat(nrun), flat(off), flat(start), flat(npiece), flat(nstage)]
    zs = pad_starts + tot
    zn = (pad_ends - zs) // RUN_ALIGN
    n_used = (pad_ends[-1:] // bm).astype(I32)
    tail = jnp.concatenate([n_used, jnp.full((1,), n_blocks, I32)])
    blk_row = jnp.arange(n_blocks, dtype=I32) * bm
    block_e = jnp.minimum(jnp.sum((blk_row[:, None] >= pad_ends[None, :]).astype(I32), axis=1), N_EXPERTS - 1)
    E, F2 = b_gu.shape
    b_gu_p = b_gu.reshape(E, F2 // GU_GROUP, LANES, 2).transpose(0, 1, 3, 2).reshape(E, 1, F2)
    route2 = route.reshape(T, LANES)
    xs = _moe_dispatch(u2.reshape(T, D), route2, run_tables + [flat(zs), flat(zn), tail], n_blocks * bm)
    ys = _moe_experts(xs, block_e, n_used, w_gu, b_gu_p, w_dn, b_dn[:, None, :])
    out = _moe_combine(ys, run_tables, x1.reshape(T, D), route2, gt2, ln_g, ln_b, S // tm)
    return out.reshape(B, S, D)


def kernel(x, c, w_ada, b_ada, w_in, shift_mu, rwkv_w0, rwkv_w2, rwkv_a0, rwkv_a2, rwkv_g2, rwkv_k_k, rwkv_k_a, rwkv_r_k, rwkv_ln_w, rwkv_ln_b, mla_q_norm, mla_w_q_up, mla_kv_norm, mla_w_uk, mla_w_uv, idx_w_q, idx_ln_g, idx_ln_b, w_out, ln1_g, ln1_b, w_router, b_router, w_gu, b_gu, w_dn, b_dn, ln2_g, ln2_b):
    depth = w_ada.shape[0]
    assert depth == 1, "DeepNorm constants below are for a single layer"
    l = 0
    mod = _ada_mod(c, w_ada[l], b_ada[l])
    sh1, sc1, gt1, sh2, sc2, gt2 = [m[:, None, :] for m in jnp.split(mod, 6, axis=-1)]
    r, lw, k, v, kk, a, g, qabs, iq, ik, iw, cl, clt = _in_proj(
        x, sc1, sh1, w_in[l], shift_mu[l], rwkv_w0[l], rwkv_w2[l], rwkv_a0[l], rwkv_a2[l], rwkv_g2[l], rwkv_k_k[l],
        rwkv_k_a[l], mla_q_norm[l], mla_w_q_up[l], mla_w_uk[l], mla_kv_norm[l], idx_w_q[l], idx_ln_g[l], idx_ln_b[l])
    o_rwkv = _rwkv_scan(r, lw, k, v, kk, a, g, rwkv_r_k[l], rwkv_ln_w[l], rwkv_ln_b[l])
    o_dsa = _dsa_attn(iq, iw, qabs, ik, cl, clt, mla_w_uv[l])
    x1, u2, route, tbl, totals = _mix_out(o_rwkv, o_dsa, x, gt1, sc2, sh2, w_out[l], ln1_g[l], ln1_b[l],
                                          w_router[l], b_router[l])
    return _moe_and_norm(x1, u2, route, tbl, totals, gt2, w_gu[l], b_gu[l], w_dn[l], b_dn[l], ln2_g[l], ln2_b[l])
```

```python
import functools
import math

import jax
import jax.numpy as jnp
import numpy as np
from jax import lax
from jax.experimental import pallas as pl
from jax.experimental.pallas import tpu as pltpu

F32 = jnp.float32
BF16 = jnp.bfloat16
I32 = jnp.int32

RWKV_HEAD = 64
N_RWKV_HEADS = 8
D_RWKV = RWKV_HEAD * N_RWKV_HEADS
RANK_DECAY = 64
RANK_A = 64
RANK_GATE = 128
RWKV_GN_EPS = 64e-5
ATT_HEAD = 64
N_ATT_HEADS = 8
D_ATT = ATT_HEAD * N_ATT_HEADS
Q_LORA = 256
KV_LORA = 128
IDX_HEADS = 8
IDX_DIM = 64
TOPK_MAX = 256
N_EXPERTS = 32
TOP_K_EXPERTS = 4
SWIGLU_LIMIT = 7.0
SWIGLU_ALPHA = 1.702
NEG_BIG = -1e30
LOG2E = 1.4426950408889634
INT_MIN = -(2 ** 31)

LANES = 128
SUBLANES = 8
VMEM_LIMIT = 48 * 1024 * 1024
VMEM_LIMIT_EXPERTS = 58 * 1024 * 1024

TM_IN = 512
TM_PROJ = 256
L_CHUNK = 64
CHUNKS_PER_STEP = 4
TQ = 256
KEY_CHUNK = 128
SUM_ROWS = 64
TM_ROUTE = 256
BM_EXPERT = 512

_SEG = (("r", 512, 512), ("k", 512, 512), ("v", 512, 512), ("w", 64, 128), ("a", 64, 128), ("g", 128, 128),
        ("q", 256, 256), ("kv", 128, 128), ("ik", 64, 128), ("iw", 8, 128))
N_SHIFT_P = 512 * 3 + 128 * 3
N_IN_P = sum(s[2] for s in _SEG)


def _cparams(sem):
    return pltpu.CompilerParams(dimension_semantics=sem, vmem_limit_bytes=VMEM_LIMIT)


def _bdot(a, b):
    return jnp.dot(a.astype(BF16), b.astype(BF16), preferred_element_type=F32)


def _bdot_nt(a, b):
    return lax.dot_general(a.astype(BF16), b.astype(BF16), (((1,), (1,)), ((), ())), preferred_element_type=F32)


def _bdot_tn(a, b):
    return lax.dot_general(a.astype(BF16), b.astype(BF16), (((0,), (0,)), ((), ())), preferred_element_type=F32)


def _split2(a):
    hi = a.astype(BF16)
    lo = (a - hi.astype(F32)).astype(BF16)
    return hi, lo


def _split3(a):
    hi = a.astype(BF16)
    r1 = a - hi.astype(F32)
    mid = r1.astype(BF16)
    lo = (r1 - mid.astype(F32)).astype(BF16)
    return hi, mid, lo


def _dot3(a, b, dims=(((1,), (0,)), ((), ()))):
    ah, al = _split2(a)
    bh, bl = _split2(b)
    d = functools.partial(lax.dot_general, dimension_numbers=dims, preferred_element_type=F32)
    return d(ah, bh) + (d(ah, bl) + d(al, bh))


def _dot_exact_rhs(a, b_exact, nsplit=3):
    parts = _split3(a) if nsplit == 3 else _split2(a)
    acc = None
    for p in parts[::-1]:
        t = jnp.dot(p, b_exact, preferred_element_type=F32)
        acc = t if acc is None else acc + t
    return acc


def _dot_exact_lhs(a_exact, b, nsplit=3):
    parts = _split3(b) if nsplit == 3 else _split2(b)
    acc = None
    for p in parts[::-1]:
        t = jnp.dot(a_exact, p, preferred_element_type=F32)
        acc = t if acc is None else acc + t
    return acc


def _sigmoid(x):
    return 1.0 / (1.0 + jnp.exp(-x))


def _softplus(x):
    return jnp.maximum(x, 0.0) + jnp.log(1.0 + jnp.exp(-jnp.abs(x)))


def _ada_kernel(c_ref, w_ref, b_ref, o_ref):
    c = c_ref[...]
    o_ref[...] = _dot3(c * _sigmoid(c), w_ref[...]) + b_ref[...]


def _ada_mod(c, w_ada, b_ada):
    B, D = c.shape
    N = w_ada.shape[1]
    tn = 1024
    return pl.pallas_call(
        _ada_kernel,
        grid=(N // tn,),
        in_specs=[pl.BlockSpec((B, D), lambda j: (0, 0)),
                  pl.BlockSpec((D, tn), lambda j: (0, j)),
                  pl.BlockSpec((1, tn), lambda j: (0, j))],
        out_specs=pl.BlockSpec((B, tn), lambda j: (0, j)),
        out_shape=jax.ShapeDtypeStruct((B, N), F32),
        compiler_params=_cparams(("arbitrary",)),
        name="ada_mod",
    )(c, w_ada, b_ada.reshape(1, N))


def _in_proj_kernel(x_ref, sc_ref, sh_ref, win_ref, mu_ref, w0_ref, w2_ref, a0_ref, a2_ref, g2_ref, kk_ref, ka_ref,
                    ones_ref, qn_ref, wq_ref, wuk_ref, kvn_ref, wiq_ref, ig_ref, ib_ref,
                    r_o, lw_o, k_o, v_o, kkn_o, a_o, g_o, qabs_o, iq_o, ik_o, iw_o, cl_o, clt_o, carry):
    i = pl.program_id(1)
    tm = x_ref.shape[0]

    @pl.when(i == 0)
    def _():
        carry[...] = jnp.zeros_like(carry)

    u = x_ref[...] * (1.0 + sc_ref[...]) + sh_ref[...]
    p = _bdot(u, win_ref[...])
    ps = p[:, :N_SHIFT_P]
    rows = lax.broadcasted_iota(I32, (tm, 1), 0)
    prev = jnp.where(rows == 0, carry[0:1, :], pltpu.roll(ps, 1, 0))
    carry[0:1, :] = ps[tm - 1:tm, :]
    ps = ps + mu_ref[...] * (prev - ps)

    pr, pk, pv = ps[:, 0:512], ps[:, 512:1024], ps[:, 1024:1536]
    pw, pa, pg = ps[:, 1536:1664], ps[:, 1664:1792], ps[:, 1792:1920]
    w_log = -_softplus(-(w0_ref[...] + _dot3(jnp.tanh(pw), w2_ref[...]))) - 0.5
    lw_o[...] = -jnp.exp(w_log)
    a = _sigmoid(a0_ref[...] + _dot3(pa, a2_ref[...]))
    g_o[...] = _dot3(_sigmoid(pg), g2_ref[...])
    kk = pk * kk_ref[...]
    ssq = _dot_exact_rhs(kk * kk, ones_ref[...], nsplit=2)
    kkn_o[...] = kk / jnp.maximum(jnp.sqrt(ssq), 1e-12)
    k_o[...] = pk * (1.0 + (a - 1.0) * ka_ref[...])
    r_o[...] = pr
    v_o[...] = pv
    a_o[...] = a

    pq, pkv = p[:, 1920:2176], p[:, 2176:2304]
    pik, piw = p[:, 2304:2432], p[:, 2432:2560]
    q_lat = pq * lax.rsqrt(jnp.mean(pq * pq, axis=-1, keepdims=True) + 1e-6) * qn_ref[...]
    q = _bdot(q_lat, wq_ref[...])
    qabs_o[...] = (_bdot(q, wuk_ref[...]) * (ATT_HEAD ** -0.5 * LOG2E)).astype(BF16)
    c_lat = pkv * lax.rsqrt(jnp.mean(pkv * pkv, axis=-1, keepdims=True) + 1e-6) * kvn_ref[...]
    spos = i * tm + rows
    s_hi = (spos >> 7).astype(F32)
    s_lo = (spos & (LANES - 1)).astype(F32)
    lane_t = lax.broadcasted_iota(I32, (1, LANES), 1)
    extra = jnp.where(lane_t == 0, 1.0, jnp.where((lane_t == 1) | (lane_t == 2), s_hi,
                                                  jnp.where((lane_t == 3) | (lane_t == 4), s_lo, 0.0)))
    c_aug = jnp.concatenate([c_lat, extra], axis=-1)
    cl_o[...] = c_aug.astype(BF16)
    clt_o[...] = jnp.transpose(c_aug).astype(BF16)
    iq_o[...] = (_bdot(q_lat, wiq_ref[...]) * (IDX_DIM ** -0.5)).astype(BF16)
    lane = lax.broadcasted_iota(I32, (1, LANES), 1)
    valid = lane < IDX_DIM
    mu = jnp.sum(pik, axis=-1, keepdims=True) * (1.0 / IDX_DIM)
    dlt = jnp.where(valid, pik - mu, 0.0)
    var = jnp.sum(dlt * dlt, axis=-1, keepdims=True) * (1.0 / IDX_DIM)
    ik_o[...] = jnp.where(valid, dlt * lax.rsqrt(var + 1e-5) * ig_ref[...] + ib_ref[...], 0.0).astype(BF16)
    iw_o[...] = piw * (IDX_HEADS ** -0.5)


def _pad_cols(w, widths):
    parts, o = [], 0
    for true, padded in widths:
        seg = w[..., o:o + true]
        if padded > true:
            seg = jnp.pad(seg, [(0, 0)] * (w.ndim - 1) + [(0, padded - true)])
        parts.append(seg)
        o += true
    return jnp.concatenate(parts, axis=-1)


def _pad_rows(w, rows):
    return jnp.pad(w, ((0, rows - w.shape[0]), (0, 0)))


def _block_diag(blocks):
    H, a, b = blocks.shape
    eye = jnp.eye(H, dtype=blocks.dtype)
    return (eye[:, None, :, None] * blocks[:, :, None, :]).reshape(H * a, H * b)


def _head_ones(n, head):
    idx = np.arange(n) // head
    return jnp.asarray(idx[:, None] == idx[None, :], BF16)


def _in_proj(x, sc1, sh1, w_in, shift_mu, w0, w2, a0, a2, g2, k_k, k_a, q_norm, w_q_up, w_uk, kv_norm, idx_w_q,
             idx_ln_g, idx_ln_b):
    B, S, D = x.shape
    tm = min(TM_IN, S)
    widths = tuple((s[1], s[2]) for s in _SEG)
    win_p = _pad_cols(w_in, widths).astype(BF16)
    mu_p = _pad_cols(shift_mu.reshape(1, -1), widths[:6])
    w2_p = _pad_rows(w2, LANES)
    a2_p = _pad_rows(a2, LANES)
    wuk_bd = _block_diag(w_uk).astype(BF16)
    wiq_p = _pad_cols(idx_w_q, ((IDX_DIM, LANES),) * IDX_HEADS).astype(BF16)
    ig_p = _pad_cols(idx_ln_g.reshape(1, -1), ((IDX_DIM, LANES),))
    ib_p = _pad_cols(idx_ln_b.reshape(1, -1), ((IDX_DIM, LANES),))
    row = lambda v: v.reshape(1, -1)
    tok = lambda n: pl.BlockSpec((None, tm, n), lambda b, i: (b, i, 0))
    mod = pl.BlockSpec((None, 1, D), lambda b, i: (b, 0, 0))
    full = lambda a: pl.BlockSpec(a.shape, lambda b, i: (0,) * a.ndim)
    consts = [win_p, mu_p, row(w0), w2_p, row(a0), a2_p, g2, row(k_k), row(k_a), _head_ones(D_RWKV, RWKV_HEAD),
              row(q_norm), w_q_up.astype(BF16), wuk_bd, row(kv_norm), wiq_p, ig_p, ib_p]
    outs = [(D_RWKV, F32)] * 7 + [(N_ATT_HEADS * KV_LORA, BF16), (IDX_HEADS * LANES, BF16), (LANES, BF16),
                                  (LANES, F32), (KV_LORA + LANES, BF16)]
    return pl.pallas_call(
        _in_proj_kernel,
        grid=(B, S // tm),
        in_specs=[tok(D), mod, mod] + [full(a) for a in consts],
        out_specs=[tok(n) for n, _ in outs] + [pl.BlockSpec((None, KV_LORA + LANES, tm), lambda b, i: (b, 0, i))],
        out_shape=[jax.ShapeDtypeStruct((B, S, n), dt) for n, dt in outs]
        + [jax.ShapeDtypeStruct((B, KV_LORA + LANES, S), BF16)],
        scratch_shapes=[pltpu.VMEM((8, N_SHIFT_P), F32)],
        compiler_params=_cparams(("arbitrary", "arbitrary")),
        name="in_proj",
    )(x, sc1, sh1, *consts)


def _rwkv_kernel(r_ref, lw_ref, k_ref, v_ref, kk_ref, a_ref, g_ref, rk_ref, lnw_ref, lnb_ref, tri_ref, ones_ref,
                 o_ref, state, *, L):
    c = pl.program_id(1)
    nc = r_ref.shape[0] // L

    @pl.when(c == 0)
    def _():
        state[...] = jnp.zeros_like(state)

    r, lw, k, v, kk, a = r_ref[...], lw_ref[...], k_ref[...], v_ref[...], kk_ref[...], a_ref[...]
    cum = _dot_exact_lhs(tri_ref[...], lw)
    rows = [slice(ci * L, (ci + 1) * L) for ci in range(nc)]
    last = [cum[(ci + 1) * L - 1:(ci + 1) * L, :] for ci in range(nc)]
    cum_last = jnp.concatenate([jnp.broadcast_to(z, (L, z.shape[1])) for z in last], axis=0)
    w_incl = jnp.exp(cum)
    w_inv = jnp.exp(-cum)
    w_rel = jnp.exp(cum_last - cum)
    w_last = [jnp.exp(z) for z in last]
    bvec = kk * a
    at = -kk * jnp.exp(cum - lw)
    rt = r * w_incl
    bt = bvec * w_inv
    kt = k * w_inv
    bh = bvec * w_rel
    kh = k * w_rel
    ti = lax.broadcasted_iota(I32, (L, L), 0)
    tj = lax.broadcasted_iota(I32, (L, L), 1)
    strict = tj < ti
    incl = tj <= ti
    eye = lax.broadcasted_iota(I32, (RWKV_HEAD, RWKV_HEAD), 0) == lax.broadcasted_iota(I32, (RWKV_HEAD, RWKV_HEAD), 1)
    NT = (((1,), (1,)), ((), ()))
    TN = (((0,), (0,)), ((), ()))
    heads = range(N_RWKV_HEADS)
    sls = [slice(h * RWKV_HEAD, (h + 1) * RWKV_HEAD) for h in heads]
    units = [(ci, h) for ci in range(nc) for h in heads]
    idx = range(len(units))
    mm = lambda x, y, dims=(((1,), (0,)), ((), ())): lax.dot_general(
        x.astype(BF16), y.astype(BF16), dims, preferred_element_type=F32)
    at_b, rt_b, bt_b, kt_b, bh_b, kh_b, v_b = [z.astype(BF16) for z in (at, rt, bt, kt, bh, kh, v)]
    cut = lambda z, u: z[rows[u[0]], sls[u[1]]]
    vh = [cut(v_b, u) for u in units]
    ath = [cut(at_b, u) for u in units]
    ar = [jnp.concatenate([cut(at_b, u), cut(rt_b, u)], axis=0) for u in units]
    g_b = [mm(ar[i], cut(bt_b, units[i]), NT) for i in idx]
    g_k = [mm(ar[i], cut(kt_b, units[i]), NT) for i in idx]
    n_ab = [jnp.where(strict, g_b[i][:L], 0.0) for i in idx]
    a_ak = [jnp.where(strict, g_k[i][:L], 0.0) for i in idx]
    a_rb = [jnp.where(incl, g_b[i][L:], 0.0).astype(BF16) for i in idx]
    a_rk = [jnp.where(incl, g_k[i][L:], 0.0) for i in idx]
    akv = [mm(a_ak[i], vh[i]) for i in idx]
    eye_l = jnp.where(ti == tj, 1.0, 0.0)
    tinv = [eye_l + n_ab[i] for i in idx]
    pw = n_ab
    for _ in range(int(math.log2(L)) - 1):
        pw = [mm(pw[i], pw[i]) for i in idx]
        tinv = [tinv[i] + mm(pw[i], tinv[i]) for i in idx]
    tinv = [t.astype(BF16) for t in tinv]
    a_t = [mm(tinv[i], ath[i]).astype(BF16) for i in idx]
    y = [mm(tinv[i], akv[i]).astype(BF16) for i in idx]
    m_c = [jnp.where(eye, w_last[units[i][0]][:, sls[units[i][1]]], 0.0) + mm(a_t[i], cut(bh_b, units[i]), TN)
           for i in idx]
    c_c = [mm(y[i], cut(bh_b, units[i]), TN) + mm(vh[i], cut(kh_b, units[i]), TN) for i in idx]
    q_c = [cut(rt, units[i]) + mm(a_rb[i], a_t[i]) for i in idx]
    o_loc = [mm(a_rb[i], y[i]) + mm(a_rk[i], vh[i]) for i in idx]
    s = [state[h] for h in heads]
    for ci in range(nc):
        for h in heads:
            i = ci * N_RWKV_HEADS + h
            o = o_loc[i] + _dot3(q_c[i], s[h], NT)
            s[h] = _dot3(s[h], m_c[i]) + c_c[i]
            mu = jnp.mean(o, axis=-1, keepdims=True)
            d = o - mu
            var = jnp.mean(d * d, axis=-1, keepdims=True)
            o_ref[rows[ci], sls[h]] = d * lax.rsqrt(var + RWKV_GN_EPS)
    for h in heads:
        state[h] = s[h]
    bonus = _dot_exact_rhs(r * k * rk_ref[...], ones_ref[...], nsplit=3) * v
    o_ref[...] = (o_ref[...] * lnw_ref[...] + lnb_ref[...] + bonus) * g_ref[...]


def _rwkv_scan(r, lw, k, v, kk, a, g, r_k, ln_w, ln_b):
    B, S, DR = r.shape
    L = min(L_CHUNK, S)
    lb = min(L * CHUNKS_PER_STEP, S)
    tri = jnp.asarray(np.kron(np.eye(lb // L), np.tril(np.ones((L, L)))), BF16)
    row = lambda z: z.reshape(1, -1)
    tok = pl.BlockSpec((None, lb, DR), lambda b, c: (b, c, 0))
    full = lambda z: pl.BlockSpec(z.shape, lambda b, c: (0,) * z.ndim)
    consts = [row(r_k), row(ln_w), row(ln_b), tri, _head_ones(DR, RWKV_HEAD)]
    return pl.pallas_call(
        functools.partial(_rwkv_kernel, L=L),
        grid=(B, S // lb),
        in_specs=[tok] * 7 + [full(z) for z in consts],
        out_specs=tok,
        out_shape=jax.ShapeDtypeStruct((B, S, DR), F32),
        scratch_shapes=[pltpu.VMEM((N_RWKV_HEADS, RWKV_HEAD, RWKV_HEAD), F32)],
        compiler_params=_cparams(("arbitrary", "arbitrary")),
        name="rwkv_scan",
    )(r, lw, k, v, kk, a, g, *consts)


def _alibi_cols():
    slope = np.asarray([2.0 ** (-8.0 * (h + 1) / N_ATT_HEADS) * LOG2E for h in range(N_ATT_HEADS)], np.float32)
    c_hi = slope.astype(BF16).astype(np.float32)
    c_lo = (slope - c_hi).astype(BF16).astype(np.float32)
    t = np.zeros((N_ATT_HEADS, LANES), np.float32)
    t[:, 1], t[:, 2], t[:, 3], t[:, 4] = LANES * c_hi, LANES * c_lo, c_hi, c_lo
    return jnp.asarray(t.astype(BF16))


def _dsa_kernel(iq_ref, iw_ref, qa_ref, ik_ref, ca_ref, cat_ref, wuv_ref, tril_ref, acol_ref, o_ref, key_ref,
                bias_ref, lg_ref, p_ref, *, topk, q_off, select):
    qi = pl.program_id(1) + q_off
    tq = qa_ref.shape[0]
    sk = ca_ref.shape[0]
    kc = min(KEY_CHUNK, sk)
    chunks = [slice(c * kc, (c + 1) * kc) for c in range(sk // kc)]
    tpos = qi * tq + lax.broadcasted_iota(I32, (1, tq), 1)
    srow = lax.broadcasted_iota(I32, (kc, tq), 0)
    NT = (((1,), (1,)), ((), ()))

    if not select:
        for c, cs in enumerate(chunks):
            bias_ref[cs, :] = jnp.where(srow + c * kc <= tpos, 0.0, NEG_BIG)
    else:
        iw_t = jnp.transpose(iw_ref[...])
        for c, cs in enumerate(chunks):
            ikc = ik_ref[cs, :]
            score = jnp.zeros((kc, tq), F32)
            for h in range(IDX_HEADS):
                s = lax.dot_general(ikc, iq_ref[:, h * LANES:(h + 1) * LANES], NT, preferred_element_type=F32)
                score = score + iw_t[h:h + 1, :] * jnp.maximum(s, 0.0)
            bits = pltpu.bitcast(score + 0.0, I32)
            key = bits ^ ((bits >> 31) & 0x7FFFFFFF)
            key_ref[cs, :] = jnp.where(srow + c * kc <= tpos, key, INT_MIN)
        kcount = jnp.minimum(topk, tpos + 1).astype(F32)

        def sum_keys(x):
            part = jnp.sum(x.reshape(x.shape[0] // SUM_ROWS, SUM_ROWS, x.shape[1]), axis=0)
            return jnp.sum(part, axis=0, keepdims=True)

        def count_ge(cand):
            return sum_keys(jnp.where(key_ref[...] >= cand, 1.0, 0.0))

        thr = jnp.where(count_ge(jnp.zeros((1, tq), I32)) >= kcount, 0, INT_MIN).astype(I32)

        def bit_step(i, thr):
            cand = thr | (1 << (30 - i))
            return jnp.where(count_ge(cand) >= kcount, cand, thr)

        thr = lax.fori_loop(0, 31, bit_step, thr)

        need = kcount - sum_keys(jnp.where(key_ref[...] > thr, 1.0, 0.0))
        carry = jnp.zeros((1, tq), F32)
        for j in range(sk // LANES):
            sl = slice(j * LANES, (j + 1) * LANES)
            key = key_ref[sl, :]
            eq = key == thr
            e = jnp.where(eq, 1.0, 0.0)
            before = jnp.dot(tril_ref[...], e.astype(BF16), preferred_element_type=F32) + carry
            sel = (key > thr) | (eq & (before < need))
            bias_ref[sl, :] = jnp.where(sel, 0.0, NEG_BIG)
            carry = carry + jnp.sum(e, axis=0, keepdims=True)

    outs = []
    for h in range(N_ATT_HEADS):
        q_aug = jnp.concatenate([qa_ref[:, h * KV_LORA:(h + 1) * KV_LORA],
                                 jnp.broadcast_to(acol_ref[h:h + 1, :], (tq, LANES))], axis=-1)
        lg_all = lax.dot_general(ca_ref[...], q_aug, NT, preferred_element_type=F32)
        m = jnp.full((1, tq), -jnp.inf, F32)
        for cs in chunks:
            lg = lg_all[cs, :] + bias_ref[cs, :]
            lg_ref[cs, :] = lg
            m = jnp.maximum(m, jnp.max(lg, axis=0, keepdims=True))
        for cs in chunks:
            p_ref[cs, :] = jnp.exp2(lg_ref[cs, :] - m).astype(BF16)
        pv = jnp.dot(cat_ref[...], p_ref[...], preferred_element_type=F32)
        outs.append((pv[:KV_LORA, :] / pv[KV_LORA:KV_LORA + 1, :]).astype(BF16))
    o_lat_t = jnp.concatenate(outs, axis=0)
    o_ref[...] = lax.dot_general(o_lat_t, wuv_ref[...], (((0,), (0,)), ((), ())), preferred_element_type=F32)


def _dsa_attn(iq, iw, qabs, ik, ca, cat, w_uv):
    B, S, _ = iq.shape
    tq = min(TQ, S)
    topk = min(TOPK_MAX, S // 4)
    wuv_bd = _block_diag(w_uv).astype(BF16)
    tril = jnp.asarray(np.tril(np.ones((LANES, LANES)), -1), BF16)
    acol = _alibi_cols()
    full = lambda z: pl.BlockSpec(z.shape, lambda b, i: (0,) * z.ndim)
    nq = 1
    outs = []
    for q_off in range(0, S // tq, nq):
        sk = (q_off + nq) * tq
        tok = lambda n, q_off=q_off: pl.BlockSpec((None, tq, n), lambda b, i: (b, i + q_off, 0))
        seq = lambda n, sk=sk: pl.BlockSpec((None, sk, n), lambda b, i: (b, 0, 0))
        outs.append(pl.pallas_call(
            functools.partial(_dsa_kernel, topk=topk, q_off=q_off, select=sk > topk),
            grid=(B, nq),
            in_specs=[tok(IDX_HEADS * LANES), tok(LANES), tok(N_ATT_HEADS * KV_LORA), seq(LANES), seq(KV_LORA + LANES),
                      pl.BlockSpec((None, KV_LORA + LANES, sk), lambda b, i: (b, 0, 0)),
                      full(wuv_bd), full(tril), full(acol)],
            out_specs=pl.BlockSpec((None, tq, D_ATT), lambda b, i: (b, i, 0)),
            out_shape=jax.ShapeDtypeStruct((B, nq * tq, D_ATT), F32),
            scratch_shapes=[pltpu.VMEM((sk, tq), I32), pltpu.VMEM((sk, tq), F32), pltpu.VMEM((sk, tq), F32),
                            pltpu.VMEM((sk, tq), BF16)],
            compiler_params=_cparams(("arbitrary", "arbitrary")),
            name=f"dsa_attn_k{sk}",
        )(iq, iw, qabs, ik, ca, cat, wuv_bd, tril, acol))
    return jnp.concatenate(outs, axis=1)


def _layernorm_rows(y, g, b):
    mu = jnp.mean(y, axis=-1, keepdims=True)
    d = y - mu
    var = jnp.mean(d * d, axis=-1, keepdims=True)
    return d * lax.rsqrt(var + 1e-5) * g + b


def _mix_kernel(orw_ref, ods_ref, x_ref, gt_ref, sc_ref, sh_ref, wtop_ref, wbot_ref, g_ref, b_ref, wr_ref, br_ref,
                tril_ref, triu_ref, x1_o, u2_o, route_o, tbl_o, cnt_o, carry, *, alpha):
    first = (pl.program_id(0) == 0) & (pl.program_id(1) == 0)
    tm = x_ref.shape[0]

    @pl.when(first)
    def _():
        carry[...] = jnp.zeros_like(carry)

    mix = _bdot(orw_ref[...], wtop_ref[...]) + _bdot(ods_ref[...], wbot_ref[...])
    x1 = _layernorm_rows(alpha * x_ref[...] + (1.0 + gt_ref[...]) * mix, g_ref[...], b_ref[...])
    x1_o[...] = x1
    u2 = x1 * (1.0 + sc_ref[...]) + sh_ref[...]
    u2_o[...] = u2

    lg = _dot3(u2, wr_ref[...]) + br_ref[...]
    lane = lax.broadcasted_iota(I32, (tm, LANES), 1)
    lane_f = lane.astype(F32)
    idxs, vals = [], []
    for _ in range(TOP_K_EXPERTS):
        m = jnp.max(lg, axis=-1, keepdims=True)
        idx = jnp.min(jnp.where(lg == m, lane_f, float(LANES)), axis=-1, keepdims=True).astype(I32)
        idxs.append(idx)
        vals.append(m)
        lg = jnp.where(lane == idx, -jnp.inf, lg)
    es = [jnp.exp(v - vals[0]) for v in vals]
    den = es[0] + es[1] + es[2] + es[3]
    hot = jnp.zeros((tm, LANES), F32)
    for idx in idxs:
        hot = hot + jnp.where(lane == idx, 1.0, 0.0)
    before = jnp.dot(tril_ref[...], hot.astype(BF16), preferred_element_type=F32)
    cnt = jnp.sum(hot, axis=0, keepdims=True)
    n_run = jnp.floor((cnt + (RUN_ALIGN - 1)) * (1.0 / RUN_ALIGN))
    n_stage = jnp.floor((n_run * RUN_ALIGN + (STAGE_ALIGN - 1)) * (1.0 / STAGE_ALIGN))
    off = jnp.dot(jnp.broadcast_to(n_stage, (SUBLANES, LANES)).astype(BF16), triu_ref[...],
                  preferred_element_type=F32)[0:1, :] * STAGE_ALIGN
    where_in_stage = off + before
    route = jnp.zeros((tm, LANES), F32)
    for k in range(TOP_K_EXPERTS):
        spos = jnp.sum(jnp.where(lane == idxs[k], where_in_stage, 0.0), axis=-1, keepdims=True)
        route = jnp.where(lane == k, idxs[k].astype(F32), route)
        route = jnp.where(lane == TOP_K_EXPERTS + k, es[k] / den, route)
        route = jnp.where(lane == 2 * TOP_K_EXPERTS + k, spos, route)
    route_o[...] = route
    sub = lax.broadcasted_iota(I32, (SUBLANES, LANES), 0)
    tbl_o[...] = jnp.where(sub == 0, n_run, jnp.where(sub == 1, off, jnp.where(sub == 2, carry[0:1, :], 0.0)))
    carry[0:1, :] = carry[0:1, :] + n_run * RUN_ALIGN
    cnt_o[...] = carry[...]


def _mix_out(o_rwkv, o_dsa, x, gt1, sc2, sh2, w_out, ln_g, ln_b, w_router, b_router):
    B, S, D = x.shape
    tm = min(TM_PROJ, S)
    alpha = 2.0 ** 0.25
    wtop = w_out[:D_RWKV].astype(BF16)
    wbot = w_out[D_RWKV:].astype(BF16)
    wr_p = jnp.pad(w_router, ((0, 0), (0, LANES - N_EXPERTS)))
    br_p = jnp.pad(b_router.reshape(1, -1), ((0, 0), (0, LANES - N_EXPERTS)), constant_values=NEG_BIG)
    tril = jnp.asarray(np.tril(np.ones((tm, tm)), -1), BF16)
    triu = jnp.asarray(np.triu(np.ones((LANES, LANES)), 1), BF16)
    row = lambda v: v.reshape(1, -1)
    tok = lambda n: pl.BlockSpec((None, tm, n), lambda b, i: (b, i, 0))
    mod = pl.BlockSpec((None, 1, D), lambda b, i: (b, 0, 0))
    full = lambda a: pl.BlockSpec(a.shape, lambda b, i: (0,) * a.ndim)
    consts = [wtop, wbot, row(ln_g), row(ln_b), wr_p, br_p, tril, triu]
    return pl.pallas_call(
        functools.partial(_mix_kernel, alpha=alpha),
        grid=(B, S // tm),
        in_specs=[tok(D_RWKV), tok(D_ATT), tok(D), mod, mod, mod] + [full(a) for a in consts],
        out_specs=[tok(D), tok(D), tok(LANES), pl.BlockSpec((None, None, SUBLANES, LANES), lambda b, i: (b, i, 0, 0)),
                   pl.BlockSpec((SUBLANES, LANES), lambda b, i: (0, 0))],
        out_shape=[jax.ShapeDtypeStruct((B, S, D), F32), jax.ShapeDtypeStruct((B, S, D), F32),
                   jax.ShapeDtypeStruct((B, S, LANES), F32),
                   jax.ShapeDtypeStruct((B, S // tm, SUBLANES, LANES), F32),
                   jax.ShapeDtypeStruct((SUBLANES, LANES), F32)],
        scratch_shapes=[pltpu.VMEM((8, LANES), F32)],
        compiler_params=_cparams(("arbitrary", "arbitrary")),
        name="mix_out",
    )(o_rwkv, o_dsa, x, gt1, sc2, sh2, *consts)


RUN_ALIGN = SUBLANES
STAGE_ALIGN = 16
STAGE_ROWS = 1536
STAGE_CHUNK = 256


def _run_copies(nrun_ref, tile, copy_of):
    for e in range(N_EXPERTS):
        pieces = (nrun_ref[tile * N_EXPERTS + e] * RUN_ALIGN + STAGE_ALIGN - 1) // STAGE_ALIGN

        def piece(j, carry, e=e):
            copy_of(e, j).start(priority=e % 2)
            return carry

        lax.fori_loop(0, pieces, piece, 0)


def _dispatch_kernel(nrun_ref, off_ref, start_ref, npiece_ref, nstage_ref, zs_ref, zn_ref, tail_ref,
                     u_ref, route_ref, xs_out, stag, zeros, sem, zsem):
    i = pl.program_id(0)
    n = pl.num_programs(0)
    tm = u_ref.shape[0]
    bm = zeros.shape[0]
    slot = i % 2

    @pl.when(i == 0)
    def _():
        zeros[...] = jnp.zeros_like(zeros)
        fills = []
        for e in range(N_EXPERTS):
            for b in range(bm.bit_length()):
                rows = RUN_ALIGN << b
                if rows > bm:
                    break
                done = (zn_ref[e] >> (b + 1)) << (b + 1)
                dst = pl.multiple_of(zs_ref[e] + done * RUN_ALIGN, RUN_ALIGN)
                fills.append(((zn_ref[e] >> b) & 1 == 1,
                              pltpu.make_async_copy(zeros.at[pl.ds(0, rows)], xs_out.at[pl.ds(dst, rows)], zsem)))
        for pred, cp in fills:
            pl.when(pred)(cp.start)

        def tail_copy(j):
            return pltpu.make_async_copy(zeros, xs_out.at[pl.ds(pl.multiple_of(j * bm, bm), bm)], zsem)

        lax.fori_loop(tail_ref[0], tail_ref[1], lambda j, c: (tail_copy(j).start(), c)[1], 0)
        for pred, cp in fills:
            pl.when(pred)(cp.wait)
        lax.fori_loop(tail_ref[0], tail_ref[1], lambda j, c: (tail_copy(j).wait(), c)[1], 0)

    route_t = jnp.transpose(route_ref[...])
    spos = [route_t[2 * TOP_K_EXPERTS + k:2 * TOP_K_EXPERTS + k + 1, :].astype(I32) for k in range(TOP_K_EXPERTS)]
    ub = u_ref[...].astype(BF16)
    srow = lax.broadcasted_iota(I32, (STAGE_CHUNK, tm), 0)
    for c in range(STAGE_ROWS // STAGE_CHUNK):
        @pl.when(c * STAGE_CHUNK < nstage_ref[i])
        def _(c=c):
            rows = srow + c * STAGE_CHUNK
            sel = (rows == spos[0]) | (rows == spos[1]) | (rows == spos[2]) | (rows == spos[3])
            stag[slot, c * STAGE_CHUNK:(c + 1) * STAGE_CHUNK, :] = jnp.dot(
                jnp.where(sel, 1.0, 0.0).astype(BF16), ub, preferred_element_type=F32)

    def piece_copy(s, tile):
        def copy_of(e, j):
            src = pl.multiple_of(off_ref[tile * N_EXPERTS + e] + j * STAGE_ALIGN, STAGE_ALIGN)
            dst = pl.multiple_of(start_ref[tile * N_EXPERTS + e] + j * STAGE_ALIGN, RUN_ALIGN)
            return pltpu.make_async_copy(stag.at[s, pl.ds(src, STAGE_ALIGN)], xs_out.at[pl.ds(dst, STAGE_ALIGN)], sem)
        return copy_of

    def drain(tile):
        def w(j, carry):
            pltpu.make_async_copy(stag.at[0, pl.ds(0, STAGE_ALIGN)], xs_out.at[pl.ds(0, STAGE_ALIGN)], sem).wait()
            return carry
        lax.fori_loop(0, npiece_ref[tile], w, 0)

    @pl.when(i > 0)
    def _():
        drain(i - 1)

    _run_copies(nrun_ref, i, piece_copy(slot, i))

    @pl.when(i == n - 1)
    def _():
        drain(i)


def _moe_dispatch(u2, route, tables, n_rows):
    T, D = u2.shape
    tm = TM_ROUTE
    bm = BM_EXPERT
    return pl.pallas_call(
        _dispatch_kernel,
        grid_spec=pltpu.PrefetchScalarGridSpec(
            num_scalar_prefetch=len(tables),
            grid=(T // tm,),
            in_specs=[pl.BlockSpec((tm, D), lambda i, *_: (i, 0)),
                      pl.BlockSpec((tm, LANES), lambda i, *_: (i, 0))],
            out_specs=pl.BlockSpec(memory_space=pl.ANY),
            scratch_shapes=[pltpu.VMEM((2, STAGE_ROWS, D), F32), pltpu.VMEM((bm, D), F32),
                            pltpu.SemaphoreType.DMA(()), pltpu.SemaphoreType.DMA(())],
        ),
        out_shape=jax.ShapeDtypeStruct((n_rows, D), F32),
        compiler_params=_cparams(("arbitrary",)),
        name="moe_dispatch",
    )(*tables, u2, route)


GU_GROUP = 2 * LANES


def _deinterleave_perm():
    p = np.zeros((GU_GROUP, GU_GROUP), np.float32)
    l = np.arange(LANES)
    p[2 * l, l] = 1.0
    p[2 * l + 1, LANES + l] = 1.0
    return jnp.asarray(p, BF16)


def _expert_kernel(be_ref, nb_ref, valid_ref, xs_ref, wgu_hbm, bgu_ref, wd_hbm, bd_ref, perm_ref, ys_ref,
                   wg_buf, wd_buf, wp, wdb, sem):
    i = pl.program_id(0)
    bm = xs_ref.shape[0]
    e = be_ref[i]
    used = i < nb_ref[0]
    new_expert = (i == 0) | (e != be_ref[jnp.maximum(i - 1, 0)])
    n_groups = wp.shape[1] // GU_GROUP
    n_experts = wgu_hbm.shape[0]

    def fetch(ex, s):
        return (pltpu.make_async_copy(wgu_hbm.at[ex], wg_buf.at[s], sem.at[0, s]),
                pltpu.make_async_copy(wd_hbm.at[ex], wd_buf.at[s], sem.at[1, s]))

    @pl.when(used & new_expert)
    def _():
        s = e % 2

        @pl.when(i == 0)
        def _():
            for cp in fetch(e, s):
                cp.start()

        for cp in fetch(e, s):
            cp.wait()

        @pl.when(e + 1 < n_experts)
        def _():
            for cp in fetch(e + 1, 1 - s):
                cp.start()

        for j in range(n_groups):
            sl = slice(j * GU_GROUP, (j + 1) * GU_GROUP)
            wp[:, sl] = jnp.dot(wg_buf[s, :, sl].astype(BF16), perm_ref[...], preferred_element_type=F32).astype(BF16)
        wdb[...] = wd_buf[s].astype(BF16)

    def compute(m):
        xb = xs_ref[:m, :].astype(BF16)
        gu = jnp.dot(xb, wp[...], preferred_element_type=F32) + bgu_ref[...]
        hs = []
        for j in range(n_groups):
            gate = jnp.minimum(gu[:, j * GU_GROUP:j * GU_GROUP + LANES], SWIGLU_LIMIT)
            up = jnp.clip(gu[:, j * GU_GROUP + LANES:(j + 1) * GU_GROUP], -SWIGLU_LIMIT, SWIGLU_LIMIT)
            hs.append(((up + 1.0) * (gate * _sigmoid(gate * SWIGLU_ALPHA))).astype(BF16))
        h = jnp.concatenate(hs, axis=-1)
        ys_ref[:m, :] = jnp.dot(h, wdb[...], preferred_element_type=F32) + bd_ref[...]

    half = bm // 2

    @pl.when(used & (valid_ref[i] > half))
    def _():
        compute(bm)

    @pl.when(used & (valid_ref[i] <= half))
    def _():
        compute(half)
        ys_ref[half:, :] = jnp.zeros((bm - half, ys_ref.shape[1]), F32)

    @pl.when(jnp.logical_not(used))
    def _():
        ys_ref[...] = jnp.zeros_like(ys_ref)


def _moe_experts(xs, block_e, n_used, valid, w_gu, b_gu_p, w_dn, b_dn):
    n_rows, D = xs.shape
    E, _, F2 = w_gu.shape
    bm = BM_EXPERT
    n_blocks = n_rows // bm
    perm = _deinterleave_perm()
    wspec = lambda shp: pl.BlockSpec((None,) + shp, lambda i, be, nb, va: (be[i], 0, 0))
    hbm = pl.BlockSpec(memory_space=pl.ANY)
    return pl.pallas_call(
        _expert_kernel,
        grid_spec=pltpu.PrefetchScalarGridSpec(
            num_scalar_prefetch=3,
            grid=(n_blocks,),
            in_specs=[pl.BlockSpec((bm, D), lambda i, be, nb, va: (jnp.minimum(i, nb[0] - 1), 0)),
                      hbm, wspec((1, F2)), hbm, wspec((1, D)),
                      pl.BlockSpec(perm.shape, lambda i, be, nb, va: (0, 0))],
            out_specs=pl.BlockSpec((bm, D), lambda i, be, nb, va: (i, 0)),
            scratch_shapes=[pltpu.VMEM((2, D, F2), F32), pltpu.VMEM((2, F2 // 2, D), F32),
                            pltpu.VMEM((D, F2), BF16), pltpu.VMEM((F2 // 2, D), BF16),
                            pltpu.SemaphoreType.DMA((2, 2))],
        ),
        out_shape=jax.ShapeDtypeStruct((n_rows, D), F32),
        compiler_params=pltpu.CompilerParams(dimension_semantics=("arbitrary",), vmem_limit_bytes=VMEM_LIMIT_EXPERTS),
        name="moe_experts",
    )(block_e, n_used, valid, xs, w_gu, b_gu_p, w_dn, b_dn, perm)


def _combine_kernel(nrun_ref, off_ref, start_ref, npiece_ref, nstage_ref, ys_ref, x1_ref, route_ref, gt_ref, g_ref,
                    b_ref, o_ref, stag, sem, *, alpha):
    i = pl.program_id(0)
    n = pl.num_programs(0)
    tm = x1_ref.shape[0]
    slot = i % 2

    def gather(tile, s):
        def copy_of(e, j):
            src = pl.multiple_of(start_ref[tile * N_EXPERTS + e] + j * STAGE_ALIGN, RUN_ALIGN)
            dst = pl.multiple_of(off_ref[tile * N_EXPERTS + e] + j * STAGE_ALIGN, STAGE_ALIGN)
            return pltpu.make_async_copy(ys_ref.at[pl.ds(src, STAGE_ALIGN)], stag.at[s, pl.ds(dst, STAGE_ALIGN)],
                                         sem.at[s])
        _run_copies(nrun_ref, tile, copy_of)

    @pl.when(i == 0)
    def _():
        stag[...] = jnp.zeros_like(stag)
        gather(0, 0)

    @pl.when(i + 1 < n)
    def _():
        gather(i + 1, 1 - slot)

    def w(j, carry):
        pltpu.make_async_copy(ys_ref.at[pl.ds(0, STAGE_ALIGN)], stag.at[slot, pl.ds(0, STAGE_ALIGN)],
                              sem.at[slot]).wait()
        return carry
    lax.fori_loop(0, npiece_ref[i], w, 0)

    route = route_ref[...]
    wide = lambda col: jnp.broadcast_to(col, (tm, STAGE_CHUNK))
    spos = [wide(route[:, 2 * TOP_K_EXPERTS + k:2 * TOP_K_EXPERTS + k + 1].astype(I32)) for k in range(TOP_K_EXPERTS)]
    gate = [wide(route[:, TOP_K_EXPERTS + k:TOP_K_EXPERTS + k + 1]) for k in range(TOP_K_EXPERTS)]
    scol = lax.broadcasted_iota(I32, (tm, STAGE_CHUNK), 1)
    acc_ref = o_ref
    acc_ref[...] = jnp.zeros_like(acc_ref)
    for c in range(STAGE_ROWS // STAGE_CHUNK):
        @pl.when(c * STAGE_CHUNK < nstage_ref[i])
        def _(c=c):
            cols = scol + c * STAGE_CHUNK
            wgt = jnp.zeros((tm, STAGE_CHUNK), F32)
            for k in range(TOP_K_EXPERTS):
                wgt = wgt + jnp.where(cols == spos[k], gate[k], 0.0)
            acc_ref[...] += jnp.dot(wgt.astype(BF16), stag[slot, c * STAGE_CHUNK:(c + 1) * STAGE_CHUNK, :].astype(BF16),
                                    preferred_element_type=F32)
    o_ref[...] = _layernorm_rows(alpha * x1_ref[...] + (1.0 + gt_ref[...]) * acc_ref[...], g_ref[...], b_ref[...])


def _moe_combine(ys, tables, x1, route, gt2, ln_g, ln_b, tiles_per_batch):
    T, D = x1.shape
    tm = TM_ROUTE
    row = lambda v: v.reshape(1, -1)
    return pl.pallas_call(
        functools.partial(_combine_kernel, alpha=2.0 ** 0.25),
        grid_spec=pltpu.PrefetchScalarGridSpec(
            num_scalar_prefetch=len(tables),
            grid=(T // tm,),
            in_specs=[pl.BlockSpec(memory_space=pl.ANY),
                      pl.BlockSpec((tm, D), lambda i, *_: (i, 0)),
                      pl.BlockSpec((tm, LANES), lambda i, *_: (i, 0)),
                      pl.BlockSpec((None, 1, D), lambda i, *_: (i // tiles_per_batch, 0, 0)),
                      pl.BlockSpec((1, D), lambda i, *_: (0, 0)),
                      pl.BlockSpec((1, D), lambda i, *_: (0, 0))],
            out_specs=pl.BlockSpec((tm, D), lambda i, *_: (i, 0)),
            scratch_shapes=[pltpu.VMEM((2, STAGE_ROWS, D), F32), pltpu.SemaphoreType.DMA((2,))],
        ),
        out_shape=jax.ShapeDtypeStruct((T, D), F32),
        compiler_params=_cparams(("arbitrary",)),
        name="moe_combine",
    )(*tables, ys, x1, route, gt2, row(ln_g), row(ln_b))


def _moe_and_norm(x1, u2, route, tbl, totals, gt2, w_gu, b_gu, w_dn, b_dn, ln_g, ln_b):
    B, S, D = x1.shape
    T = B * S
    bm = BM_EXPERT
    tm = TM_ROUTE
    assert T % tm == 0 and TM_PROJ == tm and STAGE_ROWS >= tm * TOP_K_EXPERTS + N_EXPERTS * (STAGE_ALIGN - 1)
    n_tiles = T // tm
    max_rows = T * TOP_K_EXPERTS + n_tiles * N_EXPERTS * (RUN_ALIGN - 1) + N_EXPERTS * STAGE_ALIGN
    n_blocks = -(-max_rows // bm) + N_EXPERTS
    tot = totals[0, :N_EXPERTS].astype(I32)
    padded = (tot + STAGE_ALIGN + bm - 1) // bm * bm
    pad_ends = jnp.cumsum(padded)
    pad_starts = pad_ends - padded
    t3 = tbl.reshape(n_tiles, SUBLANES, LANES)[:, :, :N_EXPERTS].astype(I32)
    nrun, off, base = t3[:, 0, :], t3[:, 1, :], t3[:, 2, :]
    start = pad_starts[None, :] + base
    pieces = (nrun * RUN_ALIGN + STAGE_ALIGN - 1) // STAGE_ALIGN
    npiece = jnp.sum(pieces, axis=1)
    nstage = jnp.sum(pieces, axis=1) * STAGE_ALIGN
    flat = lambda z: z.reshape(-1).astype(I32)
    run_tables = [flat(nrun), flat(off), flat(start), flat(npiece), flat(nstage)]
    zs = pad_starts + tot
    zn = (pad_ends - zs) // RUN_ALIGN
    n_used = (pad_ends[-1:] // bm).astype(I32)
    tail = jnp.concatenate([n_used, jnp.full((1,), n_blocks, I32)])
    blk_row = jnp.arange(n_blocks, dtype=I32) * bm
    block_e = jnp.minimum(jnp.sum((blk_row[:, None] >= pad_ends[None, :]).astype(I32), axis=1), N_EXPERTS - 1)
    E, F2 = b_gu.shape
    b_gu_p = b_gu.reshape(E, F2 // GU_GROUP, LANES, 2).transpose(0, 1, 3, 2).reshape(E, 1, F2)
    route2 = route.reshape(T, LANES)
    xs = _moe_dispatch(u2.reshape(T, D), route2, run_tables + [flat(zs), flat(zn), tail], n_blocks * bm)
    valid = jnp.clip(zs[block_e] - blk_row, 0, bm)
    ys = _moe_experts(xs, block_e, n_used, valid, w_gu, b_gu_p, w_dn, b_dn[:, None, :])
    out = _moe_combine(ys, run_tables, x1.reshape(T, D), route2, gt2, ln_g, ln_b, S // tm)
    return out.reshape(B, S, D)


def kernel(x, c, w_ada, b_ada, w_in, shift_mu, rwkv_w0, rwkv_w2, rwkv_a0, rwkv_a2, rwkv_g2, rwkv_k_k, rwkv_k_a, rwkv_r_k, rwkv_ln_w, rwkv_ln_b, mla_q_norm, mla_w_q_up, mla_kv_norm, mla_w_uk, mla_w_uv, idx_w_q, idx_ln_g, idx_ln_b, w_out, ln1_g, ln1_b, w_router, b_router, w_gu, b_gu, w_dn, b_dn, ln2_g, ln2_b):
    depth = w_ada.shape[0]
    assert depth == 1, "DeepNorm constants below are for a single layer"
    l = 0
    mod = _ada_mod(c, w_ada[l], b_ada[l])
    sh1, sc1, gt1, sh2, sc2, gt2 = [m[:, None, :] for m in jnp.split(mod, 6, axis=-1)]
    r, lw, k, v, kk, a, g, qabs, iq, ik, iw, cl, clt = _in_proj(
        x, sc1, sh1, w_in[l], shift_mu[l], rwkv_w0[l], rwkv_w2[l], rwkv_a0[l], rwkv_a2[l], rwkv_g2[l], rwkv_k_k[l],
        rwkv_k_a[l], mla_q_norm[l], mla_w_q_up[l], mla_w_uk[l], mla_kv_norm[l], idx_w_q[l], idx_ln_g[l], idx_ln_b[l])
    o_rwkv = _rwkv_scan(r, lw, k, v, kk, a, g, rwkv_r_k[l], rwkv_ln_w[l], rwkv_ln_b[l])
    o_dsa = _dsa_attn(iq, iw, qabs, ik, cl, clt, mla_w_uv[l])
    x1, u2, route, tbl, totals = _mix_out(o_rwkv, o_dsa, x, gt1, sc2, sh2, w_out[l], ln1_g[l], ln1_b[l],
                                          w_router[l], b_router[l])
    return _moe_and_norm(x1, u2, route, tbl, totals, gt2, w_gu[l], b_gu[l], w_dn[l], b_dn[l], ln2_g[l], ln2_b[l])
```

```python
import functools
import math

import jax
import jax.numpy as jnp
import numpy as np
from jax import lax
from jax.experimental import pallas as pl
from jax.experimental.pallas import tpu as pltpu

F32 = jnp.float32
BF16 = jnp.bfloat16
I32 = jnp.int32

RWKV_HEAD = 64
N_RWKV_HEADS = 8
D_RWKV = RWKV_HEAD * N_RWKV_HEADS
RANK_DECAY = 64
RANK_A = 64
RANK_GATE = 128
RWKV_GN_EPS = 64e-5
ATT_HEAD = 64
N_ATT_HEADS = 8
D_ATT = ATT_HEAD * N_ATT_HEADS
Q_LORA = 256
KV_LORA = 128
IDX_HEADS = 8
IDX_DIM = 64
TOPK_MAX = 256
N_EXPERTS = 32
TOP_K_EXPERTS = 4
SWIGLU_LIMIT = 7.0
SWIGLU_ALPHA = 1.702
NEG_BIG = -1e30
LOG2E = 1.4426950408889634
INT_MIN = -(2 ** 31)

LANES = 128
SUBLANES = 8
VMEM_LIMIT = 48 * 1024 * 1024
VMEM_LIMIT_EXPERTS = 58 * 1024 * 1024

TM_IN = 512
TM_PROJ = 256
L_CHUNK = 64
CHUNKS_PER_STEP = 4
TQ = 256
KEY_CHUNK = 128
SUM_ROWS = 64
TM_ROUTE = 256
BM_EXPERT = 1024

_SEG = (("r", 512, 512), ("k", 512, 512), ("v", 512, 512), ("w", 64, 128), ("a", 64, 128), ("g", 128, 128),
        ("q", 256, 256), ("kv", 128, 128), ("ik", 64, 128), ("iw", 8, 128))
N_SHIFT_P = 512 * 3 + 128 * 3
N_IN_P = sum(s[2] for s in _SEG)


def _cparams(sem):
    return pltpu.CompilerParams(dimension_semantics=sem, vmem_limit_bytes=VMEM_LIMIT)


def _bdot(a, b):
    return jnp.dot(a.astype(BF16), b.astype(BF16), preferred_element_type=F32)


def _bdot_nt(a, b):
    return lax.dot_general(a.astype(BF16), b.astype(BF16), (((1,), (1,)), ((), ())), preferred_element_type=F32)


def _bdot_tn(a, b):
    return lax.dot_general(a.astype(BF16), b.astype(BF16), (((0,), (0,)), ((), ())), preferred_element_type=F32)


def _split2(a):
    hi = a.astype(BF16)
    lo = (a - hi.astype(F32)).astype(BF16)
    return hi, lo


def _split3(a):
    hi = a.astype(BF16)
    r1 = a - hi.astype(F32)
    mid = r1.astype(BF16)
    lo = (r1 - mid.astype(F32)).astype(BF16)
    return hi, mid, lo


def _dot3(a, b, dims=(((1,), (0,)), ((), ()))):
    ah, al = _split2(a)
    bh, bl = _split2(b)
    d = functools.partial(lax.dot_general, dimension_numbers=dims, preferred_element_type=F32)
    return d(ah, bh) + (d(ah, bl) + d(al, bh))


def _dot_exact_rhs(a, b_exact, nsplit=3):
    parts = _split3(a) if nsplit == 3 else _split2(a)
    acc = None
    for p in parts[::-1]:
        t = jnp.dot(p, b_exact, preferred_element_type=F32)
        acc = t if acc is None else acc + t
    return acc


def _dot_exact_lhs(a_exact, b, nsplit=3):
    parts = _split3(b) if nsplit == 3 else _split2(b)
    acc = None
    for p in parts[::-1]:
        t = jnp.dot(a_exact, p, preferred_element_type=F32)
        acc = t if acc is None else acc + t
    return acc


def _sigmoid(x):
    return 1.0 / (1.0 + jnp.exp(-x))


def _softplus(x):
    return jnp.maximum(x, 0.0) + jnp.log(1.0 + jnp.exp(-jnp.abs(x)))


def _ada_kernel(c_ref, w_ref, b_ref, o_ref):
    c = c_ref[...]
    o_ref[...] = _dot3(c * _sigmoid(c), w_ref[...]) + b_ref[...]


def _ada_mod(c, w_ada, b_ada):
    B, D = c.shape
    N = w_ada.shape[1]
    tn = 1024
    return pl.pallas_call(
        _ada_kernel,
        grid=(N // tn,),
        in_specs=[pl.BlockSpec((B, D), lambda j: (0, 0)),
                  pl.BlockSpec((D, tn), lambda j: (0, j)),
                  pl.BlockSpec((1, tn), lambda j: (0, j))],
        out_specs=pl.BlockSpec((B, tn), lambda j: (0, j)),
        out_shape=jax.ShapeDtypeStruct((B, N), F32),
        compiler_params=_cparams(("arbitrary",)),
        name="ada_mod",
    )(c, w_ada, b_ada.reshape(1, N))


def _in_proj_kernel(x_ref, sc_ref, sh_ref, win_ref, mu_ref, w0_ref, w2_ref, a0_ref, a2_ref, g2_ref, kk_ref, ka_ref,
                    ones_ref, qn_ref, wq_ref, wuk_ref, kvn_ref, wiq_ref, ig_ref, ib_ref,
                    r_o, lw_o, k_o, v_o, kkn_o, a_o, g_o, qabs_o, iq_o, ik_o, iw_o, cl_o, clt_o, carry):
    i = pl.program_id(1)
    tm = x_ref.shape[0]

    @pl.when(i == 0)
    def _():
        carry[...] = jnp.zeros_like(carry)

    u = x_ref[...] * (1.0 + sc_ref[...]) + sh_ref[...]
    p = _bdot(u, win_ref[...])
    ps = p[:, :N_SHIFT_P]
    rows = lax.broadcasted_iota(I32, (tm, 1), 0)
    prev = jnp.where(rows == 0, carry[0:1, :], pltpu.roll(ps, 1, 0))
    carry[0:1, :] = ps[tm - 1:tm, :]
    ps = ps + mu_ref[...] * (prev - ps)

    pr, pk, pv = ps[:, 0:512], ps[:, 512:1024], ps[:, 1024:1536]
    pw, pa, pg = ps[:, 1536:1664], ps[:, 1664:1792], ps[:, 1792:1920]
    w_log = -_softplus(-(w0_ref[...] + _dot3(jnp.tanh(pw), w2_ref[...]))) - 0.5
    lw_o[...] = -jnp.exp(w_log)
    a = _sigmoid(a0_ref[...] + _dot3(pa, a2_ref[...]))
    g_o[...] = _dot3(_sigmoid(pg), g2_ref[...])
    kk = pk * kk_ref[...]
    ssq = _dot_exact_rhs(kk * kk, ones_ref[...], nsplit=2)
    kkn_o[...] = kk / jnp.maximum(jnp.sqrt(ssq), 1e-12)
    k_o[...] = pk * (1.0 + (a - 1.0) * ka_ref[...])
    r_o[...] = pr
    v_o[...] = pv
    a_o[...] = a

    pq, pkv = p[:, 1920:2176], p[:, 2176:2304]
    pik, piw = p[:, 2304:2432], p[:, 2432:2560]
    q_lat = pq * lax.rsqrt(jnp.mean(pq * pq, axis=-1, keepdims=True) + 1e-6) * qn_ref[...]
    q = _bdot(q_lat, wq_ref[...])
    qabs_o[...] = (_bdot(q, wuk_ref[...]) * (ATT_HEAD ** -0.5 * LOG2E)).astype(BF16)
    c_lat = pkv * lax.rsqrt(jnp.mean(pkv * pkv, axis=-1, keepdims=True) + 1e-6) * kvn_ref[...]
    spos = i * tm + rows
    s_hi = (spos >> 7).astype(F32)
    s_lo = (spos & (LANES - 1)).astype(F32)
    lane_t = lax.broadcasted_iota(I32, (1, LANES), 1)
    extra = jnp.where(lane_t == 0, 1.0, jnp.where((lane_t == 1) | (lane_t == 2), s_hi,
                                                  jnp.where((lane_t == 3) | (lane_t == 4), s_lo, 0.0)))
    c_aug = jnp.concatenate([c_lat, extra], axis=-1)
    cl_o[...] = c_aug.astype(BF16)
    clt_o[...] = jnp.transpose(c_aug).astype(BF16)
    iq_o[...] = (_bdot(q_lat, wiq_ref[...]) * (IDX_DIM ** -0.5)).astype(BF16)
    lane = lax.broadcasted_iota(I32, (1, LANES), 1)
    valid = lane < IDX_DIM
    mu = jnp.sum(pik, axis=-1, keepdims=True) * (1.0 / IDX_DIM)
    dlt = jnp.where(valid, pik - mu, 0.0)
    var = jnp.sum(dlt * dlt, axis=-1, keepdims=True) * (1.0 / IDX_DIM)
    ik_o[...] = jnp.where(valid, dlt * lax.rsqrt(var + 1e-5) * ig_ref[...] + ib_ref[...], 0.0).astype(BF16)
    iw_o[...] = piw * (IDX_HEADS ** -0.5)


def _pad_cols(w, widths):
    parts, o = [], 0
    for true, padded in widths:
        seg = w[..., o:o + true]
        if padded > true:
            seg = jnp.pad(seg, [(0, 0)] * (w.ndim - 1) + [(0, padded - true)])
        parts.append(seg)
        o += true
    return jnp.concatenate(parts, axis=-1)


def _pad_rows(w, rows):
    return jnp.pad(w, ((0, rows - w.shape[0]), (0, 0)))


def _block_diag(blocks):
    H, a, b = blocks.shape
    eye = jnp.eye(H, dtype=blocks.dtype)
    return (eye[:, None, :, None] * blocks[:, :, None, :]).reshape(H * a, H * b)


def _head_ones(n, head):
    idx = np.arange(n) // head
    return jnp.asarray(idx[:, None] == idx[None, :], BF16)


def _in_proj(x, sc1, sh1, w_in, shift_mu, w0, w2, a0, a2, g2, k_k, k_a, q_norm, w_q_up, w_uk, kv_norm, idx_w_q,
             idx_ln_g, idx_ln_b):
    B, S, D = x.shape
    tm = min(TM_IN, S)
    widths = tuple((s[1], s[2]) for s in _SEG)
    win_p = _pad_cols(w_in, widths).astype(BF16)
    mu_p = _pad_cols(shift_mu.reshape(1, -1), widths[:6])
    w2_p = _pad_rows(w2, LANES)
    a2_p = _pad_rows(a2, LANES)
    wuk_bd = _block_diag(w_uk).astype(BF16)
    wiq_p = _pad_cols(idx_w_q, ((IDX_DIM, LANES),) * IDX_HEADS).astype(BF16)
    ig_p = _pad_cols(idx_ln_g.reshape(1, -1), ((IDX_DIM, LANES),))
    ib_p = _pad_cols(idx_ln_b.reshape(1, -1), ((IDX_DIM, LANES),))
    row = lambda v: v.reshape(1, -1)
    tok = lambda n: pl.BlockSpec((None, tm, n), lambda b, i: (b, i, 0))
    mod = pl.BlockSpec((None, 1, D), lambda b, i: (b, 0, 0))
    full = lambda a: pl.BlockSpec(a.shape, lambda b, i: (0,) * a.ndim)
    consts = [win_p, mu_p, row(w0), w2_p, row(a0), a2_p, g2, row(k_k), row(k_a), _head_ones(D_RWKV, RWKV_HEAD),
              row(q_norm), w_q_up.astype(BF16), wuk_bd, row(kv_norm), wiq_p, ig_p, ib_p]
    outs = [(D_RWKV, F32)] * 7 + [(N_ATT_HEADS * KV_LORA, BF16), (IDX_HEADS * LANES, BF16), (LANES, BF16),
                                  (LANES, F32), (KV_LORA + LANES, BF16)]
    return pl.pallas_call(
        _in_proj_kernel,
        grid=(B, S // tm),
        in_specs=[tok(D), mod, mod] + [full(a) for a in consts],
        out_specs=[tok(n) for n, _ in outs] + [pl.BlockSpec((None, KV_LORA + LANES, tm), lambda b, i: (b, 0, i))],
        out_shape=[jax.ShapeDtypeStruct((B, S, n), dt) for n, dt in outs]
        + [jax.ShapeDtypeStruct((B, KV_LORA + LANES, S), BF16)],
        scratch_shapes=[pltpu.VMEM((8, N_SHIFT_P), F32)],
        compiler_params=_cparams(("arbitrary", "arbitrary")),
        name="in_proj",
    )(x, sc1, sh1, *consts)


def _rwkv_kernel(r_ref, lw_ref, k_ref, v_ref, kk_ref, a_ref, g_ref, rk_ref, lnw_ref, lnb_ref, tri_ref, ones_ref,
                 o_ref, state, *, L):
    c = pl.program_id(1)
    nc = r_ref.shape[0] // L

    @pl.when(c == 0)
    def _():
        state[...] = jnp.zeros_like(state)

    r, lw, k, v, kk, a = r_ref[...], lw_ref[...], k_ref[...], v_ref[...], kk_ref[...], a_ref[...]
    cum = _dot_exact_lhs(tri_ref[...], lw)
    rows = [slice(ci * L, (ci + 1) * L) for ci in range(nc)]
    last = [cum[(ci + 1) * L - 1:(ci + 1) * L, :] for ci in range(nc)]
    cum_last = jnp.concatenate([jnp.broadcast_to(z, (L, z.shape[1])) for z in last], axis=0)
    w_incl = jnp.exp(cum)
    w_inv = jnp.exp(-cum)
    w_rel = jnp.exp(cum_last - cum)
    w_last = [jnp.exp(z) for z in last]
    bvec = kk * a
    at = -kk * jnp.exp(cum - lw)
    rt = r * w_incl
    bt = bvec * w_inv
    kt = k * w_inv
    bh = bvec * w_rel
    kh = k * w_rel
    ti = lax.broadcasted_iota(I32, (L, L), 0)
    tj = lax.broadcasted_iota(I32, (L, L), 1)
    strict = tj < ti
    incl = tj <= ti
    eye = lax.broadcasted_iota(I32, (RWKV_HEAD, RWKV_HEAD), 0) == lax.broadcasted_iota(I32, (RWKV_HEAD, RWKV_HEAD), 1)
    NT = (((1,), (1,)), ((), ()))
    TN = (((0,), (0,)), ((), ()))
    heads = range(N_RWKV_HEADS)
    sls = [slice(h * RWKV_HEAD, (h + 1) * RWKV_HEAD) for h in heads]
    units = [(ci, h) for ci in range(nc) for h in heads]
    idx = range(len(units))
    mm = lambda x, y, dims=(((1,), (0,)), ((), ())): lax.dot_general(
        x.astype(BF16), y.astype(BF16), dims, preferred_element_type=F32)
    at_b, rt_b, bt_b, kt_b, bh_b, kh_b, v_b = [z.astype(BF16) for z in (at, rt, bt, kt, bh, kh, v)]
    cut = lambda z, u: z[rows[u[0]], sls[u[1]]]
    vh = [cut(v_b, u) for u in units]
    ath = [cut(at_b, u) for u in units]
    ar = [jnp.concatenate([cut(at_b, u), cut(rt_b, u)], axis=0) for u in units]
    g_b = [mm(ar[i], cut(bt_b, units[i]), NT) for i in idx]
    g_k = [mm(ar[i], cut(kt_b, units[i]), NT) for i in idx]
    n_ab = [jnp.where(strict, g_b[i][:L], 0.0) for i in idx]
    a_ak = [jnp.where(strict, g_k[i][:L], 0.0) for i in idx]
    a_rb = [jnp.where(incl, g_b[i][L:], 0.0).astype(BF16) for i in idx]
    a_rk = [jnp.where(incl, g_k[i][L:], 0.0) for i in idx]
    akv = [mm(a_ak[i], vh[i]) for i in idx]
    eye_l = jnp.where(ti == tj, 1.0, 0.0)
    tinv = [eye_l + n_ab[i] for i in idx]
    pw = n_ab
    for _ in range(int(math.log2(L)) - 1):
        pw = [mm(pw[i], pw[i]) for i in idx]
        tinv = [tinv[i] + mm(pw[i], tinv[i]) for i in idx]
    tinv = [t.astype(BF16) for t in tinv]
    a_t = [mm(tinv[i], ath[i]).astype(BF16) for i in idx]
    y = [mm(tinv[i], akv[i]).astype(BF16) for i in idx]
    m_c = [jnp.where(eye, w_last[units[i][0]][:, sls[units[i][1]]], 0.0) + mm(a_t[i], cut(bh_b, units[i]), TN)
           for i in idx]
    c_c = [mm(y[i], cut(bh_b, units[i]), TN) + mm(vh[i], cut(kh_b, units[i]), TN) for i in idx]
    q_c = [cut(rt, units[i]) + mm(a_rb[i], a_t[i]) for i in idx]
    o_loc = [mm(a_rb[i], y[i]) + mm(a_rk[i], vh[i]) for i in idx]
    s = [state[h] for h in heads]
    for ci in range(nc):
        for h in heads:
            i = ci * N_RWKV_HEADS + h
            o = o_loc[i] + _dot3(q_c[i], s[h], NT)
            s[h] = _dot3(s[h], m_c[i]) + c_c[i]
            mu = jnp.mean(o, axis=-1, keepdims=True)
            d = o - mu
            var = jnp.mean(d * d, axis=-1, keepdims=True)
            o_ref[rows[ci], sls[h]] = d * lax.rsqrt(var + RWKV_GN_EPS)
    for h in heads:
        state[h] = s[h]
    bonus = _dot_exact_rhs(r * k * rk_ref[...], ones_ref[...], nsplit=3) * v
    o_ref[...] = (o_ref[...] * lnw_ref[...] + lnb_ref[...] + bonus) * g_ref[...]


def _rwkv_scan(r, lw, k, v, kk, a, g, r_k, ln_w, ln_b):
    B, S, DR = r.shape
    L = min(L_CHUNK, S)
    lb = min(L * CHUNKS_PER_STEP, S)
    tri = jnp.asarray(np.kron(np.eye(lb // L), np.tril(np.ones((L, L)))), BF16)
    row = lambda z: z.reshape(1, -1)
    tok = pl.BlockSpec((None, lb, DR), lambda b, c: (b, c, 0))
    full = lambda z: pl.BlockSpec(z.shape, lambda b, c: (0,) * z.ndim)
    consts = [row(r_k), row(ln_w), row(ln_b), tri, _head_ones(DR, RWKV_HEAD)]
    return pl.pallas_call(
        functools.partial(_rwkv_kernel, L=L),
        grid=(B, S // lb),
        in_specs=[tok] * 7 + [full(z) for z in consts],
        out_specs=tok,
        out_shape=jax.ShapeDtypeStruct((B, S, DR), F32),
        scratch_shapes=[pltpu.VMEM((N_RWKV_HEADS, RWKV_HEAD, RWKV_HEAD), F32)],
        compiler_params=_cparams(("arbitrary", "arbitrary")),
        name="rwkv_scan",
    )(r, lw, k, v, kk, a, g, *consts)


def _alibi_cols():
    slope = np.asarray([2.0 ** (-8.0 * (h + 1) / N_ATT_HEADS) * LOG2E for h in range(N_ATT_HEADS)], np.float32)
    c_hi = slope.astype(BF16).astype(np.float32)
    c_lo = (slope - c_hi).astype(BF16).astype(np.float32)
    t = np.zeros((N_ATT_HEADS, LANES), np.float32)
    t[:, 1], t[:, 2], t[:, 3], t[:, 4] = LANES * c_hi, LANES * c_lo, c_hi, c_lo
    return jnp.asarray(t.astype(BF16))


def _dsa_kernel(iq_ref, iw_ref, qa_ref, ik_ref, ca_ref, cat_ref, wuv_ref, tril_ref, acol_ref, o_ref, key_ref,
                bias_ref, lg_ref, p_ref, *, topk, q_off, select):
    qi = pl.program_id(1) + q_off
    tq = qa_ref.shape[0]
    sk = ca_ref.shape[0]
    kc = min(KEY_CHUNK, sk)
    chunks = [slice(c * kc, (c + 1) * kc) for c in range(sk // kc)]
    tpos = qi * tq + lax.broadcasted_iota(I32, (1, tq), 1)
    srow = lax.broadcasted_iota(I32, (kc, tq), 0)
    NT = (((1,), (1,)), ((), ()))

    if not select:
        for c, cs in enumerate(chunks):
            bias_ref[cs, :] = jnp.where(srow + c * kc <= tpos, 0.0, NEG_BIG)
    else:
        iw_t = jnp.transpose(iw_ref[...])
        for c, cs in enumerate(chunks):
            ikc = ik_ref[cs, :]
            score = jnp.zeros((kc, tq), F32)
            for h in range(IDX_HEADS):
                s = lax.dot_general(ikc, iq_ref[:, h * LANES:(h + 1) * LANES], NT, preferred_element_type=F32)
                score = score + iw_t[h:h + 1, :] * jnp.maximum(s, 0.0)
            bits = pltpu.bitcast(score + 0.0, I32)
            key = bits ^ ((bits >> 31) & 0x7FFFFFFF)
            key_ref[cs, :] = jnp.where(srow + c * kc <= tpos, key, INT_MIN)
        kcount = jnp.minimum(topk, tpos + 1).astype(F32)

        def sum_keys(x):
            part = jnp.sum(x.reshape(x.shape[0] // SUM_ROWS, SUM_ROWS, x.shape[1]), axis=0)
            return jnp.sum(part, axis=0, keepdims=True)

        def count_ge(cand):
            return sum_keys(jnp.where(key_ref[...] >= cand, 1.0, 0.0))

        thr = jnp.where(count_ge(jnp.zeros((1, tq), I32)) >= kcount, 0, INT_MIN).astype(I32)

        def bit_step(i, thr):
            cand = thr | (1 << (30 - i))
            return jnp.where(count_ge(cand) >= kcount, cand, thr)

        thr = lax.fori_loop(0, 31, bit_step, thr)

        need = kcount - sum_keys(jnp.where(key_ref[...] > thr, 1.0, 0.0))
        carry = jnp.zeros((1, tq), F32)
        for j in range(sk // LANES):
            sl = slice(j * LANES, (j + 1) * LANES)
            key = key_ref[sl, :]
            eq = key == thr
            e = jnp.where(eq, 1.0, 0.0)
            before = jnp.dot(tril_ref[...], e.astype(BF16), preferred_element_type=F32) + carry
            sel = (key > thr) | (eq & (before < need))
            bias_ref[sl, :] = jnp.where(sel, 0.0, NEG_BIG)
            carry = carry + jnp.sum(e, axis=0, keepdims=True)

    outs = []
    for h in range(N_ATT_HEADS):
        q_aug = jnp.concatenate([qa_ref[:, h * KV_LORA:(h + 1) * KV_LORA],
                                 jnp.broadcast_to(acol_ref[h:h + 1, :], (tq, LANES))], axis=-1)
        lg_all = lax.dot_general(ca_ref[...], q_aug, NT, preferred_element_type=F32)
        m = jnp.full((1, tq), -jnp.inf, F32)
        for cs in chunks:
            lg = lg_all[cs, :] + bias_ref[cs, :]
            lg_ref[cs, :] = lg
            m = jnp.maximum(m, jnp.max(lg, axis=0, keepdims=True))
        for cs in chunks:
            p_ref[cs, :] = jnp.exp2(lg_ref[cs, :] - m).astype(BF16)
        pv = jnp.dot(cat_ref[...], p_ref[...], preferred_element_type=F32)
        outs.append((pv[:KV_LORA, :] / pv[KV_LORA:KV_LORA + 1, :]).astype(BF16))
    o_lat_t = jnp.concatenate(outs, axis=0)
    o_ref[...] = lax.dot_general(o_lat_t, wuv_ref[...], (((0,), (0,)), ((), ())), preferred_element_type=F32)


def _dsa_attn(iq, iw, qabs, ik, ca, cat, w_uv):
    B, S, _ = iq.shape
    tq = min(TQ, S)
    topk = min(TOPK_MAX, S // 4)
    wuv_bd = _block_diag(w_uv).astype(BF16)
    tril = jnp.asarray(np.tril(np.ones((LANES, LANES)), -1), BF16)
    acol = _alibi_cols()
    full = lambda z: pl.BlockSpec(z.shape, lambda b, i: (0,) * z.ndim)
    nq = 1
    outs = []
    for q_off in range(0, S // tq, nq):
        sk = (q_off + nq) * tq
        tok = lambda n, q_off=q_off: pl.BlockSpec((None, tq, n), lambda b, i: (b, i + q_off, 0))
        seq = lambda n, sk=sk: pl.BlockSpec((None, sk, n), lambda b, i: (b, 0, 0))
        outs.append(pl.pallas_call(
            functools.partial(_dsa_kernel, topk=topk, q_off=q_off, select=sk > topk),
            grid=(B, nq),
            in_specs=[tok(IDX_HEADS * LANES), tok(LANES), tok(N_ATT_HEADS * KV_LORA), seq(LANES), seq(KV_LORA + LANES),
                      pl.BlockSpec((None, KV_LORA + LANES, sk), lambda b, i: (b, 0, 0)),
                      full(wuv_bd), full(tril), full(acol)],
            out_specs=pl.BlockSpec((None, tq, D_ATT), lambda b, i: (b, i, 0)),
            out_shape=jax.ShapeDtypeStruct((B, nq * tq, D_ATT), F32),
            scratch_shapes=[pltpu.VMEM((sk, tq), I32), pltpu.VMEM((sk, tq), F32), pltpu.VMEM((sk, tq), F32),
                            pltpu.VMEM((sk, tq), BF16)],
            compiler_params=_cparams(("arbitrary", "arbitrary")),
            name=f"dsa_attn_k{sk}",
        )(iq, iw, qabs, ik, ca, cat, wuv_bd, tril, acol))
    return jnp.concatenate(outs, axis=1)


def _layernorm_rows(y, g, b):
    mu = jnp.mean(y, axis=-1, keepdims=True)
    d = y - mu
    var = jnp.mean(d * d, axis=-1, keepdims=True)
    return d * lax.rsqrt(var + 1e-5) * g + b


def _mix_kernel(orw_ref, ods_ref, x_ref, gt_ref, sc_ref, sh_ref, wtop_ref, wbot_ref, g_ref, b_ref, wr_ref, br_ref,
                tril_ref, triu_ref, x1_o, u2_o, route_o, tbl_o, cnt_o, carry, *, alpha):
    first = (pl.program_id(0) == 0) & (pl.program_id(1) == 0)
    tm = x_ref.shape[0]

    @pl.when(first)
    def _():
        carry[...] = jnp.zeros_like(carry)

    mix = _bdot(orw_ref[...], wtop_ref[...]) + _bdot(ods_ref[...], wbot_ref[...])
    x1 = _layernorm_rows(alpha * x_ref[...] + (1.0 + gt_ref[...]) * mix, g_ref[...], b_ref[...])
    x1_o[...] = x1
    u2 = x1 * (1.0 + sc_ref[...]) + sh_ref[...]
    u2_o[...] = u2

    lg = _dot3(u2, wr_ref[...]) + br_ref[...]
    lane = lax.broadcasted_iota(I32, (tm, LANES), 1)
    lane_f = lane.astype(F32)
    idxs, vals = [], []
    for _ in range(TOP_K_EXPERTS):
        m = jnp.max(lg, axis=-1, keepdims=True)
        idx = jnp.min(jnp.where(lg == m, lane_f, float(LANES)), axis=-1, keepdims=True).astype(I32)
        idxs.append(idx)
        vals.append(m)
        lg = jnp.where(lane == idx, -jnp.inf, lg)
    es = [jnp.exp(v - vals[0]) for v in vals]
    den = es[0] + es[1] + es[2] + es[3]
    hot = jnp.zeros((tm, LANES), F32)
    for idx in idxs:
        hot = hot + jnp.where(lane == idx, 1.0, 0.0)
    before = jnp.dot(tril_ref[...], hot.astype(BF16), preferred_element_type=F32)
    cnt = jnp.sum(hot, axis=0, keepdims=True)
    n_run = jnp.floor((cnt + (RUN_ALIGN - 1)) * (1.0 / RUN_ALIGN))
    n_stage = jnp.floor((n_run * RUN_ALIGN + (STAGE_ALIGN - 1)) * (1.0 / STAGE_ALIGN))
    off = jnp.dot(jnp.broadcast_to(n_stage, (SUBLANES, LANES)).astype(BF16), triu_ref[...],
                  preferred_element_type=F32)[0:1, :] * STAGE_ALIGN
    where_in_stage = off + before
    route = jnp.zeros((tm, LANES), F32)
    for k in range(TOP_K_EXPERTS):
        spos = jnp.sum(jnp.where(lane == idxs[k], where_in_stage, 0.0), axis=-1, keepdims=True)
        route = jnp.where(lane == k, idxs[k].astype(F32), route)
        route = jnp.where(lane == TOP_K_EXPERTS + k, es[k] / den, route)
        route = jnp.where(lane == 2 * TOP_K_EXPERTS + k, spos, route)
    route_o[...] = route
    sub = lax.broadcasted_iota(I32, (SUBLANES, LANES), 0)
    tbl_o[...] = jnp.where(sub == 0, n_run, jnp.where(sub == 1, off, jnp.where(sub == 2, carry[0:1, :], 0.0)))
    carry[0:1, :] = carry[0:1, :] + n_run * RUN_ALIGN
    cnt_o[...] = carry[...]


def _mix_out(o_rwkv, o_dsa, x, gt1, sc2, sh2, w_out, ln_g, ln_b, w_router, b_router):
    B, S, D = x.shape
    tm = min(TM_PROJ, S)
    alpha = 2.0 ** 0.25
    wtop = w_out[:D_RWKV].astype(BF16)
    wbot = w_out[D_RWKV:].astype(BF16)
    wr_p = jnp.pad(w_router, ((0, 0), (0, LANES - N_EXPERTS)))
    br_p = jnp.pad(b_router.reshape(1, -1), ((0, 0), (0, LANES - N_EXPERTS)), constant_values=NEG_BIG)
    tril = jnp.asarray(np.tril(np.ones((tm, tm)), -1), BF16)
    triu = jnp.asarray(np.triu(np.ones((LANES, LANES)), 1), BF16)
    row = lambda v: v.reshape(1, -1)
    tok = lambda n: pl.BlockSpec((None, tm, n), lambda b, i: (b, i, 0))
    mod = pl.BlockSpec((None, 1, D), lambda b, i: (b, 0, 0))
    full = lambda a: pl.BlockSpec(a.shape, lambda b, i: (0,) * a.ndim)
    consts = [wtop, wbot, row(ln_g), row(ln_b), wr_p, br_p, tril, triu]
    return pl.pallas_call(
        functools.partial(_mix_kernel, alpha=alpha),
        grid=(B, S // tm),
        in_specs=[tok(D_RWKV), tok(D_ATT), tok(D), mod, mod, mod] + [full(a) for a in consts],
        out_specs=[tok(D), tok(D), tok(LANES), pl.BlockSpec((None, None, SUBLANES, LANES), lambda b, i: (b, i, 0, 0)),
                   pl.BlockSpec((SUBLANES, LANES), lambda b, i: (0, 0))],
        out_shape=[jax.ShapeDtypeStruct((B, S, D), F32), jax.ShapeDtypeStruct((B, S, D), F32),
                   jax.ShapeDtypeStruct((B, S, LANES), F32),
                   jax.ShapeDtypeStruct((B, S // tm, SUBLANES, LANES), F32),
                   jax.ShapeDtypeStruct((SUBLANES, LANES), F32)],
        scratch_shapes=[pltpu.VMEM((8, LANES), F32)],
        compiler_params=_cparams(("arbitrary", "arbitrary")),
        name="mix_out",
    )(o_rwkv, o_dsa, x, gt1, sc2, sh2, *consts)


RUN_ALIGN = SUBLANES
STAGE_ALIGN = 16
STAGE_ROWS = 1536
STAGE_CHUNK = 256
DISPATCH_SLOTS = 3


def _run_copies(nrun_ref, tile, copy_of):
    for e in range(N_EXPERTS):
        pieces = (nrun_ref[tile * N_EXPERTS + e] * RUN_ALIGN + STAGE_ALIGN - 1) // STAGE_ALIGN

        def piece(j, carry, e=e):
            copy_of(e, j).start(priority=e % 2)
            return carry

        lax.fori_loop(0, pieces, piece, 0)


def _dispatch_kernel(nrun_ref, off_ref, start_ref, nfull_ref, nodd_ref, nstage_ref, zs_ref, zn_ref, tail_ref,
                     u_ref, route_ref, xs_out, stag, zeros, sem, zsem):
    i = pl.program_id(0)
    n = pl.num_programs(0)
    tm = u_ref.shape[0]
    bm = zeros.shape[0]
    slot = i % DISPATCH_SLOTS

    @pl.when(i == 0)
    def _():
        zeros[...] = jnp.zeros_like(zeros)
        fills = []
        for e in range(N_EXPERTS):
            for b in range(bm.bit_length()):
                rows = RUN_ALIGN << b
                if rows > bm:
                    break
                done = (zn_ref[e] >> (b + 1)) << (b + 1)
                dst = pl.multiple_of(zs_ref[e] + done * RUN_ALIGN, RUN_ALIGN)
                fills.append(((zn_ref[e] >> b) & 1 == 1,
                              pltpu.make_async_copy(zeros.at[pl.ds(0, rows)], xs_out.at[pl.ds(dst, rows)], zsem)))
        for pred, cp in fills:
            pl.when(pred)(cp.start)

        def tail_copy(j):
            return pltpu.make_async_copy(zeros, xs_out.at[pl.ds(pl.multiple_of(j * bm, bm), bm)], zsem)

        lax.fori_loop(tail_ref[0], tail_ref[1], lambda j, c: (tail_copy(j).start(), c)[1], 0)
        for pred, cp in fills:
            pl.when(pred)(cp.wait)
        lax.fori_loop(tail_ref[0], tail_ref[1], lambda j, c: (tail_copy(j).wait(), c)[1], 0)

    route_t = jnp.transpose(route_ref[...])
    spos = [route_t[2 * TOP_K_EXPERTS + k:2 * TOP_K_EXPERTS + k + 1, :].astype(I32) for k in range(TOP_K_EXPERTS)]
    ub = u_ref[...].astype(BF16)
    srow = lax.broadcasted_iota(I32, (STAGE_CHUNK, tm), 0)
    for c in range(STAGE_ROWS // STAGE_CHUNK):
        @pl.when(c * STAGE_CHUNK < nstage_ref[i])
        def _(c=c):
            rows = srow + c * STAGE_CHUNK
            sel = (rows == spos[0]) | (rows == spos[1]) | (rows == spos[2]) | (rows == spos[3])
            stag[slot, c * STAGE_CHUNK:(c + 1) * STAGE_CHUNK, :] = jnp.dot(
                jnp.where(sel, 1.0, 0.0).astype(BF16), ub, preferred_element_type=F32)

    def piece(e, j, rows):
        src = pl.multiple_of(off_ref[i * N_EXPERTS + e] + j * STAGE_ALIGN, RUN_ALIGN)
        dst = pl.multiple_of(start_ref[i * N_EXPERTS + e] + j * STAGE_ALIGN, RUN_ALIGN)
        return pltpu.make_async_copy(stag.at[slot, pl.ds(src, rows)], xs_out.at[pl.ds(dst, rows)], sem.at[slot])

    def drain(tile):
        for cnt_ref, rows in ((nfull_ref, STAGE_ALIGN), (nodd_ref, RUN_ALIGN)):
            def w(j, carry, rows=rows):
                pltpu.make_async_copy(stag.at[0, pl.ds(0, rows)], xs_out.at[pl.ds(0, rows)],
                                      sem.at[tile % DISPATCH_SLOTS]).wait()
                return carry
            lax.fori_loop(0, cnt_ref[tile], w, 0)

    @pl.when(i >= DISPATCH_SLOTS - 1)
    def _():
        drain(i - (DISPATCH_SLOTS - 1))

    per_piece = STAGE_ALIGN // RUN_ALIGN
    for e in range(N_EXPERTS):
        n_units = nrun_ref[i * N_EXPERTS + e]

        def full(j, carry, e=e):
            piece(e, j, STAGE_ALIGN).start(priority=e % 2)
            return carry

        lax.fori_loop(0, n_units // per_piece, full, 0)

        @pl.when(n_units % per_piece == 1)
        def _(e=e, n_units=n_units):
            piece(e, n_units // per_piece, RUN_ALIGN).start(priority=e % 2)

    @pl.when(i == n - 1)
    def _():
        for back in range(DISPATCH_SLOTS - 2, -1, -1):
            @pl.when(i >= back)
            def _(back=back):
                drain(i - back)


def _moe_dispatch(u2, route, tables, n_rows):
    T, D = u2.shape
    tm = TM_ROUTE
    bm = BM_EXPERT
    return pl.pallas_call(
        _dispatch_kernel,
        grid_spec=pltpu.PrefetchScalarGridSpec(
            num_scalar_prefetch=len(tables),
            grid=(T // tm,),
            in_specs=[pl.BlockSpec((tm, D), lambda i, *_: (i, 0)),
                      pl.BlockSpec((tm, LANES), lambda i, *_: (i, 0))],
            out_specs=pl.BlockSpec(memory_space=pl.ANY),
            scratch_shapes=[pltpu.VMEM((DISPATCH_SLOTS, STAGE_ROWS, D), F32), pltpu.VMEM((bm, D), F32),
                            pltpu.SemaphoreType.DMA((DISPATCH_SLOTS,)), pltpu.SemaphoreType.DMA(())],
        ),
        out_shape=jax.ShapeDtypeStruct((n_rows, D), F32),
        compiler_params=_cparams(("arbitrary",)),
        name="moe_dispatch",
    )(*tables, u2, route)


GU_GROUP = 2 * LANES


def _deinterleave_perm():
    p = np.zeros((GU_GROUP, GU_GROUP), np.float32)
    l = np.arange(LANES)
    p[2 * l, l] = 1.0
    p[2 * l + 1, LANES + l] = 1.0
    return jnp.asarray(p, BF16)


def _expert_kernel(be_ref, nb_ref, valid_ref, xs_ref, wgu_hbm, bgu_ref, wd_hbm, bd_ref, perm_ref, ys_ref,
                   wg_buf, wd_buf, wp, wdb, sem):
    i = pl.program_id(0)
    bm = xs_ref.shape[0]
    e = be_ref[i]
    used = i < nb_ref[0]
    new_expert = (i == 0) | (e != be_ref[jnp.maximum(i - 1, 0)])
    n_groups = wp.shape[1] // GU_GROUP
    n_experts = wgu_hbm.shape[0]

    def fetch(ex):
        return (pltpu.make_async_copy(wgu_hbm.at[ex], wg_buf, sem.at[0]),
                pltpu.make_async_copy(wd_hbm.at[ex], wd_buf, sem.at[1]))

    @pl.when(used & new_expert)
    def _():
        @pl.when(i == 0)
        def _():
            for cp in fetch(e):
                cp.start()

        for cp in fetch(e):
            cp.wait()
        for j in range(n_groups):
            sl = slice(j * GU_GROUP, (j + 1) * GU_GROUP)
            wp[:, sl] = jnp.dot(wg_buf[:, sl].astype(BF16), perm_ref[...], preferred_element_type=F32).astype(BF16)
        wdb[...] = wd_buf[...].astype(BF16)

        @pl.when(e + 1 < n_experts)
        def _():
            for cp in fetch(e + 1):
                cp.start()

    def compute(m):
        xb = xs_ref[:m, :].astype(BF16)
        gu = jnp.dot(xb, wp[...], preferred_element_type=F32) + bgu_ref[...]
        hs = []
        for j in range(n_groups):
            gate = jnp.minimum(gu[:, j * GU_GROUP:j * GU_GROUP + LANES], SWIGLU_LIMIT)
            up = jnp.clip(gu[:, j * GU_GROUP + LANES:(j + 1) * GU_GROUP], -SWIGLU_LIMIT, SWIGLU_LIMIT)
            hs.append(((up + 1.0) * (gate * _sigmoid(gate * SWIGLU_ALPHA))).astype(BF16))
        h = jnp.concatenate(hs, axis=-1)
        ys_ref[:m, :] = jnp.dot(h, wdb[...], preferred_element_type=F32) + bd_ref[...]

    half = bm // 2

    @pl.when(used & (valid_ref[i] > half))
    def _():
        compute(bm)

    @pl.when(used & (valid_ref[i] <= half))
    def _():
        compute(half)
        ys_ref[half:, :] = jnp.zeros((bm - half, ys_ref.shape[1]), F32)

    @pl.when(jnp.logical_not(used))
    def _():
        ys_ref[...] = jnp.zeros_like(ys_ref)


def _moe_experts(xs, block_e, n_used, valid, w_gu, b_gu_p, w_dn, b_dn):
    n_rows, D = xs.shape
    E, _, F2 = w_gu.shape
    bm = BM_EXPERT
    n_blocks = n_rows // bm
    perm = _deinterleave_perm()
    wspec = lambda shp: pl.BlockSpec((None,) + shp, lambda i, be, nb, va: (be[i], 0, 0))
    hbm = pl.BlockSpec(memory_space=pl.ANY)
    return pl.pallas_call(
        _expert_kernel,
        grid_spec=pltpu.PrefetchScalarGridSpec(
            num_scalar_prefetch=3,
            grid=(n_blocks,),
            in_specs=[pl.BlockSpec((bm, D), lambda i, be, nb, va: (jnp.minimum(i, nb[0] - 1), 0)),
                      hbm, wspec((1, F2)), hbm, wspec((1, D)),
                      pl.BlockSpec(perm.shape, lambda i, be, nb, va: (0, 0))],
            out_specs=pl.BlockSpec((bm, D), lambda i, be, nb, va: (i, 0)),
            scratch_shapes=[pltpu.VMEM((D, F2), F32), pltpu.VMEM((F2 // 2, D), F32),
                            pltpu.VMEM((D, F2), BF16), pltpu.VMEM((F2 // 2, D), BF16),
                            pltpu.SemaphoreType.DMA((2,))],
        ),
        out_shape=jax.ShapeDtypeStruct((n_rows, D), F32),
        compiler_params=pltpu.CompilerParams(dimension_semantics=("arbitrary",), vmem_limit_bytes=VMEM_LIMIT_EXPERTS),
        name="moe_experts",
    )(block_e, n_used, valid, xs, w_gu, b_gu_p, w_dn, b_dn, perm)


def _combine_kernel(nrun_ref, off_ref, start_ref, npiece_ref, nstage_ref, ys_ref, x1_ref, route_ref, gt_ref, g_ref,
                    b_ref, o_ref, stag, sem, *, alpha):
    i = pl.program_id(0)
    n = pl.num_programs(0)
    tm = x1_ref.shape[0]
    slot = i % 2

    def gather(tile, s):
        def copy_of(e, j):
            src = pl.multiple_of(start_ref[tile * N_EXPERTS + e] + j * STAGE_ALIGN, RUN_ALIGN)
            dst = pl.multiple_of(off_ref[tile * N_EXPERTS + e] + j * STAGE_ALIGN, STAGE_ALIGN)
            return pltpu.make_async_copy(ys_ref.at[pl.ds(src, STAGE_ALIGN)], stag.at[s, pl.ds(dst, STAGE_ALIGN)],
                                         sem.at[s])
        _run_copies(nrun_ref, tile, copy_of)

    @pl.when(i == 0)
    def _():
        stag[...] = jnp.zeros_like(stag)
        gather(0, 0)

    @pl.when(i + 1 < n)
    def _():
        gather(i + 1, 1 - slot)

    def w(j, carry):
        pltpu.make_async_copy(ys_ref.at[pl.ds(0, STAGE_ALIGN)], stag.at[slot, pl.ds(0, STAGE_ALIGN)],
                              sem.at[slot]).wait()
        return carry
    lax.fori_loop(0, npiece_ref[i], w, 0)

    route = route_ref[...]
    wide = lambda col: jnp.broadcast_to(col, (tm, STAGE_CHUNK))
    spos = [wide(route[:, 2 * TOP_K_EXPERTS + k:2 * TOP_K_EXPERTS + k + 1].astype(I32)) for k in range(TOP_K_EXPERTS)]
    gate = [wide(route[:, TOP_K_EXPERTS + k:TOP_K_EXPERTS + k + 1]) for k in range(TOP_K_EXPERTS)]
    scol = lax.broadcasted_iota(I32, (tm, STAGE_CHUNK), 1)
    acc_ref = o_ref
    acc_ref[...] = jnp.zeros_like(acc_ref)
    for c in range(STAGE_ROWS // STAGE_CHUNK):
        @pl.when(c * STAGE_CHUNK < nstage_ref[i])
        def _(c=c):
            cols = scol + c * STAGE_CHUNK
            wgt = jnp.zeros((tm, STAGE_CHUNK), F32)
            for k in range(TOP_K_EXPERTS):
                wgt = wgt + jnp.where(cols == spos[k], gate[k], 0.0)
            acc_ref[...] += jnp.dot(wgt.astype(BF16), stag[slot, c * STAGE_CHUNK:(c + 1) * STAGE_CHUNK, :].astype(BF16),
                                    preferred_element_type=F32)
    o_ref[...] = _layernorm_rows(alpha * x1_ref[...] + (1.0 + gt_ref[...]) * acc_ref[...], g_ref[...], b_ref[...])


def _moe_combine(ys, tables, x1, route, gt2, ln_g, ln_b, tiles_per_batch):
    T, D = x1.shape
    tm = TM_ROUTE
    row = lambda v: v.reshape(1, -1)
    return pl.pallas_call(
        functools.partial(_combine_kernel, alpha=2.0 ** 0.25),
        grid_spec=pltpu.PrefetchScalarGridSpec(
            num_scalar_prefetch=len(tables),
            grid=(T // tm,),
            in_specs=[pl.BlockSpec(memory_space=pl.ANY),
                      pl.BlockSpec((tm, D), lambda i, *_: (i, 0)),
                      pl.BlockSpec((tm, LANES), lambda i, *_: (i, 0)),
                      pl.BlockSpec((None, 1, D), lambda i, *_: (i // tiles_per_batch, 0, 0)),
                      pl.BlockSpec((1, D), lambda i, *_: (0, 0)),
                      pl.BlockSpec((1, D), lambda i, *_: (0, 0))],
            out_specs=pl.BlockSpec((tm, D), lambda i, *_: (i, 0)),
            scratch_shapes=[pltpu.VMEM((2, STAGE_ROWS, D), F32), pltpu.SemaphoreType.DMA((2,))],
        ),
        out_shape=jax.ShapeDtypeStruct((T, D), F32),
        compiler_params=_cparams(("arbitrary",)),
        name="moe_combine",
    )(*tables, ys, x1, route, gt2, row(ln_g), row(ln_b))


def _moe_and_norm(x1, u2, route, tbl, totals, gt2, w_gu, b_gu, w_dn, b_dn, ln_g, ln_b):
    B, S, D = x1.shape
    T = B * S
    bm = BM_EXPERT
    tm = TM_ROUTE
    assert T % tm == 0 and TM_PROJ == tm and STAGE_ROWS >= tm * TOP_K_EXPERTS + N_EXPERTS * (STAGE_ALIGN - 1)
    n_tiles = T // tm
    max_rows = T * TOP_K_EXPERTS + n_tiles * N_EXPERTS * (RUN_ALIGN - 1) + N_EXPERTS * STAGE_ALIGN
    n_blocks = -(-max_rows // bm) + N_EXPERTS
    tot = totals[0, :N_EXPERTS].astype(I32)
    padded = (tot + STAGE_ALIGN + bm - 1) // bm * bm
    pad_ends = jnp.cumsum(padded)
    pad_starts = pad_ends - padded
    t3 = tbl.reshape(n_tiles, SUBLANES, LANES)[:, :, :N_EXPERTS].astype(I32)
    nrun, off, base = t3[:, 0, :], t3[:, 1, :], t3[:, 2, :]
    start = pad_starts[None, :] + base
    pieces = (nrun * RUN_ALIGN + STAGE_ALIGN - 1) // STAGE_ALIGN
    npiece = jnp.sum(pieces, axis=1)
    nstage = jnp.sum(pieces, axis=1) * STAGE_ALIGN
    flat = lambda z: z.reshape(-1).astype(I32)
    run_tables = [flat(nrun), flat(off), flat(start), flat(npiece), flat(nstage)]
    assert STAGE_ALIGN == 2 * RUN_ALIGN
    nfull = jnp.sum(nrun // 2, axis=1)
    nodd = jnp.sum(nrun % 2, axis=1)
    disp_tables = [flat(nrun), flat(off), flat(start), flat(nfull), flat(nodd), flat(nstage)]
    zs = pad_starts + tot
    zn = (pad_ends - zs) // RUN_ALIGN
    n_used = (pad_ends[-1:] // bm).astype(I32)
    tail = jnp.concatenate([n_used, jnp.full((1,), n_blocks, I32)])
    blk_row = jnp.arange(n_blocks, dtype=I32) * bm
    block_e = jnp.minimum(jnp.sum((blk_row[:, None] >= pad_ends[None, :]).astype(I32), axis=1), N_EXPERTS - 1)
    E, F2 = b_gu.shape
    b_gu_p = b_gu.reshape(E, F2 // GU_GROUP, LANES, 2).transpose(0, 1, 3, 2).reshape(E, 1, F2)
    route2 = route.reshape(T, LANES)
    xs = _moe_dispatch(u2.reshape(T, D), route2, disp_tables + [flat(zs), flat(zn), tail], n_blocks * bm)
    valid = jnp.clip(zs[block_e] - blk_row, 0, bm)
    ys = _moe_experts(xs, block_e, n_used, valid, w_gu, b_gu_p, w_dn, b_dn[:, None, :])
    out = _moe_combine(ys, run_tables, x1.reshape(T, D), route2, gt2, ln_g, ln_b, S // tm)
    return out.reshape(B, S, D)


def kernel(x, c, w_ada, b_ada, w_in, shift_mu, rwkv_w0, rwkv_w2, rwkv_a0, rwkv_a2, rwkv_g2, rwkv_k_k, rwkv_k_a, rwkv_r_k, rwkv_ln_w, rwkv_ln_b, mla_q_norm, mla_w_q_up, mla_kv_norm, mla_w_uk, mla_w_uv, idx_w_q, idx_ln_g, idx_ln_b, w_out, ln1_g, ln1_b, w_router, b_router, w_gu, b_gu, w_dn, b_dn, ln2_g, ln2_b):
    depth = w_ada.shape[0]
    assert depth == 1, "DeepNorm constants below are for a single layer"
    l = 0
    mod = _ada_mod(c, w_ada[l], b_ada[l])
    sh1, sc1, gt1, sh2, sc2, gt2 = [m[:, None, :] for m in jnp.split(mod, 6, axis=-1)]
    r, lw, k, v, kk, a, g, qabs, iq, ik, iw, cl, clt = _in_proj(
        x, sc1, sh1, w_in[l], shift_mu[l], rwkv_w0[l], rwkv_w2[l], rwkv_a0[l], rwkv_a2[l], rwkv_g2[l], rwkv_k_k[l],
        rwkv_k_a[l], mla_q_norm[l], mla_w_q_up[l], mla_w_uk[l], mla_kv_norm[l], idx_w_q[l], idx_ln_g[l], idx_ln_b[l])
    o_rwkv = _rwkv_scan(r, lw, k, v, kk, a, g, rwkv_r_k[l], rwkv_ln_w[l], rwkv_ln_b[l])
    o_dsa = _dsa_attn(iq, iw, qabs, ik, cl, clt, mla_w_uv[l])
    x1, u2, route, tbl, totals = _mix_out(o_rwkv, o_dsa, x, gt1, sc2, sh2, w_out[l], ln1_g[l], ln1_b[l],
                                          w_router[l], b_router[l])
    return _moe_and_norm(x1, u2, route, tbl, totals, gt2, w_gu[l], b_gu[l], w_dn[l], b_dn[l], ln2_g[l], ln2_b[l])
```

```python
import functools
import math

import jax
import jax.numpy as jnp
import numpy as np
from jax import lax
from jax.experimental import pallas as pl
from jax.experimental.pallas import tpu as pltpu

F32 = jnp.float32
BF16 = jnp.bfloat16
I32 = jnp.int32

RWKV_HEAD = 64
N_RWKV_HEADS = 8
D_RWKV = RWKV_HEAD * N_RWKV_HEADS
RANK_DECAY = 64
RANK_A = 64
RANK_GATE = 128
RWKV_GN_EPS = 64e-5
ATT_HEAD = 64
N_ATT_HEADS = 8
D_ATT = ATT_HEAD * N_ATT_HEADS
Q_LORA = 256
KV_LORA = 128
IDX_HEADS = 8
IDX_DIM = 64
TOPK_MAX = 256
N_EXPERTS = 32
TOP_K_EXPERTS = 4
SWIGLU_LIMIT = 7.0
SWIGLU_ALPHA = 1.702
NEG_BIG = -1e30
LOG2E = 1.4426950408889634
INT_MIN = -(2 ** 31)

LANES = 128
SUBLANES = 8
VMEM_LIMIT = 48 * 1024 * 1024
VMEM_LIMIT_EXPERTS = 58 * 1024 * 1024

TM_IN = 512
TM_PROJ = 256
L_CHUNK = 64
CHUNKS_PER_STEP = 4
TQ = 256
KEY_CHUNK = 128
SUM_ROWS = 64
TM_ROUTE = 256
BM_EXPERT = 1024

_SEG = (("r", 512, 512), ("k", 512, 512), ("v", 512, 512), ("w", 64, 128), ("a", 64, 128), ("g", 128, 128),
        ("q", 256, 256), ("kv", 128, 128), ("ik", 64, 128), ("iw", 8, 128))
N_SHIFT_P = 512 * 3 + 128 * 3
N_IN_P = sum(s[2] for s in _SEG)


def _cparams(sem):
    return pltpu.CompilerParams(dimension_semantics=sem, vmem_limit_bytes=VMEM_LIMIT)


def _bdot(a, b):
    return jnp.dot(a.astype(BF16), b.astype(BF16), preferred_element_type=F32)


def _bdot_nt(a, b):
    return lax.dot_general(a.astype(BF16), b.astype(BF16), (((1,), (1,)), ((), ())), preferred_element_type=F32)


def _bdot_tn(a, b):
    return lax.dot_general(a.astype(BF16), b.astype(BF16), (((0,), (0,)), ((), ())), preferred_element_type=F32)


def _split2(a):
    hi = a.astype(BF16)
    lo = (a - hi.astype(F32)).astype(BF16)
    return hi, lo


def _split3(a):
    hi = a.astype(BF16)
    r1 = a - hi.astype(F32)
    mid = r1.astype(BF16)
    lo = (r1 - mid.astype(F32)).astype(BF16)
    return hi, mid, lo


def _dot3(a, b, dims=(((1,), (0,)), ((), ()))):
    ah, al = _split2(a)
    bh, bl = _split2(b)
    d = functools.partial(lax.dot_general, dimension_numbers=dims, preferred_element_type=F32)
    return d(ah, bh) + (d(ah, bl) + d(al, bh))


def _dot_exact_rhs(a, b_exact, nsplit=3):
    parts = _split3(a) if nsplit == 3 else _split2(a)
    acc = None
    for p in parts[::-1]:
        t = jnp.dot(p, b_exact, preferred_element_type=F32)
        acc = t if acc is None else acc + t
    return acc


def _dot_exact_lhs(a_exact, b, nsplit=3):
    parts = _split3(b) if nsplit == 3 else _split2(b)
    acc = None
    for p in parts[::-1]:
        t = jnp.dot(a_exact, p, preferred_element_type=F32)
        acc = t if acc is None else acc + t
    return acc


def _sigmoid(x):
    return 1.0 / (1.0 + jnp.exp(-x))


def _softplus(x):
    return jnp.maximum(x, 0.0) + jnp.log(1.0 + jnp.exp(-jnp.abs(x)))


def _ada_kernel(c_ref, w_ref, b_ref, o_ref):
    c = c_ref[...]
    o_ref[...] = _dot3(c * _sigmoid(c), w_ref[...]) + b_ref[...]


def _ada_mod(c, w_ada, b_ada):
    B, D = c.shape
    N = w_ada.shape[1]
    tn = 1024
    return pl.pallas_call(
        _ada_kernel,
        grid=(N // tn,),
        in_specs=[pl.BlockSpec((B, D), lambda j: (0, 0)),
                  pl.BlockSpec((D, tn), lambda j: (0, j)),
                  pl.BlockSpec((1, tn), lambda j: (0, j))],
        out_specs=pl.BlockSpec((B, tn), lambda j: (0, j)),
        out_shape=jax.ShapeDtypeStruct((B, N), F32),
        compiler_params=_cparams(("arbitrary",)),
        name="ada_mod",
    )(c, w_ada, b_ada.reshape(1, N))


def _in_proj_kernel(x_ref, sc_ref, sh_ref, win_ref, mu_ref, w0_ref, w2_ref, a0_ref, a2_ref, g2_ref, kk_ref, ka_ref,
                    ones_ref, qn_ref, wq_ref, wuk_ref, kvn_ref, wiq_ref, ig_ref, ib_ref,
                    r_o, lw_o, k_o, v_o, kkn_o, a_o, g_o, qabs_o, iq_o, ik_o, iw_o, cl_o, clt_o, carry):
    i = pl.program_id(1)
    tm = x_ref.shape[0]

    @pl.when(i == 0)
    def _():
        carry[...] = jnp.zeros_like(carry)

    u = x_ref[...] * (1.0 + sc_ref[...]) + sh_ref[...]
    p = _bdot(u, win_ref[...])
    ps = p[:, :N_SHIFT_P]
    rows = lax.broadcasted_iota(I32, (tm, 1), 0)
    prev = jnp.where(rows == 0, carry[0:1, :], pltpu.roll(ps, 1, 0))
    carry[0:1, :] = ps[tm - 1:tm, :]
    ps = ps + mu_ref[...] * (prev - ps)

    pr, pk, pv = ps[:, 0:512], ps[:, 512:1024], ps[:, 1024:1536]
    pw, pa, pg = ps[:, 1536:1664], ps[:, 1664:1792], ps[:, 1792:1920]
    w_log = -_softplus(-(w0_ref[...] + _dot3(jnp.tanh(pw), w2_ref[...]))) - 0.5
    lw_o[...] = -jnp.exp(w_log)
    a = _sigmoid(a0_ref[...] + _dot3(pa, a2_ref[...]))
    g_o[...] = _dot3(_sigmoid(pg), g2_ref[...])
    kk = pk * kk_ref[...]
    ssq = _dot_exact_rhs(kk * kk, ones_ref[...], nsplit=2)
    kkn_o[...] = kk / jnp.maximum(jnp.sqrt(ssq), 1e-12)
    k_o[...] = pk * (1.0 + (a - 1.0) * ka_ref[...])
    r_o[...] = pr
    v_o[...] = pv
    a_o[...] = a

    pq, pkv = p[:, 1920:2176], p[:, 2176:2304]
    pik, piw = p[:, 2304:2432], p[:, 2432:2560]
    q_lat = pq * lax.rsqrt(jnp.mean(pq * pq, axis=-1, keepdims=True) + 1e-6) * qn_ref[...]
    q = _bdot(q_lat, wq_ref[...])
    qabs_o[...] = (_bdot(q, wuk_ref[...]) * (ATT_HEAD ** -0.5 * LOG2E)).astype(BF16)
    c_lat = pkv * lax.rsqrt(jnp.mean(pkv * pkv, axis=-1, keepdims=True) + 1e-6) * kvn_ref[...]
    spos = i * tm + rows
    s_hi = (spos >> 7).astype(F32)
    s_lo = (spos & (LANES - 1)).astype(F32)
    lane_t = lax.broadcasted_iota(I32, (1, LANES), 1)
    extra = jnp.where(lane_t == 0, 1.0, jnp.where((lane_t == 1) | (lane_t == 2), s_hi,
                                                  jnp.where((lane_t == 3) | (lane_t == 4), s_lo, 0.0)))
    c_aug = jnp.concatenate([c_lat, extra], axis=-1)
    cl_o[...] = c_aug.astype(BF16)
    clt_o[...] = jnp.transpose(c_aug).astype(BF16)
    iq_o[...] = (_bdot(q_lat, wiq_ref[...]) * (IDX_DIM ** -0.5)).astype(BF16)
    lane = lax.broadcasted_iota(I32, (1, LANES), 1)
    valid = lane < IDX_DIM
    mu = jnp.sum(pik, axis=-1, keepdims=True) * (1.0 / IDX_DIM)
    dlt = jnp.where(valid, pik - mu, 0.0)
    var = jnp.sum(dlt * dlt, axis=-1, keepdims=True) * (1.0 / IDX_DIM)
    ik_o[...] = jnp.where(valid, dlt * lax.rsqrt(var + 1e-5) * ig_ref[...] + ib_ref[...], 0.0).astype(BF16)
    iw_o[...] = piw * (IDX_HEADS ** -0.5)


def _pad_cols(w, widths):
    parts, o = [], 0
    for true, padded in widths:
        seg = w[..., o:o + true]
        if padded > true:
            seg = jnp.pad(seg, [(0, 0)] * (w.ndim - 1) + [(0, padded - true)])
        parts.append(seg)
        o += true
    return jnp.concatenate(parts, axis=-1)


def _pad_rows(w, rows):
    return jnp.pad(w, ((0, rows - w.shape[0]), (0, 0)))


def _block_diag(blocks):
    H, a, b = blocks.shape
    eye = jnp.eye(H, dtype=blocks.dtype)
    return (eye[:, None, :, None] * blocks[:, :, None, :]).reshape(H * a, H * b)


def _head_ones(n, head):
    idx = np.arange(n) // head
    return jnp.asarray(idx[:, None] == idx[None, :], BF16)


def _in_proj(x, sc1, sh1, w_in, shift_mu, w0, w2, a0, a2, g2, k_k, k_a, q_norm, w_q_up, w_uk, kv_norm, idx_w_q,
             idx_ln_g, idx_ln_b):
    B, S, D = x.shape
    tm = min(TM_IN, S)
    widths = tuple((s[1], s[2]) for s in _SEG)
    win_p = _pad_cols(w_in, widths).astype(BF16)
    mu_p = _pad_cols(shift_mu.reshape(1, -1), widths[:6])
    w2_p = _pad_rows(w2, LANES)
    a2_p = _pad_rows(a2, LANES)
    wuk_bd = _block_diag(w_uk).astype(BF16)
    wiq_p = _pad_cols(idx_w_q, ((IDX_DIM, LANES),) * IDX_HEADS).astype(BF16)
    ig_p = _pad_cols(idx_ln_g.reshape(1, -1), ((IDX_DIM, LANES),))
    ib_p = _pad_cols(idx_ln_b.reshape(1, -1), ((IDX_DIM, LANES),))
    row = lambda v: v.reshape(1, -1)
    tok = lambda n: pl.BlockSpec((None, tm, n), lambda b, i: (b, i, 0))
    mod = pl.BlockSpec((None, 1, D), lambda b, i: (b, 0, 0))
    full = lambda a: pl.BlockSpec(a.shape, lambda b, i: (0,) * a.ndim)
    consts = [win_p, mu_p, row(w0), w2_p, row(a0), a2_p, g2, row(k_k), row(k_a), _head_ones(D_RWKV, RWKV_HEAD),
              row(q_norm), w_q_up.astype(BF16), wuk_bd, row(kv_norm), wiq_p, ig_p, ib_p]
    outs = [(D_RWKV, F32)] * 7 + [(N_ATT_HEADS * KV_LORA, BF16), (IDX_HEADS * LANES, BF16), (LANES, BF16),
                                  (LANES, F32), (KV_LORA + LANES, BF16)]
    return pl.pallas_call(
        _in_proj_kernel,
        grid=(B, S // tm),
        in_specs=[tok(D), mod, mod] + [full(a) for a in consts],
        out_specs=[tok(n) for n, _ in outs] + [pl.BlockSpec((None, KV_LORA + LANES, tm), lambda b, i: (b, 0, i))],
        out_shape=[jax.ShapeDtypeStruct((B, S, n), dt) for n, dt in outs]
        + [jax.ShapeDtypeStruct((B, KV_LORA + LANES, S), BF16)],
        scratch_shapes=[pltpu.VMEM((8, N_SHIFT_P), F32)],
        compiler_params=_cparams(("arbitrary", "arbitrary")),
        name="in_proj",
    )(x, sc1, sh1, *consts)


def _rwkv_kernel(r_ref, lw_ref, k_ref, v_ref, kk_ref, a_ref, g_ref, rk_ref, lnw_ref, lnb_ref, tri_ref, ones_ref,
                 o_ref, state, *, L):
    c = pl.program_id(1)
    nc = r_ref.shape[0] // L

    @pl.when(c == 0)
    def _():
        state[...] = jnp.zeros_like(state)

    r, lw, k, v, kk, a = r_ref[...], lw_ref[...], k_ref[...], v_ref[...], kk_ref[...], a_ref[...]
    cum = _dot_exact_lhs(tri_ref[...], lw)
    rows = [slice(ci * L, (ci + 1) * L) for ci in range(nc)]
    last = [cum[(ci + 1) * L - 1:(ci + 1) * L, :] for ci in range(nc)]
    cum_last = jnp.concatenate([jnp.broadcast_to(z, (L, z.shape[1])) for z in last], axis=0)
    w_incl = jnp.exp(cum)
    w_inv = jnp.exp(-cum)
    w_rel = jnp.exp(cum_last - cum)
    w_last = [jnp.exp(z) for z in last]
    bvec = kk * a
    at = -kk * jnp.exp(cum - lw)
    rt = r * w_incl
    bt = bvec * w_inv
    kt = k * w_inv
    bh = bvec * w_rel
    kh = k * w_rel
    ti = lax.broadcasted_iota(I32, (L, L), 0)
    tj = lax.broadcasted_iota(I32, (L, L), 1)
    strict = tj < ti
    incl = tj <= ti
    eye = lax.broadcasted_iota(I32, (RWKV_HEAD, RWKV_HEAD), 0) == lax.broadcasted_iota(I32, (RWKV_HEAD, RWKV_HEAD), 1)
    NT = (((1,), (1,)), ((), ()))
    TN = (((0,), (0,)), ((), ()))
    heads = range(N_RWKV_HEADS)
    sls = [slice(h * RWKV_HEAD, (h + 1) * RWKV_HEAD) for h in heads]
    units = [(ci, h) for ci in range(nc) for h in heads]
    idx = range(len(units))
    mm = lambda x, y, dims=(((1,), (0,)), ((), ())): lax.dot_general(
        x.astype(BF16), y.astype(BF16), dims, preferred_element_type=F32)
    at_b, rt_b, bt_b, kt_b, bh_b, kh_b, v_b = [z.astype(BF16) for z in (at, rt, bt, kt, bh, kh, v)]
    cut = lambda z, u: z[rows[u[0]], sls[u[1]]]
    vh = [cut(v_b, u) for u in units]
    ath = [cut(at_b, u) for u in units]
    ar = [jnp.concatenate([cut(at_b, u), cut(rt_b, u)], axis=0) for u in units]
    g_b = [mm(ar[i], cut(bt_b, units[i]), NT) for i in idx]
    g_k = [mm(ar[i], cut(kt_b, units[i]), NT) for i in idx]
    n_ab = [jnp.where(strict, g_b[i][:L], 0.0) for i in idx]
    a_ak = [jnp.where(strict, g_k[i][:L], 0.0) for i in idx]
    a_rb = [jnp.where(incl, g_b[i][L:], 0.0).astype(BF16) for i in idx]
    a_rk = [jnp.where(incl, g_k[i][L:], 0.0) for i in idx]
    akv = [mm(a_ak[i], vh[i]) for i in idx]
    eye_l = jnp.where(ti == tj, 1.0, 0.0)
    tinv = [eye_l + n_ab[i] for i in idx]
    pw = n_ab
    for _ in range(int(math.log2(L)) - 1):
        pw = [mm(pw[i], pw[i]) for i in idx]
        tinv = [tinv[i] + mm(pw[i], tinv[i]) for i in idx]
    tinv = [t.astype(BF16) for t in tinv]
    a_t = [mm(tinv[i], ath[i]).astype(BF16) for i in idx]
    y = [mm(tinv[i], akv[i]).astype(BF16) for i in idx]
    m_c = [jnp.where(eye, w_last[units[i][0]][:, sls[units[i][1]]], 0.0) + mm(a_t[i], cut(bh_b, units[i]), TN)
           for i in idx]
    c_c = [mm(y[i], cut(bh_b, units[i]), TN) + mm(vh[i], cut(kh_b, units[i]), TN) for i in idx]
    q_c = [cut(rt, units[i]) + mm(a_rb[i], a_t[i]) for i in idx]
    o_loc = [mm(a_rb[i], y[i]) + mm(a_rk[i], vh[i]) for i in idx]
    s = [state[h] for h in heads]
    for ci in range(nc):
        for h in heads:
            i = ci * N_RWKV_HEADS + h
            o = o_loc[i] + mm(q_c[i], s[h], NT)
            s[h] = mm(s[h], m_c[i]) + c_c[i]
            mu = jnp.mean(o, axis=-1, keepdims=True)
            d = o - mu
            var = jnp.mean(d * d, axis=-1, keepdims=True)
            o_ref[rows[ci], sls[h]] = d * lax.rsqrt(var + RWKV_GN_EPS)
    for h in heads:
        state[h] = s[h]
    bonus = _dot_exact_rhs(r * k * rk_ref[...], ones_ref[...], nsplit=3) * v
    o_ref[...] = (o_ref[...] * lnw_ref[...] + lnb_ref[...] + bonus) * g_ref[...]


def _rwkv_scan(r, lw, k, v, kk, a, g, r_k, ln_w, ln_b):
    B, S, DR = r.shape
    L = min(L_CHUNK, S)
    lb = min(L * CHUNKS_PER_STEP, S)
    tri = jnp.asarray(np.kron(np.eye(lb // L), np.tril(np.ones((L, L)))), BF16)
    row = lambda z: z.reshape(1, -1)
    tok = pl.BlockSpec((None, lb, DR), lambda b, c: (b, c, 0))
    full = lambda z: pl.BlockSpec(z.shape, lambda b, c: (0,) * z.ndim)
    consts = [row(r_k), row(ln_w), row(ln_b), tri, _head_ones(DR, RWKV_HEAD)]
    return pl.pallas_call(
        functools.partial(_rwkv_kernel, L=L),
        grid=(B, S // lb),
        in_specs=[tok] * 7 + [full(z) for z in consts],
        out_specs=tok,
        out_shape=jax.ShapeDtypeStruct((B, S, DR), F32),
        scratch_shapes=[pltpu.VMEM((N_RWKV_HEADS, RWKV_HEAD, RWKV_HEAD), F32)],
        compiler_params=_cparams(("arbitrary", "arbitrary")),
        name="rwkv_scan",
    )(r, lw, k, v, kk, a, g, *consts)


def _alibi_cols():
    slope = np.asarray([2.0 ** (-8.0 * (h + 1) / N_ATT_HEADS) * LOG2E for h in range(N_ATT_HEADS)], np.float32)
    c_hi = slope.astype(BF16).astype(np.float32)
    c_lo = (slope - c_hi).astype(BF16).astype(np.float32)
    t = np.zeros((N_ATT_HEADS, LANES), np.float32)
    t[:, 1], t[:, 2], t[:, 3], t[:, 4] = LANES * c_hi, LANES * c_lo, c_hi, c_lo
    return jnp.asarray(t.astype(BF16))


def _dsa_kernel(iq_ref, iw_ref, qa_ref, ik_ref, ca_ref, cat_ref, wuv_ref, tril_ref, acol_ref, o_ref, key_ref,
                bias_ref, lg_ref, p_ref, *, topk, q_off, select):
    qi = pl.program_id(1) + q_off
    tq = qa_ref.shape[0]
    sk = ca_ref.shape[0]
    kc = min(KEY_CHUNK, sk)
    chunks = [slice(c * kc, (c + 1) * kc) for c in range(sk // kc)]
    tpos = qi * tq + lax.broadcasted_iota(I32, (1, tq), 1)
    srow = lax.broadcasted_iota(I32, (kc, tq), 0)
    NT = (((1,), (1,)), ((), ()))

    if not select:
        for c, cs in enumerate(chunks):
            bias_ref[cs, :] = jnp.where(srow + c * kc <= tpos, 0.0, NEG_BIG)
    else:
        iw_t = jnp.transpose(iw_ref[...])
        for c, cs in enumerate(chunks):
            ikc = ik_ref[cs, :]
            score = jnp.zeros((kc, tq), F32)
            for h in range(IDX_HEADS):
                s = lax.dot_general(ikc, iq_ref[:, h * LANES:(h + 1) * LANES], NT, preferred_element_type=F32)
                score = score + iw_t[h:h + 1, :] * jnp.maximum(s, 0.0)
            bits = pltpu.bitcast(score + 0.0, I32)
            key = bits ^ ((bits >> 31) & 0x7FFFFFFF)
            key_ref[cs, :] = jnp.where(srow + c * kc <= tpos, key, INT_MIN)
        kcount = jnp.minimum(topk, tpos + 1).astype(F32)

        def sum_keys(x):
            part = jnp.sum(x.reshape(x.shape[0] // SUM_ROWS, SUM_ROWS, x.shape[1]), axis=0)
            return jnp.sum(part, axis=0, keepdims=True)

        def count_ge(cand):
            return sum_keys(jnp.where(key_ref[...] >= cand, 1.0, 0.0))

        thr = jnp.where(count_ge(jnp.zeros((1, tq), I32)) >= kcount, 0, INT_MIN).astype(I32)

        def bit_step(i, thr):
            cand = thr | (1 << (30 - i))
            return jnp.where(count_ge(cand) >= kcount, cand, thr)

        thr = lax.fori_loop(0, 31, bit_step, thr)

        need = kcount - sum_keys(jnp.where(key_ref[...] > thr, 1.0, 0.0))
        carry = jnp.zeros((1, tq), F32)
        for j in range(sk // LANES):
            sl = slice(j * LANES, (j + 1) * LANES)
            key = key_ref[sl, :]
            eq = key == thr
            e = jnp.where(eq, 1.0, 0.0)
            before = jnp.dot(tril_ref[...], e.astype(BF16), preferred_element_type=F32) + carry
            sel = (key > thr) | (eq & (before < need))
            bias_ref[sl, :] = jnp.where(sel, 0.0, NEG_BIG)
            carry = carry + jnp.sum(e, axis=0, keepdims=True)

    outs = []
    for h in range(N_ATT_HEADS):
        q_aug = jnp.concatenate([qa_ref[:, h * KV_LORA:(h + 1) * KV_LORA],
                                 jnp.broadcast_to(acol_ref[h:h + 1, :], (tq, LANES))], axis=-1)
        lg_all = lax.dot_general(ca_ref[...], q_aug, NT, preferred_element_type=F32)
        m = jnp.full((1, tq), -jnp.inf, F32)
        for cs in chunks:
            lg = lg_all[cs, :] + bias_ref[cs, :]
            lg_ref[cs, :] = lg
            m = jnp.maximum(m, jnp.max(lg, axis=0, keepdims=True))
        for cs in chunks:
            p_ref[cs, :] = jnp.exp2(lg_ref[cs, :] - m).astype(BF16)
        pv = jnp.dot(cat_ref[...], p_ref[...], preferred_element_type=F32)
        outs.append((pv[:KV_LORA, :] / pv[KV_LORA:KV_LORA + 1, :]).astype(BF16))
    o_lat_t = jnp.concatenate(outs, axis=0)
    o_ref[...] = lax.dot_general(o_lat_t, wuv_ref[...], (((0,), (0,)), ((), ())), preferred_element_type=F32)


def _dsa_attn(iq, iw, qabs, ik, ca, cat, w_uv):
    B, S, _ = iq.shape
    tq = min(TQ, S)
    topk = min(TOPK_MAX, S // 4)
    wuv_bd = _block_diag(w_uv).astype(BF16)
    tril = jnp.asarray(np.tril(np.ones((LANES, LANES)), -1), BF16)
    acol = _alibi_cols()
    full = lambda z: pl.BlockSpec(z.shape, lambda b, i: (0,) * z.ndim)
    nq = 1
    outs = []
    for q_off in range(0, S // tq, nq):
        sk = (q_off + nq) * tq
        tok = lambda n, q_off=q_off: pl.BlockSpec((None, tq, n), lambda b, i: (b, i + q_off, 0))
        seq = lambda n, sk=sk: pl.BlockSpec((None, sk, n), lambda b, i: (b, 0, 0))
        outs.append(pl.pallas_call(
            functools.partial(_dsa_kernel, topk=topk, q_off=q_off, select=sk > topk),
            grid=(B, nq),
            in_specs=[tok(IDX_HEADS * LANES), tok(LANES), tok(N_ATT_HEADS * KV_LORA), seq(LANES), seq(KV_LORA + LANES),
                      pl.BlockSpec((None, KV_LORA + LANES, sk), lambda b, i: (b, 0, 0)),
                      full(wuv_bd), full(tril), full(acol)],
            out_specs=pl.BlockSpec((None, tq, D_ATT), lambda b, i: (b, i, 0)),
            out_shape=jax.ShapeDtypeStruct((B, nq * tq, D_ATT), F32),
            scratch_shapes=[pltpu.VMEM((sk, tq), I32), pltpu.VMEM((sk, tq), F32), pltpu.VMEM((sk, tq), F32),
                            pltpu.VMEM((sk, tq), BF16)],
            compiler_params=_cparams(("arbitrary", "arbitrary")),
            name=f"dsa_attn_k{sk}",
        )(iq, iw, qabs, ik, ca, cat, wuv_bd, tril, acol))
    return jnp.concatenate(outs, axis=1)


def _layernorm_rows(y, g, b):
    mu = jnp.mean(y, axis=-1, keepdims=True)
    d = y - mu
    var = jnp.mean(d * d, axis=-1, keepdims=True)
    return d * lax.rsqrt(var + 1e-5) * g + b


def _mix_kernel(orw_ref, ods_ref, x_ref, gt_ref, sc_ref, sh_ref, wtop_ref, wbot_ref, g_ref, b_ref, wr_ref, br_ref,
                tril_ref, triu_ref, x1_o, u2_o, route_o, tbl_o, cnt_o, carry, *, alpha):
    first = (pl.program_id(0) == 0) & (pl.program_id(1) == 0)
    tm = x_ref.shape[0]

    @pl.when(first)
    def _():
        carry[...] = jnp.zeros_like(carry)

    mix = _bdot(orw_ref[...], wtop_ref[...]) + _bdot(ods_ref[...], wbot_ref[...])
    x1 = _layernorm_rows(alpha * x_ref[...] + (1.0 + gt_ref[...]) * mix, g_ref[...], b_ref[...])
    x1_o[...] = x1
    u2 = x1 * (1.0 + sc_ref[...]) + sh_ref[...]
    u2_o[...] = u2

    lg = _dot3(u2, wr_ref[...]) + br_ref[...]
    lane = lax.broadcasted_iota(I32, (tm, LANES), 1)
    lane_f = lane.astype(F32)
    idxs, vals = [], []
    for _ in range(TOP_K_EXPERTS):
        m = jnp.max(lg, axis=-1, keepdims=True)
        idx = jnp.min(jnp.where(lg == m, lane_f, float(LANES)), axis=-1, keepdims=True).astype(I32)
        idxs.append(idx)
        vals.append(m)
        lg = jnp.where(lane == idx, -jnp.inf, lg)
    es = [jnp.exp(v - vals[0]) for v in vals]
    den = es[0] + es[1] + es[2] + es[3]
    hot = jnp.zeros((tm, LANES), F32)
    for idx in idxs:
        hot = hot + jnp.where(lane == idx, 1.0, 0.0)
    before = jnp.dot(tril_ref[...], hot.astype(BF16), preferred_element_type=F32)
    cnt = jnp.sum(hot, axis=0, keepdims=True)
    n_run = jnp.floor((cnt + (RUN_ALIGN - 1)) * (1.0 / RUN_ALIGN))
    n_stage = jnp.floor((n_run * RUN_ALIGN + (STAGE_ALIGN - 1)) * (1.0 / STAGE_ALIGN))
    off = jnp.dot(jnp.broadcast_to(n_stage, (SUBLANES, LANES)).astype(BF16), triu_ref[...],
                  preferred_element_type=F32)[0:1, :] * STAGE_ALIGN
    where_in_stage = off + before
    route = jnp.zeros((tm, LANES), F32)
    for k in range(TOP_K_EXPERTS):
        spos = jnp.sum(jnp.where(lane == idxs[k], where_in_stage, 0.0), axis=-1, keepdims=True)
        route = jnp.where(lane == k, idxs[k].astype(F32), route)
        route = jnp.where(lane == TOP_K_EXPERTS + k, es[k] / den, route)
        route = jnp.where(lane == 2 * TOP_K_EXPERTS + k, spos, route)
    route_o[...] = route
    sub = lax.broadcasted_iota(I32, (SUBLANES, LANES), 0)
    tbl_o[...] = jnp.where(sub == 0, n_run, jnp.where(sub == 1, off, jnp.where(sub == 2, carry[0:1, :], 0.0)))
    carry[0:1, :] = carry[0:1, :] + n_run * RUN_ALIGN
    cnt_o[...] = carry[...]


def _mix_out(o_rwkv, o_dsa, x, gt1, sc2, sh2, w_out, ln_g, ln_b, w_router, b_router):
    B, S, D = x.shape
    tm = min(TM_PROJ, S)
    alpha = 2.0 ** 0.25
    wtop = w_out[:D_RWKV].astype(BF16)
    wbot = w_out[D_RWKV:].astype(BF16)
    wr_p = jnp.pad(w_router, ((0, 0), (0, LANES - N_EXPERTS)))
    br_p = jnp.pad(b_router.reshape(1, -1), ((0, 0), (0, LANES - N_EXPERTS)), constant_values=NEG_BIG)
    tril = jnp.asarray(np.tril(np.ones((tm, tm)), -1), BF16)
    triu = jnp.asarray(np.triu(np.ones((LANES, LANES)), 1), BF16)
    row = lambda v: v.reshape(1, -1)
    tok = lambda n: pl.BlockSpec((None, tm, n), lambda b, i: (b, i, 0))
    mod = pl.BlockSpec((None, 1, D), lambda b, i: (b, 0, 0))
    full = lambda a: pl.BlockSpec(a.shape, lambda b, i: (0,) * a.ndim)
    consts = [wtop, wbot, row(ln_g), row(ln_b), wr_p, br_p, tril, triu]
    return pl.pallas_call(
        functools.partial(_mix_kernel, alpha=alpha),
        grid=(B, S // tm),
        in_specs=[tok(D_RWKV), tok(D_ATT), tok(D), mod, mod, mod] + [full(a) for a in consts],
        out_specs=[tok(D), tok(D), tok(LANES), pl.BlockSpec((None, None, SUBLANES, LANES), lambda b, i: (b, i, 0, 0)),
                   pl.BlockSpec((SUBLANES, LANES), lambda b, i: (0, 0))],
        out_shape=[jax.ShapeDtypeStruct((B, S, D), F32), jax.ShapeDtypeStruct((B, S, D), F32),
                   jax.ShapeDtypeStruct((B, S, LANES), F32),
                   jax.ShapeDtypeStruct((B, S // tm, SUBLANES, LANES), F32),
                   jax.ShapeDtypeStruct((SUBLANES, LANES), F32)],
        scratch_shapes=[pltpu.VMEM((8, LANES), F32)],
        compiler_params=_cparams(("arbitrary", "arbitrary")),
        name="mix_out",
    )(o_rwkv, o_dsa, x, gt1, sc2, sh2, *consts)


RUN_ALIGN = SUBLANES
STAGE_ALIGN = 16
STAGE_ROWS = 1536
STAGE_CHUNK = 256


def _run_copies(nrun_ref, tile, copy_of):
    for e in range(N_EXPERTS):
        pieces = (nrun_ref[tile * N_EXPERTS + e] * RUN_ALIGN + STAGE_ALIGN - 1) // STAGE_ALIGN

        def piece(j, carry, e=e):
            copy_of(e, j).start(priority=e % 2)
            return carry

        lax.fori_loop(0, pieces, piece, 0)


def _dispatch_kernel(nrun_ref, off_ref, start_ref, npiece_ref, nstage_ref, zs_ref, zn_ref, tail_ref,
                     u_ref, route_ref, xs_out, stag, zeros, sem, zsem):
    i = pl.program_id(0)
    n = pl.num_programs(0)
    tm = u_ref.shape[0]
    bm = zeros.shape[0]
    slot = i % 2

    @pl.when(i == 0)
    def _():
        zeros[...] = jnp.zeros_like(zeros)
        fills = []
        for e in range(N_EXPERTS):
            for b in range(bm.bit_length()):
                rows = RUN_ALIGN << b
                if rows > bm:
                    break
                done = (zn_ref[e] >> (b + 1)) << (b + 1)
                dst = pl.multiple_of(zs_ref[e] + done * RUN_ALIGN, RUN_ALIGN)
                fills.append(((zn_ref[e] >> b) & 1 == 1,
                              pltpu.make_async_copy(zeros.at[pl.ds(0, rows)], xs_out.at[pl.ds(dst, rows)], zsem)))
        for pred, cp in fills:
            pl.when(pred)(cp.start)

        def tail_copy(j):
            return pltpu.make_async_copy(zeros, xs_out.at[pl.ds(pl.multiple_of(j * bm, bm), bm)], zsem)

        lax.fori_loop(tail_ref[0], tail_ref[1], lambda j, c: (tail_copy(j).start(), c)[1], 0)
        for pred, cp in fills:
            pl.when(pred)(cp.wait)
        lax.fori_loop(tail_ref[0], tail_ref[1], lambda j, c: (tail_copy(j).wait(), c)[1], 0)

    route_t = jnp.transpose(route_ref[...])
    spos = [route_t[2 * TOP_K_EXPERTS + k:2 * TOP_K_EXPERTS + k + 1, :].astype(I32) for k in range(TOP_K_EXPERTS)]
    ub = u_ref[...].astype(BF16)
    srow = lax.broadcasted_iota(I32, (STAGE_CHUNK, tm), 0)
    for c in range(STAGE_ROWS // STAGE_CHUNK):
        @pl.when(c * STAGE_CHUNK < nstage_ref[i])
        def _(c=c):
            rows = srow + c * STAGE_CHUNK
            sel = (rows == spos[0]) | (rows == spos[1]) | (rows == spos[2]) | (rows == spos[3])
            stag[slot, c * STAGE_CHUNK:(c + 1) * STAGE_CHUNK, :] = jnp.dot(
                jnp.where(sel, 1.0, 0.0).astype(BF16), ub, preferred_element_type=F32)

    def piece_copy(s, tile):
        def copy_of(e, j):
            src = pl.multiple_of(off_ref[tile * N_EXPERTS + e] + j * STAGE_ALIGN, STAGE_ALIGN)
            dst = pl.multiple_of(start_ref[tile * N_EXPERTS + e] + j * STAGE_ALIGN, RUN_ALIGN)
            return pltpu.make_async_copy(stag.at[s, pl.ds(src, STAGE_ALIGN)], xs_out.at[pl.ds(dst, STAGE_ALIGN)], sem)
        return copy_of

    def drain(tile):
        def w(j, carry):
            pltpu.make_async_copy(stag.at[0, pl.ds(0, STAGE_ALIGN)], xs_out.at[pl.ds(0, STAGE_ALIGN)], sem).wait()
            return carry
        lax.fori_loop(0, npiece_ref[tile], w, 0)

    @pl.when(i > 0)
    def _():
        drain(i - 1)

    _run_copies(nrun_ref, i, piece_copy(slot, i))

    @pl.when(i == n - 1)
    def _():
        drain(i)


def _moe_dispatch(u2, route, tables, n_rows):
    T, D = u2.shape
    tm = TM_ROUTE
    bm = BM_EXPERT
    return pl.pallas_call(
        _dispatch_kernel,
        grid_spec=pltpu.PrefetchScalarGridSpec(
            num_scalar_prefetch=len(tables),
            grid=(T // tm,),
            in_specs=[pl.BlockSpec((tm, D), lambda i, *_: (i, 0)),
                      pl.BlockSpec((tm, LANES), lambda i, *_: (i, 0))],
            out_specs=pl.BlockSpec(memory_space=pl.ANY),
            scratch_shapes=[pltpu.VMEM((2, STAGE_ROWS, D), F32), pltpu.VMEM((bm, D), F32),
                            pltpu.SemaphoreType.DMA(()), pltpu.SemaphoreType.DMA(())],
        ),
        out_shape=jax.ShapeDtypeStruct((n_rows, D), F32),
        compiler_params=_cparams(("arbitrary",)),
        name="moe_dispatch",
    )(*tables, u2, route)


GU_GROUP = 2 * LANES


def _deinterleave_perm():
    p = np.zeros((GU_GROUP, GU_GROUP), np.float32)
    l = np.arange(LANES)
    p[2 * l, l] = 1.0
    p[2 * l + 1, LANES + l] = 1.0
    return jnp.asarray(p, BF16)


def _expert_kernel(be_ref, nb_ref, valid_ref, xs_ref, wgu_hbm, bgu_ref, wd_hbm, bd_ref, perm_ref, ys_ref,
                   wg_buf, wd_buf, wp, wdb, sem):
    i = pl.program_id(0)
    bm = xs_ref.shape[0]
    e = be_ref[i]
    used = i < nb_ref[0]
    new_expert = (i == 0) | (e != be_ref[jnp.maximum(i - 1, 0)])
    n_groups = wp.shape[1] // GU_GROUP
    n_experts = wgu_hbm.shape[0]

    def fetch(ex):
        return (pltpu.make_async_copy(wgu_hbm.at[ex], wg_buf, sem.at[0]),
                pltpu.make_async_copy(wd_hbm.at[ex], wd_buf, sem.at[1]))

    @pl.when(used & new_expert)
    def _():
        @pl.when(i == 0)
        def _():
            for cp in fetch(e):
                cp.start()

        for cp in fetch(e):
            cp.wait()
        for j in range(n_groups):
            sl = slice(j * GU_GROUP, (j + 1) * GU_GROUP)
            wp[:, sl] = jnp.dot(wg_buf[:, sl].astype(BF16), perm_ref[...], preferred_element_type=F32).astype(BF16)
        wdb[...] = wd_buf[...].astype(BF16)

        @pl.when(e + 1 < n_experts)
        def _():
            for cp in fetch(e + 1):
                cp.start()

    def compute(m):
        xb = xs_ref[:m, :].astype(BF16)
        gu = jnp.dot(xb, wp[...], preferred_element_type=F32) + bgu_ref[...]
        hs = []
        for j in range(n_groups):
            gate = jnp.minimum(gu[:, j * GU_GROUP:j * GU_GROUP + LANES], SWIGLU_LIMIT)
            up = jnp.clip(gu[:, j * GU_GROUP + LANES:(j + 1) * GU_GROUP], -SWIGLU_LIMIT, SWIGLU_LIMIT)
            hs.append(((up + 1.0) * (gate * _sigmoid(gate * SWIGLU_ALPHA))).astype(BF16))
        h = jnp.concatenate(hs, axis=-1)
        ys_ref[:m, :] = jnp.dot(h, wdb[...], preferred_element_type=F32) + bd_ref[...]

    half = bm // 2

    @pl.when(used & (valid_ref[i] > half))
    def _():
        compute(bm)

    @pl.when(used & (valid_ref[i] <= half))
    def _():
        compute(half)
        ys_ref[half:, :] = jnp.zeros((bm - half, ys_ref.shape[1]), F32)

    @pl.when(jnp.logical_not(used))
    def _():
        ys_ref[...] = jnp.zeros_like(ys_ref)


def _moe_experts(xs, block_e, n_used, valid, w_gu, b_gu_p, w_dn, b_dn):
    n_rows, D = xs.shape
    E, _, F2 = w_gu.shape
    bm = BM_EXPERT
    n_blocks = n_rows // bm
    perm = _deinterleave_perm()
    wspec = lambda shp: pl.BlockSpec((None,) + shp, lambda i, be, nb, va: (be[i], 0, 0))
    hbm = pl.BlockSpec(memory_space=pl.ANY)
    return pl.pallas_call(
        _expert_kernel,
        grid_spec=pltpu.PrefetchScalarGridSpec(
            num_scalar_prefetch=3,
            grid=(n_blocks,),
            in_specs=[pl.BlockSpec((bm, D), lambda i, be, nb, va: (jnp.minimum(i, nb[0] - 1), 0)),
                      hbm, wspec((1, F2)), hbm, wspec((1, D)),
                      pl.BlockSpec(perm.shape, lambda i, be, nb, va: (0, 0))],
            out_specs=pl.BlockSpec((bm, D), lambda i, be, nb, va: (i, 0)),
            scratch_shapes=[pltpu.VMEM((D, F2), F32), pltpu.VMEM((F2 // 2, D), F32),
                            pltpu.VMEM((D, F2), BF16), pltpu.VMEM((F2 // 2, D), BF16),
                            pltpu.SemaphoreType.DMA((2,))],
        ),
        out_shape=jax.ShapeDtypeStruct((n_rows, D), F32),
        compiler_params=pltpu.CompilerParams(dimension_semantics=("arbitrary",), vmem_limit_bytes=VMEM_LIMIT_EXPERTS),
        name="moe_experts",
    )(block_e, n_used, valid, xs, w_gu, b_gu_p, w_dn, b_dn, perm)


def _combine_kernel(nrun_ref, off_ref, start_ref, npiece_ref, nstage_ref, ys_ref, x1_ref, route_ref, gt_ref, g_ref,
                    b_ref, o_ref, stag, sem, *, alpha):
    i = pl.program_id(0)
    n = pl.num_programs(0)
    tm = x1_ref.shape[0]
    slot = i % 2

    def gather(tile, s):
        def copy_of(e, j):
            src = pl.multiple_of(start_ref[tile * N_EXPERTS + e] + j * STAGE_ALIGN, RUN_ALIGN)
            dst = pl.multiple_of(off_ref[tile * N_EXPERTS + e] + j * STAGE_ALIGN, STAGE_ALIGN)
            return pltpu.make_async_copy(ys_ref.at[pl.ds(src, STAGE_ALIGN)], stag.at[s, pl.ds(dst, STAGE_ALIGN)],
                                         sem.at[s])
        _run_copies(nrun_ref, tile, copy_of)

    @pl.when(i == 0)
    def _():
        stag[...] = jnp.zeros_like(stag)
        gather(0, 0)

    @pl.when(i + 1 < n)
    def _():
        gather(i + 1, 1 - slot)

    def w(j, carry):
        pltpu.make_async_copy(ys_ref.at[pl.ds(0, STAGE_ALIGN)], stag.at[slot, pl.ds(0, STAGE_ALIGN)],
                              sem.at[slot]).wait()
        return carry
    lax.fori_loop(0, npiece_ref[i], w, 0)

    route = route_ref[...]
    wide = lambda col: jnp.broadcast_to(col, (tm, STAGE_CHUNK))
    spos = [wide(route[:, 2 * TOP_K_EXPERTS + k:2 * TOP_K_EXPERTS + k + 1].astype(I32)) for k in range(TOP_K_EXPERTS)]
    gate = [wide(route[:, TOP_K_EXPERTS + k:TOP_K_EXPERTS + k + 1]) for k in range(TOP_K_EXPERTS)]
    scol = lax.broadcasted_iota(I32, (tm, STAGE_CHUNK), 1)
    acc_ref = o_ref
    acc_ref[...] = jnp.zeros_like(acc_ref)
    for c in range(STAGE_ROWS // STAGE_CHUNK):
        @pl.when(c * STAGE_CHUNK < nstage_ref[i])
        def _(c=c):
            cols = scol + c * STAGE_CHUNK
            wgt = jnp.zeros((tm, STAGE_CHUNK), F32)
            for k in range(TOP_K_EXPERTS):
                wgt = wgt + jnp.where(cols == spos[k], gate[k], 0.0)
            acc_ref[...] += jnp.dot(wgt.astype(BF16), stag[slot, c * STAGE_CHUNK:(c + 1) * STAGE_CHUNK, :].astype(BF16),
                                    preferred_element_type=F32)
    o_ref[...] = _layernorm_rows(alpha * x1_ref[...] + (1.0 + gt_ref[...]) * acc_ref[...], g_ref[...], b_ref[...])


def _moe_combine(ys, tables, x1, route, gt2, ln_g, ln_b, tiles_per_batch):
    T, D = x1.shape
    tm = TM_ROUTE
    row = lambda v: v.reshape(1, -1)
    return pl.pallas_call(
        functools.partial(_combine_kernel, alpha=2.0 ** 0.25),
        grid_spec=pltpu.PrefetchScalarGridSpec(
            num_scalar_prefetch=len(tables),
            grid=(T // tm,),
            in_specs=[pl.BlockSpec(memory_space=pl.ANY),
                      pl.BlockSpec((tm, D), lambda i, *_: (i, 0)),
                      pl.BlockSpec((tm, LANES), lambda i, *_: (i, 0)),
                      pl.BlockSpec((None, 1, D), lambda i, *_: (i // tiles_per_batch, 0, 0)),
                      pl.BlockSpec((1, D), lambda i, *_: (0, 0)),
                      pl.BlockSpec((1, D), lambda i, *_: (0, 0))],
            out_specs=pl.BlockSpec((tm, D), lambda i, *_: (i, 0)),
            scratch_shapes=[pltpu.VMEM((2, STAGE_ROWS, D), F32), pltpu.SemaphoreType.DMA((2,))],
        ),
        out_shape=jax.ShapeDtypeStruct((T, D), F32),
        compiler_params=_cparams(("arbitrary",)),
        name="moe_combine",
    )(*tables, ys, x1, route, gt2, row(ln_g), row(ln_b))


def _moe_and_norm(x1, u2, route, tbl, totals, gt2, w_gu, b_gu, w_dn, b_dn, ln_g, ln_b):
    B, S, D = x1.shape
    T = B * S
    bm = BM_EXPERT
    tm = TM_ROUTE
    assert T % tm == 0 and TM_PROJ == tm and STAGE_ROWS >= tm * TOP_K_EXPERTS + N_EXPERTS * (STAGE_ALIGN - 1)
    n_tiles = T // tm
    max_rows = T * TOP_K_EXPERTS + n_tiles * N_EXPERTS * (RUN_ALIGN - 1) + N_EXPERTS * STAGE_ALIGN
    n_blocks = -(-max_rows // bm) + N_EXPERTS
    tot = totals[0, :N_EXPERTS].astype(I32)
    padded = (tot + STAGE_ALIGN + bm - 1) // bm * bm
    pad_ends = jnp.cumsum(padded)
    pad_starts = pad_ends - padded
    t3 = tbl.reshape(n_tiles, SUBLANES, LANES)[:, :, :N_EXPERTS].astype(I32)
    nrun, off, base = t3[:, 0, :], t3[:, 1, :], t3[:, 2, :]
    start = pad_starts[None, :] + base
    pieces = (nrun * RUN_ALIGN + STAGE_ALIGN - 1) // STAGE_ALIGN
    npiece = jnp.sum(pieces, axis=1)
    nstage = jnp.sum(pieces, axis=1) * STAGE_ALIGN
    flat = lambda z: z.reshape(-1).astype(I32)
    run_tables = [flat(nrun), flat(off), flat(start), flat(npiece), flat(nstage)]
    zs = pad_starts + tot
    zn = (pad_ends - zs) // RUN_ALIGN
    n_used = (pad_ends[-1:] // bm).astype(I32)
    tail = jnp.concatenate([n_used, jnp.full((1,), n_blocks, I32)])
    blk_row = jnp.arange(n_blocks, dtype=I32) * bm
    block_e = jnp.minimum(jnp.sum((blk_row[:, None] >= pad_ends[None, :]).astype(I32), axis=1), N_EXPERTS - 1)
    E, F2 = b_gu.shape
    b_gu_p = b_gu.reshape(E, F2 // GU_GROUP, LANES, 2).transpose(0, 1, 3, 2).reshape(E, 1, F2)
    route2 = route.reshape(T, LANES)
    xs = _moe_dispatch(u2.reshape(T, D), route2, run_tables + [flat(zs), flat(zn), tail], n_blocks * bm)
    valid = jnp.clip(zs[block_e] - blk_row, 0, bm)
    ys = _moe_experts(xs, block_e, n_used, valid, w_gu, b_gu_p, w_dn, b_dn[:, None, :])
    out = _moe_combine(ys, run_tables, x1.reshape(T, D), route2, gt2, ln_g, ln_b, S // tm)
    return out.reshape(B, S, D)


def kernel(x, c, w_ada, b_ada, w_in, shift_mu, rwkv_w0, rwkv_w2, rwkv_a0, rwkv_a2, rwkv_g2, rwkv_k_k, rwkv_k_a, rwkv_r_k, rwkv_ln_w, rwkv_ln_b, mla_q_norm, mla_w_q_up, mla_kv_norm, mla_w_uk, mla_w_uv, idx_w_q, idx_ln_g, idx_ln_b, w_out, ln1_g, ln1_b, w_router, b_router, w_gu, b_gu, w_dn, b_dn, ln2_g, ln2_b):
    depth = w_ada.shape[0]
    assert depth == 1, "DeepNorm constants below are for a single layer"
    l = 0
    mod = _ada_mod(c, w_ada[l], b_ada[l])
    sh1, sc1, gt1, sh2, sc2, gt2 = [m[:, None, :] for m in jnp.split(mod, 6, axis=-1)]
    r, lw, k, v, kk, a, g, qabs, iq, ik, iw, cl, clt = _in_proj(
        x, sc1, sh1, w_in[l], shift_mu[l], rwkv_w0[l], rwkv_w2[l], rwkv_a0[l], rwkv_a2[l], rwkv_g2[l], rwkv_k_k[l],
        rwkv_k_a[l], mla_q_norm[l], mla_w_q_up[l], mla_w_uk[l], mla_kv_norm[l], idx_w_q[l], idx_ln_g[l], idx_ln_b[l])
    o_rwkv = _rwkv_scan(r, lw, k, v, kk, a, g, rwkv_r_k[l], rwkv_ln_w[l], rwkv_ln_b[l])
    o_dsa = _dsa_attn(iq, iw, qabs, ik, cl, clt, mla_w_uv[l])
    x1, u2, route, tbl, totals = _mix_out(o_rwkv, o_dsa, x, gt1, sc2, sh2, w_out[l], ln1_g[l], ln1_b[l],
                                          w_router[l], b_router[l])
    return _moe_and_norm(x1, u2, route, tbl, totals, gt2, w_gu[l], b_gu[l], w_dn[l], b_dn[l], ln2_g[l], ln2_b[l])
```

```python
import functools
import math

import jax
import jax.numpy as jnp
import numpy as np
from jax import lax
from jax.experimental import pallas as pl
from jax.experimental.pallas import tpu as pltpu

F32 = jnp.float32
BF16 = jnp.bfloat16
I32 = jnp.int32

RWKV_HEAD = 64
N_RWKV_HEADS = 8
D_RWKV = RWKV_HEAD * N_RWKV_HEADS
RWKV_GN_EPS = 64e-5
ATT_HEAD = 64
N_ATT_HEADS = 8
D_ATT = ATT_HEAD * N_ATT_HEADS
KV_LORA = 128
IDX_HEADS = 8
IDX_DIM = 64
TOPK_MAX = 256
N_EXPERTS = 32
TOP_K_EXPERTS = 4
SWIGLU_LIMIT = 7.0
SWIGLU_ALPHA = 1.702
NEG_BIG = -1e30
LOG2E = 1.4426950408889634
INT_MIN = -(2 ** 31)

LANES = 128
SUBLANES = 8
VMEM_LIMIT = 48 * 1024 * 1024
VMEM_LIMIT_EXPERTS = 58 * 1024 * 1024

TM_IN = 512
TM_PROJ = 256
L_CHUNK = 64
CHUNKS_PER_STEP = 4
TQ = 256
KEY_CHUNK = 128
SUM_ROWS = 128
TM_ROUTE = 256
BM_EXPERT = 1024

_SEG = (("r", 512, 512), ("k", 512, 512), ("v", 512, 512), ("w", 64, 128), ("a", 64, 128), ("g", 128, 128),
        ("q", 256, 256), ("kv", 128, 128), ("ik", 64, 128), ("iw", 8, 128))
N_SHIFT_P = 512 * 3 + 128 * 3
N_IN_P = sum(s[2] for s in _SEG)


def _cparams(sem):
    return pltpu.CompilerParams(dimension_semantics=sem, vmem_limit_bytes=VMEM_LIMIT)


def _bdot(a, b):
    return jnp.dot(a.astype(BF16), b.astype(BF16), preferred_element_type=F32)


def _split2(a):
    hi = a.astype(BF16)
    lo = (a - hi.astype(F32)).astype(BF16)
    return hi, lo


def _split3(a):
    hi = a.astype(BF16)
    r1 = a - hi.astype(F32)
    mid = r1.astype(BF16)
    lo = (r1 - mid.astype(F32)).astype(BF16)
    return hi, mid, lo


def _dot3(a, b, dims=(((1,), (0,)), ((), ()))):
    ah, al = _split2(a)
    bh, bl = _split2(b)
    d = functools.partial(lax.dot_general, dimension_numbers=dims, preferred_element_type=F32)
    return d(ah, bh) + (d(ah, bl) + d(al, bh))


def _dot_exact_rhs(a, b_exact, nsplit=3):
    parts = _split3(a) if nsplit == 3 else _split2(a)
    acc = None
    for p in parts[::-1]:
        t = jnp.dot(p, b_exact, preferred_element_type=F32)
        acc = t if acc is None else acc + t
    return acc


def _dot_exact_lhs(a_exact, b, nsplit=3):
    parts = _split3(b) if nsplit == 3 else _split2(b)
    acc = None
    for p in parts[::-1]:
        t = jnp.dot(a_exact, p, preferred_element_type=F32)
        acc = t if acc is None else acc + t
    return acc


def _sigmoid(x):
    return 1.0 / (1.0 + jnp.exp(-x))


def _softplus(x):
    return jnp.maximum(x, 0.0) + jnp.log(1.0 + jnp.exp(-jnp.abs(x)))


def _ada_kernel(c_ref, w_ref, b_ref, o_ref):
    c = c_ref[...]
    o_ref[...] = _dot3(c * _sigmoid(c), w_ref[...]) + b_ref[...]


def _ada_mod(c, w_ada, b_ada):
    B, D = c.shape
    N = w_ada.shape[1]
    tn = 1024
    return pl.pallas_call(
        _ada_kernel,
        grid=(N // tn,),
        in_specs=[pl.BlockSpec((B, D), lambda j: (0, 0)),
                  pl.BlockSpec((D, tn), lambda j: (0, j)),
                  pl.BlockSpec((1, tn), lambda j: (0, j))],
        out_specs=pl.BlockSpec((B, tn), lambda j: (0, j)),
        out_shape=jax.ShapeDtypeStruct((B, N), F32),
        compiler_params=_cparams(("arbitrary",)),
        name="ada_mod",
    )(c, w_ada, b_ada.reshape(1, N))


def _in_proj_kernel(x_ref, sc_ref, sh_ref, win_ref, mu_ref, w0_ref, w2_ref, a0_ref, a2_ref, g2_ref, kk_ref, ka_ref,
                    ones_ref, qn_ref, wq_ref, wuk_ref, kvn_ref, wiq_ref, ig_ref, ib_ref,
                    r_o, lw_o, k_o, v_o, kkn_o, a_o, g_o, qabs_o, iq_o, ik_o, iw_o, cl_o, clt_o, carry):
    i = pl.program_id(1)
    tm = x_ref.shape[0]

    @pl.when(i == 0)
    def _():
        carry[...] = jnp.zeros_like(carry)

    u = x_ref[...] * (1.0 + sc_ref[...]) + sh_ref[...]
    p = _bdot(u, win_ref[...])
    ps = p[:, :N_SHIFT_P]
    rows = lax.broadcasted_iota(I32, (tm, 1), 0)
    prev = jnp.where(rows == 0, carry[0:1, :], pltpu.roll(ps, 1, 0))
    carry[0:1, :] = ps[tm - 1:tm, :]
    ps = ps + mu_ref[...] * (prev - ps)

    pr, pk, pv = ps[:, 0:512], ps[:, 512:1024], ps[:, 1024:1536]
    pw, pa, pg = ps[:, 1536:1664], ps[:, 1664:1792], ps[:, 1792:1920]
    w_log = -_softplus(-(w0_ref[...] + _dot3(jnp.tanh(pw), w2_ref[...]))) - 0.5
    lw_o[...] = -jnp.exp(w_log)
    a = _sigmoid(a0_ref[...] + _dot3(pa, a2_ref[...]))
    g_o[...] = _dot3(_sigmoid(pg), g2_ref[...])
    kk = pk * kk_ref[...]
    ssq = _dot_exact_rhs(kk * kk, ones_ref[...], nsplit=2)
    kkn_o[...] = kk / jnp.maximum(jnp.sqrt(ssq), 1e-12)
    k_o[...] = pk * (1.0 + (a - 1.0) * ka_ref[...])
    r_o[...] = pr
    v_o[...] = pv
    a_o[...] = a

    pq, pkv = p[:, 1920:2176], p[:, 2176:2304]
    pik, piw = p[:, 2304:2432], p[:, 2432:2560]
    q_lat = pq * lax.rsqrt(jnp.mean(pq * pq, axis=-1, keepdims=True) + 1e-6) * qn_ref[...]
    q = _bdot(q_lat, wq_ref[...])
    qabs_o[...] = (_bdot(q, wuk_ref[...]) * (ATT_HEAD ** -0.5 * LOG2E)).astype(BF16)
    c_lat = pkv * lax.rsqrt(jnp.mean(pkv * pkv, axis=-1, keepdims=True) + 1e-6) * kvn_ref[...]
    spos = i * tm + rows
    s_hi = (spos >> 7).astype(F32)
    s_lo = (spos & (LANES - 1)).astype(F32)
    lane_t = lax.broadcasted_iota(I32, (1, LANES), 1)
    extra = jnp.where(lane_t == 0, 1.0, jnp.where((lane_t == 1) | (lane_t == 2), s_hi,
                                                  jnp.where((lane_t == 3) | (lane_t == 4), s_lo, 0.0)))
    c_aug = jnp.concatenate([c_lat, extra], axis=-1)
    cl_o[...] = c_aug.astype(BF16)
    clt_o[...] = jnp.transpose(c_aug).astype(BF16)
    iq_o[...] = (_bdot(q_lat, wiq_ref[...]) * (IDX_DIM ** -0.5)).astype(BF16)
    lane = lax.broadcasted_iota(I32, (1, LANES), 1)
    valid = lane < IDX_DIM
    mu = jnp.sum(pik, axis=-1, keepdims=True) * (1.0 / IDX_DIM)
    dlt = jnp.where(valid, pik - mu, 0.0)
    var = jnp.sum(dlt * dlt, axis=-1, keepdims=True) * (1.0 / IDX_DIM)
    ik_o[...] = jnp.where(valid, dlt * lax.rsqrt(var + 1e-5) * ig_ref[...] + ib_ref[...], 0.0).astype(BF16)
    iw_o[...] = piw * (IDX_HEADS ** -0.5)


def _pad_cols(w, widths):
    parts, o = [], 0
    for true, padded in widths:
        seg = w[..., o:o + true]
        if padded > true:
            seg = jnp.pad(seg, [(0, 0)] * (w.ndim - 1) + [(0, padded - true)])
        parts.append(seg)
        o += true
    return jnp.concatenate(parts, axis=-1)


def _pad_rows(w, rows):
    return jnp.pad(w, ((0, rows - w.shape[0]), (0, 0)))


def _block_diag(blocks):
    H, a, b = blocks.shape
    eye = jnp.eye(H, dtype=blocks.dtype)
    return (eye[:, None, :, None] * blocks[:, :, None, :]).reshape(H * a, H * b)


def _head_ones(n, head):
    idx = np.arange(n) // head
    return jnp.asarray(idx[:, None] == idx[None, :], BF16)


def _in_proj(x, sc1, sh1, w_in, shift_mu, w0, w2, a0, a2, g2, k_k, k_a, q_norm, w_q_up, w_uk, kv_norm, idx_w_q,
             idx_ln_g, idx_ln_b):
    B, S, D = x.shape
    tm = min(TM_IN, S)
    widths = tuple((s[1], s[2]) for s in _SEG)
    win_p = _pad_cols(w_in, widths).astype(BF16)
    mu_p = _pad_cols(shift_mu.reshape(1, -1), widths[:6])
    w2_p = _pad_rows(w2, LANES)
    a2_p = _pad_rows(a2, LANES)
    wuk_bd = _block_diag(w_uk).astype(BF16)
    wiq_p = _pad_cols(idx_w_q, ((IDX_DIM, LANES),) * IDX_HEADS).astype(BF16)
    ig_p = _pad_cols(idx_ln_g.reshape(1, -1), ((IDX_DIM, LANES),))
    ib_p = _pad_cols(idx_ln_b.reshape(1, -1), ((IDX_DIM, LANES),))
    row = lambda v: v.reshape(1, -1)
    tok = lambda n: pl.BlockSpec((None, tm, n), lambda b, i: (b, i, 0))
    mod = pl.BlockSpec((None, 1, D), lambda b, i: (b, 0, 0))
    full = lambda a: pl.BlockSpec(a.shape, lambda b, i: (0,) * a.ndim)
    consts = [win_p, mu_p, row(w0), w2_p, row(a0), a2_p, g2, row(k_k), row(k_a), _head_ones(D_RWKV, RWKV_HEAD),
              row(q_norm), w_q_up.astype(BF16), wuk_bd, row(kv_norm), wiq_p, ig_p, ib_p]
    outs = [(D_RWKV, F32)] * 7 + [(N_ATT_HEADS * KV_LORA, BF16), (IDX_HEADS * LANES, BF16), (LANES, BF16),
                                  (LANES, F32), (KV_LORA + LANES, BF16)]
    return pl.pallas_call(
        _in_proj_kernel,
        grid=(B, S // tm),
        in_specs=[tok(D), mod, mod] + [full(a) for a in consts],
        out_specs=[tok(n) for n, _ in outs] + [pl.BlockSpec((None, KV_LORA + LANES, tm), lambda b, i: (b, 0, i))],
        out_shape=[jax.ShapeDtypeStruct((B, S, n), dt) for n, dt in outs]
        + [jax.ShapeDtypeStruct((B, KV_LORA + LANES, S), BF16)],
        scratch_shapes=[pltpu.VMEM((8, N_SHIFT_P), F32)],
        compiler_params=_cparams(("arbitrary", "arbitrary")),
        name="in_proj",
    )(x, sc1, sh1, *consts)


def _rwkv_kernel(r_ref, lw_ref, k_ref, v_ref, kk_ref, a_ref, g_ref, rk_ref, lnw_ref, lnb_ref, tri_ref, ones_ref,
                 o_ref, state, *, L):
    c = pl.program_id(1)
    nc = r_ref.shape[0] // L

    @pl.when(c == 0)
    def _():
        state[...] = jnp.zeros_like(state)

    r, lw, k, v, kk, a = r_ref[...], lw_ref[...], k_ref[...], v_ref[...], kk_ref[...], a_ref[...]
    cum = _dot_exact_lhs(tri_ref[...], lw)
    rows = [slice(ci * L, (ci + 1) * L) for ci in range(nc)]
    last = [cum[(ci + 1) * L - 1:(ci + 1) * L, :] for ci in range(nc)]
    cum_last = jnp.concatenate([jnp.broadcast_to(z, (L, z.shape[1])) for z in last], axis=0)
    w_incl = jnp.exp(cum)
    w_inv = jnp.exp(-cum)
    w_rel = jnp.exp(cum_last - cum)
    w_last = [jnp.exp(z) for z in last]
    bvec = kk * a
    at = -kk * jnp.exp(cum - lw)
    rt = r * w_incl
    bt = bvec * w_inv
    kt = k * w_inv
    bh = bvec * w_rel
    kh = k * w_rel
    ti = lax.broadcasted_iota(I32, (L, L), 0)
    tj = lax.broadcasted_iota(I32, (L, L), 1)
    strict = tj < ti
    incl = tj <= ti
    eye = lax.broadcasted_iota(I32, (RWKV_HEAD, RWKV_HEAD), 0) == lax.broadcasted_iota(I32, (RWKV_HEAD, RWKV_HEAD), 1)
    NT = (((1,), (1,)), ((), ()))
    TN = (((0,), (0,)), ((), ()))
    heads = range(N_RWKV_HEADS)
    sls = [slice(h * RWKV_HEAD, (h + 1) * RWKV_HEAD) for h in heads]
    units = [(ci, h) for ci in range(nc) for h in heads]
    idx = range(len(units))
    mm = lambda x, y, dims=(((1,), (0,)), ((), ())): lax.dot_general(
        x.astype(BF16), y.astype(BF16), dims, preferred_element_type=F32)
    at_b, rt_b, bt_b, kt_b, bh_b, kh_b, v_b = [z.astype(BF16) for z in (at, rt, bt, kt, bh, kh, v)]
    cut = lambda z, u: z[rows[u[0]], sls[u[1]]]
    vh = [cut(v_b, u) for u in units]
    ath = [cut(at_b, u) for u in units]
    ar = [jnp.concatenate([cut(at_b, u), cut(rt_b, u)], axis=0) for u in units]
    g_b = [mm(ar[i], cut(bt_b, units[i]), NT) for i in idx]
    g_k = [mm(ar[i], cut(kt_b, units[i]), NT) for i in idx]
    n_ab = [jnp.where(strict, g_b[i][:L], 0.0) for i in idx]
    a_ak = [jnp.where(strict, g_k[i][:L], 0.0) for i in idx]
    a_rb = [jnp.where(incl, g_b[i][L:], 0.0).astype(BF16) for i in idx]
    a_rk = [jnp.where(incl, g_k[i][L:], 0.0) for i in idx]
    akv = [mm(a_ak[i], vh[i]) for i in idx]
    eye_l = jnp.where(ti == tj, 1.0, 0.0)
    tinv = [eye_l + n_ab[i] for i in idx]
    pw = n_ab
    for _ in range(int(math.log2(L)) - 1):
        pw = [mm(pw[i], pw[i]) for i in idx]
        tinv = [tinv[i] + mm(pw[i], tinv[i]) for i in idx]
    tinv = [t.astype(BF16) for t in tinv]
    a_t = [mm(tinv[i], ath[i]).astype(BF16) for i in idx]
    y = [mm(tinv[i], akv[i]).astype(BF16) for i in idx]
    m_c = [jnp.where(eye, w_last[units[i][0]][:, sls[units[i][1]]], 0.0) + mm(a_t[i], cut(bh_b, units[i]), TN)
           for i in idx]
    c_c = [mm(y[i], cut(bh_b, units[i]), TN) + mm(vh[i], cut(kh_b, units[i]), TN) for i in idx]
    q_c = [cut(rt, units[i]) + mm(a_rb[i], a_t[i]) for i in idx]
    o_loc = [mm(a_rb[i], y[i]) + mm(a_rk[i], vh[i]) for i in idx]
    s = [state[h] for h in heads]
    for ci in range(nc):
        for h in heads:
            i = ci * N_RWKV_HEADS + h
            o = o_loc[i] + mm(q_c[i], s[h], NT)
            s[h] = mm(s[h], m_c[i]) + c_c[i]
            mu = jnp.mean(o, axis=-1, keepdims=True)
            d = o - mu
            var = jnp.mean(d * d, axis=-1, keepdims=True)
            o_ref[rows[ci], sls[h]] = d * lax.rsqrt(var + RWKV_GN_EPS)
    for h in heads:
        state[h] = s[h]
    bonus = _dot_exact_rhs(r * k * rk_ref[...], ones_ref[...], nsplit=3) * v
    o_ref[...] = (o_ref[...] * lnw_ref[...] + lnb_ref[...] + bonus) * g_ref[...]


def _rwkv_scan(r, lw, k, v, kk, a, g, r_k, ln_w, ln_b):
    B, S, DR = r.shape
    L = min(L_CHUNK, S)
    lb = min(L * CHUNKS_PER_STEP, S)
    tri = jnp.asarray(np.kron(np.eye(lb // L), np.tril(np.ones((L, L)))), BF16)
    row = lambda z: z.reshape(1, -1)
    tok = pl.BlockSpec((None, lb, DR), lambda b, c: (b, c, 0))
    full = lambda z: pl.BlockSpec(z.shape, lambda b, c: (0,) * z.ndim)
    consts = [row(r_k), row(ln_w), row(ln_b), tri, _head_ones(DR, RWKV_HEAD)]
    return pl.pallas_call(
        functools.partial(_rwkv_kernel, L=L),
        grid=(B, S // lb),
        in_specs=[tok] * 7 + [full(z) for z in consts],
        out_specs=tok,
        out_shape=jax.ShapeDtypeStruct((B, S, DR), F32),
        scratch_shapes=[pltpu.VMEM((N_RWKV_HEADS, RWKV_HEAD, RWKV_HEAD), F32)],
        compiler_params=_cparams(("arbitrary", "arbitrary")),
        name="rwkv_scan",
    )(r, lw, k, v, kk, a, g, *consts)


def _alibi_cols():
    slope = np.asarray([2.0 ** (-8.0 * (h + 1) / N_ATT_HEADS) * LOG2E for h in range(N_ATT_HEADS)], np.float32)
    c_hi = slope.astype(BF16).astype(np.float32)
    c_lo = (slope - c_hi).astype(BF16).astype(np.float32)
    t = np.zeros((N_ATT_HEADS, LANES), np.float32)
    t[:, 1], t[:, 2], t[:, 3], t[:, 4] = LANES * c_hi, LANES * c_lo, c_hi, c_lo
    return jnp.asarray(t.astype(BF16))


def _dsa_kernel(iq_ref, iw_ref, qa_ref, ik_ref, ca_ref, cat_ref, wuv_ref, tril_ref, acol_ref, o_ref, key_ref,
                bias_ref, lg_ref, p_ref, *, topk, q_off, select):
    qi = pl.program_id(1) + q_off
    tq = qa_ref.shape[0]
    sk = ca_ref.shape[0]
    kc = min(KEY_CHUNK, sk)
    chunks = [slice(c * kc, (c + 1) * kc) for c in range(sk // kc)]
    tpos = qi * tq + lax.broadcasted_iota(I32, (1, tq), 1)
    srow = lax.broadcasted_iota(I32, (kc, tq), 0)
    NT = (((1,), (1,)), ((), ()))

    if not select:
        for c, cs in enumerate(chunks):
            bias_ref[cs, :] = jnp.where(srow + c * kc <= tpos, 0.0, NEG_BIG)
    else:
        iw_t = jnp.transpose(iw_ref[...])
        for c, cs in enumerate(chunks):
            ikc = ik_ref[cs, :]
            score = jnp.zeros((kc, tq), F32)
            for h in range(IDX_HEADS):
                s = lax.dot_general(ikc, iq_ref[:, h * LANES:(h + 1) * LANES], NT, preferred_element_type=F32)
                score = score + iw_t[h:h + 1, :] * jnp.maximum(s, 0.0)
            bits = pltpu.bitcast(score + 0.0, I32)
            key = bits ^ ((bits >> 31) & 0x7FFFFFFF)
            key_ref[cs, :] = jnp.where(srow + c * kc <= tpos, key, INT_MIN)
        kcount = jnp.minimum(topk, tpos + 1).astype(F32)

        def sum_keys(x):
            part = jnp.sum(x.reshape(x.shape[0] // SUM_ROWS, SUM_ROWS, x.shape[1]), axis=0)
            return jnp.sum(part, axis=0, keepdims=True)

        def count_ge(cand):
            return sum_keys(jnp.where(key_ref[...] >= cand, 1.0, 0.0))

        thr = jnp.where(count_ge(jnp.zeros((1, tq), I32)) >= kcount, 0, INT_MIN).astype(I32)

        def bit_step(i, thr):
            cand = thr | (1 << (30 - i))
            return jnp.where(count_ge(cand) >= kcount, cand, thr)

        thr = lax.fori_loop(0, 31, bit_step, thr)

        need = kcount - sum_keys(jnp.where(key_ref[...] > thr, 1.0, 0.0))
        carry = jnp.zeros((1, tq), F32)
        for j in range(sk // LANES):
            sl = slice(j * LANES, (j + 1) * LANES)
            key = key_ref[sl, :]
            eq = key == thr
            e = jnp.where(eq, 1.0, 0.0)
            before = jnp.dot(tril_ref[...], e.astype(BF16), preferred_element_type=F32) + carry
            sel = (key > thr) | (eq & (before < need))
            bias_ref[sl, :] = jnp.where(sel, 0.0, NEG_BIG)
            carry = carry + jnp.sum(e, axis=0, keepdims=True)

    outs = []
    for h in range(N_ATT_HEADS):
        q_aug = jnp.concatenate([qa_ref[:, h * KV_LORA:(h + 1) * KV_LORA],
                                 jnp.broadcast_to(acol_ref[h:h + 1, :], (tq, LANES))], axis=-1)
        lg_all = lax.dot_general(ca_ref[...], q_aug, NT, preferred_element_type=F32)
        m = jnp.full((1, tq), -jnp.inf, F32)
        for cs in chunks:
            lg = lg_all[cs, :] + bias_ref[cs, :]
            lg_ref[cs, :] = lg
            m = jnp.maximum(m, jnp.max(lg, axis=0, keepdims=True))
        for cs in chunks:
            p_ref[cs, :] = jnp.exp2(lg_ref[cs, :] - m).astype(BF16)
        pv = jnp.dot(cat_ref[...], p_ref[...], preferred_element_type=F32)
        outs.append((pv[:KV_LORA, :] / pv[KV_LORA:KV_LORA + 1, :]).astype(BF16))
    o_lat_t = jnp.concatenate(outs, axis=0)
    o_ref[...] = lax.dot_general(o_lat_t, wuv_ref[...], (((0,), (0,)), ((), ())), preferred_element_type=F32)


def _dsa_attn(iq, iw, qabs, ik, ca, cat, w_uv):
    B, S, _ = iq.shape
    tq = min(TQ, S)
    topk = min(TOPK_MAX, S // 4)
    wuv_bd = _block_diag(w_uv).astype(BF16)
    tril = jnp.asarray(np.tril(np.ones((LANES, LANES)), -1), BF16)
    acol = _alibi_cols()
    full = lambda z: pl.BlockSpec(z.shape, lambda b, i: (0,) * z.ndim)
    nq = 1
    outs = []
    for q_off in range(0, S // tq, nq):
        sk = (q_off + nq) * tq
        tok = lambda n, q_off=q_off: pl.BlockSpec((None, tq, n), lambda b, i: (b, i + q_off, 0))
        seq = lambda n, sk=sk: pl.BlockSpec((None, sk, n), lambda b, i: (b, 0, 0))
        outs.append(pl.pallas_call(
            functools.partial(_dsa_kernel, topk=topk, q_off=q_off, select=sk > topk),
            grid=(B, nq),
            in_specs=[tok(IDX_HEADS * LANES), tok(LANES), tok(N_ATT_HEADS * KV_LORA), seq(LANES), seq(KV_LORA + LANES),
                      pl.BlockSpec((None, KV_LORA + LANES, sk), lambda b, i: (b, 0, 0)),
                      full(wuv_bd), full(tril), full(acol)],
            out_specs=pl.BlockSpec((None, tq, D_ATT), lambda b, i: (b, i, 0)),
            out_shape=jax.ShapeDtypeStruct((B, nq * tq, D_ATT), F32),
            scratch_shapes=[pltpu.VMEM((sk, tq), I32), pltpu.VMEM((sk, tq), F32), pltpu.VMEM((sk, tq), F32),
                            pltpu.VMEM((sk, tq), BF16)],
            compiler_params=_cparams(("arbitrary", "arbitrary")),
            name=f"dsa_attn_k{sk}",
        )(iq, iw, qabs, ik, ca, cat, wuv_bd, tril, acol))
    return jnp.concatenate(outs, axis=1)


def _layernorm_rows(y, g, b):
    mu = jnp.mean(y, axis=-1, keepdims=True)
    d = y - mu
    var = jnp.mean(d * d, axis=-1, keepdims=True)
    return d * lax.rsqrt(var + 1e-5) * g + b


def _mix_kernel(orw_ref, ods_ref, x_ref, gt_ref, sc_ref, sh_ref, wtop_ref, wbot_ref, g_ref, b_ref, wr_ref, br_ref,
                tril_ref, triu_ref, x1_o, u2_o, route_o, tbl_o, cnt_o, carry, *, alpha):
    first = (pl.program_id(0) == 0) & (pl.program_id(1) == 0)
    tm = x_ref.shape[0]

    @pl.when(first)
    def _():
        carry[...] = jnp.zeros_like(carry)

    mix = _bdot(orw_ref[...], wtop_ref[...]) + _bdot(ods_ref[...], wbot_ref[...])
    x1 = _layernorm_rows(alpha * x_ref[...] + (1.0 + gt_ref[...]) * mix, g_ref[...], b_ref[...])
    x1_o[...] = x1
    u2 = x1 * (1.0 + sc_ref[...]) + sh_ref[...]
    u2_o[...] = u2

    lg = _dot3(u2, wr_ref[...]) + br_ref[...]
    lane = lax.broadcasted_iota(I32, (tm, LANES), 1)
    lane_f = lane.astype(F32)
    idxs, vals = [], []
    for _ in range(TOP_K_EXPERTS):
        m = jnp.max(lg, axis=-1, keepdims=True)
        idx = jnp.min(jnp.where(lg == m, lane_f, float(LANES)), axis=-1, keepdims=True).astype(I32)
        idxs.append(idx)
        vals.append(m)
        lg = jnp.where(lane == idx, -jnp.inf, lg)
    es = [jnp.exp(v - vals[0]) for v in vals]
    den = es[0] + es[1] + es[2] + es[3]
    hot = jnp.zeros((tm, LANES), F32)
    for idx in idxs:
        hot = hot + jnp.where(lane == idx, 1.0, 0.0)
    before = jnp.dot(tril_ref[...], hot.astype(BF16), preferred_element_type=F32)
    cnt = jnp.sum(hot, axis=0, keepdims=True)
    n_run = jnp.floor((cnt + (RUN_ALIGN - 1)) * (1.0 / RUN_ALIGN))
    n_stage = jnp.floor((n_run * RUN_ALIGN + (STAGE_ALIGN - 1)) * (1.0 / STAGE_ALIGN))
    off = jnp.dot(jnp.broadcast_to(n_stage, (SUBLANES, LANES)).astype(BF16), triu_ref[...],
                  preferred_element_type=F32)[0:1, :] * STAGE_ALIGN
    where_in_stage = off + before
    route = jnp.zeros((tm, LANES), F32)
    for k in range(TOP_K_EXPERTS):
        spos = jnp.sum(jnp.where(lane == idxs[k], where_in_stage, 0.0), axis=-1, keepdims=True)
        route = jnp.where(lane == k, idxs[k].astype(F32), route)
        route = jnp.where(lane == TOP_K_EXPERTS + k, es[k] / den, route)
        route = jnp.where(lane == 2 * TOP_K_EXPERTS + k, spos, route)
    route_o[...] = route
    sub = lax.broadcasted_iota(I32, (SUBLANES, LANES), 0)
    tbl_o[...] = jnp.where(sub == 0, n_run, jnp.where(sub == 1, off, jnp.where(sub == 2, carry[0:1, :], 0.0)))
    carry[0:1, :] = carry[0:1, :] + n_run * RUN_ALIGN
    cnt_o[...] = carry[...]


def _mix_out(o_rwkv, o_dsa, x, gt1, sc2, sh2, w_out, ln_g, ln_b, w_router, b_router):
    B, S, D = x.shape
    tm = min(TM_PROJ, S)
    alpha = 2.0 ** 0.25
    wtop = w_out[:D_RWKV].astype(BF16)
    wbot = w_out[D_RWKV:].astype(BF16)
    wr_p = jnp.pad(w_router, ((0, 0), (0, LANES - N_EXPERTS)))
    br_p = jnp.pad(b_router.reshape(1, -1), ((0, 0), (0, LANES - N_EXPERTS)), constant_values=NEG_BIG)
    tril = jnp.asarray(np.tril(np.ones((tm, tm)), -1), BF16)
    triu = jnp.asarray(np.triu(np.ones((LANES, LANES)), 1), BF16)
    row = lambda v: v.reshape(1, -1)
    tok = lambda n: pl.BlockSpec((None, tm, n), lambda b, i: (b, i, 0))
    mod = pl.BlockSpec((None, 1, D), lambda b, i: (b, 0, 0))
    full = lambda a: pl.BlockSpec(a.shape, lambda b, i: (0,) * a.ndim)
    consts = [wtop, wbot, row(ln_g), row(ln_b), wr_p, br_p, tril, triu]
    return pl.pallas_call(
        functools.partial(_mix_kernel, alpha=alpha),
        grid=(B, S // tm),
        in_specs=[tok(D_RWKV), tok(D_ATT), tok(D), mod, mod, mod] + [full(a) for a in consts],
        out_specs=[tok(D), tok(D), tok(LANES), pl.BlockSpec((None, None, SUBLANES, LANES), lambda b, i: (b, i, 0, 0)),
                   pl.BlockSpec((SUBLANES, LANES), lambda b, i: (0, 0))],
        out_shape=[jax.ShapeDtypeStruct((B, S, D), F32), jax.ShapeDtypeStruct((B, S, D), F32),
                   jax.ShapeDtypeStruct((B, S, LANES), F32),
                   jax.ShapeDtypeStruct((B, S // tm, SUBLANES, LANES), F32),
                   jax.ShapeDtypeStruct((SUBLANES, LANES), F32)],
        scratch_shapes=[pltpu.VMEM((8, LANES), F32)],
        compiler_params=_cparams(("arbitrary", "arbitrary")),
        name="mix_out",
    )(o_rwkv, o_dsa, x, gt1, sc2, sh2, *consts)


RUN_ALIGN = SUBLANES
STAGE_ALIGN = 16
STAGE_ROWS = 1536
STAGE_CHUNK = 256


def _run_copies(nrun_ref, tile, copy_of):
    for e in range(N_EXPERTS):
        pieces = (nrun_ref[tile * N_EXPERTS + e] * RUN_ALIGN + STAGE_ALIGN - 1) // STAGE_ALIGN

        def piece(j, carry, e=e):
            copy_of(e, j).start(priority=e % 2)
            return carry

        lax.fori_loop(0, pieces, piece, 0)


def _dispatch_kernel(nrun_ref, off_ref, start_ref, npiece_ref, nstage_ref, zs_ref, zn_ref, tail_ref,
                     u_ref, route_ref, xs_out, stag, zeros, sem, zsem):
    i = pl.program_id(0)
    n = pl.num_programs(0)
    tm = u_ref.shape[0]
    bm = zeros.shape[0]
    slot = i % 2

    @pl.when(i == 0)
    def _():
        zeros[...] = jnp.zeros_like(zeros)
        fills = []
        for e in range(N_EXPERTS):
            for b in range(bm.bit_length()):
                rows = RUN_ALIGN << b
                if rows > bm:
                    break
                done = (zn_ref[e] >> (b + 1)) << (b + 1)
                dst = pl.multiple_of(zs_ref[e] + done * RUN_ALIGN, RUN_ALIGN)
                fills.append(((zn_ref[e] >> b) & 1 == 1,
                              pltpu.make_async_copy(zeros.at[pl.ds(0, rows)], xs_out.at[pl.ds(dst, rows)], zsem)))
        for pred, cp in fills:
            pl.when(pred)(cp.start)

        def tail_copy(j):
            return pltpu.make_async_copy(zeros, xs_out.at[pl.ds(pl.multiple_of(j * bm, bm), bm)], zsem)

        lax.fori_loop(tail_ref[0], tail_ref[1], lambda j, c: (tail_copy(j).start(), c)[1], 0)
        for pred, cp in fills:
            pl.when(pred)(cp.wait)
        lax.fori_loop(tail_ref[0], tail_ref[1], lambda j, c: (tail_copy(j).wait(), c)[1], 0)

    route_t = jnp.transpose(route_ref[...])
    spos = [route_t[2 * TOP_K_EXPERTS + k:2 * TOP_K_EXPERTS + k + 1, :].astype(I32) for k in range(TOP_K_EXPERTS)]
    ub = u_ref[...].astype(BF16)
    srow = lax.broadcasted_iota(I32, (STAGE_CHUNK, tm), 0)
    for c in range(STAGE_ROWS // STAGE_CHUNK):
        @pl.when(c * STAGE_CHUNK < nstage_ref[i])
        def _(c=c):
            rows = srow + c * STAGE_CHUNK
            sel = (rows == spos[0]) | (rows == spos[1]) | (rows == spos[2]) | (rows == spos[3])
            stag[slot, c * STAGE_CHUNK:(c + 1) * STAGE_CHUNK, :] = jnp.dot(
                jnp.where(sel, 1.0, 0.0).astype(BF16), ub, preferred_element_type=F32)

    def piece_copy(s, tile):
        def copy_of(e, j):
            src = pl.multiple_of(off_ref[tile * N_EXPERTS + e] + j * STAGE_ALIGN, STAGE_ALIGN)
            dst = pl.multiple_of(start_ref[tile * N_EXPERTS + e] + j * STAGE_ALIGN, RUN_ALIGN)
            return pltpu.make_async_copy(stag.at[s, pl.ds(src, STAGE_ALIGN)], xs_out.at[pl.ds(dst, STAGE_ALIGN)], sem)
        return copy_of

    def drain(tile):
        def w(j, carry):
            pltpu.make_async_copy(stag.at[0, pl.ds(0, STAGE_ALIGN)], xs_out.at[pl.ds(0, STAGE_ALIGN)], sem).wait()
            return carry
        lax.fori_loop(0, npiece_ref[tile], w, 0)

    @pl.when(i > 0)
    def _():
        drain(i - 1)

    _run_copies(nrun_ref, i, piece_copy(slot, i))

    @pl.when(i == n - 1)
    def _():
        drain(i)


def _moe_dispatch(u2, route, tables, n_rows):
    T, D = u2.shape
    tm = TM_ROUTE
    bm = BM_EXPERT
    return pl.pallas_call(
        _dispatch_kernel,
        grid_spec=pltpu.PrefetchScalarGridSpec(
            num_scalar_prefetch=len(tables),
            grid=(T // tm,),
            in_specs=[pl.BlockSpec((tm, D), lambda i, *_: (i, 0)),
                      pl.BlockSpec((tm, LANES), lambda i, *_: (i, 0))],
            out_specs=pl.BlockSpec(memory_space=pl.ANY),
            scratch_shapes=[pltpu.VMEM((2, STAGE_ROWS, D), F32), pltpu.VMEM((bm, D), F32),
                            pltpu.SemaphoreType.DMA(()), pltpu.SemaphoreType.DMA(())],
        ),
        out_shape=jax.ShapeDtypeStruct((n_rows, D), F32),
        compiler_params=_cparams(("arbitrary",)),
        name="moe_dispatch",
    )(*tables, u2, route)


GU_GROUP = 2 * LANES


def _deinterleave_perm():
    p = np.zeros((GU_GROUP, GU_GROUP), np.float32)
    l = np.arange(LANES)
    p[2 * l, l] = 1.0
    p[2 * l + 1, LANES + l] = 1.0
    return jnp.asarray(p, BF16)


def _expert_kernel(be_ref, nb_ref, valid_ref, xs_ref, wgu_hbm, bgu_ref, wd_hbm, bd_ref, perm_ref, ys_ref,
                   wg_buf, wd_buf, wp, wdb, sem):
    i = pl.program_id(0)
    bm = xs_ref.shape[0]
    e = be_ref[i]
    used = i < nb_ref[0]
    new_expert = (i == 0) | (e != be_ref[jnp.maximum(i - 1, 0)])
    n_groups = wp.shape[1] // GU_GROUP
    n_experts = wgu_hbm.shape[0]

    def fetch(ex):
        return (pltpu.make_async_copy(wgu_hbm.at[ex], wg_buf, sem.at[0]),
                pltpu.make_async_copy(wd_hbm.at[ex], wd_buf, sem.at[1]))

    @pl.when(used & new_expert)
    def _():
        @pl.when(i == 0)
        def _():
            for cp in fetch(e):
                cp.start()

        for cp in fetch(e):
            cp.wait()
        for j in range(n_groups):
            sl = slice(j * GU_GROUP, (j + 1) * GU_GROUP)
            wp[:, sl] = jnp.dot(wg_buf[:, sl].astype(BF16), perm_ref[...], preferred_element_type=F32).astype(BF16)
        wdb[...] = wd_buf[...].astype(BF16)

        @pl.when(e + 1 < n_experts)
        def _():
            for cp in fetch(e + 1):
                cp.start()

    def compute(m):
        xb = xs_ref[:m, :].astype(BF16)
        gu = jnp.dot(xb, wp[...], preferred_element_type=F32) + bgu_ref[...]
        hs = []
        for j in range(n_groups):
            gate = jnp.minimum(gu[:, j * GU_GROUP:j * GU_GROUP + LANES], SWIGLU_LIMIT)
            up = jnp.clip(gu[:, j * GU_GROUP + LANES:(j + 1) * GU_GROUP], -SWIGLU_LIMIT, SWIGLU_LIMIT)
            hs.append(((up + 1.0) * (gate * _sigmoid(gate * SWIGLU_ALPHA))).astype(BF16))
        h = jnp.concatenate(hs, axis=-1)
        ys_ref[:m, :] = jnp.dot(h, wdb[...], preferred_element_type=F32) + bd_ref[...]

    half = bm // 2

    @pl.when(used & (valid_ref[i] > half))
    def _():
        compute(bm)

    @pl.when(used & (valid_ref[i] <= half))
    def _():
        compute(half)
        ys_ref[half:, :] = jnp.zeros((bm - half, ys_ref.shape[1]), F32)

    @pl.when(jnp.logical_not(used))
    def _():
        ys_ref[...] = jnp.zeros_like(ys_ref)


def _moe_experts(xs, block_e, n_used, valid, w_gu, b_gu_p, w_dn, b_dn):
    n_rows, D = xs.shape
    E, _, F2 = w_gu.shape
    bm = BM_EXPERT
    n_blocks = n_rows // bm
    perm = _deinterleave_perm()
    wspec = lambda shp: pl.BlockSpec((None,) + shp, lambda i, be, nb, va: (be[i], 0, 0))
    hbm = pl.BlockSpec(memory_space=pl.ANY)
    return pl.pallas_call(
        _expert_kernel,
        grid_spec=pltpu.PrefetchScalarGridSpec(
            num_scalar_prefetch=3,
            grid=(n_blocks,),
            in_specs=[pl.BlockSpec((bm, D), lambda i, be, nb, va: (jnp.minimum(i, nb[0] - 1), 0)),
                      hbm, wspec((1, F2)), hbm, wspec((1, D)),
                      pl.BlockSpec(perm.shape, lambda i, be, nb, va: (0, 0))],
            out_specs=pl.BlockSpec((bm, D), lambda i, be, nb, va: (i, 0)),
            scratch_shapes=[pltpu.VMEM((D, F2), F32), pltpu.VMEM((F2 // 2, D), F32),
                            pltpu.VMEM((D, F2), BF16), pltpu.VMEM((F2 // 2, D), BF16),
                            pltpu.SemaphoreType.DMA((2,))],
        ),
        out_shape=jax.ShapeDtypeStruct((n_rows, D), F32),
        compiler_params=pltpu.CompilerParams(dimension_semantics=("arbitrary",), vmem_limit_bytes=VMEM_LIMIT_EXPERTS),
        name="moe_experts",
    )(block_e, n_used, valid, xs, w_gu, b_gu_p, w_dn, b_dn, perm)


def _combine_kernel(nrun_ref, off_ref, start_ref, npiece_ref, nstage_ref, ys_ref, x1_ref, route_ref, gt_ref, g_ref,
                    b_ref, o_ref, stag, sem, *, alpha):
    i = pl.program_id(0)
    n = pl.num_programs(0)
    tm = x1_ref.shape[0]
    slot = i % 2

    def gather(tile, s):
        def copy_of(e, j):
            src = pl.multiple_of(start_ref[tile * N_EXPERTS + e] + j * STAGE_ALIGN, RUN_ALIGN)
            dst = pl.multiple_of(off_ref[tile * N_EXPERTS + e] + j * STAGE_ALIGN, STAGE_ALIGN)
            return pltpu.make_async_copy(ys_ref.at[pl.ds(src, STAGE_ALIGN)], stag.at[s, pl.ds(dst, STAGE_ALIGN)],
                                         sem.at[s])
        _run_copies(nrun_ref, tile, copy_of)

    @pl.when(i == 0)
    def _():
        stag[...] = jnp.zeros_like(stag)
        gather(0, 0)

    @pl.when(i + 1 < n)
    def _():
        gather(i + 1, 1 - slot)

    def w(j, carry):
        pltpu.make_async_copy(ys_ref.at[pl.ds(0, STAGE_ALIGN)], stag.at[slot, pl.ds(0, STAGE_ALIGN)],
                              sem.at[slot]).wait()
        return carry
    lax.fori_loop(0, npiece_ref[i], w, 0)

    route = route_ref[...]
    wide = lambda col: jnp.broadcast_to(col, (tm, STAGE_CHUNK))
    spos = [wide(route[:, 2 * TOP_K_EXPERTS + k:2 * TOP_K_EXPERTS + k + 1].astype(I32)) for k in range(TOP_K_EXPERTS)]
    gate = [wide(route[:, TOP_K_EXPERTS + k:TOP_K_EXPERTS + k + 1]) for k in range(TOP_K_EXPERTS)]
    scol = lax.broadcasted_iota(I32, (tm, STAGE_CHUNK), 1)
    acc_ref = o_ref
    acc_ref[...] = jnp.zeros_like(acc_ref)
    for c in range(STAGE_ROWS // STAGE_CHUNK):
        @pl.when(c * STAGE_CHUNK < nstage_ref[i])
        def _(c=c):
            cols = scol + c * STAGE_CHUNK
            wgt = jnp.zeros((tm, STAGE_CHUNK), F32)
            for k in range(TOP_K_EXPERTS):
                wgt = wgt + jnp.where(cols == spos[k], gate[k], 0.0)
            acc_ref[...] += jnp.dot(wgt.astype(BF16), stag[slot, c * STAGE_CHUNK:(c + 1) * STAGE_CHUNK, :].astype(BF16),
                                    preferred_element_type=F32)
    o_ref[...] = _layernorm_rows(alpha * x1_ref[...] + (1.0 + gt_ref[...]) * acc_ref[...], g_ref[...], b_ref[...])


def _moe_combine(ys, tables, x1, route, gt2, ln_g, ln_b, tiles_per_batch):
    T, D = x1.shape
    tm = TM_ROUTE
    row = lambda v: v.reshape(1, -1)
    return pl.pallas_call(
        functools.partial(_combine_kernel, alpha=2.0 ** 0.25),
        grid_spec=pltpu.PrefetchScalarGridSpec(
            num_scalar_prefetch=len(tables),
            grid=(T // tm,),
            in_specs=[pl.BlockSpec(memory_space=pl.ANY),
                      pl.BlockSpec((tm, D), lambda i, *_: (i, 0)),
                      pl.BlockSpec((tm, LANES), lambda i, *_: (i, 0)),
                      pl.BlockSpec((None, 1, D), lambda i, *_: (i // tiles_per_batch, 0, 0)),
                      pl.BlockSpec((1, D), lambda i, *_: (0, 0)),
                      pl.BlockSpec((1, D), lambda i, *_: (0, 0))],
            out_specs=pl.BlockSpec((tm, D), lambda i, *_: (i, 0)),
            scratch_shapes=[pltpu.VMEM((2, STAGE_ROWS, D), F32), pltpu.SemaphoreType.DMA((2,))],
        ),
        out_shape=jax.ShapeDtypeStruct((T, D), F32),
        compiler_params=_cparams(("arbitrary",)),
        name="moe_combine",
    )(*tables, ys, x1, route, gt2, row(ln_g), row(ln_b))


def _moe_and_norm(x1, u2, route, tbl, totals, gt2, w_gu, b_gu, w_dn, b_dn, ln_g, ln_b):
    B, S, D = x1.shape
    T = B * S
    bm = BM_EXPERT
    tm = TM_ROUTE
    assert T % tm == 0 and TM_PROJ == tm and STAGE_ROWS >= tm * TOP_K_EXPERTS + N_EXPERTS * (STAGE_ALIGN - 1)
    n_tiles = T // tm
    max_rows = T * TOP_K_EXPERTS + n_tiles * N_EXPERTS * (RUN_ALIGN - 1) + N_EXPERTS * STAGE_ALIGN
    n_blocks = -(-max_rows // bm) + N_EXPERTS
    tot = totals[0, :N_EXPERTS].astype(I32)
    padded = (tot + STAGE_ALIGN + bm - 1) // bm * bm
    pad_ends = jnp.cumsum(padded)
    pad_starts = pad_ends - padded
    t3 = tbl.reshape(n_tiles, SUBLANES, LANES)[:, :, :N_EXPERTS].astype(I32)
    nrun, off, base = t3[:, 0, :], t3[:, 1, :], t3[:, 2, :]
    start = pad_starts[None, :] + base
    pieces = (nrun * RUN_ALIGN + STAGE_ALIGN - 1) // STAGE_ALIGN
    npiece = jnp.sum(pieces, axis=1)
    nstage = jnp.sum(pieces, axis=1) * STAGE_ALIGN
    flat = lambda z: z.reshape(-1).astype(I32)
    run_tables = [flat(nrun), flat(off), flat(start), flat(npiece), flat(nstage)]
    zs = pad_starts + tot
    zn = (pad_ends - zs) // RUN_ALIGN
    n_used = (pad_ends[-1:] // bm).astype(I32)
    tail = jnp.concatenate([n_used, jnp.full((1,), n_blocks, I32)])
    blk_row = jnp.arange(n_blocks, dtype=I32) * bm
    block_e = jnp.minimum(jnp.sum((blk_row[:, None] >= pad_ends[None, :]).astype(I32), axis=1), N_EXPERTS - 1)
    E, F2 = b_gu.shape
    b_gu_p = b_gu.reshape(E, F2 // GU_GROUP, LANES, 2).transpose(0, 1, 3, 2).reshape(E, 1, F2)
    route2 = route.reshape(T, LANES)
    xs = _moe_dispatch(u2.reshape(T, D), route2, run_tables + [flat(zs), flat(zn), tail], n_blocks * bm)
    valid = jnp.clip(zs[block_e] - blk_row, 0, bm)
    ys = _moe_experts(xs, block_e, n_used, valid, w_gu, b_gu_p, w_dn, b_dn[:, None, :])
    out = _moe_combine(ys, run_tables, x1.reshape(T, D), route2, gt2, ln_g, ln_b, S // tm)
    return out.reshape(B, S, D)


def kernel(x, c, w_ada, b_ada, w_in, shift_mu, rwkv_w0, rwkv_w2, rwkv_a0, rwkv_a2, rwkv_g2, rwkv_k_k, rwkv_k_a, rwkv_r_k, rwkv_ln_w, rwkv_ln_b, mla_q_norm, mla_w_q_up, mla_kv_norm, mla_w_uk, mla_w_uv, idx_w_q, idx_ln_g, idx_ln_b, w_out, ln1_g, ln1_b, w_router, b_router, w_gu, b_gu, w_dn, b_dn, ln2_g, ln2_b):
    depth = w_ada.shape[0]
    assert depth == 1, "DeepNorm constants below are for a single layer"
    l = 0
    mod = _ada_mod(c, w_ada[l], b_ada[l])
    sh1, sc1, gt1, sh2, sc2, gt2 = [m[:, None, :] for m in jnp.split(mod, 6, axis=-1)]
    r, lw, k, v, kk, a, g, qabs, iq, ik, iw, cl, clt = _in_proj(
        x, sc1, sh1, w_in[l], shift_mu[l], rwkv_w0[l], rwkv_w2[l], rwkv_a0[l], rwkv_a2[l], rwkv_g2[l], rwkv_k_k[l],
        rwkv_k_a[l], mla_q_norm[l], mla_w_q_up[l], mla_w_uk[l], mla_kv_norm[l], idx_w_q[l], idx_ln_g[l], idx_ln_b[l])
    o_rwkv = _rwkv_scan(r, lw, k, v, kk, a, g, rwkv_r_k[l], rwkv_ln_w[l], rwkv_ln_b[l])
    o_dsa = _dsa_attn(iq, iw, qabs, ik, cl, clt, mla_w_uv[l])
    x1, u2, route, tbl, totals = _mix_out(o_rwkv, o_dsa, x, gt1, sc2, sh2, w_out[l], ln1_g[l], ln1_b[l],
                                          w_router[l], b_router[l])
    return _moe_and_norm(x1, u2, route, tbl, totals, gt2, w_gu[l], b_gu[l], w_dn[l], b_dn[l], ln2_g[l], ln2_b[l])
```

```python
import functools
import math

import jax
import jax.numpy as jnp
import numpy as np
from jax import lax
from jax.experimental import pallas as pl
from jax.experimental.pallas import tpu as pltpu

F32 = jnp.float32
BF16 = jnp.bfloat16
I32 = jnp.int32

RWKV_HEAD = 64
N_RWKV_HEADS = 8
D_RWKV = RWKV_HEAD * N_RWKV_HEADS
RWKV_GN_EPS = 64e-5
ATT_HEAD = 64
N_ATT_HEADS = 8
D_ATT = ATT_HEAD * N_ATT_HEADS
KV_LORA = 128
IDX_HEADS = 8
IDX_DIM = 64
TOPK_MAX = 256
N_EXPERTS = 32
TOP_K_EXPERTS = 4
SWIGLU_LIMIT = 7.0
SWIGLU_ALPHA = 1.702
NEG_BIG = -1e30
LOG2E = 1.4426950408889634
INT_MIN = -(2 ** 31)

LANES = 128
SUBLANES = 8
VMEM_LIMIT = 48 * 1024 * 1024
VMEM_LIMIT_EXPERTS = 58 * 1024 * 1024

TM_IN = 512
TM_PROJ = 256
L_CHUNK = 64
CHUNKS_PER_STEP = 4
TQ = 256
KEY_CHUNK = 128
SUM_ROWS = 128
TM_ROUTE = 256
BM_EXPERT = 1024

_SEG = (("r", 512, 512), ("k", 512, 512), ("v", 512, 512), ("w", 64, 128), ("a", 64, 128), ("g", 128, 128),
        ("q", 256, 256), ("kv", 128, 128), ("ik", 64, 128), ("iw", 8, 128))
N_SHIFT_P = 512 * 3 + 128 * 3
N_IN_P = sum(s[2] for s in _SEG)


def _cparams(sem):
    return pltpu.CompilerParams(dimension_semantics=sem, vmem_limit_bytes=VMEM_LIMIT)


def _bdot(a, b):
    return jnp.dot(a.astype(BF16), b.astype(BF16), preferred_element_type=F32)


def _split2(a):
    hi = a.astype(BF16)
    lo = (a - hi.astype(F32)).astype(BF16)
    return hi, lo


def _split3(a):
    hi = a.astype(BF16)
    r1 = a - hi.astype(F32)
    mid = r1.astype(BF16)
    lo = (r1 - mid.astype(F32)).astype(BF16)
    return hi, mid, lo


def _dot3(a, b, dims=(((1,), (0,)), ((), ()))):
    ah, al = _split2(a)
    bh, bl = _split2(b)
    d = functools.partial(lax.dot_general, dimension_numbers=dims, preferred_element_type=F32)
    return d(ah, bh) + (d(ah, bl) + d(al, bh))


def _dot_exact_rhs(a, b_exact, nsplit=3):
    parts = _split3(a) if nsplit == 3 else _split2(a)
    acc = None
    for p in parts[::-1]:
        t = jnp.dot(p, b_exact, preferred_element_type=F32)
        acc = t if acc is None else acc + t
    return acc


def _dot_exact_lhs(a_exact, b, nsplit=3):
    parts = _split3(b) if nsplit == 3 else _split2(b)
    acc = None
    for p in parts[::-1]:
        t = jnp.dot(a_exact, p, preferred_element_type=F32)
        acc = t if acc is None else acc + t
    return acc


def _sigmoid(x):
    return 1.0 / (1.0 + jnp.exp(-x))


def _softplus(x):
    return jnp.maximum(x, 0.0) + jnp.log(1.0 + jnp.exp(-jnp.abs(x)))


def _ada_kernel(c_ref, w_ref, b_ref, o_ref):
    c = c_ref[...]
    o_ref[...] = _dot3(c * _sigmoid(c), w_ref[...]) + b_ref[...]


def _ada_mod(c, w_ada, b_ada):
    B, D = c.shape
    N = w_ada.shape[1]
    tn = 1024
    return pl.pallas_call(
        _ada_kernel,
        grid=(N // tn,),
        in_specs=[pl.BlockSpec((B, D), lambda j: (0, 0)),
                  pl.BlockSpec((D, tn), lambda j: (0, j)),
                  pl.BlockSpec((1, tn), lambda j: (0, j))],
        out_specs=pl.BlockSpec((B, tn), lambda j: (0, j)),
        out_shape=jax.ShapeDtypeStruct((B, N), F32),
        compiler_params=_cparams(("arbitrary",)),
        name="ada_mod",
    )(c, w_ada, b_ada.reshape(1, N))


def _in_proj_kernel(x_ref, sc_ref, sh_ref, win_ref, mu_ref, w0_ref, w2_ref, a0_ref, a2_ref, g2_ref, kk_ref, ka_ref,
                    ones_ref, qn_ref, wq_ref, wuk_ref, kvn_ref, wiq_ref, ig_ref, ib_ref,
                    r_o, lw_o, k_o, v_o, kkn_o, a_o, g_o, qabs_o, iq_o, ik_o, iw_o, cl_o, clt_o, carry):
    i = pl.program_id(1)
    tm = x_ref.shape[0]

    @pl.when(i == 0)
    def _():
        carry[...] = jnp.zeros_like(carry)

    u = x_ref[...] * (1.0 + sc_ref[...]) + sh_ref[...]
    p = _bdot(u, win_ref[...])
    ps = p[:, :N_SHIFT_P]
    rows = lax.broadcasted_iota(I32, (tm, 1), 0)
    prev = jnp.where(rows == 0, carry[0:1, :], pltpu.roll(ps, 1, 0))
    carry[0:1, :] = ps[tm - 1:tm, :]
    ps = ps + mu_ref[...] * (prev - ps)

    pr, pk, pv = ps[:, 0:512], ps[:, 512:1024], ps[:, 1024:1536]
    pw, pa, pg = ps[:, 1536:1664], ps[:, 1664:1792], ps[:, 1792:1920]
    w_log = -_softplus(-(w0_ref[...] + _dot3(jnp.tanh(pw), w2_ref[...]))) - 0.5
    lw_o[...] = -jnp.exp(w_log)
    a = _sigmoid(a0_ref[...] + _dot3(pa, a2_ref[...]))
    g_o[...] = _dot3(_sigmoid(pg), g2_ref[...])
    kk = pk * kk_ref[...]
    ssq = _dot_exact_rhs(kk * kk, ones_ref[...], nsplit=2)
    kkn_o[...] = kk / jnp.maximum(jnp.sqrt(ssq), 1e-12)
    k_o[...] = pk * (1.0 + (a - 1.0) * ka_ref[...])
    r_o[...] = pr
    v_o[...] = pv
    a_o[...] = a

    pq, pkv = p[:, 1920:2176], p[:, 2176:2304]
    pik, piw = p[:, 2304:2432], p[:, 2432:2560]
    q_lat = pq * lax.rsqrt(jnp.mean(pq * pq, axis=-1, keepdims=True) + 1e-6) * qn_ref[...]
    q = _bdot(q_lat, wq_ref[...])
    qabs_o[...] = (_bdot(q, wuk_ref[...]) * (ATT_HEAD ** -0.5 * LOG2E)).astype(BF16)
    c_lat = pkv * lax.rsqrt(jnp.mean(pkv * pkv, axis=-1, keepdims=True) + 1e-6) * kvn_ref[...]
    spos = i * tm + rows
    s_hi = (spos >> 7).astype(F32)
    s_lo = (spos & (LANES - 1)).astype(F32)
    lane_t = lax.broadcasted_iota(I32, (1, LANES), 1)
    extra = jnp.where(lane_t == 0, 1.0, jnp.where((lane_t == 1) | (lane_t == 2), s_hi,
                                                  jnp.where((lane_t == 3) | (lane_t == 4), s_lo, 0.0)))
    c_aug = jnp.concatenate([c_lat, extra], axis=-1)
    cl_o[...] = c_aug.astype(BF16)
    clt_o[...] = jnp.transpose(c_aug).astype(BF16)
    iq_o[...] = (_bdot(q_lat, wiq_ref[...]) * (IDX_DIM ** -0.5)).astype(BF16)
    lane = lax.broadcasted_iota(I32, (1, LANES), 1)
    valid = lane < IDX_DIM
    mu = jnp.sum(pik, axis=-1, keepdims=True) * (1.0 / IDX_DIM)
    dlt = jnp.where(valid, pik - mu, 0.0)
    var = jnp.sum(dlt * dlt, axis=-1, keepdims=True) * (1.0 / IDX_DIM)
    ik_o[...] = jnp.where(valid, dlt * lax.rsqrt(var + 1e-5) * ig_ref[...] + ib_ref[...], 0.0).astype(BF16)
    iw_o[...] = piw * (IDX_HEADS ** -0.5)


def _pad_cols(w, widths):
    parts, o = [], 0
    for true, padded in widths:
        seg = w[..., o:o + true]
        if padded > true:
            seg = jnp.pad(seg, [(0, 0)] * (w.ndim - 1) + [(0, padded - true)])
        parts.append(seg)
        o += true
    return jnp.concatenate(parts, axis=-1)


def _pad_rows(w, rows):
    return jnp.pad(w, ((0, rows - w.shape[0]), (0, 0)))


def _block_diag(blocks):
    H, a, b = blocks.shape
    eye = jnp.eye(H, dtype=blocks.dtype)
    return (eye[:, None, :, None] * blocks[:, :, None, :]).reshape(H * a, H * b)


def _head_ones(n, head):
    idx = np.arange(n) // head
    return jnp.asarray(idx[:, None] == idx[None, :], BF16)


def _in_proj(x, sc1, sh1, w_in, shift_mu, w0, w2, a0, a2, g2, k_k, k_a, q_norm, w_q_up, w_uk, kv_norm, idx_w_q,
             idx_ln_g, idx_ln_b):
    B, S, D = x.shape
    tm = min(TM_IN, S)
    widths = tuple((s[1], s[2]) for s in _SEG)
    win_p = _pad_cols(w_in, widths).astype(BF16)
    mu_p = _pad_cols(shift_mu.reshape(1, -1), widths[:6])
    w2_p = _pad_rows(w2, LANES)
    a2_p = _pad_rows(a2, LANES)
    wuk_bd = _block_diag(w_uk).astype(BF16)
    wiq_p = _pad_cols(idx_w_q, ((IDX_DIM, LANES),) * IDX_HEADS).astype(BF16)
    ig_p = _pad_cols(idx_ln_g.reshape(1, -1), ((IDX_DIM, LANES),))
    ib_p = _pad_cols(idx_ln_b.reshape(1, -1), ((IDX_DIM, LANES),))
    row = lambda v: v.reshape(1, -1)
    tok = lambda n: pl.BlockSpec((None, tm, n), lambda b, i: (b, i, 0))
    mod = pl.BlockSpec((None, 1, D), lambda b, i: (b, 0, 0))
    full = lambda a: pl.BlockSpec(a.shape, lambda b, i: (0,) * a.ndim)
    consts = [win_p, mu_p, row(w0), w2_p, row(a0), a2_p, g2, row(k_k), row(k_a), _head_ones(D_RWKV, RWKV_HEAD),
              row(q_norm), w_q_up.astype(BF16), wuk_bd, row(kv_norm), wiq_p, ig_p, ib_p]
    outs = [(D_RWKV, F32)] * 7 + [(N_ATT_HEADS * KV_LORA, BF16), (IDX_HEADS * LANES, BF16), (LANES, BF16),
                                  (LANES, F32), (KV_LORA + LANES, BF16)]
    return pl.pallas_call(
        _in_proj_kernel,
        grid=(B, S // tm),
        in_specs=[tok(D), mod, mod] + [full(a) for a in consts],
        out_specs=[tok(n) for n, _ in outs] + [pl.BlockSpec((None, KV_LORA + LANES, tm), lambda b, i: (b, 0, i))],
        out_shape=[jax.ShapeDtypeStruct((B, S, n), dt) for n, dt in outs]
        + [jax.ShapeDtypeStruct((B, KV_LORA + LANES, S), BF16)],
        scratch_shapes=[pltpu.VMEM((8, N_SHIFT_P), F32)],
        compiler_params=_cparams(("arbitrary", "arbitrary")),
        name="in_proj",
    )(x, sc1, sh1, *consts)


def _rwkv_kernel(r_ref, lw_ref, k_ref, v_ref, kk_ref, a_ref, g_ref, rk_ref, lnw_ref, lnb_ref, tri_ref, ones_ref,
                 o_ref, state, *, L):
    c = pl.program_id(1)
    nc = r_ref.shape[0] // L

    @pl.when(c == 0)
    def _():
        state[...] = jnp.zeros_like(state)

    r, lw, k, v, kk, a = r_ref[...], lw_ref[...], k_ref[...], v_ref[...], kk_ref[...], a_ref[...]
    cum = _dot_exact_lhs(tri_ref[...], lw)
    rows = [slice(ci * L, (ci + 1) * L) for ci in range(nc)]
    last = [cum[(ci + 1) * L - 1:(ci + 1) * L, :] for ci in range(nc)]
    cum_last = jnp.concatenate([jnp.broadcast_to(z, (L, z.shape[1])) for z in last], axis=0)
    w_incl = jnp.exp(cum)
    w_inv = jnp.exp(-cum)
    w_rel = jnp.exp(cum_last - cum)
    w_last = [jnp.exp(z) for z in last]
    bvec = kk * a
    at = -kk * jnp.exp(cum - lw)
    rt = r * w_incl
    bt = bvec * w_inv
    kt = k * w_inv
    bh = bvec * w_rel
    kh = k * w_rel
    ti = lax.broadcasted_iota(I32, (L, L), 0)
    tj = lax.broadcasted_iota(I32, (L, L), 1)
    strict = tj < ti
    incl = tj <= ti
    eye = lax.broadcasted_iota(I32, (RWKV_HEAD, RWKV_HEAD), 0) == lax.broadcasted_iota(I32, (RWKV_HEAD, RWKV_HEAD), 1)
    NT = (((1,), (1,)), ((), ()))
    TN = (((0,), (0,)), ((), ()))
    heads = range(N_RWKV_HEADS)
    sls = [slice(h * RWKV_HEAD, (h + 1) * RWKV_HEAD) for h in heads]
    units = [(ci, h) for ci in range(nc) for h in heads]
    idx = range(len(units))
    mm = lambda x, y, dims=(((1,), (0,)), ((), ())): lax.dot_general(
        x.astype(BF16), y.astype(BF16), dims, preferred_element_type=F32)
    at_b, rt_b, bt_b, kt_b, bh_b, kh_b, v_b = [z.astype(BF16) for z in (at, rt, bt, kt, bh, kh, v)]
    cut = lambda z, u: z[rows[u[0]], sls[u[1]]]
    vh = [cut(v_b, u) for u in units]
    ath = [cut(at_b, u) for u in units]
    ar = [jnp.concatenate([cut(at_b, u), cut(rt_b, u)], axis=0) for u in units]
    g_b = [mm(ar[i], cut(bt_b, units[i]), NT) for i in idx]
    g_k = [mm(ar[i], cut(kt_b, units[i]), NT) for i in idx]
    n_ab = [jnp.where(strict, g_b[i][:L], 0.0) for i in idx]
    a_ak = [jnp.where(strict, g_k[i][:L], 0.0) for i in idx]
    a_rb = [jnp.where(incl, g_b[i][L:], 0.0).astype(BF16) for i in idx]
    a_rk = [jnp.where(incl, g_k[i][L:], 0.0) for i in idx]
    akv = [mm(a_ak[i], vh[i]) for i in idx]
    eye_l = jnp.where(ti == tj, 1.0, 0.0)
    tinv = [eye_l + n_ab[i] for i in idx]
    pw = n_ab
    for _ in range(int(math.log2(L)) - 1):
        pw = [mm(pw[i], pw[i]) for i in idx]
        tinv = [tinv[i] + mm(pw[i], tinv[i]) for i in idx]
    tinv = [t.astype(BF16) for t in tinv]
    a_t = [mm(tinv[i], ath[i]).astype(BF16) for i in idx]
    y = [mm(tinv[i], akv[i]).astype(BF16) for i in idx]
    m_c = [jnp.where(eye, w_last[units[i][0]][:, sls[units[i][1]]], 0.0) + mm(a_t[i], cut(bh_b, units[i]), TN)
           for i in idx]
    c_c = [mm(y[i], cut(bh_b, units[i]), TN) + mm(vh[i], cut(kh_b, units[i]), TN) for i in idx]
    q_c = [cut(rt, units[i]) + mm(a_rb[i], a_t[i]) for i in idx]
    o_loc = [mm(a_rb[i], y[i]) + mm(a_rk[i], vh[i]) for i in idx]
    s = [state[h] for h in heads]
    for ci in range(nc):
        for h in heads:
            i = ci * N_RWKV_HEADS + h
            o = o_loc[i] + mm(q_c[i], s[h], NT)
            s[h] = mm(s[h], m_c[i]) + c_c[i]
            mu = jnp.mean(o, axis=-1, keepdims=True)
            d = o - mu
            var = jnp.mean(d * d, axis=-1, keepdims=True)
            o_ref[rows[ci], sls[h]] = d * lax.rsqrt(var + RWKV_GN_EPS)
    for h in heads:
        state[h] = s[h]
    bonus = _dot_exact_rhs(r * k * rk_ref[...], ones_ref[...], nsplit=3) * v
    o_ref[...] = (o_ref[...] * lnw_ref[...] + lnb_ref[...] + bonus) * g_ref[...]


def _rwkv_scan(r, lw, k, v, kk, a, g, r_k, ln_w, ln_b):
    B, S, DR = r.shape
    L = min(L_CHUNK, S)
    lb = min(L * CHUNKS_PER_STEP, S)
    tri = jnp.asarray(np.kron(np.eye(lb // L), np.tril(np.ones((L, L)))), BF16)
    row = lambda z: z.reshape(1, -1)
    tok = pl.BlockSpec((None, lb, DR), lambda b, c: (b, c, 0))
    full = lambda z: pl.BlockSpec(z.shape, lambda b, c: (0,) * z.ndim)
    consts = [row(r_k), row(ln_w), row(ln_b), tri, _head_ones(DR, RWKV_HEAD)]
    return pl.pallas_call(
        functools.partial(_rwkv_kernel, L=L),
        grid=(B, S // lb),
        in_specs=[tok] * 7 + [full(z) for z in consts],
        out_specs=tok,
        out_shape=jax.ShapeDtypeStruct((B, S, DR), F32),
        scratch_shapes=[pltpu.VMEM((N_RWKV_HEADS, RWKV_HEAD, RWKV_HEAD), F32)],
        compiler_params=_cparams(("arbitrary", "arbitrary")),
        name="rwkv_scan",
    )(r, lw, k, v, kk, a, g, *consts)


def _alibi_cols():
    slope = np.asarray([2.0 ** (-8.0 * (h + 1) / N_ATT_HEADS) * LOG2E for h in range(N_ATT_HEADS)], np.float32)
    c_hi = slope.astype(BF16).astype(np.float32)
    c_lo = (slope - c_hi).astype(BF16).astype(np.float32)
    t = np.zeros((N_ATT_HEADS, LANES), np.float32)
    t[:, 1], t[:, 2], t[:, 3], t[:, 4] = LANES * c_hi, LANES * c_lo, c_hi, c_lo
    return jnp.asarray(t.astype(BF16))


def _dsa_kernel(iq_ref, iw_ref, qa_ref, ik_ref, ca_ref, cat_ref, wuv_ref, tril_ref, acol_ref, o_ref, key_ref,
                bias_ref, lg_ref, p_ref, *, topk, q_off, select):
    qi = pl.program_id(1) + q_off
    tq = qa_ref.shape[0]
    sk = ca_ref.shape[0]
    kc = min(KEY_CHUNK, sk)
    chunks = [slice(c * kc, (c + 1) * kc) for c in range(sk // kc)]
    tpos = qi * tq + lax.broadcasted_iota(I32, (1, tq), 1)
    srow = lax.broadcasted_iota(I32, (kc, tq), 0)
    NT = (((1,), (1,)), ((), ()))

    if not select:
        for c, cs in enumerate(chunks):
            bias_ref[cs, :] = jnp.where(srow + c * kc <= tpos, 0.0, NEG_BIG)
    else:
        iw_t = jnp.transpose(iw_ref[...])
        for c, cs in enumerate(chunks):
            ikc = ik_ref[cs, :]
            score = jnp.zeros((kc, tq), F32)
            for h in range(IDX_HEADS):
                s = lax.dot_general(ikc, iq_ref[:, h * LANES:(h + 1) * LANES], NT, preferred_element_type=F32)
                score = score + iw_t[h:h + 1, :] * jnp.maximum(s, 0.0)
            bits = pltpu.bitcast(score + 0.0, I32)
            key = bits ^ ((bits >> 31) & 0x7FFFFFFF)
            key_ref[cs, :] = jnp.where(srow + c * kc <= tpos, key, INT_MIN)
        kcount = jnp.minimum(topk, tpos + 1).astype(F32)

        def sum_keys(x):
            part = jnp.sum(x.reshape(x.shape[0] // SUM_ROWS, SUM_ROWS, x.shape[1]), axis=0)
            return jnp.sum(part, axis=0, keepdims=True)

        def count_ge(cand):
            return sum_keys(jnp.where(key_ref[...] >= cand, 1.0, 0.0))

        thr = jnp.where(count_ge(jnp.zeros((1, tq), I32)) >= kcount, 0, INT_MIN).astype(I32)

        def bit_step(i, thr):
            cand = thr | (1 << (30 - i))
            return jnp.where(count_ge(cand) >= kcount, cand, thr)

        thr = lax.fori_loop(0, 31, bit_step, thr)

        need = kcount - sum_keys(jnp.where(key_ref[...] > thr, 1.0, 0.0))
        carry = jnp.zeros((1, tq), F32)
        for j in range(sk // LANES):
            sl = slice(j * LANES, (j + 1) * LANES)
            key = key_ref[sl, :]
            eq = key == thr
            e = jnp.where(eq, 1.0, 0.0)
            before = jnp.dot(tril_ref[...], e.astype(BF16), preferred_element_type=F32) + carry
            sel = (key > thr) | (eq & (before < need))
            bias_ref[sl, :] = jnp.where(sel, 0.0, NEG_BIG)
            carry = carry + jnp.sum(e, axis=0, keepdims=True)

    outs = []
    for h in range(N_ATT_HEADS):
        q_aug = jnp.concatenate([qa_ref[:, h * KV_LORA:(h + 1) * KV_LORA],
                                 jnp.broadcast_to(acol_ref[h:h + 1, :], (tq, LANES))], axis=-1)
        lg_all = lax.dot_general(ca_ref[...], q_aug, NT, preferred_element_type=F32)
        m = jnp.full((1, tq), -jnp.inf, F32)
        for cs in chunks:
            lg = lg_all[cs, :] + bias_ref[cs, :]
            lg_ref[cs, :] = lg
            m = jnp.maximum(m, jnp.max(lg, axis=0, keepdims=True))
        for cs in chunks:
            p_ref[cs, :] = jnp.exp2(lg_ref[cs, :] - m).astype(BF16)
        pv = jnp.dot(cat_ref[...], p_ref[...], preferred_element_type=F32)
        outs.append((pv[:KV_LORA, :] / pv[KV_LORA:KV_LORA + 1, :]).astype(BF16))
    o_lat_t = jnp.concatenate(outs, axis=0)
    o_ref[...] = lax.dot_general(o_lat_t, wuv_ref[...], (((0,), (0,)), ((), ())), preferred_element_type=F32)


def _dsa_attn(iq, iw, qabs, ik, ca, cat, w_uv):
    B, S, _ = iq.shape
    tq = min(TQ, S)
    topk = min(TOPK_MAX, S // 4)
    wuv_bd = _block_diag(w_uv).astype(BF16)
    tril = jnp.asarray(np.tril(np.ones((LANES, LANES)), -1), BF16)
    acol = _alibi_cols()
    full = lambda z: pl.BlockSpec(z.shape, lambda b, i: (0,) * z.ndim)
    nq = 1
    outs = []
    for q_off in range(0, S // tq, nq):
        sk = (q_off + nq) * tq
        tok = lambda n, q_off=q_off: pl.BlockSpec((None, tq, n), lambda b, i: (b, i + q_off, 0))
        seq = lambda n, sk=sk: pl.BlockSpec((None, sk, n), lambda b, i: (b, 0, 0))
        outs.append(pl.pallas_call(
            functools.partial(_dsa_kernel, topk=topk, q_off=q_off, select=sk > topk),
            grid=(B, nq),
            in_specs=[tok(IDX_HEADS * LANES), tok(LANES), tok(N_ATT_HEADS * KV_LORA), seq(LANES), seq(KV_LORA + LANES),
                      pl.BlockSpec((None, KV_LORA + LANES, sk), lambda b, i: (b, 0, 0)),
                      full(wuv_bd), full(tril), full(acol)],
            out_specs=pl.BlockSpec((None, tq, D_ATT), lambda b, i: (b, i, 0)),
            out_shape=jax.ShapeDtypeStruct((B, nq * tq, D_ATT), F32),
            scratch_shapes=[pltpu.VMEM((sk, tq), I32), pltpu.VMEM((sk, tq), F32), pltpu.VMEM((sk, tq), F32),
                            pltpu.VMEM((sk, tq), BF16)],
            compiler_params=_cparams(("arbitrary", "arbitrary")),
            name=f"dsa_attn_k{sk}",
        )(iq, iw, qabs, ik, ca, cat, wuv_bd, tril, acol))
    return jnp.concatenate(outs, axis=1)


def _layernorm_rows(y, g, b):
    mu = jnp.mean(y, axis=-1, keepdims=True)
    d = y - mu
    var = jnp.mean(d * d, axis=-1, keepdims=True)
    return d * lax.rsqrt(var + 1e-5) * g + b


def _mix_kernel(orw_ref, ods_ref, x_ref, gt_ref, sc_ref, sh_ref, wtop_ref, wbot_ref, g_ref, b_ref, wr_ref, br_ref,
                tril_ref, triu_ref, x1_o, u2_o, route_o, tbl_o, cnt_o, carry, *, alpha):
    first = (pl.program_id(0) == 0) & (pl.program_id(1) == 0)
    tm = x_ref.shape[0]

    @pl.when(first)
    def _():
        carry[...] = jnp.zeros_like(carry)

    mix = _bdot(orw_ref[...], wtop_ref[...]) + _bdot(ods_ref[...], wbot_ref[...])
    x1 = _layernorm_rows(alpha * x_ref[...] + (1.0 + gt_ref[...]) * mix, g_ref[...], b_ref[...])
    x1_o[...] = x1
    u2 = x1 * (1.0 + sc_ref[...]) + sh_ref[...]
    u2_o[...] = u2

    lg = _dot3(u2, wr_ref[...]) + br_ref[...]
    lane = lax.broadcasted_iota(I32, (tm, LANES), 1)
    lane_f = lane.astype(F32)
    idxs, vals = [], []
    for _ in range(TOP_K_EXPERTS):
        m = jnp.max(lg, axis=-1, keepdims=True)
        idx = jnp.min(jnp.where(lg == m, lane_f, float(LANES)), axis=-1, keepdims=True).astype(I32)
        idxs.append(idx)
        vals.append(m)
        lg = jnp.where(lane == idx, -jnp.inf, lg)
    es = [jnp.exp(v - vals[0]) for v in vals]
    den = es[0] + es[1] + es[2] + es[3]
    hot = jnp.zeros((tm, LANES), F32)
    for idx in idxs:
        hot = hot + jnp.where(lane == idx, 1.0, 0.0)
    before = jnp.dot(tril_ref[...], hot.astype(BF16), preferred_element_type=F32)
    cnt = jnp.sum(hot, axis=0, keepdims=True)
    n_run = jnp.floor((cnt + (RUN_ALIGN - 1)) * (1.0 / RUN_ALIGN))
    n_stage = jnp.floor((n_run * RUN_ALIGN + (STAGE_ALIGN - 1)) * (1.0 / STAGE_ALIGN))
    off = jnp.dot(jnp.broadcast_to(n_stage, (SUBLANES, LANES)).astype(BF16), triu_ref[...],
                  preferred_element_type=F32)[0:1, :] * STAGE_ALIGN
    where_in_stage = off + before
    route = jnp.zeros((tm, LANES), F32)
    for k in range(TOP_K_EXPERTS):
        spos = jnp.sum(jnp.where(lane == idxs[k], where_in_stage, 0.0), axis=-1, keepdims=True)
        route = jnp.where(lane == k, idxs[k].astype(F32), route)
        route = jnp.where(lane == TOP_K_EXPERTS + k, es[k] / den, route)
        route = jnp.where(lane == 2 * TOP_K_EXPERTS + k, spos, route)
    route_o[...] = route
    sub = lax.broadcasted_iota(I32, (SUBLANES, LANES), 0)
    tbl_o[...] = jnp.where(sub == 0, n_run, jnp.where(sub == 1, off, jnp.where(sub == 2, carry[0:1, :], 0.0)))
    carry[0:1, :] = carry[0:1, :] + n_run * RUN_ALIGN
    cnt_o[...] = carry[...]


def _mix_out(o_rwkv, o_dsa, x, gt1, sc2, sh2, w_out, ln_g, ln_b, w_router, b_router):
    B, S, D = x.shape
    tm = min(TM_PROJ, S)
    alpha = 2.0 ** 0.25
    wtop = w_out[:D_RWKV].astype(BF16)
    wbot = w_out[D_RWKV:].astype(BF16)
    wr_p = jnp.pad(w_router, ((0, 0), (0, LANES - N_EXPERTS)))
    br_p = jnp.pad(b_router.reshape(1, -1), ((0, 0), (0, LANES - N_EXPERTS)), constant_values=NEG_BIG)
    tril = jnp.asarray(np.tril(np.ones((tm, tm)), -1), BF16)
    triu = jnp.asarray(np.triu(np.ones((LANES, LANES)), 1), BF16)
    row = lambda v: v.reshape(1, -1)
    tok = lambda n: pl.BlockSpec((None, tm, n), lambda b, i: (b, i, 0))
    mod = pl.BlockSpec((None, 1, D), lambda b, i: (b, 0, 0))
    full = lambda a: pl.BlockSpec(a.shape, lambda b, i: (0,) * a.ndim)
    consts = [wtop, wbot, row(ln_g), row(ln_b), wr_p, br_p, tril, triu]
    return pl.pallas_call(
        functools.partial(_mix_kernel, alpha=alpha),
        grid=(B, S // tm),
        in_specs=[tok(D_RWKV), tok(D_ATT), tok(D), mod, mod, mod] + [full(a) for a in consts],
        out_specs=[tok(D), tok(D), tok(LANES), pl.BlockSpec((None, None, SUBLANES, LANES), lambda b, i: (b, i, 0, 0)),
                   pl.BlockSpec((SUBLANES, LANES), lambda b, i: (0, 0))],
        out_shape=[jax.ShapeDtypeStruct((B, S, D), F32), jax.ShapeDtypeStruct((B, S, D), F32),
                   jax.ShapeDtypeStruct((B, S, LANES), F32),
                   jax.ShapeDtypeStruct((B, S // tm, SUBLANES, LANES), F32),
                   jax.ShapeDtypeStruct((SUBLANES, LANES), F32)],
        scratch_shapes=[pltpu.VMEM((8, LANES), F32)],
        compiler_params=_cparams(("arbitrary", "arbitrary")),
        name="mix_out",
    )(o_rwkv, o_dsa, x, gt1, sc2, sh2, *consts)


RUN_ALIGN = SUBLANES
STAGE_ALIGN = 16
STAGE_ROWS = 1536
STAGE_CHUNK = 256


def _run_copies(nrun_ref, tile, copy_of):
    for e in range(N_EXPERTS):
        pieces = (nrun_ref[tile * N_EXPERTS + e] * RUN_ALIGN + STAGE_ALIGN - 1) // STAGE_ALIGN

        def piece(j, carry, e=e):
            copy_of(e, j).start(priority=e % 2)
            return carry

        lax.fori_loop(0, pieces, piece, 0)


def _dispatch_kernel(nrun_ref, off_ref, start_ref, npiece_ref, nstage_ref, zs_ref, zn_ref, tail_ref,
                     u_ref, route_ref, xs_out, stag, zeros, sem, zsem):
    i = pl.program_id(0)
    n = pl.num_programs(0)
    tm = u_ref.shape[0]
    bm = zeros.shape[0]
    slot = i % 2

    @pl.when(i == 0)
    def _():
        zeros[...] = jnp.zeros_like(zeros)
        fills = []
        for e in range(N_EXPERTS):
            for b in range(bm.bit_length()):
                rows = RUN_ALIGN << b
                if rows > bm:
                    break
                done = (zn_ref[e] >> (b + 1)) << (b + 1)
                dst = pl.multiple_of(zs_ref[e] + done * RUN_ALIGN, RUN_ALIGN)
                fills.append(((zn_ref[e] >> b) & 1 == 1,
                              pltpu.make_async_copy(zeros.at[pl.ds(0, rows)], xs_out.at[pl.ds(dst, rows)], zsem)))
        for pred, cp in fills:
            pl.when(pred)(cp.start)

        def tail_copy(j):
            return pltpu.make_async_copy(zeros, xs_out.at[pl.ds(pl.multiple_of(j * bm, bm), bm)], zsem)

        lax.fori_loop(tail_ref[0], tail_ref[1], lambda j, c: (tail_copy(j).start(), c)[1], 0)
        for pred, cp in fills:
            pl.when(pred)(cp.wait)
        lax.fori_loop(tail_ref[0], tail_ref[1], lambda j, c: (tail_copy(j).wait(), c)[1], 0)

    route_t = jnp.transpose(route_ref[...])
    spos = [route_t[2 * TOP_K_EXPERTS + k:2 * TOP_K_EXPERTS + k + 1, :].astype(I32) for k in range(TOP_K_EXPERTS)]
    ub = u_ref[...].astype(BF16)
    srow = lax.broadcasted_iota(I32, (STAGE_CHUNK, tm), 0)
    for c in range(STAGE_ROWS // STAGE_CHUNK):
        @pl.when(c * STAGE_CHUNK < nstage_ref[i])
        def _(c=c):
            rows = srow + c * STAGE_CHUNK
            sel = (rows == spos[0]) | (rows == spos[1]) | (rows == spos[2]) | (rows == spos[3])
            stag[slot, c * STAGE_CHUNK:(c + 1) * STAGE_CHUNK, :] = jnp.dot(
                jnp.where(sel, 1.0, 0.0).astype(BF16), ub, preferred_element_type=F32)

    def piece_copy(s, tile):
        def copy_of(e, j):
            src = pl.multiple_of(off_ref[tile * N_EXPERTS + e] + j * STAGE_ALIGN, STAGE_ALIGN)
            dst = pl.multiple_of(start_ref[tile * N_EXPERTS + e] + j * STAGE_ALIGN, RUN_ALIGN)
            return pltpu.make_async_copy(stag.at[s, pl.ds(src, STAGE_ALIGN)], xs_out.at[pl.ds(dst, STAGE_ALIGN)], sem)
        return copy_of

    def drain(tile):
        def w(j, carry):
            pltpu.make_async_copy(stag.at[0, pl.ds(0, STAGE_ALIGN)], xs_out.at[pl.ds(0, STAGE_ALIGN)], sem).wait()
            return carry
        lax.fori_loop(0, npiece_ref[tile], w, 0)

    @pl.when(i > 0)
    def _():
        drain(i - 1)

    _run_copies(nrun_ref, i, piece_copy(slot, i))

    @pl.when(i == n - 1)
    def _():
        drain(i)


def _moe_dispatch(u2, route, tables, n_rows):
    T, D = u2.shape
    tm = TM_ROUTE
    bm = BM_EXPERT
    return pl.pallas_call(
        _dispatch_kernel,
        grid_spec=pltpu.PrefetchScalarGridSpec(
            num_scalar_prefetch=len(tables),
            grid=(T // tm,),
            in_specs=[pl.BlockSpec((tm, D), lambda i, *_: (i, 0)),
                      pl.BlockSpec((tm, LANES), lambda i, *_: (i, 0))],
            out_specs=pl.BlockSpec(memory_space=pl.ANY),
            scratch_shapes=[pltpu.VMEM((2, STAGE_ROWS, D), F32), pltpu.VMEM((bm, D), F32),
                            pltpu.SemaphoreType.DMA(()), pltpu.SemaphoreType.DMA(())],
        ),
        out_shape=jax.ShapeDtypeStruct((n_rows, D), F32),
        compiler_params=_cparams(("arbitrary",)),
        name="moe_dispatch",
    )(*tables, u2, route)


GU_GROUP = 2 * LANES


def _deinterleave_perm():
    p = np.zeros((GU_GROUP, GU_GROUP), np.float32)
    l = np.arange(LANES)
    p[2 * l, l] = 1.0
    p[2 * l + 1, LANES + l] = 1.0
    return jnp.asarray(p, BF16)


def _expert_kernel(be_ref, nb_ref, valid_ref, xs_ref, wgu_hbm, bgu_ref, wd_hbm, bd_ref, perm_ref, ys_ref,
                   wg_buf, wd_buf, wp, wdb, sem):
    i = pl.program_id(0)
    bm = xs_ref.shape[0]
    e = be_ref[i]
    used = i < nb_ref[0]
    new_expert = (i == 0) | (e != be_ref[jnp.maximum(i - 1, 0)])
    n_groups = wp.shape[1] // GU_GROUP
    n_experts = wgu_hbm.shape[0]

    def fetch(ex):
        return (pltpu.make_async_copy(wgu_hbm.at[ex], wg_buf, sem.at[0]),
                pltpu.make_async_copy(wd_hbm.at[ex], wd_buf, sem.at[1]))

    @pl.when(used & new_expert)
    def _():
        @pl.when(i == 0)
        def _():
            for cp in fetch(e):
                cp.start()

        for cp in fetch(e):
            cp.wait()
        for j in range(n_groups):
            sl = slice(j * GU_GROUP, (j + 1) * GU_GROUP)
            wp[:, sl] = jnp.dot(wg_buf[:, sl].astype(BF16), perm_ref[...], preferred_element_type=F32).astype(BF16)
        wdb[...] = wd_buf[...].astype(BF16)

        @pl.when(e + 1 < n_experts)
        def _():
            for cp in fetch(e + 1):
                cp.start()

    def compute(m):
        xb = xs_ref[:m, :].astype(BF16)
        gu = jnp.dot(xb, wp[...], preferred_element_type=F32) + bgu_ref[...]
        hs = []
        for j in range(n_groups):
            gate = jnp.minimum(gu[:, j * GU_GROUP:j * GU_GROUP + LANES], SWIGLU_LIMIT)
            up = jnp.clip(gu[:, j * GU_GROUP + LANES:(j + 1) * GU_GROUP], -SWIGLU_LIMIT, SWIGLU_LIMIT)
            hs.append(((up + 1.0) * (gate * _sigmoid(gate * SWIGLU_ALPHA))).astype(BF16))
        h = jnp.concatenate(hs, axis=-1)
        ys_ref[:m, :] = jnp.dot(h, wdb[...], preferred_element_type=F32) + bd_ref[...]

    sizes = (bm, bm // 2, bm // 4)
    for m, smaller in zip(sizes, sizes[1:] + (0,)):
        @pl.when(used & (valid_ref[i] <= m) & (valid_ref[i] > smaller))
        def _(m=m):
            compute(m)
            if m < bm:
                ys_ref[m:, :] = jnp.zeros((bm - m, ys_ref.shape[1]), F32)

    @pl.when(used & (valid_ref[i] <= 0))
    def _():
        ys_ref[...] = jnp.zeros_like(ys_ref)

    @pl.when(jnp.logical_not(used))
    def _():
        ys_ref[...] = jnp.zeros_like(ys_ref)


def _moe_experts(xs, block_e, n_used, valid, w_gu, b_gu_p, w_dn, b_dn):
    n_rows, D = xs.shape
    E, _, F2 = w_gu.shape
    bm = BM_EXPERT
    n_blocks = n_rows // bm
    perm = _deinterleave_perm()
    wspec = lambda shp: pl.BlockSpec((None,) + shp, lambda i, be, nb, va: (be[i], 0, 0))
    hbm = pl.BlockSpec(memory_space=pl.ANY)
    return pl.pallas_call(
        _expert_kernel,
        grid_spec=pltpu.PrefetchScalarGridSpec(
            num_scalar_prefetch=3,
            grid=(n_blocks,),
            in_specs=[pl.BlockSpec((bm, D), lambda i, be, nb, va: (jnp.minimum(i, nb[0] - 1), 0)),
                      hbm, wspec((1, F2)), hbm, wspec((1, D)),
                      pl.BlockSpec(perm.shape, lambda i, be, nb, va: (0, 0))],
            out_specs=pl.BlockSpec((bm, D), lambda i, be, nb, va: (i, 0)),
            scratch_shapes=[pltpu.VMEM((D, F2), F32), pltpu.VMEM((F2 // 2, D), F32),
                            pltpu.VMEM((D, F2), BF16), pltpu.VMEM((F2 // 2, D), BF16),
                            pltpu.SemaphoreType.DMA((2,))],
        ),
        out_shape=jax.ShapeDtypeStruct((n_rows, D), F32),
        compiler_params=pltpu.CompilerParams(dimension_semantics=("arbitrary",), vmem_limit_bytes=VMEM_LIMIT_EXPERTS),
        name="moe_experts",
    )(block_e, n_used, valid, xs, w_gu, b_gu_p, w_dn, b_dn, perm)


def _combine_kernel(nrun_ref, off_ref, start_ref, npiece_ref, nstage_ref, ys_ref, x1_ref, route_ref, gt_ref, g_ref,
                    b_ref, o_ref, stag, sem, *, alpha):
    i = pl.program_id(0)
    n = pl.num_programs(0)
    tm = x1_ref.shape[0]
    slot = i % 2

    def gather(tile, s):
        def copy_of(e, j):
            src = pl.multiple_of(start_ref[tile * N_EXPERTS + e] + j * STAGE_ALIGN, RUN_ALIGN)
            dst = pl.multiple_of(off_ref[tile * N_EXPERTS + e] + j * STAGE_ALIGN, STAGE_ALIGN)
            return pltpu.make_async_copy(ys_ref.at[pl.ds(src, STAGE_ALIGN)], stag.at[s, pl.ds(dst, STAGE_ALIGN)],
                                         sem.at[s])
        _run_copies(nrun_ref, tile, copy_of)

    @pl.when(i == 0)
    def _():
        stag[...] = jnp.zeros_like(stag)
        gather(0, 0)

    @pl.when(i + 1 < n)
    def _():
        gather(i + 1, 1 - slot)

    def w(j, carry):
        pltpu.make_async_copy(ys_ref.at[pl.ds(0, STAGE_ALIGN)], stag.at[slot, pl.ds(0, STAGE_ALIGN)],
                              sem.at[slot]).wait()
        return carry
    lax.fori_loop(0, npiece_ref[i], w, 0)

    route = route_ref[...]
    wide = lambda col: jnp.broadcast_to(col, (tm, STAGE_CHUNK))
    spos = [wide(route[:, 2 * TOP_K_EXPERTS + k:2 * TOP_K_EXPERTS + k + 1].astype(I32)) for k in range(TOP_K_EXPERTS)]
    gate = [wide(route[:, TOP_K_EXPERTS + k:TOP_K_EXPERTS + k + 1]) for k in range(TOP_K_EXPERTS)]
    scol = lax.broadcasted_iota(I32, (tm, STAGE_CHUNK), 1)
    acc_ref = o_ref
    acc_ref[...] = jnp.zeros_like(acc_ref)
    for c in range(STAGE_ROWS // STAGE_CHUNK):
        @pl.when(c * STAGE_CHUNK < nstage_ref[i])
        def _(c=c):
            cols = scol + c * STAGE_CHUNK
            wgt = jnp.zeros((tm, STAGE_CHUNK), F32)
            for k in range(TOP_K_EXPERTS):
                wgt = wgt + jnp.where(cols == spos[k], gate[k], 0.0)
            acc_ref[...] += jnp.dot(wgt.astype(BF16), stag[slot, c * STAGE_CHUNK:(c + 1) * STAGE_CHUNK, :].astype(BF16),
                                    preferred_element_type=F32)
    o_ref[...] = _layernorm_rows(alpha * x1_ref[...] + (1.0 + gt_ref[...]) * acc_ref[...], g_ref[...], b_ref[...])


def _moe_combine(ys, tables, x1, route, gt2, ln_g, ln_b, tiles_per_batch):
    T, D = x1.shape
    tm = TM_ROUTE
    row = lambda v: v.reshape(1, -1)
    return pl.pallas_call(
        functools.partial(_combine_kernel, alpha=2.0 ** 0.25),
        grid_spec=pltpu.PrefetchScalarGridSpec(
            num_scalar_prefetch=len(tables),
            grid=(T // tm,),
            in_specs=[pl.BlockSpec(memory_space=pl.ANY),
                      pl.BlockSpec((tm, D), lambda i, *_: (i, 0)),
                      pl.BlockSpec((tm, LANES), lambda i, *_: (i, 0)),
                      pl.BlockSpec((None, 1, D), lambda i, *_: (i // tiles_per_batch, 0, 0)),
                      pl.BlockSpec((1, D), lambda i, *_: (0, 0)),
                      pl.BlockSpec((1, D), lambda i, *_: (0, 0))],
            out_specs=pl.BlockSpec((tm, D), lambda i, *_: (i, 0)),
            scratch_shapes=[pltpu.VMEM((2, STAGE_ROWS, D), F32), pltpu.SemaphoreType.DMA((2,))],
        ),
        out_shape=jax.ShapeDtypeStruct((T, D), F32),
        compiler_params=_cparams(("arbitrary",)),
        name="moe_combine",
    )(*tables, ys, x1, route, gt2, row(ln_g), row(ln_b))


def _moe_and_norm(x1, u2, route, tbl, totals, gt2, w_gu, b_gu, w_dn, b_dn, ln_g, ln_b):
    B, S, D = x1.shape
    T = B * S
    bm = BM_EXPERT
    tm = TM_ROUTE
    assert T % tm == 0 and TM_PROJ == tm and STAGE_ROWS >= tm * TOP_K_EXPERTS + N_EXPERTS * (STAGE_ALIGN - 1)
    n_tiles = T // tm
    max_rows = T * TOP_K_EXPERTS + n_tiles * N_EXPERTS * (RUN_ALIGN - 1) + N_EXPERTS * STAGE_ALIGN
    n_blocks = -(-max_rows // bm) + N_EXPERTS
    tot = totals[0, :N_EXPERTS].astype(I32)
    padded = (tot + STAGE_ALIGN + bm - 1) // bm * bm
    pad_ends = jnp.cumsum(padded)
    pad_starts = pad_ends - padded
    t3 = tbl.reshape(n_tiles, SUBLANES, LANES)[:, :, :N_EXPERTS].astype(I32)
    nrun, off, base = t3[:, 0, :], t3[:, 1, :], t3[:, 2, :]
    start = pad_starts[None, :] + base
    pieces = (nrun * RUN_ALIGN + STAGE_ALIGN - 1) // STAGE_ALIGN
    npiece = jnp.sum(pieces, axis=1)
    nstage = jnp.sum(pieces, axis=1) * STAGE_ALIGN
    flat = lambda z: z.reshape(-1).astype(I32)
    run_tables = [flat(nrun), flat(off), flat(start), flat(npiece), flat(nstage)]
    zs = pad_starts + tot
    zn = (pad_ends - zs) // RUN_ALIGN
    n_used = (pad_ends[-1:] // bm).astype(I32)
    tail = jnp.concatenate([n_used, jnp.full((1,), n_blocks, I32)])
    blk_row = jnp.arange(n_blocks, dtype=I32) * bm
    block_e = jnp.minimum(jnp.sum((blk_row[:, None] >= pad_ends[None, :]).astype(I32), axis=1), N_EXPERTS - 1)
    E, F2 = b_gu.shape
    b_gu_p = b_gu.reshape(E, F2 // GU_GROUP, LANES, 2).transpose(0, 1, 3, 2).reshape(E, 1, F2)
    route2 = route.reshape(T, LANES)
    xs = _moe_dispatch(u2.reshape(T, D), route2, run_tables + [flat(zs), flat(zn), tail], n_blocks * bm)
    valid = jnp.clip(zs[block_e] - blk_row, 0, bm)
    ys = _moe_experts(xs, block_e, n_used, valid, w_gu, b_gu_p, w_dn, b_dn[:, None, :])
    out = _moe_combine(ys, run_tables, x1.reshape(T, D), route2, gt2, ln_g, ln_b, S // tm)
    return out.reshape(B, S, D)


def kernel(x, c, w_ada, b_ada, w_in, shift_mu, rwkv_w0, rwkv_w2, rwkv_a0, rwkv_a2, rwkv_g2, rwkv_k_k, rwkv_k_a, rwkv_r_k, rwkv_ln_w, rwkv_ln_b, mla_q_norm, mla_w_q_up, mla_kv_norm, mla_w_uk, mla_w_uv, idx_w_q, idx_ln_g, idx_ln_b, w_out, ln1_g, ln1_b, w_router, b_router, w_gu, b_gu, w_dn, b_dn, ln2_g, ln2_b):
    depth = w_ada.shape[0]
    assert depth == 1, "DeepNorm constants below are for a single layer"
    l = 0
    mod = _ada_mod(c, w_ada[l], b_ada[l])
    sh1, sc1, gt1, sh2, sc2, gt2 = [m[:, None, :] for m in jnp.split(mod, 6, axis=-1)]
    r, lw, k, v, kk, a, g, qabs, iq, ik, iw, cl, clt = _in_proj(
        x, sc1, sh1, w_in[l], shift_mu[l], rwkv_w0[l], rwkv_w2[l], rwkv_a0[l], rwkv_a2[l], rwkv_g2[l], rwkv_k_k[l],
        rwkv_k_a[l], mla_q_norm[l], mla_w_q_up[l], mla_w_uk[l], mla_kv_norm[l], idx_w_q[l], idx_ln_g[l], idx_ln_b[l])
    o_rwkv = _rwkv_scan(r, lw, k, v, kk, a, g, rwkv_r_k[l], rwkv_ln_w[l], rwkv_ln_b[l])
    o_dsa = _dsa_attn(iq, iw, qabs, ik, cl, clt, mla_w_uv[l])
    x1, u2, route, tbl, totals = _mix_out(o_rwkv, o_dsa, x, gt1, sc2, sh2, w_out[l], ln1_g[l], ln1_b[l],
                                          w_router[l], b_router[l])
    return _moe_and_norm(x1, u2, route, tbl, totals, gt2, w_gu[l], b_gu[l], w_dn[l], b_dn[l], ln2_g[l], ln2_b[l])
```

```python
import functools
import math

import jax
import jax.numpy as jnp
import numpy as np
from jax import lax
from jax.experimental import pallas as pl
from jax.experimental.pallas import tpu as pltpu

F32 = jnp.float32
BF16 = jnp.bfloat16
I32 = jnp.int32

RWKV_HEAD = 64
N_RWKV_HEADS = 8
D_RWKV = RWKV_HEAD * N_RWKV_HEADS
RWKV_GN_EPS = 64e-5
ATT_HEAD = 64
N_ATT_HEADS = 8
D_ATT = ATT_HEAD * N_ATT_HEADS
KV_LORA = 128
IDX_HEADS = 8
IDX_DIM = 64
TOPK_MAX = 256
N_EXPERTS = 32
TOP_K_EXPERTS = 4
SWIGLU_LIMIT = 7.0
SWIGLU_ALPHA = 1.702
NEG_BIG = -1e30
LOG2E = 1.4426950408889634
INT_MIN = -(2 ** 31)

LANES = 128
SUBLANES = 8
VMEM_LIMIT = 48 * 1024 * 1024
VMEM_LIMIT_EXPERTS = 58 * 1024 * 1024

TM_IN = 512
TM_PROJ = 256
L_CHUNK = 64
CHUNKS_PER_STEP = 4
TQ = 256
KEY_CHUNK = 128
SUM_ROWS = 128
TM_ROUTE = 256
BM_EXPERT = 1024

_SEG = (("r", 512, 512), ("k", 512, 512), ("v", 512, 512), ("w", 64, 128), ("a", 64, 128), ("g", 128, 128),
        ("q", 256, 256), ("kv", 128, 128), ("ik", 64, 128), ("iw", 8, 128))
N_SHIFT_P = 512 * 3 + 128 * 3
N_IN_P = sum(s[2] for s in _SEG)


def _cparams(sem):
    return pltpu.CompilerParams(dimension_semantics=sem, vmem_limit_bytes=VMEM_LIMIT)


def _bdot(a, b):
    return jnp.dot(a.astype(BF16), b.astype(BF16), preferred_element_type=F32)


def _split2(a):
    hi = a.astype(BF16)
    lo = (a - hi.astype(F32)).astype(BF16)
    return hi, lo


def _split3(a):
    hi = a.astype(BF16)
    r1 = a - hi.astype(F32)
    mid = r1.astype(BF16)
    lo = (r1 - mid.astype(F32)).astype(BF16)
    return hi, mid, lo


def _dot3(a, b, dims=(((1,), (0,)), ((), ()))):
    ah, al = _split2(a)
    bh, bl = _split2(b)
    d = functools.partial(lax.dot_general, dimension_numbers=dims, preferred_element_type=F32)
    return d(ah, bh) + (d(ah, bl) + d(al, bh))


def _dot_exact_rhs(a, b_exact, nsplit=3):
    parts = _split3(a) if nsplit == 3 else _split2(a)
    acc = None
    for p in parts[::-1]:
        t = jnp.dot(p, b_exact, preferred_element_type=F32)
        acc = t if acc is None else acc + t
    return acc


def _dot_exact_lhs(a_exact, b, nsplit=3):
    parts = _split3(b) if nsplit == 3 else _split2(b)
    acc = None
    for p in parts[::-1]:
        t = jnp.dot(a_exact, p, preferred_element_type=F32)
        acc = t if acc is None else acc + t
    return acc


def _sigmoid(x):
    return 1.0 / (1.0 + jnp.exp(-x))


def _softplus(x):
    return jnp.maximum(x, 0.0) + jnp.log(1.0 + jnp.exp(-jnp.abs(x)))


def _ada_kernel(c_ref, w_ref, b_ref, o_ref):
    c = c_ref[...]
    o_ref[...] = _dot3(c * _sigmoid(c), w_ref[...]) + b_ref[...]


def _ada_mod(c, w_ada, b_ada):
    B, D = c.shape
    N = w_ada.shape[1]
    tn = 1024
    return pl.pallas_call(
        _ada_kernel,
        grid=(N // tn,),
        in_specs=[pl.BlockSpec((B, D), lambda j: (0, 0)),
                  pl.BlockSpec((D, tn), lambda j: (0, j)),
                  pl.BlockSpec((1, tn), lambda j: (0, j))],
        out_specs=pl.BlockSpec((B, tn), lambda j: (0, j)),
        out_shape=jax.ShapeDtypeStruct((B, N), F32),
        compiler_params=_cparams(("arbitrary",)),
        name="ada_mod",
    )(c, w_ada, b_ada.reshape(1, N))


def _in_proj_kernel(x_ref, sc_ref, sh_ref, win_ref, mu_ref, w0_ref, w2_ref, a0_ref, a2_ref, g2_ref, kk_ref, ka_ref,
                    ones_ref, qn_ref, wq_ref, wuk_ref, kvn_ref, wiq_ref, ig_ref, ib_ref,
                    r_o, lw_o, k_o, v_o, kkn_o, a_o, g_o, qabs_o, iq_o, ik_o, iw_o, cl_o, clt_o, carry):
    i = pl.program_id(1)
    tm = x_ref.shape[0]

    @pl.when(i == 0)
    def _():
        carry[...] = jnp.zeros_like(carry)

    u = x_ref[...] * (1.0 + sc_ref[...]) + sh_ref[...]
    p = _bdot(u, win_ref[...])
    ps = p[:, :N_SHIFT_P]
    rows = lax.broadcasted_iota(I32, (tm, 1), 0)
    prev = jnp.where(rows == 0, carry[0:1, :], pltpu.roll(ps, 1, 0))
    carry[0:1, :] = ps[tm - 1:tm, :]
    ps = ps + mu_ref[...] * (prev - ps)

    pr, pk, pv = ps[:, 0:512], ps[:, 512:1024], ps[:, 1024:1536]
    pw, pa, pg = ps[:, 1536:1664], ps[:, 1664:1792], ps[:, 1792:1920]
    w_log = -_softplus(-(w0_ref[...] + _dot3(jnp.tanh(pw), w2_ref[...]))) - 0.5
    lw_o[...] = -jnp.exp(w_log)
    a = _sigmoid(a0_ref[...] + _dot3(pa, a2_ref[...]))
    g_o[...] = _dot3(_sigmoid(pg), g2_ref[...])
    kk = pk * kk_ref[...]
    ssq = _dot_exact_rhs(kk * kk, ones_ref[...], nsplit=2)
    kkn_o[...] = kk / jnp.maximum(jnp.sqrt(ssq), 1e-12)
    k_o[...] = pk * (1.0 + (a - 1.0) * ka_ref[...])
    r_o[...] = pr
    v_o[...] = pv
    a_o[...] = a

    pq, pkv = p[:, 1920:2176], p[:, 2176:2304]
    pik, piw = p[:, 2304:2432], p[:, 2432:2560]
    q_lat = pq * lax.rsqrt(jnp.mean(pq * pq, axis=-1, keepdims=True) + 1e-6) * qn_ref[...]
    q = _bdot(q_lat, wq_ref[...])
    qabs_o[...] = (_bdot(q, wuk_ref[...]) * (ATT_HEAD ** -0.5 * LOG2E)).astype(BF16)
    c_lat = pkv * lax.rsqrt(jnp.mean(pkv * pkv, axis=-1, keepdims=True) + 1e-6) * kvn_ref[...]
    spos = i * tm + rows
    s_hi = (spos >> 7).astype(F32)
    s_lo = (spos & (LANES - 1)).astype(F32)
    lane_t = lax.broadcasted_iota(I32, (1, LANES), 1)
    extra = jnp.where(lane_t == 0, 1.0, jnp.where((lane_t == 1) | (lane_t == 2), s_hi,
                                                  jnp.where((lane_t == 3) | (lane_t == 4), s_lo, 0.0)))
    c_aug = jnp.concatenate([c_lat, extra], axis=-1)
    cl_o[...] = c_aug.astype(BF16)
    clt_o[...] = jnp.transpose(c_aug).astype(BF16)
    iq_o[...] = (_bdot(q_lat, wiq_ref[...]) * (IDX_DIM ** -0.5)).astype(BF16)
    lane = lax.broadcasted_iota(I32, (1, LANES), 1)
    valid = lane < IDX_DIM
    mu = jnp.sum(pik, axis=-1, keepdims=True) * (1.0 / IDX_DIM)
    dlt = jnp.where(valid, pik - mu, 0.0)
    var = jnp.sum(dlt * dlt, axis=-1, keepdims=True) * (1.0 / IDX_DIM)
    ik_o[...] = jnp.where(valid, dlt * lax.rsqrt(var + 1e-5) * ig_ref[...] + ib_ref[...], 0.0).astype(BF16)
    iw_o[...] = piw * (IDX_HEADS ** -0.5)


def _pad_cols(w, widths):
    parts, o = [], 0
    for true, padded in widths:
        seg = w[..., o:o + true]
        if padded > true:
            seg = jnp.pad(seg, [(0, 0)] * (w.ndim - 1) + [(0, padded - true)])
        parts.append(seg)
        o += true
    return jnp.concatenate(parts, axis=-1)


def _pad_rows(w, rows):
    return jnp.pad(w, ((0, rows - w.shape[0]), (0, 0)))


def _block_diag(blocks):
    H, a, b = blocks.shape
    eye = jnp.eye(H, dtype=blocks.dtype)
    return (eye[:, None, :, None] * blocks[:, :, None, :]).reshape(H * a, H * b)


def _head_ones(n, head):
    idx = np.arange(n) // head
    return jnp.asarray(idx[:, None] == idx[None, :], BF16)


def _in_proj(x, sc1, sh1, w_in, shift_mu, w0, w2, a0, a2, g2, k_k, k_a, q_norm, w_q_up, w_uk, kv_norm, idx_w_q,
             idx_ln_g, idx_ln_b):
    B, S, D = x.shape
    tm = min(TM_IN, S)
    widths = tuple((s[1], s[2]) for s in _SEG)
    win_p = _pad_cols(w_in, widths).astype(BF16)
    mu_p = _pad_cols(shift_mu.reshape(1, -1), widths[:6])
    w2_p = _pad_rows(w2, LANES)
    a2_p = _pad_rows(a2, LANES)
    wuk_bd = _block_diag(w_uk).astype(BF16)
    wiq_p = _pad_cols(idx_w_q, ((IDX_DIM, LANES),) * IDX_HEADS).astype(BF16)
    ig_p = _pad_cols(idx_ln_g.reshape(1, -1), ((IDX_DIM, LANES),))
    ib_p = _pad_cols(idx_ln_b.reshape(1, -1), ((IDX_DIM, LANES),))
    row = lambda v: v.reshape(1, -1)
    tok = lambda n: pl.BlockSpec((None, tm, n), lambda b, i: (b, i, 0))
    mod = pl.BlockSpec((None, 1, D), lambda b, i: (b, 0, 0))
    full = lambda a: pl.BlockSpec(a.shape, lambda b, i: (0,) * a.ndim)
    consts = [win_p, mu_p, row(w0), w2_p, row(a0), a2_p, g2, row(k_k), row(k_a), _head_ones(D_RWKV, RWKV_HEAD),
              row(q_norm), w_q_up.astype(BF16), wuk_bd, row(kv_norm), wiq_p, ig_p, ib_p]
    outs = [(D_RWKV, F32)] * 7 + [(N_ATT_HEADS * KV_LORA, BF16), (IDX_HEADS * LANES, BF16), (LANES, BF16),
                                  (LANES, F32), (KV_LORA + LANES, BF16)]
    return pl.pallas_call(
        _in_proj_kernel,
        grid=(B, S // tm),
        in_specs=[tok(D), mod, mod] + [full(a) for a in consts],
        out_specs=[tok(n) for n, _ in outs] + [pl.BlockSpec((None, KV_LORA + LANES, tm), lambda b, i: (b, 0, i))],
        out_shape=[jax.ShapeDtypeStruct((B, S, n), dt) for n, dt in outs]
        + [jax.ShapeDtypeStruct((B, KV_LORA + LANES, S), BF16)],
        scratch_shapes=[pltpu.VMEM((8, N_SHIFT_P), F32)],
        compiler_params=_cparams(("arbitrary", "arbitrary")),
        name="in_proj",
    )(x, sc1, sh1, *consts)


def _rwkv_kernel(r_ref, lw_ref, k_ref, v_ref, kk_ref, a_ref, g_ref, rk_ref, lnw_ref, lnb_ref, tri_ref, ones_ref,
                 o_ref, state, *, L):
    c = pl.program_id(1)
    nc = r_ref.shape[0] // L

    @pl.when(c == 0)
    def _():
        state[...] = jnp.zeros_like(state)

    r, lw, k, v, kk, a = r_ref[...], lw_ref[...], k_ref[...], v_ref[...], kk_ref[...], a_ref[...]
    cum = _dot_exact_lhs(tri_ref[...], lw)
    rows = [slice(ci * L, (ci + 1) * L) for ci in range(nc)]
    last = [cum[(ci + 1) * L - 1:(ci + 1) * L, :] for ci in range(nc)]
    cum_last = jnp.concatenate([jnp.broadcast_to(z, (L, z.shape[1])) for z in last], axis=0)
    w_incl = jnp.exp(cum)
    w_inv = jnp.exp(-cum)
    w_rel = jnp.exp(cum_last - cum)
    w_last = [jnp.exp(z) for z in last]
    bvec = kk * a
    at = -kk * jnp.exp(cum - lw)
    rt = r * w_incl
    bt = bvec * w_inv
    kt = k * w_inv
    bh = bvec * w_rel
    kh = k * w_rel
    ti = lax.broadcasted_iota(I32, (L, L), 0)
    tj = lax.broadcasted_iota(I32, (L, L), 1)
    strict = tj < ti
    incl = tj <= ti
    eye = lax.broadcasted_iota(I32, (RWKV_HEAD, RWKV_HEAD), 0) == lax.broadcasted_iota(I32, (RWKV_HEAD, RWKV_HEAD), 1)
    NT = (((1,), (1,)), ((), ()))
    TN = (((0,), (0,)), ((), ()))
    heads = range(N_RWKV_HEADS)
    sls = [slice(h * RWKV_HEAD, (h + 1) * RWKV_HEAD) for h in heads]
    units = [(ci, h) for ci in range(nc) for h in heads]
    idx = range(len(units))
    mm = lambda x, y, dims=(((1,), (0,)), ((), ())): lax.dot_general(
        x.astype(BF16), y.astype(BF16), dims, preferred_element_type=F32)
    at_b, rt_b, bt_b, kt_b, bh_b, kh_b, v_b = [z.astype(BF16) for z in (at, rt, bt, kt, bh, kh, v)]
    cut = lambda z, u: z[rows[u[0]], sls[u[1]]]
    vh = [cut(v_b, u) for u in units]
    ath = [cut(at_b, u) for u in units]
    ar = [jnp.concatenate([cut(at_b, u), cut(rt_b, u)], axis=0) for u in units]
    g_b = [mm(ar[i], cut(bt_b, units[i]), NT) for i in idx]
    g_k = [mm(ar[i], cut(kt_b, units[i]), NT) for i in idx]
    n_ab = [jnp.where(strict, g_b[i][:L], 0.0) for i in idx]
    a_ak = [jnp.where(strict, g_k[i][:L], 0.0) for i in idx]
    a_rb = [jnp.where(incl, g_b[i][L:], 0.0).astype(BF16) for i in idx]
    a_rk = [jnp.where(incl, g_k[i][L:], 0.0) for i in idx]
    akv = [mm(a_ak[i], vh[i]) for i in idx]
    eye_l = jnp.where(ti == tj, 1.0, 0.0)
    tinv = [eye_l + n_ab[i] for i in idx]
    pw = n_ab
    for _ in range(int(math.log2(L)) - 1):
        pw = [mm(pw[i], pw[i]) for i in idx]
        tinv = [tinv[i] + mm(pw[i], tinv[i]) for i in idx]
    tinv = [t.astype(BF16) for t in tinv]
    a_t = [mm(tinv[i], ath[i]).astype(BF16) for i in idx]
    y = [mm(tinv[i], akv[i]).astype(BF16) for i in idx]
    m_c = [jnp.where(eye, w_last[units[i][0]][:, sls[units[i][1]]], 0.0) + mm(a_t[i], cut(bh_b, units[i]), TN)
           for i in idx]
    c_c = [mm(y[i], cut(bh_b, units[i]), TN) + mm(vh[i], cut(kh_b, units[i]), TN) for i in idx]
    q_c = [cut(rt, units[i]) + mm(a_rb[i], a_t[i]) for i in idx]
    o_loc = [mm(a_rb[i], y[i]) + mm(a_rk[i], vh[i]) for i in idx]
    s = [state[h] for h in heads]
    for ci in range(nc):
        for h in heads:
            i = ci * N_RWKV_HEADS + h
            o = o_loc[i] + mm(q_c[i], s[h], NT)
            s[h] = mm(s[h], m_c[i]) + c_c[i]
            mu = jnp.mean(o, axis=-1, keepdims=True)
            d = o - mu
            var = jnp.mean(d * d, axis=-1, keepdims=True)
            o_ref[rows[ci], sls[h]] = d * lax.rsqrt(var + RWKV_GN_EPS)
    for h in heads:
        state[h] = s[h]
    bonus = _dot_exact_rhs(r * k * rk_ref[...], ones_ref[...], nsplit=3) * v
    o_ref[...] = (o_ref[...] * lnw_ref[...] + lnb_ref[...] + bonus) * g_ref[...]


def _rwkv_scan(r, lw, k, v, kk, a, g, r_k, ln_w, ln_b):
    B, S, DR = r.shape
    L = min(L_CHUNK, S)
    lb = min(L * CHUNKS_PER_STEP, S)
    tri = jnp.asarray(np.kron(np.eye(lb // L), np.tril(np.ones((L, L)))), BF16)
    row = lambda z: z.reshape(1, -1)
    tok = pl.BlockSpec((None, lb, DR), lambda b, c: (b, c, 0))
    full = lambda z: pl.BlockSpec(z.shape, lambda b, c: (0,) * z.ndim)
    consts = [row(r_k), row(ln_w), row(ln_b), tri, _head_ones(DR, RWKV_HEAD)]
    return pl.pallas_call(
        functools.partial(_rwkv_kernel, L=L),
        grid=(B, S // lb),
        in_specs=[tok] * 7 + [full(z) for z in consts],
        out_specs=tok,
        out_shape=jax.ShapeDtypeStruct((B, S, DR), F32),
        scratch_shapes=[pltpu.VMEM((N_RWKV_HEADS, RWKV_HEAD, RWKV_HEAD), F32)],
        compiler_params=_cparams(("arbitrary", "arbitrary")),
        name="rwkv_scan",
    )(r, lw, k, v, kk, a, g, *consts)


def _alibi_cols():
    slope = np.asarray([2.0 ** (-8.0 * (h + 1) / N_ATT_HEADS) * LOG2E for h in range(N_ATT_HEADS)], np.float32)
    c_hi = slope.astype(BF16).astype(np.float32)
    c_lo = (slope - c_hi).astype(BF16).astype(np.float32)
    t = np.zeros((N_ATT_HEADS, LANES), np.float32)
    t[:, 1], t[:, 2], t[:, 3], t[:, 4] = LANES * c_hi, LANES * c_lo, c_hi, c_lo
    return jnp.asarray(t.astype(BF16))


def _dsa_kernel(iq_ref, iw_ref, qa_ref, ik_ref, ca_ref, cat_ref, wuv_ref, tril_ref, acol_ref, o_ref, key_ref,
                bias_ref, lg_ref, p_ref, *, topk, q_off, select):
    qi = pl.program_id(1) + q_off
    tq = qa_ref.shape[0]
    sk = ca_ref.shape[0]
    kc = min(KEY_CHUNK, sk)
    chunks = [slice(c * kc, (c + 1) * kc) for c in range(sk // kc)]
    tpos = qi * tq + lax.broadcasted_iota(I32, (1, tq), 1)
    srow = lax.broadcasted_iota(I32, (kc, tq), 0)
    NT = (((1,), (1,)), ((), ()))

    if not select:
        for c, cs in enumerate(chunks):
            bias_ref[cs, :] = jnp.where(srow + c * kc <= tpos, 0.0, NEG_BIG)
    else:
        iw_t = jnp.transpose(iw_ref[...])
        for c, cs in enumerate(chunks):
            ikc = ik_ref[cs, :]
            score = jnp.zeros((kc, tq), F32)
            for h in range(IDX_HEADS):
                s = lax.dot_general(ikc, iq_ref[:, h * LANES:(h + 1) * LANES], NT, preferred_element_type=F32)
                score = score + iw_t[h:h + 1, :] * jnp.maximum(s, 0.0)
            bits = pltpu.bitcast(score + 0.0, I32)
            key = bits ^ ((bits >> 31) & 0x7FFFFFFF)
            key_ref[cs, :] = jnp.where(srow + c * kc <= tpos, key, INT_MIN)
        kcount = jnp.minimum(topk, tpos + 1).astype(F32)

        def sum_keys(x):
            part = jnp.sum(x.reshape(x.shape[0] // SUM_ROWS, SUM_ROWS, x.shape[1]), axis=0)
            return jnp.sum(part, axis=0, keepdims=True)

        def count_ge(cand):
            return sum_keys(jnp.where(key_ref[...] >= cand, 1.0, 0.0))

        thr = jnp.where(count_ge(jnp.zeros((1, tq), I32)) >= kcount, 0, INT_MIN).astype(I32)

        def bit_step(i, thr):
            cand = thr | (1 << (30 - i))
            return jnp.where(count_ge(cand) >= kcount, cand, thr)

        thr = lax.fori_loop(0, 31, bit_step, thr)

        tied = jnp.max(count_ge(thr) - kcount) > 0.0

        @pl.when(jnp.logical_not(tied))
        def _():
            for cs in chunks:
                bias_ref[cs, :] = jnp.where(key_ref[cs, :] >= thr, 0.0, NEG_BIG)

        @pl.when(tied)
        def _():
            need = kcount - sum_keys(jnp.where(key_ref[...] > thr, 1.0, 0.0))
            carry = jnp.zeros((1, tq), F32)
            for j in range(sk // LANES):
                sl = slice(j * LANES, (j + 1) * LANES)
                key = key_ref[sl, :]
                eq = key == thr
                e = jnp.where(eq, 1.0, 0.0)
                before = jnp.dot(tril_ref[...], e.astype(BF16), preferred_element_type=F32) + carry
                sel = (key > thr) | (eq & (before < need))
                bias_ref[sl, :] = jnp.where(sel, 0.0, NEG_BIG)
                carry = carry + jnp.sum(e, axis=0, keepdims=True)

    outs = []
    for h in range(N_ATT_HEADS):
        q_aug = jnp.concatenate([qa_ref[:, h * KV_LORA:(h + 1) * KV_LORA],
                                 jnp.broadcast_to(acol_ref[h:h + 1, :], (tq, LANES))], axis=-1)
        lg_all = lax.dot_general(ca_ref[...], q_aug, NT, preferred_element_type=F32)
        m = jnp.full((1, tq), -jnp.inf, F32)
        for cs in chunks:
            lg = lg_all[cs, :] + bias_ref[cs, :]
            lg_ref[cs, :] = lg
            m = jnp.maximum(m, jnp.max(lg, axis=0, keepdims=True))
        for cs in chunks:
            p_ref[cs, :] = jnp.exp2(lg_ref[cs, :] - m).astype(BF16)
        pv = jnp.dot(cat_ref[...], p_ref[...], preferred_element_type=F32)
        outs.append((pv[:KV_LORA, :] / pv[KV_LORA:KV_LORA + 1, :]).astype(BF16))
    o_lat_t = jnp.concatenate(outs, axis=0)
    o_ref[...] = lax.dot_general(o_lat_t, wuv_ref[...], (((0,), (0,)), ((), ())), preferred_element_type=F32)


def _dsa_attn(iq, iw, qabs, ik, ca, cat, w_uv):
    B, S, _ = iq.shape
    tq = min(TQ, S)
    topk = min(TOPK_MAX, S // 4)
    wuv_bd = _block_diag(w_uv).astype(BF16)
    tril = jnp.asarray(np.tril(np.ones((LANES, LANES)), -1), BF16)
    acol = _alibi_cols()
    full = lambda z: pl.BlockSpec(z.shape, lambda b, i: (0,) * z.ndim)
    nq = 1
    outs = []
    for q_off in range(0, S // tq, nq):
        sk = (q_off + nq) * tq
        tok = lambda n, q_off=q_off: pl.BlockSpec((None, tq, n), lambda b, i: (b, i + q_off, 0))
        seq = lambda n, sk=sk: pl.BlockSpec((None, sk, n), lambda b, i: (b, 0, 0))
        outs.append(pl.pallas_call(
            functools.partial(_dsa_kernel, topk=topk, q_off=q_off, select=sk > topk),
            grid=(B, nq),
            in_specs=[tok(IDX_HEADS * LANES), tok(LANES), tok(N_ATT_HEADS * KV_LORA), seq(LANES), seq(KV_LORA + LANES),
                      pl.BlockSpec((None, KV_LORA + LANES, sk), lambda b, i: (b, 0, 0)),
                      full(wuv_bd), full(tril), full(acol)],
            out_specs=pl.BlockSpec((None, tq, D_ATT), lambda b, i: (b, i, 0)),
            out_shape=jax.ShapeDtypeStruct((B, nq * tq, D_ATT), F32),
            scratch_shapes=[pltpu.VMEM((sk, tq), I32), pltpu.VMEM((sk, tq), F32), pltpu.VMEM((sk, tq), F32),
                            pltpu.VMEM((sk, tq), BF16)],
            compiler_params=_cparams(("arbitrary", "arbitrary")),
            name=f"dsa_attn_k{sk}",
        )(iq, iw, qabs, ik, ca, cat, wuv_bd, tril, acol))
    return jnp.concatenate(outs, axis=1)


def _layernorm_rows(y, g, b):
    mu = jnp.mean(y, axis=-1, keepdims=True)
    d = y - mu
    var = jnp.mean(d * d, axis=-1, keepdims=True)
    return d * lax.rsqrt(var + 1e-5) * g + b


def _mix_kernel(orw_ref, ods_ref, x_ref, gt_ref, sc_ref, sh_ref, wtop_ref, wbot_ref, g_ref, b_ref, wr_ref, br_ref,
                tril_ref, triu_ref, x1_o, u2_o, route_o, tbl_o, cnt_o, carry, *, alpha):
    first = (pl.program_id(0) == 0) & (pl.program_id(1) == 0)
    tm = x_ref.shape[0]

    @pl.when(first)
    def _():
        carry[...] = jnp.zeros_like(carry)

    mix = _bdot(orw_ref[...], wtop_ref[...]) + _bdot(ods_ref[...], wbot_ref[...])
    x1 = _layernorm_rows(alpha * x_ref[...] + (1.0 + gt_ref[...]) * mix, g_ref[...], b_ref[...])
    x1_o[...] = x1
    u2 = x1 * (1.0 + sc_ref[...]) + sh_ref[...]
    u2_o[...] = u2

    lg = _dot3(u2, wr_ref[...]) + br_ref[...]
    lane = lax.broadcasted_iota(I32, (tm, LANES), 1)
    lane_f = lane.astype(F32)
    idxs, vals = [], []
    for _ in range(TOP_K_EXPERTS):
        m = jnp.max(lg, axis=-1, keepdims=True)
        idx = jnp.min(jnp.where(lg == m, lane_f, float(LANES)), axis=-1, keepdims=True).astype(I32)
        idxs.append(idx)
        vals.append(m)
        lg = jnp.where(lane == idx, -jnp.inf, lg)
    es = [jnp.exp(v - vals[0]) for v in vals]
    den = es[0] + es[1] + es[2] + es[3]
    hot = jnp.zeros((tm, LANES), F32)
    for idx in idxs:
        hot = hot + jnp.where(lane == idx, 1.0, 0.0)
    before = jnp.dot(tril_ref[...], hot.astype(BF16), preferred_element_type=F32)
    cnt = jnp.sum(hot, axis=0, keepdims=True)
    n_run = jnp.floor((cnt + (RUN_ALIGN - 1)) * (1.0 / RUN_ALIGN))
    n_stage = jnp.floor((n_run * RUN_ALIGN + (STAGE_ALIGN - 1)) * (1.0 / STAGE_ALIGN))
    off = jnp.dot(jnp.broadcast_to(n_stage, (SUBLANES, LANES)).astype(BF16), triu_ref[...],
                  preferred_element_type=F32)[0:1, :] * STAGE_ALIGN
    where_in_stage = off + before
    route = jnp.zeros((tm, LANES), F32)
    for k in range(TOP_K_EXPERTS):
        spos = jnp.sum(jnp.where(lane == idxs[k], where_in_stage, 0.0), axis=-1, keepdims=True)
        route = jnp.where(lane == k, idxs[k].astype(F32), route)
        route = jnp.where(lane == TOP_K_EXPERTS + k, es[k] / den, route)
        route = jnp.where(lane == 2 * TOP_K_EXPERTS + k, spos, route)
    route_o[...] = route
    sub = lax.broadcasted_iota(I32, (SUBLANES, LANES), 0)
    tbl_o[...] = jnp.where(sub == 0, n_run, jnp.where(sub == 1, off, jnp.where(sub == 2, carry[0:1, :], 0.0)))
    carry[0:1, :] = carry[0:1, :] + n_run * RUN_ALIGN
    cnt_o[...] = carry[...]


def _mix_out(o_rwkv, o_dsa, x, gt1, sc2, sh2, w_out, ln_g, ln_b, w_router, b_router):
    B, S, D = x.shape
    tm = min(TM_PROJ, S)
    alpha = 2.0 ** 0.25
    wtop = w_out[:D_RWKV].astype(BF16)
    wbot = w_out[D_RWKV:].astype(BF16)
    wr_p = jnp.pad(w_router, ((0, 0), (0, LANES - N_EXPERTS)))
    br_p = jnp.pad(b_router.reshape(1, -1), ((0, 0), (0, LANES - N_EXPERTS)), constant_values=NEG_BIG)
    tril = jnp.asarray(np.tril(np.ones((tm, tm)), -1), BF16)
    triu = jnp.asarray(np.triu(np.ones((LANES, LANES)), 1), BF16)
    row = lambda v: v.reshape(1, -1)
    tok = lambda n: pl.BlockSpec((None, tm, n), lambda b, i: (b, i, 0))
    mod = pl.BlockSpec((None, 1, D), lambda b, i: (b, 0, 0))
    full = lambda a: pl.BlockSpec(a.shape, lambda b, i: (0,) * a.ndim)
    consts = [wtop, wbot, row(ln_g), row(ln_b), wr_p, br_p, tril, triu]
    return pl.pallas_call(
        functools.partial(_mix_kernel, alpha=alpha),
        grid=(B, S // tm),
        in_specs=[tok(D_RWKV), tok(D_ATT), tok(D), mod, mod, mod] + [full(a) for a in consts],
        out_specs=[tok(D), tok(D), tok(LANES), pl.BlockSpec((None, None, SUBLANES, LANES), lambda b, i: (b, i, 0, 0)),
                   pl.BlockSpec((SUBLANES, LANES), lambda b, i: (0, 0))],
        out_shape=[jax.ShapeDtypeStruct((B, S, D), F32), jax.ShapeDtypeStruct((B, S, D), F32),
                   jax.ShapeDtypeStruct((B, S, LANES), F32),
                   jax.ShapeDtypeStruct((B, S // tm, SUBLANES, LANES), F32),
                   jax.ShapeDtypeStruct((SUBLANES, LANES), F32)],
        scratch_shapes=[pltpu.VMEM((8, LANES), F32)],
        compiler_params=_cparams(("arbitrary", "arbitrary")),
        name="mix_out",
    )(o_rwkv, o_dsa, x, gt1, sc2, sh2, *consts)


RUN_ALIGN = SUBLANES
STAGE_ALIGN = 16
STAGE_ROWS = 1536
STAGE_CHUNK = 256


def _run_copies(nrun_ref, tile, copy_of):
    for e in range(N_EXPERTS):
        pieces = (nrun_ref[tile * N_EXPERTS + e] * RUN_ALIGN + STAGE_ALIGN - 1) // STAGE_ALIGN

        def piece(j, carry, e=e):
            copy_of(e, j).start(priority=e % 2)
            return carry

        lax.fori_loop(0, pieces, piece, 0)


def _dispatch_kernel(nrun_ref, off_ref, start_ref, npiece_ref, nstage_ref, zs_ref, zn_ref, tail_ref,
                     u_ref, route_ref, xs_out, stag, zeros, sem, zsem):
    i = pl.program_id(0)
    n = pl.num_programs(0)
    tm = u_ref.shape[0]
    bm = zeros.shape[0]
    slot = i % 2

    @pl.when(i == 0)
    def _():
        zeros[...] = jnp.zeros_like(zeros)
        fills = []
        for e in range(N_EXPERTS):
            for b in range(bm.bit_length()):
                rows = RUN_ALIGN << b
                if rows > bm:
                    break
                done = (zn_ref[e] >> (b + 1)) << (b + 1)
                dst = pl.multiple_of(zs_ref[e] + done * RUN_ALIGN, RUN_ALIGN)
                fills.append(((zn_ref[e] >> b) & 1 == 1,
                              pltpu.make_async_copy(zeros.at[pl.ds(0, rows)], xs_out.at[pl.ds(dst, rows)], zsem)))
        for pred, cp in fills:
            pl.when(pred)(cp.start)

        def tail_copy(j):
            return pltpu.make_async_copy(zeros, xs_out.at[pl.ds(pl.multiple_of(j * bm, bm), bm)], zsem)

        lax.fori_loop(tail_ref[0], tail_ref[1], lambda j, c: (tail_copy(j).start(), c)[1], 0)
        for pred, cp in fills:
            pl.when(pred)(cp.wait)
        lax.fori_loop(tail_ref[0], tail_ref[1], lambda j, c: (tail_copy(j).wait(), c)[1], 0)

    route_t = jnp.transpose(route_ref[...])
    spos = [route_t[2 * TOP_K_EXPERTS + k:2 * TOP_K_EXPERTS + k + 1, :].astype(I32) for k in range(TOP_K_EXPERTS)]
    ub = u_ref[...].astype(BF16)
    srow = lax.broadcasted_iota(I32, (STAGE_CHUNK, tm), 0)
    for c in range(STAGE_ROWS // STAGE_CHUNK):
        @pl.when(c * STAGE_CHUNK < nstage_ref[i])
        def _(c=c):
            rows = srow + c * STAGE_CHUNK
            sel = (rows == spos[0]) | (rows == spos[1]) | (rows == spos[2]) | (rows == spos[3])
            stag[slot, c * STAGE_CHUNK:(c + 1) * STAGE_CHUNK, :] = jnp.dot(
                jnp.where(sel, 1.0, 0.0).astype(BF16), ub, preferred_element_type=F32)

    def piece_copy(s, tile):
        def copy_of(e, j):
            src = pl.multiple_of(off_ref[tile * N_EXPERTS + e] + j * STAGE_ALIGN, STAGE_ALIGN)
            dst = pl.multiple_of(start_ref[tile * N_EXPERTS + e] + j * STAGE_ALIGN, RUN_ALIGN)
            return pltpu.make_async_copy(stag.at[s, pl.ds(src, STAGE_ALIGN)], xs_out.at[pl.ds(dst, STAGE_ALIGN)], sem)
        return copy_of

    def drain(tile):
        def w(j, carry):
            pltpu.make_async_copy(stag.at[0, pl.ds(0, STAGE_ALIGN)], xs_out.at[pl.ds(0, STAGE_ALIGN)], sem).wait()
            return carry
        lax.fori_loop(0, npiece_ref[tile], w, 0)

    @pl.when(i > 0)
    def _():
        drain(i - 1)

    _run_copies(nrun_ref, i, piece_copy(slot, i))

    @pl.when(i == n - 1)
    def _():
        drain(i)


def _moe_dispatch(u2, route, tables, n_rows):
    T, D = u2.shape
    tm = TM_ROUTE
    bm = BM_EXPERT
    return pl.pallas_call(
        _dispatch_kernel,
        grid_spec=pltpu.PrefetchScalarGridSpec(
            num_scalar_prefetch=len(tables),
            grid=(T // tm,),
            in_specs=[pl.BlockSpec((tm, D), lambda i, *_: (i, 0)),
                      pl.BlockSpec((tm, LANES), lambda i, *_: (i, 0))],
            out_specs=pl.BlockSpec(memory_space=pl.ANY),
            scratch_shapes=[pltpu.VMEM((2, STAGE_ROWS, D), F32), pltpu.VMEM((bm, D), F32),
                            pltpu.SemaphoreType.DMA(()), pltpu.SemaphoreType.DMA(())],
        ),
        out_shape=jax.ShapeDtypeStruct((n_rows, D), F32),
        compiler_params=_cparams(("arbitrary",)),
        name="moe_dispatch",
    )(*tables, u2, route)


GU_GROUP = 2 * LANES


def _deinterleave_perm():
    p = np.zeros((GU_GROUP, GU_GROUP), np.float32)
    l = np.arange(LANES)
    p[2 * l, l] = 1.0
    p[2 * l + 1, LANES + l] = 1.0
    return jnp.asarray(p, BF16)


def _expert_kernel(be_ref, nb_ref, valid_ref, xs_ref, wgu_hbm, bgu_ref, wd_hbm, bd_ref, perm_ref, ys_ref,
                   wg_buf, wd_buf, wp, wdb, sem):
    i = pl.program_id(0)
    bm = xs_ref.shape[0]
    e = be_ref[i]
    used = i < nb_ref[0]
    new_expert = (i == 0) | (e != be_ref[jnp.maximum(i - 1, 0)])
    n_groups = wp.shape[1] // GU_GROUP
    n_experts = wgu_hbm.shape[0]

    def fetch(ex):
        return (pltpu.make_async_copy(wgu_hbm.at[ex], wg_buf, sem.at[0]),
                pltpu.make_async_copy(wd_hbm.at[ex], wd_buf, sem.at[1]))

    @pl.when(used & new_expert)
    def _():
        @pl.when(i == 0)
        def _():
            for cp in fetch(e):
                cp.start()

        for cp in fetch(e):
            cp.wait()
        for j in range(n_groups):
            sl = slice(j * GU_GROUP, (j + 1) * GU_GROUP)
            wp[:, sl] = jnp.dot(wg_buf[:, sl].astype(BF16), perm_ref[...], preferred_element_type=F32).astype(BF16)
        wdb[...] = wd_buf[...].astype(BF16)

        @pl.when(e + 1 < n_experts)
        def _():
            for cp in fetch(e + 1):
                cp.start()

    def compute(m):
        xb = xs_ref[:m, :].astype(BF16)
        gu = jnp.dot(xb, wp[...], preferred_element_type=F32) + bgu_ref[...]
        hs = []
        for j in range(n_groups):
            gate = jnp.minimum(gu[:, j * GU_GROUP:j * GU_GROUP + LANES], SWIGLU_LIMIT)
            up = jnp.clip(gu[:, j * GU_GROUP + LANES:(j + 1) * GU_GROUP], -SWIGLU_LIMIT, SWIGLU_LIMIT)
            hs.append(((up + 1.0) * (gate * _sigmoid(gate * SWIGLU_ALPHA))).astype(BF16))
        h = jnp.concatenate(hs, axis=-1)
        ys_ref[:m, :] = jnp.dot(h, wdb[...], preferred_element_type=F32) + bd_ref[...]

    sizes = (bm, bm // 2, bm // 4)
    for m, smaller in zip(sizes, sizes[1:] + (0,)):
        @pl.when(used & (valid_ref[i] <= m) & (valid_ref[i] > smaller))
        def _(m=m):
            compute(m)
            if m < bm:
                ys_ref[m:, :] = jnp.zeros((bm - m, ys_ref.shape[1]), F32)

    @pl.when(used & (valid_ref[i] <= 0))
    def _():
        ys_ref[...] = jnp.zeros_like(ys_ref)

    @pl.when(jnp.logical_not(used))
    def _():
        ys_ref[...] = jnp.zeros_like(ys_ref)


def _moe_experts(xs, block_e, n_used, valid, w_gu, b_gu_p, w_dn, b_dn):
    n_rows, D = xs.shape
    E, _, F2 = w_gu.shape
    bm = BM_EXPERT
    n_blocks = n_rows // bm
    perm = _deinterleave_perm()
    wspec = lambda shp: pl.BlockSpec((None,) + shp, lambda i, be, nb, va: (be[i], 0, 0))
    hbm = pl.BlockSpec(memory_space=pl.ANY)
    return pl.pallas_call(
        _expert_kernel,
        grid_spec=pltpu.PrefetchScalarGridSpec(
            num_scalar_prefetch=3,
            grid=(n_blocks,),
            in_specs=[pl.BlockSpec((bm, D), lambda i, be, nb, va: (jnp.minimum(i, nb[0] - 1), 0)),
                      hbm, wspec((1, F2)), hbm, wspec((1, D)),
                      pl.BlockSpec(perm.shape, lambda i, be, nb, va: (0, 0))],
            out_specs=pl.BlockSpec((bm, D), lambda i, be, nb, va: (i, 0)),
            scratch_shapes=[pltpu.VMEM((D, F2), F32), pltpu.VMEM((F2 // 2, D), F32),
                            pltpu.VMEM((D, F2), BF16), pltpu.VMEM((F2 // 2, D), BF16),
                            pltpu.SemaphoreType.DMA((2,))],
        ),
        out_shape=jax.ShapeDtypeStruct((n_rows, D), F32),
        compiler_params=pltpu.CompilerParams(dimension_semantics=("arbitrary",), vmem_limit_bytes=VMEM_LIMIT_EXPERTS),
        name="moe_experts",
    )(block_e, n_used, valid, xs, w_gu, b_gu_p, w_dn, b_dn, perm)


def _combine_kernel(nrun_ref, off_ref, start_ref, npiece_ref, nstage_ref, ys_ref, x1_ref, route_ref, gt_ref, g_ref,
                    b_ref, o_ref, stag, sem, *, alpha):
    i = pl.program_id(0)
    n = pl.num_programs(0)
    tm = x1_ref.shape[0]
    slot = i % 2

    def gather(tile, s):
        def copy_of(e, j):
            src = pl.multiple_of(start_ref[tile * N_EXPERTS + e] + j * STAGE_ALIGN, RUN_ALIGN)
            dst = pl.multiple_of(off_ref[tile * N_EXPERTS + e] + j * STAGE_ALIGN, STAGE_ALIGN)
            return pltpu.make_async_copy(ys_ref.at[pl.ds(src, STAGE_ALIGN)], stag.at[s, pl.ds(dst, STAGE_ALIGN)],
                                         sem.at[s])
        _run_copies(nrun_ref, tile, copy_of)

    @pl.when(i == 0)
    def _():
        stag[...] = jnp.zeros_like(stag)
        gather(0, 0)

    @pl.when(i + 1 < n)
    def _():
        gather(i + 1, 1 - slot)

    def w(j, carry):
        pltpu.make_async_copy(ys_ref.at[pl.ds(0, STAGE_ALIGN)], stag.at[slot, pl.ds(0, STAGE_ALIGN)],
                              sem.at[slot]).wait()
        return carry
    lax.fori_loop(0, npiece_ref[i], w, 0)

    route = route_ref[...]
    wide = lambda col: jnp.broadcast_to(col, (tm, STAGE_CHUNK))
    spos = [wide(route[:, 2 * TOP_K_EXPERTS + k:2 * TOP_K_EXPERTS + k + 1].astype(I32)) for k in range(TOP_K_EXPERTS)]
    gate = [wide(route[:, TOP_K_EXPERTS + k:TOP_K_EXPERTS + k + 1]) for k in range(TOP_K_EXPERTS)]
    scol = lax.broadcasted_iota(I32, (tm, STAGE_CHUNK), 1)
    acc_ref = o_ref
    acc_ref[...] = jnp.zeros_like(acc_ref)
    for c in range(STAGE_ROWS // STAGE_CHUNK):
        @pl.when(c * STAGE_CHUNK < nstage_ref[i])
        def _(c=c):
            cols = scol + c * STAGE_CHUNK
            wgt = jnp.zeros((tm, STAGE_CHUNK), F32)
            for k in range(TOP_K_EXPERTS):
                wgt = wgt + jnp.where(cols == spos[k], gate[k], 0.0)
            acc_ref[...] += jnp.dot(wgt.astype(BF16), stag[slot, c * STAGE_CHUNK:(c + 1) * STAGE_CHUNK, :].astype(BF16),
                                    preferred_element_type=F32)
    o_ref[...] = _layernorm_rows(alpha * x1_ref[...] + (1.0 + gt_ref[...]) * acc_ref[...], g_ref[...], b_ref[...])


def _moe_combine(ys, tables, x1, route, gt2, ln_g, ln_b, tiles_per_batch):
    T, D = x1.shape
    tm = TM_ROUTE
    row = lambda v: v.reshape(1, -1)
    return pl.pallas_call(
        functools.partial(_combine_kernel, alpha=2.0 ** 0.25),
        grid_spec=pltpu.PrefetchScalarGridSpec(
            num_scalar_prefetch=len(tables),
            grid=(T // tm,),
            in_specs=[pl.BlockSpec(memory_space=pl.ANY),
                      pl.BlockSpec((tm, D), lambda i, *_: (i, 0)),
                      pl.BlockSpec((tm, LANES), lambda i, *_: (i, 0)),
                      pl.BlockSpec((None, 1, D), lambda i, *_: (i // tiles_per_batch, 0, 0)),
                      pl.BlockSpec((1, D), lambda i, *_: (0, 0)),
                      pl.BlockSpec((1, D), lambda i, *_: (0, 0))],
            out_specs=pl.BlockSpec((tm, D), lambda i, *_: (i, 0)),
            scratch_shapes=[pltpu.VMEM((2, STAGE_ROWS, D), F32), pltpu.SemaphoreType.DMA((2,))],
        ),
        out_shape=jax.ShapeDtypeStruct((T, D), F32),
        compiler_params=_cparams(("arbitrary",)),
        name="moe_combine",
    )(*tables, ys, x1, route, gt2, row(ln_g), row(ln_b))


def _moe_and_norm(x1, u2, route, tbl, totals, gt2, w_gu, b_gu, w_dn, b_dn, ln_g, ln_b):
    B, S, D = x1.shape
    T = B * S
    bm = BM_EXPERT
    tm = TM_ROUTE
    assert T % tm == 0 and TM_PROJ == tm and STAGE_ROWS >= tm * TOP_K_EXPERTS + N_EXPERTS * (STAGE_ALIGN - 1)
    n_tiles = T // tm
    max_rows = T * TOP_K_EXPERTS + n_tiles * N_EXPERTS * (RUN_ALIGN - 1) + N_EXPERTS * STAGE_ALIGN
    n_blocks = -(-max_rows // bm) + N_EXPERTS
    tot = totals[0, :N_EXPERTS].astype(I32)
    padded = (tot + STAGE_ALIGN + bm - 1) // bm * bm
    pad_ends = jnp.cumsum(padded)
    pad_starts = pad_ends - padded
    t3 = tbl.reshape(n_tiles, SUBLANES, LANES)[:, :, :N_EXPERTS].astype(I32)
    nrun, off, base = t3[:, 0, :], t3[:, 1, :], t3[:, 2, :]
    start = pad_starts[None, :] + base
    pieces = (nrun * RUN_ALIGN + STAGE_ALIGN - 1) // STAGE_ALIGN
    npiece = jnp.sum(pieces, axis=1)
    nstage = jnp.sum(pieces, axis=1) * STAGE_ALIGN
    flat = lambda z: z.reshape(-1).astype(I32)
    run_tables = [flat(nrun), flat(off), flat(start), flat(npiece), flat(nstage)]
    zs = pad_starts + tot
    zn = (pad_ends - zs) // RUN_ALIGN
    n_used = (pad_ends[-1:] // bm).astype(I32)
    tail = jnp.concatenate([n_used, jnp.full((1,), n_blocks, I32)])
    blk_row = jnp.arange(n_blocks, dtype=I32) * bm
    block_e = jnp.minimum(jnp.sum((blk_row[:, None] >= pad_ends[None, :]).astype(I32), axis=1), N_EXPERTS - 1)
    E, F2 = b_gu.shape
    b_gu_p = b_gu.reshape(E, F2 // GU_GROUP, LANES, 2).transpose(0, 1, 3, 2).reshape(E, 1, F2)
    route2 = route.reshape(T, LANES)
    xs = _moe_dispatch(u2.reshape(T, D), route2, run_tables + [flat(zs), flat(zn), tail], n_blocks * bm)
    valid = jnp.clip(zs[block_e] - blk_row, 0, bm)
    ys = _moe_experts(xs, block_e, n_used, valid, w_gu, b_gu_p, w_dn, b_dn[:, None, :])
    out = _moe_combine(ys, run_tables, x1.reshape(T, D), route2, gt2, ln_g, ln_b, S // tm)
    return out.reshape(B, S, D)


def kernel(x, c, w_ada, b_ada, w_in, shift_mu, rwkv_w0, rwkv_w2, rwkv_a0, rwkv_a2, rwkv_g2, rwkv_k_k, rwkv_k_a, rwkv_r_k, rwkv_ln_w, rwkv_ln_b, mla_q_norm, mla_w_q_up, mla_kv_norm, mla_w_uk, mla_w_uv, idx_w_q, idx_ln_g, idx_ln_b, w_out, ln1_g, ln1_b, w_router, b_router, w_gu, b_gu, w_dn, b_dn, ln2_g, ln2_b):
    depth = w_ada.shape[0]
    assert depth == 1, "DeepNorm constants below are for a single layer"
    l = 0
    mod = _ada_mod(c, w_ada[l], b_ada[l])
    sh1, sc1, gt1, sh2, sc2, gt2 = [m[:, None, :] for m in jnp.split(mod, 6, axis=-1)]
    r, lw, k, v, kk, a, g, qabs, iq, ik, iw, cl, clt = _in_proj(
        x, sc1, sh1, w_in[l], shift_mu[l], rwkv_w0[l], rwkv_w2[l], rwkv_a0[l], rwkv_a2[l], rwkv_g2[l], rwkv_k_k[l],
        rwkv_k_a[l], mla_q_norm[l], mla_w_q_up[l], mla_w_uk[l], mla_kv_norm[l], idx_w_q[l], idx_ln_g[l], idx_ln_b[l])
    o_rwkv = _rwkv_scan(r, lw, k, v, kk, a, g, rwkv_r_k[l], rwkv_ln_w[l], rwkv_ln_b[l])
    o_dsa = _dsa_attn(iq, iw, qabs, ik, cl, clt, mla_w_uv[l])
    x1, u2, route, tbl, totals = _mix_out(o_rwkv, o_dsa, x, gt1, sc2, sh2, w_out[l], ln1_g[l], ln1_b[l],
                                          w_router[l], b_router[l])
    return _moe_and_norm(x1, u2, route, tbl, totals, gt2, w_gu[l], b_gu[l], w_dn[l], b_dn[l], ln2_g[l], ln2_b[l])
```

```python
import functools
import math

import jax
import jax.numpy as jnp
import numpy as np
from jax import lax
from jax.experimental import pallas as pl
from jax.experimental.pallas import tpu as pltpu

F32 = jnp.float32
BF16 = jnp.bfloat16
I32 = jnp.int32

RWKV_HEAD = 64
N_RWKV_HEADS = 8
D_RWKV = RWKV_HEAD * N_RWKV_HEADS
RWKV_GN_EPS = 64e-5
ATT_HEAD = 64
N_ATT_HEADS = 8
D_ATT = ATT_HEAD * N_ATT_HEADS
KV_LORA = 128
IDX_HEADS = 8
IDX_DIM = 64
TOPK_MAX = 256
N_EXPERTS = 32
TOP_K_EXPERTS = 4
SWIGLU_LIMIT = 7.0
SWIGLU_ALPHA = 1.702
NEG_BIG = -1e30
LOG2E = 1.4426950408889634
INT_MIN = -(2 ** 31)

LANES = 128
SUBLANES = 8
VMEM_LIMIT = 48 * 1024 * 1024
VMEM_LIMIT_EXPERTS = 58 * 1024 * 1024

TM_IN = 512
TM_PROJ = 256
L_CHUNK = 64
CHUNKS_PER_STEP = 4
TQ = 256
KEY_CHUNK = 256
SUM_ROWS = 128
TM_ROUTE = 256
BM_EXPERT = 1024

_SEG = (("r", 512, 512), ("k", 512, 512), ("v", 512, 512), ("w", 64, 128), ("a", 64, 128), ("g", 128, 128),
        ("q", 256, 256), ("kv", 128, 128), ("ik", 64, 128), ("iw", 8, 128))
N_SHIFT_P = 512 * 3 + 128 * 3
N_IN_P = sum(s[2] for s in _SEG)


def _cparams(sem):
    return pltpu.CompilerParams(dimension_semantics=sem, vmem_limit_bytes=VMEM_LIMIT)


def _bdot(a, b):
    return jnp.dot(a.astype(BF16), b.astype(BF16), preferred_element_type=F32)


def _split2(a):
    hi = a.astype(BF16)
    lo = (a - hi.astype(F32)).astype(BF16)
    return hi, lo


def _split3(a):
    hi = a.astype(BF16)
    r1 = a - hi.astype(F32)
    mid = r1.astype(BF16)
    lo = (r1 - mid.astype(F32)).astype(BF16)
    return hi, mid, lo


def _dot3(a, b, dims=(((1,), (0,)), ((), ()))):
    ah, al = _split2(a)
    bh, bl = _split2(b)
    d = functools.partial(lax.dot_general, dimension_numbers=dims, preferred_element_type=F32)
    return d(ah, bh) + (d(ah, bl) + d(al, bh))


def _dot_exact_rhs(a, b_exact, nsplit=3):
    parts = _split3(a) if nsplit == 3 else _split2(a)
    acc = None
    for p in parts[::-1]:
        t = jnp.dot(p, b_exact, preferred_element_type=F32)
        acc = t if acc is None else acc + t
    return acc


def _dot_exact_lhs(a_exact, b, nsplit=3):
    parts = _split3(b) if nsplit == 3 else _split2(b)
    acc = None
    for p in parts[::-1]:
        t = jnp.dot(a_exact, p, preferred_element_type=F32)
        acc = t if acc is None else acc + t
    return acc


def _sigmoid(x):
    return 1.0 / (1.0 + jnp.exp(-x))


def _softplus(x):
    return jnp.maximum(x, 0.0) + jnp.log(1.0 + jnp.exp(-jnp.abs(x)))


def _ada_kernel(c_ref, w_ref, b_ref, o_ref):
    c = c_ref[...]
    o_ref[...] = _dot3(c * _sigmoid(c), w_ref[...]) + b_ref[...]


def _ada_mod(c, w_ada, b_ada):
    B, D = c.shape
    N = w_ada.shape[1]
    tn = 1024
    return pl.pallas_call(
        _ada_kernel,
        grid=(N // tn,),
        in_specs=[pl.BlockSpec((B, D), lambda j: (0, 0)),
                  pl.BlockSpec((D, tn), lambda j: (0, j)),
                  pl.BlockSpec((1, tn), lambda j: (0, j))],
        out_specs=pl.BlockSpec((B, tn), lambda j: (0, j)),
        out_shape=jax.ShapeDtypeStruct((B, N), F32),
        compiler_params=_cparams(("arbitrary",)),
        name="ada_mod",
    )(c, w_ada, b_ada.reshape(1, N))


def _in_proj_kernel(x_ref, sc_ref, sh_ref, win_ref, mu_ref, w0_ref, w2_ref, a0_ref, a2_ref, g2_ref, kk_ref, ka_ref,
                    ones_ref, qn_ref, wq_ref, wuk_ref, kvn_ref, wiq_ref, ig_ref, ib_ref,
                    r_o, lw_o, k_o, v_o, kkn_o, a_o, g_o, qabs_o, iq_o, ik_o, iw_o, cl_o, clt_o, carry):
    i = pl.program_id(1)
    tm = x_ref.shape[0]

    @pl.when(i == 0)
    def _():
        carry[...] = jnp.zeros_like(carry)

    u = x_ref[...] * (1.0 + sc_ref[...]) + sh_ref[...]
    p = _bdot(u, win_ref[...])
    ps = p[:, :N_SHIFT_P]
    rows = lax.broadcasted_iota(I32, (tm, 1), 0)
    prev = jnp.where(rows == 0, carry[0:1, :], pltpu.roll(ps, 1, 0))
    carry[0:1, :] = ps[tm - 1:tm, :]
    ps = ps + mu_ref[...] * (prev - ps)

    pr, pk, pv = ps[:, 0:512], ps[:, 512:1024], ps[:, 1024:1536]
    pw, pa, pg = ps[:, 1536:1664], ps[:, 1664:1792], ps[:, 1792:1920]
    w_log = -_softplus(-(w0_ref[...] + _dot3(jnp.tanh(pw), w2_ref[...]))) - 0.5
    lw_o[...] = -jnp.exp(w_log)
    a = _sigmoid(a0_ref[...] + _dot3(pa, a2_ref[...]))
    g_o[...] = _dot3(_sigmoid(pg), g2_ref[...])
    kk = pk * kk_ref[...]
    ssq = _dot_exact_rhs(kk * kk, ones_ref[...], nsplit=2)
    kkn_o[...] = kk / jnp.maximum(jnp.sqrt(ssq), 1e-12)
    k_o[...] = pk * (1.0 + (a - 1.0) * ka_ref[...])
    r_o[...] = pr
    v_o[...] = pv
    a_o[...] = a

    pq, pkv = p[:, 1920:2176], p[:, 2176:2304]
    pik, piw = p[:, 2304:2432], p[:, 2432:2560]
    q_lat = pq * lax.rsqrt(jnp.mean(pq * pq, axis=-1, keepdims=True) + 1e-6) * qn_ref[...]
    q = _bdot(q_lat, wq_ref[...])
    qabs_o[...] = (_bdot(q, wuk_ref[...]) * (ATT_HEAD ** -0.5 * LOG2E)).astype(BF16)
    c_lat = pkv * lax.rsqrt(jnp.mean(pkv * pkv, axis=-1, keepdims=True) + 1e-6) * kvn_ref[...]
    spos = i * tm + rows
    s_hi = (spos >> 7).astype(F32)
    s_lo = (spos & (LANES - 1)).astype(F32)
    lane_t = lax.broadcasted_iota(I32, (1, LANES), 1)
    extra = jnp.where(lane_t == 0, 1.0, jnp.where((lane_t == 1) | (lane_t == 2), s_hi,
                                                  jnp.where((lane_t == 3) | (lane_t == 4), s_lo, 0.0)))
    c_aug = jnp.concatenate([c_lat, extra], axis=-1)
    cl_o[...] = c_aug.astype(BF16)
    clt_o[...] = jnp.transpose(c_aug).astype(BF16)
    iq_o[...] = (_bdot(q_lat, wiq_ref[...]) * (IDX_DIM ** -0.5)).astype(BF16)
    lane = lax.broadcasted_iota(I32, (1, LANES), 1)
    valid = lane < IDX_DIM
    mu = jnp.sum(pik, axis=-1, keepdims=True) * (1.0 / IDX_DIM)
    dlt = jnp.where(valid, pik - mu, 0.0)
    var = jnp.sum(dlt * dlt, axis=-1, keepdims=True) * (1.0 / IDX_DIM)
    ik_o[...] = jnp.where(valid, dlt * lax.rsqrt(var + 1e-5) * ig_ref[...] + ib_ref[...], 0.0).astype(BF16)
    iw_o[...] = piw * (IDX_HEADS ** -0.5)


def _pad_cols(w, widths):
    parts, o = [], 0
    for true, padded in widths:
        seg = w[..., o:o + true]
        if padded > true:
            seg = jnp.pad(seg, [(0, 0)] * (w.ndim - 1) + [(0, padded - true)])
        parts.append(seg)
        o += true
    return jnp.concatenate(parts, axis=-1)


def _pad_rows(w, rows):
    return jnp.pad(w, ((0, rows - w.shape[0]), (0, 0)))


def _block_diag(blocks):
    H, a, b = blocks.shape
    eye = jnp.eye(H, dtype=blocks.dtype)
    return (eye[:, None, :, None] * blocks[:, :, None, :]).reshape(H * a, H * b)


def _head_ones(n, head):
    idx = np.arange(n) // head
    return jnp.asarray(idx[:, None] == idx[None, :], BF16)


def _in_proj(x, sc1, sh1, w_in, shift_mu, w0, w2, a0, a2, g2, k_k, k_a, q_norm, w_q_up, w_uk, kv_norm, idx_w_q,
             idx_ln_g, idx_ln_b):
    B, S, D = x.shape
    tm = min(TM_IN, S)
    widths = tuple((s[1], s[2]) for s in _SEG)
    win_p = _pad_cols(w_in, widths).astype(BF16)
    mu_p = _pad_cols(shift_mu.reshape(1, -1), widths[:6])
    w2_p = _pad_rows(w2, LANES)
    a2_p = _pad_rows(a2, LANES)
    wuk_bd = _block_diag(w_uk).astype(BF16)
    wiq_p = _pad_cols(idx_w_q, ((IDX_DIM, LANES),) * IDX_HEADS).astype(BF16)
    ig_p = _pad_cols(idx_ln_g.reshape(1, -1), ((IDX_DIM, LANES),))
    ib_p = _pad_cols(idx_ln_b.reshape(1, -1), ((IDX_DIM, LANES),))
    row = lambda v: v.reshape(1, -1)
    tok = lambda n: pl.BlockSpec((None, tm, n), lambda b, i: (b, i, 0))
    mod = pl.BlockSpec((None, 1, D), lambda b, i: (b, 0, 0))
    full = lambda a: pl.BlockSpec(a.shape, lambda b, i: (0,) * a.ndim)
    consts = [win_p, mu_p, row(w0), w2_p, row(a0), a2_p, g2, row(k_k), row(k_a), _head_ones(D_RWKV, RWKV_HEAD),
              row(q_norm), w_q_up.astype(BF16), wuk_bd, row(kv_norm), wiq_p, ig_p, ib_p]
    outs = [(D_RWKV, F32)] * 7 + [(N_ATT_HEADS * KV_LORA, BF16), (IDX_HEADS * LANES, BF16), (LANES, BF16),
                                  (LANES, F32), (KV_LORA + LANES, BF16)]
    return pl.pallas_call(
        _in_proj_kernel,
        grid=(B, S // tm),
        in_specs=[tok(D), mod, mod] + [full(a) for a in consts],
        out_specs=[tok(n) for n, _ in outs] + [pl.BlockSpec((None, KV_LORA + LANES, tm), lambda b, i: (b, 0, i))],
        out_shape=[jax.ShapeDtypeStruct((B, S, n), dt) for n, dt in outs]
        + [jax.ShapeDtypeStruct((B, KV_LORA + LANES, S), BF16)],
        scratch_shapes=[pltpu.VMEM((8, N_SHIFT_P), F32)],
        compiler_params=_cparams(("arbitrary", "arbitrary")),
        name="in_proj",
    )(x, sc1, sh1, *consts)


def _rwkv_kernel(r_ref, lw_ref, k_ref, v_ref, kk_ref, a_ref, g_ref, rk_ref, lnw_ref, lnb_ref, tri_ref, ones_ref,
                 o_ref, state, *, L):
    c = pl.program_id(1)
    nc = r_ref.shape[0] // L

    @pl.when(c == 0)
    def _():
        state[...] = jnp.zeros_like(state)

    r, lw, k, v, kk, a = r_ref[...], lw_ref[...], k_ref[...], v_ref[...], kk_ref[...], a_ref[...]
    cum = _dot_exact_lhs(tri_ref[...], lw)
    rows = [slice(ci * L, (ci + 1) * L) for ci in range(nc)]
    last = [cum[(ci + 1) * L - 1:(ci + 1) * L, :] for ci in range(nc)]
    cum_last = jnp.concatenate([jnp.broadcast_to(z, (L, z.shape[1])) for z in last], axis=0)
    w_incl = jnp.exp(cum)
    w_inv = jnp.exp(-cum)
    w_rel = jnp.exp(cum_last - cum)
    w_last = [jnp.exp(z) for z in last]
    bvec = kk * a
    at = -kk * jnp.exp(cum - lw)
    rt = r * w_incl
    bt = bvec * w_inv
    kt = k * w_inv
    bh = bvec * w_rel
    kh = k * w_rel
    ti = lax.broadcasted_iota(I32, (L, L), 0)
    tj = lax.broadcasted_iota(I32, (L, L), 1)
    strict = tj < ti
    incl = tj <= ti
    eye = lax.broadcasted_iota(I32, (RWKV_HEAD, RWKV_HEAD), 0) == lax.broadcasted_iota(I32, (RWKV_HEAD, RWKV_HEAD), 1)
    NT = (((1,), (1,)), ((), ()))
    TN = (((0,), (0,)), ((), ()))
    heads = range(N_RWKV_HEADS)
    sls = [slice(h * RWKV_HEAD, (h + 1) * RWKV_HEAD) for h in heads]
    units = [(ci, h) for ci in range(nc) for h in heads]
    idx = range(len(units))
    mm = lambda x, y, dims=(((1,), (0,)), ((), ())): lax.dot_general(
        x.astype(BF16), y.astype(BF16), dims, preferred_element_type=F32)
    at_b, rt_b, bt_b, kt_b, bh_b, kh_b, v_b = [z.astype(BF16) for z in (at, rt, bt, kt, bh, kh, v)]
    cut = lambda z, u: z[rows[u[0]], sls[u[1]]]
    vh = [cut(v_b, u) for u in units]
    ath = [cut(at_b, u) for u in units]
    ar = [jnp.concatenate([cut(at_b, u), cut(rt_b, u)], axis=0) for u in units]
    g_b = [mm(ar[i], cut(bt_b, units[i]), NT) for i in idx]
    g_k = [mm(ar[i], cut(kt_b, units[i]), NT) for i in idx]
    n_ab = [jnp.where(strict, g_b[i][:L], 0.0) for i in idx]
    a_ak = [jnp.where(strict, g_k[i][:L], 0.0) for i in idx]
    a_rb = [jnp.where(incl, g_b[i][L:], 0.0).astype(BF16) for i in idx]
    a_rk = [jnp.where(incl, g_k[i][L:], 0.0) for i in idx]
    akv = [mm(a_ak[i], vh[i]) for i in idx]
    eye_l = jnp.where(ti == tj, 1.0, 0.0)
    tinv = [eye_l + n_ab[i] for i in idx]
    pw = n_ab
    for _ in range(int(math.log2(L)) - 1):
        pw = [mm(pw[i], pw[i]) for i in idx]
        tinv = [tinv[i] + mm(pw[i], tinv[i]) for i in idx]
    tinv = [t.astype(BF16) for t in tinv]
    a_t = [mm(tinv[i], ath[i]).astype(BF16) for i in idx]
    y = [mm(tinv[i], akv[i]).astype(BF16) for i in idx]
    m_c = [jnp.where(eye, w_last[units[i][0]][:, sls[units[i][1]]], 0.0) + mm(a_t[i], cut(bh_b, units[i]), TN)
           for i in idx]
    c_c = [mm(y[i], cut(bh_b, units[i]), TN) + mm(vh[i], cut(kh_b, units[i]), TN) for i in idx]
    q_c = [cut(rt, units[i]) + mm(a_rb[i], a_t[i]) for i in idx]
    o_loc = [mm(a_rb[i], y[i]) + mm(a_rk[i], vh[i]) for i in idx]
    s = [state[h] for h in heads]
    for ci in range(nc):
        for h in heads:
            i = ci * N_RWKV_HEADS + h
            o = o_loc[i] + mm(q_c[i], s[h], NT)
            s[h] = mm(s[h], m_c[i]) + c_c[i]
            mu = jnp.mean(o, axis=-1, keepdims=True)
            d = o - mu
            var = jnp.mean(d * d, axis=-1, keepdims=True)
            o_ref[rows[ci], sls[h]] = d * lax.rsqrt(var + RWKV_GN_EPS)
    for h in heads:
        state[h] = s[h]
    bonus = _dot_exact_rhs(r * k * rk_ref[...], ones_ref[...], nsplit=3) * v
    o_ref[...] = (o_ref[...] * lnw_ref[...] + lnb_ref[...] + bonus) * g_ref[...]


def _rwkv_scan(r, lw, k, v, kk, a, g, r_k, ln_w, ln_b):
    B, S, DR = r.shape
    L = min(L_CHUNK, S)
    lb = min(L * CHUNKS_PER_STEP, S)
    tri = jnp.asarray(np.kron(np.eye(lb // L), np.tril(np.ones((L, L)))), BF16)
    row = lambda z: z.reshape(1, -1)
    tok = pl.BlockSpec((None, lb, DR), lambda b, c: (b, c, 0))
    full = lambda z: pl.BlockSpec(z.shape, lambda b, c: (0,) * z.ndim)
    consts = [row(r_k), row(ln_w), row(ln_b), tri, _head_ones(DR, RWKV_HEAD)]
    return pl.pallas_call(
        functools.partial(_rwkv_kernel, L=L),
        grid=(B, S // lb),
        in_specs=[tok] * 7 + [full(z) for z in consts],
        out_specs=tok,
        out_shape=jax.ShapeDtypeStruct((B, S, DR), F32),
        scratch_shapes=[pltpu.VMEM((N_RWKV_HEADS, RWKV_HEAD, RWKV_HEAD), F32)],
        compiler_params=_cparams(("arbitrary", "arbitrary")),
        name="rwkv_scan",
    )(r, lw, k, v, kk, a, g, *consts)


def _alibi_cols():
    slope = np.asarray([2.0 ** (-8.0 * (h + 1) / N_ATT_HEADS) * LOG2E for h in range(N_ATT_HEADS)], np.float32)
    c_hi = slope.astype(BF16).astype(np.float32)
    c_lo = (slope - c_hi).astype(BF16).astype(np.float32)
    t = np.zeros((N_ATT_HEADS, LANES), np.float32)
    t[:, 1], t[:, 2], t[:, 3], t[:, 4] = LANES * c_hi, LANES * c_lo, c_hi, c_lo
    return jnp.asarray(t.astype(BF16))


def _dsa_kernel(iq_ref, iw_ref, qa_ref, ik_ref, ca_ref, cat_ref, wuv_ref, tril_ref, acol_ref, o_ref, key_ref,
                bias_ref, lg_ref, p_ref, *, topk, q_off, select):
    qi = pl.program_id(1) + q_off
    tq = qa_ref.shape[0]
    sk = ca_ref.shape[0]
    kc = min(KEY_CHUNK, sk)
    chunks = [slice(c * kc, (c + 1) * kc) for c in range(sk // kc)]
    tpos = qi * tq + lax.broadcasted_iota(I32, (1, tq), 1)
    srow = lax.broadcasted_iota(I32, (kc, tq), 0)
    NT = (((1,), (1,)), ((), ()))

    if not select:
        for c, cs in enumerate(chunks):
            bias_ref[cs, :] = jnp.where(srow + c * kc <= tpos, 0.0, NEG_BIG)
    else:
        iw_t = jnp.transpose(iw_ref[...])
        for c, cs in enumerate(chunks):
            ikc = ik_ref[cs, :]
            score = jnp.zeros((kc, tq), F32)
            for h in range(IDX_HEADS):
                s = lax.dot_general(ikc, iq_ref[:, h * LANES:(h + 1) * LANES], NT, preferred_element_type=F32)
                score = score + iw_t[h:h + 1, :] * jnp.maximum(s, 0.0)
            bits = pltpu.bitcast(score + 0.0, I32)
            key = bits ^ ((bits >> 31) & 0x7FFFFFFF)
            key_ref[cs, :] = jnp.where(srow + c * kc <= tpos, key, INT_MIN)
        kcount = jnp.minimum(topk, tpos + 1).astype(F32)

        def sum_keys(x):
            part = jnp.sum(x.reshape(x.shape[0] // SUM_ROWS, SUM_ROWS, x.shape[1]), axis=0)
            return jnp.sum(part, axis=0, keepdims=True)

        def count_ge(cand):
            return sum_keys(jnp.where(key_ref[...] >= cand, 1.0, 0.0))

        thr = jnp.where(count_ge(jnp.zeros((1, tq), I32)) >= kcount, 0, INT_MIN).astype(I32)

        def bit_step(i, thr):
            cand = thr | (1 << (30 - i))
            return jnp.where(count_ge(cand) >= kcount, cand, thr)

        thr = lax.fori_loop(0, 31, bit_step, thr)

        need = kcount - sum_keys(jnp.where(key_ref[...] > thr, 1.0, 0.0))
        carry = jnp.zeros((1, tq), F32)
        for j in range(sk // LANES):
            sl = slice(j * LANES, (j + 1) * LANES)
            key = key_ref[sl, :]
            eq = key == thr
            e = jnp.where(eq, 1.0, 0.0)
            before = jnp.dot(tril_ref[...], e.astype(BF16), preferred_element_type=F32) + carry
            sel = (key > thr) | (eq & (before < need))
            bias_ref[sl, :] = jnp.where(sel, 0.0, NEG_BIG)
            carry = carry + jnp.sum(e, axis=0, keepdims=True)

    outs = []
    for h in range(N_ATT_HEADS):
        q_aug = jnp.concatenate([qa_ref[:, h * KV_LORA:(h + 1) * KV_LORA],
                                 jnp.broadcast_to(acol_ref[h:h + 1, :], (tq, LANES))], axis=-1)
        lg_all = lax.dot_general(ca_ref[...], q_aug, NT, preferred_element_type=F32)
        m = jnp.full((1, tq), -jnp.inf, F32)
        for cs in chunks:
            lg = lg_all[cs, :] + bias_ref[cs, :]
            lg_ref[cs, :] = lg
            m = jnp.maximum(m, jnp.max(lg, axis=0, keepdims=True))
        for cs in chunks:
            p_ref[cs, :] = jnp.exp2(lg_ref[cs, :] - m).astype(BF16)
        pv = jnp.dot(cat_ref[...], p_ref[...], preferred_element_type=F32)
        outs.append((pv[:KV_LORA, :] / pv[KV_LORA:KV_LORA + 1, :]).astype(BF16))
    o_lat_t = jnp.concatenate(outs, axis=0)
    o_ref[...] = lax.dot_general(o_lat_t, wuv_ref[...], (((0,), (0,)), ((), ())), preferred_element_type=F32)


def _dsa_attn(iq, iw, qabs, ik, ca, cat, w_uv):
    B, S, _ = iq.shape
    tq = min(TQ, S)
    topk = min(TOPK_MAX, S // 4)
    wuv_bd = _block_diag(w_uv).astype(BF16)
    tril = jnp.asarray(np.tril(np.ones((LANES, LANES)), -1), BF16)
    acol = _alibi_cols()
    full = lambda z: pl.BlockSpec(z.shape, lambda b, i: (0,) * z.ndim)
    nq = 1
    outs = []
    for q_off in range(0, S // tq, nq):
        sk = (q_off + nq) * tq
        tok = lambda n, q_off=q_off: pl.BlockSpec((None, tq, n), lambda b, i: (b, i + q_off, 0))
        seq = lambda n, sk=sk: pl.BlockSpec((None, sk, n), lambda b, i: (b, 0, 0))
        outs.append(pl.pallas_call(
            functools.partial(_dsa_kernel, topk=topk, q_off=q_off, select=sk > topk),
            grid=(B, nq),
            in_specs=[tok(IDX_HEADS * LANES), tok(LANES), tok(N_ATT_HEADS * KV_LORA), seq(LANES), seq(KV_LORA + LANES),
                      pl.BlockSpec((None, KV_LORA + LANES, sk), lambda b, i: (b, 0, 0)),
                      full(wuv_bd), full(tril), full(acol)],
            out_specs=pl.BlockSpec((None, tq, D_ATT), lambda b, i: (b, i, 0)),
            out_shape=jax.ShapeDtypeStruct((B, nq * tq, D_ATT), F32),
            scratch_shapes=[pltpu.VMEM((sk, tq), I32), pltpu.VMEM((sk, tq), F32), pltpu.VMEM((sk, tq), F32),
                            pltpu.VMEM((sk, tq), BF16)],
            compiler_params=_cparams(("arbitrary", "arbitrary")),
            name=f"dsa_attn_k{sk}",
        )(iq, iw, qabs, ik, ca, cat, wuv_bd, tril, acol))
    return jnp.concatenate(outs, axis=1)


def _layernorm_rows(y, g, b):
    mu = jnp.mean(y, axis=-1, keepdims=True)
    d = y - mu
    var = jnp.mean(d * d, axis=-1, keepdims=True)
    return d * lax.rsqrt(var + 1e-5) * g + b


def _mix_kernel(orw_ref, ods_ref, x_ref, gt_ref, sc_ref, sh_ref, wtop_ref, wbot_ref, g_ref, b_ref, wr_ref, br_ref,
                tril_ref, triu_ref, x1_o, u2_o, route_o, tbl_o, cnt_o, carry, *, alpha):
    first = (pl.program_id(0) == 0) & (pl.program_id(1) == 0)
    tm = x_ref.shape[0]

    @pl.when(first)
    def _():
        carry[...] = jnp.zeros_like(carry)

    mix = _bdot(orw_ref[...], wtop_ref[...]) + _bdot(ods_ref[...], wbot_ref[...])
    x1 = _layernorm_rows(alpha * x_ref[...] + (1.0 + gt_ref[...]) * mix, g_ref[...], b_ref[...])
    x1_o[...] = x1
    u2 = x1 * (1.0 + sc_ref[...]) + sh_ref[...]
    u2_o[...] = u2

    lg = _dot3(u2, wr_ref[...]) + br_ref[...]
    lane = lax.broadcasted_iota(I32, (tm, LANES), 1)
    lane_f = lane.astype(F32)
    idxs, vals = [], []
    for _ in range(TOP_K_EXPERTS):
        m = jnp.max(lg, axis=-1, keepdims=True)
        idx = jnp.min(jnp.where(lg == m, lane_f, float(LANES)), axis=-1, keepdims=True).astype(I32)
        idxs.append(idx)
        vals.append(m)
        lg = jnp.where(lane == idx, -jnp.inf, lg)
    es = [jnp.exp(v - vals[0]) for v in vals]
    den = es[0] + es[1] + es[2] + es[3]
    hot = jnp.zeros((tm, LANES), F32)
    for idx in idxs:
        hot = hot + jnp.where(lane == idx, 1.0, 0.0)
    before = jnp.dot(tril_ref[...], hot.astype(BF16), preferred_element_type=F32)
    cnt = jnp.sum(hot, axis=0, keepdims=True)
    n_run = jnp.floor((cnt + (RUN_ALIGN - 1)) * (1.0 / RUN_ALIGN))
    n_stage = jnp.floor((n_run * RUN_ALIGN + (STAGE_ALIGN - 1)) * (1.0 / STAGE_ALIGN))
    off = jnp.dot(jnp.broadcast_to(n_stage, (SUBLANES, LANES)).astype(BF16), triu_ref[...],
                  preferred_element_type=F32)[0:1, :] * STAGE_ALIGN
    where_in_stage = off + before
    route = jnp.zeros((tm, LANES), F32)
    for k in range(TOP_K_EXPERTS):
        spos = jnp.sum(jnp.where(lane == idxs[k], where_in_stage, 0.0), axis=-1, keepdims=True)
        route = jnp.where(lane == k, idxs[k].astype(F32), route)
        route = jnp.where(lane == TOP_K_EXPERTS + k, es[k] / den, route)
        route = jnp.where(lane == 2 * TOP_K_EXPERTS + k, spos, route)
    route_o[...] = route
    sub = lax.broadcasted_iota(I32, (SUBLANES, LANES), 0)
    tbl_o[...] = jnp.where(sub == 0, n_run, jnp.where(sub == 1, off, jnp.where(sub == 2, carry[0:1, :], 0.0)))
    carry[0:1, :] = carry[0:1, :] + n_run * RUN_ALIGN
    cnt_o[...] = carry[...]


def _mix_out(o_rwkv, o_dsa, x, gt1, sc2, sh2, w_out, ln_g, ln_b, w_router, b_router):
    B, S, D = x.shape
    tm = min(TM_PROJ, S)
    alpha = 2.0 ** 0.25
    wtop = w_out[:D_RWKV].astype(BF16)
    wbot = w_out[D_RWKV:].astype(BF16)
    wr_p = jnp.pad(w_router, ((0, 0), (0, LANES - N_EXPERTS)))
    br_p = jnp.pad(b_router.reshape(1, -1), ((0, 0), (0, LANES - N_EXPERTS)), constant_values=NEG_BIG)
    tril = jnp.asarray(np.tril(np.ones((tm, tm)), -1), BF16)
    triu = jnp.asarray(np.triu(np.ones((LANES, LANES)), 1), BF16)
    row = lambda v: v.reshape(1, -1)
    tok = lambda n: pl.BlockSpec((None, tm, n), lambda b, i: (b, i, 0))
    mod = pl.BlockSpec((None, 1, D), lambda b, i: (b, 0, 0))
    full = lambda a: pl.BlockSpec(a.shape, lambda b, i: (0,) * a.ndim)
    consts = [wtop, wbot, row(ln_g), row(ln_b), wr_p, br_p, tril, triu]
    return pl.pallas_call(
        functools.partial(_mix_kernel, alpha=alpha),
        grid=(B, S // tm),
        in_specs=[tok(D_RWKV), tok(D_ATT), tok(D), mod, mod, mod] + [full(a) for a in consts],
        out_specs=[tok(D), tok(D), tok(LANES), pl.BlockSpec((None, None, SUBLANES, LANES), lambda b, i: (b, i, 0, 0)),
                   pl.BlockSpec((SUBLANES, LANES), lambda b, i: (0, 0))],
        out_shape=[jax.ShapeDtypeStruct((B, S, D), F32), jax.ShapeDtypeStruct((B, S, D), F32),
                   jax.ShapeDtypeStruct((B, S, LANES), F32),
                   jax.ShapeDtypeStruct((B, S // tm, SUBLANES, LANES), F32),
                   jax.ShapeDtypeStruct((SUBLANES, LANES), F32)],
        scratch_shapes=[pltpu.VMEM((8, LANES), F32)],
        compiler_params=_cparams(("arbitrary", "arbitrary")),
        name="mix_out",
    )(o_rwkv, o_dsa, x, gt1, sc2, sh2, *consts)


RUN_ALIGN = SUBLANES
STAGE_ALIGN = 16
STAGE_ROWS = 1536
STAGE_CHUNK = 256


def _run_copies(nrun_ref, tile, copy_of):
    for e in range(N_EXPERTS):
        pieces = (nrun_ref[tile * N_EXPERTS + e] * RUN_ALIGN + STAGE_ALIGN - 1) // STAGE_ALIGN

        def piece(j, carry, e=e):
            copy_of(e, j).start(priority=e % 2)
            return carry

        lax.fori_loop(0, pieces, piece, 0)


def _dispatch_kernel(nrun_ref, off_ref, start_ref, npiece_ref, nstage_ref, zs_ref, zn_ref, tail_ref,
                     u_ref, route_ref, xs_out, stag, zeros, sem, zsem):
    i = pl.program_id(0)
    n = pl.num_programs(0)
    tm = u_ref.shape[0]
    bm = zeros.shape[0]
    slot = i % 2

    @pl.when(i == 0)
    def _():
        zeros[...] = jnp.zeros_like(zeros)
        fills = []
        for e in range(N_EXPERTS):
            for b in range(bm.bit_length()):
                rows = RUN_ALIGN << b
                if rows > bm:
                    break
                done = (zn_ref[e] >> (b + 1)) << (b + 1)
                dst = pl.multiple_of(zs_ref[e] + done * RUN_ALIGN, RUN_ALIGN)
                fills.append(((zn_ref[e] >> b) & 1 == 1,
                              pltpu.make_async_copy(zeros.at[pl.ds(0, rows)], xs_out.at[pl.ds(dst, rows)], zsem)))
        for pred, cp in fills:
            pl.when(pred)(cp.start)

        def tail_copy(j):
            return pltpu.make_async_copy(zeros, xs_out.at[pl.ds(pl.multiple_of(j * bm, bm), bm)], zsem)

        lax.fori_loop(tail_ref[0], tail_ref[1], lambda j, c: (tail_copy(j).start(), c)[1], 0)
        for pred, cp in fills:
            pl.when(pred)(cp.wait)
        lax.fori_loop(tail_ref[0], tail_ref[1], lambda j, c: (tail_copy(j).wait(), c)[1], 0)

    route_t = jnp.transpose(route_ref[...])
    spos = [route_t[2 * TOP_K_EXPERTS + k:2 * TOP_K_EXPERTS + k + 1, :].astype(I32) for k in range(TOP_K_EXPERTS)]
    ub = u_ref[...].astype(BF16)
    srow = lax.broadcasted_iota(I32, (STAGE_CHUNK, tm), 0)
    for c in range(STAGE_ROWS // STAGE_CHUNK):
        @pl.when(c * STAGE_CHUNK < nstage_ref[i])
        def _(c=c):
            rows = srow + c * STAGE_CHUNK
            sel = (rows == spos[0]) | (rows == spos[1]) | (rows == spos[2]) | (rows == spos[3])
            stag[slot, c * STAGE_CHUNK:(c + 1) * STAGE_CHUNK, :] = jnp.dot(
                jnp.where(sel, 1.0, 0.0).astype(BF16), ub, preferred_element_type=F32)

    def piece_copy(s, tile):
        def copy_of(e, j):
            src = pl.multiple_of(off_ref[tile * N_EXPERTS + e] + j * STAGE_ALIGN, STAGE_ALIGN)
            dst = pl.multiple_of(start_ref[tile * N_EXPERTS + e] + j * STAGE_ALIGN, RUN_ALIGN)
            return pltpu.make_async_copy(stag.at[s, pl.ds(src, STAGE_ALIGN)], xs_out.at[pl.ds(dst, STAGE_ALIGN)], sem)
        return copy_of

    def drain(tile):
        def w(j, carry):
            pltpu.make_async_copy(stag.at[0, pl.ds(0, STAGE_ALIGN)], xs_out.at[pl.ds(0, STAGE_ALIGN)], sem).wait()
            return carry
        lax.fori_loop(0, npiece_ref[tile], w, 0)

    @pl.when(i > 0)
    def _():
        drain(i - 1)

    _run_copies(nrun_ref, i, piece_copy(slot, i))

    @pl.when(i == n - 1)
    def _():
        drain(i)


def _moe_dispatch(u2, route, tables, n_rows):
    T, D = u2.shape
    tm = TM_ROUTE
    bm = BM_EXPERT
    return pl.pallas_call(
        _dispatch_kernel,
        grid_spec=pltpu.PrefetchScalarGridSpec(
            num_scalar_prefetch=len(tables),
            grid=(T // tm,),
            in_specs=[pl.BlockSpec((tm, D), lambda i, *_: (i, 0)),
                      pl.BlockSpec((tm, LANES), lambda i, *_: (i, 0))],
            out_specs=pl.BlockSpec(memory_space=pl.ANY),
            scratch_shapes=[pltpu.VMEM((2, STAGE_ROWS, D), F32), pltpu.VMEM((bm, D), F32),
                            pltpu.SemaphoreType.DMA(()), pltpu.SemaphoreType.DMA(())],
        ),
        out_shape=jax.ShapeDtypeStruct((n_rows, D), F32),
        compiler_params=_cparams(("arbitrary",)),
        name="moe_dispatch",
    )(*tables, u2, route)


GU_GROUP = 2 * LANES


def _deinterleave_perm():
    p = np.zeros((GU_GROUP, GU_GROUP), np.float32)
    l = np.arange(LANES)
    p[2 * l, l] = 1.0
    p[2 * l + 1, LANES + l] = 1.0
    return jnp.asarray(p, BF16)


def _expert_kernel(be_ref, nb_ref, valid_ref, xs_ref, wgu_hbm, bgu_ref, wd_hbm, bd_ref, perm_ref, ys_ref,
                   wg_buf, wd_buf, wp, wdb, sem):
    i = pl.program_id(0)
    bm = xs_ref.shape[0]
    e = be_ref[i]
    used = i < nb_ref[0]
    new_expert = (i == 0) | (e != be_ref[jnp.maximum(i - 1, 0)])
    n_groups = wp.shape[1] // GU_GROUP
    n_experts = wgu_hbm.shape[0]

    def fetch(ex):
        return (pltpu.make_async_copy(wgu_hbm.at[ex], wg_buf, sem.at[0]),
                pltpu.make_async_copy(wd_hbm.at[ex], wd_buf, sem.at[1]))

    @pl.when(used & new_expert)
    def _():
        @pl.when(i == 0)
        def _():
            for cp in fetch(e):
                cp.start()

        for cp in fetch(e):
            cp.wait()
        for j in range(n_groups):
            sl = slice(j * GU_GROUP, (j + 1) * GU_GROUP)
            wp[:, sl] = jnp.dot(wg_buf[:, sl].astype(BF16), perm_ref[...], preferred_element_type=F32).astype(BF16)
        wdb[...] = wd_buf[...].astype(BF16)

        @pl.when(e + 1 < n_experts)
        def _():
            for cp in fetch(e + 1):
                cp.start()

    def compute(m):
        xb = xs_ref[:m, :].astype(BF16)
        gu = jnp.dot(xb, wp[...], preferred_element_type=F32) + bgu_ref[...]
        hs = []
        for j in range(n_groups):
            gate = jnp.minimum(gu[:, j * GU_GROUP:j * GU_GROUP + LANES], SWIGLU_LIMIT)
            up = jnp.clip(gu[:, j * GU_GROUP + LANES:(j + 1) * GU_GROUP], -SWIGLU_LIMIT, SWIGLU_LIMIT)
            hs.append(((up + 1.0) * (gate * _sigmoid(gate * SWIGLU_ALPHA))).astype(BF16))
        h = jnp.concatenate(hs, axis=-1)
        ys_ref[:m, :] = jnp.dot(h, wdb[...], preferred_element_type=F32) + bd_ref[...]

    sizes = (bm, bm // 2, bm // 4)
    for m, smaller in zip(sizes, sizes[1:] + (0,)):
        @pl.when(used & (valid_ref[i] <= m) & (valid_ref[i] > smaller))
        def _(m=m):
            compute(m)
            if m < bm:
                ys_ref[m:, :] = jnp.zeros((bm - m, ys_ref.shape[1]), F32)

    @pl.when(used & (valid_ref[i] <= 0))
    def _():
        ys_ref[...] = jnp.zeros_like(ys_ref)

    @pl.when(jnp.logical_not(used))
    def _():
        ys_ref[...] = jnp.zeros_like(ys_ref)


def _moe_experts(xs, block_e, n_used, valid, w_gu, b_gu_p, w_dn, b_dn):
    n_rows, D = xs.shape
    E, _, F2 = w_gu.shape
    bm = BM_EXPERT
    n_blocks = n_rows // bm
    perm = _deinterleave_perm()
    wspec = lambda shp: pl.BlockSpec((None,) + shp, lambda i, be, nb, va: (be[i], 0, 0))
    hbm = pl.BlockSpec(memory_space=pl.ANY)
    return pl.pallas_call(
        _expert_kernel,
        grid_spec=pltpu.PrefetchScalarGridSpec(
            num_scalar_prefetch=3,
            grid=(n_blocks,),
            in_specs=[pl.BlockSpec((bm, D), lambda i, be, nb, va: (jnp.minimum(i, nb[0] - 1), 0)),
                      hbm, wspec((1, F2)), hbm, wspec((1, D)),
                      pl.BlockSpec(perm.shape, lambda i, be, nb, va: (0, 0))],
            out_specs=pl.BlockSpec((bm, D), lambda i, be, nb, va: (i, 0)),
            scratch_shapes=[pltpu.VMEM((D, F2), F32), pltpu.VMEM((F2 // 2, D), F32),
                            pltpu.VMEM((D, F2), BF16), pltpu.VMEM((F2 // 2, D), BF16),
                            pltpu.SemaphoreType.DMA((2,))],
        ),
        out_shape=jax.ShapeDtypeStruct((n_rows, D), F32),
        compiler_params=pltpu.CompilerParams(dimension_semantics=("arbitrary",), vmem_limit_bytes=VMEM_LIMIT_EXPERTS),
        name="moe_experts",
    )(block_e, n_used, valid, xs, w_gu, b_gu_p, w_dn, b_dn, perm)


def _combine_kernel(nrun_ref, off_ref, start_ref, npiece_ref, nstage_ref, ys_ref, x1_ref, route_ref, gt_ref, g_ref,
                    b_ref, o_ref, stag, sem, *, alpha):
    i = pl.program_id(0)
    n = pl.num_programs(0)
    tm = x1_ref.shape[0]
    slot = i % 2

    def gather(tile, s):
        def copy_of(e, j):
            src = pl.multiple_of(start_ref[tile * N_EXPERTS + e] + j * STAGE_ALIGN, RUN_ALIGN)
            dst = pl.multiple_of(off_ref[tile * N_EXPERTS + e] + j * STAGE_ALIGN, STAGE_ALIGN)
            return pltpu.make_async_copy(ys_ref.at[pl.ds(src, STAGE_ALIGN)], stag.at[s, pl.ds(dst, STAGE_ALIGN)],
                                         sem.at[s])
        _run_copies(nrun_ref, tile, copy_of)

    @pl.when(i == 0)
    def _():
        stag[...] = jnp.zeros_like(stag)
        gather(0, 0)

    @pl.when(i + 1 < n)
    def _():
        gather(i + 1, 1 - slot)

    def w(j, carry):
        pltpu.make_async_copy(ys_ref.at[pl.ds(0, STAGE_ALIGN)], stag.at[slot, pl.ds(0, STAGE_ALIGN)],
                              sem.at[slot]).wait()
        return carry
    lax.fori_loop(0, npiece_ref[i], w, 0)

    route = route_ref[...]
    wide = lambda col: jnp.broadcast_to(col, (tm, STAGE_CHUNK))
    spos = [wide(route[:, 2 * TOP_K_EXPERTS + k:2 * TOP_K_EXPERTS + k + 1].astype(I32)) for k in range(TOP_K_EXPERTS)]
    gate = [wide(route[:, TOP_K_EXPERTS + k:TOP_K_EXPERTS + k + 1]) for k in range(TOP_K_EXPERTS)]
    scol = lax.broadcasted_iota(I32, (tm, STAGE_CHUNK), 1)
    acc_ref = o_ref
    acc_ref[...] = jnp.zeros_like(acc_ref)
    for c in range(STAGE_ROWS // STAGE_CHUNK):
        @pl.when(c * STAGE_CHUNK < nstage_ref[i])
        def _(c=c):
            cols = scol + c * STAGE_CHUNK
            wgt = jnp.zeros((tm, STAGE_CHUNK), F32)
            for k in range(TOP_K_EXPERTS):
                wgt = wgt + jnp.where(cols == spos[k], gate[k], 0.0)
            acc_ref[...] += jnp.dot(wgt.astype(BF16), stag[slot, c * STAGE_CHUNK:(c + 1) * STAGE_CHUNK, :].astype(BF16),
                                    preferred_element_type=F32)
    o_ref[...] = _layernorm_rows(alpha * x1_ref[...] + (1.0 + gt_ref[...]) * acc_ref[...], g_ref[...], b_ref[...])


def _moe_combine(ys, tables, x1, route, gt2, ln_g, ln_b, tiles_per_batch):
    T, D = x1.shape
    tm = TM_ROUTE
    row = lambda v: v.reshape(1, -1)
    return pl.pallas_call(
        functools.partial(_combine_kernel, alpha=2.0 ** 0.25),
        grid_spec=pltpu.PrefetchScalarGridSpec(
            num_scalar_prefetch=len(tables),
            grid=(T // tm,),
            in_specs=[pl.BlockSpec(memory_space=pl.ANY),
                      pl.BlockSpec((tm, D), lambda i, *_: (i, 0)),
                      pl.BlockSpec((tm, LANES), lambda i, *_: (i, 0)),
                      pl.BlockSpec((None, 1, D), lambda i, *_: (i // tiles_per_batch, 0, 0)),
                      pl.BlockSpec((1, D), lambda i, *_: (0, 0)),
                      pl.BlockSpec((1, D), lambda i, *_: (0, 0))],
            out_specs=pl.BlockSpec((tm, D), lambda i, *_: (i, 0)),
            scratch_shapes=[pltpu.VMEM((2, STAGE_ROWS, D), F32), pltpu.SemaphoreType.DMA((2,))],
        ),
        out_shape=jax.ShapeDtypeStruct((T, D), F32),
        compiler_params=_cparams(("arbitrary",)),
        name="moe_combine",
    )(*tables, ys, x1, route, gt2, row(ln_g), row(ln_b))


def _moe_and_norm(x1, u2, route, tbl, totals, gt2, w_gu, b_gu, w_dn, b_dn, ln_g, ln_b):
    B, S, D = x1.shape
    T = B * S
    bm = BM_EXPERT
    tm = TM_ROUTE
    assert T % tm == 0 and TM_PROJ == tm and STAGE_ROWS >= tm * TOP_K_EXPERTS + N_EXPERTS * (STAGE_ALIGN - 1)
    n_tiles = T // tm
    max_rows = T * TOP_K_EXPERTS + n_tiles * N_EXPERTS * (RUN_ALIGN - 1) + N_EXPERTS * STAGE_ALIGN
    n_blocks = -(-max_rows // bm) + N_EXPERTS
    tot = totals[0, :N_EXPERTS].astype(I32)
    padded = (tot + STAGE_ALIGN + bm - 1) // bm * bm
    pad_ends = jnp.cumsum(padded)
    pad_starts = pad_ends - padded
    t3 = tbl.reshape(n_tiles, SUBLANES, LANES)[:, :, :N_EXPERTS].astype(I32)
    nrun, off, base = t3[:, 0, :], t3[:, 1, :], t3[:, 2, :]
    start = pad_starts[None, :] + base
    pieces = (nrun * RUN_ALIGN + STAGE_ALIGN - 1) // STAGE_ALIGN
    npiece = jnp.sum(pieces, axis=1)
    nstage = jnp.sum(pieces, axis=1) * STAGE_ALIGN
    flat = lambda z: z.reshape(-1).astype(I32)
    run_tables = [flat(nrun), flat(off), flat(start), flat(npiece), flat(nstage)]
    zs = pad_starts + tot
    zn = (pad_ends - zs) // RUN_ALIGN
    n_used = (pad_ends[-1:] // bm).astype(I32)
    tail = jnp.concatenate([n_used, jnp.full((1,), n_blocks, I32)])
    blk_row = jnp.arange(n_blocks, dtype=I32) * bm
    block_e = jnp.minimum(jnp.sum((blk_row[:, None] >= pad_ends[None, :]).astype(I32), axis=1), N_EXPERTS - 1)
    E, F2 = b_gu.shape
    b_gu_p = b_gu.reshape(E, F2 // GU_GROUP, LANES, 2).transpose(0, 1, 3, 2).reshape(E, 1, F2)
    route2 = route.reshape(T, LANES)
    xs = _moe_dispatch(u2.reshape(T, D), route2, run_tables + [flat(zs), flat(zn), tail], n_blocks * bm)
    valid = jnp.clip(zs[block_e] - blk_row, 0, bm)
    ys = _moe_experts(xs, block_e, n_used, valid, w_gu, b_gu_p, w_dn, b_dn[:, None, :])
    out = _moe_combine(ys, run_tables, x1.reshape(T, D), route2, gt2, ln_g, ln_b, S // tm)
    return out.reshape(B, S, D)


def kernel(x, c, w_ada, b_ada, w_in, shift_mu, rwkv_w0, rwkv_w2, rwkv_a0, rwkv_a2, rwkv_g2, rwkv_k_k, rwkv_k_a, rwkv_r_k, rwkv_ln_w, rwkv_ln_b, mla_q_norm, mla_w_q_up, mla_kv_norm, mla_w_uk, mla_w_uv, idx_w_q, idx_ln_g, idx_ln_b, w_out, ln1_g, ln1_b, w_router, b_router, w_gu, b_gu, w_dn, b_dn, ln2_g, ln2_b):
    depth = w_ada.shape[0]
    assert depth == 1, "DeepNorm constants below are for a single layer"
    l = 0
    mod = _ada_mod(c, w_ada[l], b_ada[l])
    sh1, sc1, gt1, sh2, sc2, gt2 = [m[:, None, :] for m in jnp.split(mod, 6, axis=-1)]
    r, lw, k, v, kk, a, g, qabs, iq, ik, iw, cl, clt = _in_proj(
        x, sc1, sh1, w_in[l], shift_mu[l], rwkv_w0[l], rwkv_w2[l], rwkv_a0[l], rwkv_a2[l], rwkv_g2[l], rwkv_k_k[l],
        rwkv_k_a[l], mla_q_norm[l], mla_w_q_up[l], mla_w_uk[l], mla_kv_norm[l], idx_w_q[l], idx_ln_g[l], idx_ln_b[l])
    o_rwkv = _rwkv_scan(r, lw, k, v, kk, a, g, rwkv_r_k[l], rwkv_ln_w[l], rwkv_ln_b[l])
    o_dsa = _dsa_attn(iq, iw, qabs, ik, cl, clt, mla_w_uv[l])
    x1, u2, route, tbl, totals = _mix_out(o_rwkv, o_dsa, x, gt1, sc2, sh2, w_out[l], ln1_g[l], ln1_b[l],
                                          w_router[l], b_router[l])
    return _moe_and_norm(x1, u2, route, tbl, totals, gt2, w_gu[l], b_gu[l], w_dn[l], b_dn[l], ln2_g[l], ln2_b[l])
```

```python
import functools
import math

import jax
import jax.numpy as jnp
import numpy as np
from jax import lax
from jax.experimental import pallas as pl
from jax.experimental.pallas import tpu as pltpu

F32 = jnp.float32
BF16 = jnp.bfloat16
I32 = jnp.int32

RWKV_HEAD = 64
N_RWKV_HEADS = 8
D_RWKV = RWKV_HEAD * N_RWKV_HEADS
RWKV_GN_EPS = 64e-5
ATT_HEAD = 64
N_ATT_HEADS = 8
D_ATT = ATT_HEAD * N_ATT_HEADS
KV_LORA = 128
IDX_HEADS = 8
IDX_DIM = 64
TOPK_MAX = 256
N_EXPERTS = 32
TOP_K_EXPERTS = 4
SWIGLU_LIMIT = 7.0
SWIGLU_ALPHA = 1.702
NEG_BIG = -1e30
LOG2E = 1.4426950408889634
INT_MIN = -(2 ** 31)

LANES = 128
SUBLANES = 8
VMEM_LIMIT = 48 * 1024 * 1024
VMEM_LIMIT_EXPERTS = 58 * 1024 * 1024

TM_IN = 512
TM_PROJ = 256
L_CHUNK = 64
CHUNKS_PER_STEP = 4
TQ = 256
KEY_CHUNK = 256
SUM_ROWS = 128
TM_ROUTE = 256
BM_EXPERT = 1024

_SEG = (("r", 512, 512), ("k", 512, 512), ("v", 512, 512), ("w", 64, 128), ("a", 64, 128), ("g", 128, 128),
        ("q", 256, 256), ("kv", 128, 128), ("ik", 64, 128), ("iw", 8, 128))
N_SHIFT_P = 512 * 3 + 128 * 3
N_IN_P = sum(s[2] for s in _SEG)


def _cparams(sem):
    return pltpu.CompilerParams(dimension_semantics=sem, vmem_limit_bytes=VMEM_LIMIT)


def _bdot(a, b):
    return jnp.dot(a.astype(BF16), b.astype(BF16), preferred_element_type=F32)


def _split2(a):
    hi = a.astype(BF16)
    lo = (a - hi.astype(F32)).astype(BF16)
    return hi, lo


def _split3(a):
    hi = a.astype(BF16)
    r1 = a - hi.astype(F32)
    mid = r1.astype(BF16)
    lo = (r1 - mid.astype(F32)).astype(BF16)
    return hi, mid, lo


def _dot3(a, b, dims=(((1,), (0,)), ((), ()))):
    ah, al = _split2(a)
    bh, bl = _split2(b)
    d = functools.partial(lax.dot_general, dimension_numbers=dims, preferred_element_type=F32)
    return d(ah, bh) + (d(ah, bl) + d(al, bh))


def _dot_exact_rhs(a, b_exact, nsplit=3):
    parts = _split3(a) if nsplit == 3 else _split2(a)
    acc = None
    for p in parts[::-1]:
        t = jnp.dot(p, b_exact, preferred_element_type=F32)
        acc = t if acc is None else acc + t
    return acc


def _dot_exact_lhs(a_exact, b, nsplit=3):
    parts = _split3(b) if nsplit == 3 else _split2(b)
    acc = None
    for p in parts[::-1]:
        t = jnp.dot(a_exact, p, preferred_element_type=F32)
        acc = t if acc is None else acc + t
    return acc


def _sigmoid(x):
    return 1.0 / (1.0 + jnp.exp(-x))


def _softplus(x):
    return jnp.maximum(x, 0.0) + jnp.log(1.0 + jnp.exp(-jnp.abs(x)))


def _ada_kernel(c_ref, w_ref, b_ref, o_ref):
    c = c_ref[...]
    o_ref[...] = _dot3(c * _sigmoid(c), w_ref[...]) + b_ref[...]


def _ada_mod(c, w_ada, b_ada):
    B, D = c.shape
    N = w_ada.shape[1]
    tn = 1024
    return pl.pallas_call(
        _ada_kernel,
        grid=(N // tn,),
        in_specs=[pl.BlockSpec((B, D), lambda j: (0, 0)),
                  pl.BlockSpec((D, tn), lambda j: (0, j)),
                  pl.BlockSpec((1, tn), lambda j: (0, j))],
        out_specs=pl.BlockSpec((B, tn), lambda j: (0, j)),
        out_shape=jax.ShapeDtypeStruct((B, N), F32),
        compiler_params=_cparams(("arbitrary",)),
        name="ada_mod",
    )(c, w_ada, b_ada.reshape(1, N))


def _in_proj_kernel(x_ref, sc_ref, sh_ref, win_ref, mu_ref, w0_ref, w2_ref, a0_ref, a2_ref, g2_ref, kk_ref, ka_ref,
                    ones_ref, qn_ref, wq_ref, wuk_ref, kvn_ref, wiq_ref, ig_ref, ib_ref,
                    r_o, lw_o, k_o, v_o, kkn_o, a_o, g_o, qabs_o, iq_o, ik_o, iw_o, cl_o, clt_o, carry):
    i = pl.program_id(1)
    tm = x_ref.shape[0]

    @pl.when(i == 0)
    def _():
        carry[...] = jnp.zeros_like(carry)

    u = x_ref[...] * (1.0 + sc_ref[...]) + sh_ref[...]
    p = _bdot(u, win_ref[...])
    ps = p[:, :N_SHIFT_P]
    rows = lax.broadcasted_iota(I32, (tm, 1), 0)
    prev = jnp.where(rows == 0, carry[0:1, :], pltpu.roll(ps, 1, 0))
    carry[0:1, :] = ps[tm - 1:tm, :]
    ps = ps + mu_ref[...] * (prev - ps)

    pr, pk, pv = ps[:, 0:512], ps[:, 512:1024], ps[:, 1024:1536]
    pw, pa, pg = ps[:, 1536:1664], ps[:, 1664:1792], ps[:, 1792:1920]
    w_log = -_softplus(-(w0_ref[...] + _dot3(jnp.tanh(pw), w2_ref[...]))) - 0.5
    lw_o[...] = -jnp.exp(w_log)
    a = _sigmoid(a0_ref[...] + _dot3(pa, a2_ref[...]))
    g_o[...] = _dot3(_sigmoid(pg), g2_ref[...])
    kk = pk * kk_ref[...]
    ssq = _dot_exact_rhs(kk * kk, ones_ref[...], nsplit=2)
    kkn_o[...] = kk / jnp.maximum(jnp.sqrt(ssq), 1e-12)
    k_o[...] = pk * (1.0 + (a - 1.0) * ka_ref[...])
    r_o[...] = pr
    v_o[...] = pv
    a_o[...] = a

    pq, pkv = p[:, 1920:2176], p[:, 2176:2304]
    pik, piw = p[:, 2304:2432], p[:, 2432:2560]
    q_lat = pq * lax.rsqrt(jnp.mean(pq * pq, axis=-1, keepdims=True) + 1e-6) * qn_ref[...]
    q = _bdot(q_lat, wq_ref[...])
    qabs_o[...] = (_bdot(q, wuk_ref[...]) * (ATT_HEAD ** -0.5 * LOG2E)).astype(BF16)
    c_lat = pkv * lax.rsqrt(jnp.mean(pkv * pkv, axis=-1, keepdims=True) + 1e-6) * kvn_ref[...]
    spos = i * tm + rows
    s_hi = (spos >> 7).astype(F32)
    s_lo = (spos & (LANES - 1)).astype(F32)
    lane_t = lax.broadcasted_iota(I32, (1, LANES), 1)
    extra = jnp.where(lane_t == 0, 1.0, jnp.where((lane_t == 1) | (lane_t == 2), s_hi,
                                                  jnp.where((lane_t == 3) | (lane_t == 4), s_lo, 0.0)))
    c_aug = jnp.concatenate([c_lat, extra], axis=-1)
    cl_o[...] = c_aug.astype(BF16)
    clt_o[...] = jnp.transpose(c_aug).astype(BF16)
    iq_o[...] = (_bdot(q_lat, wiq_ref[...]) * (IDX_DIM ** -0.5)).astype(BF16)
    lane = lax.broadcasted_iota(I32, (1, LANES), 1)
    valid = lane < IDX_DIM
    mu = jnp.sum(pik, axis=-1, keepdims=True) * (1.0 / IDX_DIM)
    dlt = jnp.where(valid, pik - mu, 0.0)
    var = jnp.sum(dlt * dlt, axis=-1, keepdims=True) * (1.0 / IDX_DIM)
    ik_o[...] = jnp.where(valid, dlt * lax.rsqrt(var + 1e-5) * ig_ref[...] + ib_ref[...], 0.0).astype(BF16)
    iw_o[...] = piw * (IDX_HEADS ** -0.5)


def _pad_cols(w, widths):
    parts, o = [], 0
    for true, padded in widths:
        seg = w[..., o:o + true]
        if padded > true:
            seg = jnp.pad(seg, [(0, 0)] * (w.ndim - 1) + [(0, padded - true)])
        parts.append(seg)
        o += true
    return jnp.concatenate(parts, axis=-1)


def _pad_rows(w, rows):
    return jnp.pad(w, ((0, rows - w.shape[0]), (0, 0)))


def _block_diag(blocks):
    H, a, b = blocks.shape
    eye = jnp.eye(H, dtype=blocks.dtype)
    return (eye[:, None, :, None] * blocks[:, :, None, :]).reshape(H * a, H * b)


def _head_ones(n, head):
    idx = np.arange(n) // head
    return jnp.asarray(idx[:, None] == idx[None, :], BF16)


def _in_proj(x, sc1, sh1, w_in, shift_mu, w0, w2, a0, a2, g2, k_k, k_a, q_norm, w_q_up, w_uk, kv_norm, idx_w_q,
             idx_ln_g, idx_ln_b):
    B, S, D = x.shape
    tm = min(TM_IN, S)
    widths = tuple((s[1], s[2]) for s in _SEG)
    win_p = _pad_cols(w_in, widths).astype(BF16)
    mu_p = _pad_cols(shift_mu.reshape(1, -1), widths[:6])
    w2_p = _pad_rows(w2, LANES)
    a2_p = _pad_rows(a2, LANES)
    wuk_bd = _block_diag(w_uk).astype(BF16)
    wiq_p = _pad_cols(idx_w_q, ((IDX_DIM, LANES),) * IDX_HEADS).astype(BF16)
    ig_p = _pad_cols(idx_ln_g.reshape(1, -1), ((IDX_DIM, LANES),))
    ib_p = _pad_cols(idx_ln_b.reshape(1, -1), ((IDX_DIM, LANES),))
    row = lambda v: v.reshape(1, -1)
    tok = lambda n: pl.BlockSpec((None, tm, n), lambda b, i: (b, i, 0))
    mod = pl.BlockSpec((None, 1, D), lambda b, i: (b, 0, 0))
    full = lambda a: pl.BlockSpec(a.shape, lambda b, i: (0,) * a.ndim)
    consts = [win_p, mu_p, row(w0), w2_p, row(a0), a2_p, g2, row(k_k), row(k_a), _head_ones(D_RWKV, RWKV_HEAD),
              row(q_norm), w_q_up.astype(BF16), wuk_bd, row(kv_norm), wiq_p, ig_p, ib_p]
    outs = [(D_RWKV, F32)] * 7 + [(N_ATT_HEADS * KV_LORA, BF16), (IDX_HEADS * LANES, BF16), (LANES, BF16),
                                  (LANES, F32), (KV_LORA + LANES, BF16)]
    return pl.pallas_call(
        _in_proj_kernel,
        grid=(B, S // tm),
        in_specs=[tok(D), mod, mod] + [full(a) for a in consts],
        out_specs=[tok(n) for n, _ in outs] + [pl.BlockSpec((None, KV_LORA + LANES, tm), lambda b, i: (b, 0, i))],
        out_shape=[jax.ShapeDtypeStruct((B, S, n), dt) for n, dt in outs]
        + [jax.ShapeDtypeStruct((B, KV_LORA + LANES, S), BF16)],
        scratch_shapes=[pltpu.VMEM((8, N_SHIFT_P), F32)],
        compiler_params=_cparams(("arbitrary", "arbitrary")),
        name="in_proj",
    )(x, sc1, sh1, *consts)


def _rwkv_kernel(r_ref, lw_ref, k_ref, v_ref, kk_ref, a_ref, g_ref, rk_ref, lnw_ref, lnb_ref, tri_ref, ones_ref,
                 o_ref, state, *, L):
    c = pl.program_id(1)
    nc = r_ref.shape[0] // L

    @pl.when(c == 0)
    def _():
        state[...] = jnp.zeros_like(state)

    r, lw, k, v, kk, a = r_ref[...], lw_ref[...], k_ref[...], v_ref[...], kk_ref[...], a_ref[...]
    cum = _dot_exact_lhs(tri_ref[...], lw)
    rows = [slice(ci * L, (ci + 1) * L) for ci in range(nc)]
    last = [cum[(ci + 1) * L - 1:(ci + 1) * L, :] for ci in range(nc)]
    cum_last = jnp.concatenate([jnp.broadcast_to(z, (L, z.shape[1])) for z in last], axis=0)
    w_incl = jnp.exp(cum)
    w_inv = jnp.exp(-cum)
    w_rel = jnp.exp(cum_last - cum)
    w_last = [jnp.exp(z) for z in last]
    bvec = kk * a
    at = -kk * jnp.exp(cum - lw)
    rt = r * w_incl
    bt = bvec * w_inv
    kt = k * w_inv
    bh = bvec * w_rel
    kh = k * w_rel
    ti = lax.broadcasted_iota(I32, (L, L), 0)
    tj = lax.broadcasted_iota(I32, (L, L), 1)
    strict = tj < ti
    incl = tj <= ti
    eye = lax.broadcasted_iota(I32, (RWKV_HEAD, RWKV_HEAD), 0) == lax.broadcasted_iota(I32, (RWKV_HEAD, RWKV_HEAD), 1)
    NT = (((1,), (1,)), ((), ()))
    TN = (((0,), (0,)), ((), ()))
    heads = range(N_RWKV_HEADS)
    sls = [slice(h * RWKV_HEAD, (h + 1) * RWKV_HEAD) for h in heads]
    units = [(ci, h) for ci in range(nc) for h in heads]
    idx = range(len(units))
    mm = lambda x, y, dims=(((1,), (0,)), ((), ())): lax.dot_general(
        x.astype(BF16), y.astype(BF16), dims, preferred_element_type=F32)
    at_b, rt_b, bt_b, kt_b, bh_b, kh_b, v_b = [z.astype(BF16) for z in (at, rt, bt, kt, bh, kh, v)]
    cut = lambda z, u: z[rows[u[0]], sls[u[1]]]
    vh = [cut(v_b, u) for u in units]
    ath = [cut(at_b, u) for u in units]
    ar = [jnp.concatenate([cut(at_b, u), cut(rt_b, u)], axis=0) for u in units]
    g_b = [mm(ar[i], cut(bt_b, units[i]), NT) for i in idx]
    g_k = [mm(ar[i], cut(kt_b, units[i]), NT) for i in idx]
    n_ab = [jnp.where(strict, g_b[i][:L], 0.0) for i in idx]
    a_ak = [jnp.where(strict, g_k[i][:L], 0.0) for i in idx]
    a_rb = [jnp.where(incl, g_b[i][L:], 0.0).astype(BF16) for i in idx]
    a_rk = [jnp.where(incl, g_k[i][L:], 0.0) for i in idx]
    akv = [mm(a_ak[i], vh[i]) for i in idx]
    eye_l = jnp.where(ti == tj, 1.0, 0.0)
    tinv = [eye_l + n_ab[i] for i in idx]
    pw = n_ab
    for _ in range(int(math.log2(L)) - 1):
        pw = [mm(pw[i], pw[i]) for i in idx]
        tinv = [tinv[i] + mm(pw[i], tinv[i]) for i in idx]
    tinv = [t.astype(BF16) for t in tinv]
    a_t = [mm(tinv[i], ath[i]).astype(BF16) for i in idx]
    y = [mm(tinv[i], akv[i]).astype(BF16) for i in idx]
    m_c = [jnp.where(eye, w_last[units[i][0]][:, sls[units[i][1]]], 0.0) + mm(a_t[i], cut(bh_b, units[i]), TN)
           for i in idx]
    c_c = [mm(y[i], cut(bh_b, units[i]), TN) + mm(vh[i], cut(kh_b, units[i]), TN) for i in idx]
    q_c = [cut(rt, units[i]) + mm(a_rb[i], a_t[i]) for i in idx]
    o_loc = [mm(a_rb[i], y[i]) + mm(a_rk[i], vh[i]) for i in idx]
    s = [state[h] for h in heads]
    for ci in range(nc):
        for h in heads:
            i = ci * N_RWKV_HEADS + h
            o = o_loc[i] + mm(q_c[i], s[h], NT)
            s[h] = mm(s[h], m_c[i]) + c_c[i]
            mu = jnp.mean(o, axis=-1, keepdims=True)
            d = o - mu
            var = jnp.mean(d * d, axis=-1, keepdims=True)
            o_ref[rows[ci], sls[h]] = d * lax.rsqrt(var + RWKV_GN_EPS)
    for h in heads:
        state[h] = s[h]
    bonus = _dot_exact_rhs(r * k * rk_ref[...], ones_ref[...], nsplit=3) * v
    o_ref[...] = (o_ref[...] * lnw_ref[...] + lnb_ref[...] + bonus) * g_ref[...]


def _rwkv_scan(r, lw, k, v, kk, a, g, r_k, ln_w, ln_b):
    B, S, DR = r.shape
    L = min(L_CHUNK, S)
    lb = min(L * CHUNKS_PER_STEP, S)
    tri = jnp.asarray(np.kron(np.eye(lb // L), np.tril(np.ones((L, L)))), BF16)
    row = lambda z: z.reshape(1, -1)
    tok = pl.BlockSpec((None, lb, DR), lambda b, c: (b, c, 0))
    full = lambda z: pl.BlockSpec(z.shape, lambda b, c: (0,) * z.ndim)
    consts = [row(r_k), row(ln_w), row(ln_b), tri, _head_ones(DR, RWKV_HEAD)]
    return pl.pallas_call(
        functools.partial(_rwkv_kernel, L=L),
        grid=(B, S // lb),
        in_specs=[tok] * 7 + [full(z) for z in consts],
        out_specs=tok,
        out_shape=jax.ShapeDtypeStruct((B, S, DR), F32),
        scratch_shapes=[pltpu.VMEM((N_RWKV_HEADS, RWKV_HEAD, RWKV_HEAD), F32)],
        compiler_params=_cparams(("arbitrary", "arbitrary")),
        name="rwkv_scan",
    )(r, lw, k, v, kk, a, g, *consts)


def _alibi_cols():
    slope = np.asarray([2.0 ** (-8.0 * (h + 1) / N_ATT_HEADS) * LOG2E for h in range(N_ATT_HEADS)], np.float32)
    c_hi = slope.astype(BF16).astype(np.float32)
    c_lo = (slope - c_hi).astype(BF16).astype(np.float32)
    t = np.zeros((N_ATT_HEADS, LANES), np.float32)
    t[:, 1], t[:, 2], t[:, 3], t[:, 4] = LANES * c_hi, LANES * c_lo, c_hi, c_lo
    return jnp.asarray(t.astype(BF16))


def _dsa_kernel(iq_ref, iw_ref, qa_ref, ik_ref, ca_ref, cat_ref, wuv_ref, tril_ref, acol_ref, o_ref, key_ref,
                bias_ref, lg_ref, p_ref, *, topk, q_off, select):
    qi = pl.program_id(1) + q_off
    tq = qa_ref.shape[0]
    sk = ca_ref.shape[0]
    kc = min(KEY_CHUNK, sk)
    chunks = [slice(c * kc, (c + 1) * kc) for c in range(sk // kc)]
    tpos = qi * tq + lax.broadcasted_iota(I32, (1, tq), 1)
    srow = lax.broadcasted_iota(I32, (kc, tq), 0)
    NT = (((1,), (1,)), ((), ()))

    if not select:
        for c, cs in enumerate(chunks):
            bias_ref[cs, :] = jnp.where(srow + c * kc <= tpos, 0.0, NEG_BIG)
    else:
        iw_t = jnp.transpose(iw_ref[...])
        for c, cs in enumerate(chunks):
            ikc = ik_ref[cs, :]
            score = jnp.zeros((kc, tq), F32)
            for h in range(IDX_HEADS):
                s = lax.dot_general(ikc, iq_ref[:, h * LANES:(h + 1) * LANES], NT, preferred_element_type=F32)
                score = score + iw_t[h:h + 1, :] * jnp.maximum(s, 0.0)
            bits = pltpu.bitcast(score + 0.0, I32)
            key = bits ^ ((bits >> 31) & 0x7FFFFFFF)
            key_ref[cs, :] = jnp.where(srow + c * kc <= tpos, key, INT_MIN)
        kcount = jnp.minimum(topk, tpos + 1).astype(F32)

        def sum_keys(x):
            part = jnp.sum(x.reshape(x.shape[0] // SUM_ROWS, SUM_ROWS, x.shape[1]), axis=0)
            return jnp.sum(part, axis=0, keepdims=True)

        def count_ge(cand):
            return sum_keys(jnp.where(key_ref[...] >= cand, 1.0, 0.0))

        thr = jnp.where(count_ge(jnp.zeros((1, tq), I32)) >= kcount, 0, INT_MIN).astype(I32)

        def bit_step(i, thr):
            cand = thr | (1 << (30 - i))
            return jnp.where(count_ge(cand) >= kcount, cand, thr)

        thr = lax.fori_loop(0, 31, bit_step, thr)

        need = kcount - sum_keys(jnp.where(key_ref[...] > thr, 1.0, 0.0))
        carry = jnp.zeros((1, tq), F32)
        for j in range(sk // LANES):
            sl = slice(j * LANES, (j + 1) * LANES)
            key = key_ref[sl, :]
            eq = key == thr
            e = jnp.where(eq, 1.0, 0.0)
            before = jnp.dot(tril_ref[...], e.astype(BF16), preferred_element_type=F32) + carry
            sel = (key > thr) | (eq & (before < need))
            bias_ref[sl, :] = jnp.where(sel, 0.0, NEG_BIG)
            carry = carry + jnp.sum(e, axis=0, keepdims=True)

    outs = []
    for h in range(N_ATT_HEADS):
        q_aug = jnp.concatenate([qa_ref[:, h * KV_LORA:(h + 1) * KV_LORA],
                                 jnp.broadcast_to(acol_ref[h:h + 1, :], (tq, LANES))], axis=-1)
        lg_all = lax.dot_general(ca_ref[...], q_aug, NT, preferred_element_type=F32)
        m = jnp.full((1, tq), -jnp.inf, F32)
        for cs in chunks:
            lg = lg_all[cs, :] + bias_ref[cs, :]
            lg_ref[cs, :] = lg
            m = jnp.maximum(m, jnp.max(lg, axis=0, keepdims=True))
        for cs in chunks:
            p_ref[cs, :] = jnp.exp2(lg_ref[cs, :] - m).astype(BF16)
        pv = jnp.dot(cat_ref[...], p_ref[...], preferred_element_type=F32)
        outs.append((pv[:KV_LORA, :] / pv[KV_LORA:KV_LORA + 1, :]).astype(BF16))
    o_lat_t = jnp.concatenate(outs, axis=0)
    o_ref[...] = lax.dot_general(o_lat_t, wuv_ref[...], (((0,), (0,)), ((), ())), preferred_element_type=F32)


def _dsa_attn(iq, iw, qabs, ik, ca, cat, w_uv):
    B, S, _ = iq.shape
    tq = min(TQ, S)
    topk = min(TOPK_MAX, S // 4)
    wuv_bd = _block_diag(w_uv).astype(BF16)
    tril = jnp.asarray(np.tril(np.ones((LANES, LANES)), -1), BF16)
    acol = _alibi_cols()
    full = lambda z: pl.BlockSpec(z.shape, lambda b, i: (0,) * z.ndim)
    nq = 1
    outs = []
    for q_off in range(0, S // tq, nq):
        sk = (q_off + nq) * tq
        tok = lambda n, q_off=q_off: pl.BlockSpec((None, tq, n), lambda b, i: (b, i + q_off, 0))
        seq = lambda n, sk=sk: pl.BlockSpec((None, sk, n), lambda b, i: (b, 0, 0))
        outs.append(pl.pallas_call(
            functools.partial(_dsa_kernel, topk=topk, q_off=q_off, select=sk > topk),
            grid=(B, nq),
            in_specs=[tok(IDX_HEADS * LANES), tok(LANES), tok(N_ATT_HEADS * KV_LORA), seq(LANES), seq(KV_LORA + LANES),
                      pl.BlockSpec((None, KV_LORA + LANES, sk), lambda b, i: (b, 0, 0)),
                      full(wuv_bd), full(tril), full(acol)],
            out_specs=pl.BlockSpec((None, tq, D_ATT), lambda b, i: (b, i, 0)),
            out_shape=jax.ShapeDtypeStruct((B, nq * tq, D_ATT), F32),
            scratch_shapes=[pltpu.VMEM((sk, tq), I32), pltpu.VMEM((sk, tq), F32), pltpu.VMEM((sk, tq), F32),
                            pltpu.VMEM((sk, tq), BF16)],
            compiler_params=_cparams(("arbitrary", "arbitrary")),
            name=f"dsa_attn_k{sk}",
        )(iq, iw, qabs, ik, ca, cat, wuv_bd, tril, acol))
    return jnp.concatenate(outs, axis=1)


def _layernorm_rows(y, g, b):
    mu = jnp.mean(y, axis=-1, keepdims=True)
    d = y - mu
    var = jnp.mean(d * d, axis=-1, keepdims=True)
    return d * lax.rsqrt(var + 1e-5) * g + b


def _mix_kernel(orw_ref, ods_ref, x_ref, gt_ref, sc_ref, sh_ref, wtop_ref, wbot_ref, g_ref, b_ref, wr_ref, br_ref,
                tril_ref, triu_ref, x1_o, u2_o, route_o, tbl_o, cnt_o, carry, *, alpha):
    first = (pl.program_id(0) == 0) & (pl.program_id(1) == 0)
    tm = x_ref.shape[0]

    @pl.when(first)
    def _():
        carry[...] = jnp.zeros_like(carry)

    mix = _bdot(orw_ref[...], wtop_ref[...]) + _bdot(ods_ref[...], wbot_ref[...])
    x1 = _layernorm_rows(alpha * x_ref[...] + (1.0 + gt_ref[...]) * mix, g_ref[...], b_ref[...])
    x1_o[...] = x1
    u2 = x1 * (1.0 + sc_ref[...]) + sh_ref[...]
    u2_o[...] = u2

    lg = _dot3(u2, wr_ref[...]) + br_ref[...]
    lane = lax.broadcasted_iota(I32, (tm, LANES), 1)
    lane_f = lane.astype(F32)
    idxs, vals = [], []
    for _ in range(TOP_K_EXPERTS):
        m = jnp.max(lg, axis=-1, keepdims=True)
        idx = jnp.min(jnp.where(lg == m, lane_f, float(LANES)), axis=-1, keepdims=True).astype(I32)
        idxs.append(idx)
        vals.append(m)
        lg = jnp.where(lane == idx, -jnp.inf, lg)
    es = [jnp.exp(v - vals[0]) for v in vals]
    den = es[0] + es[1] + es[2] + es[3]
    hot = jnp.zeros((tm, LANES), F32)
    for idx in idxs:
        hot = hot + jnp.where(lane == idx, 1.0, 0.0)
    before = jnp.dot(tril_ref[...], hot.astype(BF16), preferred_element_type=F32)
    cnt = jnp.sum(hot, axis=0, keepdims=True)
    n_run = jnp.floor((cnt + (RUN_ALIGN - 1)) * (1.0 / RUN_ALIGN))
    n_stage = jnp.floor((n_run * RUN_ALIGN + (STAGE_ALIGN - 1)) * (1.0 / STAGE_ALIGN))
    off = jnp.dot(jnp.broadcast_to(n_stage, (SUBLANES, LANES)).astype(BF16), triu_ref[...],
                  preferred_element_type=F32)[0:1, :] * STAGE_ALIGN
    where_in_stage = off + before
    route = jnp.zeros((tm, LANES), F32)
    for k in range(TOP_K_EXPERTS):
        spos = jnp.sum(jnp.where(lane == idxs[k], where_in_stage, 0.0), axis=-1, keepdims=True)
        route = jnp.where(lane == k, idxs[k].astype(F32), route)
        route = jnp.where(lane == TOP_K_EXPERTS + k, es[k] / den, route)
        route = jnp.where(lane == 2 * TOP_K_EXPERTS + k, spos, route)
    route_o[...] = route
    sub = lax.broadcasted_iota(I32, (SUBLANES, LANES), 0)
    tbl_o[...] = jnp.where(sub == 0, n_run, jnp.where(sub == 1, off, jnp.where(sub == 2, carry[0:1, :], 0.0)))
    carry[0:1, :] = carry[0:1, :] + n_run * RUN_ALIGN
    cnt_o[...] = carry[...]


def _mix_out(o_rwkv, o_dsa, x, gt1, sc2, sh2, w_out, ln_g, ln_b, w_router, b_router):
    B, S, D = x.shape
    tm = min(TM_PROJ, S)
    alpha = 2.0 ** 0.25
    wtop = w_out[:D_RWKV].astype(BF16)
    wbot = w_out[D_RWKV:].astype(BF16)
    wr_p = jnp.pad(w_router, ((0, 0), (0, LANES - N_EXPERTS)))
    br_p = jnp.pad(b_router.reshape(1, -1), ((0, 0), (0, LANES - N_EXPERTS)), constant_values=NEG_BIG)
    tril = jnp.asarray(np.tril(np.ones((tm, tm)), -1), BF16)
    triu = jnp.asarray(np.triu(np.ones((LANES, LANES)), 1), BF16)
    row = lambda v: v.reshape(1, -1)
    tok = lambda n: pl.BlockSpec((None, tm, n), lambda b, i: (b, i, 0))
    mod = pl.BlockSpec((None, 1, D), lambda b, i: (b, 0, 0))
    full = lambda a: pl.BlockSpec(a.shape, lambda b, i: (0,) * a.ndim)
    consts = [wtop, wbot, row(ln_g), row(ln_b), wr_p, br_p, tril, triu]
    return pl.pallas_call(
        functools.partial(_mix_kernel, alpha=alpha),
        grid=(B, S // tm),
        in_specs=[tok(D_RWKV), tok(D_ATT), tok(D), mod, mod, mod] + [full(a) for a in consts],
        out_specs=[tok(D), tok(D), tok(LANES), pl.BlockSpec((None, None, SUBLANES, LANES), lambda b, i: (b, i, 0, 0)),
                   pl.BlockSpec((SUBLANES, LANES), lambda b, i: (0, 0))],
        out_shape=[jax.ShapeDtypeStruct((B, S, D), F32), jax.ShapeDtypeStruct((B, S, D), F32),
                   jax.ShapeDtypeStruct((B, S, LANES), F32),
                   jax.ShapeDtypeStruct((B, S // tm, SUBLANES, LANES), F32),
                   jax.ShapeDtypeStruct((SUBLANES, LANES), F32)],
        scratch_shapes=[pltpu.VMEM((8, LANES), F32)],
        compiler_params=_cparams(("arbitrary", "arbitrary")),
        name="mix_out",
    )(o_rwkv, o_dsa, x, gt1, sc2, sh2, *consts)


RUN_ALIGN = SUBLANES
STAGE_ALIGN = 16
STAGE_ROWS = 1536
STAGE_CHUNK = 512


def _run_copies(nrun_ref, tile, copy_of):
    for e in range(N_EXPERTS):
        pieces = (nrun_ref[tile * N_EXPERTS + e] * RUN_ALIGN + STAGE_ALIGN - 1) // STAGE_ALIGN

        def piece(j, carry, e=e):
            copy_of(e, j).start(priority=e % 2)
            return carry

        lax.fori_loop(0, pieces, piece, 0)


def _dispatch_kernel(nrun_ref, off_ref, start_ref, npiece_ref, nstage_ref, zs_ref, zn_ref, tail_ref,
                     u_ref, route_ref, xs_out, stag, zeros, sem, zsem):
    i = pl.program_id(0)
    n = pl.num_programs(0)
    tm = u_ref.shape[0]
    bm = zeros.shape[0]
    slot = i % 2

    @pl.when(i == 0)
    def _():
        zeros[...] = jnp.zeros_like(zeros)
        fills = []
        for e in range(N_EXPERTS):
            for b in range(bm.bit_length()):
                rows = RUN_ALIGN << b
                if rows > bm:
                    break
                done = (zn_ref[e] >> (b + 1)) << (b + 1)
                dst = pl.multiple_of(zs_ref[e] + done * RUN_ALIGN, RUN_ALIGN)
                fills.append(((zn_ref[e] >> b) & 1 == 1,
                              pltpu.make_async_copy(zeros.at[pl.ds(0, rows)], xs_out.at[pl.ds(dst, rows)], zsem)))
        for pred, cp in fills:
            pl.when(pred)(cp.start)

        def tail_copy(j):
            return pltpu.make_async_copy(zeros, xs_out.at[pl.ds(pl.multiple_of(j * bm, bm), bm)], zsem)

        lax.fori_loop(tail_ref[0], tail_ref[1], lambda j, c: (tail_copy(j).start(), c)[1], 0)
        for pred, cp in fills:
            pl.when(pred)(cp.wait)
        lax.fori_loop(tail_ref[0], tail_ref[1], lambda j, c: (tail_copy(j).wait(), c)[1], 0)

    route_t = jnp.transpose(route_ref[...])
    spos = [route_t[2 * TOP_K_EXPERTS + k:2 * TOP_K_EXPERTS + k + 1, :].astype(I32) for k in range(TOP_K_EXPERTS)]
    ub = u_ref[...].astype(BF16)
    srow = lax.broadcasted_iota(I32, (STAGE_CHUNK, tm), 0)
    for c in range(STAGE_ROWS // STAGE_CHUNK):
        @pl.when(c * STAGE_CHUNK < nstage_ref[i])
        def _(c=c):
            rows = srow + c * STAGE_CHUNK
            sel = (rows == spos[0]) | (rows == spos[1]) | (rows == spos[2]) | (rows == spos[3])
            stag[slot, c * STAGE_CHUNK:(c + 1) * STAGE_CHUNK, :] = jnp.dot(
                jnp.where(sel, 1.0, 0.0).astype(BF16), ub, preferred_element_type=F32)

    def piece_copy(s, tile):
        def copy_of(e, j):
            src = pl.multiple_of(off_ref[tile * N_EXPERTS + e] + j * STAGE_ALIGN, STAGE_ALIGN)
            dst = pl.multiple_of(start_ref[tile * N_EXPERTS + e] + j * STAGE_ALIGN, RUN_ALIGN)
            return pltpu.make_async_copy(stag.at[s, pl.ds(src, STAGE_ALIGN)], xs_out.at[pl.ds(dst, STAGE_ALIGN)], sem)
        return copy_of

    def drain(tile):
        def w(j, carry):
            pltpu.make_async_copy(stag.at[0, pl.ds(0, STAGE_ALIGN)], xs_out.at[pl.ds(0, STAGE_ALIGN)], sem).wait()
            return carry
        lax.fori_loop(0, npiece_ref[tile], w, 0)

    @pl.when(i > 0)
    def _():
        drain(i - 1)

    _run_copies(nrun_ref, i, piece_copy(slot, i))

    @pl.when(i == n - 1)
    def _():
        drain(i)


def _moe_dispatch(u2, route, tables, n_rows):
    T, D = u2.shape
    tm = TM_ROUTE
    bm = BM_EXPERT
    return pl.pallas_call(
        _dispatch_kernel,
        grid_spec=pltpu.PrefetchScalarGridSpec(
            num_scalar_prefetch=len(tables),
            grid=(T // tm,),
            in_specs=[pl.BlockSpec((tm, D), lambda i, *_: (i, 0)),
                      pl.BlockSpec((tm, LANES), lambda i, *_: (i, 0))],
            out_specs=pl.BlockSpec(memory_space=pl.ANY),
            scratch_shapes=[pltpu.VMEM((2, STAGE_ROWS, D), F32), pltpu.VMEM((bm, D), F32),
                            pltpu.SemaphoreType.DMA(()), pltpu.SemaphoreType.DMA(())],
        ),
        out_shape=jax.ShapeDtypeStruct((n_rows, D), F32),
        compiler_params=_cparams(("arbitrary",)),
        name="moe_dispatch",
    )(*tables, u2, route)


GU_GROUP = 2 * LANES


def _deinterleave_perm():
    p = np.zeros((GU_GROUP, GU_GROUP), np.float32)
    l = np.arange(LANES)
    p[2 * l, l] = 1.0
    p[2 * l + 1, LANES + l] = 1.0
    return jnp.asarray(p, BF16)


def _expert_kernel(be_ref, nb_ref, valid_ref, xs_ref, wgu_hbm, bgu_ref, wd_hbm, bd_ref, perm_ref, ys_ref,
                   wg_buf, wd_buf, wp, wdb, sem):
    i = pl.program_id(0)
    bm = xs_ref.shape[0]
    e = be_ref[i]
    used = i < nb_ref[0]
    new_expert = (i == 0) | (e != be_ref[jnp.maximum(i - 1, 0)])
    n_groups = wp.shape[1] // GU_GROUP
    n_experts = wgu_hbm.shape[0]

    def fetch(ex):
        return (pltpu.make_async_copy(wgu_hbm.at[ex], wg_buf, sem.at[0]),
                pltpu.make_async_copy(wd_hbm.at[ex], wd_buf, sem.at[1]))

    @pl.when(used & new_expert)
    def _():
        @pl.when(i == 0)
        def _():
            for cp in fetch(e):
                cp.start()

        for cp in fetch(e):
            cp.wait()
        for j in range(n_groups):
            sl = slice(j * GU_GROUP, (j + 1) * GU_GROUP)
            wp[:, sl] = jnp.dot(wg_buf[:, sl].astype(BF16), perm_ref[...], preferred_element_type=F32).astype(BF16)
        wdb[...] = wd_buf[...].astype(BF16)

        @pl.when(e + 1 < n_experts)
        def _():
            for cp in fetch(e + 1):
                cp.start()

    def compute(m):
        xb = xs_ref[:m, :].astype(BF16)
        gu = jnp.dot(xb, wp[...], preferred_element_type=F32) + bgu_ref[...]
        hs = []
        for j in range(n_groups):
            gate = jnp.minimum(gu[:, j * GU_GROUP:j * GU_GROUP + LANES], SWIGLU_LIMIT)
            up = jnp.clip(gu[:, j * GU_GROUP + LANES:(j + 1) * GU_GROUP], -SWIGLU_LIMIT, SWIGLU_LIMIT)
            hs.append(((up + 1.0) * (gate * _sigmoid(gate * SWIGLU_ALPHA))).astype(BF16))
        h = jnp.concatenate(hs, axis=-1)
        ys_ref[:m, :] = jnp.dot(h, wdb[...], preferred_element_type=F32) + bd_ref[...]

    sizes = (bm, bm // 2, bm // 4)
    for m, smaller in zip(sizes, sizes[1:] + (0,)):
        @pl.when(used & (valid_ref[i] <= m) & (valid_ref[i] > smaller))
        def _(m=m):
            compute(m)
            if m < bm:
                ys_ref[m:, :] = jnp.zeros((bm - m, ys_ref.shape[1]), F32)

    @pl.when(used & (valid_ref[i] <= 0))
    def _():
        ys_ref[...] = jnp.zeros_like(ys_ref)

    @pl.when(jnp.logical_not(used))
    def _():
        ys_ref[...] = jnp.zeros_like(ys_ref)


def _moe_experts(xs, block_e, n_used, valid, w_gu, b_gu_p, w_dn, b_dn):
    n_rows, D = xs.shape
    E, _, F2 = w_gu.shape
    bm = BM_EXPERT
    n_blocks = n_rows // bm
    perm = _deinterleave_perm()
    wspec = lambda shp: pl.BlockSpec((None,) + shp, lambda i, be, nb, va: (be[i], 0, 0))
    hbm = pl.BlockSpec(memory_space=pl.ANY)
    return pl.pallas_call(
        _expert_kernel,
        grid_spec=pltpu.PrefetchScalarGridSpec(
            num_scalar_prefetch=3,
            grid=(n_blocks,),
            in_specs=[pl.BlockSpec((bm, D), lambda i, be, nb, va: (jnp.minimum(i, nb[0] - 1), 0)),
                      hbm, wspec((1, F2)), hbm, wspec((1, D)),
                      pl.BlockSpec(perm.shape, lambda i, be, nb, va: (0, 0))],
            out_specs=pl.BlockSpec((bm, D), lambda i, be, nb, va: (i, 0)),
            scratch_shapes=[pltpu.VMEM((D, F2), F32), pltpu.VMEM((F2 // 2, D), F32),
                            pltpu.VMEM((D, F2), BF16), pltpu.VMEM((F2 // 2, D), BF16),
                            pltpu.SemaphoreType.DMA((2,))],
        ),
        out_shape=jax.ShapeDtypeStruct((n_rows, D), F32),
        compiler_params=pltpu.CompilerParams(dimension_semantics=("arbitrary",), vmem_limit_bytes=VMEM_LIMIT_EXPERTS),
        name="moe_experts",
    )(block_e, n_used, valid, xs, w_gu, b_gu_p, w_dn, b_dn, perm)


def _combine_kernel(nrun_ref, off_ref, start_ref, npiece_ref, nstage_ref, ys_ref, x1_ref, route_ref, gt_ref, g_ref,
                    b_ref, o_ref, stag, sem, *, alpha):
    i = pl.program_id(0)
    n = pl.num_programs(0)
    tm = x1_ref.shape[0]
    slot = i % 2

    def gather(tile, s):
        def copy_of(e, j):
            src = pl.multiple_of(start_ref[tile * N_EXPERTS + e] + j * STAGE_ALIGN, RUN_ALIGN)
            dst = pl.multiple_of(off_ref[tile * N_EXPERTS + e] + j * STAGE_ALIGN, STAGE_ALIGN)
            return pltpu.make_async_copy(ys_ref.at[pl.ds(src, STAGE_ALIGN)], stag.at[s, pl.ds(dst, STAGE_ALIGN)],
                                         sem.at[s])
        _run_copies(nrun_ref, tile, copy_of)

    @pl.when(i == 0)
    def _():
        stag[...] = jnp.zeros_like(stag)
        gather(0, 0)

    @pl.when(i + 1 < n)
    def _():
        gather(i + 1, 1 - slot)

    def w(j, carry):
        pltpu.make_async_copy(ys_ref.at[pl.ds(0, STAGE_ALIGN)], stag.at[slot, pl.ds(0, STAGE_ALIGN)],
                              sem.at[slot]).wait()
        return carry
    lax.fori_loop(0, npiece_ref[i], w, 0)

    route = route_ref[...]
    wide = lambda col: jnp.broadcast_to(col, (tm, STAGE_CHUNK))
    spos = [wide(route[:, 2 * TOP_K_EXPERTS + k:2 * TOP_K_EXPERTS + k + 1].astype(I32)) for k in range(TOP_K_EXPERTS)]
    gate = [wide(route[:, TOP_K_EXPERTS + k:TOP_K_EXPERTS + k + 1]) for k in range(TOP_K_EXPERTS)]
    scol = lax.broadcasted_iota(I32, (tm, STAGE_CHUNK), 1)
    acc_ref = o_ref
    acc_ref[...] = jnp.zeros_like(acc_ref)
    for c in range(STAGE_ROWS // STAGE_CHUNK):
        @pl.when(c * STAGE_CHUNK < nstage_ref[i])
        def _(c=c):
            cols = scol + c * STAGE_CHUNK
            wgt = jnp.zeros((tm, STAGE_CHUNK), F32)
            for k in range(TOP_K_EXPERTS):
                wgt = wgt + jnp.where(cols == spos[k], gate[k], 0.0)
            acc_ref[...] += jnp.dot(wgt.astype(BF16), stag[slot, c * STAGE_CHUNK:(c + 1) * STAGE_CHUNK, :].astype(BF16),
                                    preferred_element_type=F32)
    o_ref[...] = _layernorm_rows(alpha * x1_ref[...] + (1.0 + gt_ref[...]) * acc_ref[...], g_ref[...], b_ref[...])


def _moe_combine(ys, tables, x1, route, gt2, ln_g, ln_b, tiles_per_batch):
    T, D = x1.shape
    tm = TM_ROUTE
    row = lambda v: v.reshape(1, -1)
    return pl.pallas_call(
        functools.partial(_combine_kernel, alpha=2.0 ** 0.25),
        grid_spec=pltpu.PrefetchScalarGridSpec(
            num_scalar_prefetch=len(tables),
            grid=(T // tm,),
            in_specs=[pl.BlockSpec(memory_space=pl.ANY),
                      pl.BlockSpec((tm, D), lambda i, *_: (i, 0)),
                      pl.BlockSpec((tm, LANES), lambda i, *_: (i, 0)),
                      pl.BlockSpec((None, 1, D), lambda i, *_: (i // tiles_per_batch, 0, 0)),
                      pl.BlockSpec((1, D), lambda i, *_: (0, 0)),
                      pl.BlockSpec((1, D), lambda i, *_: (0, 0))],
            out_specs=pl.BlockSpec((tm, D), lambda i, *_: (i, 0)),
            scratch_shapes=[pltpu.VMEM((2, STAGE_ROWS, D), F32), pltpu.SemaphoreType.DMA((2,))],
        ),
        out_shape=jax.ShapeDtypeStruct((T, D), F32),
        compiler_params=_cparams(("arbitrary",)),
        name="moe_combine",
    )(*tables, ys, x1, route, gt2, row(ln_g), row(ln_b))


def _moe_and_norm(x1, u2, route, tbl, totals, gt2, w_gu, b_gu, w_dn, b_dn, ln_g, ln_b):
    B, S, D = x1.shape
    T = B * S
    bm = BM_EXPERT
    tm = TM_ROUTE
    assert T % tm == 0 and TM_PROJ == tm and STAGE_ROWS >= tm * TOP_K_EXPERTS + N_EXPERTS * (STAGE_ALIGN - 1)
    n_tiles = T // tm
    max_rows = T * TOP_K_EXPERTS + n_tiles * N_EXPERTS * (RUN_ALIGN - 1) + N_EXPERTS * STAGE_ALIGN
    n_blocks = -(-max_rows // bm) + N_EXPERTS
    tot = totals[0, :N_EXPERTS].astype(I32)
    padded = (tot + STAGE_ALIGN + bm - 1) // bm * bm
    pad_ends = jnp.cumsum(padded)
    pad_starts = pad_ends - padded
    t3 = tbl.reshape(n_tiles, SUBLANES, LANES)[:, :, :N_EXPERTS].astype(I32)
    nrun, off, base = t3[:, 0, :], t3[:, 1, :], t3[:, 2, :]
    start = pad_starts[None, :] + base
    pieces = (nrun * RUN_ALIGN + STAGE_ALIGN - 1) // STAGE_ALIGN
    npiece = jnp.sum(pieces, axis=1)
    nstage = jnp.sum(pieces, axis=1) * STAGE_ALIGN
    flat = lambda z: z.reshape(-1).astype(I32)
    run_tables = [flat(nrun), flat(off), flat(start), flat(npiece), flat(nstage)]
    zs = pad_starts + tot
    zn = (pad_ends - zs) // RUN_ALIGN
    n_used = (pad_ends[-1:] // bm).astype(I32)
    tail = jnp.concatenate([n_used, jnp.full((1,), n_blocks, I32)])
    blk_row = jnp.arange(n_blocks, dtype=I32) * bm
    block_e = jnp.minimum(jnp.sum((blk_row[:, None] >= pad_ends[None, :]).astype(I32), axis=1), N_EXPERTS - 1)
    E, F2 = b_gu.shape
    b_gu_p = b_gu.reshape(E, F2 // GU_GROUP, LANES, 2).transpose(0, 1, 3, 2).reshape(E, 1, F2)
    route2 = route.reshape(T, LANES)
    xs = _moe_dispatch(u2.reshape(T, D), route2, run_tables + [flat(zs), flat(zn), tail], n_blocks * bm)
    valid = jnp.clip(zs[block_e] - blk_row, 0, bm)
    ys = _moe_experts(xs, block_e, n_used, valid, w_gu, b_gu_p, w_dn, b_dn[:, None, :])
    out = _moe_combine(ys, run_tables, x1.reshape(T, D), route2, gt2, ln_g, ln_b, S // tm)
    return out.reshape(B, S, D)


def kernel(x, c, w_ada, b_ada, w_in, shift_mu, rwkv_w0, rwkv_w2, rwkv_a0, rwkv_a2, rwkv_g2, rwkv_k_k, rwkv_k_a, rwkv_r_k, rwkv_ln_w, rwkv_ln_b, mla_q_norm, mla_w_q_up, mla_kv_norm, mla_w_uk, mla_w_uv, idx_w_q, idx_ln_g, idx_ln_b, w_out, ln1_g, ln1_b, w_router, b_router, w_gu, b_gu, w_dn, b_dn, ln2_g, ln2_b):
    depth = w_ada.shape[0]
    assert depth == 1, "DeepNorm constants below are for a single layer"
    l = 0
    mod = _ada_mod(c, w_ada[l], b_ada[l])
    sh1, sc1, gt1, sh2, sc2, gt2 = [m[:, None, :] for m in jnp.split(mod, 6, axis=-1)]
    r, lw, k, v, kk, a, g, qabs, iq, ik, iw, cl, clt = _in_proj(
        x, sc1, sh1, w_in[l], shift_mu[l], rwkv_w0[l], rwkv_w2[l], rwkv_a0[l], rwkv_a2[l], rwkv_g2[l], rwkv_k_k[l],
        rwkv_k_a[l], mla_q_norm[l], mla_w_q_up[l], mla_w_uk[l], mla_kv_norm[l], idx_w_q[l], idx_ln_g[l], idx_ln_b[l])
    o_rwkv = _rwkv_scan(r, lw, k, v, kk, a, g, rwkv_r_k[l], rwkv_ln_w[l], rwkv_ln_b[l])
    o_dsa = _dsa_attn(iq, iw, qabs, ik, cl, clt, mla_w_uv[l])
    x1, u2, route, tbl, totals = _mix_out(o_rwkv, o_dsa, x, gt1, sc2, sh2, w_out[l], ln1_g[l], ln1_b[l],
                                          w_router[l], b_router[l])
    return _moe_and_norm(x1, u2, route, tbl, totals, gt2, w_gu[l], b_gu[l], w_dn[l], b_dn[l], ln2_g[l], ln2_b[l])
```

```python
import functools
import math

import jax
import jax.numpy as jnp
import numpy as np
from jax import lax
from jax.experimental import pallas as pl
from jax.experimental.pallas import tpu as pltpu

F32 = jnp.float32
BF16 = jnp.bfloat16
I32 = jnp.int32

RWKV_HEAD = 64
N_RWKV_HEADS = 8
D_RWKV = RWKV_HEAD * N_RWKV_HEADS
RWKV_GN_EPS = 64e-5
ATT_HEAD = 64
N_ATT_HEADS = 8
D_ATT = ATT_HEAD * N_ATT_HEADS
KV_LORA = 128
IDX_HEADS = 8
IDX_DIM = 64
TOPK_MAX = 256
N_EXPERTS = 32
TOP_K_EXPERTS = 4
SWIGLU_LIMIT = 7.0
SWIGLU_ALPHA = 1.702
NEG_BIG = -1e30
LOG2E = 1.4426950408889634
INT_MIN = -(2 ** 31)

LANES = 128
SUBLANES = 8
VMEM_LIMIT = 48 * 1024 * 1024
VMEM_LIMIT_EXPERTS = 58 * 1024 * 1024

TM_IN = 512
TM_PROJ = 256
L_CHUNK = 64
CHUNKS_PER_STEP = 4
TQ = 256
KEY_CHUNK = 256
SUM_ROWS = 128
TM_ROUTE = 256
BM_EXPERT = 1024

_SEG = (("r", 512, 512), ("k", 512, 512), ("v", 512, 512), ("w", 64, 128), ("a", 64, 128), ("g", 128, 128),
        ("q", 256, 256), ("kv", 128, 128), ("ik", 64, 128), ("iw", 8, 128))
N_SHIFT_P = 512 * 3 + 128 * 3
N_IN_P = sum(s[2] for s in _SEG)


def _cparams(sem):
    return pltpu.CompilerParams(dimension_semantics=sem, vmem_limit_bytes=VMEM_LIMIT)


def _bdot(a, b):
    return jnp.dot(a.astype(BF16), b.astype(BF16), preferred_element_type=F32)


def _split2(a):
    hi = a.astype(BF16)
    lo = (a - hi.astype(F32)).astype(BF16)
    return hi, lo


def _split3(a):
    hi = a.astype(BF16)
    r1 = a - hi.astype(F32)
    mid = r1.astype(BF16)
    lo = (r1 - mid.astype(F32)).astype(BF16)
    return hi, mid, lo


def _dot3(a, b, dims=(((1,), (0,)), ((), ()))):
    ah, al = _split2(a)
    bh, bl = _split2(b)
    d = functools.partial(lax.dot_general, dimension_numbers=dims, preferred_element_type=F32)
    return d(ah, bh) + (d(ah, bl) + d(al, bh))


def _dot_exact_rhs(a, b_exact, nsplit=3):
    parts = _split3(a) if nsplit == 3 else _split2(a)
    acc = None
    for p in parts[::-1]:
        t = jnp.dot(p, b_exact, preferred_element_type=F32)
        acc = t if acc is None else acc + t
    return acc


def _dot_exact_lhs(a_exact, b, nsplit=3):
    parts = _split3(b) if nsplit == 3 else _split2(b)
    acc = None
    for p in parts[::-1]:
        t = jnp.dot(a_exact, p, preferred_element_type=F32)
        acc = t if acc is None else acc + t
    return acc


def _sigmoid(x):
    return 1.0 / (1.0 + jnp.exp(-x))


def _softplus(x):
    return jnp.maximum(x, 0.0) + jnp.log(1.0 + jnp.exp(-jnp.abs(x)))


def _ada_kernel(c_ref, w_ref, b_ref, o_ref):
    c = c_ref[...]
    o_ref[...] = _dot3(c * _sigmoid(c), w_ref[...]) + b_ref[...]


def _ada_mod(c, w_ada, b_ada):
    B, D = c.shape
    N = w_ada.shape[1]
    tn = 1024
    return pl.pallas_call(
        _ada_kernel,
        grid=(N // tn,),
        in_specs=[pl.BlockSpec((B, D), lambda j: (0, 0)),
                  pl.BlockSpec((D, tn), lambda j: (0, j)),
                  pl.BlockSpec((1, tn), lambda j: (0, j))],
        out_specs=pl.BlockSpec((B, tn), lambda j: (0, j)),
        out_shape=jax.ShapeDtypeStruct((B, N), F32),
        compiler_params=_cparams(("arbitrary",)),
        name="ada_mod",
    )(c, w_ada, b_ada.reshape(1, N))


def _in_proj_kernel(x_ref, sc_ref, sh_ref, win_ref, mu_ref, w0_ref, w2_ref, a0_ref, a2_ref, g2_ref, kk_ref, ka_ref,
                    ones_ref, qn_ref, wq_ref, wuk_ref, kvn_ref, wiq_ref, ig_ref, ib_ref,
                    r_o, lw_o, k_o, v_o, kkn_o, a_o, g_o, qabs_o, iq_o, ik_o, iw_o, cl_o, clt_o, carry):
    i = pl.program_id(1)
    tm = x_ref.shape[0]

    @pl.when(i == 0)
    def _():
        carry[...] = jnp.zeros_like(carry)

    u = x_ref[...] * (1.0 + sc_ref[...]) + sh_ref[...]
    p = _bdot(u, win_ref[...])
    ps = p[:, :N_SHIFT_P]
    rows = lax.broadcasted_iota(I32, (tm, 1), 0)
    prev = jnp.where(rows == 0, carry[0:1, :], pltpu.roll(ps, 1, 0))
    carry[0:1, :] = ps[tm - 1:tm, :]
    ps = ps + mu_ref[...] * (prev - ps)

    pr, pk, pv = ps[:, 0:512], ps[:, 512:1024], ps[:, 1024:1536]
    pw, pa, pg = ps[:, 1536:1664], ps[:, 1664:1792], ps[:, 1792:1920]
    w_log = -_softplus(-(w0_ref[...] + _dot3(jnp.tanh(pw), w2_ref[...]))) - 0.5
    lw_o[...] = -jnp.exp(w_log)
    a = _sigmoid(a0_ref[...] + _dot3(pa, a2_ref[...]))
    g_o[...] = _dot3(_sigmoid(pg), g2_ref[...])
    kk = pk * kk_ref[...]
    ssq = _dot_exact_rhs(kk * kk, ones_ref[...], nsplit=2)
    kkn_o[...] = kk / jnp.maximum(jnp.sqrt(ssq), 1e-12)
    k_o[...] = pk * (1.0 + (a - 1.0) * ka_ref[...])
    r_o[...] = pr
    v_o[...] = pv
    a_o[...] = a

    pq, pkv = p[:, 1920:2176], p[:, 2176:2304]
    pik, piw = p[:, 2304:2432], p[:, 2432:2560]
    q_lat = pq * lax.rsqrt(jnp.mean(pq * pq, axis=-1, keepdims=True) + 1e-6) * qn_ref[...]
    q = _bdot(q_lat, wq_ref[...])
    qabs_o[...] = (_bdot(q, wuk_ref[...]) * (ATT_HEAD ** -0.5 * LOG2E)).astype(BF16)
    c_lat = pkv * lax.rsqrt(jnp.mean(pkv * pkv, axis=-1, keepdims=True) + 1e-6) * kvn_ref[...]
    spos = i * tm + rows
    s_hi = (spos >> 7).astype(F32)
    s_lo = (spos & (LANES - 1)).astype(F32)
    lane_t = lax.broadcasted_iota(I32, (1, LANES), 1)
    extra = jnp.where(lane_t == 0, 1.0, jnp.where((lane_t == 1) | (lane_t == 2), s_hi,
                                                  jnp.where((lane_t == 3) | (lane_t == 4), s_lo, 0.0)))
    c_aug = jnp.concatenate([c_lat, extra], axis=-1)
    cl_o[...] = c_aug.astype(BF16)
    clt_o[...] = jnp.transpose(c_aug).astype(BF16)
    iq_o[...] = (_bdot(q_lat, wiq_ref[...]) * (IDX_DIM ** -0.5)).astype(BF16)
    lane = lax.broadcasted_iota(I32, (1, LANES), 1)
    valid = lane < IDX_DIM
    mu = jnp.sum(pik, axis=-1, keepdims=True) * (1.0 / IDX_DIM)
    dlt = jnp.where(valid, pik - mu, 0.0)
    var = jnp.sum(dlt * dlt, axis=-1, keepdims=True) * (1.0 / IDX_DIM)
    ik_o[...] = jnp.where(valid, dlt * lax.rsqrt(var + 1e-5) * ig_ref[...] + ib_ref[...], 0.0).astype(BF16)
    iw_o[...] = piw * (IDX_HEADS ** -0.5)


def _pad_cols(w, widths):
    parts, o = [], 0
    for true, padded in widths:
        seg = w[..., o:o + true]
        if padded > true:
            seg = jnp.pad(seg, [(0, 0)] * (w.ndim - 1) + [(0, padded - true)])
        parts.append(seg)
        o += true
    return jnp.concatenate(parts, axis=-1)


def _pad_rows(w, rows):
    return jnp.pad(w, ((0, rows - w.shape[0]), (0, 0)))


def _block_diag(blocks):
    H, a, b = blocks.shape
    eye = jnp.eye(H, dtype=blocks.dtype)
    return (eye[:, None, :, None] * blocks[:, :, None, :]).reshape(H * a, H * b)


def _head_ones(n, head):
    idx = np.arange(n) // head
    return jnp.asarray(idx[:, None] == idx[None, :], BF16)


def _in_proj(x, sc1, sh1, w_in, shift_mu, w0, w2, a0, a2, g2, k_k, k_a, q_norm, w_q_up, w_uk, kv_norm, idx_w_q,
             idx_ln_g, idx_ln_b):
    B, S, D = x.shape
    tm = min(TM_IN, S)
    widths = tuple((s[1], s[2]) for s in _SEG)
    win_p = _pad_cols(w_in, widths).astype(BF16)
    mu_p = _pad_cols(shift_mu.reshape(1, -1), widths[:6])
    w2_p = _pad_rows(w2, LANES)
    a2_p = _pad_rows(a2, LANES)
    wuk_bd = _block_diag(w_uk).astype(BF16)
    wiq_p = _pad_cols(idx_w_q, ((IDX_DIM, LANES),) * IDX_HEADS).astype(BF16)
    ig_p = _pad_cols(idx_ln_g.reshape(1, -1), ((IDX_DIM, LANES),))
    ib_p = _pad_cols(idx_ln_b.reshape(1, -1), ((IDX_DIM, LANES),))
    row = lambda v: v.reshape(1, -1)
    tok = lambda n: pl.BlockSpec((None, tm, n), lambda b, i: (b, i, 0))
    mod = pl.BlockSpec((None, 1, D), lambda b, i: (b, 0, 0))
    full = lambda a: pl.BlockSpec(a.shape, lambda b, i: (0,) * a.ndim)
    consts = [win_p, mu_p, row(w0), w2_p, row(a0), a2_p, g2, row(k_k), row(k_a), _head_ones(D_RWKV, RWKV_HEAD),
              row(q_norm), w_q_up.astype(BF16), wuk_bd, row(kv_norm), wiq_p, ig_p, ib_p]
    outs = [(D_RWKV, F32)] * 7 + [(N_ATT_HEADS * KV_LORA, BF16), (IDX_HEADS * LANES, BF16), (LANES, BF16),
                                  (LANES, F32), (KV_LORA + LANES, BF16)]
    return pl.pallas_call(
        _in_proj_kernel,
        grid=(B, S // tm),
        in_specs=[tok(D), mod, mod] + [full(a) for a in consts],
        out_specs=[tok(n) for n, _ in outs] + [pl.BlockSpec((None, KV_LORA + LANES, tm), lambda b, i: (b, 0, i))],
        out_shape=[jax.ShapeDtypeStruct((B, S, n), dt) for n, dt in outs]
        + [jax.ShapeDtypeStruct((B, KV_LORA + LANES, S), BF16)],
        scratch_shapes=[pltpu.VMEM((8, N_SHIFT_P), F32)],
        compiler_params=_cparams(("arbitrary", "arbitrary")),
        name="in_proj",
    )(x, sc1, sh1, *consts)


def _rwkv_kernel(r_ref, lw_ref, k_ref, v_ref, kk_ref, a_ref, g_ref, rk_ref, lnw_ref, lnb_ref, tri_ref, ones_ref,
                 o_ref, state, *, L):
    c = pl.program_id(1)
    nc = r_ref.shape[0] // L

    @pl.when(c == 0)
    def _():
        state[...] = jnp.zeros_like(state)

    r, lw, k, v, kk, a = r_ref[...], lw_ref[...], k_ref[...], v_ref[...], kk_ref[...], a_ref[...]
    cum = _dot_exact_lhs(tri_ref[...], lw)
    rows = [slice(ci * L, (ci + 1) * L) for ci in range(nc)]
    last = [cum[(ci + 1) * L - 1:(ci + 1) * L, :] for ci in range(nc)]
    cum_last = jnp.concatenate([jnp.broadcast_to(z, (L, z.shape[1])) for z in last], axis=0)
    w_incl = jnp.exp(cum)
    w_inv = jnp.exp(-cum)
    w_rel = jnp.exp(cum_last - cum)
    w_last = [jnp.exp(z) for z in last]
    bvec = kk * a
    at = -kk * jnp.exp(cum - lw)
    rt = r * w_incl
    bt = bvec * w_inv
    kt = k * w_inv
    bh = bvec * w_rel
    kh = k * w_rel
    ti = lax.broadcasted_iota(I32, (L, L), 0)
    tj = lax.broadcasted_iota(I32, (L, L), 1)
    strict = tj < ti
    incl = tj <= ti
    eye = lax.broadcasted_iota(I32, (RWKV_HEAD, RWKV_HEAD), 0) == lax.broadcasted_iota(I32, (RWKV_HEAD, RWKV_HEAD), 1)
    NT = (((1,), (1,)), ((), ()))
    TN = (((0,), (0,)), ((), ()))
    heads = range(N_RWKV_HEADS)
    sls = [slice(h * RWKV_HEAD, (h + 1) * RWKV_HEAD) for h in heads]
    units = [(ci, h) for ci in range(nc) for h in heads]
    idx = range(len(units))
    mm = lambda x, y, dims=(((1,), (0,)), ((), ())): lax.dot_general(
        x.astype(BF16), y.astype(BF16), dims, preferred_element_type=F32)
    at_b, rt_b, bt_b, kt_b, bh_b, kh_b, v_b = [z.astype(BF16) for z in (at, rt, bt, kt, bh, kh, v)]
    cut = lambda z, u: z[rows[u[0]], sls[u[1]]]
    vh = [cut(v_b, u) for u in units]
    ath = [cut(at_b, u) for u in units]
    ar = [jnp.concatenate([cut(at_b, u), cut(rt_b, u)], axis=0) for u in units]
    g_b = [mm(ar[i], cut(bt_b, units[i]), NT) for i in idx]
    g_k = [mm(ar[i], cut(kt_b, units[i]), NT) for i in idx]
    n_ab = [jnp.where(strict, g_b[i][:L], 0.0) for i in idx]
    a_ak = [jnp.where(strict, g_k[i][:L], 0.0) for i in idx]
    a_rb = [jnp.where(incl, g_b[i][L:], 0.0).astype(BF16) for i in idx]
    a_rk = [jnp.where(incl, g_k[i][L:], 0.0) for i in idx]
    akv = [mm(a_ak[i], vh[i]) for i in idx]
    eye_l = jnp.where(ti == tj, 1.0, 0.0)
    tinv = [eye_l + n_ab[i] for i in idx]
    pw = n_ab
    for _ in range(int(math.log2(L)) - 1):
        pw = [mm(pw[i], pw[i]) for i in idx]
        tinv = [tinv[i] + mm(pw[i], tinv[i]) for i in idx]
    tinv = [t.astype(BF16) for t in tinv]
    a_t = [mm(tinv[i], ath[i]).astype(BF16) for i in idx]
    y = [mm(tinv[i], akv[i]).astype(BF16) for i in idx]
    m_c = [jnp.where(eye, w_last[units[i][0]][:, sls[units[i][1]]], 0.0) + mm(a_t[i], cut(bh_b, units[i]), TN)
           for i in idx]
    c_c = [mm(y[i], cut(bh_b, units[i]), TN) + mm(vh[i], cut(kh_b, units[i]), TN) for i in idx]
    q_c = [cut(rt, units[i]) + mm(a_rb[i], a_t[i]) for i in idx]
    o_loc = [mm(a_rb[i], y[i]) + mm(a_rk[i], vh[i]) for i in idx]
    s = [state[h] for h in heads]
    for ci in range(nc):
        for h in heads:
            i = ci * N_RWKV_HEADS + h
            o = o_loc[i] + mm(q_c[i], s[h], NT)
            s[h] = mm(s[h], m_c[i]) + c_c[i]
            mu = jnp.mean(o, axis=-1, keepdims=True)
            d = o - mu
            var = jnp.mean(d * d, axis=-1, keepdims=True)
            o_ref[rows[ci], sls[h]] = d * lax.rsqrt(var + RWKV_GN_EPS)
    for h in heads:
        state[h] = s[h]
    bonus = _dot_exact_rhs(r * k * rk_ref[...], ones_ref[...], nsplit=3) * v
    o_ref[...] = (o_ref[...] * lnw_ref[...] + lnb_ref[...] + bonus) * g_ref[...]


def _rwkv_scan(r, lw, k, v, kk, a, g, r_k, ln_w, ln_b):
    B, S, DR = r.shape
    L = min(L_CHUNK, S)
    lb = min(L * CHUNKS_PER_STEP, S)
    tri = jnp.asarray(np.kron(np.eye(lb // L), np.tril(np.ones((L, L)))), BF16)
    row = lambda z: z.reshape(1, -1)
    tok = pl.BlockSpec((None, lb, DR), lambda b, c: (b, c, 0))
    full = lambda z: pl.BlockSpec(z.shape, lambda b, c: (0,) * z.ndim)
    consts = [row(r_k), row(ln_w), row(ln_b), tri, _head_ones(DR, RWKV_HEAD)]
    return pl.pallas_call(
        functools.partial(_rwkv_kernel, L=L),
        grid=(B, S // lb),
        in_specs=[tok] * 7 + [full(z) for z in consts],
        out_specs=tok,
        out_shape=jax.ShapeDtypeStruct((B, S, DR), F32),
        scratch_shapes=[pltpu.VMEM((N_RWKV_HEADS, RWKV_HEAD, RWKV_HEAD), F32)],
        compiler_params=_cparams(("arbitrary", "arbitrary")),
        name="rwkv_scan",
    )(r, lw, k, v, kk, a, g, *consts)


def _alibi_cols():
    slope = np.asarray([2.0 ** (-8.0 * (h + 1) / N_ATT_HEADS) * LOG2E for h in range(N_ATT_HEADS)], np.float32)
    c_hi = slope.astype(BF16).astype(np.float32)
    c_lo = (slope - c_hi).astype(BF16).astype(np.float32)
    t = np.zeros((N_ATT_HEADS, LANES), np.float32)
    t[:, 1], t[:, 2], t[:, 3], t[:, 4] = LANES * c_hi, LANES * c_lo, c_hi, c_lo
    return jnp.asarray(t.astype(BF16))


def _dsa_kernel(iq_ref, iw_ref, qa_ref, ik_ref, ca_ref, cat_ref, wuv_ref, tril_ref, acol_ref, o_ref, key_ref,
                bias_ref, lg_ref, p_ref, *, topk, q_off, select):
    qi = pl.program_id(1) + q_off
    tq = qa_ref.shape[0]
    sk = ca_ref.shape[0]
    kc = min(KEY_CHUNK, sk)
    chunks = [slice(c * kc, (c + 1) * kc) for c in range(sk // kc)]
    tpos = qi * tq + lax.broadcasted_iota(I32, (1, tq), 1)
    srow = lax.broadcasted_iota(I32, (kc, tq), 0)
    NT = (((1,), (1,)), ((), ()))

    if not select:
        for c, cs in enumerate(chunks):
            bias_ref[cs, :] = jnp.where(srow + c * kc <= tpos, 0.0, NEG_BIG)
    else:
        iw_t = jnp.transpose(iw_ref[...])
        for c, cs in enumerate(chunks):
            ikc = ik_ref[cs, :]
            score = jnp.zeros((kc, tq), F32)
            for h in range(IDX_HEADS):
                s = lax.dot_general(ikc, iq_ref[:, h * LANES:(h + 1) * LANES], NT, preferred_element_type=F32)
                score = score + iw_t[h:h + 1, :] * jnp.maximum(s, 0.0)
            bits = pltpu.bitcast(score + 0.0, I32)
            key = bits ^ ((bits >> 31) & 0x7FFFFFFF)
            key_ref[cs, :] = jnp.where(srow + c * kc <= tpos, key, INT_MIN)
        kcount = jnp.minimum(topk, tpos + 1).astype(F32)

        def sum_keys(x):
            part = jnp.sum(x.reshape(x.shape[0] // SUM_ROWS, SUM_ROWS, x.shape[1]), axis=0)
            return jnp.sum(part, axis=0, keepdims=True)

        def count_ge(cand):
            return sum_keys(jnp.where(key_ref[...] >= cand, 1.0, 0.0))

        thr = jnp.where(count_ge(jnp.zeros((1, tq), I32)) >= kcount, 0, INT_MIN).astype(I32)

        def bit_step(i, thr):
            cand = thr | (1 << (30 - i))
            return jnp.where(count_ge(cand) >= kcount, cand, thr)

        thr = lax.fori_loop(0, 31, bit_step, thr)

        need = kcount - sum_keys(jnp.where(key_ref[...] > thr, 1.0, 0.0))
        carry = jnp.zeros((1, tq), F32)
        for j in range(sk // LANES):
            sl = slice(j * LANES, (j + 1) * LANES)
            key = key_ref[sl, :]
            eq = key == thr
            e = jnp.where(eq, 1.0, 0.0)
            before = jnp.dot(tril_ref[...], e.astype(BF16), preferred_element_type=F32) + carry
            sel = (key > thr) | (eq & (before < need))
            bias_ref[sl, :] = jnp.where(sel, 0.0, NEG_BIG)
            carry = carry + jnp.sum(e, axis=0, keepdims=True)

    outs = []
    for h in range(N_ATT_HEADS):
        q_aug = jnp.concatenate([qa_ref[:, h * KV_LORA:(h + 1) * KV_LORA],
                                 jnp.broadcast_to(acol_ref[h:h + 1, :], (tq, LANES))], axis=-1)
        lg_all = lax.dot_general(ca_ref[...], q_aug, NT, preferred_element_type=F32)
        m = jnp.full((1, tq), -jnp.inf, F32)
        for cs in chunks:
            lg = lg_all[cs, :] + bias_ref[cs, :]
            lg_ref[cs, :] = lg
            m = jnp.maximum(m, jnp.max(lg, axis=0, keepdims=True))
        for cs in chunks:
            p_ref[cs, :] = jnp.exp2(lg_ref[cs, :] - m).astype(BF16)
        pv = jnp.dot(cat_ref[...], p_ref[...], preferred_element_type=F32)
        outs.append((pv[:KV_LORA, :] / pv[KV_LORA:KV_LORA + 1, :]).astype(BF16))
    o_lat_t = jnp.concatenate(outs, axis=0)
    o_ref[...] = lax.dot_general(o_lat_t, wuv_ref[...], (((0,), (0,)), ((), ())), preferred_element_type=F32)


def _dsa_attn(iq, iw, qabs, ik, ca, cat, w_uv):
    B, S, _ = iq.shape
    tq = min(TQ, S)
    topk = min(TOPK_MAX, S // 4)
    wuv_bd = _block_diag(w_uv).astype(BF16)
    tril = jnp.asarray(np.tril(np.ones((LANES, LANES)), -1), BF16)
    acol = _alibi_cols()
    full = lambda z: pl.BlockSpec(z.shape, lambda b, i: (0,) * z.ndim)
    nq = 1
    outs = []
    for q_off in range(0, S // tq, nq):
        sk = (q_off + nq) * tq
        tok = lambda n, q_off=q_off: pl.BlockSpec((None, tq, n), lambda b, i: (b, i + q_off, 0))
        seq = lambda n, sk=sk: pl.BlockSpec((None, sk, n), lambda b, i: (b, 0, 0))
        outs.append(pl.pallas_call(
            functools.partial(_dsa_kernel, topk=topk, q_off=q_off, select=sk > topk),
            grid=(B, nq),
            in_specs=[tok(IDX_HEADS * LANES), tok(LANES), tok(N_ATT_HEADS * KV_LORA), seq(LANES), seq(KV_LORA + LANES),
                      pl.BlockSpec((None, KV_LORA + LANES, sk), lambda b, i: (b, 0, 0)),
                      full(wuv_bd), full(tril), full(acol)],
            out_specs=pl.BlockSpec((None, tq, D_ATT), lambda b, i: (b, i, 0)),
            out_shape=jax.ShapeDtypeStruct((B, nq * tq, D_ATT), F32),
            scratch_shapes=[pltpu.VMEM((sk, tq), I32), pltpu.VMEM((sk, tq), F32), pltpu.VMEM((sk, tq), F32),
                            pltpu.VMEM((sk, tq), BF16)],
            compiler_params=_cparams(("arbitrary", "arbitrary")),
            name=f"dsa_attn_k{sk}",
        )(iq, iw, qabs, ik, ca, cat, wuv_bd, tril, acol))
    return jnp.concatenate(outs, axis=1)


def _layernorm_rows(y, g, b):
    mu = jnp.mean(y, axis=-1, keepdims=True)
    d = y - mu
    var = jnp.mean(d * d, axis=-1, keepdims=True)
    return d * lax.rsqrt(var + 1e-5) * g + b


def _mix_kernel(orw_ref, ods_ref, x_ref, gt_ref, sc_ref, sh_ref, wtop_ref, wbot_ref, g_ref, b_ref, wr_ref, br_ref,
                tril_ref, triu_ref, x1_o, u2_o, route_o, tbl_o, cnt_o, carry, *, alpha):
    first = (pl.program_id(0) == 0) & (pl.program_id(1) == 0)
    tm = x_ref.shape[0]

    @pl.when(first)
    def _():
        carry[...] = jnp.zeros_like(carry)

    mix = _bdot(orw_ref[...], wtop_ref[...]) + _bdot(ods_ref[...], wbot_ref[...])
    x1 = _layernorm_rows(alpha * x_ref[...] + (1.0 + gt_ref[...]) * mix, g_ref[...], b_ref[...])
    x1_o[...] = x1
    u2 = x1 * (1.0 + sc_ref[...]) + sh_ref[...]
    u2_o[...] = u2

    lg = _dot3(u2, wr_ref[...]) + br_ref[...]
    lane = lax.broadcasted_iota(I32, (tm, LANES), 1)
    lane_f = lane.astype(F32)
    idxs, vals = [], []
    for _ in range(TOP_K_EXPERTS):
        m = jnp.max(lg, axis=-1, keepdims=True)
        idx = jnp.min(jnp.where(lg == m, lane_f, float(LANES)), axis=-1, keepdims=True).astype(I32)
        idxs.append(idx)
        vals.append(m)
        lg = jnp.where(lane == idx, -jnp.inf, lg)
    es = [jnp.exp(v - vals[0]) for v in vals]
    den = es[0] + es[1] + es[2] + es[3]
    hot = jnp.zeros((tm, LANES), F32)
    for idx in idxs:
        hot = hot + jnp.where(lane == idx, 1.0, 0.0)
    before = jnp.dot(tril_ref[...], hot.astype(BF16), preferred_element_type=F32)
    cnt = jnp.sum(hot, axis=0, keepdims=True)
    n_run = jnp.floor((cnt + (RUN_ALIGN - 1)) * (1.0 / RUN_ALIGN))
    n_stage = jnp.floor((n_run * RUN_ALIGN + (STAGE_ALIGN - 1)) * (1.0 / STAGE_ALIGN))
    off = jnp.dot(jnp.broadcast_to(n_stage, (SUBLANES, LANES)).astype(BF16), triu_ref[...],
                  preferred_element_type=F32)[0:1, :] * STAGE_ALIGN
    where_in_stage = off + before
    route = jnp.zeros((tm, LANES), F32)
    for k in range(TOP_K_EXPERTS):
        spos = jnp.sum(jnp.where(lane == idxs[k], where_in_stage, 0.0), axis=-1, keepdims=True)
        route = jnp.where(lane == k, idxs[k].astype(F32), route)
        route = jnp.where(lane == TOP_K_EXPERTS + k, es[k] / den, route)
        route = jnp.where(lane == 2 * TOP_K_EXPERTS + k, spos, route)
    route_o[...] = route
    sub = lax.broadcasted_iota(I32, (SUBLANES, LANES), 0)
    tbl_o[...] = jnp.where(sub == 0, n_run, jnp.where(sub == 1, off, jnp.where(sub == 2, carry[0:1, :], 0.0)))
    carry[0:1, :] = carry[0:1, :] + n_run * RUN_ALIGN
    cnt_o[...] = carry[...]


def _mix_out(o_rwkv, o_dsa, x, gt1, sc2, sh2, w_out, ln_g, ln_b, w_router, b_router):
    B, S, D = x.shape
    tm = min(TM_PROJ, S)
    alpha = 2.0 ** 0.25
    wtop = w_out[:D_RWKV].astype(BF16)
    wbot = w_out[D_RWKV:].astype(BF16)
    wr_p = jnp.pad(w_router, ((0, 0), (0, LANES - N_EXPERTS)))
    br_p = jnp.pad(b_router.reshape(1, -1), ((0, 0), (0, LANES - N_EXPERTS)), constant_values=NEG_BIG)
    tril = jnp.asarray(np.tril(np.ones((tm, tm)), -1), BF16)
    triu = jnp.asarray(np.triu(np.ones((LANES, LANES)), 1), BF16)
    row = lambda v: v.reshape(1, -1)
    tok = lambda n: pl.BlockSpec((None, tm, n), lambda b, i: (b, i, 0))
    mod = pl.BlockSpec((None, 1, D), lambda b, i: (b, 0, 0))
    full = lambda a: pl.BlockSpec(a.shape, lambda b, i: (0,) * a.ndim)
    consts = [wtop, wbot, row(ln_g), row(ln_b), wr_p, br_p, tril, triu]
    return pl.pallas_call(
        functools.partial(_mix_kernel, alpha=alpha),
        grid=(B, S // tm),
        in_specs=[tok(D_RWKV), tok(D_ATT), tok(D), mod, mod, mod] + [full(a) for a in consts],
        out_specs=[tok(D), tok(D), tok(LANES), pl.BlockSpec((None, None, SUBLANES, LANES), lambda b, i: (b, i, 0, 0)),
                   pl.BlockSpec((SUBLANES, LANES), lambda b, i: (0, 0))],
        out_shape=[jax.ShapeDtypeStruct((B, S, D), F32), jax.ShapeDtypeStruct((B, S, D), F32),
                   jax.ShapeDtypeStruct((B, S, LANES), F32),
                   jax.ShapeDtypeStruct((B, S // tm, SUBLANES, LANES), F32),
                   jax.ShapeDtypeStruct((SUBLANES, LANES), F32)],
        scratch_shapes=[pltpu.VMEM((8, LANES), F32)],
        compiler_params=_cparams(("arbitrary", "arbitrary")),
        name="mix_out",
    )(o_rwkv, o_dsa, x, gt1, sc2, sh2, *consts)


RUN_ALIGN = SUBLANES
STAGE_ALIGN = 16
STAGE_ROWS = 1536
STAGE_CHUNK = 1536


def _run_copies(nrun_ref, tile, copy_of):
    for e in range(N_EXPERTS):
        pieces = (nrun_ref[tile * N_EXPERTS + e] * RUN_ALIGN + STAGE_ALIGN - 1) // STAGE_ALIGN

        def piece(j, carry, e=e):
            copy_of(e, j).start(priority=e % 2)
            return carry

        lax.fori_loop(0, pieces, piece, 0)


def _dispatch_kernel(nrun_ref, off_ref, start_ref, npiece_ref, nstage_ref, zs_ref, zn_ref, tail_ref,
                     u_ref, route_ref, xs_out, stag, zeros, sem, zsem):
    i = pl.program_id(0)
    n = pl.num_programs(0)
    tm = u_ref.shape[0]
    bm = zeros.shape[0]
    slot = i % 2

    @pl.when(i == 0)
    def _():
        zeros[...] = jnp.zeros_like(zeros)
        fills = []
        for e in range(N_EXPERTS):
            for b in range(bm.bit_length()):
                rows = RUN_ALIGN << b
                if rows > bm:
                    break
                done = (zn_ref[e] >> (b + 1)) << (b + 1)
                dst = pl.multiple_of(zs_ref[e] + done * RUN_ALIGN, RUN_ALIGN)
                fills.append(((zn_ref[e] >> b) & 1 == 1,
                              pltpu.make_async_copy(zeros.at[pl.ds(0, rows)], xs_out.at[pl.ds(dst, rows)], zsem)))
        for pred, cp in fills:
            pl.when(pred)(cp.start)

        def tail_copy(j):
            return pltpu.make_async_copy(zeros, xs_out.at[pl.ds(pl.multiple_of(j * bm, bm), bm)], zsem)

        lax.fori_loop(tail_ref[0], tail_ref[1], lambda j, c: (tail_copy(j).start(), c)[1], 0)
        for pred, cp in fills:
            pl.when(pred)(cp.wait)
        lax.fori_loop(tail_ref[0], tail_ref[1], lambda j, c: (tail_copy(j).wait(), c)[1], 0)

    route_t = jnp.transpose(route_ref[...])
    spos = [route_t[2 * TOP_K_EXPERTS + k:2 * TOP_K_EXPERTS + k + 1, :].astype(I32) for k in range(TOP_K_EXPERTS)]
    ub = u_ref[...].astype(BF16)
    srow = lax.broadcasted_iota(I32, (STAGE_CHUNK, tm), 0)
    for c in range(STAGE_ROWS // STAGE_CHUNK):
        @pl.when(c * STAGE_CHUNK < nstage_ref[i])
        def _(c=c):
            rows = srow + c * STAGE_CHUNK
            sel = (rows == spos[0]) | (rows == spos[1]) | (rows == spos[2]) | (rows == spos[3])
            stag[slot, c * STAGE_CHUNK:(c + 1) * STAGE_CHUNK, :] = jnp.dot(
                jnp.where(sel, 1.0, 0.0).astype(BF16), ub, preferred_element_type=F32)

    def piece_copy(s, tile):
        def copy_of(e, j):
            src = pl.multiple_of(off_ref[tile * N_EXPERTS + e] + j * STAGE_ALIGN, STAGE_ALIGN)
            dst = pl.multiple_of(start_ref[tile * N_EXPERTS + e] + j * STAGE_ALIGN, RUN_ALIGN)
            return pltpu.make_async_copy(stag.at[s, pl.ds(src, STAGE_ALIGN)], xs_out.at[pl.ds(dst, STAGE_ALIGN)], sem)
        return copy_of

    def drain(tile):
        def w(j, carry):
            pltpu.make_async_copy(stag.at[0, pl.ds(0, STAGE_ALIGN)], xs_out.at[pl.ds(0, STAGE_ALIGN)], sem).wait()
            return carry
        lax.fori_loop(0, npiece_ref[tile], w, 0)

    @pl.when(i > 0)
    def _():
        drain(i - 1)

    _run_copies(nrun_ref, i, piece_copy(slot, i))

    @pl.when(i == n - 1)
    def _():
        drain(i)


def _moe_dispatch(u2, route, tables, n_rows):
    T, D = u2.shape
    tm = TM_ROUTE
    bm = BM_EXPERT
    return pl.pallas_call(
        _dispatch_kernel,
        grid_spec=pltpu.PrefetchScalarGridSpec(
            num_scalar_prefetch=len(tables),
            grid=(T // tm,),
            in_specs=[pl.BlockSpec((tm, D), lambda i, *_: (i, 0)),
                      pl.BlockSpec((tm, LANES), lambda i, *_: (i, 0))],
            out_specs=pl.BlockSpec(memory_space=pl.ANY),
            scratch_shapes=[pltpu.VMEM((2, STAGE_ROWS, D), F32), pltpu.VMEM((bm, D), F32),
                            pltpu.SemaphoreType.DMA(()), pltpu.SemaphoreType.DMA(())],
        ),
        out_shape=jax.ShapeDtypeStruct((n_rows, D), F32),
        compiler_params=_cparams(("arbitrary",)),
        name="moe_dispatch",
    )(*tables, u2, route)


GU_GROUP = 2 * LANES


def _deinterleave_perm():
    p = np.zeros((GU_GROUP, GU_GROUP), np.float32)
    l = np.arange(LANES)
    p[2 * l, l] = 1.0
    p[2 * l + 1, LANES + l] = 1.0
    return jnp.asarray(p, BF16)


def _expert_kernel(be_ref, nb_ref, valid_ref, xs_ref, wgu_hbm, bgu_ref, wd_hbm, bd_ref, perm_ref, ys_ref,
                   wg_buf, wd_buf, wp, wdb, sem):
    i = pl.program_id(0)
    bm = xs_ref.shape[0]
    e = be_ref[i]
    used = i < nb_ref[0]
    new_expert = (i == 0) | (e != be_ref[jnp.maximum(i - 1, 0)])
    n_groups = wp.shape[1] // GU_GROUP
    n_experts = wgu_hbm.shape[0]

    def fetch(ex):
        return (pltpu.make_async_copy(wgu_hbm.at[ex], wg_buf, sem.at[0]),
                pltpu.make_async_copy(wd_hbm.at[ex], wd_buf, sem.at[1]))

    @pl.when(used & new_expert)
    def _():
        @pl.when(i == 0)
        def _():
            for cp in fetch(e):
                cp.start()

        for cp in fetch(e):
            cp.wait()
        for j in range(n_groups):
            sl = slice(j * GU_GROUP, (j + 1) * GU_GROUP)
            wp[:, sl] = jnp.dot(wg_buf[:, sl].astype(BF16), perm_ref[...], preferred_element_type=F32).astype(BF16)
        wdb[...] = wd_buf[...].astype(BF16)

        @pl.when(e + 1 < n_experts)
        def _():
            for cp in fetch(e + 1):
                cp.start()

    def compute(m):
        xb = xs_ref[:m, :].astype(BF16)
        gu = jnp.dot(xb, wp[...], preferred_element_type=F32) + bgu_ref[...]
        hs = []
        for j in range(n_groups):
            gate = jnp.minimum(gu[:, j * GU_GROUP:j * GU_GROUP + LANES], SWIGLU_LIMIT)
            up = jnp.clip(gu[:, j * GU_GROUP + LANES:(j + 1) * GU_GROUP], -SWIGLU_LIMIT, SWIGLU_LIMIT)
            hs.append(((up + 1.0) * (gate * _sigmoid(gate * SWIGLU_ALPHA))).astype(BF16))
        h = jnp.concatenate(hs, axis=-1)
        ys_ref[:m, :] = jnp.dot(h, wdb[...], preferred_element_type=F32) + bd_ref[...]

    sizes = (bm, bm // 2, bm // 4)
    for m, smaller in zip(sizes, sizes[1:] + (0,)):
        @pl.when(used & (valid_ref[i] <= m) & (valid_ref[i] > smaller))
        def _(m=m):
            compute(m)
            if m < bm:
                ys_ref[m:, :] = jnp.zeros((bm - m, ys_ref.shape[1]), F32)

    @pl.when(used & (valid_ref[i] <= 0))
    def _():
        ys_ref[...] = jnp.zeros_like(ys_ref)

    @pl.when(jnp.logical_not(used))
    def _():
        ys_ref[...] = jnp.zeros_like(ys_ref)


def _moe_experts(xs, block_e, n_used, valid, w_gu, b_gu_p, w_dn, b_dn):
    n_rows, D = xs.shape
    E, _, F2 = w_gu.shape
    bm = BM_EXPERT
    n_blocks = n_rows // bm
    perm = _deinterleave_perm()
    wspec = lambda shp: pl.BlockSpec((None,) + shp, lambda i, be, nb, va: (be[i], 0, 0))
    hbm = pl.BlockSpec(memory_space=pl.ANY)
    return pl.pallas_call(
        _expert_kernel,
        grid_spec=pltpu.PrefetchScalarGridSpec(
            num_scalar_prefetch=3,
            grid=(n_blocks,),
            in_specs=[pl.BlockSpec((bm, D), lambda i, be, nb, va: (jnp.minimum(i, nb[0] - 1), 0)),
                      hbm, wspec((1, F2)), hbm, wspec((1, D)),
                      pl.BlockSpec(perm.shape, lambda i, be, nb, va: (0, 0))],
            out_specs=pl.BlockSpec((bm, D), lambda i, be, nb, va: (i, 0)),
            scratch_shapes=[pltpu.VMEM((D, F2), F32), pltpu.VMEM((F2 // 2, D), F32),
                            pltpu.VMEM((D, F2), BF16), pltpu.VMEM((F2 // 2, D), BF16),
                            pltpu.SemaphoreType.DMA((2,))],
        ),
        out_shape=jax.ShapeDtypeStruct((n_rows, D), F32),
        compiler_params=pltpu.CompilerParams(dimension_semantics=("arbitrary",), vmem_limit_bytes=VMEM_LIMIT_EXPERTS),
        name="moe_experts",
    )(block_e, n_used, valid, xs, w_gu, b_gu_p, w_dn, b_dn, perm)


def _combine_kernel(nrun_ref, off_ref, start_ref, npiece_ref, nstage_ref, ys_ref, x1_ref, route_ref, gt_ref, g_ref,
                    b_ref, o_ref, stag, sem, *, alpha):
    i = pl.program_id(0)
    n = pl.num_programs(0)
    tm = x1_ref.shape[0]
    slot = i % 2

    def gather(tile, s):
        def copy_of(e, j):
            src = pl.multiple_of(start_ref[tile * N_EXPERTS + e] + j * STAGE_ALIGN, RUN_ALIGN)
            dst = pl.multiple_of(off_ref[tile * N_EXPERTS + e] + j * STAGE_ALIGN, STAGE_ALIGN)
            return pltpu.make_async_copy(ys_ref.at[pl.ds(src, STAGE_ALIGN)], stag.at[s, pl.ds(dst, STAGE_ALIGN)],
                                         sem.at[s])
        _run_copies(nrun_ref, tile, copy_of)

    @pl.when(i == 0)
    def _():
        stag[...] = jnp.zeros_like(stag)
        gather(0, 0)

    @pl.when(i + 1 < n)
    def _():
        gather(i + 1, 1 - slot)

    def w(j, carry):
        pltpu.make_async_copy(ys_ref.at[pl.ds(0, STAGE_ALIGN)], stag.at[slot, pl.ds(0, STAGE_ALIGN)],
                              sem.at[slot]).wait()
        return carry
    lax.fori_loop(0, npiece_ref[i], w, 0)

    route = route_ref[...]
    wide = lambda col: jnp.broadcast_to(col, (tm, STAGE_CHUNK))
    spos = [wide(route[:, 2 * TOP_K_EXPERTS + k:2 * TOP_K_EXPERTS + k + 1].astype(I32)) for k in range(TOP_K_EXPERTS)]
    gate = [wide(route[:, TOP_K_EXPERTS + k:TOP_K_EXPERTS + k + 1]) for k in range(TOP_K_EXPERTS)]
    scol = lax.broadcasted_iota(I32, (tm, STAGE_CHUNK), 1)
    acc_ref = o_ref
    acc_ref[...] = jnp.zeros_like(acc_ref)
    for c in range(STAGE_ROWS // STAGE_CHUNK):
        @pl.when(c * STAGE_CHUNK < nstage_ref[i])
        def _(c=c):
            cols = scol + c * STAGE_CHUNK
            wgt = jnp.zeros((tm, STAGE_CHUNK), F32)
            for k in range(TOP_K_EXPERTS):
                wgt = wgt + jnp.where(cols == spos[k], gate[k], 0.0)
            acc_ref[...] += jnp.dot(wgt.astype(BF16), stag[slot, c * STAGE_CHUNK:(c + 1) * STAGE_CHUNK, :].astype(BF16),
                                    preferred_element_type=F32)
    o_ref[...] = _layernorm_rows(alpha * x1_ref[...] + (1.0 + gt_ref[...]) * acc_ref[...], g_ref[...], b_ref[...])


def _moe_combine(ys, tables, x1, route, gt2, ln_g, ln_b, tiles_per_batch):
    T, D = x1.shape
    tm = TM_ROUTE
    row = lambda v: v.reshape(1, -1)
    return pl.pallas_call(
        functools.partial(_combine_kernel, alpha=2.0 ** 0.25),
        grid_spec=pltpu.PrefetchScalarGridSpec(
            num_scalar_prefetch=len(tables),
            grid=(T // tm,),
            in_specs=[pl.BlockSpec(memory_space=pl.ANY),
                      pl.BlockSpec((tm, D), lambda i, *_: (i, 0)),
                      pl.BlockSpec((tm, LANES), lambda i, *_: (i, 0)),
                      pl.BlockSpec((None, 1, D), lambda i, *_: (i // tiles_per_batch, 0, 0)),
                      pl.BlockSpec((1, D), lambda i, *_: (0, 0)),
                      pl.BlockSpec((1, D), lambda i, *_: (0, 0))],
            out_specs=pl.BlockSpec((tm, D), lambda i, *_: (i, 0)),
            scratch_shapes=[pltpu.VMEM((2, STAGE_ROWS, D), F32), pltpu.SemaphoreType.DMA((2,))],
        ),
        out_shape=jax.ShapeDtypeStruct((T, D), F32),
        compiler_params=_cparams(("arbitrary",)),
        name="moe_combine",
    )(*tables, ys, x1, route, gt2, row(ln_g), row(ln_b))


def _moe_and_norm(x1, u2, route, tbl, totals, gt2, w_gu, b_gu, w_dn, b_dn, ln_g, ln_b):
    B, S, D = x1.shape
    T = B * S
    bm = BM_EXPERT
    tm = TM_ROUTE
    assert T % tm == 0 and TM_PROJ == tm and STAGE_ROWS >= tm * TOP_K_EXPERTS + N_EXPERTS * (STAGE_ALIGN - 1)
    n_tiles = T // tm
    max_rows = T * TOP_K_EXPERTS + n_tiles * N_EXPERTS * (RUN_ALIGN - 1) + N_EXPERTS * STAGE_ALIGN
    n_blocks = -(-max_rows // bm) + N_EXPERTS
    tot = totals[0, :N_EXPERTS].astype(I32)
    padded = (tot + STAGE_ALIGN + bm - 1) // bm * bm
    pad_ends = jnp.cumsum(padded)
    pad_starts = pad_ends - padded
    t3 = tbl.reshape(n_tiles, SUBLANES, LANES)[:, :, :N_EXPERTS].astype(I32)
    nrun, off, base = t3[:, 0, :], t3[:, 1, :], t3[:, 2, :]
    start = pad_starts[None, :] + base
    pieces = (nrun * RUN_ALIGN + STAGE_ALIGN - 1) // STAGE_ALIGN
    npiece = jnp.sum(pieces, axis=1)
    nstage = jnp.sum(pieces, axis=1) * STAGE_ALIGN
    flat = lambda z: z.reshape(-1).astype(I32)
    run_tables = [flat(nrun), flat(off), flat(start), flat(npiece), flat(nstage)]
    zs = pad_starts + tot
    zn = (pad_ends - zs) // RUN_ALIGN
    n_used = (pad_ends[-1:] // bm).astype(I32)
    tail = jnp.concatenate([n_used, jnp.full((1,), n_blocks, I32)])
    blk_row = jnp.arange(n_blocks, dtype=I32) * bm
    block_e = jnp.minimum(jnp.sum((blk_row[:, None] >= pad_ends[None, :]).astype(I32), axis=1), N_EXPERTS - 1)
    E, F2 = b_gu.shape
    b_gu_p = b_gu.reshape(E, F2 // GU_GROUP, LANES, 2).transpose(0, 1, 3, 2).reshape(E, 1, F2)
    route2 = route.reshape(T, LANES)
    xs = _moe_dispatch(u2.reshape(T, D), route2, run_tables + [flat(zs), flat(zn), tail], n_blocks * bm)
    valid = jnp.clip(zs[block_e] - blk_row, 0, bm)
    ys = _moe_experts(xs, block_e, n_used, valid, w_gu, b_gu_p, w_dn, b_dn[:, None, :])
    out = _moe_combine(ys, run_tables, x1.reshape(T, D), route2, gt2, ln_g, ln_b, S // tm)
    return out.reshape(B, S, D)


def kernel(x, c, w_ada, b_ada, w_in, shift_mu, rwkv_w0, rwkv_w2, rwkv_a0, rwkv_a2, rwkv_g2, rwkv_k_k, rwkv_k_a, rwkv_r_k, rwkv_ln_w, rwkv_ln_b, mla_q_norm, mla_w_q_up, mla_kv_norm, mla_w_uk, mla_w_uv, idx_w_q, idx_ln_g, idx_ln_b, w_out, ln1_g, ln1_b, w_router, b_router, w_gu, b_gu, w_dn, b_dn, ln2_g, ln2_b):
    depth = w_ada.shape[0]
    assert depth == 1, "DeepNorm constants below are for a single layer"
    l = 0
    mod = _ada_mod(c, w_ada[l], b_ada[l])
    sh1, sc1, gt1, sh2, sc2, gt2 = [m[:, None, :] for m in jnp.split(mod, 6, axis=-1)]
    r, lw, k, v, kk, a, g, qabs, iq, ik, iw, cl, clt = _in_proj(
        x, sc1, sh1, w_in[l], shift_mu[l], rwkv_w0[l], rwkv_w2[l], rwkv_a0[l], rwkv_a2[l], rwkv_g2[l], rwkv_k_k[l],
        rwkv_k_a[l], mla_q_norm[l], mla_w_q_up[l], mla_w_uk[l], mla_kv_norm[l], idx_w_q[l], idx_ln_g[l], idx_ln_b[l])
    o_rwkv = _rwkv_scan(r, lw, k, v, kk, a, g, rwkv_r_k[l], rwkv_ln_w[l], rwkv_ln_b[l])
    o_dsa = _dsa_attn(iq, iw, qabs, ik, cl, clt, mla_w_uv[l])
    x1, u2, route, tbl, totals = _mix_out(o_rwkv, o_dsa, x, gt1, sc2, sh2, w_out[l], ln1_g[l], ln1_b[l],
                                          w_router[l], b_router[l])
    return _moe_and_norm(x1, u2, route, tbl, totals, gt2, w_gu[l], b_gu[l], w_dn[l], b_dn[l], ln2_g[l], ln2_b[l])
```

```python
import functools
import math

import jax
import jax.numpy as jnp
import numpy as np
from jax import lax
from jax.experimental import pallas as pl
from jax.experimental.pallas import tpu as pltpu

F32 = jnp.float32
BF16 = jnp.bfloat16
I32 = jnp.int32

RWKV_HEAD = 64
N_RWKV_HEADS = 8
D_RWKV = RWKV_HEAD * N_RWKV_HEADS
RWKV_GN_EPS = 64e-5
ATT_HEAD = 64
N_ATT_HEADS = 8
D_ATT = ATT_HEAD * N_ATT_HEADS
KV_LORA = 128
IDX_HEADS = 8
IDX_DIM = 64
TOPK_MAX = 256
N_EXPERTS = 32
TOP_K_EXPERTS = 4
SWIGLU_LIMIT = 7.0
SWIGLU_ALPHA = 1.702
NEG_BIG = -1e30
LOG2E = 1.4426950408889634
INT_MIN = -(2 ** 31)

LANES = 128
SUBLANES = 8
VMEM_LIMIT = 48 * 1024 * 1024
VMEM_LIMIT_EXPERTS = 58 * 1024 * 1024

TM_IN = 512
TM_PROJ = 256
L_CHUNK = 64
CHUNKS_PER_STEP = 4
TQ = 256
KEY_CHUNK = 256
SUM_ROWS = 128
TM_ROUTE = 256
BM_EXPERT = 1024

_SEG = (("r", 512, 512), ("k", 512, 512), ("v", 512, 512), ("w", 64, 128), ("a", 64, 128), ("g", 128, 128),
        ("q", 256, 256), ("kv", 128, 128), ("ik", 64, 128), ("iw", 8, 128))
N_SHIFT_P = 512 * 3 + 128 * 3
N_IN_P = sum(s[2] for s in _SEG)


def _cparams(sem):
    return pltpu.CompilerParams(dimension_semantics=sem, vmem_limit_bytes=VMEM_LIMIT)


def _bdot(a, b):
    return jnp.dot(a.astype(BF16), b.astype(BF16), preferred_element_type=F32)


def _split2(a):
    hi = a.astype(BF16)
    lo = (a - hi.astype(F32)).astype(BF16)
    return hi, lo


def _split3(a):
    hi = a.astype(BF16)
    r1 = a - hi.astype(F32)
    mid = r1.astype(BF16)
    lo = (r1 - mid.astype(F32)).astype(BF16)
    return hi, mid, lo


def _dot3(a, b, dims=(((1,), (0,)), ((), ()))):
    ah, al = _split2(a)
    bh, bl = _split2(b)
    d = functools.partial(lax.dot_general, dimension_numbers=dims, preferred_element_type=F32)
    return d(ah, bh) + (d(ah, bl) + d(al, bh))


def _dot_exact_rhs(a, b_exact, nsplit=3):
    parts = _split3(a) if nsplit == 3 else _split2(a)
    acc = None
    for p in parts[::-1]:
        t = jnp.dot(p, b_exact, preferred_element_type=F32)
        acc = t if acc is None else acc + t
    return acc


def _dot_exact_lhs(a_exact, b, nsplit=3):
    parts = _split3(b) if nsplit == 3 else _split2(b)
    acc = None
    for p in parts[::-1]:
        t = jnp.dot(a_exact, p, preferred_element_type=F32)
        acc = t if acc is None else acc + t
    return acc


def _sigmoid(x):
    return 1.0 / (1.0 + jnp.exp(-x))


def _softplus(x):
    return jnp.maximum(x, 0.0) + jnp.log(1.0 + jnp.exp(-jnp.abs(x)))


def _ada_kernel(c_ref, w_ref, b_ref, o_ref):
    c = c_ref[...]
    o_ref[...] = _dot3(c * _sigmoid(c), w_ref[...]) + b_ref[...]


def _ada_mod(c, w_ada, b_ada):
    B, D = c.shape
    N = w_ada.shape[1]
    tn = 1024
    return pl.pallas_call(
        _ada_kernel,
        grid=(N // tn,),
        in_specs=[pl.BlockSpec((B, D), lambda j: (0, 0)),
                  pl.BlockSpec((D, tn), lambda j: (0, j)),
                  pl.BlockSpec((1, tn), lambda j: (0, j))],
        out_specs=pl.BlockSpec((B, tn), lambda j: (0, j)),
        out_shape=jax.ShapeDtypeStruct((B, N), F32),
        compiler_params=_cparams(("arbitrary",)),
        name="ada_mod",
    )(c, w_ada, b_ada.reshape(1, N))


def _in_proj_kernel(x_ref, sc_ref, sh_ref, win_ref, mu_ref, w0_ref, w2_ref, a0_ref, a2_ref, g2_ref, kk_ref, ka_ref,
                    ones_ref, qn_ref, wq_ref, wuk_ref, kvn_ref, wiq_ref, ig_ref, ib_ref,
                    r_o, lw_o, k_o, v_o, kkn_o, a_o, g_o, qabs_o, iq_o, ik_o, iw_o, cl_o, clt_o, carry):
    i = pl.program_id(1)
    tm = x_ref.shape[0]

    @pl.when(i == 0)
    def _():
        carry[...] = jnp.zeros_like(carry)

    u = x_ref[...] * (1.0 + sc_ref[...]) + sh_ref[...]
    p = _bdot(u, win_ref[...])
    ps = p[:, :N_SHIFT_P]
    rows = lax.broadcasted_iota(I32, (tm, 1), 0)
    prev = jnp.where(rows == 0, carry[0:1, :], pltpu.roll(ps, 1, 0))
    carry[0:1, :] = ps[tm - 1:tm, :]
    ps = ps + mu_ref[...] * (prev - ps)

    pr, pk, pv = ps[:, 0:512], ps[:, 512:1024], ps[:, 1024:1536]
    pw, pa, pg = ps[:, 1536:1664], ps[:, 1664:1792], ps[:, 1792:1920]
    w_log = -_softplus(-(w0_ref[...] + _dot3(jnp.tanh(pw), w2_ref[...]))) - 0.5
    lw_o[...] = -jnp.exp(w_log)
    a = _sigmoid(a0_ref[...] + _dot3(pa, a2_ref[...]))
    g_o[...] = _dot3(_sigmoid(pg), g2_ref[...])
    kk = pk * kk_ref[...]
    ssq = _dot_exact_rhs(kk * kk, ones_ref[...], nsplit=2)
    kkn_o[...] = kk / jnp.maximum(jnp.sqrt(ssq), 1e-12)
    k_o[...] = pk * (1.0 + (a - 1.0) * ka_ref[...])
    r_o[...] = pr
    v_o[...] = pv
    a_o[...] = a

    pq, pkv = p[:, 1920:2176], p[:, 2176:2304]
    pik, piw = p[:, 2304:2432], p[:, 2432:2560]
    q_lat = pq * lax.rsqrt(jnp.mean(pq * pq, axis=-1, keepdims=True) + 1e-6) * qn_ref[...]
    q = _bdot(q_lat, wq_ref[...])
    qabs_o[...] = (_bdot(q, wuk_ref[...]) * (ATT_HEAD ** -0.5 * LOG2E)).astype(BF16)
    c_lat = pkv * lax.rsqrt(jnp.mean(pkv * pkv, axis=-1, keepdims=True) + 1e-6) * kvn_ref[...]
    spos = i * tm + rows
    s_hi = (spos >> 7).astype(F32)
    s_lo = (spos & (LANES - 1)).astype(F32)
    lane_t = lax.broadcasted_iota(I32, (1, LANES), 1)
    extra = jnp.where(lane_t == 0, 1.0, jnp.where((lane_t == 1) | (lane_t == 2), s_hi,
                                                  jnp.where((lane_t == 3) | (lane_t == 4), s_lo, 0.0)))
    c_aug = jnp.concatenate([c_lat, extra], axis=-1)
    cl_o[...] = c_aug.astype(BF16)
    clt_o[...] = jnp.transpose(c_aug).astype(BF16)
    iq_o[...] = (_bdot(q_lat, wiq_ref[...]) * (IDX_DIM ** -0.5)).astype(BF16)
    lane = lax.broadcasted_iota(I32, (1, LANES), 1)
    valid = lane < IDX_DIM
    mu = jnp.sum(pik, axis=-1, keepdims=True) * (1.0 / IDX_DIM)
    dlt = jnp.where(valid, pik - mu, 0.0)
    var = jnp.sum(dlt * dlt, axis=-1, keepdims=True) * (1.0 / IDX_DIM)
    ik_o[...] = jnp.where(valid, dlt * lax.rsqrt(var + 1e-5) * ig_ref[...] + ib_ref[...], 0.0).astype(BF16)
    iw_o[...] = piw * (IDX_HEADS ** -0.5)


def _pad_cols(w, widths):
    parts, o = [], 0
    for true, padded in widths:
        seg = w[..., o:o + true]
        if padded > true:
            seg = jnp.pad(seg, [(0, 0)] * (w.ndim - 1) + [(0, padded - true)])
        parts.append(seg)
        o += true
    return jnp.concatenate(parts, axis=-1)


def _pad_rows(w, rows):
    return jnp.pad(w, ((0, rows - w.shape[0]), (0, 0)))


def _block_diag(blocks):
    H, a, b = blocks.shape
    eye = jnp.eye(H, dtype=blocks.dtype)
    return (eye[:, None, :, None] * blocks[:, :, None, :]).reshape(H * a, H * b)


def _head_ones(n, head):
    idx = np.arange(n) // head
    return jnp.asarray(idx[:, None] == idx[None, :], BF16)


def _in_proj(x, sc1, sh1, w_in, shift_mu, w0, w2, a0, a2, g2, k_k, k_a, q_norm, w_q_up, w_uk, kv_norm, idx_w_q,
             idx_ln_g, idx_ln_b):
    B, S, D = x.shape
    tm = min(TM_IN, S)
    widths = tuple((s[1], s[2]) for s in _SEG)
    win_p = _pad_cols(w_in, widths).astype(BF16)
    mu_p = _pad_cols(shift_mu.reshape(1, -1), widths[:6])
    w2_p = _pad_rows(w2, LANES)
    a2_p = _pad_rows(a2, LANES)
    wuk_bd = _block_diag(w_uk).astype(BF16)
    wiq_p = _pad_cols(idx_w_q, ((IDX_DIM, LANES),) * IDX_HEADS).astype(BF16)
    ig_p = _pad_cols(idx_ln_g.reshape(1, -1), ((IDX_DIM, LANES),))
    ib_p = _pad_cols(idx_ln_b.reshape(1, -1), ((IDX_DIM, LANES),))
    row = lambda v: v.reshape(1, -1)
    tok = lambda n: pl.BlockSpec((None, tm, n), lambda b, i: (b, i, 0))
    mod = pl.BlockSpec((None, 1, D), lambda b, i: (b, 0, 0))
    full = lambda a: pl.BlockSpec(a.shape, lambda b, i: (0,) * a.ndim)
    consts = [win_p, mu_p, row(w0), w2_p, row(a0), a2_p, g2, row(k_k), row(k_a), _head_ones(D_RWKV, RWKV_HEAD),
              row(q_norm), w_q_up.astype(BF16), wuk_bd, row(kv_norm), wiq_p, ig_p, ib_p]
    outs = [(D_RWKV, F32)] * 7 + [(N_ATT_HEADS * KV_LORA, BF16), (IDX_HEADS * LANES, BF16), (LANES, BF16),
                                  (LANES, F32), (KV_LORA + LANES, BF16)]
    return pl.pallas_call(
        _in_proj_kernel,
        grid=(B, S // tm),
        in_specs=[tok(D), mod, mod] + [full(a) for a in consts],
        out_specs=[tok(n) for n, _ in outs] + [pl.BlockSpec((None, KV_LORA + LANES, tm), lambda b, i: (b, 0, i))],
        out_shape=[jax.ShapeDtypeStruct((B, S, n), dt) for n, dt in outs]
        + [jax.ShapeDtypeStruct((B, KV_LORA + LANES, S), BF16)],
        scratch_shapes=[pltpu.VMEM((8, N_SHIFT_P), F32)],
        compiler_params=_cparams(("arbitrary", "arbitrary")),
        name="in_proj",
    )(x, sc1, sh1, *consts)


def _rwkv_kernel(r_ref, lw_ref, k_ref, v_ref, kk_ref, a_ref, g_ref, rk_ref, lnw_ref, lnb_ref, tri_ref, ones_ref,
                 o_ref, state, *, L):
    c = pl.program_id(1)
    nc = r_ref.shape[0] // L

    @pl.when(c == 0)
    def _():
        state[...] = jnp.zeros_like(state)

    r, lw, k, v, kk, a = r_ref[...], lw_ref[...], k_ref[...], v_ref[...], kk_ref[...], a_ref[...]
    cum = _dot_exact_lhs(tri_ref[...], lw)
    rows = [slice(ci * L, (ci + 1) * L) for ci in range(nc)]
    last = [cum[(ci + 1) * L - 1:(ci + 1) * L, :] for ci in range(nc)]
    cum_last = jnp.concatenate([jnp.broadcast_to(z, (L, z.shape[1])) for z in last], axis=0)
    w_incl = jnp.exp(cum)
    w_inv = jnp.exp(-cum)
    w_rel = jnp.exp(cum_last - cum)
    w_last = [jnp.exp(z) for z in last]
    bvec = kk * a
    at = -kk * jnp.exp(cum - lw)
    rt = r * w_incl
    bt = bvec * w_inv
    kt = k * w_inv
    bh = bvec * w_rel
    kh = k * w_rel
    ti = lax.broadcasted_iota(I32, (L, L), 0)
    tj = lax.broadcasted_iota(I32, (L, L), 1)
    strict = tj < ti
    incl = tj <= ti
    eye = lax.broadcasted_iota(I32, (RWKV_HEAD, RWKV_HEAD), 0) == lax.broadcasted_iota(I32, (RWKV_HEAD, RWKV_HEAD), 1)
    NT = (((1,), (1,)), ((), ()))
    TN = (((0,), (0,)), ((), ()))
    heads = range(N_RWKV_HEADS)
    sls = [slice(h * RWKV_HEAD, (h + 1) * RWKV_HEAD) for h in heads]
    units = [(ci, h) for ci in range(nc) for h in heads]
    idx = range(len(units))
    mm = lambda x, y, dims=(((1,), (0,)), ((), ())): lax.dot_general(
        x.astype(BF16), y.astype(BF16), dims, preferred_element_type=F32)
    at_b, rt_b, bt_b, kt_b, bh_b, kh_b, v_b = [z.astype(BF16) for z in (at, rt, bt, kt, bh, kh, v)]
    cut = lambda z, u: z[rows[u[0]], sls[u[1]]]
    vh = [cut(v_b, u) for u in units]
    ath = [cut(at_b, u) for u in units]
    ar = [jnp.concatenate([cut(at_b, u), cut(rt_b, u)], axis=0) for u in units]
    g_b = [mm(ar[i], cut(bt_b, units[i]), NT) for i in idx]
    g_k = [mm(ar[i], cut(kt_b, units[i]), NT) for i in idx]
    n_ab = [jnp.where(strict, g_b[i][:L], 0.0) for i in idx]
    a_ak = [jnp.where(strict, g_k[i][:L], 0.0) for i in idx]
    a_rb = [jnp.where(incl, g_b[i][L:], 0.0).astype(BF16) for i in idx]
    a_rk = [jnp.where(incl, g_k[i][L:], 0.0) for i in idx]
    akv = [mm(a_ak[i], vh[i]) for i in idx]
    eye_l = jnp.where(ti == tj, 1.0, 0.0)
    tinv = [eye_l + n_ab[i] for i in idx]
    pw = n_ab
    for _ in range(int(math.log2(L)) - 1):
        pw = [mm(pw[i], pw[i]) for i in idx]
        tinv = [tinv[i] + mm(pw[i], tinv[i]) for i in idx]
    tinv = [t.astype(BF16) for t in tinv]
    a_t = [mm(tinv[i], ath[i]).astype(BF16) for i in idx]
    y = [mm(tinv[i], akv[i]).astype(BF16) for i in idx]
    m_c = [jnp.where(eye, w_last[units[i][0]][:, sls[units[i][1]]], 0.0) + mm(a_t[i], cut(bh_b, units[i]), TN)
           for i in idx]
    c_c = [mm(y[i], cut(bh_b, units[i]), TN) + mm(vh[i], cut(kh_b, units[i]), TN) for i in idx]
    q_c = [cut(rt, units[i]) + mm(a_rb[i], a_t[i]) for i in idx]
    o_loc = [mm(a_rb[i], y[i]) + mm(a_rk[i], vh[i]) for i in idx]
    s = [state[h] for h in heads]
    for ci in range(nc):
        for h in heads:
            i = ci * N_RWKV_HEADS + h
            o = o_loc[i] + mm(q_c[i], s[h], NT)
            s[h] = mm(s[h], m_c[i]) + c_c[i]
            mu = jnp.mean(o, axis=-1, keepdims=True)
            d = o - mu
            var = jnp.mean(d * d, axis=-1, keepdims=True)
            o_ref[rows[ci], sls[h]] = d * lax.rsqrt(var + RWKV_GN_EPS)
    for h in heads:
        state[h] = s[h]
    bonus = _dot_exact_rhs(r * k * rk_ref[...], ones_ref[...], nsplit=3) * v
    o_ref[...] = (o_ref[...] * lnw_ref[...] + lnb_ref[...] + bonus) * g_ref[...]


def _rwkv_scan(r, lw, k, v, kk, a, g, r_k, ln_w, ln_b):
    B, S, DR = r.shape
    L = min(L_CHUNK, S)
    lb = min(L * CHUNKS_PER_STEP, S)
    tri = jnp.asarray(np.kron(np.eye(lb // L), np.tril(np.ones((L, L)))), BF16)
    row = lambda z: z.reshape(1, -1)
    tok = pl.BlockSpec((None, lb, DR), lambda b, c: (b, c, 0))
    full = lambda z: pl.BlockSpec(z.shape, lambda b, c: (0,) * z.ndim)
    consts = [row(r_k), row(ln_w), row(ln_b), tri, _head_ones(DR, RWKV_HEAD)]
    return pl.pallas_call(
        functools.partial(_rwkv_kernel, L=L),
        grid=(B, S // lb),
        in_specs=[tok] * 7 + [full(z) for z in consts],
        out_specs=tok,
        out_shape=jax.ShapeDtypeStruct((B, S, DR), F32),
        scratch_shapes=[pltpu.VMEM((N_RWKV_HEADS, RWKV_HEAD, RWKV_HEAD), F32)],
        compiler_params=_cparams(("arbitrary", "arbitrary")),
        name="rwkv_scan",
    )(r, lw, k, v, kk, a, g, *consts)


def _alibi_cols():
    slope = np.asarray([2.0 ** (-8.0 * (h + 1) / N_ATT_HEADS) * LOG2E for h in range(N_ATT_HEADS)], np.float32)
    c_hi = slope.astype(BF16).astype(np.float32)
    c_lo = (slope - c_hi).astype(BF16).astype(np.float32)
    t = np.zeros((N_ATT_HEADS, LANES), np.float32)
    t[:, 1], t[:, 2], t[:, 3], t[:, 4] = LANES * c_hi, LANES * c_lo, c_hi, c_lo
    return jnp.asarray(t.astype(BF16))


def _dsa_kernel(iq_ref, iw_ref, qa_ref, ik_ref, ca_ref, cat_ref, wuv_ref, tril_ref, acol_ref, o_ref, key_ref,
                bias_ref, lg_ref, p_ref, *, topk, q_off, select):
    qi = pl.program_id(1) + q_off
    tq = qa_ref.shape[0]
    sk = ca_ref.shape[0]
    kc = min(KEY_CHUNK, sk)
    chunks = [slice(c * kc, (c + 1) * kc) for c in range(sk // kc)]
    tpos = qi * tq + lax.broadcasted_iota(I32, (1, tq), 1)
    srow = lax.broadcasted_iota(I32, (kc, tq), 0)
    NT = (((1,), (1,)), ((), ()))

    if not select:
        for c, cs in enumerate(chunks):
            bias_ref[cs, :] = jnp.where(srow + c * kc <= tpos, 0.0, NEG_BIG)
    else:
        iw_t = jnp.transpose(iw_ref[...])
        for c, cs in enumerate(chunks):
            ikc = ik_ref[cs, :]
            score = jnp.zeros((kc, tq), F32)
            for h in range(IDX_HEADS):
                s = lax.dot_general(ikc, iq_ref[:, h * LANES:(h + 1) * LANES], NT, preferred_element_type=F32)
                score = score + iw_t[h:h + 1, :] * jnp.maximum(s, 0.0)
            bits = pltpu.bitcast(score + 0.0, I32)
            key = bits ^ ((bits >> 31) & 0x7FFFFFFF)
            key_ref[cs, :] = jnp.where(srow + c * kc <= tpos, key, INT_MIN)
        kcount = jnp.minimum(topk, tpos + 1).astype(F32)

        def sum_keys(x):
            part = jnp.sum(x.reshape(x.shape[0] // SUM_ROWS, SUM_ROWS, x.shape[1]), axis=0)
            return jnp.sum(part, axis=0, keepdims=True)

        def count_ge(cand):
            return sum_keys(jnp.where(key_ref[...] >= cand, 1.0, 0.0))

        thr = jnp.where(count_ge(jnp.zeros((1, tq), I32)) >= kcount, 0, INT_MIN).astype(I32)

        def bit_step(i, thr):
            cand = thr | (1 << (30 - i))
            return jnp.where(count_ge(cand) >= kcount, cand, thr)

        thr = lax.fori_loop(0, 31, bit_step, thr)

        need = kcount - sum_keys(jnp.where(key_ref[...] > thr, 1.0, 0.0))
        carry = jnp.zeros((1, tq), F32)
        for j in range(sk // LANES):
            sl = slice(j * LANES, (j + 1) * LANES)
            key = key_ref[sl, :]
            eq = key == thr
            e = jnp.where(eq, 1.0, 0.0)
            before = jnp.dot(tril_ref[...], e.astype(BF16), preferred_element_type=F32) + carry
            sel = (key > thr) | (eq & (before < need))
            bias_ref[sl, :] = jnp.where(sel, 0.0, NEG_BIG)
            carry = carry + jnp.sum(e, axis=0, keepdims=True)

    outs = []
    for h in range(N_ATT_HEADS):
        q_aug = jnp.concatenate([qa_ref[:, h * KV_LORA:(h + 1) * KV_LORA],
                                 jnp.broadcast_to(acol_ref[h:h + 1, :], (tq, LANES))], axis=-1)
        lg_all = lax.dot_general(ca_ref[...], q_aug, NT, preferred_element_type=F32)
        m = jnp.full((1, tq), -jnp.inf, F32)
        for cs in chunks:
            lg = lg_all[cs, :] + bias_ref[cs, :]
            lg_ref[cs, :] = lg
            m = jnp.maximum(m, jnp.max(lg, axis=0, keepdims=True))
        for cs in chunks:
            p_ref[cs, :] = jnp.exp2(lg_ref[cs, :] - m).astype(BF16)
        pv = jnp.dot(cat_ref[...], p_ref[...], preferred_element_type=F32)
        outs.append((pv[:KV_LORA, :] / pv[KV_LORA:KV_LORA + 1, :]).astype(BF16))
    o_lat_t = jnp.concatenate(outs, axis=0)
    o_ref[...] = lax.dot_general(o_lat_t, wuv_ref[...], (((0,), (0,)), ((), ())), preferred_element_type=F32)


def _dsa_attn(iq, iw, qabs, ik, ca, cat, w_uv):
    B, S, _ = iq.shape
    tq = min(TQ, S)
    topk = min(TOPK_MAX, S // 4)
    wuv_bd = _block_diag(w_uv).astype(BF16)
    tril = jnp.asarray(np.tril(np.ones((LANES, LANES)), -1), BF16)
    acol = _alibi_cols()
    full = lambda z: pl.BlockSpec(z.shape, lambda b, i: (0,) * z.ndim)
    nq = 1
    outs = []
    for q_off in range(0, S // tq, nq):
        sk = (q_off + nq) * tq
        tok = lambda n, q_off=q_off: pl.BlockSpec((None, tq, n), lambda b, i: (b, i + q_off, 0))
        seq = lambda n, sk=sk: pl.BlockSpec((None, sk, n), lambda b, i: (b, 0, 0))
        outs.append(pl.pallas_call(
            functools.partial(_dsa_kernel, topk=topk, q_off=q_off, select=sk > topk),
            grid=(B, nq),
            in_specs=[tok(IDX_HEADS * LANES), tok(LANES), tok(N_ATT_HEADS * KV_LORA), seq(LANES), seq(KV_LORA + LANES),
                      pl.BlockSpec((None, KV_LORA + LANES, sk), lambda b, i: (b, 0, 0)),
                      full(wuv_bd), full(tril), full(acol)],
            out_specs=pl.BlockSpec((None, tq, D_ATT), lambda b, i: (b, i, 0)),
            out_shape=jax.ShapeDtypeStruct((B, nq * tq, D_ATT), F32),
            scratch_shapes=[pltpu.VMEM((sk, tq), I32), pltpu.VMEM((sk, tq), F32), pltpu.VMEM((sk, tq), F32),
                            pltpu.VMEM((sk, tq), BF16)],
            compiler_params=_cparams(("arbitrary", "arbitrary")),
            name=f"dsa_attn_k{sk}",
        )(iq, iw, qabs, ik, ca, cat, wuv_bd, tril, acol))
    return jnp.concatenate(outs, axis=1)


def _layernorm_rows(y, g, b):
    mu = jnp.mean(y, axis=-1, keepdims=True)
    d = y - mu
    var = jnp.mean(d * d, axis=-1, keepdims=True)
    return d * lax.rsqrt(var + 1e-5) * g + b


def _mix_kernel(orw_ref, ods_ref, x_ref, gt_ref, sc_ref, sh_ref, wtop_ref, wbot_ref, g_ref, b_ref, wr_ref, br_ref,
                tril_ref, triu_ref, x1_o, u2_o, route_o, tbl_o, cnt_o, carry, *, alpha):
    first = (pl.program_id(0) == 0) & (pl.program_id(1) == 0)
    tm = x_ref.shape[0]

    @pl.when(first)
    def _():
        carry[...] = jnp.zeros_like(carry)

    mix = _bdot(orw_ref[...], wtop_ref[...]) + _bdot(ods_ref[...], wbot_ref[...])
    x1 = _layernorm_rows(alpha * x_ref[...] + (1.0 + gt_ref[...]) * mix, g_ref[...], b_ref[...])
    x1_o[...] = x1
    u2 = x1 * (1.0 + sc_ref[...]) + sh_ref[...]
    u2_o[...] = u2

    lg = _dot3(u2, wr_ref[...]) + br_ref[...]
    lane = lax.broadcasted_iota(I32, (tm, LANES), 1)
    lane_f = lane.astype(F32)
    idxs, vals = [], []
    for _ in range(TOP_K_EXPERTS):
        m = jnp.max(lg, axis=-1, keepdims=True)
        idx = jnp.min(jnp.where(lg == m, lane_f, float(LANES)), axis=-1, keepdims=True).astype(I32)
        idxs.append(idx)
        vals.append(m)
        lg = jnp.where(lane == idx, -jnp.inf, lg)
    es = [jnp.exp(v - vals[0]) for v in vals]
    den = es[0] + es[1] + es[2] + es[3]
    hot = jnp.zeros((tm, LANES), F32)
    for idx in idxs:
        hot = hot + jnp.where(lane == idx, 1.0, 0.0)
    before = jnp.dot(tril_ref[...], hot.astype(BF16), preferred_element_type=F32)
    cnt = jnp.sum(hot, axis=0, keepdims=True)
    n_run = jnp.floor((cnt + (RUN_ALIGN - 1)) * (1.0 / RUN_ALIGN))
    n_stage = jnp.floor((n_run * RUN_ALIGN + (STAGE_ALIGN - 1)) * (1.0 / STAGE_ALIGN))
    off = jnp.dot(jnp.broadcast_to(n_stage, (SUBLANES, LANES)).astype(BF16), triu_ref[...],
                  preferred_element_type=F32)[0:1, :] * STAGE_ALIGN
    where_in_stage = off + before
    route = jnp.zeros((tm, LANES), F32)
    for k in range(TOP_K_EXPERTS):
        spos = jnp.sum(jnp.where(lane == idxs[k], where_in_stage, 0.0), axis=-1, keepdims=True)
        route = jnp.where(lane == k, idxs[k].astype(F32), route)
        route = jnp.where(lane == TOP_K_EXPERTS + k, es[k] / den, route)
        route = jnp.where(lane == 2 * TOP_K_EXPERTS + k, spos, route)
    route_o[...] = route
    sub = lax.broadcasted_iota(I32, (SUBLANES, LANES), 0)
    tbl_o[...] = jnp.where(sub == 0, n_run, jnp.where(sub == 1, off, jnp.where(sub == 2, carry[0:1, :], 0.0)))
    carry[0:1, :] = carry[0:1, :] + n_run * RUN_ALIGN
    cnt_o[...] = carry[...]


def _mix_out(o_rwkv, o_dsa, x, gt1, sc2, sh2, w_out, ln_g, ln_b, w_router, b_router):
    B, S, D = x.shape
    tm = min(TM_PROJ, S)
    alpha = 2.0 ** 0.25
    wtop = w_out[:D_RWKV].astype(BF16)
    wbot = w_out[D_RWKV:].astype(BF16)
    wr_p = jnp.pad(w_router, ((0, 0), (0, LANES - N_EXPERTS)))
    br_p = jnp.pad(b_router.reshape(1, -1), ((0, 0), (0, LANES - N_EXPERTS)), constant_values=NEG_BIG)
    tril = jnp.asarray(np.tril(np.ones((tm, tm)), -1), BF16)
    triu = jnp.asarray(np.triu(np.ones((LANES, LANES)), 1), BF16)
    row = lambda v: v.reshape(1, -1)
    tok = lambda n: pl.BlockSpec((None, tm, n), lambda b, i: (b, i, 0))
    mod = pl.BlockSpec((None, 1, D), lambda b, i: (b, 0, 0))
    full = lambda a: pl.BlockSpec(a.shape, lambda b, i: (0,) * a.ndim)
    consts = [wtop, wbot, row(ln_g), row(ln_b), wr_p, br_p, tril, triu]
    return pl.pallas_call(
        functools.partial(_mix_kernel, alpha=alpha),
        grid=(B, S // tm),
        in_specs=[tok(D_RWKV), tok(D_ATT), tok(D), mod, mod, mod] + [full(a) for a in consts],
        out_specs=[tok(D), tok(D), tok(LANES), pl.BlockSpec((None, None, SUBLANES, LANES), lambda b, i: (b, i, 0, 0)),
                   pl.BlockSpec((SUBLANES, LANES), lambda b, i: (0, 0))],
        out_shape=[jax.ShapeDtypeStruct((B, S, D), F32), jax.ShapeDtypeStruct((B, S, D), F32),
                   jax.ShapeDtypeStruct((B, S, LANES), F32),
                   jax.ShapeDtypeStruct((B, S // tm, SUBLANES, LANES), F32),
                   jax.ShapeDtypeStruct((SUBLANES, LANES), F32)],
        scratch_shapes=[pltpu.VMEM((8, LANES), F32)],
        compiler_params=_cparams(("arbitrary", "arbitrary")),
        name="mix_out",
    )(o_rwkv, o_dsa, x, gt1, sc2, sh2, *consts)


RUN_ALIGN = SUBLANES
STAGE_ALIGN = 16
STAGE_ROWS = 1536


def _run_copies(nrun_ref, tile, copy_of):
    for e in range(N_EXPERTS):
        pieces = (nrun_ref[tile * N_EXPERTS + e] * RUN_ALIGN + STAGE_ALIGN - 1) // STAGE_ALIGN

        def piece(j, carry, e=e):
            copy_of(e, j).start(priority=e % 2)
            return carry

        lax.fori_loop(0, pieces, piece, 0)


def _dispatch_kernel(nrun_ref, off_ref, start_ref, npiece_ref, zs_ref, zn_ref, tail_ref,
                     u_ref, route_ref, xs_out, stag, zeros, sem, zsem):
    i = pl.program_id(0)
    n = pl.num_programs(0)
    tm = u_ref.shape[0]
    bm = zeros.shape[0]
    slot = i % 2

    @pl.when(i == 0)
    def _():
        zeros[...] = jnp.zeros_like(zeros)
        fills = []
        for e in range(N_EXPERTS):
            for b in range(bm.bit_length()):
                rows = RUN_ALIGN << b
                if rows > bm:
                    break
                done = (zn_ref[e] >> (b + 1)) << (b + 1)
                dst = pl.multiple_of(zs_ref[e] + done * RUN_ALIGN, RUN_ALIGN)
                fills.append(((zn_ref[e] >> b) & 1 == 1,
                              pltpu.make_async_copy(zeros.at[pl.ds(0, rows)], xs_out.at[pl.ds(dst, rows)], zsem)))
        for pred, cp in fills:
            pl.when(pred)(cp.start)

        def tail_copy(j):
            return pltpu.make_async_copy(zeros, xs_out.at[pl.ds(pl.multiple_of(j * bm, bm), bm)], zsem)

        lax.fori_loop(tail_ref[0], tail_ref[1], lambda j, c: (tail_copy(j).start(), c)[1], 0)
        for pred, cp in fills:
            pl.when(pred)(cp.wait)
        lax.fori_loop(tail_ref[0], tail_ref[1], lambda j, c: (tail_copy(j).wait(), c)[1], 0)

    route_t = jnp.transpose(route_ref[...])
    spos = [route_t[2 * TOP_K_EXPERTS + k:2 * TOP_K_EXPERTS + k + 1, :].astype(I32) for k in range(TOP_K_EXPERTS)]
    rows = lax.broadcasted_iota(I32, (STAGE_ROWS, tm), 0)
    sel = (rows == spos[0]) | (rows == spos[1]) | (rows == spos[2]) | (rows == spos[3])
    stag[slot] = jnp.dot(jnp.where(sel, 1.0, 0.0).astype(BF16), u_ref[...].astype(BF16), preferred_element_type=F32)

    def piece_copy(s, tile):
        def copy_of(e, j):
            src = pl.multiple_of(off_ref[tile * N_EXPERTS + e] + j * STAGE_ALIGN, STAGE_ALIGN)
            dst = pl.multiple_of(start_ref[tile * N_EXPERTS + e] + j * STAGE_ALIGN, RUN_ALIGN)
            return pltpu.make_async_copy(stag.at[s, pl.ds(src, STAGE_ALIGN)], xs_out.at[pl.ds(dst, STAGE_ALIGN)], sem)
        return copy_of

    def drain(tile):
        def w(j, carry):
            pltpu.make_async_copy(stag.at[0, pl.ds(0, STAGE_ALIGN)], xs_out.at[pl.ds(0, STAGE_ALIGN)], sem).wait()
            return carry
        lax.fori_loop(0, npiece_ref[tile], w, 0)

    @pl.when(i > 0)
    def _():
        drain(i - 1)

    _run_copies(nrun_ref, i, piece_copy(slot, i))

    @pl.when(i == n - 1)
    def _():
        drain(i)


def _moe_dispatch(u2, route, tables, n_rows):
    T, D = u2.shape
    tm = TM_ROUTE
    bm = BM_EXPERT
    return pl.pallas_call(
        _dispatch_kernel,
        grid_spec=pltpu.PrefetchScalarGridSpec(
            num_scalar_prefetch=len(tables),
            grid=(T // tm,),
            in_specs=[pl.BlockSpec((tm, D), lambda i, *_: (i, 0)),
                      pl.BlockSpec((tm, LANES), lambda i, *_: (i, 0))],
            out_specs=pl.BlockSpec(memory_space=pl.ANY),
            scratch_shapes=[pltpu.VMEM((2, STAGE_ROWS, D), F32), pltpu.VMEM((bm, D), F32),
                            pltpu.SemaphoreType.DMA(()), pltpu.SemaphoreType.DMA(())],
        ),
        out_shape=jax.ShapeDtypeStruct((n_rows, D), F32),
        compiler_params=_cparams(("arbitrary",)),
        name="moe_dispatch",
    )(*tables, u2, route)


GU_GROUP = 2 * LANES


def _deinterleave_perm():
    p = np.zeros((GU_GROUP, GU_GROUP), np.float32)
    l = np.arange(LANES)
    p[2 * l, l] = 1.0
    p[2 * l + 1, LANES + l] = 1.0
    return jnp.asarray(p, BF16)


def _expert_kernel(be_ref, nb_ref, valid_ref, xs_ref, wgu_hbm, bgu_ref, wd_hbm, bd_ref, perm_ref, ys_ref,
                   wg_buf, wd_buf, wp, wdb, sem):
    i = pl.program_id(0)
    bm = xs_ref.shape[0]
    e = be_ref[i]
    used = i < nb_ref[0]
    new_expert = (i == 0) | (e != be_ref[jnp.maximum(i - 1, 0)])
    n_groups = wp.shape[1] // GU_GROUP
    n_experts = wgu_hbm.shape[0]

    def fetch(ex):
        return (pltpu.make_async_copy(wgu_hbm.at[ex], wg_buf, sem.at[0]),
                pltpu.make_async_copy(wd_hbm.at[ex], wd_buf, sem.at[1]))

    @pl.when(used & new_expert)
    def _():
        @pl.when(i == 0)
        def _():
            for cp in fetch(e):
                cp.start()

        for cp in fetch(e):
            cp.wait()
        for j in range(n_groups):
            sl = slice(j * GU_GROUP, (j + 1) * GU_GROUP)
            wp[:, sl] = jnp.dot(wg_buf[:, sl].astype(BF16), perm_ref[...], preferred_element_type=F32).astype(BF16)
        wdb[...] = wd_buf[...].astype(BF16)

        @pl.when(e + 1 < n_experts)
        def _():
            for cp in fetch(e + 1):
                cp.start()

    def compute(m):
        xb = xs_ref[:m, :].astype(BF16)
        gu = jnp.dot(xb, wp[...], preferred_element_type=F32) + bgu_ref[...]
        hs = []
        for j in range(n_groups):
            gate = jnp.minimum(gu[:, j * GU_GROUP:j * GU_GROUP + LANES], SWIGLU_LIMIT)
            up = jnp.clip(gu[:, j * GU_GROUP + LANES:(j + 1) * GU_GROUP], -SWIGLU_LIMIT, SWIGLU_LIMIT)
            hs.append(((up + 1.0) * (gate * _sigmoid(gate * SWIGLU_ALPHA))).astype(BF16))
        h = jnp.concatenate(hs, axis=-1)
        ys_ref[:m, :] = jnp.dot(h, wdb[...], preferred_element_type=F32) + bd_ref[...]

    sizes = (bm, bm // 2, bm // 4)
    for m, smaller in zip(sizes, sizes[1:] + (0,)):
        @pl.when(used & (valid_ref[i] <= m) & (valid_ref[i] > smaller))
        def _(m=m):
            compute(m)
            if m < bm:
                ys_ref[m:, :] = jnp.zeros((bm - m, ys_ref.shape[1]), F32)

    @pl.when(used & (valid_ref[i] <= 0))
    def _():
        ys_ref[...] = jnp.zeros_like(ys_ref)

    @pl.when(jnp.logical_not(used))
    def _():
        ys_ref[...] = jnp.zeros_like(ys_ref)


def _moe_experts(xs, block_e, n_used, valid, w_gu, b_gu_p, w_dn, b_dn):
    n_rows, D = xs.shape
    E, _, F2 = w_gu.shape
    bm = BM_EXPERT
    n_blocks = n_rows // bm
    perm = _deinterleave_perm()
    wspec = lambda shp: pl.BlockSpec((None,) + shp, lambda i, be, nb, va: (be[i], 0, 0))
    hbm = pl.BlockSpec(memory_space=pl.ANY)
    return pl.pallas_call(
        _expert_kernel,
        grid_spec=pltpu.PrefetchScalarGridSpec(
            num_scalar_prefetch=3,
            grid=(n_blocks,),
            in_specs=[pl.BlockSpec((bm, D), lambda i, be, nb, va: (jnp.minimum(i, nb[0] - 1), 0)),
                      hbm, wspec((1, F2)), hbm, wspec((1, D)),
                      pl.BlockSpec(perm.shape, lambda i, be, nb, va: (0, 0))],
            out_specs=pl.BlockSpec((bm, D), lambda i, be, nb, va: (i, 0)),
            scratch_shapes=[pltpu.VMEM((D, F2), F32), pltpu.VMEM((F2 // 2, D), F32),
                            pltpu.VMEM((D, F2), BF16), pltpu.VMEM((F2 // 2, D), BF16),
                            pltpu.SemaphoreType.DMA((2,))],
        ),
        out_shape=jax.ShapeDtypeStruct((n_rows, D), F32),
        compiler_params=pltpu.CompilerParams(dimension_semantics=("arbitrary",), vmem_limit_bytes=VMEM_LIMIT_EXPERTS),
        name="moe_experts",
    )(block_e, n_used, valid, xs, w_gu, b_gu_p, w_dn, b_dn, perm)


def _combine_kernel(nrun_ref, off_ref, start_ref, npiece_ref, ys_ref, x1_ref, route_ref, gt_ref, g_ref,
                    b_ref, o_ref, stag, sem, *, alpha):
    i = pl.program_id(0)
    n = pl.num_programs(0)
    tm = x1_ref.shape[0]
    slot = i % 2

    def gather(tile, s):
        def copy_of(e, j):
            src = pl.multiple_of(start_ref[tile * N_EXPERTS + e] + j * STAGE_ALIGN, RUN_ALIGN)
            dst = pl.multiple_of(off_ref[tile * N_EXPERTS + e] + j * STAGE_ALIGN, STAGE_ALIGN)
            return pltpu.make_async_copy(ys_ref.at[pl.ds(src, STAGE_ALIGN)], stag.at[s, pl.ds(dst, STAGE_ALIGN)],
                                         sem.at[s])
        _run_copies(nrun_ref, tile, copy_of)

    @pl.when(i == 0)
    def _():
        stag[...] = jnp.zeros_like(stag)
        gather(0, 0)

    @pl.when(i + 1 < n)
    def _():
        gather(i + 1, 1 - slot)

    def w(j, carry):
        pltpu.make_async_copy(ys_ref.at[pl.ds(0, STAGE_ALIGN)], stag.at[slot, pl.ds(0, STAGE_ALIGN)],
                              sem.at[slot]).wait()
        return carry
    lax.fori_loop(0, npiece_ref[i], w, 0)

    route = route_ref[...]
    cols = lax.broadcasted_iota(I32, (tm, STAGE_ROWS), 1)
    wgt = jnp.zeros((tm, STAGE_ROWS), F32)
    for k in range(TOP_K_EXPERTS):
        spos = route[:, 2 * TOP_K_EXPERTS + k:2 * TOP_K_EXPERTS + k + 1].astype(I32)
        wgt = wgt + jnp.where(cols == spos, route[:, TOP_K_EXPERTS + k:TOP_K_EXPERTS + k + 1], 0.0)
    ffn = jnp.dot(wgt.astype(BF16), stag[slot].astype(BF16), preferred_element_type=F32)
    o_ref[...] = _layernorm_rows(alpha * x1_ref[...] + (1.0 + gt_ref[...]) * ffn, g_ref[...], b_ref[...])


def _moe_combine(ys, tables, x1, route, gt2, ln_g, ln_b, tiles_per_batch):
    T, D = x1.shape
    tm = TM_ROUTE
    row = lambda v: v.reshape(1, -1)
    return pl.pallas_call(
        functools.partial(_combine_kernel, alpha=2.0 ** 0.25),
        grid_spec=pltpu.PrefetchScalarGridSpec(
            num_scalar_prefetch=len(tables),
            grid=(T // tm,),
            in_specs=[pl.BlockSpec(memory_space=pl.ANY),
                      pl.BlockSpec((tm, D), lambda i, *_: (i, 0)),
                      pl.BlockSpec((tm, LANES), lambda i, *_: (i, 0)),
                      pl.BlockSpec((None, 1, D), lambda i, *_: (i // tiles_per_batch, 0, 0)),
                      pl.BlockSpec((1, D), lambda i, *_: (0, 0)),
                      pl.BlockSpec((1, D), lambda i, *_: (0, 0))],
            out_specs=pl.BlockSpec((tm, D), lambda i, *_: (i, 0)),
            scratch_shapes=[pltpu.VMEM((2, STAGE_ROWS, D), F32), pltpu.SemaphoreType.DMA((2,))],
        ),
        out_shape=jax.ShapeDtypeStruct((T, D), F32),
        compiler_params=_cparams(("arbitrary",)),
        name="moe_combine",
    )(*tables, ys, x1, route, gt2, row(ln_g), row(ln_b))


def _moe_and_norm(x1, u2, route, tbl, totals, gt2, w_gu, b_gu, w_dn, b_dn, ln_g, ln_b):
    B, S, D = x1.shape
    T = B * S
    bm = BM_EXPERT
    tm = TM_ROUTE
    assert T % tm == 0 and TM_PROJ == tm and STAGE_ROWS >= tm * TOP_K_EXPERTS + N_EXPERTS * (STAGE_ALIGN - 1)
    n_tiles = T // tm
    max_rows = T * TOP_K_EXPERTS + n_tiles * N_EXPERTS * (RUN_ALIGN - 1) + N_EXPERTS * STAGE_ALIGN
    n_blocks = -(-max_rows // bm) + N_EXPERTS
    tot = totals[0, :N_EXPERTS].astype(I32)
    padded = (tot + STAGE_ALIGN + bm - 1) // bm * bm
    pad_ends = jnp.cumsum(padded)
    pad_starts = pad_ends - padded
    t3 = tbl.reshape(n_tiles, SUBLANES, LANES)[:, :, :N_EXPERTS].astype(I32)
    nrun, off, base = t3[:, 0, :], t3[:, 1, :], t3[:, 2, :]
    start = pad_starts[None, :] + base
    pieces = (nrun * RUN_ALIGN + STAGE_ALIGN - 1) // STAGE_ALIGN
    npiece = jnp.sum(pieces, axis=1)
    flat = lambda z: z.reshape(-1).astype(I32)
    run_tables = [flat(nrun), flat(off), flat(start), flat(npiece)]
    zs = pad_starts + tot
    zn = (pad_ends - zs) // RUN_ALIGN
    n_used = (pad_ends[-1:] // bm).astype(I32)
    tail = jnp.concatenate([n_used, jnp.full((1,), n_blocks, I32)])
    blk_row = jnp.arange(n_blocks, dtype=I32) * bm
    block_e = jnp.minimum(jnp.sum((blk_row[:, None] >= pad_ends[None, :]).astype(I32), axis=1), N_EXPERTS - 1)
    E, F2 = b_gu.shape
    b_gu_p = b_gu.reshape(E, F2 // GU_GROUP, LANES, 2).transpose(0, 1, 3, 2).reshape(E, 1, F2)
    route2 = route.reshape(T, LANES)
    xs = _moe_dispatch(u2.reshape(T, D), route2, run_tables + [flat(zs), flat(zn), tail], n_blocks * bm)
    valid = jnp.clip(zs[block_e] - blk_row, 0, bm)
    ys = _moe_experts(xs, block_e, n_used, valid, w_gu, b_gu_p, w_dn, b_dn[:, None, :])
    out = _moe_combine(ys, run_tables, x1.reshape(T, D), route2, gt2, ln_g, ln_b, S // tm)
    return out.reshape(B, S, D)


def kernel(x, c, w_ada, b_ada, w_in, shift_mu, rwkv_w0, rwkv_w2, rwkv_a0, rwkv_a2, rwkv_g2, rwkv_k_k, rwkv_k_a, rwkv_r_k, rwkv_ln_w, rwkv_ln_b, mla_q_norm, mla_w_q_up, mla_kv_norm, mla_w_uk, mla_w_uv, idx_w_q, idx_ln_g, idx_ln_b, w_out, ln1_g, ln1_b, w_router, b_router, w_gu, b_gu, w_dn, b_dn, ln2_g, ln2_b):
    depth = w_ada.shape[0]
    assert depth == 1, "DeepNorm constants below are for a single layer"
    l = 0
    mod = _ada_mod(c, w_ada[l], b_ada[l])
    sh1, sc1, gt1, sh2, sc2, gt2 = [m[:, None, :] for m in jnp.split(mod, 6, axis=-1)]
    r, lw, k, v, kk, a, g, qabs, iq, ik, iw, cl, clt = _in_proj(
        x, sc1, sh1, w_in[l], shift_mu[l], rwkv_w0[l], rwkv_w2[l], rwkv_a0[l], rwkv_a2[l], rwkv_g2[l], rwkv_k_k[l],
        rwkv_k_a[l], mla_q_norm[l], mla_w_q_up[l], mla_w_uk[l], mla_kv_norm[l], idx_w_q[l], idx_ln_g[l], idx_ln_b[l])
    o_rwkv = _rwkv_scan(r, lw, k, v, kk, a, g, rwkv_r_k[l], rwkv_ln_w[l], rwkv_ln_b[l])
    o_dsa = _dsa_attn(iq, iw, qabs, ik, cl, clt, mla_w_uv[l])
    x1, u2, route, tbl, totals = _mix_out(o_rwkv, o_dsa, x, gt1, sc2, sh2, w_out[l], ln1_g[l], ln1_b[l],
                                          w_router[l], b_router[l])
    return _moe_and_norm(x1, u2, route, tbl, totals, gt2, w_gu[l], b_gu[l], w_dn[l], b_dn[l], ln2_g[l], ln2_b[l])
```

```python
import functools
import math

import jax
import jax.numpy as jnp
import numpy as np
from jax import lax
from jax.experimental import pallas as pl
from jax.experimental.pallas import tpu as pltpu

F32 = jnp.float32
BF16 = jnp.bfloat16
I32 = jnp.int32

RWKV_HEAD = 64
N_RWKV_HEADS = 8
D_RWKV = RWKV_HEAD * N_RWKV_HEADS
RWKV_GN_EPS = 64e-5
ATT_HEAD = 64
N_ATT_HEADS = 8
D_ATT = ATT_HEAD * N_ATT_HEADS
KV_LORA = 128
IDX_HEADS = 8
IDX_DIM = 64
TOPK_MAX = 256
N_EXPERTS = 32
TOP_K_EXPERTS = 4
SWIGLU_LIMIT = 7.0
SWIGLU_ALPHA = 1.702
NEG_BIG = -1e30
LOG2E = 1.4426950408889634
INT_MIN = -(2 ** 31)

LANES = 128
SUBLANES = 8
VMEM_LIMIT = 48 * 1024 * 1024
VMEM_LIMIT_EXPERTS = 58 * 1024 * 1024

TM_IN = 512
TM_PROJ = 256
L_CHUNK = 64
CHUNKS_PER_STEP = 4
TQ = 256
KEY_CHUNK = 256
SUM_ROWS = 128
TM_ROUTE = 256
BM_EXPERT = 1024

_SEG = (("r", 512, 512), ("k", 512, 512), ("v", 512, 512), ("w", 64, 128), ("a", 64, 128), ("g", 128, 128),
        ("q", 256, 256), ("kv", 128, 128), ("ik", 64, 128), ("iw", 8, 128))
N_SHIFT_P = 512 * 3 + 128 * 3
N_IN_P = sum(s[2] for s in _SEG)


def _cparams(sem):
    return pltpu.CompilerParams(dimension_semantics=sem, vmem_limit_bytes=VMEM_LIMIT)


def _bdot(a, b):
    return jnp.dot(a.astype(BF16), b.astype(BF16), preferred_element_type=F32)


def _split2(a):
    hi = a.astype(BF16)
    lo = (a - hi.astype(F32)).astype(BF16)
    return hi, lo


def _split3(a):
    hi = a.astype(BF16)
    r1 = a - hi.astype(F32)
    mid = r1.astype(BF16)
    lo = (r1 - mid.astype(F32)).astype(BF16)
    return hi, mid, lo


def _dot3(a, b, dims=(((1,), (0,)), ((), ()))):
    ah, al = _split2(a)
    bh, bl = _split2(b)
    d = functools.partial(lax.dot_general, dimension_numbers=dims, preferred_element_type=F32)
    return d(ah, bh) + (d(ah, bl) + d(al, bh))


def _dot_exact_rhs(a, b_exact, nsplit=3):
    parts = _split3(a) if nsplit == 3 else _split2(a)
    acc = None
    for p in parts[::-1]:
        t = jnp.dot(p, b_exact, preferred_element_type=F32)
        acc = t if acc is None else acc + t
    return acc


def _dot_exact_lhs(a_exact, b, nsplit=3):
    parts = _split3(b) if nsplit == 3 else _split2(b)
    acc = None
    for p in parts[::-1]:
        t = jnp.dot(a_exact, p, preferred_element_type=F32)
        acc = t if acc is None else acc + t
    return acc


def _sigmoid(x):
    return 1.0 / (1.0 + jnp.exp(-x))


def _softplus(x):
    return jnp.maximum(x, 0.0) + jnp.log(1.0 + jnp.exp(-jnp.abs(x)))


def _ada_kernel(c_ref, w_ref, b_ref, o_ref):
    c = c_ref[...]
    o_ref[...] = _dot3(c * _sigmoid(c), w_ref[...]) + b_ref[...]


def _ada_mod(c, w_ada, b_ada):
    B, D = c.shape
    N = w_ada.shape[1]
    tn = 1024
    return pl.pallas_call(
        _ada_kernel,
        grid=(N // tn,),
        in_specs=[pl.BlockSpec((B, D), lambda j: (0, 0)),
                  pl.BlockSpec((D, tn), lambda j: (0, j)),
                  pl.BlockSpec((1, tn), lambda j: (0, j))],
        out_specs=pl.BlockSpec((B, tn), lambda j: (0, j)),
        out_shape=jax.ShapeDtypeStruct((B, N), F32),
        compiler_params=_cparams(("arbitrary",)),
        name="ada_mod",
    )(c, w_ada, b_ada.reshape(1, N))


def _in_proj_kernel(x_ref, sc_ref, sh_ref, win_ref, mu_ref, w0_ref, w2_ref, a0_ref, a2_ref, g2_ref, kk_ref, ka_ref,
                    ones_ref, qn_ref, wq_ref, wuk_ref, kvn_ref, wiq_ref, ig_ref, ib_ref,
                    r_o, lw_o, k_o, v_o, kkn_o, a_o, g_o, qabs_o, iq_o, ik_o, iw_o, cl_o, clt_o, carry):
    i = pl.program_id(1)
    tm = x_ref.shape[0]

    @pl.when(i == 0)
    def _():
        carry[...] = jnp.zeros_like(carry)

    u = x_ref[...] * (1.0 + sc_ref[...]) + sh_ref[...]
    p = _bdot(u, win_ref[...])
    ps = p[:, :N_SHIFT_P]
    rows = lax.broadcasted_iota(I32, (tm, 1), 0)
    prev = jnp.where(rows == 0, carry[0:1, :], pltpu.roll(ps, 1, 0))
    carry[0:1, :] = ps[tm - 1:tm, :]
    ps = ps + mu_ref[...] * (prev - ps)

    pr, pk, pv = ps[:, 0:512], ps[:, 512:1024], ps[:, 1024:1536]
    pw, pa, pg = ps[:, 1536:1664], ps[:, 1664:1792], ps[:, 1792:1920]
    w_log = -_softplus(-(w0_ref[...] + _dot3(jnp.tanh(pw), w2_ref[...]))) - 0.5
    lw_o[...] = -jnp.exp(w_log)
    a = _sigmoid(a0_ref[...] + _dot3(pa, a2_ref[...]))
    g_o[...] = _dot3(_sigmoid(pg), g2_ref[...])
    kk = pk * kk_ref[...]
    ssq = _dot_exact_rhs(kk * kk, ones_ref[...], nsplit=2)
    kkn_o[...] = kk / jnp.maximum(jnp.sqrt(ssq), 1e-12)
    k_o[...] = pk * (1.0 + (a - 1.0) * ka_ref[...])
    r_o[...] = pr
    v_o[...] = pv
    a_o[...] = a

    pq, pkv = p[:, 1920:2176], p[:, 2176:2304]
    pik, piw = p[:, 2304:2432], p[:, 2432:2560]
    q_lat = pq * lax.rsqrt(jnp.mean(pq * pq, axis=-1, keepdims=True) + 1e-6) * qn_ref[...]
    q = _bdot(q_lat, wq_ref[...])
    qabs_o[...] = (_bdot(q, wuk_ref[...]) * (ATT_HEAD ** -0.5 * LOG2E)).astype(BF16)
    c_lat = pkv * lax.rsqrt(jnp.mean(pkv * pkv, axis=-1, keepdims=True) + 1e-6) * kvn_ref[...]
    spos = i * tm + rows
    s_hi = (spos >> 7).astype(F32)
    s_lo = (spos & (LANES - 1)).astype(F32)
    lane_t = lax.broadcasted_iota(I32, (1, LANES), 1)
    extra = jnp.where(lane_t == 0, 1.0, jnp.where((lane_t == 1) | (lane_t == 2), s_hi,
                                                  jnp.where((lane_t == 3) | (lane_t == 4), s_lo, 0.0)))
    c_aug = jnp.concatenate([c_lat, extra], axis=-1)
    cl_o[...] = c_aug.astype(BF16)
    clt_o[...] = jnp.transpose(c_aug).astype(BF16)
    iq_o[...] = (_bdot(q_lat, wiq_ref[...]) * (IDX_DIM ** -0.5)).astype(BF16)
    lane = lax.broadcasted_iota(I32, (1, LANES), 1)
    valid = lane < IDX_DIM
    mu = jnp.sum(pik, axis=-1, keepdims=True) * (1.0 / IDX_DIM)
    dlt = jnp.where(valid, pik - mu, 0.0)
    var = jnp.sum(dlt * dlt, axis=-1, keepdims=True) * (1.0 / IDX_DIM)
    ik_o[...] = jnp.where(valid, dlt * lax.rsqrt(var + 1e-5) * ig_ref[...] + ib_ref[...], 0.0).astype(BF16)
    iw_o[...] = piw * (IDX_HEADS ** -0.5)


def _pad_cols(w, widths):
    parts, o = [], 0
    for true, padded in widths:
        seg = w[..., o:o + true]
        if padded > true:
            seg = jnp.pad(seg, [(0, 0)] * (w.ndim - 1) + [(0, padded - true)])
        parts.append(seg)
        o += true
    return jnp.concatenate(parts, axis=-1)


def _pad_rows(w, rows):
    return jnp.pad(w, ((0, rows - w.shape[0]), (0, 0)))


def _block_diag(blocks):
    H, a, b = blocks.shape
    eye = jnp.eye(H, dtype=blocks.dtype)
    return (eye[:, None, :, None] * blocks[:, :, None, :]).reshape(H * a, H * b)


def _head_ones(n, head):
    idx = np.arange(n) // head
    return jnp.asarray(idx[:, None] == idx[None, :], BF16)


def _in_proj(x, sc1, sh1, w_in, shift_mu, w0, w2, a0, a2, g2, k_k, k_a, q_norm, w_q_up, w_uk, kv_norm, idx_w_q,
             idx_ln_g, idx_ln_b):
    B, S, D = x.shape
    tm = min(TM_IN, S)
    widths = tuple((s[1], s[2]) for s in _SEG)
    win_p = _pad_cols(w_in, widths).astype(BF16)
    mu_p = _pad_cols(shift_mu.reshape(1, -1), widths[:6])
    w2_p = _pad_rows(w2, LANES)
    a2_p = _pad_rows(a2, LANES)
    wuk_bd = _block_diag(w_uk).astype(BF16)
    wiq_p = _pad_cols(idx_w_q, ((IDX_DIM, LANES),) * IDX_HEADS).astype(BF16)
    ig_p = _pad_cols(idx_ln_g.reshape(1, -1), ((IDX_DIM, LANES),))
    ib_p = _pad_cols(idx_ln_b.reshape(1, -1), ((IDX_DIM, LANES),))
    row = lambda v: v.reshape(1, -1)
    tok = lambda n: pl.BlockSpec((None, tm, n), lambda b, i: (b, i, 0))
    mod = pl.BlockSpec((None, 1, D), lambda b, i: (b, 0, 0))
    full = lambda a: pl.BlockSpec(a.shape, lambda b, i: (0,) * a.ndim)
    consts = [win_p, mu_p, row(w0), w2_p, row(a0), a2_p, g2, row(k_k), row(k_a), _head_ones(D_RWKV, RWKV_HEAD),
              row(q_norm), w_q_up.astype(BF16), wuk_bd, row(kv_norm), wiq_p, ig_p, ib_p]
    outs = [(D_RWKV, F32)] * 7 + [(N_ATT_HEADS * KV_LORA, BF16), (IDX_HEADS * LANES, BF16), (LANES, BF16),
                                  (LANES, F32), (KV_LORA + LANES, BF16)]
    return pl.pallas_call(
        _in_proj_kernel,
        grid=(B, S // tm),
        in_specs=[tok(D), mod, mod] + [full(a) for a in consts],
        out_specs=[tok(n) for n, _ in outs] + [pl.BlockSpec((None, KV_LORA + LANES, tm), lambda b, i: (b, 0, i))],
        out_shape=[jax.ShapeDtypeStruct((B, S, n), dt) for n, dt in outs]
        + [jax.ShapeDtypeStruct((B, KV_LORA + LANES, S), BF16)],
        scratch_shapes=[pltpu.VMEM((8, N_SHIFT_P), F32)],
        compiler_params=_cparams(("arbitrary", "arbitrary")),
        name="in_proj",
    )(x, sc1, sh1, *consts)


def _rwkv_kernel(r_ref, lw_ref, k_ref, v_ref, kk_ref, a_ref, g_ref, rk_ref, lnw_ref, lnb_ref, tri_ref, ones_ref,
                 o_ref, state, *, L):
    c = pl.program_id(1)
    nc = r_ref.shape[0] // L

    @pl.when(c == 0)
    def _():
        state[...] = jnp.zeros_like(state)

    r, lw, k, v, kk, a = r_ref[...], lw_ref[...], k_ref[...], v_ref[...], kk_ref[...], a_ref[...]
    cum = _dot_exact_lhs(tri_ref[...], lw)
    rows = [slice(ci * L, (ci + 1) * L) for ci in range(nc)]
    last = [cum[(ci + 1) * L - 1:(ci + 1) * L, :] for ci in range(nc)]
    cum_last = jnp.concatenate([jnp.broadcast_to(z, (L, z.shape[1])) for z in last], axis=0)
    w_incl = jnp.exp(cum)
    w_inv = jnp.exp(-cum)
    w_rel = jnp.exp(cum_last - cum)
    w_last = [jnp.exp(z) for z in last]
    bvec = kk * a
    at = -kk * jnp.exp(cum - lw)
    rt = r * w_incl
    bt = bvec * w_inv
    kt = k * w_inv
    bh = bvec * w_rel
    kh = k * w_rel
    ti = lax.broadcasted_iota(I32, (L, L), 0)
    tj = lax.broadcasted_iota(I32, (L, L), 1)
    strict = tj < ti
    incl = tj <= ti
    eye = lax.broadcasted_iota(I32, (RWKV_HEAD, RWKV_HEAD), 0) == lax.broadcasted_iota(I32, (RWKV_HEAD, RWKV_HEAD), 1)
    NT = (((1,), (1,)), ((), ()))
    TN = (((0,), (0,)), ((), ()))
    heads = range(N_RWKV_HEADS)
    sls = [slice(h * RWKV_HEAD, (h + 1) * RWKV_HEAD) for h in heads]
    units = [(ci, h) for ci in range(nc) for h in heads]
    idx = range(len(units))
    mm = lambda x, y, dims=(((1,), (0,)), ((), ())): lax.dot_general(
        x.astype(BF16), y.astype(BF16), dims, preferred_element_type=F32)
    at_b, rt_b, bt_b, kt_b, bh_b, kh_b, v_b = [z.astype(BF16) for z in (at, rt, bt, kt, bh, kh, v)]
    cut = lambda z, u: z[rows[u[0]], sls[u[1]]]
    vh = [cut(v_b, u) for u in units]
    ath = [cut(at_b, u) for u in units]
    ar = [jnp.concatenate([cut(at_b, u), cut(rt_b, u)], axis=0) for u in units]
    g_b = [mm(ar[i], cut(bt_b, units[i]), NT) for i in idx]
    g_k = [mm(ar[i], cut(kt_b, units[i]), NT) for i in idx]
    n_ab = [jnp.where(strict, g_b[i][:L], 0.0) for i in idx]
    a_ak = [jnp.where(strict, g_k[i][:L], 0.0) for i in idx]
    a_rb = [jnp.where(incl, g_b[i][L:], 0.0).astype(BF16) for i in idx]
    a_rk = [jnp.where(incl, g_k[i][L:], 0.0) for i in idx]
    akv = [mm(a_ak[i], vh[i]) for i in idx]
    eye_l = jnp.where(ti == tj, 1.0, 0.0)
    tinv = [eye_l + n_ab[i] for i in idx]
    pw = n_ab
    for _ in range(int(math.log2(L)) - 1):
        pw = [mm(pw[i], pw[i]) for i in idx]
        tinv = [tinv[i] + mm(pw[i], tinv[i]) for i in idx]
    tinv = [t.astype(BF16) for t in tinv]
    a_t = [mm(tinv[i], ath[i]).astype(BF16) for i in idx]
    y = [mm(tinv[i], akv[i]).astype(BF16) for i in idx]
    m_c = [jnp.where(eye, w_last[units[i][0]][:, sls[units[i][1]]], 0.0) + mm(a_t[i], cut(bh_b, units[i]), TN)
           for i in idx]
    c_c = [mm(y[i], cut(bh_b, units[i]), TN) + mm(vh[i], cut(kh_b, units[i]), TN) for i in idx]
    q_c = [cut(rt, units[i]) + mm(a_rb[i], a_t[i]) for i in idx]
    o_loc = [mm(a_rb[i], y[i]) + mm(a_rk[i], vh[i]) for i in idx]
    s = [state[h] for h in heads]
    for ci in range(nc):
        for h in heads:
            i = ci * N_RWKV_HEADS + h
            o = o_loc[i] + mm(q_c[i], s[h], NT)
            s[h] = mm(s[h], m_c[i]) + c_c[i]
            mu = jnp.mean(o, axis=-1, keepdims=True)
            d = o - mu
            var = jnp.mean(d * d, axis=-1, keepdims=True)
            o_ref[rows[ci], sls[h]] = d * lax.rsqrt(var + RWKV_GN_EPS)
    for h in heads:
        state[h] = s[h]
    bonus = _dot_exact_rhs(r * k * rk_ref[...], ones_ref[...], nsplit=3) * v
    o_ref[...] = (o_ref[...] * lnw_ref[...] + lnb_ref[...] + bonus) * g_ref[...]


def _rwkv_scan(r, lw, k, v, kk, a, g, r_k, ln_w, ln_b):
    B, S, DR = r.shape
    L = min(L_CHUNK, S)
    lb = min(L * CHUNKS_PER_STEP, S)
    tri = jnp.asarray(np.kron(np.eye(lb // L), np.tril(np.ones((L, L)))), BF16)
    row = lambda z: z.reshape(1, -1)
    tok = pl.BlockSpec((None, lb, DR), lambda b, c: (b, c, 0))
    full = lambda z: pl.BlockSpec(z.shape, lambda b, c: (0,) * z.ndim)
    consts = [row(r_k), row(ln_w), row(ln_b), tri, _head_ones(DR, RWKV_HEAD)]
    return pl.pallas_call(
        functools.partial(_rwkv_kernel, L=L),
        grid=(B, S // lb),
        in_specs=[tok] * 7 + [full(z) for z in consts],
        out_specs=tok,
        out_shape=jax.ShapeDtypeStruct((B, S, DR), F32),
        scratch_shapes=[pltpu.VMEM((N_RWKV_HEADS, RWKV_HEAD, RWKV_HEAD), F32)],
        compiler_params=_cparams(("arbitrary", "arbitrary")),
        name="rwkv_scan",
    )(r, lw, k, v, kk, a, g, *consts)


def _alibi_cols():
    slope = np.asarray([2.0 ** (-8.0 * (h + 1) / N_ATT_HEADS) * LOG2E for h in range(N_ATT_HEADS)], np.float32)
    c_hi = slope.astype(BF16).astype(np.float32)
    c_lo = (slope - c_hi).astype(BF16).astype(np.float32)
    t = np.zeros((N_ATT_HEADS, LANES), np.float32)
    t[:, 1], t[:, 2], t[:, 3], t[:, 4] = LANES * c_hi, LANES * c_lo, c_hi, c_lo
    return jnp.asarray(t.astype(BF16))


def _dsa_kernel(iq_ref, iw_ref, qa_ref, ik_ref, ca_ref, cat_ref, wuv_ref, tril_ref, acol_ref, o_ref, key_ref,
                bias_ref, lg_ref, p_ref, *, topk, q_off, select):
    qi = pl.program_id(1) + q_off
    tq = qa_ref.shape[0]
    sk = ca_ref.shape[0]
    kc = min(KEY_CHUNK, sk)
    chunks = [slice(c * kc, (c + 1) * kc) for c in range(sk // kc)]
    tpos = qi * tq + lax.broadcasted_iota(I32, (1, tq), 1)
    srow = lax.broadcasted_iota(I32, (kc, tq), 0)
    NT = (((1,), (1,)), ((), ()))

    if not select:
        for c, cs in enumerate(chunks):
            bias_ref[cs, :] = jnp.where(srow + c * kc <= tpos, 0.0, NEG_BIG)
    else:
        iw_t = jnp.transpose(iw_ref[...])
        for c, cs in enumerate(chunks):
            ikc = ik_ref[cs, :]
            score = jnp.zeros((kc, tq), F32)
            for h in range(IDX_HEADS):
                s = lax.dot_general(ikc, iq_ref[:, h * LANES:(h + 1) * LANES], NT, preferred_element_type=F32)
                score = score + iw_t[h:h + 1, :] * jnp.maximum(s, 0.0)
            bits = pltpu.bitcast(score + 0.0, I32)
            key = bits ^ ((bits >> 31) & 0x7FFFFFFF)
            key_ref[cs, :] = jnp.where(srow + c * kc <= tpos, key, INT_MIN)
        kcount = jnp.minimum(topk, tpos + 1).astype(F32)

        def sum_keys(x):
            part = jnp.sum(x.reshape(x.shape[0] // SUM_ROWS, SUM_ROWS, x.shape[1]), axis=0)
            return jnp.sum(part, axis=0, keepdims=True)

        def count_ge(cand):
            return sum_keys(jnp.where(key_ref[...] >= cand, 1.0, 0.0))

        thr = jnp.where(count_ge(jnp.zeros((1, tq), I32)) >= kcount, 0, INT_MIN).astype(I32)

        def bit_step(i, thr):
            cand = thr | (1 << (30 - i))
            return jnp.where(count_ge(cand) >= kcount, cand, thr)

        thr = lax.fori_loop(0, 31, bit_step, thr)

        need = kcount - sum_keys(jnp.where(key_ref[...] > thr, 1.0, 0.0))
        carry = jnp.zeros((1, tq), F32)
        for j in range(sk // LANES):
            sl = slice(j * LANES, (j + 1) * LANES)
            key = key_ref[sl, :]
            eq = key == thr
            e = jnp.where(eq, 1.0, 0.0)
            before = jnp.dot(tril_ref[...], e.astype(BF16), preferred_element_type=F32) + carry
            sel = (key > thr) | (eq & (before < need))
            bias_ref[sl, :] = jnp.where(sel, 0.0, NEG_BIG)
            carry = carry + jnp.sum(e, axis=0, keepdims=True)

    outs = []
    for h in range(N_ATT_HEADS):
        q_aug = jnp.concatenate([qa_ref[:, h * KV_LORA:(h + 1) * KV_LORA],
                                 jnp.broadcast_to(acol_ref[h:h + 1, :], (tq, LANES))], axis=-1)
        lg_all = lax.dot_general(ca_ref[...], q_aug, NT, preferred_element_type=F32)
        m = jnp.full((1, tq), -jnp.inf, F32)
        for cs in chunks:
            lg = lg_all[cs, :] + bias_ref[cs, :]
            lg_ref[cs, :] = lg
            m = jnp.maximum(m, jnp.max(lg, axis=0, keepdims=True))
        for cs in chunks:
            p_ref[cs, :] = jnp.exp2(lg_ref[cs, :] - m).astype(BF16)
        pv = jnp.dot(cat_ref[...], p_ref[...], preferred_element_type=F32)
        outs.append((pv[:KV_LORA, :] / pv[KV_LORA:KV_LORA + 1, :]).astype(BF16))
    o_lat_t = jnp.concatenate(outs, axis=0)
    o_ref[...] = lax.dot_general(o_lat_t, wuv_ref[...], (((0,), (0,)), ((), ())), preferred_element_type=F32)


def _dsa_attn(iq, iw, qabs, ik, ca, cat, w_uv):
    B, S, _ = iq.shape
    tq = min(TQ, S)
    topk = min(TOPK_MAX, S // 4)
    wuv_bd = _block_diag(w_uv).astype(BF16)
    tril = jnp.asarray(np.tril(np.ones((LANES, LANES)), -1), BF16)
    acol = _alibi_cols()
    full = lambda z: pl.BlockSpec(z.shape, lambda b, i: (0,) * z.ndim)
    nq = 1
    outs = []
    for q_off in range(0, S // tq, nq):
        sk = (q_off + nq) * tq
        tok = lambda n, q_off=q_off: pl.BlockSpec((None, tq, n), lambda b, i: (b, i + q_off, 0))
        seq = lambda n, sk=sk: pl.BlockSpec((None, sk, n), lambda b, i: (b, 0, 0))
        outs.append(pl.pallas_call(
            functools.partial(_dsa_kernel, topk=topk, q_off=q_off, select=sk > topk),
            grid=(B, nq),
            in_specs=[tok(IDX_HEADS * LANES), tok(LANES), tok(N_ATT_HEADS * KV_LORA), seq(LANES), seq(KV_LORA + LANES),
                      pl.BlockSpec((None, KV_LORA + LANES, sk), lambda b, i: (b, 0, 0)),
                      full(wuv_bd), full(tril), full(acol)],
            out_specs=pl.BlockSpec((None, tq, D_ATT), lambda b, i: (b, i, 0)),
            out_shape=jax.ShapeDtypeStruct((B, nq * tq, D_ATT), F32),
            scratch_shapes=[pltpu.VMEM((sk, tq), I32), pltpu.VMEM((sk, tq), F32), pltpu.VMEM((sk, tq), F32),
                            pltpu.VMEM((sk, tq), BF16)],
            compiler_params=_cparams(("arbitrary", "arbitrary")),
            name=f"dsa_attn_k{sk}",
        )(iq, iw, qabs, ik, ca, cat, wuv_bd, tril, acol))
    return jnp.concatenate(outs, axis=1)


def _layernorm_rows(y, g, b):
    mu = jnp.mean(y, axis=-1, keepdims=True)
    d = y - mu
    var = jnp.mean(d * d, axis=-1, keepdims=True)
    return d * lax.rsqrt(var + 1e-5) * g + b


def _mix_kernel(orw_ref, ods_ref, x_ref, gt_ref, sc_ref, sh_ref, wtop_ref, wbot_ref, g_ref, b_ref, wr_ref, br_ref,
                tril_ref, triu_ref, x1_o, u2_o, route_o, tbl_o, cnt_o, carry, *, alpha):
    first = (pl.program_id(0) == 0) & (pl.program_id(1) == 0)
    tm = x_ref.shape[0]

    @pl.when(first)
    def _():
        carry[...] = jnp.zeros_like(carry)

    mix = _bdot(orw_ref[...], wtop_ref[...]) + _bdot(ods_ref[...], wbot_ref[...])
    x1 = _layernorm_rows(alpha * x_ref[...] + (1.0 + gt_ref[...]) * mix, g_ref[...], b_ref[...])
    x1_o[...] = x1
    u2 = x1 * (1.0 + sc_ref[...]) + sh_ref[...]
    u2_o[...] = u2

    lg = _dot3(u2, wr_ref[...]) + br_ref[...]
    lane = lax.broadcasted_iota(I32, (tm, LANES), 1)
    lane_f = lane.astype(F32)
    idxs, vals = [], []
    for _ in range(TOP_K_EXPERTS):
        m = jnp.max(lg, axis=-1, keepdims=True)
        idx = jnp.min(jnp.where(lg == m, lane_f, float(LANES)), axis=-1, keepdims=True).astype(I32)
        idxs.append(idx)
        vals.append(m)
        lg = jnp.where(lane == idx, -jnp.inf, lg)
    es = [jnp.exp(v - vals[0]) for v in vals]
    den = es[0] + es[1] + es[2] + es[3]
    hot = jnp.zeros((tm, LANES), F32)
    for idx in idxs:
        hot = hot + jnp.where(lane == idx, 1.0, 0.0)
    before = jnp.dot(tril_ref[...], hot.astype(BF16), preferred_element_type=F32)
    cnt = jnp.sum(hot, axis=0, keepdims=True)
    n_run = jnp.floor((cnt + (RUN_ALIGN - 1)) * (1.0 / RUN_ALIGN))
    n_stage = jnp.floor((n_run * RUN_ALIGN + (STAGE_ALIGN - 1)) * (1.0 / STAGE_ALIGN))
    off = jnp.dot(jnp.broadcast_to(n_stage, (SUBLANES, LANES)).astype(BF16), triu_ref[...],
                  preferred_element_type=F32)[0:1, :] * STAGE_ALIGN
    where_in_stage = off + before
    route = jnp.zeros((tm, LANES), F32)
    for k in range(TOP_K_EXPERTS):
        spos = jnp.sum(jnp.where(lane == idxs[k], where_in_stage, 0.0), axis=-1, keepdims=True)
        route = jnp.where(lane == k, idxs[k].astype(F32), route)
        route = jnp.where(lane == TOP_K_EXPERTS + k, es[k] / den, route)
        route = jnp.where(lane == 2 * TOP_K_EXPERTS + k, spos, route)
    route_o[...] = route
    sub = lax.broadcasted_iota(I32, (SUBLANES, LANES), 0)
    tbl_o[...] = jnp.where(sub == 0, n_run, jnp.where(sub == 1, off, jnp.where(sub == 2, carry[0:1, :], 0.0)))
    carry[0:1, :] = carry[0:1, :] + n_run * RUN_ALIGN
    cnt_o[...] = carry[...]


def _mix_out(o_rwkv, o_dsa, x, gt1, sc2, sh2, w_out, ln_g, ln_b, w_router, b_router):
    B, S, D = x.shape
    tm = min(TM_PROJ, S)
    alpha = 2.0 ** 0.25
    wtop = w_out[:D_RWKV].astype(BF16)
    wbot = w_out[D_RWKV:].astype(BF16)
    wr_p = jnp.pad(w_router, ((0, 0), (0, LANES - N_EXPERTS)))
    br_p = jnp.pad(b_router.reshape(1, -1), ((0, 0), (0, LANES - N_EXPERTS)), constant_values=NEG_BIG)
    tril = jnp.asarray(np.tril(np.ones((tm, tm)), -1), BF16)
    triu = jnp.asarray(np.triu(np.ones((LANES, LANES)), 1), BF16)
    row = lambda v: v.reshape(1, -1)
    tok = lambda n: pl.BlockSpec((None, tm, n), lambda b, i: (b, i, 0))
    mod = pl.BlockSpec((None, 1, D), lambda b, i: (b, 0, 0))
    full = lambda a: pl.BlockSpec(a.shape, lambda b, i: (0,) * a.ndim)
    consts = [wtop, wbot, row(ln_g), row(ln_b), wr_p, br_p, tril, triu]
    return pl.pallas_call(
        functools.partial(_mix_kernel, alpha=alpha),
        grid=(B, S // tm),
        in_specs=[tok(D_RWKV), tok(D_ATT), tok(D), mod, mod, mod] + [full(a) for a in consts],
        out_specs=[tok(D), tok(D), tok(LANES), pl.BlockSpec((None, None, SUBLANES, LANES), lambda b, i: (b, i, 0, 0)),
                   pl.BlockSpec((SUBLANES, LANES), lambda b, i: (0, 0))],
        out_shape=[jax.ShapeDtypeStruct((B, S, D), F32), jax.ShapeDtypeStruct((B, S, D), F32),
                   jax.ShapeDtypeStruct((B, S, LANES), F32),
                   jax.ShapeDtypeStruct((B, S // tm, SUBLANES, LANES), F32),
                   jax.ShapeDtypeStruct((SUBLANES, LANES), F32)],
        scratch_shapes=[pltpu.VMEM((8, LANES), F32)],
        compiler_params=_cparams(("arbitrary", "arbitrary")),
        name="mix_out",
    )(o_rwkv, o_dsa, x, gt1, sc2, sh2, *consts)


RUN_ALIGN = SUBLANES
STAGE_ALIGN = 8
STAGE_ROWS = 1280


def _run_copies(nrun_ref, tile, copy_of):
    for e in range(N_EXPERTS):
        pieces = (nrun_ref[tile * N_EXPERTS + e] * RUN_ALIGN + STAGE_ALIGN - 1) // STAGE_ALIGN

        def piece(j, carry, e=e):
            copy_of(e, j).start(priority=e % 2)
            return carry

        lax.fori_loop(0, pieces, piece, 0)


def _dispatch_kernel(nrun_ref, off_ref, start_ref, npiece_ref, zs_ref, zn_ref, tail_ref,
                     u_ref, route_ref, xs_out, stag, zeros, sem, zsem):
    i = pl.program_id(0)
    n = pl.num_programs(0)
    tm = u_ref.shape[0]
    bm = zeros.shape[0]
    slot = i % 2

    @pl.when(i == 0)
    def _():
        zeros[...] = jnp.zeros_like(zeros)
        fills = []
        for e in range(N_EXPERTS):
            for b in range(bm.bit_length()):
                rows = RUN_ALIGN << b
                if rows > bm:
                    break
                done = (zn_ref[e] >> (b + 1)) << (b + 1)
                dst = pl.multiple_of(zs_ref[e] + done * RUN_ALIGN, RUN_ALIGN)
                fills.append(((zn_ref[e] >> b) & 1 == 1,
                              pltpu.make_async_copy(zeros.at[pl.ds(0, rows)], xs_out.at[pl.ds(dst, rows)], zsem)))
        for pred, cp in fills:
            pl.when(pred)(cp.start)

        def tail_copy(j):
            return pltpu.make_async_copy(zeros, xs_out.at[pl.ds(pl.multiple_of(j * bm, bm), bm)], zsem)

        lax.fori_loop(tail_ref[0], tail_ref[1], lambda j, c: (tail_copy(j).start(), c)[1], 0)
        for pred, cp in fills:
            pl.when(pred)(cp.wait)
        lax.fori_loop(tail_ref[0], tail_ref[1], lambda j, c: (tail_copy(j).wait(), c)[1], 0)

    route_t = jnp.transpose(route_ref[...])
    spos = [route_t[2 * TOP_K_EXPERTS + k:2 * TOP_K_EXPERTS + k + 1, :].astype(I32) for k in range(TOP_K_EXPERTS)]
    rows = lax.broadcasted_iota(I32, (STAGE_ROWS, tm), 0)
    sel = (rows == spos[0]) | (rows == spos[1]) | (rows == spos[2]) | (rows == spos[3])
    stag[slot] = jnp.dot(jnp.where(sel, 1.0, 0.0).astype(BF16), u_ref[...].astype(BF16), preferred_element_type=F32)

    def piece_copy(s, tile):
        def copy_of(e, j):
            src = pl.multiple_of(off_ref[tile * N_EXPERTS + e] + j * STAGE_ALIGN, STAGE_ALIGN)
            dst = pl.multiple_of(start_ref[tile * N_EXPERTS + e] + j * STAGE_ALIGN, RUN_ALIGN)
            return pltpu.make_async_copy(stag.at[s, pl.ds(src, STAGE_ALIGN)], xs_out.at[pl.ds(dst, STAGE_ALIGN)], sem)
        return copy_of

    def drain(tile):
        def w(j, carry):
            pltpu.make_async_copy(stag.at[0, pl.ds(0, STAGE_ALIGN)], xs_out.at[pl.ds(0, STAGE_ALIGN)], sem).wait()
            return carry
        lax.fori_loop(0, npiece_ref[tile], w, 0)

    @pl.when(i > 0)
    def _():
        drain(i - 1)

    _run_copies(nrun_ref, i, piece_copy(slot, i))

    @pl.when(i == n - 1)
    def _():
        drain(i)


def _moe_dispatch(u2, route, tables, n_rows):
    T, D = u2.shape
    tm = TM_ROUTE
    bm = BM_EXPERT
    return pl.pallas_call(
        _dispatch_kernel,
        grid_spec=pltpu.PrefetchScalarGridSpec(
            num_scalar_prefetch=len(tables),
            grid=(T // tm,),
            in_specs=[pl.BlockSpec((tm, D), lambda i, *_: (i, 0)),
                      pl.BlockSpec((tm, LANES), lambda i, *_: (i, 0))],
            out_specs=pl.BlockSpec(memory_space=pl.ANY),
            scratch_shapes=[pltpu.VMEM((2, STAGE_ROWS, D), F32), pltpu.VMEM((bm, D), F32),
                            pltpu.SemaphoreType.DMA(()), pltpu.SemaphoreType.DMA(())],
        ),
        out_shape=jax.ShapeDtypeStruct((n_rows, D), F32),
        compiler_params=_cparams(("arbitrary",)),
        name="moe_dispatch",
    )(*tables, u2, route)


GU_GROUP = 2 * LANES


def _deinterleave_perm():
    p = np.zeros((GU_GROUP, GU_GROUP), np.float32)
    l = np.arange(LANES)
    p[2 * l, l] = 1.0
    p[2 * l + 1, LANES + l] = 1.0
    return jnp.asarray(p, BF16)


def _expert_kernel(be_ref, nb_ref, valid_ref, xs_ref, wgu_hbm, bgu_ref, wd_hbm, bd_ref, perm_ref, ys_ref,
                   wg_buf, wd_buf, wp, wdb, sem):
    i = pl.program_id(0)
    bm = xs_ref.shape[0]
    e = be_ref[i]
    used = i < nb_ref[0]
    new_expert = (i == 0) | (e != be_ref[jnp.maximum(i - 1, 0)])
    n_groups = wp.shape[1] // GU_GROUP
    n_experts = wgu_hbm.shape[0]

    def fetch(ex):
        return (pltpu.make_async_copy(wgu_hbm.at[ex], wg_buf, sem.at[0]),
                pltpu.make_async_copy(wd_hbm.at[ex], wd_buf, sem.at[1]))

    @pl.when(used & new_expert)
    def _():
        @pl.when(i == 0)
        def _():
            for cp in fetch(e):
                cp.start()

        for cp in fetch(e):
            cp.wait()
        for j in range(n_groups):
            sl = slice(j * GU_GROUP, (j + 1) * GU_GROUP)
            wp[:, sl] = jnp.dot(wg_buf[:, sl].astype(BF16), perm_ref[...], preferred_element_type=F32).astype(BF16)
        wdb[...] = wd_buf[...].astype(BF16)

        @pl.when(e + 1 < n_experts)
        def _():
            for cp in fetch(e + 1):
                cp.start()

    def compute(m):
        xb = xs_ref[:m, :].astype(BF16)
        gu = jnp.dot(xb, wp[...], preferred_element_type=F32) + bgu_ref[...]
        hs = []
        for j in range(n_groups):
            gate = jnp.minimum(gu[:, j * GU_GROUP:j * GU_GROUP + LANES], SWIGLU_LIMIT)
            up = jnp.clip(gu[:, j * GU_GROUP + LANES:(j + 1) * GU_GROUP], -SWIGLU_LIMIT, SWIGLU_LIMIT)
            hs.append(((up + 1.0) * (gate * _sigmoid(gate * SWIGLU_ALPHA))).astype(BF16))
        h = jnp.concatenate(hs, axis=-1)
        ys_ref[:m, :] = jnp.dot(h, wdb[...], preferred_element_type=F32) + bd_ref[...]

    sizes = (bm, bm // 2, bm // 4)
    for m, smaller in zip(sizes, sizes[1:] + (0,)):
        @pl.when(used & (valid_ref[i] <= m) & (valid_ref[i] > smaller))
        def _(m=m):
            compute(m)
            if m < bm:
                ys_ref[m:, :] = jnp.zeros((bm - m, ys_ref.shape[1]), F32)

    @pl.when(used & (valid_ref[i] <= 0))
    def _():
        ys_ref[...] = jnp.zeros_like(ys_ref)

    @pl.when(jnp.logical_not(used))
    def _():
        ys_ref[...] = jnp.zeros_like(ys_ref)


def _moe_experts(xs, block_e, n_used, valid, w_gu, b_gu_p, w_dn, b_dn):
    n_rows, D = xs.shape
    E, _, F2 = w_gu.shape
    bm = BM_EXPERT
    n_blocks = n_rows // bm
    perm = _deinterleave_perm()
    wspec = lambda shp: pl.BlockSpec((None,) + shp, lambda i, be, nb, va: (be[i], 0, 0))
    hbm = pl.BlockSpec(memory_space=pl.ANY)
    return pl.pallas_call(
        _expert_kernel,
        grid_spec=pltpu.PrefetchScalarGridSpec(
            num_scalar_prefetch=3,
            grid=(n_blocks,),
            in_specs=[pl.BlockSpec((bm, D), lambda i, be, nb, va: (jnp.minimum(i, nb[0] - 1), 0)),
                      hbm, wspec((1, F2)), hbm, wspec((1, D)),
                      pl.BlockSpec(perm.shape, lambda i, be, nb, va: (0, 0))],
            out_specs=pl.BlockSpec((bm, D), lambda i, be, nb, va: (i, 0)),
            scratch_shapes=[pltpu.VMEM((D, F2), F32), pltpu.VMEM((F2 // 2, D), F32),
                            pltpu.VMEM((D, F2), BF16), pltpu.VMEM((F2 // 2, D), BF16),
                            pltpu.SemaphoreType.DMA((2,))],
        ),
        out_shape=jax.ShapeDtypeStruct((n_rows, D), F32),
        compiler_params=pltpu.CompilerParams(dimension_semantics=("arbitrary",), vmem_limit_bytes=VMEM_LIMIT_EXPERTS),
        name="moe_experts",
    )(block_e, n_used, valid, xs, w_gu, b_gu_p, w_dn, b_dn, perm)


def _combine_kernel(nrun_ref, off_ref, start_ref, npiece_ref, ys_ref, x1_ref, route_ref, gt_ref, g_ref,
                    b_ref, o_ref, stag, sem, *, alpha):
    i = pl.program_id(0)
    n = pl.num_programs(0)
    tm = x1_ref.shape[0]
    slot = i % 2

    def gather(tile, s):
        def copy_of(e, j):
            src = pl.multiple_of(start_ref[tile * N_EXPERTS + e] + j * STAGE_ALIGN, RUN_ALIGN)
            dst = pl.multiple_of(off_ref[tile * N_EXPERTS + e] + j * STAGE_ALIGN, STAGE_ALIGN)
            return pltpu.make_async_copy(ys_ref.at[pl.ds(src, STAGE_ALIGN)], stag.at[s, pl.ds(dst, STAGE_ALIGN)],
                                         sem.at[s])
        _run_copies(nrun_ref, tile, copy_of)

    @pl.when(i == 0)
    def _():
        stag[...] = jnp.zeros_like(stag)
        gather(0, 0)

    @pl.when(i + 1 < n)
    def _():
        gather(i + 1, 1 - slot)

    def w(j, carry):
        pltpu.make_async_copy(ys_ref.at[pl.ds(0, STAGE_ALIGN)], stag.at[slot, pl.ds(0, STAGE_ALIGN)],
                              sem.at[slot]).wait()
        return carry
    lax.fori_loop(0, npiece_ref[i], w, 0)

    route = route_ref[...]
    cols = lax.broadcasted_iota(I32, (tm, STAGE_ROWS), 1)
    wgt = jnp.zeros((tm, STAGE_ROWS), F32)
    for k in range(TOP_K_EXPERTS):
        spos = route[:, 2 * TOP_K_EXPERTS + k:2 * TOP_K_EXPERTS + k + 1].astype(I32)
        wgt = wgt + jnp.where(cols == spos, route[:, TOP_K_EXPERTS + k:TOP_K_EXPERTS + k + 1], 0.0)
    ffn = jnp.dot(wgt.astype(BF16), stag[slot].astype(BF16), preferred_element_type=F32)
    o_ref[...] = _layernorm_rows(alpha * x1_ref[...] + (1.0 + gt_ref[...]) * ffn, g_ref[...], b_ref[...])


def _moe_combine(ys, tables, x1, route, gt2, ln_g, ln_b, tiles_per_batch):
    T, D = x1.shape
    tm = TM_ROUTE
    row = lambda v: v.reshape(1, -1)
    return pl.pallas_call(
        functools.partial(_combine_kernel, alpha=2.0 ** 0.25),
        grid_spec=pltpu.PrefetchScalarGridSpec(
            num_scalar_prefetch=len(tables),
            grid=(T // tm,),
            in_specs=[pl.BlockSpec(memory_space=pl.ANY),
                      pl.BlockSpec((tm, D), lambda i, *_: (i, 0)),
                      pl.BlockSpec((tm, LANES), lambda i, *_: (i, 0)),
                      pl.BlockSpec((None, 1, D), lambda i, *_: (i // tiles_per_batch, 0, 0)),
                      pl.BlockSpec((1, D), lambda i, *_: (0, 0)),
                      pl.BlockSpec((1, D), lambda i, *_: (0, 0))],
            out_specs=pl.BlockSpec((tm, D), lambda i, *_: (i, 0)),
            scratch_shapes=[pltpu.VMEM((2, STAGE_ROWS, D), F32), pltpu.SemaphoreType.DMA((2,))],
        ),
        out_shape=jax.ShapeDtypeStruct((T, D), F32),
        compiler_params=_cparams(("arbitrary",)),
        name="moe_combine",
    )(*tables, ys, x1, route, gt2, row(ln_g), row(ln_b))


def _moe_and_norm(x1, u2, route, tbl, totals, gt2, w_gu, b_gu, w_dn, b_dn, ln_g, ln_b):
    B, S, D = x1.shape
    T = B * S
    bm = BM_EXPERT
    tm = TM_ROUTE
    assert T % tm == 0 and TM_PROJ == tm and STAGE_ROWS >= tm * TOP_K_EXPERTS + N_EXPERTS * (STAGE_ALIGN - 1)
    n_tiles = T // tm
    max_rows = T * TOP_K_EXPERTS + n_tiles * N_EXPERTS * (RUN_ALIGN - 1) + N_EXPERTS * STAGE_ALIGN
    n_blocks = -(-max_rows // bm) + N_EXPERTS
    tot = totals[0, :N_EXPERTS].astype(I32)
    padded = (tot + STAGE_ALIGN + bm - 1) // bm * bm
    pad_ends = jnp.cumsum(padded)
    pad_starts = pad_ends - padded
    t3 = tbl.reshape(n_tiles, SUBLANES, LANES)[:, :, :N_EXPERTS].astype(I32)
    nrun, off, base = t3[:, 0, :], t3[:, 1, :], t3[:, 2, :]
    start = pad_starts[None, :] + base
    pieces = (nrun * RUN_ALIGN + STAGE_ALIGN - 1) // STAGE_ALIGN
    npiece = jnp.sum(pieces, axis=1)
    flat = lambda z: z.reshape(-1).astype(I32)
    run_tables = [flat(nrun), flat(off), flat(start), flat(npiece)]
    zs = pad_starts + tot
    zn = (pad_ends - zs) // RUN_ALIGN
    n_used = (pad_ends[-1:] // bm).astype(I32)
    tail = jnp.concatenate([n_used, jnp.full((1,), n_blocks, I32)])
    blk_row = jnp.arange(n_blocks, dtype=I32) * bm
    block_e = jnp.minimum(jnp.sum((blk_row[:, None] >= pad_ends[None, :]).astype(I32), axis=1), N_EXPERTS - 1)
    E, F2 = b_gu.shape
    b_gu_p = b_gu.reshape(E, F2 // GU_GROUP, LANES, 2).transpose(0, 1, 3, 2).reshape(E, 1, F2)
    route2 = route.reshape(T, LANES)
    xs = _moe_dispatch(u2.reshape(T, D), route2, run_tables + [flat(zs), flat(zn), tail], n_blocks * bm)
    valid = jnp.clip(zs[block_e] - blk_row, 0, bm)
    ys = _moe_experts(xs, block_e, n_used, valid, w_gu, b_gu_p, w_dn, b_dn[:, None, :])
    out = _moe_combine(ys, run_tables, x1.reshape(T, D), route2, gt2, ln_g, ln_b, S // tm)
    return out.reshape(B, S, D)


def kernel(x, c, w_ada, b_ada, w_in, shift_mu, rwkv_w0, rwkv_w2, rwkv_a0, rwkv_a2, rwkv_g2, rwkv_k_k, rwkv_k_a, rwkv_r_k, rwkv_ln_w, rwkv_ln_b, mla_q_norm, mla_w_q_up, mla_kv_norm, mla_w_uk, mla_w_uv, idx_w_q, idx_ln_g, idx_ln_b, w_out, ln1_g, ln1_b, w_router, b_router, w_gu, b_gu, w_dn, b_dn, ln2_g, ln2_b):
    depth = w_ada.shape[0]
    assert depth == 1, "DeepNorm constants below are for a single layer"
    l = 0
    mod = _ada_mod(c, w_ada[l], b_ada[l])
    sh1, sc1, gt1, sh2, sc2, gt2 = [m[:, None, :] for m in jnp.split(mod, 6, axis=-1)]
    r, lw, k, v, kk, a, g, qabs, iq, ik, iw, cl, clt = _in_proj(
        x, sc1, sh1, w_in[l], shift_mu[l], rwkv_w0[l], rwkv_w2[l], rwkv_a0[l], rwkv_a2[l], rwkv_g2[l], rwkv_k_k[l],
        rwkv_k_a[l], mla_q_norm[l], mla_w_q_up[l], mla_w_uk[l], mla_kv_norm[l], idx_w_q[l], idx_ln_g[l], idx_ln_b[l])
    o_rwkv = _rwkv_scan(r, lw, k, v, kk, a, g, rwkv_r_k[l], rwkv_ln_w[l], rwkv_ln_b[l])
    o_dsa = _dsa_attn(iq, iw, qabs, ik, cl, clt, mla_w_uv[l])
    x1, u2, route, tbl, totals = _mix_out(o_rwkv, o_dsa, x, gt1, sc2, sh2, w_out[l], ln1_g[l], ln1_b[l],
                                          w_router[l], b_router[l])
    return _moe_and_norm(x1, u2, route, tbl, totals, gt2, w_gu[l], b_gu[l], w_dn[l], b_dn[l], ln2_g[l], ln2_b[l])
```

```python
import functools
import math

import jax
import jax.numpy as jnp
import numpy as np
from jax import lax
from jax.experimental import pallas as pl
from jax.experimental.pallas import tpu as pltpu

F32 = jnp.float32
BF16 = jnp.bfloat16
I32 = jnp.int32

RWKV_HEAD = 64
N_RWKV_HEADS = 8
D_RWKV = RWKV_HEAD * N_RWKV_HEADS
RWKV_GN_EPS = 64e-5
ATT_HEAD = 64
N_ATT_HEADS = 8
D_ATT = ATT_HEAD * N_ATT_HEADS
KV_LORA = 128
IDX_HEADS = 8
IDX_DIM = 64
TOPK_MAX = 256
N_EXPERTS = 32
TOP_K_EXPERTS = 4
SWIGLU_LIMIT = 7.0
SWIGLU_ALPHA = 1.702
NEG_BIG = -1e30
LOG2E = 1.4426950408889634
INT_MIN = -(2 ** 31)

LANES = 128
SUBLANES = 8
VMEM_LIMIT = 48 * 1024 * 1024
VMEM_LIMIT_EXPERTS = 58 * 1024 * 1024

TM_IN = 512
TM_PROJ = 256
L_CHUNK = 64
CHUNKS_PER_STEP = 4
TQ = 256
KEY_CHUNK = 256
SUM_ROWS = 128
TM_ROUTE = 256
BM_EXPERT = 1024

_SEG = (("r", 512, 512), ("k", 512, 512), ("v", 512, 512), ("w", 64, 128), ("a", 64, 128), ("g", 128, 128),
        ("q", 256, 256), ("kv", 128, 128), ("ik", 64, 128), ("iw", 8, 128))
N_SHIFT_P = 512 * 3 + 128 * 3
N_IN_P = sum(s[2] for s in _SEG)


def _cparams(sem):
    return pltpu.CompilerParams(dimension_semantics=sem, vmem_limit_bytes=VMEM_LIMIT)


def _bdot(a, b):
    return jnp.dot(a.astype(BF16), b.astype(BF16), preferred_element_type=F32)


def _split2(a):
    hi = a.astype(BF16)
    lo = (a - hi.astype(F32)).astype(BF16)
    return hi, lo


def _split3(a):
    hi = a.astype(BF16)
    r1 = a - hi.astype(F32)
    mid = r1.astype(BF16)
    lo = (r1 - mid.astype(F32)).astype(BF16)
    return hi, mid, lo


def _dot3(a, b, dims=(((1,), (0,)), ((), ()))):
    ah, al = _split2(a)
    bh, bl = _split2(b)
    d = functools.partial(lax.dot_general, dimension_numbers=dims, preferred_element_type=F32)
    return d(ah, bh) + (d(ah, bl) + d(al, bh))


def _dot_exact_rhs(a, b_exact, nsplit=3):
    parts = _split3(a) if nsplit == 3 else _split2(a)
    acc = None
    for p in parts[::-1]:
        t = jnp.dot(p, b_exact, preferred_element_type=F32)
        acc = t if acc is None else acc + t
    return acc


def _dot_exact_lhs(a_exact, b, nsplit=3):
    parts = _split3(b) if nsplit == 3 else _split2(b)
    acc = None
    for p in parts[::-1]:
        t = jnp.dot(a_exact, p, preferred_element_type=F32)
        acc = t if acc is None else acc + t
    return acc


def _sigmoid(x):
    return 1.0 / (1.0 + jnp.exp(-x))


def _softplus(x):
    return jnp.maximum(x, 0.0) + jnp.log(1.0 + jnp.exp(-jnp.abs(x)))


def _ada_kernel(c_ref, w_ref, b_ref, o_ref):
    c = c_ref[...]
    o_ref[...] = _dot3(c * _sigmoid(c), w_ref[...]) + b_ref[...]


def _ada_mod(c, w_ada, b_ada):
    B, D = c.shape
    N = w_ada.shape[1]
    tn = 1024
    return pl.pallas_call(
        _ada_kernel,
        grid=(N // tn,),
        in_specs=[pl.BlockSpec((B, D), lambda j: (0, 0)),
                  pl.BlockSpec((D, tn), lambda j: (0, j)),
                  pl.BlockSpec((1, tn), lambda j: (0, j))],
        out_specs=pl.BlockSpec((B, tn), lambda j: (0, j)),
        out_shape=jax.ShapeDtypeStruct((B, N), F32),
        compiler_params=_cparams(("arbitrary",)),
        name="ada_mod",
    )(c, w_ada, b_ada.reshape(1, N))


def _in_proj_kernel(x_ref, sc_ref, sh_ref, win_ref, mu_ref, w0_ref, w2_ref, a0_ref, a2_ref, g2_ref, kk_ref, ka_ref,
                    ones_ref, qn_ref, wq_ref, wuk_ref, kvn_ref, wiq_ref, ig_ref, ib_ref,
                    r_o, lw_o, k_o, v_o, kkn_o, a_o, g_o, qabs_o, iq_o, ik_o, iw_o, cl_o, clt_o, carry):
    i = pl.program_id(1)
    tm = x_ref.shape[0]

    @pl.when(i == 0)
    def _():
        carry[...] = jnp.zeros_like(carry)

    u = x_ref[...] * (1.0 + sc_ref[...]) + sh_ref[...]
    p = _bdot(u, win_ref[...])
    ps = p[:, :N_SHIFT_P]
    rows = lax.broadcasted_iota(I32, (tm, 1), 0)
    prev = jnp.where(rows == 0, carry[0:1, :], pltpu.roll(ps, 1, 0))
    carry[0:1, :] = ps[tm - 1:tm, :]
    ps = ps + mu_ref[...] * (prev - ps)

    pr, pk, pv = ps[:, 0:512], ps[:, 512:1024], ps[:, 1024:1536]
    pw, pa, pg = ps[:, 1536:1664], ps[:, 1664:1792], ps[:, 1792:1920]
    w_log = -_softplus(-(w0_ref[...] + _dot3(jnp.tanh(pw), w2_ref[...]))) - 0.5
    lw_o[...] = -jnp.exp(w_log)
    a = _sigmoid(a0_ref[...] + _dot3(pa, a2_ref[...]))
    g_o[...] = _dot3(_sigmoid(pg), g2_ref[...])
    kk = pk * kk_ref[...]
    ssq = _dot_exact_rhs(kk * kk, ones_ref[...], nsplit=2)
    kkn_o[...] = kk / jnp.maximum(jnp.sqrt(ssq), 1e-12)
    k_o[...] = pk * (1.0 + (a - 1.0) * ka_ref[...])
    r_o[...] = pr
    v_o[...] = pv
    a_o[...] = a

    pq, pkv = p[:, 1920:2176], p[:, 2176:2304]
    pik, piw = p[:, 2304:2432], p[:, 2432:2560]
    q_lat = pq * lax.rsqrt(jnp.mean(pq * pq, axis=-1, keepdims=True) + 1e-6) * qn_ref[...]
    q = _bdot(q_lat, wq_ref[...])
    qabs_o[...] = (_bdot(q, wuk_ref[...]) * (ATT_HEAD ** -0.5 * LOG2E)).astype(BF16)
    c_lat = pkv * lax.rsqrt(jnp.mean(pkv * pkv, axis=-1, keepdims=True) + 1e-6) * kvn_ref[...]
    spos = i * tm + rows
    s_hi = (spos >> 7).astype(F32)
    s_lo = (spos & (LANES - 1)).astype(F32)
    lane_t = lax.broadcasted_iota(I32, (1, LANES), 1)
    extra = jnp.where(lane_t == 0, 1.0, jnp.where((lane_t == 1) | (lane_t == 2), s_hi,
                                                  jnp.where((lane_t == 3) | (lane_t == 4), s_lo, 0.0)))
    c_aug = jnp.concatenate([c_lat, extra], axis=-1)
    cl_o[...] = c_aug.astype(BF16)
    clt_o[...] = jnp.transpose(c_aug).astype(BF16)
    iq_o[...] = (_bdot(q_lat, wiq_ref[...]) * (IDX_DIM ** -0.5)).astype(BF16)
    lane = lax.broadcasted_iota(I32, (1, LANES), 1)
    valid = lane < IDX_DIM
    mu = jnp.sum(pik, axis=-1, keepdims=True) * (1.0 / IDX_DIM)
    dlt = jnp.where(valid, pik - mu, 0.0)
    var = jnp.sum(dlt * dlt, axis=-1, keepdims=True) * (1.0 / IDX_DIM)
    ik_o[...] = jnp.where(valid, dlt * lax.rsqrt(var + 1e-5) * ig_ref[...] + ib_ref[...], 0.0).astype(BF16)
    iw_o[...] = piw * (IDX_HEADS ** -0.5)


def _pad_cols(w, widths):
    parts, o = [], 0
    for true, padded in widths:
        seg = w[..., o:o + true]
        if padded > true:
            seg = jnp.pad(seg, [(0, 0)] * (w.ndim - 1) + [(0, padded - true)])
        parts.append(seg)
        o += true
    return jnp.concatenate(parts, axis=-1)


def _pad_rows(w, rows):
    return jnp.pad(w, ((0, rows - w.shape[0]), (0, 0)))


def _block_diag(blocks):
    H, a, b = blocks.shape
    eye = jnp.eye(H, dtype=blocks.dtype)
    return (eye[:, None, :, None] * blocks[:, :, None, :]).reshape(H * a, H * b)


def _head_ones(n, head):
    idx = np.arange(n) // head
    return jnp.asarray(idx[:, None] == idx[None, :], BF16)


def _in_proj(x, sc1, sh1, w_in, shift_mu, w0, w2, a0, a2, g2, k_k, k_a, q_norm, w_q_up, w_uk, kv_norm, idx_w_q,
             idx_ln_g, idx_ln_b):
    B, S, D = x.shape
    tm = min(TM_IN, S)
    widths = tuple((s[1], s[2]) for s in _SEG)
    win_p = _pad_cols(w_in, widths).astype(BF16)
    mu_p = _pad_cols(shift_mu.reshape(1, -1), widths[:6])
    w2_p = _pad_rows(w2, LANES)
    a2_p = _pad_rows(a2, LANES)
    wuk_bd = _block_diag(w_uk).astype(BF16)
    wiq_p = _pad_cols(idx_w_q, ((IDX_DIM, LANES),) * IDX_HEADS).astype(BF16)
    ig_p = _pad_cols(idx_ln_g.reshape(1, -1), ((IDX_DIM, LANES),))
    ib_p = _pad_cols(idx_ln_b.reshape(1, -1), ((IDX_DIM, LANES),))
    row = lambda v: v.reshape(1, -1)
    tok = lambda n: pl.BlockSpec((None, tm, n), lambda b, i: (b, i, 0))
    mod = pl.BlockSpec((None, 1, D), lambda b, i: (b, 0, 0))
    full = lambda a: pl.BlockSpec(a.shape, lambda b, i: (0,) * a.ndim)
    consts = [win_p, mu_p, row(w0), w2_p, row(a0), a2_p, g2, row(k_k), row(k_a), _head_ones(D_RWKV, RWKV_HEAD),
              row(q_norm), w_q_up.astype(BF16), wuk_bd, row(kv_norm), wiq_p, ig_p, ib_p]
    outs = [(D_RWKV, F32)] * 7 + [(N_ATT_HEADS * KV_LORA, BF16), (IDX_HEADS * LANES, BF16), (LANES, BF16),
                                  (LANES, F32), (KV_LORA + LANES, BF16)]
    return pl.pallas_call(
        _in_proj_kernel,
        grid=(B, S // tm),
        in_specs=[tok(D), mod, mod] + [full(a) for a in consts],
        out_specs=[tok(n) for n, _ in outs] + [pl.BlockSpec((None, KV_LORA + LANES, tm), lambda b, i: (b, 0, i))],
        out_shape=[jax.ShapeDtypeStruct((B, S, n), dt) for n, dt in outs]
        + [jax.ShapeDtypeStruct((B, KV_LORA + LANES, S), BF16)],
        scratch_shapes=[pltpu.VMEM((8, N_SHIFT_P), F32)],
        compiler_params=_cparams(("arbitrary", "arbitrary")),
        name="in_proj",
    )(x, sc1, sh1, *consts)


def _rwkv_kernel(r_ref, lw_ref, k_ref, v_ref, kk_ref, a_ref, g_ref, rk_ref, lnw_ref, lnb_ref, tri_ref, ones_ref,
                 o_ref, state, *, L):
    c = pl.program_id(1)
    nc = r_ref.shape[0] // L

    @pl.when(c == 0)
    def _():
        state[...] = jnp.zeros_like(state)

    r, lw, k, v, kk, a = r_ref[...], lw_ref[...], k_ref[...], v_ref[...], kk_ref[...], a_ref[...]
    cum = _dot_exact_lhs(tri_ref[...], lw)
    rows = [slice(ci * L, (ci + 1) * L) for ci in range(nc)]
    last = [cum[(ci + 1) * L - 1:(ci + 1) * L, :] for ci in range(nc)]
    cum_last = jnp.concatenate([jnp.broadcast_to(z, (L, z.shape[1])) for z in last], axis=0)
    w_incl = jnp.exp(cum)
    w_inv = jnp.exp(-cum)
    w_rel = jnp.exp(cum_last - cum)
    w_last = [jnp.exp(z) for z in last]
    bvec = kk * a
    at = -kk * jnp.exp(cum - lw)
    rt = r * w_incl
    bt = bvec * w_inv
    kt = k * w_inv
    bh = bvec * w_rel
    kh = k * w_rel
    ti = lax.broadcasted_iota(I32, (L, L), 0)
    tj = lax.broadcasted_iota(I32, (L, L), 1)
    strict = tj < ti
    incl = tj <= ti
    eye = lax.broadcasted_iota(I32, (RWKV_HEAD, RWKV_HEAD), 0) == lax.broadcasted_iota(I32, (RWKV_HEAD, RWKV_HEAD), 1)
    NT = (((1,), (1,)), ((), ()))
    TN = (((0,), (0,)), ((), ()))
    heads = range(N_RWKV_HEADS)
    sls = [slice(h * RWKV_HEAD, (h + 1) * RWKV_HEAD) for h in heads]
    units = [(ci, h) for ci in range(nc) for h in heads]
    idx = range(len(units))
    mm = lambda x, y, dims=(((1,), (0,)), ((), ())): lax.dot_general(
        x.astype(BF16), y.astype(BF16), dims, preferred_element_type=F32)
    at_b, rt_b, bt_b, kt_b, bh_b, kh_b, v_b = [z.astype(BF16) for z in (at, rt, bt, kt, bh, kh, v)]
    cut = lambda z, u: z[rows[u[0]], sls[u[1]]]
    vh = [cut(v_b, u) for u in units]
    ath = [cut(at_b, u) for u in units]
    ar = [jnp.concatenate([cut(at_b, u), cut(rt_b, u)], axis=0) for u in units]
    g_b = [mm(ar[i], cut(bt_b, units[i]), NT) for i in idx]
    g_k = [mm(ar[i], cut(kt_b, units[i]), NT) for i in idx]
    n_ab = [jnp.where(strict, g_b[i][:L], 0.0) for i in idx]
    a_ak = [jnp.where(strict, g_k[i][:L], 0.0) for i in idx]
    a_rb = [jnp.where(incl, g_b[i][L:], 0.0).astype(BF16) for i in idx]
    a_rk = [jnp.where(incl, g_k[i][L:], 0.0) for i in idx]
    akv = [mm(a_ak[i], vh[i]) for i in idx]
    eye_l = jnp.where(ti == tj, 1.0, 0.0)
    tinv = [eye_l + n_ab[i] for i in idx]
    pw = n_ab
    for _ in range(int(math.log2(L)) - 1):
        pw = [mm(pw[i], pw[i]) for i in idx]
        tinv = [tinv[i] + mm(pw[i], tinv[i]) for i in idx]
    tinv = [t.astype(BF16) for t in tinv]
    a_t = [mm(tinv[i], ath[i]).astype(BF16) for i in idx]
    y = [mm(tinv[i], akv[i]).astype(BF16) for i in idx]
    m_c = [jnp.where(eye, w_last[units[i][0]][:, sls[units[i][1]]], 0.0) + mm(a_t[i], cut(bh_b, units[i]), TN)
           for i in idx]
    c_c = [mm(y[i], cut(bh_b, units[i]), TN) + mm(vh[i], cut(kh_b, units[i]), TN) for i in idx]
    q_c = [cut(rt, units[i]) + mm(a_rb[i], a_t[i]) for i in idx]
    o_loc = [mm(a_rb[i], y[i]) + mm(a_rk[i], vh[i]) for i in idx]
    s = [state[h] for h in heads]
    for ci in range(nc):
        for h in heads:
            i = ci * N_RWKV_HEADS + h
            o = o_loc[i] + mm(q_c[i], s[h], NT)
            s[h] = mm(s[h], m_c[i]) + c_c[i]
            mu = jnp.mean(o, axis=-1, keepdims=True)
            d = o - mu
            var = jnp.mean(d * d, axis=-1, keepdims=True)
            o_ref[rows[ci], sls[h]] = d * lax.rsqrt(var + RWKV_GN_EPS)
    for h in heads:
        state[h] = s[h]
    bonus = _dot_exact_rhs(r * k * rk_ref[...], ones_ref[...], nsplit=3) * v
    o_ref[...] = (o_ref[...] * lnw_ref[...] + lnb_ref[...] + bonus) * g_ref[...]


def _rwkv_scan(r, lw, k, v, kk, a, g, r_k, ln_w, ln_b):
    B, S, DR = r.shape
    L = min(L_CHUNK, S)
    lb = min(L * CHUNKS_PER_STEP, S)
    tri = jnp.asarray(np.kron(np.eye(lb // L), np.tril(np.ones((L, L)))), BF16)
    row = lambda z: z.reshape(1, -1)
    tok = pl.BlockSpec((None, lb, DR), lambda b, c: (b, c, 0))
    full = lambda z: pl.BlockSpec(z.shape, lambda b, c: (0,) * z.ndim)
    consts = [row(r_k), row(ln_w), row(ln_b), tri, _head_ones(DR, RWKV_HEAD)]
    return pl.pallas_call(
        functools.partial(_rwkv_kernel, L=L),
        grid=(B, S // lb),
        in_specs=[tok] * 7 + [full(z) for z in consts],
        out_specs=tok,
        out_shape=jax.ShapeDtypeStruct((B, S, DR), F32),
        scratch_shapes=[pltpu.VMEM((N_RWKV_HEADS, RWKV_HEAD, RWKV_HEAD), F32)],
        compiler_params=_cparams(("arbitrary", "arbitrary")),
        name="rwkv_scan",
    )(r, lw, k, v, kk, a, g, *consts)


def _alibi_cols():
    slope = np.asarray([2.0 ** (-8.0 * (h + 1) / N_ATT_HEADS) * LOG2E for h in range(N_ATT_HEADS)], np.float32)
    c_hi = slope.astype(BF16).astype(np.float32)
    c_lo = (slope - c_hi).astype(BF16).astype(np.float32)
    t = np.zeros((N_ATT_HEADS, LANES), np.float32)
    t[:, 1], t[:, 2], t[:, 3], t[:, 4] = LANES * c_hi, LANES * c_lo, c_hi, c_lo
    return jnp.asarray(t.astype(BF16))


def _dsa_kernel(iq_ref, iw_ref, qa_ref, ik_ref, ca_ref, cat_ref, wuv_ref, tril_ref, acol_ref, o_ref, key_ref,
                bias_ref, lg_ref, p_ref, *, topk, q_off, select):
    qi = pl.program_id(1) + q_off
    tq = qa_ref.shape[0]
    sk = ca_ref.shape[0]
    kc = min(KEY_CHUNK, sk)
    chunks = [slice(c * kc, (c + 1) * kc) for c in range(sk // kc)]
    tpos = qi * tq + lax.broadcasted_iota(I32, (1, tq), 1)
    srow = lax.broadcasted_iota(I32, (kc, tq), 0)
    NT = (((1,), (1,)), ((), ()))

    if not select:
        for c, cs in enumerate(chunks):
            bias_ref[cs, :] = jnp.where(srow + c * kc <= tpos, 0.0, NEG_BIG)
    else:
        iw_t = jnp.transpose(iw_ref[...])
        for c, cs in enumerate(chunks):
            ikc = ik_ref[cs, :]
            score = jnp.zeros((kc, tq), F32)
            for h in range(IDX_HEADS):
                s = lax.dot_general(ikc, iq_ref[:, h * LANES:(h + 1) * LANES], NT, preferred_element_type=F32)
                score = score + iw_t[h:h + 1, :] * jnp.maximum(s, 0.0)
            bits = pltpu.bitcast(score + 0.0, I32)
            key = bits ^ ((bits >> 31) & 0x7FFFFFFF)
            key_ref[cs, :] = jnp.where(srow + c * kc <= tpos, key, INT_MIN)
        kcount = jnp.minimum(topk, tpos + 1).astype(F32)

        def sum_keys(x):
            part = jnp.sum(x.reshape(x.shape[0] // SUM_ROWS, SUM_ROWS, x.shape[1]), axis=0)
            return jnp.sum(part, axis=0, keepdims=True)

        def count_ge(cand):
            return sum_keys(jnp.where(key_ref[...] >= cand, 1.0, 0.0))

        thr = jnp.where(count_ge(jnp.zeros((1, tq), I32)) >= kcount, 0, INT_MIN).astype(I32)

        def bit_step(i, thr):
            cand = thr | (1 << (30 - i))
            return jnp.where(count_ge(cand) >= kcount, cand, thr)

        thr = lax.fori_loop(0, 31, bit_step, thr)

        need = kcount - sum_keys(jnp.where(key_ref[...] > thr, 1.0, 0.0))
        carry = jnp.zeros((1, tq), F32)
        for j in range(sk // LANES):
            sl = slice(j * LANES, (j + 1) * LANES)
            key = key_ref[sl, :]
            eq = key == thr
            e = jnp.where(eq, 1.0, 0.0)
            before = jnp.dot(tril_ref[...], e.astype(BF16), preferred_element_type=F32) + carry
            sel = (key > thr) | (eq & (before < need))
            bias_ref[sl, :] = jnp.where(sel, 0.0, NEG_BIG)
            carry = carry + jnp.sum(e, axis=0, keepdims=True)

    outs = []
    for h in range(N_ATT_HEADS):
        q_aug = jnp.concatenate([qa_ref[:, h * KV_LORA:(h + 1) * KV_LORA],
                                 jnp.broadcast_to(acol_ref[h:h + 1, :], (tq, LANES))], axis=-1)
        lg_all = lax.dot_general(ca_ref[...], q_aug, NT, preferred_element_type=F32)
        m = jnp.full((1, tq), -jnp.inf, F32)
        for cs in chunks:
            lg = lg_all[cs, :] + bias_ref[cs, :]
            lg_ref[cs, :] = lg
            m = jnp.maximum(m, jnp.max(lg, axis=0, keepdims=True))
        for cs in chunks:
            p_ref[cs, :] = jnp.exp2(lg_ref[cs, :] - m).astype(BF16)
        pv = jnp.dot(cat_ref[...], p_ref[...], preferred_element_type=F32)
        outs.append((pv[:KV_LORA, :] / pv[KV_LORA:KV_LORA + 1, :]).astype(BF16))
    o_lat_t = jnp.concatenate(outs, axis=0)
    o_ref[...] = lax.dot_general(o_lat_t, wuv_ref[...], (((0,), (0,)), ((), ())), preferred_element_type=F32)


def _dsa_attn(iq, iw, qabs, ik, ca, cat, w_uv):
    B, S, _ = iq.shape
    tq = min(TQ, S)
    topk = min(TOPK_MAX, S // 4)
    wuv_bd = _block_diag(w_uv).astype(BF16)
    tril = jnp.asarray(np.tril(np.ones((LANES, LANES)), -1), BF16)
    acol = _alibi_cols()
    full = lambda z: pl.BlockSpec(z.shape, lambda b, i: (0,) * z.ndim)
    nq = 1
    outs = []
    for q_off in range(0, S // tq, nq):
        sk = (q_off + nq) * tq
        tok = lambda n, q_off=q_off: pl.BlockSpec((None, tq, n), lambda b, i: (b, i + q_off, 0))
        seq = lambda n, sk=sk: pl.BlockSpec((None, sk, n), lambda b, i: (b, 0, 0))
        outs.append(pl.pallas_call(
            functools.partial(_dsa_kernel, topk=topk, q_off=q_off, select=sk > topk),
            grid=(B, nq),
            in_specs=[tok(IDX_HEADS * LANES), tok(LANES), tok(N_ATT_HEADS * KV_LORA), seq(LANES), seq(KV_LORA + LANES),
                      pl.BlockSpec((None, KV_LORA + LANES, sk), lambda b, i: (b, 0, 0)),
                      full(wuv_bd), full(tril), full(acol)],
            out_specs=pl.BlockSpec((None, tq, D_ATT), lambda b, i: (b, i, 0)),
            out_shape=jax.ShapeDtypeStruct((B, nq * tq, D_ATT), F32),
            scratch_shapes=[pltpu.VMEM((sk, tq), I32), pltpu.VMEM((sk, tq), F32), pltpu.VMEM((sk, tq), F32),
                            pltpu.VMEM((sk, tq), BF16)],
            compiler_params=_cparams(("arbitrary", "arbitrary")),
            name=f"dsa_attn_k{sk}",
        )(iq, iw, qabs, ik, ca, cat, wuv_bd, tril, acol))
    return jnp.concatenate(outs, axis=1)


def _layernorm_rows(y, g, b):
    mu = jnp.mean(y, axis=-1, keepdims=True)
    d = y - mu
    var = jnp.mean(d * d, axis=-1, keepdims=True)
    return d * lax.rsqrt(var + 1e-5) * g + b


def _mix_kernel(orw_ref, ods_ref, x_ref, gt_ref, sc_ref, sh_ref, wtop_ref, wbot_ref, g_ref, b_ref, wr_ref, br_ref,
                tril_ref, triu_ref, x1_o, u2_o, route_o, tbl_o, cnt_o, carry, *, alpha):
    first = (pl.program_id(0) == 0) & (pl.program_id(1) == 0)
    tm = x_ref.shape[0]

    @pl.when(first)
    def _():
        carry[...] = jnp.zeros_like(carry)

    mix = _bdot(orw_ref[...], wtop_ref[...]) + _bdot(ods_ref[...], wbot_ref[...])
    x1 = _layernorm_rows(alpha * x_ref[...] + (1.0 + gt_ref[...]) * mix, g_ref[...], b_ref[...])
    x1_o[...] = x1
    u2 = x1 * (1.0 + sc_ref[...]) + sh_ref[...]
    u2_o[...] = u2

    lg = _dot3(u2, wr_ref[...]) + br_ref[...]
    lane = lax.broadcasted_iota(I32, (tm, LANES), 1)
    lane_f = lane.astype(F32)
    idxs, vals = [], []
    for _ in range(TOP_K_EXPERTS):
        m = jnp.max(lg, axis=-1, keepdims=True)
        idx = jnp.min(jnp.where(lg == m, lane_f, float(LANES)), axis=-1, keepdims=True).astype(I32)
        idxs.append(idx)
        vals.append(m)
        lg = jnp.where(lane == idx, -jnp.inf, lg)
    es = [jnp.exp(v - vals[0]) for v in vals]
    den = es[0] + es[1] + es[2] + es[3]
    hot = jnp.zeros((tm, LANES), F32)
    for idx in idxs:
        hot = hot + jnp.where(lane == idx, 1.0, 0.0)
    before = jnp.dot(tril_ref[...], hot.astype(BF16), preferred_element_type=F32)
    cnt = jnp.sum(hot, axis=0, keepdims=True)
    n_run = jnp.floor((cnt + (RUN_ALIGN - 1)) * (1.0 / RUN_ALIGN))
    n_stage = jnp.floor((n_run * RUN_ALIGN + (STAGE_ALIGN - 1)) * (1.0 / STAGE_ALIGN))
    off = jnp.dot(jnp.broadcast_to(n_stage, (SUBLANES, LANES)).astype(BF16), triu_ref[...],
                  preferred_element_type=F32)[0:1, :] * STAGE_ALIGN
    where_in_stage = off + before
    route = jnp.zeros((tm, LANES), F32)
    for k in range(TOP_K_EXPERTS):
        spos = jnp.sum(jnp.where(lane == idxs[k], where_in_stage, 0.0), axis=-1, keepdims=True)
        route = jnp.where(lane == k, idxs[k].astype(F32), route)
        route = jnp.where(lane == TOP_K_EXPERTS + k, es[k] / den, route)
        route = jnp.where(lane == 2 * TOP_K_EXPERTS + k, spos, route)
    route_o[...] = route
    sub = lax.broadcasted_iota(I32, (SUBLANES, LANES), 0)
    tbl_o[...] = jnp.where(sub == 0, n_run, jnp.where(sub == 1, off, jnp.where(sub == 2, carry[0:1, :], 0.0)))
    carry[0:1, :] = carry[0:1, :] + n_run * RUN_ALIGN
    cnt_o[...] = carry[...]


def _mix_out(o_rwkv, o_dsa, x, gt1, sc2, sh2, w_out, ln_g, ln_b, w_router, b_router):
    B, S, D = x.shape
    tm = min(TM_PROJ, S)
    alpha = 2.0 ** 0.25
    wtop = w_out[:D_RWKV].astype(BF16)
    wbot = w_out[D_RWKV:].astype(BF16)
    wr_p = jnp.pad(w_router, ((0, 0), (0, LANES - N_EXPERTS)))
    br_p = jnp.pad(b_router.reshape(1, -1), ((0, 0), (0, LANES - N_EXPERTS)), constant_values=NEG_BIG)
    tril = jnp.asarray(np.tril(np.ones((tm, tm)), -1), BF16)
    triu = jnp.asarray(np.triu(np.ones((LANES, LANES)), 1), BF16)
    row = lambda v: v.reshape(1, -1)
    tok = lambda n: pl.BlockSpec((None, tm, n), lambda b, i: (b, i, 0))
    mod = pl.BlockSpec((None, 1, D), lambda b, i: (b, 0, 0))
    full = lambda a: pl.BlockSpec(a.shape, lambda b, i: (0,) * a.ndim)
    consts = [wtop, wbot, row(ln_g), row(ln_b), wr_p, br_p, tril, triu]
    return pl.pallas_call(
        functools.partial(_mix_kernel, alpha=alpha),
        grid=(B, S // tm),
        in_specs=[tok(D_RWKV), tok(D_ATT), tok(D), mod, mod, mod] + [full(a) for a in consts],
        out_specs=[tok(D), tok(D), tok(LANES), pl.BlockSpec((None, None, SUBLANES, LANES), lambda b, i: (b, i, 0, 0)),
                   pl.BlockSpec((SUBLANES, LANES), lambda b, i: (0, 0))],
        out_shape=[jax.ShapeDtypeStruct((B, S, D), F32), jax.ShapeDtypeStruct((B, S, D), F32),
                   jax.ShapeDtypeStruct((B, S, LANES), F32),
                   jax.ShapeDtypeStruct((B, S // tm, SUBLANES, LANES), F32),
                   jax.ShapeDtypeStruct((SUBLANES, LANES), F32)],
        scratch_shapes=[pltpu.VMEM((8, LANES), F32)],
        compiler_params=_cparams(("arbitrary", "arbitrary")),
        name="mix_out",
    )(o_rwkv, o_dsa, x, gt1, sc2, sh2, *consts)


RUN_ALIGN = SUBLANES
STAGE_ALIGN = 16
STAGE_ROWS = 1536


def _piece_waits(n_pieces):
    max_pieces = STAGE_ROWS // STAGE_ALIGN
    return [(STAGE_ALIGN << b, (n_pieces >> b) & 1 == 1) for b in range(max_pieces.bit_length())
            if STAGE_ALIGN << b <= STAGE_ROWS]


def _run_copies(nrun_ref, tile, copy_of):
    for e in range(N_EXPERTS):
        pieces = (nrun_ref[tile * N_EXPERTS + e] * RUN_ALIGN + STAGE_ALIGN - 1) // STAGE_ALIGN

        def piece(j, carry, e=e):
            copy_of(e, j).start(priority=e % 2)
            return carry

        lax.fori_loop(0, pieces, piece, 0)


def _dispatch_kernel(nrun_ref, off_ref, start_ref, npiece_ref, zs_ref, zn_ref, tail_ref,
                     u_ref, route_ref, xs_out, stag, zeros, sem, zsem):
    i = pl.program_id(0)
    n = pl.num_programs(0)
    tm = u_ref.shape[0]
    bm = zeros.shape[0]
    slot = i % 2

    @pl.when(i == 0)
    def _():
        zeros[...] = jnp.zeros_like(zeros)
        fills = []
        for e in range(N_EXPERTS):
            for b in range(bm.bit_length()):
                rows = RUN_ALIGN << b
                if rows > bm:
                    break
                done = (zn_ref[e] >> (b + 1)) << (b + 1)
                dst = pl.multiple_of(zs_ref[e] + done * RUN_ALIGN, RUN_ALIGN)
                fills.append(((zn_ref[e] >> b) & 1 == 1,
                              pltpu.make_async_copy(zeros.at[pl.ds(0, rows)], xs_out.at[pl.ds(dst, rows)], zsem)))
        for pred, cp in fills:
            pl.when(pred)(cp.start)

        def tail_copy(j):
            return pltpu.make_async_copy(zeros, xs_out.at[pl.ds(pl.multiple_of(j * bm, bm), bm)], zsem)

        lax.fori_loop(tail_ref[0], tail_ref[1], lambda j, c: (tail_copy(j).start(), c)[1], 0)
        for pred, cp in fills:
            pl.when(pred)(cp.wait)
        lax.fori_loop(tail_ref[0], tail_ref[1], lambda j, c: (tail_copy(j).wait(), c)[1], 0)

    route_t = jnp.transpose(route_ref[...])
    spos = [route_t[2 * TOP_K_EXPERTS + k:2 * TOP_K_EXPERTS + k + 1, :].astype(I32) for k in range(TOP_K_EXPERTS)]
    rows = lax.broadcasted_iota(I32, (STAGE_ROWS, tm), 0)
    sel = (rows == spos[0]) | (rows == spos[1]) | (rows == spos[2]) | (rows == spos[3])
    stag[slot] = jnp.dot(jnp.where(sel, 1.0, 0.0).astype(BF16), u_ref[...].astype(BF16), preferred_element_type=F32)

    def piece_copy(s, tile):
        def copy_of(e, j):
            src = pl.multiple_of(off_ref[tile * N_EXPERTS + e] + j * STAGE_ALIGN, STAGE_ALIGN)
            dst = pl.multiple_of(start_ref[tile * N_EXPERTS + e] + j * STAGE_ALIGN, RUN_ALIGN)
            return pltpu.make_async_copy(stag.at[s, pl.ds(src, STAGE_ALIGN)], xs_out.at[pl.ds(dst, STAGE_ALIGN)], sem)
        return copy_of

    def drain(tile):
        for rows, pred in _piece_waits(npiece_ref[tile]):
            pl.when(pred)(pltpu.make_async_copy(stag.at[0, pl.ds(0, rows)], xs_out.at[pl.ds(0, rows)], sem).wait)

    @pl.when(i > 0)
    def _():
        drain(i - 1)

    _run_copies(nrun_ref, i, piece_copy(slot, i))

    @pl.when(i == n - 1)
    def _():
        drain(i)


def _moe_dispatch(u2, route, tables, n_rows):
    T, D = u2.shape
    tm = TM_ROUTE
    bm = BM_EXPERT
    return pl.pallas_call(
        _dispatch_kernel,
        grid_spec=pltpu.PrefetchScalarGridSpec(
            num_scalar_prefetch=len(tables),
            grid=(T // tm,),
            in_specs=[pl.BlockSpec((tm, D), lambda i, *_: (i, 0)),
                      pl.BlockSpec((tm, LANES), lambda i, *_: (i, 0))],
            out_specs=pl.BlockSpec(memory_space=pl.ANY),
            scratch_shapes=[pltpu.VMEM((2, STAGE_ROWS, D), F32), pltpu.VMEM((bm, D), F32),
                            pltpu.SemaphoreType.DMA(()), pltpu.SemaphoreType.DMA(())],
        ),
        out_shape=jax.ShapeDtypeStruct((n_rows, D), F32),
        compiler_params=_cparams(("arbitrary",)),
        name="moe_dispatch",
    )(*tables, u2, route)


GU_GROUP = 2 * LANES


def _deinterleave_perm():
    p = np.zeros((GU_GROUP, GU_GROUP), np.float32)
    l = np.arange(LANES)
    p[2 * l, l] = 1.0
    p[2 * l + 1, LANES + l] = 1.0
    return jnp.asarray(p, BF16)


def _expert_kernel(be_ref, nb_ref, valid_ref, xs_ref, wgu_hbm, bgu_ref, wd_hbm, bd_ref, perm_ref, ys_ref,
                   wg_buf, wd_buf, wp, wdb, sem):
    i = pl.program_id(0)
    bm = xs_ref.shape[0]
    e = be_ref[i]
    used = i < nb_ref[0]
    new_expert = (i == 0) | (e != be_ref[jnp.maximum(i - 1, 0)])
    n_groups = wp.shape[1] // GU_GROUP
    n_experts = wgu_hbm.shape[0]

    def fetch(ex):
        return (pltpu.make_async_copy(wgu_hbm.at[ex], wg_buf, sem.at[0]),
                pltpu.make_async_copy(wd_hbm.at[ex], wd_buf, sem.at[1]))

    @pl.when(used & new_expert)
    def _():
        @pl.when(i == 0)
        def _():
            for cp in fetch(e):
                cp.start()

        for cp in fetch(e):
            cp.wait()
        for j in range(n_groups):
            sl = slice(j * GU_GROUP, (j + 1) * GU_GROUP)
            wp[:, sl] = jnp.dot(wg_buf[:, sl].astype(BF16), perm_ref[...], preferred_element_type=F32).astype(BF16)
        wdb[...] = wd_buf[...].astype(BF16)

        @pl.when(e + 1 < n_experts)
        def _():
            for cp in fetch(e + 1):
                cp.start()

    def compute(m):
        xb = xs_ref[:m, :].astype(BF16)
        gu = jnp.dot(xb, wp[...], preferred_element_type=F32) + bgu_ref[...]
        hs = []
        for j in range(n_groups):
            gate = jnp.minimum(gu[:, j * GU_GROUP:j * GU_GROUP + LANES], SWIGLU_LIMIT)
            up = jnp.clip(gu[:, j * GU_GROUP + LANES:(j + 1) * GU_GROUP], -SWIGLU_LIMIT, SWIGLU_LIMIT)
            hs.append(((up + 1.0) * (gate * _sigmoid(gate * SWIGLU_ALPHA))).astype(BF16))
        h = jnp.concatenate(hs, axis=-1)
        ys_ref[:m, :] = jnp.dot(h, wdb[...], preferred_element_type=F32) + bd_ref[...]

    sizes = (bm, bm // 2, bm // 4)
    for m, smaller in zip(sizes, sizes[1:] + (0,)):
        @pl.when(used & (valid_ref[i] <= m) & (valid_ref[i] > smaller))
        def _(m=m):
            compute(m)
            if m < bm:
                ys_ref[m:, :] = jnp.zeros((bm - m, ys_ref.shape[1]), F32)

    @pl.when(used & (valid_ref[i] <= 0))
    def _():
        ys_ref[...] = jnp.zeros_like(ys_ref)

    @pl.when(jnp.logical_not(used))
    def _():
        ys_ref[...] = jnp.zeros_like(ys_ref)


def _moe_experts(xs, block_e, n_used, valid, w_gu, b_gu_p, w_dn, b_dn):
    n_rows, D = xs.shape
    E, _, F2 = w_gu.shape
    bm = BM_EXPERT
    n_blocks = n_rows // bm
    perm = _deinterleave_perm()
    wspec = lambda shp: pl.BlockSpec((None,) + shp, lambda i, be, nb, va: (be[i], 0, 0))
    hbm = pl.BlockSpec(memory_space=pl.ANY)
    return pl.pallas_call(
        _expert_kernel,
        grid_spec=pltpu.PrefetchScalarGridSpec(
            num_scalar_prefetch=3,
            grid=(n_blocks,),
            in_specs=[pl.BlockSpec((bm, D), lambda i, be, nb, va: (jnp.minimum(i, nb[0] - 1), 0)),
                      hbm, wspec((1, F2)), hbm, wspec((1, D)),
                      pl.BlockSpec(perm.shape, lambda i, be, nb, va: (0, 0))],
            out_specs=pl.BlockSpec((bm, D), lambda i, be, nb, va: (i, 0)),
            scratch_shapes=[pltpu.VMEM((D, F2), F32), pltpu.VMEM((F2 // 2, D), F32),
                            pltpu.VMEM((D, F2), BF16), pltpu.VMEM((F2 // 2, D), BF16),
                            pltpu.SemaphoreType.DMA((2,))],
        ),
        out_shape=jax.ShapeDtypeStruct((n_rows, D), F32),
        compiler_params=pltpu.CompilerParams(dimension_semantics=("arbitrary",), vmem_limit_bytes=VMEM_LIMIT_EXPERTS),
        name="moe_experts",
    )(block_e, n_used, valid, xs, w_gu, b_gu_p, w_dn, b_dn, perm)


def _combine_kernel(nrun_ref, off_ref, start_ref, npiece_ref, ys_ref, x1_ref, route_ref, gt_ref, g_ref,
                    b_ref, o_ref, stag, sem, *, alpha):
    i = pl.program_id(0)
    n = pl.num_programs(0)
    tm = x1_ref.shape[0]
    slot = i % 2

    def gather(tile, s):
        def copy_of(e, j):
            src = pl.multiple_of(start_ref[tile * N_EXPERTS + e] + j * STAGE_ALIGN, RUN_ALIGN)
            dst = pl.multiple_of(off_ref[tile * N_EXPERTS + e] + j * STAGE_ALIGN, STAGE_ALIGN)
            return pltpu.make_async_copy(ys_ref.at[pl.ds(src, STAGE_ALIGN)], stag.at[s, pl.ds(dst, STAGE_ALIGN)],
                                         sem.at[s])
        _run_copies(nrun_ref, tile, copy_of)

    @pl.when(i == 0)
    def _():
        stag[...] = jnp.zeros_like(stag)
        gather(0, 0)

    @pl.when(i + 1 < n)
    def _():
        gather(i + 1, 1 - slot)

    for rows, pred in _piece_waits(npiece_ref[i]):
        pl.when(pred)(pltpu.make_async_copy(ys_ref.at[pl.ds(0, rows)], stag.at[slot, pl.ds(0, rows)], sem.at[slot]).wait)

    route = route_ref[...]
    cols = lax.broadcasted_iota(I32, (tm, STAGE_ROWS), 1)
    wgt = jnp.zeros((tm, STAGE_ROWS), F32)
    for k in range(TOP_K_EXPERTS):
        spos = route[:, 2 * TOP_K_EXPERTS + k:2 * TOP_K_EXPERTS + k + 1].astype(I32)
        wgt = wgt + jnp.where(cols == spos, route[:, TOP_K_EXPERTS + k:TOP_K_EXPERTS + k + 1], 0.0)
    ffn = jnp.dot(wgt.astype(BF16), stag[slot].astype(BF16), preferred_element_type=F32)
    o_ref[...] = _layernorm_rows(alpha * x1_ref[...] + (1.0 + gt_ref[...]) * ffn, g_ref[...], b_ref[...])


def _moe_combine(ys, tables, x1, route, gt2, ln_g, ln_b, tiles_per_batch):
    T, D = x1.shape
    tm = TM_ROUTE
    row = lambda v: v.reshape(1, -1)
    return pl.pallas_call(
        functools.partial(_combine_kernel, alpha=2.0 ** 0.25),
        grid_spec=pltpu.PrefetchScalarGridSpec(
            num_scalar_prefetch=len(tables),
            grid=(T // tm,),
            in_specs=[pl.BlockSpec(memory_space=pl.ANY),
                      pl.BlockSpec((tm, D), lambda i, *_: (i, 0)),
                      pl.BlockSpec((tm, LANES), lambda i, *_: (i, 0)),
                      pl.BlockSpec((None, 1, D), lambda i, *_: (i // tiles_per_batch, 0, 0)),
                      pl.BlockSpec((1, D), lambda i, *_: (0, 0)),
                      pl.BlockSpec((1, D), lambda i, *_: (0, 0))],
            out_specs=pl.BlockSpec((tm, D), lambda i, *_: (i, 0)),
            scratch_shapes=[pltpu.VMEM((2, STAGE_ROWS, D), F32), pltpu.SemaphoreType.DMA((2,))],
        ),
        out_shape=jax.ShapeDtypeStruct((T, D), F32),
        compiler_params=_cparams(("arbitrary",)),
        name="moe_combine",
    )(*tables, ys, x1, route, gt2, row(ln_g), row(ln_b))


def _moe_and_norm(x1, u2, route, tbl, totals, gt2, w_gu, b_gu, w_dn, b_dn, ln_g, ln_b):
    B, S, D = x1.shape
    T = B * S
    bm = BM_EXPERT
    tm = TM_ROUTE
    assert T % tm == 0 and TM_PROJ == tm and STAGE_ROWS >= tm * TOP_K_EXPERTS + N_EXPERTS * (STAGE_ALIGN - 1)
    n_tiles = T // tm
    max_rows = T * TOP_K_EXPERTS + n_tiles * N_EXPERTS * (RUN_ALIGN - 1) + N_EXPERTS * STAGE_ALIGN
    n_blocks = -(-max_rows // bm) + N_EXPERTS
    tot = totals[0, :N_EXPERTS].astype(I32)
    padded = (tot + STAGE_ALIGN + bm - 1) // bm * bm
    pad_ends = jnp.cumsum(padded)
    pad_starts = pad_ends - padded
    t3 = tbl.reshape(n_tiles, SUBLANES, LANES)[:, :, :N_EXPERTS].astype(I32)
    nrun, off, base = t3[:, 0, :], t3[:, 1, :], t3[:, 2, :]
    start = pad_starts[None, :] + base
    pieces = (nrun * RUN_ALIGN + STAGE_ALIGN - 1) // STAGE_ALIGN
    npiece = jnp.sum(pieces, axis=1)
    flat = lambda z: z.reshape(-1).astype(I32)
    run_tables = [flat(nrun), flat(off), flat(start), flat(npiece)]
    zs = pad_starts + tot
    zn = (pad_ends - zs) // RUN_ALIGN
    n_used = (pad_ends[-1:] // bm).astype(I32)
    tail = jnp.concatenate([n_used, jnp.full((1,), n_blocks, I32)])
    blk_row = jnp.arange(n_blocks, dtype=I32) * bm
    block_e = jnp.minimum(jnp.sum((blk_row[:, None] >= pad_ends[None, :]).astype(I32), axis=1), N_EXPERTS - 1)
    E, F2 = b_gu.shape
    b_gu_p = b_gu.reshape(E, F2 // GU_GROUP, LANES, 2).transpose(0, 1, 3, 2).reshape(E, 1, F2)
    route2 = route.reshape(T, LANES)
    xs = _moe_dispatch(u2.reshape(T, D), route2, run_tables + [flat(zs), flat(zn), tail], n_blocks * bm)
    valid = jnp.clip(zs[block_e] - blk_row, 0, bm)
    ys = _moe_experts(xs, block_e, n_used, valid, w_gu, b_gu_p, w_dn, b_dn[:, None, :])
    out = _moe_combine(ys, run_tables, x1.reshape(T, D), route2, gt2, ln_g, ln_b, S // tm)
    return out.reshape(B, S, D)


def kernel(x, c, w_ada, b_ada, w_in, shift_mu, rwkv_w0, rwkv_w2, rwkv_a0, rwkv_a2, rwkv_g2, rwkv_k_k, rwkv_k_a, rwkv_r_k, rwkv_ln_w, rwkv_ln_b, mla_q_norm, mla_w_q_up, mla_kv_norm, mla_w_uk, mla_w_uv, idx_w_q, idx_ln_g, idx_ln_b, w_out, ln1_g, ln1_b, w_router, b_router, w_gu, b_gu, w_dn, b_dn, ln2_g, ln2_b):
    depth = w_ada.shape[0]
    assert depth == 1, "DeepNorm constants below are for a single layer"
    l = 0
    mod = _ada_mod(c, w_ada[l], b_ada[l])
    sh1, sc1, gt1, sh2, sc2, gt2 = [m[:, None, :] for m in jnp.split(mod, 6, axis=-1)]
    r, lw, k, v, kk, a, g, qabs, iq, ik, iw, cl, clt = _in_proj(
        x, sc1, sh1, w_in[l], shift_mu[l], rwkv_w0[l], rwkv_w2[l], rwkv_a0[l], rwkv_a2[l], rwkv_g2[l], rwkv_k_k[l],
        rwkv_k_a[l], mla_q_norm[l], mla_w_q_up[l], mla_w_uk[l], mla_kv_norm[l], idx_w_q[l], idx_ln_g[l], idx_ln_b[l])
    o_rwkv = _rwkv_scan(r, lw, k, v, kk, a, g, rwkv_r_k[l], rwkv_ln_w[l], rwkv_ln_b[l])
    o_dsa = _dsa_attn(iq, iw, qabs, ik, cl, clt, mla_w_uv[l])
    x1, u2, route, tbl, totals = _mix_out(o_rwkv, o_dsa, x, gt1, sc2, sh2, w_out[l], ln1_g[l], ln1_b[l],
                                          w_router[l], b_router[l])
    return _moe_and_norm(x1, u2, route, tbl, totals, gt2, w_gu[l], b_gu[l], w_dn[l], b_dn[l], ln2_g[l], ln2_b[l])
```

```python
import functools
import math

import jax
import jax.numpy as jnp
import numpy as np
from jax import lax
from jax.experimental import pallas as pl
from jax.experimental.pallas import tpu as pltpu

F32 = jnp.float32
BF16 = jnp.bfloat16
I32 = jnp.int32

RWKV_HEAD = 64
N_RWKV_HEADS = 8
D_RWKV = RWKV_HEAD * N_RWKV_HEADS
RWKV_GN_EPS = 64e-5
ATT_HEAD = 64
N_ATT_HEADS = 8
D_ATT = ATT_HEAD * N_ATT_HEADS
KV_LORA = 128
IDX_HEADS = 8
IDX_DIM = 64
TOPK_MAX = 256
N_EXPERTS = 32
TOP_K_EXPERTS = 4
SWIGLU_LIMIT = 7.0
SWIGLU_ALPHA = 1.702
NEG_BIG = -1e30
LOG2E = 1.4426950408889634
INT_MIN = -(2 ** 31)

LANES = 128
SUBLANES = 8
VMEM_LIMIT = 48 * 1024 * 1024
VMEM_LIMIT_EXPERTS = 58 * 1024 * 1024

TM_IN = 512
TM_PROJ = 256
L_CHUNK = 64
CHUNKS_PER_STEP = 4
TQ = 256
KEY_CHUNK = 256
SUM_ROWS = 128
TM_ROUTE = 256
BM_EXPERT = 1024

_SEG = (("r", 512, 512), ("k", 512, 512), ("v", 512, 512), ("w", 64, 128), ("a", 64, 128), ("g", 128, 128),
        ("q", 256, 256), ("kv", 128, 128), ("ik", 64, 128), ("iw", 8, 128))
N_SHIFT_P = 512 * 3 + 128 * 3
N_IN_P = sum(s[2] for s in _SEG)


def _cparams(sem):
    return pltpu.CompilerParams(dimension_semantics=sem, vmem_limit_bytes=VMEM_LIMIT)


def _bdot(a, b):
    return jnp.dot(a.astype(BF16), b.astype(BF16), preferred_element_type=F32)


def _split2(a):
    hi = a.astype(BF16)
    lo = (a - hi.astype(F32)).astype(BF16)
    return hi, lo


def _split3(a):
    hi = a.astype(BF16)
    r1 = a - hi.astype(F32)
    mid = r1.astype(BF16)
    lo = (r1 - mid.astype(F32)).astype(BF16)
    return hi, mid, lo


def _dot3(a, b, dims=(((1,), (0,)), ((), ()))):
    ah, al = _split2(a)
    bh, bl = _split2(b)
    d = functools.partial(lax.dot_general, dimension_numbers=dims, preferred_element_type=F32)
    return d(ah, bh) + (d(ah, bl) + d(al, bh))


def _dot_exact_rhs(a, b_exact, nsplit=3):
    parts = _split3(a) if nsplit == 3 else _split2(a)
    acc = None
    for p in parts[::-1]:
        t = jnp.dot(p, b_exact, preferred_element_type=F32)
        acc = t if acc is None else acc + t
    return acc


def _dot_exact_lhs(a_exact, b, nsplit=3):
    parts = _split3(b) if nsplit == 3 else _split2(b)
    acc = None
    for p in parts[::-1]:
        t = jnp.dot(a_exact, p, preferred_element_type=F32)
        acc = t if acc is None else acc + t
    return acc


def _sigmoid(x):
    return 1.0 / (1.0 + jnp.exp(-x))


def _softplus(x):
    return jnp.maximum(x, 0.0) + jnp.log(1.0 + jnp.exp(-jnp.abs(x)))


def _ada_kernel(c_ref, w_ref, b_ref, o_ref):
    c = c_ref[...]
    o_ref[...] = _dot3(c * _sigmoid(c), w_ref[...]) + b_ref[...]


def _ada_mod(c, w_ada, b_ada):
    B, D = c.shape
    N = w_ada.shape[1]
    tn = 1024
    return pl.pallas_call(
        _ada_kernel,
        grid=(N // tn,),
        in_specs=[pl.BlockSpec((B, D), lambda j: (0, 0)),
                  pl.BlockSpec((D, tn), lambda j: (0, j)),
                  pl.BlockSpec((1, tn), lambda j: (0, j))],
        out_specs=pl.BlockSpec((B, tn), lambda j: (0, j)),
        out_shape=jax.ShapeDtypeStruct((B, N), F32),
        compiler_params=_cparams(("arbitrary",)),
        name="ada_mod",
    )(c, w_ada, b_ada.reshape(1, N))


def _in_proj_kernel(x_ref, sc_ref, sh_ref, win_ref, mu_ref, w0_ref, w2_ref, a0_ref, a2_ref, g2_ref, kk_ref, ka_ref,
                    ones_ref, qn_ref, wq_ref, wuk_ref, kvn_ref, wiq_ref, ig_ref, ib_ref,
                    r_o, lw_o, k_o, v_o, kkn_o, a_o, g_o, qabs_o, iq_o, ik_o, iw_o, cl_o, clt_o, carry):
    i = pl.program_id(1)
    tm = x_ref.shape[0]

    @pl.when(i == 0)
    def _():
        carry[...] = jnp.zeros_like(carry)

    u = x_ref[...] * (1.0 + sc_ref[...]) + sh_ref[...]
    p = _bdot(u, win_ref[...])
    ps = p[:, :N_SHIFT_P]
    rows = lax.broadcasted_iota(I32, (tm, 1), 0)
    prev = jnp.where(rows == 0, carry[0:1, :], pltpu.roll(ps, 1, 0))
    carry[0:1, :] = ps[tm - 1:tm, :]
    ps = ps + mu_ref[...] * (prev - ps)

    pr, pk, pv = ps[:, 0:512], ps[:, 512:1024], ps[:, 1024:1536]
    pw, pa, pg = ps[:, 1536:1664], ps[:, 1664:1792], ps[:, 1792:1920]
    w_log = -_softplus(-(w0_ref[...] + _dot3(jnp.tanh(pw), w2_ref[...]))) - 0.5
    lw_o[...] = -jnp.exp(w_log)
    a = _sigmoid(a0_ref[...] + _dot3(pa, a2_ref[...]))
    g_o[...] = _dot3(_sigmoid(pg), g2_ref[...])
    kk = pk * kk_ref[...]
    ssq = _dot_exact_rhs(kk * kk, ones_ref[...], nsplit=2)
    kkn_o[...] = kk / jnp.maximum(jnp.sqrt(ssq), 1e-12)
    k_o[...] = pk * (1.0 + (a - 1.0) * ka_ref[...])
    r_o[...] = pr
    v_o[...] = pv
    a_o[...] = a

    pq, pkv = p[:, 1920:2176], p[:, 2176:2304]
    pik, piw = p[:, 2304:2432], p[:, 2432:2560]
    q_lat = pq * lax.rsqrt(jnp.mean(pq * pq, axis=-1, keepdims=True) + 1e-6) * qn_ref[...]
    q = _bdot(q_lat, wq_ref[...])
    qabs_o[...] = (_bdot(q, wuk_ref[...]) * (ATT_HEAD ** -0.5 * LOG2E)).astype(BF16)
    c_lat = pkv * lax.rsqrt(jnp.mean(pkv * pkv, axis=-1, keepdims=True) + 1e-6) * kvn_ref[...]
    spos = i * tm + rows
    s_hi = (spos >> 7).astype(F32)
    s_lo = (spos & (LANES - 1)).astype(F32)
    lane_t = lax.broadcasted_iota(I32, (1, LANES), 1)
    extra = jnp.where(lane_t == 0, 1.0, jnp.where((lane_t == 1) | (lane_t == 2), s_hi,
                                                  jnp.where((lane_t == 3) | (lane_t == 4), s_lo, 0.0)))
    c_aug = jnp.concatenate([c_lat, extra], axis=-1)
    cl_o[...] = c_aug.astype(BF16)
    clt_o[...] = jnp.transpose(c_aug).astype(BF16)
    iq_o[...] = (_bdot(q_lat, wiq_ref[...]) * (IDX_DIM ** -0.5)).astype(BF16)
    lane = lax.broadcasted_iota(I32, (1, LANES), 1)
    valid = lane < IDX_DIM
    mu = jnp.sum(pik, axis=-1, keepdims=True) * (1.0 / IDX_DIM)
    dlt = jnp.where(valid, pik - mu, 0.0)
    var = jnp.sum(dlt * dlt, axis=-1, keepdims=True) * (1.0 / IDX_DIM)
    ik_o[...] = jnp.where(valid, dlt * lax.rsqrt(var + 1e-5) * ig_ref[...] + ib_ref[...], 0.0).astype(BF16)
    iw_o[...] = piw * (IDX_HEADS ** -0.5)


def _pad_cols(w, widths):
    parts, o = [], 0
    for true, padded in widths:
        seg = w[..., o:o + true]
        if padded > true:
            seg = jnp.pad(seg, [(0, 0)] * (w.ndim - 1) + [(0, padded - true)])
        parts.append(seg)
        o += true
    return jnp.concatenate(parts, axis=-1)


def _pad_rows(w, rows):
    return jnp.pad(w, ((0, rows - w.shape[0]), (0, 0)))


def _block_diag(blocks):
    H, a, b = blocks.shape
    eye = jnp.eye(H, dtype=blocks.dtype)
    return (eye[:, None, :, None] * blocks[:, :, None, :]).reshape(H * a, H * b)


def _head_ones(n, head):
    idx = np.arange(n) // head
    return jnp.asarray(idx[:, None] == idx[None, :], BF16)


def _in_proj(x, sc1, sh1, w_in, shift_mu, w0, w2, a0, a2, g2, k_k, k_a, q_norm, w_q_up, w_uk, kv_norm, idx_w_q,
             idx_ln_g, idx_ln_b):
    B, S, D = x.shape
    tm = min(TM_IN, S)
    widths = tuple((s[1], s[2]) for s in _SEG)
    win_p = _pad_cols(w_in, widths).astype(BF16)
    mu_p = _pad_cols(shift_mu.reshape(1, -1), widths[:6])
    w2_p = _pad_rows(w2, LANES)
    a2_p = _pad_rows(a2, LANES)
    wuk_bd = _block_diag(w_uk).astype(BF16)
    wiq_p = _pad_cols(idx_w_q, ((IDX_DIM, LANES),) * IDX_HEADS).astype(BF16)
    ig_p = _pad_cols(idx_ln_g.reshape(1, -1), ((IDX_DIM, LANES),))
    ib_p = _pad_cols(idx_ln_b.reshape(1, -1), ((IDX_DIM, LANES),))
    row = lambda v: v.reshape(1, -1)
    tok = lambda n: pl.BlockSpec((None, tm, n), lambda b, i: (b, i, 0))
    mod = pl.BlockSpec((None, 1, D), lambda b, i: (b, 0, 0))
    full = lambda a: pl.BlockSpec(a.shape, lambda b, i: (0,) * a.ndim)
    consts = [win_p, mu_p, row(w0), w2_p, row(a0), a2_p, g2, row(k_k), row(k_a), _head_ones(D_RWKV, RWKV_HEAD),
              row(q_norm), w_q_up.astype(BF16), wuk_bd, row(kv_norm), wiq_p, ig_p, ib_p]
    outs = [(D_RWKV, F32)] * 7 + [(N_ATT_HEADS * KV_LORA, BF16), (IDX_HEADS * LANES, BF16), (LANES, BF16),
                                  (LANES, F32), (KV_LORA + LANES, BF16)]
    return pl.pallas_call(
        _in_proj_kernel,
        grid=(B, S // tm),
        in_specs=[tok(D), mod, mod] + [full(a) for a in consts],
        out_specs=[tok(n) for n, _ in outs] + [pl.BlockSpec((None, KV_LORA + LANES, tm), lambda b, i: (b, 0, i))],
        out_shape=[jax.ShapeDtypeStruct((B, S, n), dt) for n, dt in outs]
        + [jax.ShapeDtypeStruct((B, KV_LORA + LANES, S), BF16)],
        scratch_shapes=[pltpu.VMEM((8, N_SHIFT_P), F32)],
        compiler_params=_cparams(("arbitrary", "arbitrary")),
        name="in_proj",
    )(x, sc1, sh1, *consts)


def _rwkv_kernel(r_ref, lw_ref, k_ref, v_ref, kk_ref, a_ref, g_ref, rk_ref, lnw_ref, lnb_ref, tri_ref, ones_ref,
                 o_ref, state, *, L):
    c = pl.program_id(1)
    nc = r_ref.shape[0] // L

    @pl.when(c == 0)
    def _():
        state[...] = jnp.zeros_like(state)

    r, lw, k, v, kk, a = r_ref[...], lw_ref[...], k_ref[...], v_ref[...], kk_ref[...], a_ref[...]
    cum = _dot_exact_lhs(tri_ref[...], lw)
    rows = [slice(ci * L, (ci + 1) * L) for ci in range(nc)]
    last = [cum[(ci + 1) * L - 1:(ci + 1) * L, :] for ci in range(nc)]
    cum_last = jnp.concatenate([jnp.broadcast_to(z, (L, z.shape[1])) for z in last], axis=0)
    w_incl = jnp.exp(cum)
    w_inv = jnp.exp(-cum)
    w_rel = jnp.exp(cum_last - cum)
    w_last = [jnp.exp(z) for z in last]
    bvec = kk * a
    at = -kk * jnp.exp(cum - lw)
    rt = r * w_incl
    bt = bvec * w_inv
    kt = k * w_inv
    bh = bvec * w_rel
    kh = k * w_rel
    ti = lax.broadcasted_iota(I32, (L, L), 0)
    tj = lax.broadcasted_iota(I32, (L, L), 1)
    strict = tj < ti
    incl = tj <= ti
    eye = lax.broadcasted_iota(I32, (RWKV_HEAD, RWKV_HEAD), 0) == lax.broadcasted_iota(I32, (RWKV_HEAD, RWKV_HEAD), 1)
    NT = (((1,), (1,)), ((), ()))
    TN = (((0,), (0,)), ((), ()))
    heads = range(N_RWKV_HEADS)
    sls = [slice(h * RWKV_HEAD, (h + 1) * RWKV_HEAD) for h in heads]
    units = [(ci, h) for ci in range(nc) for h in heads]
    idx = range(len(units))
    mm = lambda x, y, dims=(((1,), (0,)), ((), ())): lax.dot_general(
        x.astype(BF16), y.astype(BF16), dims, preferred_element_type=F32)
    at_b, rt_b, bt_b, kt_b, bh_b, kh_b, v_b = [z.astype(BF16) for z in (at, rt, bt, kt, bh, kh, v)]
    cut = lambda z, u: z[rows[u[0]], sls[u[1]]]
    vh = [cut(v_b, u) for u in units]
    ath = [cut(at_b, u) for u in units]
    ar = [jnp.concatenate([cut(at_b, u), cut(rt_b, u)], axis=0) for u in units]
    g_b = [mm(ar[i], cut(bt_b, units[i]), NT) for i in idx]
    g_k = [mm(ar[i], cut(kt_b, units[i]), NT) for i in idx]
    n_ab = [jnp.where(strict, g_b[i][:L], 0.0) for i in idx]
    a_ak = [jnp.where(strict, g_k[i][:L], 0.0) for i in idx]
    a_rb = [jnp.where(incl, g_b[i][L:], 0.0).astype(BF16) for i in idx]
    a_rk = [jnp.where(incl, g_k[i][L:], 0.0) for i in idx]
    akv = [mm(a_ak[i], vh[i]) for i in idx]
    eye_l = jnp.where(ti == tj, 1.0, 0.0)
    tinv = [eye_l + n_ab[i] for i in idx]
    pw = n_ab
    for _ in range(int(math.log2(L)) - 1):
        pw = [mm(pw[i], pw[i]) for i in idx]
        tinv = [tinv[i] + mm(pw[i], tinv[i]) for i in idx]
    tinv = [t.astype(BF16) for t in tinv]
    a_t = [mm(tinv[i], ath[i]).astype(BF16) for i in idx]
    y = [mm(tinv[i], akv[i]).astype(BF16) for i in idx]
    m_c = [jnp.where(eye, w_last[units[i][0]][:, sls[units[i][1]]], 0.0) + mm(a_t[i], cut(bh_b, units[i]), TN)
           for i in idx]
    c_c = [mm(y[i], cut(bh_b, units[i]), TN) + mm(vh[i], cut(kh_b, units[i]), TN) for i in idx]
    q_c = [cut(rt, units[i]) + mm(a_rb[i], a_t[i]) for i in idx]
    o_loc = [mm(a_rb[i], y[i]) + mm(a_rk[i], vh[i]) for i in idx]
    s = [state[h] for h in heads]
    for ci in range(nc):
        for h in heads:
            i = ci * N_RWKV_HEADS + h
            o = o_loc[i] + mm(q_c[i], s[h], NT)
            s[h] = mm(s[h], m_c[i]) + c_c[i]
            mu = jnp.mean(o, axis=-1, keepdims=True)
            d = o - mu
            var = jnp.mean(d * d, axis=-1, keepdims=True)
            o_ref[rows[ci], sls[h]] = d * lax.rsqrt(var + RWKV_GN_EPS)
    for h in heads:
        state[h] = s[h]
    bonus = _dot_exact_rhs(r * k * rk_ref[...], ones_ref[...], nsplit=3) * v
    o_ref[...] = (o_ref[...] * lnw_ref[...] + lnb_ref[...] + bonus) * g_ref[...]


def _rwkv_scan(r, lw, k, v, kk, a, g, r_k, ln_w, ln_b):
    B, S, DR = r.shape
    L = min(L_CHUNK, S)
    lb = min(L * CHUNKS_PER_STEP, S)
    tri = jnp.asarray(np.kron(np.eye(lb // L), np.tril(np.ones((L, L)))), BF16)
    row = lambda z: z.reshape(1, -1)
    tok = pl.BlockSpec((None, lb, DR), lambda b, c: (b, c, 0))
    full = lambda z: pl.BlockSpec(z.shape, lambda b, c: (0,) * z.ndim)
    consts = [row(r_k), row(ln_w), row(ln_b), tri, _head_ones(DR, RWKV_HEAD)]
    return pl.pallas_call(
        functools.partial(_rwkv_kernel, L=L),
        grid=(B, S // lb),
        in_specs=[tok] * 7 + [full(z) for z in consts],
        out_specs=tok,
        out_shape=jax.ShapeDtypeStruct((B, S, DR), F32),
        scratch_shapes=[pltpu.VMEM((N_RWKV_HEADS, RWKV_HEAD, RWKV_HEAD), F32)],
        compiler_params=_cparams(("arbitrary", "arbitrary")),
        name="rwkv_scan",
    )(r, lw, k, v, kk, a, g, *consts)


def _alibi_cols():
    slope = np.asarray([2.0 ** (-8.0 * (h + 1) / N_ATT_HEADS) * LOG2E for h in range(N_ATT_HEADS)], np.float32)
    c_hi = slope.astype(BF16).astype(np.float32)
    c_lo = (slope - c_hi).astype(BF16).astype(np.float32)
    t = np.zeros((N_ATT_HEADS, LANES), np.float32)
    t[:, 1], t[:, 2], t[:, 3], t[:, 4] = LANES * c_hi, LANES * c_lo, c_hi, c_lo
    return jnp.asarray(t.astype(BF16))


def _dsa_kernel(iq_ref, iw_ref, qa_ref, ik_ref, ca_ref, cat_ref, wuv_ref, tril_ref, acol_ref, o_ref, key_ref,
                bias_ref, lg_ref, p_ref, *, topk, q_off, select):
    qi = pl.program_id(1) + q_off
    tq = qa_ref.shape[0]
    sk = ca_ref.shape[0]
    kc = min(KEY_CHUNK, sk)
    chunks = [slice(c * kc, (c + 1) * kc) for c in range(sk // kc)]
    tpos = qi * tq + lax.broadcasted_iota(I32, (1, tq), 1)
    srow = lax.broadcasted_iota(I32, (kc, tq), 0)
    NT = (((1,), (1,)), ((), ()))

    if not select:
        for c, cs in enumerate(chunks):
            bias_ref[cs, :] = jnp.where(srow + c * kc <= tpos, 0.0, NEG_BIG)
    else:
        iw_t = jnp.transpose(iw_ref[...])
        for c, cs in enumerate(chunks):
            ikc = ik_ref[cs, :]
            score = jnp.zeros((kc, tq), F32)
            for h in range(IDX_HEADS):
                s = lax.dot_general(ikc, iq_ref[:, h * LANES:(h + 1) * LANES], NT, preferred_element_type=F32)
                score = score + iw_t[h:h + 1, :] * jnp.maximum(s, 0.0)
            bits = pltpu.bitcast(score + 0.0, I32)
            key = bits ^ ((bits >> 31) & 0x7FFFFFFF)
            key_ref[cs, :] = jnp.where(srow + c * kc <= tpos, key, INT_MIN)
        kcount = jnp.minimum(topk, tpos + 1).astype(F32)

        def sum_keys(x):
            part = jnp.sum(x.reshape(x.shape[0] // SUM_ROWS, SUM_ROWS, x.shape[1]), axis=0)
            return jnp.sum(part, axis=0, keepdims=True)

        def count_ge(cand):
            return sum_keys(jnp.where(key_ref[...] >= cand, 1.0, 0.0))

        thr = jnp.where(count_ge(jnp.zeros((1, tq), I32)) >= kcount, 0, INT_MIN).astype(I32)

        def bit_step(i, thr):
            cand = thr | (1 << (30 - i))
            return jnp.where(count_ge(cand) >= kcount, cand, thr)

        thr = lax.fori_loop(0, 31, bit_step, thr)

        need = kcount - sum_keys(jnp.where(key_ref[...] > thr, 1.0, 0.0))
        carry = jnp.zeros((1, tq), F32)
        for j in range(sk // LANES):
            sl = slice(j * LANES, (j + 1) * LANES)
            key = key_ref[sl, :]
            eq = key == thr
            e = jnp.where(eq, 1.0, 0.0)
            before = jnp.dot(tril_ref[...], e.astype(BF16), preferred_element_type=F32) + carry
            sel = (key > thr) | (eq & (before < need))
            bias_ref[sl, :] = jnp.where(sel, 0.0, NEG_BIG)
            carry = carry + jnp.sum(e, axis=0, keepdims=True)

    outs = []
    for h in range(N_ATT_HEADS):
        q_aug = jnp.concatenate([qa_ref[:, h * KV_LORA:(h + 1) * KV_LORA],
                                 jnp.broadcast_to(acol_ref[h:h + 1, :], (tq, LANES))], axis=-1)
        lg_all = lax.dot_general(ca_ref[...], q_aug, NT, preferred_element_type=F32)
        m = jnp.full((1, tq), -jnp.inf, F32)
        for cs in chunks:
            lg = lg_all[cs, :] + bias_ref[cs, :]
            lg_ref[cs, :] = lg
            m = jnp.maximum(m, jnp.max(lg, axis=0, keepdims=True))
        for cs in chunks:
            p_ref[cs, :] = jnp.exp2(lg_ref[cs, :] - m).astype(BF16)
        pv = jnp.dot(cat_ref[...], p_ref[...], preferred_element_type=F32)
        outs.append((pv[:KV_LORA, :] / pv[KV_LORA:KV_LORA + 1, :]).astype(BF16))
    o_lat_t = jnp.concatenate(outs, axis=0)
    o_ref[...] = lax.dot_general(o_lat_t, wuv_ref[...], (((0,), (0,)), ((), ())), preferred_element_type=F32)


def _dsa_attn(iq, iw, qabs, ik, ca, cat, w_uv):
    B, S, _ = iq.shape
    tq = min(TQ, S)
    topk = min(TOPK_MAX, S // 4)
    wuv_bd = _block_diag(w_uv).astype(BF16)
    tril = jnp.asarray(np.tril(np.ones((LANES, LANES)), -1), BF16)
    acol = _alibi_cols()
    full = lambda z: pl.BlockSpec(z.shape, lambda b, i: (0,) * z.ndim)
    nq = 1
    outs = []
    for q_off in range(0, S // tq, nq):
        sk = (q_off + nq) * tq
        tok = lambda n, q_off=q_off: pl.BlockSpec((None, tq, n), lambda b, i: (b, i + q_off, 0))
        seq = lambda n, sk=sk: pl.BlockSpec((None, sk, n), lambda b, i: (b, 0, 0))
        outs.append(pl.pallas_call(
            functools.partial(_dsa_kernel, topk=topk, q_off=q_off, select=sk > topk),
            grid=(B, nq),
            in_specs=[tok(IDX_HEADS * LANES), tok(LANES), tok(N_ATT_HEADS * KV_LORA), seq(LANES), seq(KV_LORA + LANES),
                      pl.BlockSpec((None, KV_LORA + LANES, sk), lambda b, i: (b, 0, 0)),
                      full(wuv_bd), full(tril), full(acol)],
            out_specs=pl.BlockSpec((None, tq, D_ATT), lambda b, i: (b, i, 0)),
            out_shape=jax.ShapeDtypeStruct((B, nq * tq, D_ATT), F32),
            scratch_shapes=[pltpu.VMEM((sk, tq), I32), pltpu.VMEM((sk, tq), F32), pltpu.VMEM((sk, tq), F32),
                            pltpu.VMEM((sk, tq), BF16)],
            compiler_params=_cparams(("arbitrary", "arbitrary")),
            name=f"dsa_attn_k{sk}",
        )(iq, iw, qabs, ik, ca, cat, wuv_bd, tril, acol))
    return jnp.concatenate(outs, axis=1)


def _layernorm_rows(y, g, b):
    mu = jnp.mean(y, axis=-1, keepdims=True)
    d = y - mu
    var = jnp.mean(d * d, axis=-1, keepdims=True)
    return d * lax.rsqrt(var + 1e-5) * g + b


def _mix_kernel(orw_ref, ods_ref, x_ref, gt_ref, sc_ref, sh_ref, wtop_ref, wbot_ref, g_ref, b_ref, wr_ref, br_ref,
                tril_ref, triu_ref, x1_o, u2_o, route_o, tbl_o, cnt_o, carry, *, alpha):
    first = (pl.program_id(0) == 0) & (pl.program_id(1) == 0)
    tm = x_ref.shape[0]

    @pl.when(first)
    def _():
        carry[...] = jnp.zeros_like(carry)

    mix = _bdot(orw_ref[...], wtop_ref[...]) + _bdot(ods_ref[...], wbot_ref[...])
    x1 = _layernorm_rows(alpha * x_ref[...] + (1.0 + gt_ref[...]) * mix, g_ref[...], b_ref[...])
    x1_o[...] = x1
    u2 = x1 * (1.0 + sc_ref[...]) + sh_ref[...]
    u2_o[...] = u2

    lg = _dot3(u2, wr_ref[...]) + br_ref[...]
    lane = lax.broadcasted_iota(I32, (tm, LANES), 1)
    lane_f = lane.astype(F32)
    idxs, vals = [], []
    for _ in range(TOP_K_EXPERTS):
        m = jnp.max(lg, axis=-1, keepdims=True)
        idx = jnp.min(jnp.where(lg == m, lane_f, float(LANES)), axis=-1, keepdims=True).astype(I32)
        idxs.append(idx)
        vals.append(m)
        lg = jnp.where(lane == idx, -jnp.inf, lg)
    es = [jnp.exp(v - vals[0]) for v in vals]
    den = es[0] + es[1] + es[2] + es[3]
    hot = jnp.zeros((tm, LANES), F32)
    for idx in idxs:
        hot = hot + jnp.where(lane == idx, 1.0, 0.0)
    before = jnp.dot(tril_ref[...], hot.astype(BF16), preferred_element_type=F32)
    cnt = jnp.sum(hot, axis=0, keepdims=True)
    n_run = jnp.floor((cnt + (RUN_ALIGN - 1)) * (1.0 / RUN_ALIGN))
    n_stage = jnp.floor((n_run * RUN_ALIGN + (STAGE_ALIGN - 1)) * (1.0 / STAGE_ALIGN))
    off = jnp.dot(jnp.broadcast_to(n_stage, (SUBLANES, LANES)).astype(BF16), triu_ref[...],
                  preferred_element_type=F32)[0:1, :] * STAGE_ALIGN
    where_in_stage = off + before
    route = jnp.zeros((tm, LANES), F32)
    for k in range(TOP_K_EXPERTS):
        spos = jnp.sum(jnp.where(lane == idxs[k], where_in_stage, 0.0), axis=-1, keepdims=True)
        route = jnp.where(lane == k, idxs[k].astype(F32), route)
        route = jnp.where(lane == TOP_K_EXPERTS + k, es[k] / den, route)
        route = jnp.where(lane == 2 * TOP_K_EXPERTS + k, spos, route)
    route_o[...] = route
    sub = lax.broadcasted_iota(I32, (SUBLANES, LANES), 0)
    tbl_o[...] = jnp.where(sub == 0, n_run, jnp.where(sub == 1, off, jnp.where(sub == 2, carry[0:1, :], 0.0)))
    carry[0:1, :] = carry[0:1, :] + n_run * RUN_ALIGN
    cnt_o[...] = carry[...]


def _mix_out(o_rwkv, o_dsa, x, gt1, sc2, sh2, w_out, ln_g, ln_b, w_router, b_router):
    B, S, D = x.shape
    tm = min(TM_PROJ, S)
    alpha = 2.0 ** 0.25
    wtop = w_out[:D_RWKV].astype(BF16)
    wbot = w_out[D_RWKV:].astype(BF16)
    wr_p = jnp.pad(w_router, ((0, 0), (0, LANES - N_EXPERTS)))
    br_p = jnp.pad(b_router.reshape(1, -1), ((0, 0), (0, LANES - N_EXPERTS)), constant_values=NEG_BIG)
    tril = jnp.asarray(np.tril(np.ones((tm, tm)), -1), BF16)
    triu = jnp.asarray(np.triu(np.ones((LANES, LANES)), 1), BF16)
    row = lambda v: v.reshape(1, -1)
    tok = lambda n: pl.BlockSpec((None, tm, n), lambda b, i: (b, i, 0))
    mod = pl.BlockSpec((None, 1, D), lambda b, i: (b, 0, 0))
    full = lambda a: pl.BlockSpec(a.shape, lambda b, i: (0,) * a.ndim)
    consts = [wtop, wbot, row(ln_g), row(ln_b), wr_p, br_p, tril, triu]
    return pl.pallas_call(
        functools.partial(_mix_kernel, alpha=alpha),
        grid=(B, S // tm),
        in_specs=[tok(D_RWKV), tok(D_ATT), tok(D), mod, mod, mod] + [full(a) for a in consts],
        out_specs=[tok(D), tok(D), tok(LANES), pl.BlockSpec((None, None, SUBLANES, LANES), lambda b, i: (b, i, 0, 0)),
                   pl.BlockSpec((SUBLANES, LANES), lambda b, i: (0, 0))],
        out_shape=[jax.ShapeDtypeStruct((B, S, D), F32), jax.ShapeDtypeStruct((B, S, D), F32),
                   jax.ShapeDtypeStruct((B, S, LANES), F32),
                   jax.ShapeDtypeStruct((B, S // tm, SUBLANES, LANES), F32),
                   jax.ShapeDtypeStruct((SUBLANES, LANES), F32)],
        scratch_shapes=[pltpu.VMEM((8, LANES), F32)],
        compiler_params=_cparams(("arbitrary", "arbitrary")),
        name="mix_out",
    )(o_rwkv, o_dsa, x, gt1, sc2, sh2, *consts)


RUN_ALIGN = SUBLANES
STAGE_ALIGN = 16
STAGE_ROWS = 1536


def _piece_waits(n_pieces):
    max_pieces = STAGE_ROWS // STAGE_ALIGN
    return [(STAGE_ALIGN << b, (n_pieces >> b) & 1 == 1) for b in range(max_pieces.bit_length())
            if STAGE_ALIGN << b <= STAGE_ROWS]


MAX_PIECES = STAGE_ROWS // STAGE_ALIGN


def _run_copies(npiece_ref, tile, copy_of):
    def piece(p, carry):
        copy_of(tile * MAX_PIECES + p).start()
        return carry

    lax.fori_loop(0, npiece_ref[tile], piece, 0)


def _dispatch_kernel(pstage_ref, prow_ref, npiece_ref, zs_ref, zn_ref, tail_ref,
                     u_ref, route_ref, xs_out, stag, zeros, sem, zsem):
    i = pl.program_id(0)
    n = pl.num_programs(0)
    tm = u_ref.shape[0]
    bm = zeros.shape[0]
    slot = i % 2

    @pl.when(i == 0)
    def _():
        zeros[...] = jnp.zeros_like(zeros)
        fills = []
        for e in range(N_EXPERTS):
            for b in range(bm.bit_length()):
                rows = RUN_ALIGN << b
                if rows > bm:
                    break
                done = (zn_ref[e] >> (b + 1)) << (b + 1)
                dst = pl.multiple_of(zs_ref[e] + done * RUN_ALIGN, RUN_ALIGN)
                fills.append(((zn_ref[e] >> b) & 1 == 1,
                              pltpu.make_async_copy(zeros.at[pl.ds(0, rows)], xs_out.at[pl.ds(dst, rows)], zsem)))
        for pred, cp in fills:
            pl.when(pred)(cp.start)

        def tail_copy(j):
            return pltpu.make_async_copy(zeros, xs_out.at[pl.ds(pl.multiple_of(j * bm, bm), bm)], zsem)

        lax.fori_loop(tail_ref[0], tail_ref[1], lambda j, c: (tail_copy(j).start(), c)[1], 0)
        for pred, cp in fills:
            pl.when(pred)(cp.wait)
        lax.fori_loop(tail_ref[0], tail_ref[1], lambda j, c: (tail_copy(j).wait(), c)[1], 0)

    route_t = jnp.transpose(route_ref[...])
    spos = [route_t[2 * TOP_K_EXPERTS + k:2 * TOP_K_EXPERTS + k + 1, :].astype(I32) for k in range(TOP_K_EXPERTS)]
    rows = lax.broadcasted_iota(I32, (STAGE_ROWS, tm), 0)
    sel = (rows == spos[0]) | (rows == spos[1]) | (rows == spos[2]) | (rows == spos[3])
    stag[slot] = jnp.dot(jnp.where(sel, 1.0, 0.0).astype(BF16), u_ref[...].astype(BF16), preferred_element_type=F32)

    def copy_of(p):
        src = pl.multiple_of(pstage_ref[p], STAGE_ALIGN)
        dst = pl.multiple_of(prow_ref[p], RUN_ALIGN)
        return pltpu.make_async_copy(stag.at[slot, pl.ds(src, STAGE_ALIGN)], xs_out.at[pl.ds(dst, STAGE_ALIGN)], sem)

    def drain(tile):
        for rows, pred in _piece_waits(npiece_ref[tile]):
            pl.when(pred)(pltpu.make_async_copy(stag.at[0, pl.ds(0, rows)], xs_out.at[pl.ds(0, rows)], sem).wait)

    @pl.when(i > 0)
    def _():
        drain(i - 1)

    _run_copies(npiece_ref, i, copy_of)

    @pl.when(i == n - 1)
    def _():
        drain(i)


def _moe_dispatch(u2, route, tables, n_rows):
    T, D = u2.shape
    tm = TM_ROUTE
    bm = BM_EXPERT
    return pl.pallas_call(
        _dispatch_kernel,
        grid_spec=pltpu.PrefetchScalarGridSpec(
            num_scalar_prefetch=len(tables),
            grid=(T // tm,),
            in_specs=[pl.BlockSpec((tm, D), lambda i, *_: (i, 0)),
                      pl.BlockSpec((tm, LANES), lambda i, *_: (i, 0))],
            out_specs=pl.BlockSpec(memory_space=pl.ANY),
            scratch_shapes=[pltpu.VMEM((2, STAGE_ROWS, D), F32), pltpu.VMEM((bm, D), F32),
                            pltpu.SemaphoreType.DMA(()), pltpu.SemaphoreType.DMA(())],
        ),
        out_shape=jax.ShapeDtypeStruct((n_rows, D), F32),
        compiler_params=_cparams(("arbitrary",)),
        name="moe_dispatch",
    )(*tables, u2, route)


GU_GROUP = 2 * LANES


def _deinterleave_perm():
    p = np.zeros((GU_GROUP, GU_GROUP), np.float32)
    l = np.arange(LANES)
    p[2 * l, l] = 1.0
    p[2 * l + 1, LANES + l] = 1.0
    return jnp.asarray(p, BF16)


def _expert_kernel(be_ref, nb_ref, valid_ref, xs_ref, wgu_hbm, bgu_ref, wd_hbm, bd_ref, perm_ref, ys_ref,
                   wg_buf, wd_buf, wp, wdb, sem):
    i = pl.program_id(0)
    bm = xs_ref.shape[0]
    e = be_ref[i]
    used = i < nb_ref[0]
    new_expert = (i == 0) | (e != be_ref[jnp.maximum(i - 1, 0)])
    n_groups = wp.shape[1] // GU_GROUP
    n_experts = wgu_hbm.shape[0]

    def fetch(ex):
        return (pltpu.make_async_copy(wgu_hbm.at[ex], wg_buf, sem.at[0]),
                pltpu.make_async_copy(wd_hbm.at[ex], wd_buf, sem.at[1]))

    @pl.when(used & new_expert)
    def _():
        @pl.when(i == 0)
        def _():
            for cp in fetch(e):
                cp.start()

        for cp in fetch(e):
            cp.wait()
        for j in range(n_groups):
            sl = slice(j * GU_GROUP, (j + 1) * GU_GROUP)
            wp[:, sl] = jnp.dot(wg_buf[:, sl].astype(BF16), perm_ref[...], preferred_element_type=F32).astype(BF16)
        wdb[...] = wd_buf[...].astype(BF16)

        @pl.when(e + 1 < n_experts)
        def _():
            for cp in fetch(e + 1):
                cp.start()

    def compute(m):
        xb = xs_ref[:m, :].astype(BF16)
        gu = jnp.dot(xb, wp[...], preferred_element_type=F32) + bgu_ref[...]
        hs = []
        for j in range(n_groups):
            gate = jnp.minimum(gu[:, j * GU_GROUP:j * GU_GROUP + LANES], SWIGLU_LIMIT)
            up = jnp.clip(gu[:, j * GU_GROUP + LANES:(j + 1) * GU_GROUP], -SWIGLU_LIMIT, SWIGLU_LIMIT)
            hs.append(((up + 1.0) * (gate * _sigmoid(gate * SWIGLU_ALPHA))).astype(BF16))
        h = jnp.concatenate(hs, axis=-1)
        ys_ref[:m, :] = jnp.dot(h, wdb[...], preferred_element_type=F32) + bd_ref[...]

    sizes = (bm, bm // 2, bm // 4)
    for m, smaller in zip(sizes, sizes[1:] + (0,)):
        @pl.when(used & (valid_ref[i] <= m) & (valid_ref[i] > smaller))
        def _(m=m):
            compute(m)
            if m < bm:
                ys_ref[m:, :] = jnp.zeros((bm - m, ys_ref.shape[1]), F32)

    @pl.when(used & (valid_ref[i] <= 0))
    def _():
        ys_ref[...] = jnp.zeros_like(ys_ref)

    @pl.when(jnp.logical_not(used))
    def _():
        ys_ref[...] = jnp.zeros_like(ys_ref)


def _moe_experts(xs, block_e, n_used, valid, w_gu, b_gu_p, w_dn, b_dn):
    n_rows, D = xs.shape
    E, _, F2 = w_gu.shape
    bm = BM_EXPERT
    n_blocks = n_rows // bm
    perm = _deinterleave_perm()
    wspec = lambda shp: pl.BlockSpec((None,) + shp, lambda i, be, nb, va: (be[i], 0, 0))
    hbm = pl.BlockSpec(memory_space=pl.ANY)
    return pl.pallas_call(
        _expert_kernel,
        grid_spec=pltpu.PrefetchScalarGridSpec(
            num_scalar_prefetch=3,
            grid=(n_blocks,),
            in_specs=[pl.BlockSpec((bm, D), lambda i, be, nb, va: (jnp.minimum(i, nb[0] - 1), 0)),
                      hbm, wspec((1, F2)), hbm, wspec((1, D)),
                      pl.BlockSpec(perm.shape, lambda i, be, nb, va: (0, 0))],
            out_specs=pl.BlockSpec((bm, D), lambda i, be, nb, va: (i, 0)),
            scratch_shapes=[pltpu.VMEM((D, F2), F32), pltpu.VMEM((F2 // 2, D), F32),
                            pltpu.VMEM((D, F2), BF16), pltpu.VMEM((F2 // 2, D), BF16),
                            pltpu.SemaphoreType.DMA((2,))],
        ),
        out_shape=jax.ShapeDtypeStruct((n_rows, D), F32),
        compiler_params=pltpu.CompilerParams(dimension_semantics=("arbitrary",), vmem_limit_bytes=VMEM_LIMIT_EXPERTS),
        name="moe_experts",
    )(block_e, n_used, valid, xs, w_gu, b_gu_p, w_dn, b_dn, perm)


def _combine_kernel(pstage_ref, prow_ref, npiece_ref, ys_ref, x1_ref, route_ref, gt_ref, g_ref,
                    b_ref, o_ref, stag, sem, *, alpha):
    i = pl.program_id(0)
    n = pl.num_programs(0)
    tm = x1_ref.shape[0]
    slot = i % 2

    def gather(tile, s):
        def copy_of(p):
            src = pl.multiple_of(prow_ref[p], RUN_ALIGN)
            dst = pl.multiple_of(pstage_ref[p], STAGE_ALIGN)
            return pltpu.make_async_copy(ys_ref.at[pl.ds(src, STAGE_ALIGN)], stag.at[s, pl.ds(dst, STAGE_ALIGN)],
                                         sem.at[s])
        _run_copies(npiece_ref, tile, copy_of)

    @pl.when(i == 0)
    def _():
        stag[...] = jnp.zeros_like(stag)
        gather(0, 0)

    @pl.when(i + 1 < n)
    def _():
        gather(i + 1, 1 - slot)

    for rows, pred in _piece_waits(npiece_ref[i]):
        pl.when(pred)(pltpu.make_async_copy(ys_ref.at[pl.ds(0, rows)], stag.at[slot, pl.ds(0, rows)], sem.at[slot]).wait)

    route = route_ref[...]
    cols = lax.broadcasted_iota(I32, (tm, STAGE_ROWS), 1)
    wgt = jnp.zeros((tm, STAGE_ROWS), F32)
    for k in range(TOP_K_EXPERTS):
        spos = route[:, 2 * TOP_K_EXPERTS + k:2 * TOP_K_EXPERTS + k + 1].astype(I32)
        wgt = wgt + jnp.where(cols == spos, route[:, TOP_K_EXPERTS + k:TOP_K_EXPERTS + k + 1], 0.0)
    ffn = jnp.dot(wgt.astype(BF16), stag[slot].astype(BF16), preferred_element_type=F32)
    o_ref[...] = _layernorm_rows(alpha * x1_ref[...] + (1.0 + gt_ref[...]) * ffn, g_ref[...], b_ref[...])


def _moe_combine(ys, tables, x1, route, gt2, ln_g, ln_b, tiles_per_batch):
    T, D = x1.shape
    tm = TM_ROUTE
    row = lambda v: v.reshape(1, -1)
    return pl.pallas_call(
        functools.partial(_combine_kernel, alpha=2.0 ** 0.25),
        grid_spec=pltpu.PrefetchScalarGridSpec(
            num_scalar_prefetch=len(tables),
            grid=(T // tm,),
            in_specs=[pl.BlockSpec(memory_space=pl.ANY),
                      pl.BlockSpec((tm, D), lambda i, *_: (i, 0)),
                      pl.BlockSpec((tm, LANES), lambda i, *_: (i, 0)),
                      pl.BlockSpec((None, 1, D), lambda i, *_: (i // tiles_per_batch, 0, 0)),
                      pl.BlockSpec((1, D), lambda i, *_: (0, 0)),
                      pl.BlockSpec((1, D), lambda i, *_: (0, 0))],
            out_specs=pl.BlockSpec((tm, D), lambda i, *_: (i, 0)),
            scratch_shapes=[pltpu.VMEM((2, STAGE_ROWS, D), F32), pltpu.SemaphoreType.DMA((2,))],
        ),
        out_shape=jax.ShapeDtypeStruct((T, D), F32),
        compiler_params=_cparams(("arbitrary",)),
        name="moe_combine",
    )(*tables, ys, x1, route, gt2, row(ln_g), row(ln_b))


def _moe_and_norm(x1, u2, route, tbl, totals, gt2, w_gu, b_gu, w_dn, b_dn, ln_g, ln_b):
    B, S, D = x1.shape
    T = B * S
    bm = BM_EXPERT
    tm = TM_ROUTE
    assert T % tm == 0 and TM_PROJ == tm and STAGE_ROWS >= tm * TOP_K_EXPERTS + N_EXPERTS * (STAGE_ALIGN - 1)
    n_tiles = T // tm
    max_rows = T * TOP_K_EXPERTS + n_tiles * N_EXPERTS * (RUN_ALIGN - 1) + N_EXPERTS * STAGE_ALIGN
    n_blocks = -(-max_rows // bm) + N_EXPERTS
    tot = totals[0, :N_EXPERTS].astype(I32)
    padded = (tot + STAGE_ALIGN + bm - 1) // bm * bm
    pad_ends = jnp.cumsum(padded)
    pad_starts = pad_ends - padded
    t3 = tbl.reshape(n_tiles, SUBLANES, LANES)[:, :, :N_EXPERTS].astype(I32)
    nrun, off, base = t3[:, 0, :], t3[:, 1, :], t3[:, 2, :]
    start = pad_starts[None, :] + base
    pieces = (nrun * RUN_ALIGN + STAGE_ALIGN - 1) // STAGE_ALIGN
    npiece = jnp.sum(pieces, axis=1)
    flat = lambda z: z.reshape(-1).astype(I32)
    ends = jnp.cumsum(pieces, axis=1)
    p_idx = jnp.arange(MAX_PIECES, dtype=I32)
    p_exp = jnp.minimum(jnp.sum((p_idx[None, :, None] >= ends[:, None, :]).astype(I32), axis=2), N_EXPERTS - 1)
    pick = lambda z: jnp.take_along_axis(z, p_exp, axis=1)
    within = (p_idx[None, :] - pick(ends - pieces)) * STAGE_ALIGN
    run_tables = [flat(pick(off) + within), flat(pick(start) + within), flat(npiece)]
    zs = pad_starts + tot
    zn = (pad_ends - zs) // RUN_ALIGN
    n_used = (pad_ends[-1:] // bm).astype(I32)
    tail = jnp.concatenate([n_used, jnp.full((1,), n_blocks, I32)])
    blk_row = jnp.arange(n_blocks, dtype=I32) * bm
    block_e = jnp.minimum(jnp.sum((blk_row[:, None] >= pad_ends[None, :]).astype(I32), axis=1), N_EXPERTS - 1)
    E, F2 = b_gu.shape
    b_gu_p = b_gu.reshape(E, F2 // GU_GROUP, LANES, 2).transpose(0, 1, 3, 2).reshape(E, 1, F2)
    route2 = route.reshape(T, LANES)
    xs = _moe_dispatch(u2.reshape(T, D), route2, run_tables + [flat(zs), flat(zn), tail], n_blocks * bm)
    valid = jnp.clip(zs[block_e] - blk_row, 0, bm)
    ys = _moe_experts(xs, block_e, n_used, valid, w_gu, b_gu_p, w_dn, b_dn[:, None, :])
    out = _moe_combine(ys, run_tables, x1.reshape(T, D), route2, gt2, ln_g, ln_b, S // tm)
    return out.reshape(B, S, D)


def kernel(x, c, w_ada, b_ada, w_in, shift_mu, rwkv_w0, rwkv_w2, rwkv_a0, rwkv_a2, rwkv_g2, rwkv_k_k, rwkv_k_a, rwkv_r_k, rwkv_ln_w, rwkv_ln_b, mla_q_norm, mla_w_q_up, mla_kv_norm, mla_w_uk, mla_w_uv, idx_w_q, idx_ln_g, idx_ln_b, w_out, ln1_g, ln1_b, w_router, b_router, w_gu, b_gu, w_dn, b_dn, ln2_g, ln2_b):
    depth = w_ada.shape[0]
    assert depth == 1, "DeepNorm constants below are for a single layer"
    l = 0
    mod = _ada_mod(c, w_ada[l], b_ada[l])
    sh1, sc1, gt1, sh2, sc2, gt2 = [m[:, None, :] for m in jnp.split(mod, 6, axis=-1)]
    r, lw, k, v, kk, a, g, qabs, iq, ik, iw, cl, clt = _in_proj(
        x, sc1, sh1, w_in[l], shift_mu[l], rwkv_w0[l], rwkv_w2[l], rwkv_a0[l], rwkv_a2[l], rwkv_g2[l], rwkv_k_k[l],
        rwkv_k_a[l], mla_q_norm[l], mla_w_q_up[l], mla_w_uk[l], mla_kv_norm[l], idx_w_q[l], idx_ln_g[l], idx_ln_b[l])
    o_rwkv = _rwkv_scan(r, lw, k, v, kk, a, g, rwkv_r_k[l], rwkv_ln_w[l], rwkv_ln_b[l])
    o_dsa = _dsa_attn(iq, iw, qabs, ik, cl, clt, mla_w_uv[l])
    x1, u2, route, tbl, totals = _mix_out(o_rwkv, o_dsa, x, gt1, sc2, sh2, w_out[l], ln1_g[l], ln1_b[l],
                                          w_router[l], b_router[l])
    return _moe_and_norm(x1, u2, route, tbl, totals, gt2, w_gu[l], b_gu[l], w_dn[l], b_dn[l], ln2_g[l], ln2_b[l])
```

```python
import functools
import math

import jax
import jax.numpy as jnp
import numpy as np
from jax import lax
from jax.experimental import pallas as pl
from jax.experimental.pallas import tpu as pltpu

F32 = jnp.float32
BF16 = jnp.bfloat16
I32 = jnp.int32

RWKV_HEAD = 64
N_RWKV_HEADS = 8
D_RWKV = RWKV_HEAD * N_RWKV_HEADS
RWKV_GN_EPS = 64e-5
ATT_HEAD = 64
N_ATT_HEADS = 8
D_ATT = ATT_HEAD * N_ATT_HEADS
KV_LORA = 128
IDX_HEADS = 8
IDX_DIM = 64
TOPK_MAX = 256
N_EXPERTS = 32
TOP_K_EXPERTS = 4
SWIGLU_LIMIT = 7.0
SWIGLU_ALPHA = 1.702
NEG_BIG = -1e30
LOG2E = 1.4426950408889634
INT_MIN = -(2 ** 31)

LANES = 128
SUBLANES = 8
VMEM_LIMIT = 48 * 1024 * 1024
VMEM_LIMIT_EXPERTS = 58 * 1024 * 1024

TM_IN = 512
TM_PROJ = 256
L_CHUNK = 64
CHUNKS_PER_STEP = 4
TQ = 256
KEY_CHUNK = 256
SUM_ROWS = 128
TM_ROUTE = 256
BM_EXPERT = 1024

_SEG = (("r", 512, 512), ("k", 512, 512), ("v", 512, 512), ("w", 64, 128), ("a", 64, 128), ("g", 128, 128),
        ("q", 256, 256), ("kv", 128, 128), ("ik", 64, 128), ("iw", 8, 128))
N_SHIFT_P = 512 * 3 + 128 * 3
N_IN_P = sum(s[2] for s in _SEG)


def _cparams(sem):
    return pltpu.CompilerParams(dimension_semantics=sem, vmem_limit_bytes=VMEM_LIMIT)


def _bdot(a, b):
    return jnp.dot(a.astype(BF16), b.astype(BF16), preferred_element_type=F32)


def _split2(a):
    hi = a.astype(BF16)
    lo = (a - hi.astype(F32)).astype(BF16)
    return hi, lo


def _split3(a):
    hi = a.astype(BF16)
    r1 = a - hi.astype(F32)
    mid = r1.astype(BF16)
    lo = (r1 - mid.astype(F32)).astype(BF16)
    return hi, mid, lo


def _dot3(a, b, dims=(((1,), (0,)), ((), ()))):
    ah, al = _split2(a)
    bh, bl = _split2(b)
    d = functools.partial(lax.dot_general, dimension_numbers=dims, preferred_element_type=F32)
    return d(ah, bh) + (d(ah, bl) + d(al, bh))


def _dot_exact_rhs(a, b_exact, nsplit=3):
    parts = _split3(a) if nsplit == 3 else _split2(a)
    acc = None
    for p in parts[::-1]:
        t = jnp.dot(p, b_exact, preferred_element_type=F32)
        acc = t if acc is None else acc + t
    return acc


def _dot_exact_lhs(a_exact, b, nsplit=3):
    parts = _split3(b) if nsplit == 3 else _split2(b)
    acc = None
    for p in parts[::-1]:
        t = jnp.dot(a_exact, p, preferred_element_type=F32)
        acc = t if acc is None else acc + t
    return acc


def _sigmoid(x):
    return 1.0 / (1.0 + jnp.exp(-x))


def _softplus(x):
    return jnp.maximum(x, 0.0) + jnp.log(1.0 + jnp.exp(-jnp.abs(x)))


def _ada_kernel(c_ref, w_ref, b_ref, o_ref):
    c = c_ref[...]
    o_ref[...] = _dot3(c * _sigmoid(c), w_ref[...]) + b_ref[...]


def _ada_mod(c, w_ada, b_ada):
    B, D = c.shape
    N = w_ada.shape[1]
    tn = 1024
    return pl.pallas_call(
        _ada_kernel,
        grid=(N // tn,),
        in_specs=[pl.BlockSpec((B, D), lambda j: (0, 0)),
                  pl.BlockSpec((D, tn), lambda j: (0, j)),
                  pl.BlockSpec((1, tn), lambda j: (0, j))],
        out_specs=pl.BlockSpec((B, tn), lambda j: (0, j)),
        out_shape=jax.ShapeDtypeStruct((B, N), F32),
        compiler_params=_cparams(("arbitrary",)),
        name="ada_mod",
    )(c, w_ada, b_ada.reshape(1, N))


def _in_proj_kernel(x_ref, sc_ref, sh_ref, win_ref, mu_ref, w0_ref, w2_ref, a0_ref, a2_ref, g2_ref, kk_ref, ka_ref,
                    ones_ref, qn_ref, wq_ref, wuk_ref, kvn_ref, wiq_ref, ig_ref, ib_ref,
                    r_o, lw_o, k_o, v_o, kkn_o, a_o, g_o, qabs_o, iq_o, ik_o, iw_o, cl_o, clt_o, carry):
    i = pl.program_id(1)
    tm = x_ref.shape[0]

    @pl.when(i == 0)
    def _():
        carry[...] = jnp.zeros_like(carry)

    u = x_ref[...] * (1.0 + sc_ref[...]) + sh_ref[...]
    p = _bdot(u, win_ref[...])
    ps = p[:, :N_SHIFT_P]
    rows = lax.broadcasted_iota(I32, (tm, 1), 0)
    prev = jnp.where(rows == 0, carry[0:1, :], pltpu.roll(ps, 1, 0))
    carry[0:1, :] = ps[tm - 1:tm, :]
    ps = ps + mu_ref[...] * (prev - ps)

    pr, pk, pv = ps[:, 0:512], ps[:, 512:1024], ps[:, 1024:1536]
    pw, pa, pg = ps[:, 1536:1664], ps[:, 1664:1792], ps[:, 1792:1920]
    w_log = -_softplus(-(w0_ref[...] + _dot3(jnp.tanh(pw), w2_ref[...]))) - 0.5
    lw_o[...] = -jnp.exp(w_log)
    a = _sigmoid(a0_ref[...] + _dot3(pa, a2_ref[...]))
    g_o[...] = _dot3(_sigmoid(pg), g2_ref[...])
    kk = pk * kk_ref[...]
    ssq = _dot_exact_rhs(kk * kk, ones_ref[...], nsplit=2)
    kkn_o[...] = kk / jnp.maximum(jnp.sqrt(ssq), 1e-12)
    k_o[...] = pk * (1.0 + (a - 1.0) * ka_ref[...])
    r_o[...] = pr
    v_o[...] = pv
    a_o[...] = a

    pq, pkv = p[:, 1920:2176], p[:, 2176:2304]
    pik, piw = p[:, 2304:2432], p[:, 2432:2560]
    q_lat = pq * lax.rsqrt(jnp.mean(pq * pq, axis=-1, keepdims=True) + 1e-6) * qn_ref[...]
    q = _bdot(q_lat, wq_ref[...])
    qabs_o[...] = (_bdot(q, wuk_ref[...]) * (ATT_HEAD ** -0.5 * LOG2E)).astype(BF16)
    c_lat = pkv * lax.rsqrt(jnp.mean(pkv * pkv, axis=-1, keepdims=True) + 1e-6) * kvn_ref[...]
    spos = i * tm + rows
    s_hi = (spos >> 7).astype(F32)
    s_lo = (spos & (LANES - 1)).astype(F32)
    lane_t = lax.broadcasted_iota(I32, (1, LANES), 1)
    extra = jnp.where(lane_t == 0, 1.0, jnp.where((lane_t == 1) | (lane_t == 2), s_hi,
                                                  jnp.where((lane_t == 3) | (lane_t == 4), s_lo, 0.0)))
    c_aug = jnp.concatenate([c_lat, extra], axis=-1)
    cl_o[...] = c_aug.astype(BF16)
    clt_o[...] = jnp.transpose(c_aug).astype(BF16)
    iq_o[...] = (_bdot(q_lat, wiq_ref[...]) * (IDX_DIM ** -0.5)).astype(BF16)
    lane = lax.broadcasted_iota(I32, (1, LANES), 1)
    valid = lane < IDX_DIM
    mu = jnp.sum(pik, axis=-1, keepdims=True) * (1.0 / IDX_DIM)
    dlt = jnp.where(valid, pik - mu, 0.0)
    var = jnp.sum(dlt * dlt, axis=-1, keepdims=True) * (1.0 / IDX_DIM)
    ik_o[...] = jnp.where(valid, dlt * lax.rsqrt(var + 1e-5) * ig_ref[...] + ib_ref[...], 0.0).astype(BF16)
    iw_o[...] = piw * (IDX_HEADS ** -0.5)


def _pad_cols(w, widths):
    parts, o = [], 0
    for true, padded in widths:
        seg = w[..., o:o + true]
        if padded > true:
            seg = jnp.pad(seg, [(0, 0)] * (w.ndim - 1) + [(0, padded - true)])
        parts.append(seg)
        o += true
    return jnp.concatenate(parts, axis=-1)


def _pad_rows(w, rows):
    return jnp.pad(w, ((0, rows - w.shape[0]), (0, 0)))


def _block_diag(blocks):
    H, a, b = blocks.shape
    eye = jnp.eye(H, dtype=blocks.dtype)
    return (eye[:, None, :, None] * blocks[:, :, None, :]).reshape(H * a, H * b)


def _head_ones(n, head):
    idx = np.arange(n) // head
    return jnp.asarray(idx[:, None] == idx[None, :], BF16)


def _in_proj(x, sc1, sh1, w_in, shift_mu, w0, w2, a0, a2, g2, k_k, k_a, q_norm, w_q_up, w_uk, kv_norm, idx_w_q,
             idx_ln_g, idx_ln_b):
    B, S, D = x.shape
    tm = min(TM_IN, S)
    widths = tuple((s[1], s[2]) for s in _SEG)
    win_p = _pad_cols(w_in, widths).astype(BF16)
    mu_p = _pad_cols(shift_mu.reshape(1, -1), widths[:6])
    w2_p = _pad_rows(w2, LANES)
    a2_p = _pad_rows(a2, LANES)
    wuk_bd = _block_diag(w_uk).astype(BF16)
    wiq_p = _pad_cols(idx_w_q, ((IDX_DIM, LANES),) * IDX_HEADS).astype(BF16)
    ig_p = _pad_cols(idx_ln_g.reshape(1, -1), ((IDX_DIM, LANES),))
    ib_p = _pad_cols(idx_ln_b.reshape(1, -1), ((IDX_DIM, LANES),))
    row = lambda v: v.reshape(1, -1)
    tok = lambda n: pl.BlockSpec((None, tm, n), lambda b, i: (b, i, 0))
    mod = pl.BlockSpec((None, 1, D), lambda b, i: (b, 0, 0))
    full = lambda a: pl.BlockSpec(a.shape, lambda b, i: (0,) * a.ndim)
    consts = [win_p, mu_p, row(w0), w2_p, row(a0), a2_p, g2, row(k_k), row(k_a), _head_ones(D_RWKV, RWKV_HEAD),
              row(q_norm), w_q_up.astype(BF16), wuk_bd, row(kv_norm), wiq_p, ig_p, ib_p]
    outs = [(D_RWKV, F32)] * 7 + [(N_ATT_HEADS * KV_LORA, BF16), (IDX_HEADS * LANES, BF16), (LANES, BF16),
                                  (LANES, F32), (KV_LORA + LANES, BF16)]
    return pl.pallas_call(
        _in_proj_kernel,
        grid=(B, S // tm),
        in_specs=[tok(D), mod, mod] + [full(a) for a in consts],
        out_specs=[tok(n) for n, _ in outs] + [pl.BlockSpec((None, KV_LORA + LANES, tm), lambda b, i: (b, 0, i))],
        out_shape=[jax.ShapeDtypeStruct((B, S, n), dt) for n, dt in outs]
        + [jax.ShapeDtypeStruct((B, KV_LORA + LANES, S), BF16)],
        scratch_shapes=[pltpu.VMEM((8, N_SHIFT_P), F32)],
        compiler_params=_cparams(("arbitrary", "arbitrary")),
        name="in_proj",
    )(x, sc1, sh1, *consts)


def _rwkv_kernel(r_ref, lw_ref, k_ref, v_ref, kk_ref, a_ref, g_ref, rk_ref, lnw_ref, lnb_ref, tri_ref, ones_ref,
                 o_ref, state, *, L):
    c = pl.program_id(1)
    nc = r_ref.shape[0] // L

    @pl.when(c == 0)
    def _():
        state[...] = jnp.zeros_like(state)

    r, lw, k, v, kk, a = r_ref[...], lw_ref[...], k_ref[...], v_ref[...], kk_ref[...], a_ref[...]
    cum = _dot_exact_lhs(tri_ref[...], lw)
    rows = [slice(ci * L, (ci + 1) * L) for ci in range(nc)]
    last = [cum[(ci + 1) * L - 1:(ci + 1) * L, :] for ci in range(nc)]
    cum_last = jnp.concatenate([jnp.broadcast_to(z, (L, z.shape[1])) for z in last], axis=0)
    w_incl = jnp.exp(cum)
    w_inv = jnp.exp(-cum)
    w_rel = jnp.exp(cum_last - cum)
    w_last = [jnp.exp(z) for z in last]
    bvec = kk * a
    at = -kk * jnp.exp(cum - lw)
    rt = r * w_incl
    bt = bvec * w_inv
    kt = k * w_inv
    bh = bvec * w_rel
    kh = k * w_rel
    ti = lax.broadcasted_iota(I32, (L, L), 0)
    tj = lax.broadcasted_iota(I32, (L, L), 1)
    strict = tj < ti
    incl = tj <= ti
    eye = lax.broadcasted_iota(I32, (RWKV_HEAD, RWKV_HEAD), 0) == lax.broadcasted_iota(I32, (RWKV_HEAD, RWKV_HEAD), 1)
    NT = (((1,), (1,)), ((), ()))
    TN = (((0,), (0,)), ((), ()))
    heads = range(N_RWKV_HEADS)
    sls = [slice(h * RWKV_HEAD, (h + 1) * RWKV_HEAD) for h in heads]
    units = [(ci, h) for ci in range(nc) for h in heads]
    idx = range(len(units))
    mm = lambda x, y, dims=(((1,), (0,)), ((), ())): lax.dot_general(
        x.astype(BF16), y.astype(BF16), dims, preferred_element_type=F32)
    at_b, rt_b, bt_b, kt_b, bh_b, kh_b, v_b = [z.astype(BF16) for z in (at, rt, bt, kt, bh, kh, v)]
    cut = lambda z, u: z[rows[u[0]], sls[u[1]]]
    vh = [cut(v_b, u) for u in units]
    ath = [cut(at_b, u) for u in units]
    ar = [jnp.concatenate([cut(at_b, u), cut(rt_b, u)], axis=0) for u in units]
    g_b = [mm(ar[i], cut(bt_b, units[i]), NT) for i in idx]
    g_k = [mm(ar[i], cut(kt_b, units[i]), NT) for i in idx]
    n_ab = [jnp.where(strict, g_b[i][:L], 0.0) for i in idx]
    a_ak = [jnp.where(strict, g_k[i][:L], 0.0) for i in idx]
    a_rb = [jnp.where(incl, g_b[i][L:], 0.0).astype(BF16) for i in idx]
    a_rk = [jnp.where(incl, g_k[i][L:], 0.0) for i in idx]
    akv = [mm(a_ak[i], vh[i]) for i in idx]
    eye_l = jnp.where(ti == tj, 1.0, 0.0)
    tinv = [eye_l + n_ab[i] for i in idx]
    pw = n_ab
    for _ in range(int(math.log2(L)) - 1):
        pw = [mm(pw[i], pw[i]) for i in idx]
        tinv = [tinv[i] + mm(pw[i], tinv[i]) for i in idx]
    tinv = [t.astype(BF16) for t in tinv]
    a_t = [mm(tinv[i], ath[i]).astype(BF16) for i in idx]
    y = [mm(tinv[i], akv[i]).astype(BF16) for i in idx]
    m_c = [jnp.where(eye, w_last[units[i][0]][:, sls[units[i][1]]], 0.0) + mm(a_t[i], cut(bh_b, units[i]), TN)
           for i in idx]
    c_c = [mm(y[i], cut(bh_b, units[i]), TN) + mm(vh[i], cut(kh_b, units[i]), TN) for i in idx]
    q_c = [cut(rt, units[i]) + mm(a_rb[i], a_t[i]) for i in idx]
    o_loc = [mm(a_rb[i], y[i]) + mm(a_rk[i], vh[i]) for i in idx]
    s = [state[h] for h in heads]
    for ci in range(nc):
        for h in heads:
            i = ci * N_RWKV_HEADS + h
            o = o_loc[i] + mm(q_c[i], s[h], NT)
            s[h] = mm(s[h], m_c[i]) + c_c[i]
            mu = jnp.mean(o, axis=-1, keepdims=True)
            d = o - mu
            var = jnp.mean(d * d, axis=-1, keepdims=True)
            o_ref[rows[ci], sls[h]] = d * lax.rsqrt(var + RWKV_GN_EPS)
    for h in heads:
        state[h] = s[h]
    bonus = _dot_exact_rhs(r * k * rk_ref[...], ones_ref[...], nsplit=3) * v
    o_ref[...] = (o_ref[...] * lnw_ref[...] + lnb_ref[...] + bonus) * g_ref[...]


def _rwkv_scan(r, lw, k, v, kk, a, g, r_k, ln_w, ln_b):
    B, S, DR = r.shape
    L = min(L_CHUNK, S)
    lb = min(L * CHUNKS_PER_STEP, S)
    tri = jnp.asarray(np.kron(np.eye(lb // L), np.tril(np.ones((L, L)))), BF16)
    row = lambda z: z.reshape(1, -1)
    tok = pl.BlockSpec((None, lb, DR), lambda b, c: (b, c, 0))
    full = lambda z: pl.BlockSpec(z.shape, lambda b, c: (0,) * z.ndim)
    consts = [row(r_k), row(ln_w), row(ln_b), tri, _head_ones(DR, RWKV_HEAD)]
    return pl.pallas_call(
        functools.partial(_rwkv_kernel, L=L),
        grid=(B, S // lb),
        in_specs=[tok] * 7 + [full(z) for z in consts],
        out_specs=tok,
        out_shape=jax.ShapeDtypeStruct((B, S, DR), F32),
        scratch_shapes=[pltpu.VMEM((N_RWKV_HEADS, RWKV_HEAD, RWKV_HEAD), F32)],
        compiler_params=_cparams(("arbitrary", "arbitrary")),
        name="rwkv_scan",
    )(r, lw, k, v, kk, a, g, *consts)


def _alibi_cols():
    slope = np.asarray([2.0 ** (-8.0 * (h + 1) / N_ATT_HEADS) * LOG2E for h in range(N_ATT_HEADS)], np.float32)
    c_hi = slope.astype(BF16).astype(np.float32)
    c_lo = (slope - c_hi).astype(BF16).astype(np.float32)
    t = np.zeros((N_ATT_HEADS, LANES), np.float32)
    t[:, 1], t[:, 2], t[:, 3], t[:, 4] = LANES * c_hi, LANES * c_lo, c_hi, c_lo
    return jnp.asarray(t.astype(BF16))


def _dsa_kernel(iq_ref, iw_ref, qa_ref, ik_ref, ca_ref, cat_ref, wuv_ref, tril_ref, acol_ref, o_ref, key_ref,
                bias_ref, lg_ref, p_ref, *, topk, q_off, select):
    qi = pl.program_id(1) + q_off
    tq = qa_ref.shape[0]
    sk = ca_ref.shape[0]
    kc = min(KEY_CHUNK, sk)
    chunks = [slice(c * kc, (c + 1) * kc) for c in range(sk // kc)]
    tpos = qi * tq + lax.broadcasted_iota(I32, (1, tq), 1)
    srow = lax.broadcasted_iota(I32, (kc, tq), 0)
    NT = (((1,), (1,)), ((), ()))

    if not select:
        for c, cs in enumerate(chunks):
            bias_ref[cs, :] = jnp.where(srow + c * kc <= tpos, 0.0, NEG_BIG)
    else:
        iw_t = jnp.transpose(iw_ref[...])
        for c, cs in enumerate(chunks):
            ikc = ik_ref[cs, :]
            score = jnp.zeros((kc, tq), F32)
            for h in range(IDX_HEADS):
                s = lax.dot_general(ikc, iq_ref[:, h * LANES:(h + 1) * LANES], NT, preferred_element_type=F32)
                score = score + iw_t[h:h + 1, :] * jnp.maximum(s, 0.0)
            bits = pltpu.bitcast(score + 0.0, I32)
            key = bits ^ ((bits >> 31) & 0x7FFFFFFF)
            key_ref[cs, :] = jnp.where(srow + c * kc <= tpos, key, INT_MIN)
        kcount = jnp.minimum(topk, tpos + 1).astype(F32)

        def sum_keys(x):
            part = jnp.sum(x.reshape(x.shape[0] // SUM_ROWS, SUM_ROWS, x.shape[1]), axis=0)
            return jnp.sum(part, axis=0, keepdims=True)

        def count_ge(cand):
            return sum_keys(jnp.where(key_ref[...] >= cand, 1.0, 0.0))

        thr = jnp.where(count_ge(jnp.zeros((1, tq), I32)) >= kcount, 0, INT_MIN).astype(I32)

        def bit_step(i, thr):
            cand = thr | (1 << (30 - i))
            return jnp.where(count_ge(cand) >= kcount, cand, thr)

        thr = lax.fori_loop(0, 31, bit_step, thr)

        need = kcount - sum_keys(jnp.where(key_ref[...] > thr, 1.0, 0.0))
        carry = jnp.zeros((1, tq), F32)
        for j in range(sk // LANES):
            sl = slice(j * LANES, (j + 1) * LANES)
            key = key_ref[sl, :]
            eq = key == thr
            e = jnp.where(eq, 1.0, 0.0)
            before = jnp.dot(tril_ref[...], e.astype(BF16), preferred_element_type=F32) + carry
            sel = (key > thr) | (eq & (before < need))
            bias_ref[sl, :] = jnp.where(sel, 0.0, NEG_BIG)
            carry = carry + jnp.sum(e, axis=0, keepdims=True)

    outs = []
    for h in range(N_ATT_HEADS):
        q_aug = jnp.concatenate([qa_ref[:, h * KV_LORA:(h + 1) * KV_LORA],
                                 jnp.broadcast_to(acol_ref[h:h + 1, :], (tq, LANES))], axis=-1)
        lg_all = lax.dot_general(ca_ref[...], q_aug, NT, preferred_element_type=F32)
        m = jnp.full((1, tq), -jnp.inf, F32)
        for cs in chunks:
            lg = lg_all[cs, :] + bias_ref[cs, :]
            lg_ref[cs, :] = lg
            m = jnp.maximum(m, jnp.max(lg, axis=0, keepdims=True))
        for cs in chunks:
            p_ref[cs, :] = jnp.exp2(lg_ref[cs, :] - m).astype(BF16)
        pv = jnp.dot(cat_ref[...], p_ref[...], preferred_element_type=F32)
        outs.append((pv[:KV_LORA, :] / pv[KV_LORA:KV_LORA + 1, :]).astype(BF16))
    o_lat_t = jnp.concatenate(outs, axis=0)
    o_ref[...] = lax.dot_general(o_lat_t, wuv_ref[...], (((0,), (0,)), ((), ())), preferred_element_type=F32)


def _dsa_attn(iq, iw, qabs, ik, ca, cat, w_uv):
    B, S, _ = iq.shape
    tq = min(TQ, S)
    topk = min(TOPK_MAX, S // 4)
    wuv_bd = _block_diag(w_uv).astype(BF16)
    tril = jnp.asarray(np.tril(np.ones((LANES, LANES)), -1), BF16)
    acol = _alibi_cols()
    full = lambda z: pl.BlockSpec(z.shape, lambda b, i: (0,) * z.ndim)
    nq = 1
    outs = []
    for q_off in range(0, S // tq, nq):
        sk = (q_off + nq) * tq
        tok = lambda n, q_off=q_off: pl.BlockSpec((None, tq, n), lambda b, i: (b, i + q_off, 0))
        seq = lambda n, sk=sk: pl.BlockSpec((None, sk, n), lambda b, i: (b, 0, 0))
        outs.append(pl.pallas_call(
            functools.partial(_dsa_kernel, topk=topk, q_off=q_off, select=sk > topk),
            grid=(B, nq),
            in_specs=[tok(IDX_HEADS * LANES), tok(LANES), tok(N_ATT_HEADS * KV_LORA), seq(LANES), seq(KV_LORA + LANES),
                      pl.BlockSpec((None, KV_LORA + LANES, sk), lambda b, i: (b, 0, 0)),
                      full(wuv_bd), full(tril), full(acol)],
            out_specs=pl.BlockSpec((None, tq, D_ATT), lambda b, i: (b, i, 0)),
            out_shape=jax.ShapeDtypeStruct((B, nq * tq, D_ATT), F32),
            scratch_shapes=[pltpu.VMEM((sk, tq), I32), pltpu.VMEM((sk, tq), F32), pltpu.VMEM((sk, tq), F32),
                            pltpu.VMEM((sk, tq), BF16)],
            compiler_params=_cparams(("arbitrary", "arbitrary")),
            name=f"dsa_attn_k{sk}",
        )(iq, iw, qabs, ik, ca, cat, wuv_bd, tril, acol))
    return jnp.concatenate(outs, axis=1)


def _layernorm_rows(y, g, b):
    mu = jnp.mean(y, axis=-1, keepdims=True)
    d = y - mu
    var = jnp.mean(d * d, axis=-1, keepdims=True)
    return d * lax.rsqrt(var + 1e-5) * g + b


def _mix_kernel(orw_ref, ods_ref, x_ref, gt_ref, sc_ref, sh_ref, wtop_ref, wbot_ref, g_ref, b_ref, wr_ref, br_ref,
                tril_ref, triu_ref, x1_o, u2_o, route_o, tbl_o, cnt_o, carry, *, alpha):
    first = (pl.program_id(0) == 0) & (pl.program_id(1) == 0)
    tm = x_ref.shape[0]

    @pl.when(first)
    def _():
        carry[...] = jnp.zeros_like(carry)

    mix = _bdot(orw_ref[...], wtop_ref[...]) + _bdot(ods_ref[...], wbot_ref[...])
    x1 = _layernorm_rows(alpha * x_ref[...] + (1.0 + gt_ref[...]) * mix, g_ref[...], b_ref[...])
    x1_o[...] = x1
    u2 = x1 * (1.0 + sc_ref[...]) + sh_ref[...]
    u2_o[...] = u2

    lg = _dot3(u2, wr_ref[...]) + br_ref[...]
    lane = lax.broadcasted_iota(I32, (tm, LANES), 1)
    lane_f = lane.astype(F32)
    idxs, vals = [], []
    for _ in range(TOP_K_EXPERTS):
        m = jnp.max(lg, axis=-1, keepdims=True)
        idx = jnp.min(jnp.where(lg == m, lane_f, float(LANES)), axis=-1, keepdims=True).astype(I32)
        idxs.append(idx)
        vals.append(m)
        lg = jnp.where(lane == idx, -jnp.inf, lg)
    es = [jnp.exp(v - vals[0]) for v in vals]
    den = es[0] + es[1] + es[2] + es[3]
    hot = jnp.zeros((tm, LANES), F32)
    for idx in idxs:
        hot = hot + jnp.where(lane == idx, 1.0, 0.0)
    before = jnp.dot(tril_ref[...], hot.astype(BF16), preferred_element_type=F32)
    cnt = jnp.sum(hot, axis=0, keepdims=True)
    n_run = jnp.floor((cnt + (RUN_ALIGN - 1)) * (1.0 / RUN_ALIGN))
    n_stage = jnp.floor((n_run * RUN_ALIGN + (STAGE_ALIGN - 1)) * (1.0 / STAGE_ALIGN))
    off = jnp.dot(jnp.broadcast_to(n_stage, (SUBLANES, LANES)).astype(BF16), triu_ref[...],
                  preferred_element_type=F32)[0:1, :] * STAGE_ALIGN
    where_in_stage = off + before
    route = jnp.zeros((tm, LANES), F32)
    for k in range(TOP_K_EXPERTS):
        spos = jnp.sum(jnp.where(lane == idxs[k], where_in_stage, 0.0), axis=-1, keepdims=True)
        route = jnp.where(lane == k, idxs[k].astype(F32), route)
        route = jnp.where(lane == TOP_K_EXPERTS + k, es[k] / den, route)
        route = jnp.where(lane == 2 * TOP_K_EXPERTS + k, spos, route)
    route_o[...] = route
    sub = lax.broadcasted_iota(I32, (SUBLANES, LANES), 0)
    tbl_o[...] = jnp.where(sub == 0, n_run, jnp.where(sub == 1, off, jnp.where(sub == 2, carry[0:1, :], 0.0)))
    carry[0:1, :] = carry[0:1, :] + n_run * RUN_ALIGN
    cnt_o[...] = carry[...]


def _mix_out(o_rwkv, o_dsa, x, gt1, sc2, sh2, w_out, ln_g, ln_b, w_router, b_router):
    B, S, D = x.shape
    tm = min(TM_PROJ, S)
    alpha = 2.0 ** 0.25
    wtop = w_out[:D_RWKV].astype(BF16)
    wbot = w_out[D_RWKV:].astype(BF16)
    wr_p = jnp.pad(w_router, ((0, 0), (0, LANES - N_EXPERTS)))
    br_p = jnp.pad(b_router.reshape(1, -1), ((0, 0), (0, LANES - N_EXPERTS)), constant_values=NEG_BIG)
    tril = jnp.asarray(np.tril(np.ones((tm, tm)), -1), BF16)
    triu = jnp.asarray(np.triu(np.ones((LANES, LANES)), 1), BF16)
    row = lambda v: v.reshape(1, -1)
    tok = lambda n: pl.BlockSpec((None, tm, n), lambda b, i: (b, i, 0))
    mod = pl.BlockSpec((None, 1, D), lambda b, i: (b, 0, 0))
    full = lambda a: pl.BlockSpec(a.shape, lambda b, i: (0,) * a.ndim)
    consts = [wtop, wbot, row(ln_g), row(ln_b), wr_p, br_p, tril, triu]
    return pl.pallas_call(
        functools.partial(_mix_kernel, alpha=alpha),
        grid=(B, S // tm),
        in_specs=[tok(D_RWKV), tok(D_ATT), tok(D), mod, mod, mod] + [full(a) for a in consts],
        out_specs=[tok(D), tok(D), tok(LANES), pl.BlockSpec((None, None, SUBLANES, LANES), lambda b, i: (b, i, 0, 0)),
                   pl.BlockSpec((SUBLANES, LANES), lambda b, i: (0, 0))],
        out_shape=[jax.ShapeDtypeStruct((B, S, D), F32), jax.ShapeDtypeStruct((B, S, D), F32),
                   jax.ShapeDtypeStruct((B, S, LANES), F32),
                   jax.ShapeDtypeStruct((B, S // tm, SUBLANES, LANES), F32),
                   jax.ShapeDtypeStruct((SUBLANES, LANES), F32)],
        scratch_shapes=[pltpu.VMEM((8, LANES), F32)],
        compiler_params=_cparams(("arbitrary", "arbitrary")),
        name="mix_out",
    )(o_rwkv, o_dsa, x, gt1, sc2, sh2, *consts)


RUN_ALIGN = SUBLANES
STAGE_ALIGN = 16
STAGE_ROWS = 1536


def _piece_waits(n_pieces):
    max_pieces = STAGE_ROWS // STAGE_ALIGN
    return [(STAGE_ALIGN << b, (n_pieces >> b) & 1 == 1) for b in range(max_pieces.bit_length())
            if STAGE_ALIGN << b <= STAGE_ROWS]


MAX_PIECES = STAGE_ROWS // STAGE_ALIGN


def _run_copies(npiece_ref, tile, copy_of):
    def piece(p, carry):
        copy_of(tile * MAX_PIECES + p).start()
        return carry

    lax.fori_loop(0, npiece_ref[tile], piece, 0)


def _dispatch_kernel(pstage_ref, prow_ref, npiece_ref, zs_ref, zn_ref, tail_ref,
                     u_ref, route_ref, xs_out, stag, zeros, sem, zsem):
    i = pl.program_id(0)
    n = pl.num_programs(0)
    tm = u_ref.shape[0]
    bm = zeros.shape[0]
    slot = i % 2

    @pl.when(i == 0)
    def _():
        zeros[...] = jnp.zeros_like(zeros)
        fills = []
        for e in range(N_EXPERTS):
            for b in range(bm.bit_length()):
                rows = RUN_ALIGN << b
                if rows > bm:
                    break
                done = (zn_ref[e] >> (b + 1)) << (b + 1)
                dst = pl.multiple_of(zs_ref[e] + done * RUN_ALIGN, RUN_ALIGN)
                fills.append(((zn_ref[e] >> b) & 1 == 1,
                              pltpu.make_async_copy(zeros.at[pl.ds(0, rows)], xs_out.at[pl.ds(dst, rows)], zsem)))
        for pred, cp in fills:
            pl.when(pred)(cp.start)

        def tail_copy(j):
            return pltpu.make_async_copy(zeros, xs_out.at[pl.ds(pl.multiple_of(j * bm, bm), bm)], zsem)

        lax.fori_loop(tail_ref[0], tail_ref[1], lambda j, c: (tail_copy(j).start(), c)[1], 0)
        for pred, cp in fills:
            pl.when(pred)(cp.wait)
        lax.fori_loop(tail_ref[0], tail_ref[1], lambda j, c: (tail_copy(j).wait(), c)[1], 0)

    route_t = jnp.transpose(route_ref[...])
    spos = [route_t[2 * TOP_K_EXPERTS + k:2 * TOP_K_EXPERTS + k + 1, :].astype(I32) for k in range(TOP_K_EXPERTS)]
    rows = lax.broadcasted_iota(I32, (STAGE_ROWS, tm), 0)
    sel = (rows == spos[0]) | (rows == spos[1]) | (rows == spos[2]) | (rows == spos[3])
    stag[slot] = jnp.dot(jnp.where(sel, 1.0, 0.0).astype(BF16), u_ref[...].astype(BF16), preferred_element_type=F32)

    def copy_of(p):
        src = pl.multiple_of(pstage_ref[p], STAGE_ALIGN)
        dst = pl.multiple_of(prow_ref[p], RUN_ALIGN)
        return pltpu.make_async_copy(stag.at[slot, pl.ds(src, STAGE_ALIGN)], xs_out.at[pl.ds(dst, STAGE_ALIGN)], sem)

    def drain(tile):
        for rows, pred in _piece_waits(npiece_ref[tile]):
            pl.when(pred)(pltpu.make_async_copy(stag.at[0, pl.ds(0, rows)], xs_out.at[pl.ds(0, rows)], sem).wait)

    @pl.when(i > 0)
    def _():
        drain(i - 1)

    _run_copies(npiece_ref, i, copy_of)

    @pl.when(i == n - 1)
    def _():
        drain(i)


def _moe_dispatch(u2, route, tables, n_rows):
    T, D = u2.shape
    tm = TM_ROUTE
    bm = BM_EXPERT
    return pl.pallas_call(
        _dispatch_kernel,
        grid_spec=pltpu.PrefetchScalarGridSpec(
            num_scalar_prefetch=len(tables),
            grid=(T // tm,),
            in_specs=[pl.BlockSpec((tm, D), lambda i, *_: (i, 0)),
                      pl.BlockSpec((tm, LANES), lambda i, *_: (i, 0))],
            out_specs=pl.BlockSpec(memory_space=pl.ANY),
            scratch_shapes=[pltpu.VMEM((2, STAGE_ROWS, D), F32), pltpu.VMEM((bm, D), F32),
                            pltpu.SemaphoreType.DMA(()), pltpu.SemaphoreType.DMA(())],
        ),
        out_shape=jax.ShapeDtypeStruct((n_rows, D), F32),
        compiler_params=_cparams(("arbitrary",)),
        name="moe_dispatch",
    )(*tables, u2, route)


GU_GROUP = 2 * LANES


def _deinterleave_perm():
    p = np.zeros((GU_GROUP, GU_GROUP), np.float32)
    l = np.arange(LANES)
    p[2 * l, l] = 1.0
    p[2 * l + 1, LANES + l] = 1.0
    return jnp.asarray(p, BF16)


def _expert_kernel(be_ref, nb_ref, valid_ref, xs_ref, wgu_hbm, bgu_ref, wd_hbm, bd_ref, perm_ref, ys_ref,
                   wg_buf, wd_buf, wp, wdb, sem):
    i = pl.program_id(0)
    bm = xs_ref.shape[0]
    e = be_ref[i]
    used = i < nb_ref[0]
    new_expert = (i == 0) | (e != be_ref[jnp.maximum(i - 1, 0)])
    n_groups = wp.shape[1] // GU_GROUP
    n_experts = wgu_hbm.shape[0]

    def fetch(ex):
        return (pltpu.make_async_copy(wgu_hbm.at[ex], wg_buf, sem.at[0]),
                pltpu.make_async_copy(wd_hbm.at[ex], wd_buf, sem.at[1]))

    @pl.when(used & new_expert)
    def _():
        @pl.when(i == 0)
        def _():
            for cp in fetch(e):
                cp.start()

        for cp in fetch(e):
            cp.wait()
        for j in range(n_groups):
            sl = slice(j * GU_GROUP, (j + 1) * GU_GROUP)
            wp[:, sl] = jnp.dot(wg_buf[:, sl].astype(BF16), perm_ref[...], preferred_element_type=F32).astype(BF16)
        wdb[...] = wd_buf[...].astype(BF16)

        @pl.when(e + 1 < n_experts)
        def _():
            for cp in fetch(e + 1):
                cp.start()

    def compute(m):
        xb = xs_ref[:m, :].astype(BF16)
        gu = jnp.dot(xb, wp[...], preferred_element_type=F32) + bgu_ref[...]
        hs = []
        for j in range(n_groups):
            gate = jnp.minimum(gu[:, j * GU_GROUP:j * GU_GROUP + LANES], SWIGLU_LIMIT)
            up = jnp.clip(gu[:, j * GU_GROUP + LANES:(j + 1) * GU_GROUP], -SWIGLU_LIMIT, SWIGLU_LIMIT)
            hs.append(((up + 1.0) * (gate * _sigmoid(gate * SWIGLU_ALPHA))).astype(BF16))
        h = jnp.concatenate(hs, axis=-1)
        ys_ref[:m, :] = jnp.dot(h, wdb[...], preferred_element_type=F32) + bd_ref[...]

    sizes = (bm, bm // 2, bm // 4)
    for m, smaller in zip(sizes, sizes[1:] + (0,)):
        @pl.when(used & (valid_ref[i] <= m) & (valid_ref[i] > smaller))
        def _(m=m):
            compute(m)
            if m < bm:
                ys_ref[m:, :] = jnp.zeros((bm - m, ys_ref.shape[1]), F32)

    @pl.when(used & (valid_ref[i] <= 0))
    def _():
        ys_ref[...] = jnp.zeros_like(ys_ref)

    @pl.when(jnp.logical_not(used))
    def _():
        ys_ref[...] = jnp.zeros_like(ys_ref)


def _moe_experts(xs, block_e, n_used, valid, w_gu, b_gu_p, w_dn, b_dn):
    n_rows, D = xs.shape
    E, _, F2 = w_gu.shape
    bm = BM_EXPERT
    n_blocks = n_rows // bm
    perm = _deinterleave_perm()
    wspec = lambda shp: pl.BlockSpec((None,) + shp, lambda i, be, nb, va: (be[i], 0, 0))
    hbm = pl.BlockSpec(memory_space=pl.ANY)
    return pl.pallas_call(
        _expert_kernel,
        grid_spec=pltpu.PrefetchScalarGridSpec(
            num_scalar_prefetch=3,
            grid=(n_blocks,),
            in_specs=[pl.BlockSpec((bm, D), lambda i, be, nb, va: (jnp.minimum(i, nb[0] - 1), 0)),
                      hbm, wspec((1, F2)), hbm, wspec((1, D)),
                      pl.BlockSpec(perm.shape, lambda i, be, nb, va: (0, 0))],
            out_specs=pl.BlockSpec((bm, D), lambda i, be, nb, va: (i, 0)),
            scratch_shapes=[pltpu.VMEM((D, F2), F32), pltpu.VMEM((F2 // 2, D), F32),
                            pltpu.VMEM((D, F2), BF16), pltpu.VMEM((F2 // 2, D), BF16),
                            pltpu.SemaphoreType.DMA((2,))],
        ),
        out_shape=jax.ShapeDtypeStruct((n_rows, D), F32),
        compiler_params=pltpu.CompilerParams(dimension_semantics=("arbitrary",), vmem_limit_bytes=VMEM_LIMIT_EXPERTS),
        name="moe_experts",
    )(block_e, n_used, valid, xs, w_gu, b_gu_p, w_dn, b_dn, perm)


def _combine_kernel(pstage_ref, prow_ref, npiece_ref, ys_ref, x1_ref, route_ref, gt_ref, g_ref,
                    b_ref, o_ref, stag, sem, *, alpha):
    i = pl.program_id(0)
    n = pl.num_programs(0)
    tm = x1_ref.shape[0]
    slot = i % 2

    def gather(tile, s):
        def copy_of(p):
            src = pl.multiple_of(prow_ref[p], RUN_ALIGN)
            dst = pl.multiple_of(pstage_ref[p], STAGE_ALIGN)
            return pltpu.make_async_copy(ys_ref.at[pl.ds(src, STAGE_ALIGN)], stag.at[s, pl.ds(dst, STAGE_ALIGN)],
                                         sem.at[s])
        _run_copies(npiece_ref, tile, copy_of)

    @pl.when(i == 0)
    def _():
        stag[...] = jnp.zeros_like(stag)
        gather(0, 0)

    @pl.when(i + 1 < n)
    def _():
        gather(i + 1, 1 - slot)

    for rows, pred in _piece_waits(npiece_ref[i]):
        pl.when(pred)(pltpu.make_async_copy(ys_ref.at[pl.ds(0, rows)], stag.at[slot, pl.ds(0, rows)], sem.at[slot]).wait)

    route = route_ref[...]
    cols = lax.broadcasted_iota(I32, (tm, STAGE_ROWS), 1)
    wgt = jnp.zeros((tm, STAGE_ROWS), F32)
    for k in range(TOP_K_EXPERTS):
        spos = route[:, 2 * TOP_K_EXPERTS + k:2 * TOP_K_EXPERTS + k + 1].astype(I32)
        wgt = wgt + jnp.where(cols == spos, route[:, TOP_K_EXPERTS + k:TOP_K_EXPERTS + k + 1], 0.0)
    ffn = jnp.dot(wgt.astype(BF16), stag[slot].astype(BF16), preferred_element_type=F32)
    o_ref[...] = _layernorm_rows(alpha * x1_ref[...] + (1.0 + gt_ref[...]) * ffn, g_ref[...], b_ref[...])


def _moe_combine(ys, tables, x1, route, gt2, ln_g, ln_b, tiles_per_batch):
    T, D = x1.shape
    tm = TM_ROUTE
    row = lambda v: v.reshape(1, -1)
    return pl.pallas_call(
        functools.partial(_combine_kernel, alpha=2.0 ** 0.25),
        grid_spec=pltpu.PrefetchScalarGridSpec(
            num_scalar_prefetch=len(tables),
            grid=(T // tm,),
            in_specs=[pl.BlockSpec(memory_space=pl.ANY),
                      pl.BlockSpec((tm, D), lambda i, *_: (i, 0)),
                      pl.BlockSpec((tm, LANES), lambda i, *_: (i, 0)),
                      pl.BlockSpec((None, 1, D), lambda i, *_: (i // tiles_per_batch, 0, 0)),
                      pl.BlockSpec((1, D), lambda i, *_: (0, 0)),
                      pl.BlockSpec((1, D), lambda i, *_: (0, 0))],
            out_specs=pl.BlockSpec((tm, D), lambda i, *_: (i, 0)),
            scratch_shapes=[pltpu.VMEM((2, STAGE_ROWS, D), F32), pltpu.SemaphoreType.DMA((2,))],
        ),
        out_shape=jax.ShapeDtypeStruct((T, D), F32),
        compiler_params=_cparams(("arbitrary",)),
        name="moe_combine",
    )(*tables, ys, x1, route, gt2, row(ln_g), row(ln_b))


def _moe_and_norm(x1, u2, route, tbl, totals, gt2, w_gu, b_gu, w_dn, b_dn, ln_g, ln_b):
    B, S, D = x1.shape
    T = B * S
    bm = BM_EXPERT
    tm = TM_ROUTE
    assert T % tm == 0 and TM_PROJ == tm and STAGE_ROWS >= tm * TOP_K_EXPERTS + N_EXPERTS * (STAGE_ALIGN - 1)
    n_tiles = T // tm
    max_rows = T * TOP_K_EXPERTS + n_tiles * N_EXPERTS * (RUN_ALIGN - 1) + N_EXPERTS * STAGE_ALIGN
    n_blocks = -(-max_rows // bm) + N_EXPERTS
    tot = totals[0, :N_EXPERTS].astype(I32)
    padded = (tot + STAGE_ALIGN + bm - 1) // bm * bm
    pad_ends = jnp.cumsum(padded)
    pad_starts = pad_ends - padded
    t3 = tbl.reshape(n_tiles, SUBLANES, LANES)[:, :, :N_EXPERTS].astype(I32)
    nrun, off, base = t3[:, 0, :], t3[:, 1, :], t3[:, 2, :]
    start = pad_starts[None, :] + base
    pieces = (nrun * RUN_ALIGN + STAGE_ALIGN - 1) // STAGE_ALIGN
    npiece = jnp.sum(pieces, axis=1)
    flat = lambda z: z.reshape(-1).astype(I32)
    ends = jnp.cumsum(pieces, axis=1)
    first = ends - pieces
    p_idx = jnp.arange(MAX_PIECES, dtype=I32)
    own = (p_idx[None, :, None] >= first[:, None, :]) & (p_idx[None, :, None] < ends[:, None, :])
    pick = lambda z: jnp.sum(jnp.where(own, z[:, None, :], 0), axis=2)
    within = (p_idx[None, :] - pick(first)) * STAGE_ALIGN
    run_tables = [flat(pick(off) + within), flat(pick(start) + within), flat(npiece)]
    zs = pad_starts + tot
    zn = (pad_ends - zs) // RUN_ALIGN
    n_used = (pad_ends[-1:] // bm).astype(I32)
    tail = jnp.concatenate([n_used, jnp.full((1,), n_blocks, I32)])
    blk_row = jnp.arange(n_blocks, dtype=I32) * bm
    block_e = jnp.minimum(jnp.sum((blk_row[:, None] >= pad_ends[None, :]).astype(I32), axis=1), N_EXPERTS - 1)
    E, F2 = b_gu.shape
    b_gu_p = b_gu.reshape(E, F2 // GU_GROUP, LANES, 2).transpose(0, 1, 3, 2).reshape(E, 1, F2)
    route2 = route.reshape(T, LANES)
    xs = _moe_dispatch(u2.reshape(T, D), route2, run_tables + [flat(zs), flat(zn), tail], n_blocks * bm)
    valid = jnp.clip(zs[block_e] - blk_row, 0, bm)
    ys = _moe_experts(xs, block_e, n_used, valid, w_gu, b_gu_p, w_dn, b_dn[:, None, :])
    out = _moe_combine(ys, run_tables, x1.reshape(T, D), route2, gt2, ln_g, ln_b, S // tm)
    return out.reshape(B, S, D)


def kernel(x, c, w_ada, b_ada, w_in, shift_mu, rwkv_w0, rwkv_w2, rwkv_a0, rwkv_a2, rwkv_g2, rwkv_k_k, rwkv_k_a, rwkv_r_k, rwkv_ln_w, rwkv_ln_b, mla_q_norm, mla_w_q_up, mla_kv_norm, mla_w_uk, mla_w_uv, idx_w_q, idx_ln_g, idx_ln_b, w_out, ln1_g, ln1_b, w_router, b_router, w_gu, b_gu, w_dn, b_dn, ln2_g, ln2_b):
    depth = w_ada.shape[0]
    assert depth == 1, "DeepNorm constants below are for a single layer"
    l = 0
    mod = _ada_mod(c, w_ada[l], b_ada[l])
    sh1, sc1, gt1, sh2, sc2, gt2 = [m[:, None, :] for m in jnp.split(mod, 6, axis=-1)]
    r, lw, k, v, kk, a, g, qabs, iq, ik, iw, cl, clt = _in_proj(
        x, sc1, sh1, w_in[l], shift_mu[l], rwkv_w0[l], rwkv_w2[l], rwkv_a0[l], rwkv_a2[l], rwkv_g2[l], rwkv_k_k[l],
        rwkv_k_a[l], mla_q_norm[l], mla_w_q_up[l], mla_w_uk[l], mla_kv_norm[l], idx_w_q[l], idx_ln_g[l], idx_ln_b[l])
    o_rwkv = _rwkv_scan(r, lw, k, v, kk, a, g, rwkv_r_k[l], rwkv_ln_w[l], rwkv_ln_b[l])
    o_dsa = _dsa_attn(iq, iw, qabs, ik, cl, clt, mla_w_uv[l])
    x1, u2, route, tbl, totals = _mix_out(o_rwkv, o_dsa, x, gt1, sc2, sh2, w_out[l], ln1_g[l], ln1_b[l],
                                          w_router[l], b_router[l])
    return _moe_and_norm(x1, u2, route, tbl, totals, gt2, w_gu[l], b_gu[l], w_dn[l], b_dn[l], ln2_g[l], ln2_b[l])
```

```python
import functools
import math

import jax
import jax.numpy as jnp
import numpy as np
from jax import lax
from jax.experimental import pallas as pl
from jax.experimental.pallas import tpu as pltpu

F32 = jnp.float32
BF16 = jnp.bfloat16
I32 = jnp.int32

RWKV_HEAD = 64
N_RWKV_HEADS = 8
D_RWKV = RWKV_HEAD * N_RWKV_HEADS
RWKV_GN_EPS = 64e-5
ATT_HEAD = 64
N_ATT_HEADS = 8
D_ATT = ATT_HEAD * N_ATT_HEADS
KV_LORA = 128
IDX_HEADS = 8
IDX_DIM = 64
TOPK_MAX = 256
N_EXPERTS = 32
TOP_K_EXPERTS = 4
SWIGLU_LIMIT = 7.0
SWIGLU_ALPHA = 1.702
NEG_BIG = -1e30
LOG2E = 1.4426950408889634
INT_MIN = -(2 ** 31)

LANES = 128
SUBLANES = 8
VMEM_LIMIT = 48 * 1024 * 1024
VMEM_LIMIT_EXPERTS = 58 * 1024 * 1024

TM_IN = 512
TM_PROJ = 256
L_CHUNK = 64
CHUNKS_PER_STEP = 4
TQ = 256
KEY_CHUNK = 256
SUM_ROWS = 128
TM_ROUTE = 256
BM_EXPERT = 1024

_SEG = (("r", 512, 512), ("k", 512, 512), ("v", 512, 512), ("w", 64, 128), ("a", 64, 128), ("g", 128, 128),
        ("q", 256, 256), ("kv", 128, 128), ("ik", 64, 128), ("iw", 8, 128))
N_SHIFT_P = 512 * 3 + 128 * 3
N_IN_P = sum(s[2] for s in _SEG)


def _cparams(sem):
    return pltpu.CompilerParams(dimension_semantics=sem, vmem_limit_bytes=VMEM_LIMIT)


def _bdot(a, b):
    return jnp.dot(a.astype(BF16), b.astype(BF16), preferred_element_type=F32)


def _split2(a):
    hi = a.astype(BF16)
    lo = (a - hi.astype(F32)).astype(BF16)
    return hi, lo


def _split3(a):
    hi = a.astype(BF16)
    r1 = a - hi.astype(F32)
    mid = r1.astype(BF16)
    lo = (r1 - mid.astype(F32)).astype(BF16)
    return hi, mid, lo


def _dot3(a, b, dims=(((1,), (0,)), ((), ()))):
    ah, al = _split2(a)
    bh, bl = _split2(b)
    d = functools.partial(lax.dot_general, dimension_numbers=dims, preferred_element_type=F32)
    return d(ah, bh) + (d(ah, bl) + d(al, bh))


def _dot_exact_rhs(a, b_exact, nsplit=3):
    parts = _split3(a) if nsplit == 3 else _split2(a)
    acc = None
    for p in parts[::-1]:
        t = jnp.dot(p, b_exact, preferred_element_type=F32)
        acc = t if acc is None else acc + t
    return acc


def _dot_exact_lhs(a_exact, b, nsplit=3):
    parts = _split3(b) if nsplit == 3 else _split2(b)
    acc = None
    for p in parts[::-1]:
        t = jnp.dot(a_exact, p, preferred_element_type=F32)
        acc = t if acc is None else acc + t
    return acc


def _sigmoid(x):
    return 1.0 / (1.0 + jnp.exp(-x))


def _softplus(x):
    return jnp.maximum(x, 0.0) + jnp.log(1.0 + jnp.exp(-jnp.abs(x)))


def _ada_kernel(c_ref, w_ref, b_ref, o_ref):
    c = c_ref[...]
    o_ref[...] = _dot3(c * _sigmoid(c), w_ref[...]) + b_ref[...]


def _ada_mod(c, w_ada, b_ada):
    B, D = c.shape
    N = w_ada.shape[1]
    tn = 1024
    return pl.pallas_call(
        _ada_kernel,
        grid=(N // tn,),
        in_specs=[pl.BlockSpec((B, D), lambda j: (0, 0)),
                  pl.BlockSpec((D, tn), lambda j: (0, j)),
                  pl.BlockSpec((1, tn), lambda j: (0, j))],
        out_specs=pl.BlockSpec((B, tn), lambda j: (0, j)),
        out_shape=jax.ShapeDtypeStruct((B, N), F32),
        compiler_params=_cparams(("arbitrary",)),
        name="ada_mod",
    )(c, w_ada, b_ada.reshape(1, N))


def _in_proj_kernel(x_ref, sc_ref, sh_ref, win_ref, mu_ref, w0_ref, w2_ref, a0_ref, a2_ref, g2_ref, kk_ref, ka_ref,
                    ones_ref, qn_ref, wq_ref, wuk_ref, kvn_ref, wiq_ref, ig_ref, ib_ref,
                    r_o, lw_o, k_o, v_o, kkn_o, a_o, g_o, qabs_o, iq_o, ik_o, iw_o, cl_o, clt_o, carry):
    i = pl.program_id(1)
    tm = x_ref.shape[0]

    @pl.when(i == 0)
    def _():
        carry[...] = jnp.zeros_like(carry)

    u = x_ref[...] * (1.0 + sc_ref[...]) + sh_ref[...]
    p = _bdot(u, win_ref[...])
    ps = p[:, :N_SHIFT_P]
    rows = lax.broadcasted_iota(I32, (tm, 1), 0)
    prev = jnp.where(rows == 0, carry[0:1, :], pltpu.roll(ps, 1, 0))
    carry[0:1, :] = ps[tm - 1:tm, :]
    ps = ps + mu_ref[...] * (prev - ps)

    pr, pk, pv = ps[:, 0:512], ps[:, 512:1024], ps[:, 1024:1536]
    pw, pa, pg = ps[:, 1536:1664], ps[:, 1664:1792], ps[:, 1792:1920]
    w_log = -_softplus(-(w0_ref[...] + _dot3(jnp.tanh(pw), w2_ref[...]))) - 0.5
    lw_o[...] = -jnp.exp(w_log)
    a = _sigmoid(a0_ref[...] + _dot3(pa, a2_ref[...]))
    g_o[...] = _dot3(_sigmoid(pg), g2_ref[...])
    kk = pk * kk_ref[...]
    ssq = _dot_exact_rhs(kk * kk, ones_ref[...], nsplit=2)
    kkn_o[...] = kk / jnp.maximum(jnp.sqrt(ssq), 1e-12)
    k_o[...] = pk * (1.0 + (a - 1.0) * ka_ref[...])
    r_o[...] = pr
    v_o[...] = pv
    a_o[...] = a

    pq, pkv = p[:, 1920:2176], p[:, 2176:2304]
    pik, piw = p[:, 2304:2432], p[:, 2432:2560]
    q_lat = pq * lax.rsqrt(jnp.mean(pq * pq, axis=-1, keepdims=True) + 1e-6) * qn_ref[...]
    q = _bdot(q_lat, wq_ref[...])
    qabs_o[...] = (_bdot(q, wuk_ref[...]) * (ATT_HEAD ** -0.5 * LOG2E)).astype(BF16)
    c_lat = pkv * lax.rsqrt(jnp.mean(pkv * pkv, axis=-1, keepdims=True) + 1e-6) * kvn_ref[...]
    spos = i * tm + rows
    s_hi = (spos >> 7).astype(F32)
    s_lo = (spos & (LANES - 1)).astype(F32)
    lane_t = lax.broadcasted_iota(I32, (1, LANES), 1)
    extra = jnp.where(lane_t == 0, 1.0, jnp.where((lane_t == 1) | (lane_t == 2), s_hi,
                                                  jnp.where((lane_t == 3) | (lane_t == 4), s_lo, 0.0)))
    c_aug = jnp.concatenate([c_lat, extra], axis=-1)
    cl_o[...] = c_aug.astype(BF16)
    clt_o[...] = jnp.transpose(c_aug).astype(BF16)
    iq_o[...] = (_bdot(q_lat, wiq_ref[...]) * (IDX_DIM ** -0.5)).astype(BF16)
    lane = lax.broadcasted_iota(I32, (1, LANES), 1)
    valid = lane < IDX_DIM
    mu = jnp.sum(pik, axis=-1, keepdims=True) * (1.0 / IDX_DIM)
    dlt = jnp.where(valid, pik - mu, 0.0)
    var = jnp.sum(dlt * dlt, axis=-1, keepdims=True) * (1.0 / IDX_DIM)
    ik_o[...] = jnp.where(valid, dlt * lax.rsqrt(var + 1e-5) * ig_ref[...] + ib_ref[...], 0.0).astype(BF16)
    iw_o[...] = piw * (IDX_HEADS ** -0.5)


def _pad_cols(w, widths):
    parts, o = [], 0
    for true, padded in widths:
        seg = w[..., o:o + true]
        if padded > true:
            seg = jnp.pad(seg, [(0, 0)] * (w.ndim - 1) + [(0, padded - true)])
        parts.append(seg)
        o += true
    return jnp.concatenate(parts, axis=-1)


def _pad_rows(w, rows):
    return jnp.pad(w, ((0, rows - w.shape[0]), (0, 0)))


def _block_diag(blocks):
    H, a, b = blocks.shape
    eye = jnp.eye(H, dtype=blocks.dtype)
    return (eye[:, None, :, None] * blocks[:, :, None, :]).reshape(H * a, H * b)


def _head_ones(n, head):
    idx = np.arange(n) // head
    return jnp.asarray(idx[:, None] == idx[None, :], BF16)


def _in_proj(x, sc1, sh1, w_in, shift_mu, w0, w2, a0, a2, g2, k_k, k_a, q_norm, w_q_up, w_uk, kv_norm, idx_w_q,
             idx_ln_g, idx_ln_b):
    B, S, D = x.shape
    tm = min(TM_IN, S)
    widths = tuple((s[1], s[2]) for s in _SEG)
    win_p = _pad_cols(w_in, widths).astype(BF16)
    mu_p = _pad_cols(shift_mu.reshape(1, -1), widths[:6])
    w2_p = _pad_rows(w2, LANES)
    a2_p = _pad_rows(a2, LANES)
    wuk_bd = _block_diag(w_uk).astype(BF16)
    wiq_p = _pad_cols(idx_w_q, ((IDX_DIM, LANES),) * IDX_HEADS).astype(BF16)
    ig_p = _pad_cols(idx_ln_g.reshape(1, -1), ((IDX_DIM, LANES),))
    ib_p = _pad_cols(idx_ln_b.reshape(1, -1), ((IDX_DIM, LANES),))
    row = lambda v: v.reshape(1, -1)
    tok = lambda n: pl.BlockSpec((None, tm, n), lambda b, i: (b, i, 0))
    mod = pl.BlockSpec((None, 1, D), lambda b, i: (b, 0, 0))
    full = lambda a: pl.BlockSpec(a.shape, lambda b, i: (0,) * a.ndim)
    consts = [win_p, mu_p, row(w0), w2_p, row(a0), a2_p, g2, row(k_k), row(k_a), _head_ones(D_RWKV, RWKV_HEAD),
              row(q_norm), w_q_up.astype(BF16), wuk_bd, row(kv_norm), wiq_p, ig_p, ib_p]
    outs = [(D_RWKV, F32)] * 7 + [(N_ATT_HEADS * KV_LORA, BF16), (IDX_HEADS * LANES, BF16), (LANES, BF16),
                                  (LANES, F32), (KV_LORA + LANES, BF16)]
    return pl.pallas_call(
        _in_proj_kernel,
        grid=(B, S // tm),
        in_specs=[tok(D), mod, mod] + [full(a) for a in consts],
        out_specs=[tok(n) for n, _ in outs] + [pl.BlockSpec((None, KV_LORA + LANES, tm), lambda b, i: (b, 0, i))],
        out_shape=[jax.ShapeDtypeStruct((B, S, n), dt) for n, dt in outs]
        + [jax.ShapeDtypeStruct((B, KV_LORA + LANES, S), BF16)],
        scratch_shapes=[pltpu.VMEM((8, N_SHIFT_P), F32)],
        compiler_params=_cparams(("arbitrary", "arbitrary")),
        name="in_proj",
    )(x, sc1, sh1, *consts)


def _rwkv_kernel(r_ref, lw_ref, k_ref, v_ref, kk_ref, a_ref, g_ref, rk_ref, lnw_ref, lnb_ref, tri_ref, ones_ref,
                 o_ref, state, *, L):
    c = pl.program_id(1)
    nc = r_ref.shape[0] // L

    @pl.when(c == 0)
    def _():
        state[...] = jnp.zeros_like(state)

    r, lw, k, v, kk, a = r_ref[...], lw_ref[...], k_ref[...], v_ref[...], kk_ref[...], a_ref[...]
    cum = _dot_exact_lhs(tri_ref[...], lw)
    rows = [slice(ci * L, (ci + 1) * L) for ci in range(nc)]
    last = [cum[(ci + 1) * L - 1:(ci + 1) * L, :] for ci in range(nc)]
    cum_last = jnp.concatenate([jnp.broadcast_to(z, (L, z.shape[1])) for z in last], axis=0)
    w_incl = jnp.exp(cum)
    w_inv = jnp.exp(-cum)
    w_rel = jnp.exp(cum_last - cum)
    w_last = [jnp.exp(z) for z in last]
    bvec = kk * a
    at = -kk * jnp.exp(cum - lw)
    rt = r * w_incl
    bt = bvec * w_inv
    kt = k * w_inv
    bh = bvec * w_rel
    kh = k * w_rel
    ti = lax.broadcasted_iota(I32, (L, L), 0)
    tj = lax.broadcasted_iota(I32, (L, L), 1)
    strict = tj < ti
    incl = tj <= ti
    eye = lax.broadcasted_iota(I32, (RWKV_HEAD, RWKV_HEAD), 0) == lax.broadcasted_iota(I32, (RWKV_HEAD, RWKV_HEAD), 1)
    NT = (((1,), (1,)), ((), ()))
    TN = (((0,), (0,)), ((), ()))
    heads = range(N_RWKV_HEADS)
    sls = [slice(h * RWKV_HEAD, (h + 1) * RWKV_HEAD) for h in heads]
    units = [(ci, h) for ci in range(nc) for h in heads]
    idx = range(len(units))
    mm = lambda x, y, dims=(((1,), (0,)), ((), ())): lax.dot_general(
        x.astype(BF16), y.astype(BF16), dims, preferred_element_type=F32)
    at_b, rt_b, bt_b, kt_b, bh_b, kh_b, v_b = [z.astype(BF16) for z in (at, rt, bt, kt, bh, kh, v)]
    cut = lambda z, u: z[rows[u[0]], sls[u[1]]]
    vh = [cut(v_b, u) for u in units]
    ath = [cut(at_b, u) for u in units]
    ar = [jnp.concatenate([cut(at_b, u), cut(rt_b, u)], axis=0) for u in units]
    g_b = [mm(ar[i], cut(bt_b, units[i]), NT) for i in idx]
    g_k = [mm(ar[i], cut(kt_b, units[i]), NT) for i in idx]
    n_ab = [jnp.where(strict, g_b[i][:L], 0.0) for i in idx]
    a_ak = [jnp.where(strict, g_k[i][:L], 0.0) for i in idx]
    a_rb = [jnp.where(incl, g_b[i][L:], 0.0).astype(BF16) for i in idx]
    a_rk = [jnp.where(incl, g_k[i][L:], 0.0) for i in idx]
    akv = [mm(a_ak[i], vh[i]) for i in idx]
    eye_l = jnp.where(ti == tj, 1.0, 0.0)
    tinv = [eye_l + n_ab[i] for i in idx]
    pw = n_ab
    for _ in range(int(math.log2(L)) - 1):
        pw = [mm(pw[i], pw[i]) for i in idx]
        tinv = [tinv[i] + mm(pw[i], tinv[i]) for i in idx]
    tinv = [t.astype(BF16) for t in tinv]
    a_t = [mm(tinv[i], ath[i]).astype(BF16) for i in idx]
    y = [mm(tinv[i], akv[i]).astype(BF16) for i in idx]
    m_c = [jnp.where(eye, w_last[units[i][0]][:, sls[units[i][1]]], 0.0) + mm(a_t[i], cut(bh_b, units[i]), TN)
           for i in idx]
    c_c = [mm(y[i], cut(bh_b, units[i]), TN) + mm(vh[i], cut(kh_b, units[i]), TN) for i in idx]
    q_c = [cut(rt, units[i]) + mm(a_rb[i], a_t[i]) for i in idx]
    o_loc = [mm(a_rb[i], y[i]) + mm(a_rk[i], vh[i]) for i in idx]
    s = [state[h] for h in heads]
    for ci in range(nc):
        for h in heads:
            i = ci * N_RWKV_HEADS + h
            o = o_loc[i] + mm(q_c[i], s[h], NT)
            s[h] = mm(s[h], m_c[i]) + c_c[i]
            mu = jnp.mean(o, axis=-1, keepdims=True)
            d = o - mu
            var = jnp.mean(d * d, axis=-1, keepdims=True)
            o_ref[rows[ci], sls[h]] = d * lax.rsqrt(var + RWKV_GN_EPS)
    for h in heads:
        state[h] = s[h]
    bonus = _dot_exact_rhs(r * k * rk_ref[...], ones_ref[...], nsplit=3) * v
    o_ref[...] = (o_ref[...] * lnw_ref[...] + lnb_ref[...] + bonus) * g_ref[...]


def _rwkv_scan(r, lw, k, v, kk, a, g, r_k, ln_w, ln_b):
    B, S, DR = r.shape
    L = min(L_CHUNK, S)
    lb = min(L * CHUNKS_PER_STEP, S)
    tri = jnp.asarray(np.kron(np.eye(lb // L), np.tril(np.ones((L, L)))), BF16)
    row = lambda z: z.reshape(1, -1)
    tok = pl.BlockSpec((None, lb, DR), lambda b, c: (b, c, 0))
    full = lambda z: pl.BlockSpec(z.shape, lambda b, c: (0,) * z.ndim)
    consts = [row(r_k), row(ln_w), row(ln_b), tri, _head_ones(DR, RWKV_HEAD)]
    return pl.pallas_call(
        functools.partial(_rwkv_kernel, L=L),
        grid=(B, S // lb),
        in_specs=[tok] * 7 + [full(z) for z in consts],
        out_specs=tok,
        out_shape=jax.ShapeDtypeStruct((B, S, DR), F32),
        scratch_shapes=[pltpu.VMEM((N_RWKV_HEADS, RWKV_HEAD, RWKV_HEAD), F32)],
        compiler_params=_cparams(("arbitrary", "arbitrary")),
        name="rwkv_scan",
    )(r, lw, k, v, kk, a, g, *consts)


def _alibi_cols():
    slope = np.asarray([2.0 ** (-8.0 * (h + 1) / N_ATT_HEADS) * LOG2E for h in range(N_ATT_HEADS)], np.float32)
    c_hi = slope.astype(BF16).astype(np.float32)
    c_lo = (slope - c_hi).astype(BF16).astype(np.float32)
    t = np.zeros((N_ATT_HEADS, LANES), np.float32)
    t[:, 1], t[:, 2], t[:, 3], t[:, 4] = LANES * c_hi, LANES * c_lo, c_hi, c_lo
    return jnp.asarray(t.astype(BF16))


def _dsa_kernel(iq_ref, iw_ref, qa_ref, ik_ref, ca_ref, cat_ref, wuv_ref, tril_ref, acol_ref, o_ref, key_ref,
                bias_ref, lg_ref, p_ref, *, topk, q_off, select):
    qi = pl.program_id(1) + q_off
    tq = qa_ref.shape[0]
    sk = ca_ref.shape[0]
    kc = min(KEY_CHUNK, sk)
    chunks = [slice(c * kc, (c + 1) * kc) for c in range(sk // kc)]
    tpos = qi * tq + lax.broadcasted_iota(I32, (1, tq), 1)
    srow = lax.broadcasted_iota(I32, (kc, tq), 0)
    NT = (((1,), (1,)), ((), ()))

    if not select:
        for c, cs in enumerate(chunks):
            bias_ref[cs, :] = jnp.where(srow + c * kc <= tpos, 0.0, NEG_BIG)
    else:
        iw_t = jnp.transpose(iw_ref[...])
        for c, cs in enumerate(chunks):
            ikc = ik_ref[cs, :]
            score = jnp.zeros((kc, tq), F32)
            for h in range(IDX_HEADS):
                s = lax.dot_general(ikc, iq_ref[:, h * LANES:(h + 1) * LANES], NT, preferred_element_type=F32)
                score = score + iw_t[h:h + 1, :] * jnp.maximum(s, 0.0)
            bits = pltpu.bitcast(score + 0.0, I32)
            key = bits ^ ((bits >> 31) & 0x7FFFFFFF)
            key_ref[cs, :] = jnp.where(srow + c * kc <= tpos, key, INT_MIN)
        kcount = jnp.minimum(topk, tpos + 1).astype(F32)

        def sum_keys(x):
            part = jnp.sum(x.reshape(x.shape[0] // SUM_ROWS, SUM_ROWS, x.shape[1]), axis=0)
            return jnp.sum(part, axis=0, keepdims=True)

        def count_ge(cand):
            return sum_keys(jnp.where(key_ref[...] >= cand, 1.0, 0.0))

        thr = jnp.where(count_ge(jnp.zeros((1, tq), I32)) >= kcount, 0, INT_MIN).astype(I32)

        def bit_step(i, thr):
            cand = thr | (1 << (30 - i))
            return jnp.where(count_ge(cand) >= kcount, cand, thr)

        thr = lax.fori_loop(0, 31, bit_step, thr)

        need = kcount - sum_keys(jnp.where(key_ref[...] > thr, 1.0, 0.0))
        carry = jnp.zeros((1, tq), F32)
        for j in range(sk // LANES):
            sl = slice(j * LANES, (j + 1) * LANES)
            key = key_ref[sl, :]
            eq = key == thr
            e = jnp.where(eq, 1.0, 0.0)
            before = jnp.dot(tril_ref[...], e.astype(BF16), preferred_element_type=F32) + carry
            sel = (key > thr) | (eq & (before < need))
            bias_ref[sl, :] = jnp.where(sel, 0.0, NEG_BIG)
            carry = carry + jnp.sum(e, axis=0, keepdims=True)

    outs = []
    for h in range(N_ATT_HEADS):
        q_aug = jnp.concatenate([qa_ref[:, h * KV_LORA:(h + 1) * KV_LORA],
                                 jnp.broadcast_to(acol_ref[h:h + 1, :], (tq, LANES))], axis=-1)
        lg_all = lax.dot_general(ca_ref[...], q_aug, NT, preferred_element_type=F32)
        m = jnp.full((1, tq), -jnp.inf, F32)
        for cs in chunks:
            lg = lg_all[cs, :] + bias_ref[cs, :]
            lg_ref[cs, :] = lg
            m = jnp.maximum(m, jnp.max(lg, axis=0, keepdims=True))
        for cs in chunks:
            p_ref[cs, :] = jnp.exp2(lg_ref[cs, :] - m).astype(BF16)
        pv = jnp.dot(cat_ref[...], p_ref[...], preferred_element_type=F32)
        outs.append((pv[:KV_LORA, :] / pv[KV_LORA:KV_LORA + 1, :]).astype(BF16))
    o_lat_t = jnp.concatenate(outs, axis=0)
    o_ref[...] = lax.dot_general(o_lat_t, wuv_ref[...], (((0,), (0,)), ((), ())), preferred_element_type=F32)


def _dsa_attn(iq, iw, qabs, ik, ca, cat, w_uv):
    B, S, _ = iq.shape
    tq = min(TQ, S)
    topk = min(TOPK_MAX, S // 4)
    wuv_bd = _block_diag(w_uv).astype(BF16)
    tril = jnp.asarray(np.tril(np.ones((LANES, LANES)), -1), BF16)
    acol = _alibi_cols()
    full = lambda z: pl.BlockSpec(z.shape, lambda b, i: (0,) * z.ndim)
    nq = 1
    outs = []
    for q_off in range(0, S // tq, nq):
        sk = (q_off + nq) * tq
        tok = lambda n, q_off=q_off: pl.BlockSpec((None, tq, n), lambda b, i: (b, i + q_off, 0))
        seq = lambda n, sk=sk: pl.BlockSpec((None, sk, n), lambda b, i: (b, 0, 0))
        outs.append(pl.pallas_call(
            functools.partial(_dsa_kernel, topk=topk, q_off=q_off, select=sk > topk),
            grid=(B, nq),
            in_specs=[tok(IDX_HEADS * LANES), tok(LANES), tok(N_ATT_HEADS * KV_LORA), seq(LANES), seq(KV_LORA + LANES),
                      pl.BlockSpec((None, KV_LORA + LANES, sk), lambda b, i: (b, 0, 0)),
                      full(wuv_bd), full(tril), full(acol)],
            out_specs=pl.BlockSpec((None, tq, D_ATT), lambda b, i: (b, i, 0)),
            out_shape=jax.ShapeDtypeStruct((B, nq * tq, D_ATT), F32),
            scratch_shapes=[pltpu.VMEM((sk, tq), I32), pltpu.VMEM((sk, tq), F32), pltpu.VMEM((sk, tq), F32),
                            pltpu.VMEM((sk, tq), BF16)],
            compiler_params=_cparams(("arbitrary", "arbitrary")),
            name=f"dsa_attn_k{sk}",
        )(iq, iw, qabs, ik, ca, cat, wuv_bd, tril, acol))
    return jnp.concatenate(outs, axis=1)


def _layernorm_rows(y, g, b):
    mu = jnp.mean(y, axis=-1, keepdims=True)
    d = y - mu
    var = jnp.mean(d * d, axis=-1, keepdims=True)
    return d * lax.rsqrt(var + 1e-5) * g + b


def _mix_kernel(orw_ref, ods_ref, x_ref, gt_ref, sc_ref, sh_ref, wtop_ref, wbot_ref, g_ref, b_ref, wr_ref, br_ref,
                tril_ref, triu_ref, x1_o, u2_o, route_o, tbl_o, cnt_o, carry, *, alpha):
    first = (pl.program_id(0) == 0) & (pl.program_id(1) == 0)
    tm = x_ref.shape[0]

    @pl.when(first)
    def _():
        carry[...] = jnp.zeros_like(carry)

    mix = _bdot(orw_ref[...], wtop_ref[...]) + _bdot(ods_ref[...], wbot_ref[...])
    x1 = _layernorm_rows(alpha * x_ref[...] + (1.0 + gt_ref[...]) * mix, g_ref[...], b_ref[...])
    x1_o[...] = x1
    u2 = x1 * (1.0 + sc_ref[...]) + sh_ref[...]
    u2_o[...] = u2

    lg = _dot3(u2, wr_ref[...]) + br_ref[...]
    lane = lax.broadcasted_iota(I32, (tm, LANES), 1)
    lane_f = lane.astype(F32)
    idxs, vals = [], []
    for _ in range(TOP_K_EXPERTS):
        m = jnp.max(lg, axis=-1, keepdims=True)
        idx = jnp.min(jnp.where(lg == m, lane_f, float(LANES)), axis=-1, keepdims=True).astype(I32)
        idxs.append(idx)
        vals.append(m)
        lg = jnp.where(lane == idx, -jnp.inf, lg)
    es = [jnp.exp(v - vals[0]) for v in vals]
    den = es[0] + es[1] + es[2] + es[3]
    hot = jnp.zeros((tm, LANES), F32)
    for idx in idxs:
        hot = hot + jnp.where(lane == idx, 1.0, 0.0)
    before = jnp.dot(tril_ref[...], hot.astype(BF16), preferred_element_type=F32)
    cnt = jnp.sum(hot, axis=0, keepdims=True)
    n_run = jnp.floor((cnt + (RUN_ALIGN - 1)) * (1.0 / RUN_ALIGN))
    n_stage = jnp.floor((n_run * RUN_ALIGN + (STAGE_ALIGN - 1)) * (1.0 / STAGE_ALIGN))
    off = jnp.dot(jnp.broadcast_to(n_stage, (SUBLANES, LANES)).astype(BF16), triu_ref[...],
                  preferred_element_type=F32)[0:1, :] * STAGE_ALIGN
    where_in_stage = off + before
    route = jnp.zeros((tm, LANES), F32)
    for k in range(TOP_K_EXPERTS):
        spos = jnp.sum(jnp.where(lane == idxs[k], where_in_stage, 0.0), axis=-1, keepdims=True)
        route = jnp.where(lane == k, idxs[k].astype(F32), route)
        route = jnp.where(lane == TOP_K_EXPERTS + k, es[k] / den, route)
        route = jnp.where(lane == 2 * TOP_K_EXPERTS + k, spos, route)
    route_o[...] = route
    sub = lax.broadcasted_iota(I32, (SUBLANES, LANES), 0)
    tbl_o[...] = jnp.where(sub == 0, n_run, jnp.where(sub == 1, off, jnp.where(sub == 2, carry[0:1, :], 0.0)))
    carry[0:1, :] = carry[0:1, :] + n_run * RUN_ALIGN
    cnt_o[...] = carry[...]


def _mix_out(o_rwkv, o_dsa, x, gt1, sc2, sh2, w_out, ln_g, ln_b, w_router, b_router):
    B, S, D = x.shape
    tm = min(TM_PROJ, S)
    alpha = 2.0 ** 0.25
    wtop = w_out[:D_RWKV].astype(BF16)
    wbot = w_out[D_RWKV:].astype(BF16)
    wr_p = jnp.pad(w_router, ((0, 0), (0, LANES - N_EXPERTS)))
    br_p = jnp.pad(b_router.reshape(1, -1), ((0, 0), (0, LANES - N_EXPERTS)), constant_values=NEG_BIG)
    tril = jnp.asarray(np.tril(np.ones((tm, tm)), -1), BF16)
    triu = jnp.asarray(np.triu(np.ones((LANES, LANES)), 1), BF16)
    row = lambda v: v.reshape(1, -1)
    tok = lambda n: pl.BlockSpec((None, tm, n), lambda b, i: (b, i, 0))
    mod = pl.BlockSpec((None, 1, D), lambda b, i: (b, 0, 0))
    full = lambda a: pl.BlockSpec(a.shape, lambda b, i: (0,) * a.ndim)
    consts = [wtop, wbot, row(ln_g), row(ln_b), wr_p, br_p, tril, triu]
    return pl.pallas_call(
        functools.partial(_mix_kernel, alpha=alpha),
        grid=(B, S // tm),
        in_specs=[tok(D_RWKV), tok(D_ATT), tok(D), mod, mod, mod] + [full(a) for a in consts],
        out_specs=[tok(D), tok(D), tok(LANES), pl.BlockSpec((None, None, SUBLANES, LANES), lambda b, i: (b, i, 0, 0)),
                   pl.BlockSpec((SUBLANES, LANES), lambda b, i: (0, 0))],
        out_shape=[jax.ShapeDtypeStruct((B, S, D), F32), jax.ShapeDtypeStruct((B, S, D), F32),
                   jax.ShapeDtypeStruct((B, S, LANES), F32),
                   jax.ShapeDtypeStruct((B, S // tm, SUBLANES, LANES), F32),
                   jax.ShapeDtypeStruct((SUBLANES, LANES), F32)],
        scratch_shapes=[pltpu.VMEM((8, LANES), F32)],
        compiler_params=_cparams(("arbitrary", "arbitrary")),
        name="mix_out",
    )(o_rwkv, o_dsa, x, gt1, sc2, sh2, *consts)


RUN_ALIGN = SUBLANES
STAGE_ALIGN = 16
STAGE_ROWS = 1536


MAX_PIECES = STAGE_ROWS // STAGE_ALIGN


def _piece_waits(n_pieces):
    return [(STAGE_ALIGN << b, (n_pieces >> b) & 1 == 1) for b in range(MAX_PIECES.bit_length())
            if STAGE_ALIGN << b <= STAGE_ROWS]


def _run_copies(npiece_ref, tile, copy_of):
    base = tile * MAX_PIECES
    n_pieces = npiece_ref[tile]

    def pair(j, carry):
        copy_of(base + 2 * j).start()
        copy_of(base + 2 * j + 1).start()
        return carry

    lax.fori_loop(0, n_pieces // 2, pair, 0)

    @pl.when(n_pieces % 2 == 1)
    def _():
        copy_of(base + n_pieces - 1).start()


def _dispatch_kernel(pstage_ref, prow_ref, npiece_ref, zs_ref, zn_ref, tail_ref,
                     u_ref, route_ref, xs_out, stag, zeros, sem, zsem):
    i = pl.program_id(0)
    n = pl.num_programs(0)
    tm = u_ref.shape[0]
    bm = zeros.shape[0]
    slot = i % 2

    @pl.when(i == 0)
    def _():
        zeros[...] = jnp.zeros_like(zeros)
        fills = []
        for e in range(N_EXPERTS):
            for b in range(bm.bit_length()):
                rows = RUN_ALIGN << b
                if rows > bm:
                    break
                done = (zn_ref[e] >> (b + 1)) << (b + 1)
                dst = pl.multiple_of(zs_ref[e] + done * RUN_ALIGN, RUN_ALIGN)
                fills.append(((zn_ref[e] >> b) & 1 == 1,
                              pltpu.make_async_copy(zeros.at[pl.ds(0, rows)], xs_out.at[pl.ds(dst, rows)], zsem)))
        for pred, cp in fills:
            pl.when(pred)(cp.start)

        def tail_copy(j):
            return pltpu.make_async_copy(zeros, xs_out.at[pl.ds(pl.multiple_of(j * bm, bm), bm)], zsem)

        lax.fori_loop(tail_ref[0], tail_ref[1], lambda j, c: (tail_copy(j).start(), c)[1], 0)
        for pred, cp in fills:
            pl.when(pred)(cp.wait)
        lax.fori_loop(tail_ref[0], tail_ref[1], lambda j, c: (tail_copy(j).wait(), c)[1], 0)

    route_t = jnp.transpose(route_ref[...])
    spos = [route_t[2 * TOP_K_EXPERTS + k:2 * TOP_K_EXPERTS + k + 1, :].astype(I32) for k in range(TOP_K_EXPERTS)]
    rows = lax.broadcasted_iota(I32, (STAGE_ROWS, tm), 0)
    sel = (rows == spos[0]) | (rows == spos[1]) | (rows == spos[2]) | (rows == spos[3])
    stag[slot] = jnp.dot(jnp.where(sel, 1.0, 0.0).astype(BF16), u_ref[...].astype(BF16), preferred_element_type=F32)

    def copy_of(p):
        src = pl.multiple_of(pstage_ref[p], STAGE_ALIGN)
        dst = pl.multiple_of(prow_ref[p], RUN_ALIGN)
        return pltpu.make_async_copy(stag.at[slot, pl.ds(src, STAGE_ALIGN)], xs_out.at[pl.ds(dst, STAGE_ALIGN)], sem)

    def drain(tile):
        for rows, pred in _piece_waits(npiece_ref[tile]):
            pl.when(pred)(pltpu.make_async_copy(stag.at[0, pl.ds(0, rows)], xs_out.at[pl.ds(0, rows)], sem).wait)

    @pl.when(i > 0)
    def _():
        drain(i - 1)

    _run_copies(npiece_ref, i, copy_of)

    @pl.when(i == n - 1)
    def _():
        drain(i)


def _moe_dispatch(u2, route, tables, n_rows):
    T, D = u2.shape
    tm = TM_ROUTE
    bm = BM_EXPERT
    return pl.pallas_call(
        _dispatch_kernel,
        grid_spec=pltpu.PrefetchScalarGridSpec(
            num_scalar_prefetch=len(tables),
            grid=(T // tm,),
            in_specs=[pl.BlockSpec((tm, D), lambda i, *_: (i, 0)),
                      pl.BlockSpec((tm, LANES), lambda i, *_: (i, 0))],
            out_specs=pl.BlockSpec(memory_space=pl.ANY),
            scratch_shapes=[pltpu.VMEM((2, STAGE_ROWS, D), F32), pltpu.VMEM((bm, D), F32),
                            pltpu.SemaphoreType.DMA(()), pltpu.SemaphoreType.DMA(())],
        ),
        out_shape=jax.ShapeDtypeStruct((n_rows, D), F32),
        compiler_params=_cparams(("arbitrary",)),
        name="moe_dispatch",
    )(*tables, u2, route)


GU_GROUP = 2 * LANES


def _deinterleave_perm():
    p = np.zeros((GU_GROUP, GU_GROUP), np.float32)
    l = np.arange(LANES)
    p[2 * l, l] = 1.0
    p[2 * l + 1, LANES + l] = 1.0
    return jnp.asarray(p, BF16)


def _expert_kernel(be_ref, nb_ref, valid_ref, xs_ref, wgu_hbm, bgu_ref, wd_hbm, bd_ref, perm_ref, ys_ref,
                   wg_buf, wd_buf, wp, wdb, sem):
    i = pl.program_id(0)
    bm = xs_ref.shape[0]
    e = be_ref[i]
    used = i < nb_ref[0]
    new_expert = (i == 0) | (e != be_ref[jnp.maximum(i - 1, 0)])
    n_groups = wp.shape[1] // GU_GROUP
    n_experts = wgu_hbm.shape[0]

    def fetch(ex):
        return (pltpu.make_async_copy(wgu_hbm.at[ex], wg_buf, sem.at[0]),
                pltpu.make_async_copy(wd_hbm.at[ex], wd_buf, sem.at[1]))

    @pl.when(used & new_expert)
    def _():
        @pl.when(i == 0)
        def _():
            for cp in fetch(e):
                cp.start()

        for cp in fetch(e):
            cp.wait()
        for j in range(n_groups):
            sl = slice(j * GU_GROUP, (j + 1) * GU_GROUP)
            wp[:, sl] = jnp.dot(wg_buf[:, sl].astype(BF16), perm_ref[...], preferred_element_type=F32).astype(BF16)
        wdb[...] = wd_buf[...].astype(BF16)

        @pl.when(e + 1 < n_experts)
        def _():
            for cp in fetch(e + 1):
                cp.start()

    def compute(m):
        xb = xs_ref[:m, :].astype(BF16)
        gu = jnp.dot(xb, wp[...], preferred_element_type=F32) + bgu_ref[...]
        hs = []
        for j in range(n_groups):
            gate = jnp.minimum(gu[:, j * GU_GROUP:j * GU_GROUP + LANES], SWIGLU_LIMIT)
            up = jnp.clip(gu[:, j * GU_GROUP + LANES:(j + 1) * GU_GROUP], -SWIGLU_LIMIT, SWIGLU_LIMIT)
            hs.append(((up + 1.0) * (gate * _sigmoid(gate * SWIGLU_ALPHA))).astype(BF16))
        h = jnp.concatenate(hs, axis=-1)
        ys_ref[:m, :] = jnp.dot(h, wdb[...], preferred_element_type=F32) + bd_ref[...]

    sizes = (bm, bm // 2, bm // 4)
    for m, smaller in zip(sizes, sizes[1:] + (0,)):
        @pl.when(used & (valid_ref[i] <= m) & (valid_ref[i] > smaller))
        def _(m=m):
            compute(m)
            if m < bm:
                ys_ref[m:, :] = jnp.zeros((bm - m, ys_ref.shape[1]), F32)

    @pl.when(used & (valid_ref[i] <= 0))
    def _():
        ys_ref[...] = jnp.zeros_like(ys_ref)

    @pl.when(jnp.logical_not(used))
    def _():
        ys_ref[...] = jnp.zeros_like(ys_ref)


def _moe_experts(xs, block_e, n_used, valid, w_gu, b_gu_p, w_dn, b_dn):
    n_rows, D = xs.shape
    E, _, F2 = w_gu.shape
    bm = BM_EXPERT
    n_blocks = n_rows // bm
    perm = _deinterleave_perm()
    wspec = lambda shp: pl.BlockSpec((None,) + shp, lambda i, be, nb, va: (be[i], 0, 0))
    hbm = pl.BlockSpec(memory_space=pl.ANY)
    return pl.pallas_call(
        _expert_kernel,
        grid_spec=pltpu.PrefetchScalarGridSpec(
            num_scalar_prefetch=3,
            grid=(n_blocks,),
            in_specs=[pl.BlockSpec((bm, D), lambda i, be, nb, va: (jnp.minimum(i, nb[0] - 1), 0)),
                      hbm, wspec((1, F2)), hbm, wspec((1, D)),
                      pl.BlockSpec(perm.shape, lambda i, be, nb, va: (0, 0))],
            out_specs=pl.BlockSpec((bm, D), lambda i, be, nb, va: (i, 0)),
            scratch_shapes=[pltpu.VMEM((D, F2), F32), pltpu.VMEM((F2 // 2, D), F32),
                            pltpu.VMEM((D, F2), BF16), pltpu.VMEM((F2 // 2, D), BF16),
                            pltpu.SemaphoreType.DMA((2,))],
        ),
        out_shape=jax.ShapeDtypeStruct((n_rows, D), F32),
        compiler_params=pltpu.CompilerParams(dimension_semantics=("arbitrary",), vmem_limit_bytes=VMEM_LIMIT_EXPERTS),
        name="moe_experts",
    )(block_e, n_used, valid, xs, w_gu, b_gu_p, w_dn, b_dn, perm)


def _combine_kernel(pstage_ref, prow_ref, npiece_ref, ys_ref, x1_ref, route_ref, gt_ref, g_ref,
                    b_ref, o_ref, stag, sem, *, alpha):
    i = pl.program_id(0)
    n = pl.num_programs(0)
    tm = x1_ref.shape[0]
    slot = i % 2

    def gather(tile, s):
        def copy_of(p):
            src = pl.multiple_of(prow_ref[p], RUN_ALIGN)
            dst = pl.multiple_of(pstage_ref[p], STAGE_ALIGN)
            return pltpu.make_async_copy(ys_ref.at[pl.ds(src, STAGE_ALIGN)], stag.at[s, pl.ds(dst, STAGE_ALIGN)],
                                         sem.at[s])
        _run_copies(npiece_ref, tile, copy_of)

    @pl.when(i == 0)
    def _():
        stag[...] = jnp.zeros_like(stag)
        gather(0, 0)

    @pl.when(i + 1 < n)
    def _():
        gather(i + 1, 1 - slot)

    for rows, pred in _piece_waits(npiece_ref[i]):
        pl.when(pred)(pltpu.make_async_copy(ys_ref.at[pl.ds(0, rows)], stag.at[slot, pl.ds(0, rows)], sem.at[slot]).wait)

    route = route_ref[...]
    cols = lax.broadcasted_iota(I32, (tm, STAGE_ROWS), 1)
    wgt = jnp.zeros((tm, STAGE_ROWS), F32)
    for k in range(TOP_K_EXPERTS):
        spos = route[:, 2 * TOP_K_EXPERTS + k:2 * TOP_K_EXPERTS + k + 1].astype(I32)
        wgt = wgt + jnp.where(cols == spos, route[:, TOP_K_EXPERTS + k:TOP_K_EXPERTS + k + 1], 0.0)
    ffn = jnp.dot(wgt.astype(BF16), stag[slot].astype(BF16), preferred_element_type=F32)
    o_ref[...] = _layernorm_rows(alpha * x1_ref[...] + (1.0 + gt_ref[...]) * ffn, g_ref[...], b_ref[...])


def _moe_combine(ys, tables, x1, route, gt2, ln_g, ln_b, tiles_per_batch):
    T, D = x1.shape
    tm = TM_ROUTE
    row = lambda v: v.reshape(1, -1)
    return pl.pallas_call(
        functools.partial(_combine_kernel, alpha=2.0 ** 0.25),
        grid_spec=pltpu.PrefetchScalarGridSpec(
            num_scalar_prefetch=len(tables),
            grid=(T // tm,),
            in_specs=[pl.BlockSpec(memory_space=pl.ANY),
                      pl.BlockSpec((tm, D), lambda i, *_: (i, 0)),
                      pl.BlockSpec((tm, LANES), lambda i, *_: (i, 0)),
                      pl.BlockSpec((None, 1, D), lambda i, *_: (i // tiles_per_batch, 0, 0)),
                      pl.BlockSpec((1, D), lambda i, *_: (0, 0)),
                      pl.BlockSpec((1, D), lambda i, *_: (0, 0))],
            out_specs=pl.BlockSpec((tm, D), lambda i, *_: (i, 0)),
            scratch_shapes=[pltpu.VMEM((2, STAGE_ROWS, D), F32), pltpu.SemaphoreType.DMA((2,))],
        ),
        out_shape=jax.ShapeDtypeStruct((T, D), F32),
        compiler_params=_cparams(("arbitrary",)),
        name="moe_combine",
    )(*tables, ys, x1, route, gt2, row(ln_g), row(ln_b))


def _moe_and_norm(x1, u2, route, tbl, totals, gt2, w_gu, b_gu, w_dn, b_dn, ln_g, ln_b):
    B, S, D = x1.shape
    T = B * S
    bm = BM_EXPERT
    tm = TM_ROUTE
    assert T % tm == 0 and TM_PROJ == tm and STAGE_ROWS >= tm * TOP_K_EXPERTS + N_EXPERTS * (STAGE_ALIGN - 1)
    n_tiles = T // tm
    max_rows = T * TOP_K_EXPERTS + n_tiles * N_EXPERTS * (RUN_ALIGN - 1) + N_EXPERTS * STAGE_ALIGN
    n_blocks = -(-max_rows // bm) + N_EXPERTS
    tot = totals[0, :N_EXPERTS].astype(I32)
    padded = (tot + STAGE_ALIGN + bm - 1) // bm * bm
    pad_ends = jnp.cumsum(padded)
    pad_starts = pad_ends - padded
    t3 = tbl.reshape(n_tiles, SUBLANES, LANES)[:, :, :N_EXPERTS].astype(I32)
    nrun, off, base = t3[:, 0, :], t3[:, 1, :], t3[:, 2, :]
    start = pad_starts[None, :] + base
    pieces = (nrun * RUN_ALIGN + STAGE_ALIGN - 1) // STAGE_ALIGN
    npiece = jnp.sum(pieces, axis=1)
    flat = lambda z: z.reshape(-1).astype(I32)
    ends = jnp.cumsum(pieces, axis=1)
    first = ends - pieces
    p_idx = jnp.arange(MAX_PIECES, dtype=I32)
    own = (p_idx[None, :, None] >= first[:, None, :]) & (p_idx[None, :, None] < ends[:, None, :])
    pick = lambda z: jnp.sum(jnp.where(own, z[:, None, :], 0), axis=2)
    within = (p_idx[None, :] - pick(first)) * STAGE_ALIGN
    run_tables = [flat(pick(off) + within), flat(pick(start) + within), flat(npiece)]
    zs = pad_starts + tot
    zn = (pad_ends - zs) // RUN_ALIGN
    n_used = (pad_ends[-1:] // bm).astype(I32)
    tail = jnp.concatenate([n_used, jnp.full((1,), n_blocks, I32)])
    blk_row = jnp.arange(n_blocks, dtype=I32) * bm
    block_e = jnp.minimum(jnp.sum((blk_row[:, None] >= pad_ends[None, :]).astype(I32), axis=1), N_EXPERTS - 1)
    E, F2 = b_gu.shape
    b_gu_p = b_gu.reshape(E, F2 // GU_GROUP, LANES, 2).transpose(0, 1, 3, 2).reshape(E, 1, F2)
    route2 = route.reshape(T, LANES)
    xs = _moe_dispatch(u2.reshape(T, D), route2, run_tables + [flat(zs), flat(zn), tail], n_blocks * bm)
    valid = jnp.clip(zs[block_e] - blk_row, 0, bm)
    ys = _moe_experts(xs, block_e, n_used, valid, w_gu, b_gu_p, w_dn, b_dn[:, None, :])
    out = _moe_combine(ys, run_tables, x1.reshape(T, D), route2, gt2, ln_g, ln_b, S // tm)
    return out.reshape(B, S, D)


def kernel(x, c, w_ada, b_ada, w_in, shift_mu, rwkv_w0, rwkv_w2, rwkv_a0, rwkv_a2, rwkv_g2, rwkv_k_k, rwkv_k_a, rwkv_r_k, rwkv_ln_w, rwkv_ln_b, mla_q_norm, mla_w_q_up, mla_kv_norm, mla_w_uk, mla_w_uv, idx_w_q, idx_ln_g, idx_ln_b, w_out, ln1_g, ln1_b, w_router, b_router, w_gu, b_gu, w_dn, b_dn, ln2_g, ln2_b):
    depth = w_ada.shape[0]
    assert depth == 1, "DeepNorm constants below are for a single layer"
    l = 0
    mod = _ada_mod(c, w_ada[l], b_ada[l])
    sh1, sc1, gt1, sh2, sc2, gt2 = [m[:, None, :] for m in jnp.split(mod, 6, axis=-1)]
    r, lw, k, v, kk, a, g, qabs, iq, ik, iw, cl, clt = _in_proj(
        x, sc1, sh1, w_in[l], shift_mu[l], rwkv_w0[l], rwkv_w2[l], rwkv_a0[l], rwkv_a2[l], rwkv_g2[l], rwkv_k_k[l],
        rwkv_k_a[l], mla_q_norm[l], mla_w_q_up[l], mla_w_uk[l], mla_kv_norm[l], idx_w_q[l], idx_ln_g[l], idx_ln_b[l])
    o_rwkv = _rwkv_scan(r, lw, k, v, kk, a, g, rwkv_r_k[l], rwkv_ln_w[l], rwkv_ln_b[l])
    o_dsa = _dsa_attn(iq, iw, qabs, ik, cl, clt, mla_w_uv[l])
    x1, u2, route, tbl, totals = _mix_out(o_rwkv, o_dsa, x, gt1, sc2, sh2, w_out[l], ln1_g[l], ln1_b[l],
                                          w_router[l], b_router[l])
    return _moe_and_norm(x1, u2, route, tbl, totals, gt2, w_gu[l], b_gu[l], w_dn[l], b_dn[l], ln2_g[l], ln2_b[l])
```
